```python
import math
import jax, jax.numpy as jnp
from jax import lax
import numpy as np

D_MODEL = 1024
BATCH = 16
SEQ = 4096
DEPTH = 4

CHUNK = 64
D_PLE = 256
W_BRANCH = D_MODEL // 2
N_BRANCH_COLS = 6 * W_BRANCH
CONV_A_WIDTH = 31
GMLP_BLOCK = 2 * CHUNK
N_HEADS_B = 8
HEAD_DIM_B = W_BRANCH // N_HEADS_B
POOL_WINDOWS = (2, 4, 8, 16)
N_GROUPS_C = len(POOL_WINDOWS)
GROUP_DIM_C = W_BRANCH // N_GROUPS_C
CONV_D_WIDTH = 3
DEEPNORM_ALPHA = (2.0 * DEPTH) ** 0.25
DEEPNORM_BETA = (8.0 * DEPTH) ** -0.25
LN_EPS = 1e-5

kernel_name = 'hybrid_conv_gmlp_pool_shortconv_deepnorm'


def layer_norm(x, g, b):
    xf = x.astype(jnp.float32)
    mu = jnp.mean(xf, axis=-1, keepdims=True)
    var = jnp.mean(jnp.square(xf - mu), axis=-1, keepdims=True)
    y = (xf - mu) * lax.rsqrt(var + LN_EPS)
    return (y * g.astype(jnp.float32) + b.astype(jnp.float32)).astype(x.dtype)


def causal_depthwise_conv(x, w):
    k, c = w.shape
    return lax.conv_general_dilated(
        x, w[:, None, :], window_strides=(1,), padding=[(k - 1, 0)],
        dimension_numbers=('NWC', 'WIO', 'NWC'), feature_group_count=c)


def even_mixer(x, w_in, b_in, conv_w, conv_b, ln_a_g, ln_a_b, ln_v_g, ln_v_b, w_s, b_s, w_out, b_out):
    bt, s, _ = x.shape
    z = jnp.einsum('bsd,de->bse', x, w_in) + b_in
    a_val, a_glu, a_gate, u, v, g_gate = jnp.split(z, 6, axis=-1)
    a = a_val * jax.nn.sigmoid(a_glu)
    a = causal_depthwise_conv(a, conv_w) + conv_b
    a = jax.nn.silu(layer_norm(a, ln_a_g, ln_a_b)) * jax.nn.silu(a_gate)
    u = jax.nn.gelu(u)
    v = layer_norm(jax.nn.gelu(v), ln_v_g, ln_v_b)
    v = v.reshape(bt, s // GMLP_BLOCK, GMLP_BLOCK, N_HEADS_B, HEAD_DIM_B)
    mask = jnp.tril(jnp.ones((GMLP_BLOCK, GMLP_BLOCK), dtype=bool))
    w_s = jnp.where(mask[None], w_s, jnp.zeros_like(w_s))
    sg = jnp.einsum('hts,bnshd->bnthd', w_s, v) + b_s.T[:, :, None]
    g = u * sg.reshape(bt, s, W_BRANCH) * jax.nn.silu(g_gate)
    y = jnp.concatenate([a, g], axis=-1)
    return jnp.einsum('bse,ed->bsd', y, w_out) + b_out


def odd_mixer(x, w_in, b_in, w_pool, pool_scale, conv_w, w_out, b_out):
    bt, s, _ = x.shape
    z = jnp.einsum('bsd,de->bse', x, w_in) + b_in
    c_val, c_gate, d_h, d_b, d_c, d_gate = jnp.split(z, 6, axis=-1)
    vg = c_val.reshape(bt, s, N_GROUPS_C, GROUP_DIM_C)
    cs = jnp.cumsum(vg.astype(jnp.float32), axis=1)
    pos = jnp.arange(1, s + 1, dtype=jnp.float32)
    means = []
    for gi, win in enumerate(POOL_WINDOWS):
        c_g = cs[:, :, gi]
        lag = jnp.pad(c_g, ((0, 0), (win, 0), (0, 0)))[:, :s]
        means.append((c_g - lag) / jnp.minimum(pos, float(win))[:, None])
    pooled = jnp.stack(means, axis=2).astype(vg.dtype) - vg
    c = jnp.einsum('bsgc,gce->bsge', pooled, w_pool).reshape(bt, s, W_BRANCH) * pool_scale
    c = c * jax.nn.silu(c_gate)
    d = d_b * causal_depthwise_conv(d_c * d_h, conv_w)
    d = d * jax.nn.silu(d_gate)
    y = jnp.concatenate([c, d], axis=-1)
    return jnp.einsum('bse,ed->bsd', y, w_out) + b_out


def _fwd_setup_inputs(seed: int = 0) -> dict:
    key = jax.random.key(seed)
    ks = jax.random.split(key, 26)
    ne = (DEPTH + 1) // 2
    no = DEPTH // 2
    f32 = jnp.float32
    nrm = lambda k, shape, scale: jax.random.normal(k, shape, f32) * scale
    return {
        'x': nrm(ks[0], (BATCH, SEQ, D_MODEL), 1.0),
        'p': nrm(ks[1], (DEPTH, BATCH, SEQ, D_PLE), 1.0),
        'w_in_e': nrm(ks[2], (ne, D_MODEL, N_BRANCH_COLS), D_MODEL ** -0.5),
        'b_in_e': nrm(ks[3], (ne, N_BRANCH_COLS), 0.02),
        'conv_a_w': nrm(ks[4], (ne, CONV_A_WIDTH, W_BRANCH), CONV_A_WIDTH ** -0.5),
        'conv_a_b': nrm(ks[5], (ne, W_BRANCH), 0.02),
        'ln_a_g': 1.0 + nrm(ks[6], (ne, W_BRANCH), 0.05),
        'ln_a_b': nrm(ks[7], (ne, W_BRANCH), 0.02),
        'ln_v_g': 1.0 + nrm(ks[8], (ne, W_BRANCH), 0.05),
        'ln_v_b': nrm(ks[9], (ne, W_BRANCH), 0.02),
        'w_s': nrm(ks[10], (ne, N_HEADS_B, GMLP_BLOCK, GMLP_BLOCK), 0.5 * GMLP_BLOCK ** -0.5),
        'b_s': 1.0 + nrm(ks[11], (ne, N_HEADS_B, GMLP_BLOCK), 0.1),
        'w_out_e': nrm(ks[12], (ne, 2 * W_BRANCH, D_MODEL), DEEPNORM_BETA * (2 * W_BRANCH) ** -0.5),
        'b_out_e': nrm(ks[13], (ne, D_MODEL), 0.02),
        'w_in_o': nrm(ks[14], (no, D_MODEL, N_BRANCH_COLS), D_MODEL ** -0.5),
        'b_in_o': nrm(ks[15], (no, N_BRANCH_COLS), 0.02),
        'w_pool': nrm(ks[16], (no, N_GROUPS_C, GROUP_DIM_C, GROUP_DIM_C), GROUP_DIM_C ** -0.5),
        'pool_scale': 1.0 + nrm(ks[17], (no, W_BRANCH), 0.1),
        'conv_d_w': nrm(ks[18], (no, CONV_D_WIDTH, W_BRANCH), CONV_D_WIDTH ** -0.5),
        'w_out_o': nrm(ks[19], (no, 2 * W_BRANCH, D_MODEL), DEEPNORM_BETA * (2 * W_BRANCH) ** -0.5),
        'b_out_o': nrm(ks[20], (no, D_MODEL), 0.02),
        'ln_g': 1.0 + nrm(ks[21], (DEPTH, D_MODEL), 0.05),
        'ln_b': nrm(ks[22], (DEPTH, D_MODEL), 0.02),
        'w_ple': nrm(ks[23], (DEPTH, D_PLE, D_MODEL), D_PLE ** -0.5),
        'w_ple_gate': nrm(ks[24], (DEPTH, D_MODEL, D_MODEL), D_MODEL ** -0.5),
        'b_ple_gate': nrm(ks[25], (DEPTH, D_MODEL), 0.02),
    }


def _fwd_reference(x, p, w_in_e, b_in_e, conv_a_w, conv_a_b, ln_a_g, ln_a_b, ln_v_g, ln_v_b, w_s, b_s,
              w_out_e, b_out_e, w_in_o, b_in_o, w_pool, pool_scale, conv_d_w, w_out_o, b_out_o,
              ln_g, ln_b, w_ple, w_ple_gate, b_ple_gate):
    for i in range(DEPTH):
        j = i // 2
        if i % 2 == 0:
            out = even_mixer(x, w_in_e[j], b_in_e[j], conv_a_w[j], conv_a_b[j], ln_a_g[j], ln_a_b[j],
                             ln_v_g[j], ln_v_b[j], w_s[j], b_s[j], w_out_e[j], b_out_e[j])
        else:
            out = odd_mixer(x, w_in_o[j], b_in_o[j], w_pool[j], pool_scale[j], conv_d_w[j],
                            w_out_o[j], b_out_o[j])
        h = layer_norm(DEEPNORM_ALPHA * x + out, ln_g[i], ln_b[i])
        gate = jax.nn.sigmoid(jnp.einsum('bsd,de->bse', h, w_ple_gate[i]) + b_ple_gate[i])
        x = h + gate * jnp.einsum('bsk,kd->bsd', p[i], w_ple[i])
    return x


import jax as _jax
import jax.numpy as _jnp

TWIN_FORMAT = 'train_step'
FWD_PARAMS = ['x', 'p', 'w_in_e', 'b_in_e', 'conv_a_w', 'conv_a_b', 'ln_a_g', 'ln_a_b', 'ln_v_g', 'ln_v_b', 'w_s', 'b_s', 'w_out_e', 'b_out_e', 'w_in_o', 'b_in_o', 'w_pool', 'pool_scale', 'conv_d_w', 'w_out_o', 'b_out_o', 'ln_g', 'ln_b', 'w_ple', 'w_ple_gate', 'b_ple_gate']
TWIN_WEIGHTS = ['w_in_e', 'b_in_e', 'conv_a_w', 'conv_a_b', 'ln_a_g', 'ln_a_b', 'ln_v_g', 'ln_v_b', 'w_s', 'b_s', 'w_out_e', 'b_out_e', 'w_in_o', 'b_in_o', 'w_pool', 'pool_scale', 'conv_d_w', 'w_out_o', 'b_out_o', 'ln_g', 'ln_b', 'w_ple', 'w_ple_gate', 'b_ple_gate']
TWIN_DIFF_INPUT = 'x'
TWIN_INPUTS = ['x', 'p', 'w_in_e', 'b_in_e', 'conv_a_w', 'conv_a_b', 'ln_a_g', 'ln_a_b', 'ln_v_g', 'ln_v_b', 'w_s', 'b_s', 'w_out_e', 'b_out_e', 'w_in_o', 'b_in_o', 'w_pool', 'pool_scale', 'conv_d_w', 'w_out_o', 'b_out_o', 'ln_g', 'ln_b', 'w_ple', 'w_ple_gate', 'b_ple_gate', 'loss_target', 'm_w_in_e', 'm_b_in_e', 'm_conv_a_w', 'm_conv_a_b', 'm_ln_a_g', 'm_ln_a_b', 'm_ln_v_g', 'm_ln_v_b', 'm_w_s', 'm_b_s', 'm_w_out_e', 'm_b_out_e', 'm_w_in_o', 'm_b_in_o', 'm_w_pool', 'm_pool_scale', 'm_conv_d_w', 'm_w_out_o', 'm_b_out_o', 'm_ln_g', 'm_ln_b', 'm_w_ple', 'm_w_ple_gate', 'm_b_ple_gate', 'v_w_in_e', 'v_b_in_e', 'v_conv_a_w', 'v_conv_a_b', 'v_ln_a_g', 'v_ln_a_b', 'v_ln_v_g', 'v_ln_v_b', 'v_w_s', 'v_b_s', 'v_w_out_e', 'v_b_out_e', 'v_w_in_o', 'v_b_in_o', 'v_w_pool', 'v_pool_scale', 'v_conv_d_w', 'v_w_out_o', 'v_b_out_o', 'v_ln_g', 'v_ln_b', 'v_w_ple', 'v_w_ple_gate', 'v_b_ple_gate']
TWIN_OUTPUTS = ['loss', 'grad_x', 'grad_w_in_e', 'grad_b_in_e', 'grad_conv_a_w', 'grad_conv_a_b', 'grad_ln_a_g', 'grad_ln_a_b', 'grad_ln_v_g', 'grad_ln_v_b', 'grad_w_s', 'grad_b_s', 'grad_w_out_e', 'grad_b_out_e', 'grad_w_in_o', 'grad_b_in_o', 'grad_w_pool', 'grad_pool_scale', 'grad_conv_d_w', 'grad_w_out_o', 'grad_b_out_o', 'grad_ln_g', 'grad_ln_b', 'grad_w_ple', 'grad_w_ple_gate', 'grad_b_ple_gate', 'delta_w_in_e', 'delta_b_in_e', 'delta_conv_a_w', 'delta_conv_a_b', 'delta_ln_a_g', 'delta_ln_a_b', 'delta_ln_v_g', 'delta_ln_v_b', 'delta_w_s', 'delta_b_s', 'delta_w_out_e', 'delta_b_out_e', 'delta_w_in_o', 'delta_b_in_o', 'delta_w_pool', 'delta_pool_scale', 'delta_conv_d_w', 'delta_w_out_o', 'delta_b_out_o', 'delta_ln_g', 'delta_ln_b', 'delta_w_ple', 'delta_w_ple_gate', 'delta_b_ple_gate', 'new_m_w_in_e', 'new_m_b_in_e', 'new_m_conv_a_w', 'new_m_conv_a_b', 'new_m_ln_a_g', 'new_m_ln_a_b', 'new_m_ln_v_g', 'new_m_ln_v_b', 'new_m_w_s', 'new_m_b_s', 'new_m_w_out_e', 'new_m_b_out_e', 'new_m_w_in_o', 'new_m_b_in_o', 'new_m_w_pool', 'new_m_pool_scale', 'new_m_conv_d_w', 'new_m_w_out_o', 'new_m_b_out_o', 'new_m_ln_g', 'new_m_ln_b', 'new_m_w_ple', 'new_m_w_ple_gate', 'new_m_b_ple_gate', 'new_v_w_in_e', 'new_v_b_in_e', 'new_v_conv_a_w', 'new_v_conv_a_b', 'new_v_ln_a_g', 'new_v_ln_a_b', 'new_v_ln_v_g', 'new_v_ln_v_b', 'new_v_w_s', 'new_v_b_s', 'new_v_w_out_e', 'new_v_b_out_e', 'new_v_w_in_o', 'new_v_b_in_o', 'new_v_w_pool', 'new_v_pool_scale', 'new_v_conv_d_w', 'new_v_w_out_o', 'new_v_b_out_o', 'new_v_ln_g', 'new_v_ln_b', 'new_v_w_ple', 'new_v_w_ple_gate', 'new_v_b_ple_gate']
TWIN_LEAF_KINDS = {'loss': 'loss', 'grad_x': 'grad_x', 'grad_w_in_e': 'grad_w', 'grad_b_in_e': 'grad_w', 'grad_conv_a_w': 'grad_w', 'grad_conv_a_b': 'grad_w', 'grad_ln_a_g': 'grad_w', 'grad_ln_a_b': 'grad_w', 'grad_ln_v_g': 'grad_w', 'grad_ln_v_b': 'grad_w', 'grad_w_s': 'grad_w', 'grad_b_s': 'grad_w', 'grad_w_out_e': 'grad_w', 'grad_b_out_e': 'grad_w', 'grad_w_in_o': 'grad_w', 'grad_b_in_o': 'grad_w', 'grad_w_pool': 'grad_w', 'grad_pool_scale': 'grad_w', 'grad_conv_d_w': 'grad_w', 'grad_w_out_o': 'grad_w', 'grad_b_out_o': 'grad_w', 'grad_ln_g': 'grad_w', 'grad_ln_b': 'grad_w', 'grad_w_ple': 'grad_w', 'grad_w_ple_gate': 'grad_w', 'grad_b_ple_gate': 'grad_w', 'delta_w_in_e': 'delta_w', 'delta_b_in_e': 'delta_w', 'delta_conv_a_w': 'delta_w', 'delta_conv_a_b': 'delta_w', 'delta_ln_a_g': 'delta_w', 'delta_ln_a_b': 'delta_w', 'delta_ln_v_g': 'delta_w', 'delta_ln_v_b': 'delta_w', 'delta_w_s': 'delta_w', 'delta_b_s': 'delta_w', 'delta_w_out_e': 'delta_w', 'delta_b_out_e': 'delta_w', 'delta_w_in_o': 'delta_w', 'delta_b_in_o': 'delta_w', 'delta_w_pool': 'delta_w', 'delta_pool_scale': 'delta_w', 'delta_conv_d_w': 'delta_w', 'delta_w_out_o': 'delta_w', 'delta_b_out_o': 'delta_w', 'delta_ln_g': 'delta_w', 'delta_ln_b': 'delta_w', 'delta_w_ple': 'delta_w', 'delta_w_ple_gate': 'delta_w', 'delta_b_ple_gate': 'delta_w', 'new_m_w_in_e': 'new_m', 'new_m_b_in_e': 'new_m', 'new_m_conv_a_w': 'new_m', 'new_m_conv_a_b': 'new_m', 'new_m_ln_a_g': 'new_m', 'new_m_ln_a_b': 'new_m', 'new_m_ln_v_g': 'new_m', 'new_m_ln_v_b': 'new_m', 'new_m_w_s': 'new_m', 'new_m_b_s': 'new_m', 'new_m_w_out_e': 'new_m', 'new_m_b_out_e': 'new_m', 'new_m_w_in_o': 'new_m', 'new_m_b_in_o': 'new_m', 'new_m_w_pool': 'new_m', 'new_m_pool_scale': 'new_m', 'new_m_conv_d_w': 'new_m', 'new_m_w_out_o': 'new_m', 'new_m_b_out_o': 'new_m', 'new_m_ln_g': 'new_m', 'new_m_ln_b': 'new_m', 'new_m_w_ple': 'new_m', 'new_m_w_ple_gate': 'new_m', 'new_m_b_ple_gate': 'new_m', 'new_v_w_in_e': 'new_v', 'new_v_b_in_e': 'new_v', 'new_v_conv_a_w': 'new_v', 'new_v_conv_a_b': 'new_v', 'new_v_ln_a_g': 'new_v', 'new_v_ln_a_b': 'new_v', 'new_v_ln_v_g': 'new_v', 'new_v_ln_v_b': 'new_v', 'new_v_w_s': 'new_v', 'new_v_b_s': 'new_v', 'new_v_w_out_e': 'new_v', 'new_v_b_out_e': 'new_v', 'new_v_w_in_o': 'new_v', 'new_v_b_in_o': 'new_v', 'new_v_w_pool': 'new_v', 'new_v_pool_scale': 'new_v', 'new_v_conv_d_w': 'new_v', 'new_v_w_out_o': 'new_v', 'new_v_b_out_o': 'new_v', 'new_v_ln_g': 'new_v', 'new_v_ln_b': 'new_v', 'new_v_w_ple': 'new_v', 'new_v_w_ple_gate': 'new_v', 'new_v_b_ple_gate': 'new_v'}


def _forward(args):
    return _fwd_reference(*[args[k] for k in FWD_PARAMS])


def _output_shape():
    out = _jax.eval_shape(lambda: _forward(_fwd_setup_inputs(0)))
    return out.shape, out.dtype

N_MICROBATCH = 1
ADAM_LR = 0.001
ADAM_B1 = 0.9
ADAM_B2 = 0.999
ADAM_EPS = 1e-08
ADAM_WD = 0.01
ADAM_STEP = 10
PER_EXAMPLE_BATCH_AXIS = {'x': 0, 'p': 1, 'loss_target': 0}
SHARED_INPUTS = []
_WEIGHT_DTYPES = {'w_in_e': _jnp.float32, 'b_in_e': _jnp.float32, 'conv_a_w': _jnp.float32, 'conv_a_b': _jnp.float32, 'ln_a_g': _jnp.float32, 'ln_a_b': _jnp.float32, 'ln_v_g': _jnp.float32, 'ln_v_b': _jnp.float32, 'w_s': _jnp.float32, 'b_s': _jnp.float32, 'w_out_e': _jnp.float32, 'b_out_e': _jnp.float32, 'w_in_o': _jnp.float32, 'b_in_o': _jnp.float32, 'w_pool': _jnp.float32, 'pool_scale': _jnp.float32, 'conv_d_w': _jnp.float32, 'w_out_o': _jnp.float32, 'b_out_o': _jnp.float32, 'ln_g': _jnp.float32, 'ln_b': _jnp.float32, 'w_ple': _jnp.float32, 'w_ple_gate': _jnp.float32, 'b_ple_gate': _jnp.float32}
MOMENT_SCALE = {'w_in_e': 2.422983e-02, 'b_in_e': 8.218262e-02, 'conv_a_w': 2.460587e-02, 'conv_a_b': 1.312158e-01, 'ln_a_g': 5.847750e-02, 'ln_a_b': 9.235007e-02, 'ln_v_g': 1.033640e-02, 'ln_v_b': 1.001062e-02, 'w_s': 1.438200e-02, 'b_s': 2.034632e-02, 'w_out_e': 1.119001e-01, 'b_out_e': 2.841589e+00, 'w_in_o': 5.739288e-02, 'b_in_o': 5.349713e-02, 'w_pool': 4.333849e-02, 'pool_scale': 4.471718e-02, 'conv_d_w': 6.474501e-02, 'w_out_o': 1.275116e-01, 'b_out_o': 3.077778e+00, 'ln_g': 3.343520e+01, 'ln_b': 5.662818e+00, 'w_ple': 3.344968e-01, 'w_ple_gate': 8.961551e-02, 'b_ple_gate': 3.254538e+00}


def _to_microbatches(a, axis):
    t = _jnp.moveaxis(a, axis, 0)
    t = t.reshape((N_MICROBATCH, t.shape[0] // N_MICROBATCH) + t.shape[1:])
    return _jnp.moveaxis(t, 1, axis + 1)


def setup_inputs(seed: int = 0) -> dict:
    inp = _fwd_setup_inputs(seed)
    key = _jax.random.fold_in(_jax.random.key(seed), 7919)
    shape, _ = _output_shape()
    out = dict(inp)
    out["loss_target"] = _jax.random.normal(_jax.random.fold_in(key, 0), shape, _jnp.float32)
    for i, name in enumerate(TWIN_WEIGHTS):
        w = inp[name].astype(_jnp.float32)
        if MOMENT_SCALE is None:
            s = _jnp.sqrt(_jnp.mean(_jnp.square(w)) + 1e-30)
        else:
            s = MOMENT_SCALE[name]
        km, kv = _jax.random.split(_jax.random.fold_in(key, i + 1))
        out[name] = w
        out["m_" + name] = s * _jax.random.normal(km, w.shape, _jnp.float32)
        out["v_" + name] = (s * s) * _jax.random.uniform(kv, w.shape, _jnp.float32, 0.5, 1.5)
    if N_MICROBATCH > 1:
        for name, axis in PER_EXAMPLE_BATCH_AXIS.items():
            out[name] = _to_microbatches(out[name], axis)
    return {'x': out['x'], 'p': out['p'], 'w_in_e': out['w_in_e'], 'b_in_e': out['b_in_e'], 'conv_a_w': out['conv_a_w'], 'conv_a_b': out['conv_a_b'], 'ln_a_g': out['ln_a_g'], 'ln_a_b': out['ln_a_b'], 'ln_v_g': out['ln_v_g'], 'ln_v_b': out['ln_v_b'], 'w_s': out['w_s'], 'b_s': out['b_s'], 'w_out_e': out['w_out_e'], 'b_out_e': out['b_out_e'], 'w_in_o': out['w_in_o'], 'b_in_o': out['b_in_o'], 'w_pool': out['w_pool'], 'pool_scale': out['pool_scale'], 'conv_d_w': out['conv_d_w'], 'w_out_o': out['w_out_o'], 'b_out_o': out['b_out_o'], 'ln_g': out['ln_g'], 'ln_b': out['ln_b'], 'w_ple': out['w_ple'], 'w_ple_gate': out['w_ple_gate'], 'b_ple_gate': out['b_ple_gate'], 'loss_target': out['loss_target'], 'm_w_in_e': out['m_w_in_e'], 'm_b_in_e': out['m_b_in_e'], 'm_conv_a_w': out['m_conv_a_w'], 'm_conv_a_b': out['m_conv_a_b'], 'm_ln_a_g': out['m_ln_a_g'], 'm_ln_a_b': out['m_ln_a_b'], 'm_ln_v_g': out['m_ln_v_g'], 'm_ln_v_b': out['m_ln_v_b'], 'm_w_s': out['m_w_s'], 'm_b_s': out['m_b_s'], 'm_w_out_e': out['m_w_out_e'], 'm_b_out_e': out['m_b_out_e'], 'm_w_in_o': out['m_w_in_o'], 'm_b_in_o': out['m_b_in_o'], 'm_w_pool': out['m_w_pool'], 'm_pool_scale': out['m_pool_scale'], 'm_conv_d_w': out['m_conv_d_w'], 'm_w_out_o': out['m_w_out_o'], 'm_b_out_o': out['m_b_out_o'], 'm_ln_g': out['m_ln_g'], 'm_ln_b': out['m_ln_b'], 'm_w_ple': out['m_w_ple'], 'm_w_ple_gate': out['m_w_ple_gate'], 'm_b_ple_gate': out['m_b_ple_gate'], 'v_w_in_e': out['v_w_in_e'], 'v_b_in_e': out['v_b_in_e'], 'v_conv_a_w': out['v_conv_a_w'], 'v_conv_a_b': out['v_conv_a_b'], 'v_ln_a_g': out['v_ln_a_g'], 'v_ln_a_b': out['v_ln_a_b'], 'v_ln_v_g': out['v_ln_v_g'], 'v_ln_v_b': out['v_ln_v_b'], 'v_w_s': out['v_w_s'], 'v_b_s': out['v_b_s'], 'v_w_out_e': out['v_w_out_e'], 'v_b_out_e': out['v_b_out_e'], 'v_w_in_o': out['v_w_in_o'], 'v_b_in_o': out['v_b_in_o'], 'v_w_pool': out['v_w_pool'], 'v_pool_scale': out['v_pool_scale'], 'v_conv_d_w': out['v_conv_d_w'], 'v_w_out_o': out['v_w_out_o'], 'v_b_out_o': out['v_b_out_o'], 'v_ln_g': out['v_ln_g'], 'v_ln_b': out['v_ln_b'], 'v_w_ple': out['v_w_ple'], 'v_w_ple_gate': out['v_w_ple_gate'], 'v_b_ple_gate': out['v_b_ple_gate']}


def _loss(weights, diff, rest, loss_target):
    with _jax.named_scope("forward"):
        args = {**rest, TWIN_DIFF_INPUT: diff, **{k: w.astype(_WEIGHT_DTYPES[k]) for k, w in weights.items()}}
        y = _forward(args)
    with _jax.named_scope("loss_head"):
        err = _jnp.square(y.astype(_jnp.float32) - loss_target)
        return 0.5 * _jnp.sum(_jnp.mean(err, axis=-1)) if err.ndim else 0.5 * err


def _adamw(w, g, m, v):
    m = ADAM_B1 * m + (1.0 - ADAM_B1) * g
    v = ADAM_B2 * v + (1.0 - ADAM_B2) * _jnp.square(g)
    m_hat = m / (1.0 - ADAM_B1 ** ADAM_STEP)
    v_hat = v / (1.0 - ADAM_B2 ** ADAM_STEP)
    delta = -ADAM_LR * (m_hat / (_jnp.sqrt(v_hat) + ADAM_EPS) + ADAM_WD * w)
    return delta, m, v


def reference(x, p, w_in_e, b_in_e, conv_a_w, conv_a_b, ln_a_g, ln_a_b, ln_v_g, ln_v_b, w_s, b_s, w_out_e, b_out_e, w_in_o, b_in_o, w_pool, pool_scale, conv_d_w, w_out_o, b_out_o, ln_g, ln_b, w_ple, w_ple_gate, b_ple_gate, loss_target, m_w_in_e, m_b_in_e, m_conv_a_w, m_conv_a_b, m_ln_a_g, m_ln_a_b, m_ln_v_g, m_ln_v_b, m_w_s, m_b_s, m_w_out_e, m_b_out_e, m_w_in_o, m_b_in_o, m_w_pool, m_pool_scale, m_conv_d_w, m_w_out_o, m_b_out_o, m_ln_g, m_ln_b, m_w_ple, m_w_ple_gate, m_b_ple_gate, v_w_in_e, v_b_in_e, v_conv_a_w, v_conv_a_b, v_ln_a_g, v_ln_a_b, v_ln_v_g, v_ln_v_b, v_w_s, v_b_s, v_w_out_e, v_b_out_e, v_w_in_o, v_b_in_o, v_w_pool, v_pool_scale, v_conv_d_w, v_w_out_o, v_b_out_o, v_ln_g, v_ln_b, v_w_ple, v_w_ple_gate, v_b_ple_gate):
    given = dict(x=x, p=p, w_in_e=w_in_e, b_in_e=b_in_e, conv_a_w=conv_a_w, conv_a_b=conv_a_b, ln_a_g=ln_a_g, ln_a_b=ln_a_b, ln_v_g=ln_v_g, ln_v_b=ln_v_b, w_s=w_s, b_s=b_s, w_out_e=w_out_e, b_out_e=b_out_e, w_in_o=w_in_o, b_in_o=b_in_o, w_pool=w_pool, pool_scale=pool_scale, conv_d_w=conv_d_w, w_out_o=w_out_o, b_out_o=b_out_o, ln_g=ln_g, ln_b=ln_b, w_ple=w_ple, w_ple_gate=w_ple_gate, b_ple_gate=b_ple_gate, loss_target=loss_target, m_w_in_e=m_w_in_e, m_b_in_e=m_b_in_e, m_conv_a_w=m_conv_a_w, m_conv_a_b=m_conv_a_b, m_ln_a_g=m_ln_a_g, m_ln_a_b=m_ln_a_b, m_ln_v_g=m_ln_v_g, m_ln_v_b=m_ln_v_b, m_w_s=m_w_s, m_b_s=m_b_s, m_w_out_e=m_w_out_e, m_b_out_e=m_b_out_e, m_w_in_o=m_w_in_o, m_b_in_o=m_b_in_o, m_w_pool=m_w_pool, m_pool_scale=m_pool_scale, m_conv_d_w=m_conv_d_w, m_w_out_o=m_w_out_o, m_b_out_o=m_b_out_o, m_ln_g=m_ln_g, m_ln_b=m_ln_b, m_w_ple=m_w_ple, m_w_ple_gate=m_w_ple_gate, m_b_ple_gate=m_b_ple_gate, v_w_in_e=v_w_in_e, v_b_in_e=v_b_in_e, v_conv_a_w=v_conv_a_w, v_conv_a_b=v_conv_a_b, v_ln_a_g=v_ln_a_g, v_ln_a_b=v_ln_a_b, v_ln_v_g=v_ln_v_g, v_ln_v_b=v_ln_v_b, v_w_s=v_w_s, v_b_s=v_b_s, v_w_out_e=v_w_out_e, v_b_out_e=v_b_out_e, v_w_in_o=v_w_in_o, v_b_in_o=v_b_in_o, v_w_pool=v_w_pool, v_pool_scale=v_pool_scale, v_conv_d_w=v_conv_d_w, v_w_out_o=v_w_out_o, v_b_out_o=v_b_out_o, v_ln_g=v_ln_g, v_ln_b=v_ln_b, v_w_ple=v_w_ple, v_w_ple_gate=v_w_ple_gate, v_b_ple_gate=v_b_ple_gate)
    weights = {n: given[n] for n in TWIN_WEIGHTS}
    shared = {n: given[n] for n in SHARED_INPUTS}
    per_example = {n: given[n] for n in ['x', 'p']}
    grad_fn = _jax.value_and_grad(_loss, argnums=(0, 1))

    def one_microbatch(ex, loss_target):
        ex = dict(ex)
        diff = ex.pop(TWIN_DIFF_INPUT)
        return grad_fn(weights, diff, {**shared, **ex}, loss_target)

    if N_MICROBATCH == 1:
        loss, (grad_w, grad_x) = one_microbatch(per_example, given["loss_target"])
    else:
        def body(carry, xs):
            loss_sum, grad_sum = carry
            l_k, (gw_k, gx_k) = one_microbatch(xs[0], xs[1])
            with _jax.named_scope("update"):
                return (loss_sum + l_k, _jax.tree.map(_jnp.add, grad_sum, gw_k)), gx_k

        init = (_jnp.zeros((), _jnp.float32), _jax.tree.map(_jnp.zeros_like, weights))
        (loss, grad_w), grad_x = _jax.lax.scan(body, init, (per_example, given["loss_target"]))
    with _jax.named_scope("update"):
        delta_w, new_m, new_v = {}, {}, {}
        for n in TWIN_WEIGHTS:
            delta_w[n], new_m[n], new_v[n] = _adamw(weights[n], grad_w[n], given["m_" + n], given["v_" + n])
    return (loss, grad_x, *[grad_w[n] for n in TWIN_WEIGHTS], *[delta_w[n] for n in TWIN_WEIGHTS],
            *[new_m[n] for n in TWIN_WEIGHTS], *[new_v[n] for n in TWIN_WEIGHTS])
```

```python
import functools
import math

import jax
import jax.numpy as jnp
from jax import lax
from jax.experimental import pallas as pl
from jax.experimental.pallas import tpu as pltpu

F32 = jnp.float32
MM_DTYPE = jnp.bfloat16
SEQ_TILE = 512
ROW_TILE = 512
BWD_ROW_TILE = 256
HALO = 32
CONV_CHUNK = 32
GBLK = 128
VMEM_LIMIT = 56 * 1024 * 1024

D_MODEL = 1024
W_BR = 512
N_COLS = 6 * W_BR
D_PLE = 256
KA = 31
DEPTH = 4
POOL_WINDOWS = (2, 4, 8, 16)
ALPHA = (2.0 * DEPTH) ** 0.25
LN_EPS = 1e-5
GELU_C = math.sqrt(2.0 / math.pi)

ADAM_LR, ADAM_B1, ADAM_B2, ADAM_EPS, ADAM_WD, ADAM_STEP = 0.001, 0.9, 0.999, 1e-08, 0.01, 10

MESH = pl.DeviceIdType.MESH
ANY = pl.BlockSpec(memory_space=pl.ANY)


def _cparams(sem=None):
    return pltpu.CompilerParams(dimension_semantics=sem, vmem_limit_bytes=VMEM_LIMIT)


def _sigmoid(x):
    return 1.0 / (1.0 + jnp.exp(-x))


def _silu(x):
    return x * _sigmoid(x)


def _silu_grad(x):
    s = _sigmoid(x)
    return x * s, s * (1.0 + x * (1.0 - s))


def _gelu(x):
    return 0.5 * x * (1.0 + jnp.tanh(GELU_C * (x + 0.044715 * (x * x * x))))


def _gelu_grad(x):
    x2 = x * x
    th = jnp.tanh(GELU_C * (x + 0.044715 * (x * x2)))
    return 0.5 * x * (1.0 + th), 0.5 * (1.0 + th) + 0.5 * x * (1.0 - th * th) * (GELU_C * (1.0 + 3.0 * 0.044715 * x2))


def _ln_stats(x):
    mu = jnp.mean(x, axis=-1, keepdims=True)
    d = x - mu
    var = jnp.mean(d * d, axis=-1, keepdims=True)
    rs = lax.rsqrt(var + LN_EPS)
    return d * rs, rs


def _ln_bwd(dxh, xh, rs):
    return rs * (dxh - jnp.mean(dxh, axis=-1, keepdims=True) - xh * jnp.mean(dxh * xh, axis=-1, keepdims=True))


def _mm(a):
    return a.astype(MM_DTYPE)


def _dot(a, b):
    return jnp.dot(_mm(a), _mm(b), preferred_element_type=F32)


def _dot_nt(a, b):
    return lax.dot_general(_mm(a), _mm(b), (((1,), (1,)), ((), ())), preferred_element_type=F32)


def _dot_tn(a, b):
    return lax.dot_general(_mm(a), _mm(b), (((0,), (0,)), ((), ())), preferred_element_type=F32)


def _rowsum(x):
    return jnp.sum(x, axis=0, keepdims=True)


def _in_proj(x2, w, b):
    t, d = x2.shape
    n = w.shape[1]
    tm = min(ROW_TILE, t)
    nc = 768

    def body(x_ref, w_ref, b_ref, z_ref):
        xb = _mm(x_ref[...])
        for j in range(n // nc):
            cs = slice(j * nc, (j + 1) * nc)
            z_ref[:, cs] = jnp.dot(xb, w_ref[:, cs], preferred_element_type=F32) + b_ref[:, cs]

    return pl.pallas_call(
        body, name="in_proj", grid=(t // tm,),
        in_specs=[pl.BlockSpec((tm, d), lambda i: (i, 0)), pl.BlockSpec((d, n), lambda i: (0, 0)),
                  pl.BlockSpec((1, n), lambda i: (0, 0))],
        out_specs=pl.BlockSpec((tm, n), lambda i: (i, 0)),
        out_shape=jax.ShapeDtypeStruct((t, n), F32),
        compiler_params=_cparams(("parallel",)),
    )(x2, w, b)


def _in_proj_bwd_dx(dxr, dz, w):
    t, d = dxr.shape
    n = w.shape[1]
    tm = min(ROW_TILE, t)

    def body(dxr_ref, dz_ref, w_ref, dx_ref):
        dx_ref[...] = dxr_ref[...] + _dot_nt(dz_ref[...], w_ref[...])

    return pl.pallas_call(
        body, name="in_proj_bwd_dx", grid=(t // tm,),
        in_specs=[pl.BlockSpec((tm, d), lambda i: (i, 0)), pl.BlockSpec((tm, n), lambda i: (i, 0)),
                  pl.BlockSpec((d, n), lambda i: (0, 0))],
        out_specs=pl.BlockSpec((tm, d), lambda i: (i, 0)),
        out_shape=jax.ShapeDtypeStruct((t, d), F32),
        compiler_params=_cparams(("parallel",)),
    )(dxr, dz, w)


def _in_proj_bwd_dw(x2, dz):
    t, d = x2.shape
    n = dz.shape[1]
    tm = min(ROW_TILE, t)
    nc = 768

    def body(x_ref, dz_ref, dw_ref):
        @pl.when(pl.program_id(1) == 0)
        def _():
            dw_ref[...] = jnp.zeros_like(dw_ref)
        dw_ref[...] += _dot_tn(x_ref[...], dz_ref[...])

    return pl.pallas_call(
        body, name="in_proj_bwd_dw", grid=(n // nc, t // tm),
        in_specs=[pl.BlockSpec((tm, d), lambda j, i: (i, 0)), pl.BlockSpec((tm, nc), lambda j, i: (i, j))],
        out_specs=pl.BlockSpec((d, nc), lambda j, i: (0, j)),
        out_shape=jax.ShapeDtypeStruct((d, n), F32),
        compiler_params=_cparams(("parallel", "arbitrary")),
    )(x2, dz)


def _halo_specs(ts, s_len, cols, left=True):
    per = ts // HALO
    last = s_len // HALO - 1
    if left:
        return pl.BlockSpec((1, HALO, cols), lambda b, s: (b, jnp.maximum(s * per - 1, 0), 0))
    return pl.BlockSpec((1, HALO, cols), lambda b, s: (b, jnp.minimum((s + 1) * per, last), 0))


def _build_shifts(src_ref, sh_ref, rows):
    for r in range(8):
        sh_ref[r, 0:rows, :] = src_ref[r:r + rows, :]


def _tril_masks():
    ri = lax.broadcasted_iota(jnp.int32, (GBLK, GBLK), 0)
    ci = lax.broadcasted_iota(jnp.int32, (GBLK, GBLK), 1)
    return ri >= ci, ci >= ri


def _spatial(w_ref, keep, vb):
    lane = lax.broadcasted_iota(jnp.int32, (GBLK, GBLK), 1)
    outs = []
    for p in range(4):
        xs = vb[:, p * GBLK:(p + 1) * GBLK]
        r0 = jnp.dot(_mm(jnp.where(keep, w_ref[2 * p], 0.0)), xs, preferred_element_type=F32)
        r1 = jnp.dot(_mm(jnp.where(keep, w_ref[2 * p + 1], 0.0)), xs, preferred_element_type=F32)
        outs.append(jnp.where(lane < 64, r0, r1))
    return jnp.concatenate(outs, axis=1)


def _even_fwd(z3, cw, vec, ws, bsf):
    bsz, s_len, _ = z3.shape
    ts = min(SEQ_TILE, s_len)
    ext_rows = ts + HALO

    def body(z_ref, zl_ref, cw_ref, vec_ref, ws_ref, bsf_ref, y_ref, ext_ref, sh_ref, a1_ref):
        s = pl.program_id(1)
        hl = zl_ref[0]
        a0h = hl[:, 0:W_BR] * _sigmoid(hl[:, W_BR:2 * W_BR])
        ext_ref[0:HALO, :] = jnp.where(s > 0, a0h, 0.0)
        ext_ref[HALO:ext_rows, :] = z_ref[0, :, 0:W_BR] * _sigmoid(z_ref[0, :, W_BR:2 * W_BR])
        ext_ref[ext_rows:ext_rows + 8, :] = jnp.zeros((8, W_BR), F32)
        _build_shifts(ext_ref, sh_ref, ext_rows)

        def conv_chunk(ci, carry):
            base = pl.multiple_of(ci * CONV_CHUNK, CONV_CHUNK)
            acc = jnp.zeros((CONV_CHUNK, W_BR), F32) + vec_ref[0:1, :]
            for k in range(KA):
                q, r = divmod(2 + k, 8)
                acc = acc + sh_ref[r, pl.ds(base + 8 * q, CONV_CHUNK), :] * cw_ref[k:k + 1, :]
            a1_ref[pl.ds(base, CONV_CHUNK), :] = acc
            return carry

        lax.fori_loop(0, ts // CONV_CHUNK, conv_chunk, 0)
        keep, _ = _tril_masks()

        def block(bi, carry):
            rows = pl.ds(pl.multiple_of(bi * GBLK, GBLK), GBLK)
            xh, _ = _ln_stats(a1_ref[rows, :])
            a = _silu(xh * vec_ref[1:2, :] + vec_ref[2:3, :]) * _silu(z_ref[0, rows, 2 * W_BR:3 * W_BR])
            y_ref[0, rows, 0:W_BR] = a.astype(y_ref.dtype)
            ua = _gelu(z_ref[0, rows, 3 * W_BR:4 * W_BR])
            vh, _ = _ln_stats(_gelu(z_ref[0, rows, 4 * W_BR:5 * W_BR]))
            vl = vh * vec_ref[3:4, :] + vec_ref[4:5, :]
            sg = _spatial(ws_ref, keep, _mm(vl)) + bsf_ref[...]
            g = ua * sg * _silu(z_ref[0, rows, 5 * W_BR:6 * W_BR])
            y_ref[0, rows, W_BR:2 * W_BR] = g.astype(y_ref.dtype)
            return carry

        lax.fori_loop(0, ts // GBLK, block, 0)

    full = lambda shape: pl.BlockSpec(shape, lambda b, s: (0,) * len(shape))
    return pl.pallas_call(
        body, name="even_fwd", grid=(bsz, s_len // ts),
        in_specs=[pl.BlockSpec((1, ts, N_COLS), lambda b, s: (b, s, 0)), _halo_specs(ts, s_len, 2 * W_BR),
                  full((32, W_BR)), full((8, W_BR)), full((8, GBLK, GBLK)), full((GBLK, W_BR))],
        out_specs=pl.BlockSpec((1, ts, 2 * W_BR), lambda b, s: (b, s, 0)),
        out_shape=jax.ShapeDtypeStruct((bsz, s_len, 2 * W_BR), MM_DTYPE),
        scratch_shapes=[pltpu.VMEM((ext_rows + 8, W_BR), F32), pltpu.VMEM((8, ext_rows, W_BR), F32),
                        pltpu.VMEM((ts, W_BR), F32)],
        compiler_params=_cparams(("parallel", "parallel")),
    )(z3, z3, cw, vec, ws, bsf)


def _even_bwd(z3, dy3, cw, vec, ws, wst, bsf):
    bsz, s_len, _ = z3.shape
    ts = min(SEQ_TILE, s_len)
    n_s = s_len // ts
    ext_rows = ts + 2 * HALO
    a_rows = ts + HALO

    def body(z_ref, zl_ref, zr_ref, dy_ref, dyr_ref, cw_ref, vec_ref, ws_ref, wst_ref, bsf_ref,
             dz_ref, dbin_ref, dcw_ref, dvec_ref, dws_ref, dbsf_ref,
             ext_ref, sh_ref, a1_ref, ag_ref, dya_ref, d_ref, accw_ref):
        b = pl.program_id(0)
        s = pl.program_id(1)

        @pl.when((b == 0) & (s == 0))
        def _():
            dbin_ref[...] = jnp.zeros_like(dbin_ref)
            dcw_ref[...] = jnp.zeros_like(dcw_ref)
            dvec_ref[...] = jnp.zeros_like(dvec_ref)
            dws_ref[...] = jnp.zeros_like(dws_ref)
            dbsf_ref[...] = jnp.zeros_like(dbsf_ref)

        has_right = s < n_s - 1
        hl = zl_ref[0]
        hr = zr_ref[0]
        ext_ref[0:HALO, :] = jnp.where(s > 0, hl[:, 0:W_BR] * _sigmoid(hl[:, W_BR:2 * W_BR]), 0.0)
        ext_ref[HALO:HALO + ts, :] = z_ref[0, :, 0:W_BR] * _sigmoid(z_ref[0, :, W_BR:2 * W_BR])
        ext_ref[HALO + ts:ext_rows, :] = hr[:, 0:W_BR] * _sigmoid(hr[:, W_BR:2 * W_BR])
        ext_ref[ext_rows:ext_rows + 8, :] = jnp.zeros((8, W_BR), F32)
        ag_ref[0:ts, :] = z_ref[0, :, 2 * W_BR:3 * W_BR]
        ag_ref[ts:a_rows, :] = hr[:, 2 * W_BR:3 * W_BR]
        dya_ref[0:ts, :] = dy_ref[0, :, 0:W_BR]
        dya_ref[ts:a_rows, :] = dyr_ref[0][:, 0:W_BR]
        _build_shifts(ext_ref, sh_ref, ext_rows)

        def conv_chunk(ci, carry):
            base = pl.multiple_of(ci * CONV_CHUNK, CONV_CHUNK)
            acc = jnp.zeros((CONV_CHUNK, W_BR), F32) + vec_ref[0:1, :]
            for k in range(KA):
                q, r = divmod(2 + k, 8)
                acc = acc + sh_ref[r, pl.ds(base + 8 * q, CONV_CHUNK), :] * cw_ref[k:k + 1, :]
            a1_ref[pl.ds(base, CONV_CHUNK), :] = acc
            return carry

        lax.fori_loop(0, a_rows // CONV_CHUNK, conv_chunk, 0)

        def a_chunk(base, main):
            rows = pl.ds(base, CONV_CHUNK)
            xh, rs = _ln_stats(a1_ref[rows, :])
            ln = xh * vec_ref[1:2, :] + vec_ref[2:3, :]
            sl, dsl = _silu_grad(ln)
            sgt, dsgt = _silu_grad(ag_ref[rows, :])
            dya = dya_ref[rows, :]
            dln = dya * sgt * dsl
            da1 = _ln_bwd(dln * vec_ref[1:2, :], xh, rs)
            if main:
                d_ref[rows, :] = da1
                dag = dya * sl * dsgt
                dz_ref[0, rows, 2 * W_BR:3 * W_BR] = dag.astype(dz_ref.dtype)
                dbin_ref[0:1, 2 * W_BR:3 * W_BR] += _rowsum(dag)
                dvec_ref[0:1, :] += _rowsum(da1)
                dvec_ref[1:2, :] += _rowsum(dln * xh)
                dvec_ref[2:3, :] += _rowsum(dln)
            else:
                d_ref[rows, :] = jnp.where(has_right, da1, 0.0)

        def a_main(ci, carry):
            a_chunk(pl.multiple_of(ci * CONV_CHUNK, CONV_CHUNK), True)
            return carry

        lax.fori_loop(0, ts // CONV_CHUNK, a_main, 0)
        a_chunk(ts, False)
        d_ref[a_rows:a_rows + 8, :] = jnp.zeros((8, W_BR), F32)

        accw_ref[...] = jnp.zeros_like(accw_ref)

        def dw_chunk(ci, carry):
            base = pl.multiple_of(ci * CONV_CHUNK, CONV_CHUNK)
            d = d_ref[pl.ds(base, CONV_CHUNK), :]
            for k in range(KA):
                q, r = divmod(2 + k, 8)
                prod = d * sh_ref[r, pl.ds(base + 8 * q, CONV_CHUNK), :]
                accw_ref[k] += jnp.sum(prod.reshape(CONV_CHUNK // 8, 8, W_BR), axis=0)
            return carry

        lax.fori_loop(0, ts // CONV_CHUNK, dw_chunk, 0)
        dcw_ref[...] += jnp.sum(accw_ref[...], axis=1)

        _build_shifts(d_ref, sh_ref, a_rows)

        def dx_chunk(ci, carry):
            base = pl.multiple_of(ci * CONV_CHUNK, CONV_CHUNK)
            rows = pl.ds(base, CONV_CHUNK)
            acc = jnp.zeros((CONV_CHUNK, W_BR), F32)
            for m in range(KA):
                q, r = divmod(m, 8)
                acc = acc + sh_ref[r, pl.ds(base + 8 * q, CONV_CHUNK), :] * cw_ref[KA - 1 - m:KA - m, :]
            aval = z_ref[0, rows, 0:W_BR]
            sg = _sigmoid(z_ref[0, rows, W_BR:2 * W_BR])
            dval = acc * sg
            dglu = acc * aval * sg * (1.0 - sg)
            dz_ref[0, rows, 0:W_BR] = dval.astype(dz_ref.dtype)
            dz_ref[0, rows, W_BR:2 * W_BR] = dglu.astype(dz_ref.dtype)
            dbin_ref[0:1, 0:W_BR] += _rowsum(dval)
            dbin_ref[0:1, W_BR:2 * W_BR] += _rowsum(dglu)
            return carry

        lax.fori_loop(0, ts // CONV_CHUNK, dx_chunk, 0)

        keep, keep_t = _tril_masks()
        lane = lax.broadcasted_iota(jnp.int32, (GBLK, GBLK), 1)

        def block(bi, carry):
            rows = pl.ds(pl.multiple_of(bi * GBLK, GBLK), GBLK)
            ua, dua = _gelu_grad(z_ref[0, rows, 3 * W_BR:4 * W_BR])
            va, dva = _gelu_grad(z_ref[0, rows, 4 * W_BR:5 * W_BR])
            sgt, dsgt = _silu_grad(z_ref[0, rows, 5 * W_BR:6 * W_BR])
            vh, rs = _ln_stats(va)
            vlb = _mm(vh * vec_ref[3:4, :] + vec_ref[4:5, :])
            sg = _spatial(ws_ref, keep, vlb) + bsf_ref[...]
            dyg = dy_ref[0, rows, W_BR:2 * W_BR]
            du = dyg * sg * sgt * dua
            dsg = dyg * ua * sgt
            dgg = dyg * ua * sg * dsgt
            dvl = _spatial(wst_ref, keep_t, _mm(dsg))
            for p in range(4):
                dsp = dsg[:, p * GBLK:(p + 1) * GBLK]
                vlp = vlb[:, p * GBLK:(p + 1) * GBLK]
                dws_ref[2 * p] += jnp.where(keep, _dot_nt(jnp.where(lane < 64, dsp, 0.0), vlp), 0.0)
                dws_ref[2 * p + 1] += jnp.where(keep, _dot_nt(jnp.where(lane >= 64, dsp, 0.0), vlp), 0.0)
            dbsf_ref[...] += dsg
            dvec_ref[3:4, :] += _rowsum(dvl * vh)
            dvec_ref[4:5, :] += _rowsum(dvl)
            dv = _ln_bwd(dvl * vec_ref[3:4, :], vh, rs) * dva
            dz_ref[0, rows, 3 * W_BR:4 * W_BR] = du.astype(dz_ref.dtype)
            dz_ref[0, rows, 4 * W_BR:5 * W_BR] = dv.astype(dz_ref.dtype)
            dz_ref[0, rows, 5 * W_BR:6 * W_BR] = dgg.astype(dz_ref.dtype)
            dbin_ref[0:1, 3 * W_BR:4 * W_BR] += _rowsum(du)
            dbin_ref[0:1, 4 * W_BR:5 * W_BR] += _rowsum(dv)
            dbin_ref[0:1, 5 * W_BR:6 * W_BR] += _rowsum(dgg)
            return carry

        lax.fori_loop(0, ts // GBLK, block, 0)

    full = lambda shape: pl.BlockSpec(shape, lambda b, s: (0,) * len(shape))
    acc_shapes = [(1, N_COLS), (32, W_BR), (8, W_BR), (8, GBLK, GBLK), (GBLK, W_BR)]
    return pl.pallas_call(
        body, name="even_bwd", grid=(bsz, n_s),
        in_specs=[pl.BlockSpec((1, ts, N_COLS), lambda b, s: (b, s, 0)),
                  _halo_specs(ts, s_len, N_COLS, True), _halo_specs(ts, s_len, N_COLS, False),
                  pl.BlockSpec((1, ts, 2 * W_BR), lambda b, s: (b, s, 0)), _halo_specs(ts, s_len, 2 * W_BR, False),
                  full((32, W_BR)), full((8, W_BR)), full((8, GBLK, GBLK)), full((8, GBLK, GBLK)), full((GBLK, W_BR))],
        out_specs=[pl.BlockSpec((1, ts, N_COLS), lambda b, s: (b, s, 0))] + [full(sh) for sh in acc_shapes],
        out_shape=[jax.ShapeDtypeStruct((bsz, s_len, N_COLS), MM_DTYPE)] + [jax.ShapeDtypeStruct(sh, F32) for sh in acc_shapes],
        scratch_shapes=[pltpu.VMEM((ext_rows + 8, W_BR), F32), pltpu.VMEM((8, ext_rows, W_BR), F32),
                        pltpu.VMEM((a_rows, W_BR), F32), pltpu.VMEM((a_rows, W_BR), F32), pltpu.VMEM((a_rows, W_BR), F32),
                        pltpu.VMEM((a_rows + 8, W_BR), F32), pltpu.VMEM((32, 8, W_BR), F32)],
        compiler_params=_cparams(("arbitrary", "arbitrary")),
    )(z3, z3, z3, dy3, dy3, cw, vec, ws, wst, bsf)


def _pool_stages(e_refs, rows):
    e0, e1, e2, e3, e4 = e_refs
    e1[8:rows, :] = e0[8:rows, :] + e0[7:rows - 1, :]
    e2[16:rows, GBLK:] = e1[16:rows, GBLK:] + e1[14:rows - 2, GBLK:]
    e3[24:rows, 2 * GBLK:] = e2[24:rows, 2 * GBLK:] + e2[20:rows - 4, 2 * GBLK:]
    e4[32:rows, 3 * GBLK:] = e3[32:rows, 3 * GBLK:] + e3[24:rows - 8, 3 * GBLK:]


def _pool_counts(start, n):
    pos = (start + 1 + lax.broadcasted_iota(jnp.int32, (n, 1), 0)).astype(F32)
    return [jnp.minimum(pos, float(w)) for w in POOL_WINDOWS]


def _pooled_into(e_refs, pooled_ref, s, ts):
    cnt = _pool_counts(s * ts, ts)
    for g in range(4):
        cs = slice(g * GBLK, (g + 1) * GBLK)
        pooled_ref[:, cs] = e_refs[g + 1][HALO:HALO + ts, cs] / cnt[g] - e_refs[0][HALO:HALO + ts, cs]


def _odd_fwd(z3, wp, vec):
    bsz, s_len, _ = z3.shape
    ts = min(SEQ_TILE, s_len)
    ext_rows = ts + HALO

    def body(z_ref, zl_ref, wp_ref, vec_ref, y_ref, e0, e1, e2, e3, e4, pooled_ref, dext_ref, ec_ref):
        s = pl.program_id(1)
        hl = zl_ref[0]
        e0[0:HALO, :] = jnp.where(s > 0, hl[:, 0:W_BR], 0.0)
        e0[HALO:ext_rows, :] = z_ref[0, :, 0:W_BR]
        _pool_stages((e0, e1, e2, e3, e4), ext_rows)
        _pooled_into((e0, e1, e2, e3, e4), pooled_ref, s, ts)
        dext_ref[0:HALO, :] = jnp.where(s > 0, hl[:, 2 * W_BR:3 * W_BR] * hl[:, 4 * W_BR:5 * W_BR], 0.0)
        dext_ref[HALO:ext_rows, :] = z_ref[0, :, 2 * W_BR:3 * W_BR] * z_ref[0, :, 4 * W_BR:5 * W_BR]
        ec_ref[...] = (vec_ref[1:2, :] * dext_ref[HALO - 2:HALO - 2 + ts, :] + vec_ref[2:3, :] * dext_ref[HALO - 1:HALO - 1 + ts, :]
                       + vec_ref[3:4, :] * dext_ref[HALO:HALO + ts, :])

        def block(bi, carry):
            rows = pl.ds(pl.multiple_of(bi * GBLK, GBLK), GBLK)
            pb = _mm(pooled_ref[rows, :])
            cpre = jnp.concatenate([jnp.dot(pb[:, g * GBLK:(g + 1) * GBLK], wp_ref[g], preferred_element_type=F32)
                                    for g in range(4)], axis=1)
            c = cpre * vec_ref[0:1, :] * _silu(z_ref[0, rows, W_BR:2 * W_BR])
            d = z_ref[0, rows, 3 * W_BR:4 * W_BR] * ec_ref[rows, :] * _silu(z_ref[0, rows, 5 * W_BR:6 * W_BR])
            y_ref[0, rows, 0:W_BR] = c.astype(y_ref.dtype)
            y_ref[0, rows, W_BR:2 * W_BR] = d.astype(y_ref.dtype)
            return carry

        lax.fori_loop(0, ts // GBLK, block, 0)

    full = lambda shape: pl.BlockSpec(shape, lambda b, s: (0,) * len(shape))
    ebuf = pltpu.VMEM((ext_rows, W_BR), F32)
    return pl.pallas_call(
        body, name="odd_fwd", grid=(bsz, s_len // ts),
        in_specs=[pl.BlockSpec((1, ts, N_COLS), lambda b, s: (b, s, 0)), _halo_specs(ts, s_len, N_COLS),
                  full((4, GBLK, GBLK)), full((8, W_BR))],
        out_specs=pl.BlockSpec((1, ts, 2 * W_BR), lambda b, s: (b, s, 0)),
        out_shape=jax.ShapeDtypeStruct((bsz, s_len, 2 * W_BR), MM_DTYPE),
        scratch_shapes=[ebuf, ebuf, ebuf, ebuf, ebuf, pltpu.VMEM((ts, W_BR), F32), ebuf, pltpu.VMEM((ts, W_BR), F32)],
        compiler_params=_cparams(("parallel", "parallel")),
    )(z3, z3, wp, vec)


def _odd_bwd(z3, dy3, wp, vec):
    bsz, s_len, _ = z3.shape
    ts = min(SEQ_TILE, s_len)
    n_s = s_len // ts
    ext_rows = ts + HALO

    def body(z_ref, zl_ref, zr_ref, dy_ref, dyr_ref, wp_ref, vec_ref,
             dz_ref, dbin_ref, dwp_ref, dvec_ref,
             e0, e1, e2, e3, e4, pooled_ref, dext_ref, ec_ref, q_ref, dp_ref, de_ref, f1, f2, f3, f4):
        b = pl.program_id(0)
        s = pl.program_id(1)

        @pl.when((b == 0) & (s == 0))
        def _():
            dbin_ref[...] = jnp.zeros_like(dbin_ref)
            dwp_ref[...] = jnp.zeros_like(dwp_ref)
            dvec_ref[...] = jnp.zeros_like(dvec_ref)

        has_right = s < n_s - 1
        hl = zl_ref[0]
        e0[0:HALO, :] = jnp.where(s > 0, hl[:, 0:W_BR], 0.0)
        e0[HALO:ext_rows, :] = z_ref[0, :, 0:W_BR]
        _pool_stages((e0, e1, e2, e3, e4), ext_rows)
        _pooled_into((e0, e1, e2, e3, e4), pooled_ref, s, ts)
        dext_ref[0:HALO, :] = jnp.where(s > 0, hl[:, 2 * W_BR:3 * W_BR] * hl[:, 4 * W_BR:5 * W_BR], 0.0)
        dext_ref[HALO:ext_rows, :] = z_ref[0, :, 2 * W_BR:3 * W_BR] * z_ref[0, :, 4 * W_BR:5 * W_BR]
        ec_ref[...] = (vec_ref[1:2, :] * dext_ref[HALO - 2:HALO - 2 + ts, :] + vec_ref[2:3, :] * dext_ref[HALO - 1:HALO - 1 + ts, :]
                       + vec_ref[3:4, :] * dext_ref[HALO:HALO + ts, :])

        def grads(zc_gate, zd_b, zd_gate, dyc, dyd, rows_out, n, start, valid):
            sgt = _silu(zc_gate)
            dcpre = dyc * vec_ref[0:1, :] * sgt
            db = _mm(dcpre)
            dpool = jnp.concatenate([_dot_nt(db[:, g * GBLK:(g + 1) * GBLK], wp_ref[g]) for g in range(4)], axis=1)
            cnt = _pool_counts(start, n)
            q = jnp.concatenate([dpool[:, g * GBLK:(g + 1) * GBLK] / cnt[g] for g in range(4)], axis=1)
            de = dyd * zd_b * _silu(zd_gate)
            if valid is not None:
                q = jnp.where(valid, q, 0.0)
                de = jnp.where(valid, de, 0.0)
            q_ref[rows_out, :] = q
            dp_ref[rows_out, :] = dpool
            de_ref[rows_out, :] = de
            return dcpre

        def block(bi, carry):
            base = pl.multiple_of(bi * GBLK, GBLK)
            rows = pl.ds(base, GBLK)
            cg = z_ref[0, rows, W_BR:2 * W_BR]
            dyc = dy_ref[0, rows, 0:W_BR]
            dyd = dy_ref[0, rows, W_BR:2 * W_BR]
            d_b = z_ref[0, rows, 3 * W_BR:4 * W_BR]
            d_gate = z_ref[0, rows, 5 * W_BR:6 * W_BR]
            dcpre = grads(cg, d_b, d_gate, dyc, dyd, rows, GBLK, s * ts + base, None)
            pb = _mm(pooled_ref[rows, :])
            dcb = _mm(dcpre)
            cpre = jnp.concatenate([jnp.dot(pb[:, g * GBLK:(g + 1) * GBLK], wp_ref[g], preferred_element_type=F32)
                                    for g in range(4)], axis=1)
            for g in range(4):
                cs = slice(g * GBLK, (g + 1) * GBLK)
                dwp_ref[g] += _dot_tn(pb[:, cs], dcb[:, cs])
            sgt, dsgt = _silu_grad(cg)
            dvec_ref[0:1, :] += _rowsum(dyc * cpre * sgt)
            dcg = dyc * cpre * vec_ref[0:1, :] * dsgt
            sdt, dsdt = _silu_grad(d_gate)
            ec = ec_ref[rows, :]
            ddb = dyd * ec * sdt
            ddg = dyd * d_b * ec * dsdt
            dz_ref[0, rows, W_BR:2 * W_BR] = dcg.astype(dz_ref.dtype)
            dz_ref[0, rows, 3 * W_BR:4 * W_BR] = ddb.astype(dz_ref.dtype)
            dz_ref[0, rows, 5 * W_BR:6 * W_BR] = ddg.astype(dz_ref.dtype)
            dbin_ref[0:1, W_BR:2 * W_BR] += _rowsum(dcg)
            dbin_ref[0:1, 3 * W_BR:4 * W_BR] += _rowsum(ddb)
            dbin_ref[0:1, 5 * W_BR:6 * W_BR] += _rowsum(ddg)
            return carry

        lax.fori_loop(0, ts // GBLK, block, 0)
        hr = zr_ref[0]
        dyr = dyr_ref[0]
        grads(hr[:, W_BR:2 * W_BR], hr[:, 3 * W_BR:4 * W_BR], hr[:, 5 * W_BR:6 * W_BR], dyr[:, 0:W_BR], dyr[:, W_BR:2 * W_BR],
              slice(ts, ext_rows), HALO, (s + 1) * ts, has_right)

        r1, r2, r3 = ts + 24, ts + 16, ts + 8
        f1[0:r1, :] = q_ref[0:r1, :] + q_ref[1:r1 + 1, :]
        f2[0:r2, GBLK:] = f1[0:r2, GBLK:] + f1[2:r2 + 2, GBLK:]
        f3[0:r3, 2 * GBLK:] = f2[0:r3, 2 * GBLK:] + f2[4:r3 + 4, 2 * GBLK:]
        f4[0:ts, 3 * GBLK:] = f3[0:ts, 3 * GBLK:] + f3[8:ts + 8, 3 * GBLK:]
        for g, f in enumerate((f1, f2, f3, f4)):
            cs = slice(g * GBLK, (g + 1) * GBLK)
            dvg = f[0:ts, cs] - dp_ref[0:ts, cs]
            dz_ref[0, :, cs] = dvg.astype(dz_ref.dtype)
            dbin_ref[0:1, cs] += _rowsum(dvg)

        ddc = (vec_ref[1:2, :] * de_ref[2:ts + 2, :] + vec_ref[2:3, :] * de_ref[1:ts + 1, :] + vec_ref[3:4, :] * de_ref[0:ts, :])
        d_h = z_ref[0, :, 2 * W_BR:3 * W_BR]
        d_c = z_ref[0, :, 4 * W_BR:5 * W_BR]
        ddh = ddc * d_c
        ddcc = ddc * d_h
        dz_ref[0, :, 2 * W_BR:3 * W_BR] = ddh.astype(dz_ref.dtype)
        dz_ref[0, :, 4 * W_BR:5 * W_BR] = ddcc.astype(dz_ref.dtype)
        dbin_ref[0:1, 2 * W_BR:3 * W_BR] += _rowsum(ddh)
        dbin_ref[0:1, 4 * W_BR:5 * W_BR] += _rowsum(ddcc)
        de = de_ref[0:ts, :]
        for k in range(3):
            dvec_ref[1 + k:2 + k, :] += _rowsum(de * dext_ref[HALO - 2 + k:HALO - 2 + k + ts, :])

    full = lambda shape: pl.BlockSpec(shape, lambda b, s: (0,) * len(shape))
    acc_shapes = [(1, N_COLS), (4, GBLK, GBLK), (8, W_BR)]
    ebuf = pltpu.VMEM((ext_rows, W_BR), F32)
    tbuf = pltpu.VMEM((ts, W_BR), F32)
    return pl.pallas_call(
        body, name="odd_bwd", grid=(bsz, n_s),
        in_specs=[pl.BlockSpec((1, ts, N_COLS), lambda b, s: (b, s, 0)),
                  _halo_specs(ts, s_len, N_COLS, True), _halo_specs(ts, s_len, N_COLS, False),
                  pl.BlockSpec((1, ts, 2 * W_BR), lambda b, s: (b, s, 0)), _halo_specs(ts, s_len, 2 * W_BR, False),
                  full((4, GBLK, GBLK)), full((8, W_BR))],
        out_specs=[pl.BlockSpec((1, ts, N_COLS), lambda b, s: (b, s, 0))] + [full(sh) for sh in acc_shapes],
        out_shape=[jax.ShapeDtypeStruct((bsz, s_len, N_COLS), MM_DTYPE)] + [jax.ShapeDtypeStruct(sh, F32) for sh in acc_shapes],
        scratch_shapes=[ebuf, ebuf, ebuf, ebuf, ebuf, tbuf, ebuf, tbuf, ebuf, ebuf, ebuf, ebuf, ebuf, ebuf, tbuf],
        compiler_params=_cparams(("arbitrary", "arbitrary")),
    )(z3, z3, z3, dy3, dy3, wp, vec)


def _post_fwd(y2, x2, p2, w_out, wg, wple, vec):
    t, d = x2.shape
    tm = min(ROW_TILE, t)

    def body(y_ref, x_ref, p_ref, wo_ref, wg_ref, wp_ref, vec_ref, xn_ref, r_ref, gate_ref):
        r = ALPHA * x_ref[...] + jnp.dot(y_ref[...], wo_ref[...], preferred_element_type=F32) + vec_ref[0:1, :]
        r_ref[...] = r
        xh, _ = _ln_stats(r)
        h = xh * vec_ref[1:2, :] + vec_ref[2:3, :]
        gate = _sigmoid(_dot(h, wg_ref[...]) + vec_ref[3:4, :])
        gate_ref[...] = gate
        xn_ref[...] = h + gate * _dot(p_ref[...], wp_ref[...])

    row = lambda c: pl.BlockSpec((tm, c), lambda i: (i, 0))
    full = lambda shape: pl.BlockSpec(shape, lambda i: (0,) * len(shape))
    return pl.pallas_call(
        body, name="post_fwd", grid=(t // tm,),
        in_specs=[row(d), row(d), row(D_PLE), full((d, d)), full((d, d)), full((D_PLE, d)), full((8, d))],
        out_specs=[row(d), row(d), row(d)],
        out_shape=[jax.ShapeDtypeStruct((t, d), F32)] * 3,
        compiler_params=_cparams(("parallel",)),
    )(y2, x2, p2, w_out, wg, wple, vec)


def _post_bwd(dxn, r2, gate2, p2, y2, w_out, wg, wple, vec):
    t, d = r2.shape
    tm = min(BWD_ROW_TILE, t)

    def body(dxn_ref, r_ref, gate_ref, p_ref, y_ref, wo_ref, wg_ref, wp_ref, vec_ref,
             dxr_ref, dy_ref, dwo_ref, dwg_ref, dwp_ref, dvec_ref):
        @pl.when(pl.program_id(0) == 0)
        def _():
            dwo_ref[...] = jnp.zeros_like(dwo_ref)
            dwg_ref[...] = jnp.zeros_like(dwg_ref)
            dwp_ref[...] = jnp.zeros_like(dwp_ref)
            dvec_ref[...] = jnp.zeros_like(dvec_ref)

        dxn = dxn_ref[...]
        gate = gate_ref[...]
        xh, rs = _ln_stats(r_ref[...])
        hb = _mm(xh * vec_ref[1:2, :] + vec_ref[2:3, :])
        pb = _mm(p_ref[...])
        pe = jnp.dot(pb, wp_ref[...], preferred_element_type=F32)
        dpre = dxn * pe * gate * (1.0 - gate)
        dpb = _mm(dpre)
        dh = dxn + _dot_nt(dpb, wg_ref[...])
        dwg_ref[...] += _dot_tn(hb, dpb)
        dwp_ref[...] += _dot_tn(pb, dxn * gate)
        dr = _ln_bwd(dh * vec_ref[1:2, :], xh, rs)
        drb = _mm(dr)
        dxr_ref[...] = ALPHA * dr
        dy_ref[...] = _dot_nt(drb, wo_ref[...])
        dwo_ref[...] += _dot_tn(y_ref[...], drb)
        dvec_ref[0:1, :] += _rowsum(dr)
        dvec_ref[1:2, :] += _rowsum(dh * xh)
        dvec_ref[2:3, :] += _rowsum(dh)
        dvec_ref[3:4, :] += _rowsum(dpre)

    row = lambda c: pl.BlockSpec((tm, c), lambda i: (i, 0))
    full = lambda shape: pl.BlockSpec(shape, lambda i: (0,) * len(shape))
    acc_shapes = [(d, d), (d, d), (D_PLE, d), (8, d)]
    return pl.pallas_call(
        body, name="post_bwd", grid=(t // tm,),
        in_specs=[row(d), row(d), row(d), row(D_PLE), row(d), full((d, d)), full((d, d)), full((D_PLE, d)), full((8, d))],
        out_specs=[row(d), row(d)] + [full(sh) for sh in acc_shapes],
        out_shape=[jax.ShapeDtypeStruct((t, d), F32)] * 2 + [jax.ShapeDtypeStruct(sh, F32) for sh in acc_shapes],
        compiler_params=_cparams(("arbitrary",)),
    )(dxn, r2, gate2, p2, y2, w_out, wg, wple, vec)


def _loss_head(xf, tgt):
    t, d = xf.shape
    tm = min(ROW_TILE, t)

    def body(x_ref, t_ref, dx_ref, sq_ref):
        @pl.when(pl.program_id(0) == 0)
        def _():
            sq_ref[...] = jnp.zeros_like(sq_ref)
        e = x_ref[...] - t_ref[...]
        dx_ref[...] = e / float(d)
        sq_ref[...] += _rowsum(e * e)

    row = pl.BlockSpec((tm, d), lambda i: (i, 0))
    return pl.pallas_call(
        body, name="loss_head", grid=(t // tm,), in_specs=[row, row],
        out_specs=[row, pl.BlockSpec((1, d), lambda i: (0, 0))],
        out_shape=[jax.ShapeDtypeStruct((t, d), F32), jax.ShapeDtypeStruct((1, d), F32)],
        compiler_params=_cparams(("arbitrary",)),
    )(xf, tgt)


def _place():
    x, y, c = lax.axis_index("x"), lax.axis_index("y"), lax.axis_index("c")
    chips = [(1 - x, y), (x, 1 - y), (1 - x, 1 - y)]
    return x, y, c, chips


def _shard_of(ref, ax, k, width, lo=None, ln=None):
    idx = [slice(None)] * 3
    if lo is not None:
        idx[0] = pl.ds(lo, ln)
    idx[ax] = pl.ds(k * width, width)
    return ref.at[tuple(idx)]


def _remote(src, dst, ssem, rsem, dev):
    return pltpu.make_async_remote_copy(src_ref=src, dst_ref=dst, send_sem=ssem, recv_sem=rsem, device_id=dev, device_id_type=MESH)


def _gather_weights(shards, axes, sv):
    n = len(shards)
    fulls = []
    for a, ax in zip(shards, axes):
        sh = list(a.shape)
        sh[ax] *= 4
        fulls.append(jax.ShapeDtypeStruct(tuple(sh), a.dtype))

    def body(*refs):
        sh_refs, sv_ref = refs[:n], refs[n]
        fu_refs, sva_ref = refs[n + 1:2 * n + 1], refs[2 * n + 1]
        ssem, rsem, fssem, frsem, lsem = refs[2 * n + 2:]
        x, y, c, chips = _place()
        j = 2 * x + y
        locs, sends, fwds = [], [], []
        for a in range(n):
            cp = pltpu.make_async_copy(sh_refs[a], _shard_of(fu_refs[a], axes[a], j, sh_refs[a].shape[axes[a]]), lsem.at[a])
            cp.start()
            locs.append(cp)
        cp = pltpu.make_async_copy(sv_ref, sva_ref.at[j], lsem.at[n])
        cp.start()
        locs.append(cp)
        for a in range(n):
            lh = sh_refs[a].shape[0] // 2
            w = sh_refs[a].shape[axes[a]]
            for q, (qx, qy) in enumerate(chips):
                cp = _remote(sh_refs[a].at[pl.ds(c * lh, lh)], _shard_of(fu_refs[a], axes[a], j, w, c * lh, lh),
                             ssem.at[a * 3 + q], rsem.at[a * 3 + q], (qx, qy, c))
                cp.start()
                sends.append(cp)
        for q, (qx, qy) in enumerate(chips):
            cp = _remote(sv_ref, sva_ref.at[j], ssem.at[n * 3 + q], rsem.at[n * 3 + q], (qx, qy, c))
            cp.start()
            sends.append(cp)
        for a in range(n):
            lh = sh_refs[a].shape[0] // 2
            w = sh_refs[a].shape[axes[a]]
            for q, (qx, qy) in enumerate(chips):
                piece = _shard_of(fu_refs[a], axes[a], 2 * qx + qy, w, c * lh, lh)
                _remote(piece, piece, ssem.at[a * 3 + q], rsem.at[a * 3 + q], (x, y, c)).wait_recv()
                cp = _remote(piece, piece, fssem.at[a * 3 + q], frsem.at[a * 3 + q], (x, y, 1 - c))
                cp.start()
                fwds.append(cp)
        for q, (qx, qy) in enumerate(chips):
            _remote(sv_ref, sva_ref.at[2 * qx + qy], ssem.at[n * 3 + q], rsem.at[n * 3 + q], (x, y, c)).wait_recv()
        for a in range(n):
            lh = sh_refs[a].shape[0] // 2
            w = sh_refs[a].shape[axes[a]]
            for q, (qx, qy) in enumerate(chips):
                piece = _shard_of(fu_refs[a], axes[a], 2 * qx + qy, w, (1 - c) * lh, lh)
                _remote(piece, piece, fssem.at[a * 3 + q], frsem.at[a * 3 + q], (x, y, c)).wait_recv()
        for cp in sends + fwds:
            cp.wait_send()
        for cp in locs:
            cp.wait()

    nd = 3 * n + 3
    return pl.pallas_call(
        body, name="gather_weights",
        in_specs=[ANY] * (n + 1), out_specs=[ANY] * (n + 1),
        out_shape=fulls + [jax.ShapeDtypeStruct((4,) + sv.shape, sv.dtype)],
        scratch_shapes=[pltpu.SemaphoreType.DMA((nd,)), pltpu.SemaphoreType.DMA((nd,)),
                        pltpu.SemaphoreType.DMA((3 * n,)), pltpu.SemaphoreType.DMA((3 * n,)), pltpu.SemaphoreType.DMA((n + 1,))],
        compiler_params=pltpu.CompilerParams(has_side_effects=True),
    )(*shards, sv)


def _pair_exchange(grads, small):
    n = len(grads)
    outs = [jax.ShapeDtypeStruct((g.shape[0] // 2,) + g.shape[1:], g.dtype) for g in grads]
    outs.append(jax.ShapeDtypeStruct((small.shape[0] // 2, small.shape[1]), small.dtype))

    def body(*refs):
        g_refs, o_refs = refs[:n + 1], refs[n + 1:2 * n + 2]
        ssem, rsem = refs[2 * n + 2:]
        x, y, c, _ = _place()
        cps = []
        for a in range(n + 1):
            lh = g_refs[a].shape[0] // 2
            cp = _remote(g_refs[a].at[pl.ds((1 - c) * lh, lh)], o_refs[a], ssem.at[a], rsem.at[a], (x, y, 1 - c))
            cp.start()
            cps.append(cp)
        for cp in cps:
            cp.wait()

    return pl.pallas_call(
        body, name="pair_exchange", in_specs=[ANY] * (n + 1), out_specs=[ANY] * (n + 1), out_shape=outs,
        scratch_shapes=[pltpu.SemaphoreType.DMA((n + 1,)), pltpu.SemaphoreType.DMA((n + 1,))],
        compiler_params=pltpu.CompilerParams(has_side_effects=True),
    )(*grads, small)


def _chip_scatter(sums, axes, small):
    n = len(sums)
    outs = []
    for g, ax in zip(sums, axes):
        sh = list(g.shape)
        sh[ax] //= 4
        outs.append(jax.ShapeDtypeStruct((3,) + tuple(sh), g.dtype))
    rq = small.shape[0] // 4
    outs.append(jax.ShapeDtypeStruct((3, rq, small.shape[1]), small.dtype))

    def body(*refs):
        g_refs, o_refs = refs[:n + 1], refs[n + 1:2 * n + 2]
        ssem, rsem = refs[2 * n + 2:]
        x, y, c, chips = _place()
        cps = []
        for a in range(n + 1):
            for q, (qx, qy) in enumerate(chips):
                k = 2 * qx + qy
                if a < n:
                    src = _shard_of(g_refs[a], axes[a], k, g_refs[a].shape[axes[a]] // 4)
                else:
                    src = g_refs[a].at[pl.ds(k * rq, rq)]
                cp = _remote(src, o_refs[a].at[q], ssem.at[a * 3 + q], rsem.at[a * 3 + q], (qx, qy, c))
                cp.start()
                cps.append(cp)
        for cp in cps:
            cp.wait()

    return pl.pallas_call(
        body, name="chip_scatter", in_specs=[ANY] * (n + 1), out_specs=[ANY] * (n + 1), out_shape=outs,
        scratch_shapes=[pltpu.SemaphoreType.DMA((3 * n + 3,)), pltpu.SemaphoreType.DMA((3 * n + 3,))],
        compiler_params=pltpu.CompilerParams(has_side_effects=True),
    )(*sums, small)


def _final_exchange(reds, small):
    n = len(reds)
    outs = [jax.ShapeDtypeStruct((2 * g.shape[0],) + g.shape[1:], g.dtype) for g in reds]
    outs.append(jax.ShapeDtypeStruct((8,) + small.shape, small.dtype))
    flips = [(fx, fy, fc) for fx in (0, 1) for fy in (0, 1) for fc in (0, 1)][1:]

    def body(*refs):
        g_refs, o_refs = refs[:n + 1], refs[n + 1:2 * n + 2]
        ssem, rsem, lsem = refs[2 * n + 2:]
        x, y, c, _ = _place()
        cps, locs = [], []
        for a in range(n):
            lh = g_refs[a].shape[0]
            dst = o_refs[a].at[pl.ds(c * lh, lh)]
            cp = pltpu.make_async_copy(g_refs[a], dst, lsem.at[a])
            cp.start()
            locs.append(cp)
            cp = _remote(g_refs[a], dst, ssem.at[a], rsem.at[a], (x, y, 1 - c))
            cp.start()
            cps.append(cp)
        mine = 4 * c + 2 * x + y
        cp = pltpu.make_async_copy(g_refs[n], o_refs[n].at[mine], lsem.at[n])
        cp.start()
        locs.append(cp)
        for f, (fx, fy, fc) in enumerate(flips):
            cp = _remote(g_refs[n], o_refs[n].at[mine], ssem.at[n + f], rsem.at[n + f], (x ^ fx, y ^ fy, c ^ fc))
            cp.start()
            cps.append(cp)
        for cp in cps:
            cp.wait()
        for cp in locs:
            cp.wait()

    return pl.pallas_call(
        body, name="final_exchange", in_specs=[ANY] * (n + 1), out_specs=[ANY] * (n + 1), out_shape=outs,
        scratch_shapes=[pltpu.SemaphoreType.DMA((n + 7,)), pltpu.SemaphoreType.DMA((n + 7,)), pltpu.SemaphoreType.DMA((n + 1,))],
        compiler_params=pltpu.CompilerParams(has_side_effects=True),
    )(*reds, small)


def _pair_sum(g, got, half):
    lh, a_dim, b_dim = got.shape
    tr = min(256, a_dim)

    def body(half_ref, g_ref, o_ref, s_ref):
        s_ref[...] = g_ref[...] + o_ref[...]

    return pl.pallas_call(
        body, name="pair_sum",
        grid_spec=pltpu.PrefetchScalarGridSpec(
            num_scalar_prefetch=1, grid=(lh, a_dim // tr),
            in_specs=[pl.BlockSpec((1, tr, b_dim), lambda l, i, h: (h[0] * lh + l, i, 0)),
                      pl.BlockSpec((1, tr, b_dim), lambda l, i, h: (l, i, 0))],
            out_specs=pl.BlockSpec((1, tr, b_dim), lambda l, i, h: (l, i, 0))),
        out_shape=jax.ShapeDtypeStruct(got.shape, F32),
        compiler_params=_cparams(("parallel", "parallel")),
    )(half, g, got)


def _chip_sum(own, got, ax, chip):
    _, lh, a_dim, b_dim = got.shape
    tr = min(256, a_dim)
    if ax == 2:
        own_spec = pl.BlockSpec((1, tr, b_dim), lambda l, i, k: (l, i, k[0]))
    else:
        per = a_dim // tr
        own_spec = pl.BlockSpec((1, tr, b_dim), lambda l, i, k: (l, k[0] * per + i, 0))

    def body(k_ref, own_ref, got_ref, s_ref):
        s_ref[...] = ((own_ref[...] + got_ref[0]) + got_ref[1]) + got_ref[2]

    return pl.pallas_call(
        body, name="chip_sum",
        grid_spec=pltpu.PrefetchScalarGridSpec(
            num_scalar_prefetch=1, grid=(lh, a_dim // tr),
            in_specs=[own_spec, pl.BlockSpec((3, 1, tr, b_dim), lambda l, i, k: (0, l, i, 0))],
            out_specs=pl.BlockSpec((1, tr, b_dim), lambda l, i, k: (l, i, 0))),
        out_shape=jax.ShapeDtypeStruct((lh, a_dim, b_dim), F32),
        compiler_params=_cparams(("parallel", "parallel")),
    )(chip, own, got)


def _small_sums(small, got_pair, got_chips, half, chip):
    r = small.shape[0]
    rh, rq = r // 2, r // 8

    def body1(h_ref, g_ref, o_ref, s_ref):
        s_ref[...] = g_ref[...] + o_ref[...]

    s1 = pl.pallas_call(
        body1, name="small_pair_sum",
        grid_spec=pltpu.PrefetchScalarGridSpec(
            num_scalar_prefetch=1, grid=(1,),
            in_specs=[pl.BlockSpec((rh, 128), lambda i, h: (h[0], 0)), pl.BlockSpec((rh, 128), lambda i, h: (0, 0))],
            out_specs=pl.BlockSpec((rh, 128), lambda i, h: (0, 0))),
        out_shape=jax.ShapeDtypeStruct((rh, 128), F32),
    )(half, small, got_pair)
    return s1, rq


def _small_chip_sum(s1, got, chip):
    rq = got.shape[1]

    def body(k_ref, own_ref, got_ref, s_ref):
        s_ref[...] = ((own_ref[...] + got_ref[0]) + got_ref[1]) + got_ref[2]

    return pl.pallas_call(
        body, name="small_chip_sum",
        grid_spec=pltpu.PrefetchScalarGridSpec(
            num_scalar_prefetch=1, grid=(1,),
            in_specs=[pl.BlockSpec((rq, 128), lambda i, k: (k[0], 0)), pl.BlockSpec((3, rq, 128), lambda i, k: (0, 0, 0))],
            out_specs=pl.BlockSpec((rq, 128), lambda i, k: (0, 0))),
        out_shape=jax.ShapeDtypeStruct((rq, 128), F32),
    )(chip, s1, got)


def _adam_math(w, g, m, v):
    m = ADAM_B1 * m + (1.0 - ADAM_B1) * g
    v = ADAM_B2 * v + (1.0 - ADAM_B2) * (g * g)
    m_hat = m / (1.0 - ADAM_B1 ** ADAM_STEP)
    v_hat = v / (1.0 - ADAM_B2 ** ADAM_STEP)
    return -ADAM_LR * (m_hat / (jnp.sqrt(v_hat) + ADAM_EPS) + ADAM_WD * w), m, v


def _adamw_big(w, g, m, v):
    l_dim, a_dim, b_dim = w.shape
    tr = min(256, a_dim)

    def body(w_ref, g_ref, m_ref, v_ref, d_ref, nm_ref, nv_ref):
        d_ref[...], nm_ref[...], nv_ref[...] = _adam_math(w_ref[...], g_ref[...], m_ref[...], v_ref[...])

    blk = pl.BlockSpec((1, tr, b_dim), lambda l, i: (l, i, 0))
    return pl.pallas_call(
        body, name="adamw_big", grid=(l_dim, a_dim // tr), in_specs=[blk] * 4, out_specs=[blk] * 3,
        out_shape=[jax.ShapeDtypeStruct(w.shape, F32)] * 3,
        compiler_params=_cparams(("parallel", "parallel")),
    )(w, g, m, v)


def _adamw_small(ws, gs, ms, vs):
    n = len(ws)

    def body(*refs):
        for i in range(n):
            w_ref, g_ref, m_ref, v_ref = refs[i], refs[n + i], refs[2 * n + i], refs[3 * n + i]
            d_ref, nm_ref, nv_ref = refs[4 * n + i], refs[5 * n + i], refs[6 * n + i]
            d_ref[...], nm_ref[...], nv_ref[...] = _adam_math(w_ref[...], g_ref[...], m_ref[...], v_ref[...])

    shapes = [jax.ShapeDtypeStruct(w.shape, F32) for w in ws]
    outs = pl.pallas_call(body, name="adamw_small", out_shape=shapes * 3,
                          compiler_params=_cparams())(*ws, *gs, *ms, *vs)
    return outs[:n], outs[n:2 * n], outs[2 * n:]


def _pack(arrs, row_mult):
    parts = []
    for a in arrs:
        flat = a.reshape(-1)
        pad = (-flat.shape[0]) % 1024
        parts.append(jnp.pad(flat, (0, pad)).reshape(-1, 128))
    buf = jnp.concatenate(parts, axis=0)
    pad = (-buf.shape[0]) % row_mult
    return jnp.pad(buf, ((0, pad), (0, 0)))


def _unpack(buf, shapes):
    out, row = [], 0
    for sh in shapes:
        n = math.prod(sh)
        rows = -(-n // 1024) * 8
        out.append(buf[row:row + rows].reshape(-1)[:n].reshape(sh))
        row += rows
    return out


_NAMES = ['w_in_e', 'b_in_e', 'conv_a_w', 'conv_a_b', 'ln_a_g', 'ln_a_b', 'ln_v_g', 'ln_v_b', 'w_s', 'b_s', 'w_out_e', 'b_out_e',
          'w_in_o', 'b_in_o', 'w_pool', 'pool_scale', 'conv_d_w', 'w_out_o', 'b_out_o', 'ln_g', 'ln_b', 'w_ple', 'w_ple_gate',
          'b_ple_gate']
_BIG = ['w_in_e', 'w_out_e', 'w_in_o', 'w_out_o', 'w_ple', 'w_ple_gate']
_BIG_AXES = [2, 1, 2, 1, 2, 1]
_SMALL_SHARDED = ['conv_a_w', 'b_in_o', 'pool_scale', 'conv_d_w', 'b_out_o']


def kernel(x, p, w_in_e, b_in_e, conv_a_w, conv_a_b, ln_a_g, ln_a_b, ln_v_g, ln_v_b, w_s, b_s, w_out_e, b_out_e, w_in_o, b_in_o, w_pool, pool_scale, conv_d_w, w_out_o, b_out_o, ln_g, ln_b, w_ple, w_ple_gate, b_ple_gate, loss_target, m_w_in_e, m_b_in_e, m_conv_a_w, m_conv_a_b, m_ln_a_g, m_ln_a_b, m_ln_v_g, m_ln_v_b, m_w_s, m_b_s, m_w_out_e, m_b_out_e, m_w_in_o, m_b_in_o, m_w_pool, m_pool_scale, m_conv_d_w, m_w_out_o, m_b_out_o, m_ln_g, m_ln_b, m_w_ple, m_w_ple_gate, m_b_ple_gate, v_w_in_e, v_b_in_e, v_conv_a_w, v_conv_a_b, v_ln_a_g, v_ln_a_b, v_ln_v_g, v_ln_v_b, v_w_s, v_b_s, v_w_out_e, v_b_out_e, v_w_in_o, v_b_in_o, v_w_pool, v_pool_scale, v_conv_d_w, v_w_out_o, v_b_out_o, v_ln_g, v_ln_b, v_w_ple, v_w_ple_gate, v_b_ple_gate):
    args = locals()
    wts = {n: args[n] for n in _NAMES}
    mom = {n: args["m_" + n] for n in _NAMES}
    var = {n: args["v_" + n] for n in _NAMES}
    bsz, s_len, d = x.shape
    t = bsz * s_len
    cx, cy, cc = lax.axis_index("x"), lax.axis_index("y"), lax.axis_index("c")
    chip = (2 * cx + cy).astype(jnp.int32).reshape(1)
    half = cc.astype(jnp.int32).reshape(1)

    sv = _pack([wts[n] for n in _SMALL_SHARDED], 8)
    *full_big, sv_all = _gather_weights([wts[n].astype(MM_DTYPE) for n in _BIG], _BIG_AXES, sv)
    fw = dict(zip(_BIG, full_big))
    small_parts = [_unpack(sv_all[k], [wts[n].shape for n in _SMALL_SHARDED]) for k in range(4)]
    for i, n in enumerate(_SMALL_SHARDED):
        fw[n] = jnp.concatenate([small_parts[k][i] for k in range(4)], axis=-1)
    for n in _NAMES:
        fw.setdefault(n, wts[n])

    def row8(rows, width):
        rows = [r.reshape(1, width) for r in rows]
        return jnp.concatenate(rows + [jnp.zeros((8 - len(rows), width), F32)], axis=0)

    x2 = x.reshape(t, d)
    saved = []
    for i in range(DEPTH):
        j = i // 2
        even = i % 2 == 0
        w_in, b_in, w_out, b_out = ((fw['w_in_e'], fw['b_in_e'], fw['w_out_e'], fw['b_out_e']) if even else
                                    (fw['w_in_o'], fw['b_in_o'], fw['w_out_o'], fw['b_out_o']))
        z = _in_proj(x2, w_in[j], b_in[j].reshape(1, N_COLS))
        z3 = z.reshape(bsz, s_len, N_COLS)
        if even:
            cw = jnp.concatenate([fw['conv_a_w'][j], jnp.zeros((1, W_BR), F32)], axis=0)
            mvec = row8([fw['conv_a_b'][j], fw['ln_a_g'][j], fw['ln_a_b'][j], fw['ln_v_g'][j], fw['ln_v_b'][j]], W_BR)
            bsf = jnp.repeat(fw['b_s'][j].T, W_BR // 8, axis=1)
            mix = (cw, mvec, fw['w_s'][j], jnp.swapaxes(fw['w_s'][j], 1, 2), bsf)
            y3 = _even_fwd(z3, cw, mvec, fw['w_s'][j], bsf)
        else:
            mvec = row8([fw['pool_scale'][j]] + [fw['conv_d_w'][j][k] for k in range(3)], W_BR)
            mix = (fw['w_pool'][j].astype(MM_DTYPE), mvec)
            y3 = _odd_fwd(z3, mix[0], mvec)
        pvec = row8([b_out[j], fw['ln_g'][i], fw['ln_b'][i], fw['b_ple_gate'][i]], d)
        post_w = (w_out[j], fw['w_ple_gate'][i], fw['w_ple'][i], pvec)
        p2 = p[i].reshape(t, D_PLE)
        y2 = y3.reshape(t, 2 * W_BR)
        xn, r2, gate2 = _post_fwd(y2, x2, p2, *post_w)
        saved.append((x2, z3, y2, r2, gate2, p2, w_in[j], mix, post_w))
        x2 = xn

    dx, sq = _loss_head(x2, loss_target.reshape(t, d))
    loss = lax.psum(0.5 * jnp.sum(sq) / d, ("x", "y", "c"))

    gr = {n: [None] * wts[n].shape[0] for n in _NAMES}
    for i in reversed(range(DEPTH)):
        j = i // 2
        even = i % 2 == 0
        x_in, z3, y2, r2, gate2, p2, w_in, mix, post_w = saved[i]
        dxr, dy, dwo, dwg, dwp, dpv = _post_bwd(dx, r2, gate2, p2, y2, *post_w)
        dy3 = dy.reshape(bsz, s_len, 2 * W_BR)
        sfx = '_e' if even else '_o'
        gr['w_out' + sfx][j], gr['b_out' + sfx][j] = dwo, dpv[0]
        gr['w_ple_gate'][i], gr['w_ple'][i] = dwg, dwp
        gr['ln_g'][i], gr['ln_b'][i], gr['b_ple_gate'][i] = dpv[1], dpv[2], dpv[3]
        if even:
            dz3, dbin, dcw, dmv, dws, dbsf = _even_bwd(z3, dy3, *mix)
            gr['conv_a_w'][j], gr['conv_a_b'][j] = dcw[:KA], dmv[0]
            gr['ln_a_g'][j], gr['ln_a_b'][j], gr['ln_v_g'][j], gr['ln_v_b'][j] = dmv[1], dmv[2], dmv[3], dmv[4]
            gr['w_s'][j] = dws
            gr['b_s'][j] = jnp.sum(dbsf.reshape(GBLK, 8, W_BR // 8), axis=2).T
        else:
            dz3, dbin, dwpool, dmv = _odd_bwd(z3, dy3, *mix)
            gr['w_pool'][j], gr['pool_scale'][j], gr['conv_d_w'][j] = dwpool, dmv[0], dmv[1:4]
        gr['b_in' + sfx][j] = dbin[0]
        dz2 = dz3.reshape(t, N_COLS)
        gr['w_in' + sfx][j] = _in_proj_bwd_dw(x_in, dz2)
        dx = _in_proj_bwd_dx(dxr, dz2, w_in)
    grad_x = dx.reshape(bsz, s_len, d)

    small_names = [n for n in _NAMES if n not in _BIG]
    g_big = [jnp.stack(gr[n]) for n in _BIG]
    g_small_full = [jnp.stack(gr[n]) for n in small_names]
    small = _pack(g_small_full, 64)
    *got_pair, got_small = _pair_exchange(g_big, small)
    sums = [_pair_sum(g, o, half) for g, o in zip(g_big, got_pair)]
    s1, _ = _small_sums(small, got_small, None, half, chip)
    *got_chips, got_small2 = _chip_scatter(sums, _BIG_AXES, s1)
    reds = [_chip_sum(sm, o, ax, chip) for sm, o, ax in zip(sums, got_chips, _BIG_AXES)]
    s2 = _small_chip_sum(s1, got_small2, chip)
    *g_shards, small_all = _final_exchange(reds, s2)
    g_small = _unpack(small_all.reshape(small.shape), [g.shape for g in g_small_full])
    grads = dict(zip(_BIG, g_shards))
    for n, g in zip(small_names, g_small):
        if n in _SMALL_SHARDED:
            w = wts[n].shape[-1]
            g = lax.dynamic_slice_in_dim(g, (2 * cx + cy) * w, w, axis=g.ndim - 1)
        grads[n] = g

    delta, new_m, new_v = {}, {}, {}
    for n in _BIG:
        delta[n], new_m[n], new_v[n] = _adamw_big(wts[n], grads[n], mom[n], var[n])
    ds, ms, vs = _adamw_small([wts[n] for n in small_names], [grads[n] for n in small_names],
                              [mom[n] for n in small_names], [var[n] for n in small_names])
    for n, a, b, c_ in zip(small_names, ds, ms, vs):
        delta[n], new_m[n], new_v[n] = a, b, c_

    return (loss, grad_x, *[grads[n] for n in _NAMES], *[delta[n] for n in _NAMES],
            *[new_m[n] for n in _NAMES], *[new_v[n] for n in _NAMES])
```

```python
import functools
import math

import jax
import jax.numpy as jnp
from jax import lax
from jax.experimental import pallas as pl
from jax.experimental.pallas import tpu as pltpu

F32 = jnp.float32
MM_DTYPE = jnp.bfloat16
WIRE_DTYPE = jnp.bfloat16
SEQ_TILE = 512
ROW_TILE = 512
BWD_ROW_TILE = 256
HALO = 32
CONV_CHUNK = 32
GBLK = 128
VMEM_LIMIT = 56 * 1024 * 1024

D_MODEL = 1024
W_BR = 512
N_COLS = 6 * W_BR
D_PLE = 256
KA = 31
DEPTH = 4
POOL_WINDOWS = (2, 4, 8, 16)
ALPHA = (2.0 * DEPTH) ** 0.25
LN_EPS = 1e-5
GELU_C = math.sqrt(2.0 / math.pi)

ADAM_LR, ADAM_B1, ADAM_B2, ADAM_EPS, ADAM_WD, ADAM_STEP = 0.001, 0.9, 0.999, 1e-08, 0.01, 10

MESH = pl.DeviceIdType.MESH
ANY = pl.BlockSpec(memory_space=pl.ANY)


def _cparams(sem=None):
    return pltpu.CompilerParams(dimension_semantics=sem, vmem_limit_bytes=VMEM_LIMIT)


def _sigmoid(x):
    return 1.0 / (1.0 + jnp.exp(-x))


def _silu(x):
    return x * _sigmoid(x)


def _silu_grad(x):
    s = _sigmoid(x)
    return x * s, s * (1.0 + x * (1.0 - s))


def _gelu(x):
    return 0.5 * x * (1.0 + jnp.tanh(GELU_C * (x + 0.044715 * (x * x * x))))


def _gelu_grad(x):
    x2 = x * x
    th = jnp.tanh(GELU_C * (x + 0.044715 * (x * x2)))
    return 0.5 * x * (1.0 + th), 0.5 * (1.0 + th) + 0.5 * x * (1.0 - th * th) * (GELU_C * (1.0 + 3.0 * 0.044715 * x2))


def _ln_stats(x):
    mu = jnp.mean(x, axis=-1, keepdims=True)
    d = x - mu
    var = jnp.mean(d * d, axis=-1, keepdims=True)
    rs = lax.rsqrt(var + LN_EPS)
    return d * rs, rs


def _ln_bwd(dxh, xh, rs):
    return rs * (dxh - jnp.mean(dxh, axis=-1, keepdims=True) - xh * jnp.mean(dxh * xh, axis=-1, keepdims=True))


def _mm(a):
    return a.astype(MM_DTYPE)


def _dot(a, b):
    return jnp.dot(_mm(a), _mm(b), preferred_element_type=F32)


def _dot_nt(a, b):
    return lax.dot_general(_mm(a), _mm(b), (((1,), (1,)), ((), ())), preferred_element_type=F32)


def _dot_tn(a, b):
    return lax.dot_general(_mm(a), _mm(b), (((0,), (0,)), ((), ())), preferred_element_type=F32)


def _rowsum(x):
    return jnp.sum(x, axis=0, keepdims=True)


def _in_proj(x2, w, b):
    t, d = x2.shape
    n = w.shape[1]
    tm = min(ROW_TILE, t)
    nc = 768

    def body(x_ref, w_ref, b_ref, z_ref):
        xb = _mm(x_ref[...])
        for j in range(n // nc):
            cs = slice(j * nc, (j + 1) * nc)
            z_ref[:, cs] = jnp.dot(xb, w_ref[:, cs], preferred_element_type=F32) + b_ref[:, cs]

    return pl.pallas_call(
        body, name="in_proj", grid=(t // tm,),
        in_specs=[pl.BlockSpec((tm, d), lambda i: (i, 0)), pl.BlockSpec((d, n), lambda i: (0, 0)),
                  pl.BlockSpec((1, n), lambda i: (0, 0))],
        out_specs=pl.BlockSpec((tm, n), lambda i: (i, 0)),
        out_shape=jax.ShapeDtypeStruct((t, n), F32),
        compiler_params=_cparams(("parallel",)),
    )(x2, w, b)


def _in_proj_bwd_dx(dxr, dz, w):
    t, d = dxr.shape
    n = w.shape[1]
    tm = min(ROW_TILE, t)

    def body(dxr_ref, dz_ref, w_ref, dx_ref):
        dx_ref[...] = dxr_ref[...] + _dot_nt(dz_ref[...], w_ref[...])

    return pl.pallas_call(
        body, name="in_proj_bwd_dx", grid=(t // tm,),
        in_specs=[pl.BlockSpec((tm, d), lambda i: (i, 0)), pl.BlockSpec((tm, n), lambda i: (i, 0)),
                  pl.BlockSpec((d, n), lambda i: (0, 0))],
        out_specs=pl.BlockSpec((tm, d), lambda i: (i, 0)),
        out_shape=jax.ShapeDtypeStruct((t, d), F32),
        compiler_params=_cparams(("parallel",)),
    )(dxr, dz, w)


def _in_proj_bwd_dw(x2, dz):
    t, d = x2.shape
    n = dz.shape[1]
    tm = min(ROW_TILE, t)
    nc = 768

    def body(x_ref, dz_ref, dw_ref):
        @pl.when(pl.program_id(1) == 0)
        def _():
            dw_ref[...] = jnp.zeros_like(dw_ref)
        dw_ref[...] += _dot_tn(x_ref[...], dz_ref[...])

    return pl.pallas_call(
        body, name="in_proj_bwd_dw", grid=(n // nc, t // tm),
        in_specs=[pl.BlockSpec((tm, d), lambda j, i: (i, 0)), pl.BlockSpec((tm, nc), lambda j, i: (i, j))],
        out_specs=pl.BlockSpec((d, nc), lambda j, i: (0, j)),
        out_shape=jax.ShapeDtypeStruct((d, n), F32),
        compiler_params=_cparams(("parallel", "arbitrary")),
    )(x2, dz)


def _halo_specs(ts, s_len, cols, left=True):
    per = ts // HALO
    last = s_len // HALO - 1
    if left:
        return pl.BlockSpec((1, HALO, cols), lambda b, s: (b, jnp.maximum(s * per - 1, 0), 0))
    return pl.BlockSpec((1, HALO, cols), lambda b, s: (b, jnp.minimum((s + 1) * per, last), 0))


def _build_shifts(src_ref, sh_ref, rows):
    for r in range(1, 8):
        sh_ref[r - 1, 0:rows, :] = src_ref[r:r + rows, :]


def _shifted(src_ref, sh_ref, r, start, n):
    if r == 0:
        return src_ref[pl.ds(start, n), :]
    return sh_ref[r - 1, pl.ds(start, n), :]


def _tril_masks():
    ri = lax.broadcasted_iota(jnp.int32, (GBLK, GBLK), 0)
    ci = lax.broadcasted_iota(jnp.int32, (GBLK, GBLK), 1)
    return ri >= ci, ci >= ri


def _spatial(w_ref, keep, vb):
    lane = lax.broadcasted_iota(jnp.int32, (GBLK, GBLK), 1)
    outs = []
    for p in range(4):
        xs = vb[:, p * GBLK:(p + 1) * GBLK]
        r0 = jnp.dot(_mm(jnp.where(keep, w_ref[2 * p], 0.0)), xs, preferred_element_type=F32)
        r1 = jnp.dot(_mm(jnp.where(keep, w_ref[2 * p + 1], 0.0)), xs, preferred_element_type=F32)
        outs.append(jnp.where(lane < 64, r0, r1))
    return jnp.concatenate(outs, axis=1)


def _even_fwd(z3, cw, vec, ws, bsf):
    bsz, s_len, _ = z3.shape
    ts = min(SEQ_TILE, s_len)
    ext_rows = ts + HALO

    def body(z_ref, zl_ref, cw_ref, vec_ref, ws_ref, bsf_ref, y_ref, ext_ref, sh_ref, a1_ref):
        s = pl.program_id(1)
        hl = zl_ref[0]
        a0h = hl[:, 0:W_BR] * _sigmoid(hl[:, W_BR:2 * W_BR])
        ext_ref[0:HALO, :] = jnp.where(s > 0, a0h, 0.0)
        ext_ref[HALO:ext_rows, :] = z_ref[0, :, 0:W_BR] * _sigmoid(z_ref[0, :, W_BR:2 * W_BR])
        ext_ref[ext_rows:ext_rows + 8, :] = jnp.zeros((8, W_BR), F32)
        _build_shifts(ext_ref, sh_ref, ext_rows)

        def conv_chunk(ci, carry):
            base = pl.multiple_of(ci * CONV_CHUNK, CONV_CHUNK)
            acc = jnp.zeros((CONV_CHUNK, W_BR), F32) + vec_ref[0:1, :]
            for k in range(KA):
                q, r = divmod(2 + k, 8)
                acc = acc + _shifted(ext_ref, sh_ref, r, base + 8 * q, CONV_CHUNK) * cw_ref[k:k + 1, :]
            a1_ref[pl.ds(base, CONV_CHUNK), :] = acc
            return carry

        lax.fori_loop(0, ts // CONV_CHUNK, conv_chunk, 0)
        keep, _ = _tril_masks()

        def block(bi, carry):
            rows = pl.ds(pl.multiple_of(bi * GBLK, GBLK), GBLK)
            xh, _ = _ln_stats(a1_ref[rows, :])
            a = _silu(xh * vec_ref[1:2, :] + vec_ref[2:3, :]) * _silu(z_ref[0, rows, 2 * W_BR:3 * W_BR])
            y_ref[0, rows, 0:W_BR] = a.astype(y_ref.dtype)
            ua = _gelu(z_ref[0, rows, 3 * W_BR:4 * W_BR])
            vh, _ = _ln_stats(_gelu(z_ref[0, rows, 4 * W_BR:5 * W_BR]))
            vl = vh * vec_ref[3:4, :] + vec_ref[4:5, :]
            sg = _spatial(ws_ref, keep, _mm(vl)) + bsf_ref[...]
            g = ua * sg * _silu(z_ref[0, rows, 5 * W_BR:6 * W_BR])
            y_ref[0, rows, W_BR:2 * W_BR] = g.astype(y_ref.dtype)
            return carry

        lax.fori_loop(0, ts // GBLK, block, 0)

    full = lambda shape: pl.BlockSpec(shape, lambda b, s: (0,) * len(shape))
    return pl.pallas_call(
        body, name="even_fwd", grid=(bsz, s_len // ts),
        in_specs=[pl.BlockSpec((1, ts, N_COLS), lambda b, s: (b, s, 0)), _halo_specs(ts, s_len, 2 * W_BR),
                  full((32, W_BR)), full((8, W_BR)), full((8, GBLK, GBLK)), full((GBLK, W_BR))],
        out_specs=pl.BlockSpec((1, ts, 2 * W_BR), lambda b, s: (b, s, 0)),
        out_shape=jax.ShapeDtypeStruct((bsz, s_len, 2 * W_BR), MM_DTYPE),
        scratch_shapes=[pltpu.VMEM((ext_rows + 8, W_BR), F32), pltpu.VMEM((7, ext_rows, W_BR), F32),
                        pltpu.VMEM((ts, W_BR), F32)],
        compiler_params=_cparams(("parallel", "parallel")),
    )(z3, z3, cw, vec, ws, bsf)


def _even_bwd(z3, dy3, cw, vec, ws, wst, bsf):
    bsz, s_len, _ = z3.shape
    ts = min(SEQ_TILE, s_len)
    n_s = s_len // ts
    ext_rows = ts + 2 * HALO
    a_rows = ts + HALO

    def body(z_ref, zl_ref, zr_ref, dy_ref, dyr_ref, cw_ref, vec_ref, ws_ref, wst_ref, bsf_ref,
             dz_ref, dbin_ref, dcw_ref, dvec_ref, dws_ref, dbsf_ref,
             ext_ref, sh_ref, a1_ref, ag_ref, dya_ref, d_ref, accw_ref):
        b = pl.program_id(0)
        s = pl.program_id(1)

        @pl.when((b == 0) & (s == 0))
        def _():
            dbin_ref[...] = jnp.zeros_like(dbin_ref)
            dcw_ref[...] = jnp.zeros_like(dcw_ref)
            dvec_ref[...] = jnp.zeros_like(dvec_ref)
            dws_ref[...] = jnp.zeros_like(dws_ref)
            dbsf_ref[...] = jnp.zeros_like(dbsf_ref)

        has_right = s < n_s - 1
        hl = zl_ref[0]
        hr = zr_ref[0]
        ext_ref[0:HALO, :] = jnp.where(s > 0, hl[:, 0:W_BR] * _sigmoid(hl[:, W_BR:2 * W_BR]), 0.0)
        ext_ref[HALO:HALO + ts, :] = z_ref[0, :, 0:W_BR] * _sigmoid(z_ref[0, :, W_BR:2 * W_BR])
        ext_ref[HALO + ts:ext_rows, :] = hr[:, 0:W_BR] * _sigmoid(hr[:, W_BR:2 * W_BR])
        ext_ref[ext_rows:ext_rows + 8, :] = jnp.zeros((8, W_BR), F32)
        ag_ref[0:ts, :] = z_ref[0, :, 2 * W_BR:3 * W_BR]
        ag_ref[ts:a_rows, :] = hr[:, 2 * W_BR:3 * W_BR]
        dya_ref[0:ts, :] = dy_ref[0, :, 0:W_BR]
        dya_ref[ts:a_rows, :] = dyr_ref[0][:, 0:W_BR]
        _build_shifts(ext_ref, sh_ref, ext_rows)

        def conv_chunk(ci, carry):
            base = pl.multiple_of(ci * CONV_CHUNK, CONV_CHUNK)
            acc = jnp.zeros((CONV_CHUNK, W_BR), F32) + vec_ref[0:1, :]
            for k in range(KA):
                q, r = divmod(2 + k, 8)
                acc = acc + _shifted(ext_ref, sh_ref, r, base + 8 * q, CONV_CHUNK) * cw_ref[k:k + 1, :]
            a1_ref[pl.ds(base, CONV_CHUNK), :] = acc
            return carry

        lax.fori_loop(0, a_rows // CONV_CHUNK, conv_chunk, 0)

        def a_chunk(base, n, main):
            rows = pl.ds(base, n)
            xh, rs = _ln_stats(a1_ref[rows, :])
            ln = xh * vec_ref[1:2, :] + vec_ref[2:3, :]
            sl, dsl = _silu_grad(ln)
            sgt, dsgt = _silu_grad(ag_ref[rows, :])
            dya = dya_ref[rows, :]
            dln = dya * sgt * dsl
            da1 = _ln_bwd(dln * vec_ref[1:2, :], xh, rs)
            if main:
                d_ref[rows, :] = da1
                dag = dya * sl * dsgt
                dz_ref[0, rows, 2 * W_BR:3 * W_BR] = dag.astype(dz_ref.dtype)
                dbin_ref[0:1, 2 * W_BR:3 * W_BR] += _rowsum(dag)
                dvec_ref[0:1, :] += _rowsum(da1)
                dvec_ref[1:2, :] += _rowsum(dln * xh)
                dvec_ref[2:3, :] += _rowsum(dln)
            else:
                d_ref[rows, :] = jnp.where(has_right, da1, 0.0)

        def a_main(ci, carry):
            a_chunk(pl.multiple_of(ci * GBLK, GBLK), GBLK, True)
            return carry

        lax.fori_loop(0, ts // GBLK, a_main, 0)
        a_chunk(ts, HALO, False)
        d_ref[a_rows:a_rows + 8, :] = jnp.zeros((8, W_BR), F32)

        accw_ref[...] = jnp.zeros_like(accw_ref)

        def dw_chunk(ci, carry):
            base = pl.multiple_of(ci * CONV_CHUNK, CONV_CHUNK)
            d = d_ref[pl.ds(base, CONV_CHUNK), :]
            for k in range(KA):
                q, r = divmod(2 + k, 8)
                prod = d * _shifted(ext_ref, sh_ref, r, base + 8 * q, CONV_CHUNK)
                accw_ref[k] += jnp.sum(prod.reshape(CONV_CHUNK // 8, 8, W_BR), axis=0)
            return carry

        lax.fori_loop(0, ts // CONV_CHUNK, dw_chunk, 0)
        dcw_ref[...] += jnp.sum(accw_ref[...], axis=1)

        _build_shifts(d_ref, sh_ref, a_rows)

        def dx_chunk(ci, carry):
            base = pl.multiple_of(ci * CONV_CHUNK, CONV_CHUNK)
            rows = pl.ds(base, CONV_CHUNK)
            acc = jnp.zeros((CONV_CHUNK, W_BR), F32)
            for m in range(KA):
                q, r = divmod(m, 8)
                acc = acc + _shifted(d_ref, sh_ref, r, base + 8 * q, CONV_CHUNK) * cw_ref[KA - 1 - m:KA - m, :]
            aval = z_ref[0, rows, 0:W_BR]
            sg = _sigmoid(z_ref[0, rows, W_BR:2 * W_BR])
            dval = acc * sg
            dglu = acc * aval * sg * (1.0 - sg)
            dz_ref[0, rows, 0:W_BR] = dval.astype(dz_ref.dtype)
            dz_ref[0, rows, W_BR:2 * W_BR] = dglu.astype(dz_ref.dtype)
            dbin_ref[0:1, 0:W_BR] += _rowsum(dval)
            dbin_ref[0:1, W_BR:2 * W_BR] += _rowsum(dglu)
            return carry

        lax.fori_loop(0, ts // CONV_CHUNK, dx_chunk, 0)

        keep, keep_t = _tril_masks()
        lane = lax.broadcasted_iota(jnp.int32, (GBLK, GBLK), 1)

        def block(bi, carry):
            rows = pl.ds(pl.multiple_of(bi * GBLK, GBLK), GBLK)
            ua, dua = _gelu_grad(z_ref[0, rows, 3 * W_BR:4 * W_BR])
            va, dva = _gelu_grad(z_ref[0, rows, 4 * W_BR:5 * W_BR])
            sgt, dsgt = _silu_grad(z_ref[0, rows, 5 * W_BR:6 * W_BR])
            vh, rs = _ln_stats(va)
            vlb = _mm(vh * vec_ref[3:4, :] + vec_ref[4:5, :])
            sg = _spatial(ws_ref, keep, vlb) + bsf_ref[...]
            dyg = dy_ref[0, rows, W_BR:2 * W_BR]
            du = dyg * sg * sgt * dua
            dsg = dyg * ua * sgt
            dgg = dyg * ua * sg * dsgt
            dvl = _spatial(wst_ref, keep_t, _mm(dsg))
            for p in range(4):
                dsp = dsg[:, p * GBLK:(p + 1) * GBLK]
                vlp = vlb[:, p * GBLK:(p + 1) * GBLK]
                dws_ref[2 * p] += jnp.where(keep, _dot_nt(jnp.where(lane < 64, dsp, 0.0), vlp), 0.0)
                dws_ref[2 * p + 1] += jnp.where(keep, _dot_nt(jnp.where(lane >= 64, dsp, 0.0), vlp), 0.0)
            dbsf_ref[...] += dsg
            dvec_ref[3:4, :] += _rowsum(dvl * vh)
            dvec_ref[4:5, :] += _rowsum(dvl)
            dv = _ln_bwd(dvl * vec_ref[3:4, :], vh, rs) * dva
            dz_ref[0, rows, 3 * W_BR:4 * W_BR] = du.astype(dz_ref.dtype)
            dz_ref[0, rows, 4 * W_BR:5 * W_BR] = dv.astype(dz_ref.dtype)
            dz_ref[0, rows, 5 * W_BR:6 * W_BR] = dgg.astype(dz_ref.dtype)
            dbin_ref[0:1, 3 * W_BR:4 * W_BR] += _rowsum(du)
            dbin_ref[0:1, 4 * W_BR:5 * W_BR] += _rowsum(dv)
            dbin_ref[0:1, 5 * W_BR:6 * W_BR] += _rowsum(dgg)
            return carry

        lax.fori_loop(0, ts // GBLK, block, 0)

    full = lambda shape: pl.BlockSpec(shape, lambda b, s: (0,) * len(shape))
    acc_shapes = [(1, N_COLS), (32, W_BR), (8, W_BR), (8, GBLK, GBLK), (GBLK, W_BR)]
    return pl.pallas_call(
        body, name="even_bwd", grid=(bsz, n_s),
        in_specs=[pl.BlockSpec((1, ts, N_COLS), lambda b, s: (b, s, 0)),
                  _halo_specs(ts, s_len, N_COLS, True), _halo_specs(ts, s_len, N_COLS, False),
                  pl.BlockSpec((1, ts, 2 * W_BR), lambda b, s: (b, s, 0)), _halo_specs(ts, s_len, 2 * W_BR, False),
                  full((32, W_BR)), full((8, W_BR)), full((8, GBLK, GBLK)), full((8, GBLK, GBLK)), full((GBLK, W_BR))],
        out_specs=[pl.BlockSpec((1, ts, N_COLS), lambda b, s: (b, s, 0))] + [full(sh) for sh in acc_shapes],
        out_shape=[jax.ShapeDtypeStruct((bsz, s_len, N_COLS), MM_DTYPE)] + [jax.ShapeDtypeStruct(sh, F32) for sh in acc_shapes],
        scratch_shapes=[pltpu.VMEM((ext_rows + 8, W_BR), F32), pltpu.VMEM((7, ext_rows, W_BR), F32),
                        pltpu.VMEM((a_rows, W_BR), F32), pltpu.VMEM((a_rows, W_BR), F32), pltpu.VMEM((a_rows, W_BR), F32),
                        pltpu.VMEM((a_rows + 8, W_BR), F32), pltpu.VMEM((32, 8, W_BR), F32)],
        compiler_params=_cparams(("arbitrary", "arbitrary")),
    )(z3, z3, z3, dy3, dy3, cw, vec, ws, wst, bsf)


def _pool_stages(e_refs, rows):
    e0, e1, e2, e3, e4 = e_refs
    e1[8:rows, :] = e0[8:rows, :] + e0[7:rows - 1, :]
    e2[16:rows, GBLK:] = e1[16:rows, GBLK:] + e1[14:rows - 2, GBLK:]
    e3[24:rows, 2 * GBLK:] = e2[24:rows, 2 * GBLK:] + e2[20:rows - 4, 2 * GBLK:]
    e4[32:rows, 3 * GBLK:] = e3[32:rows, 3 * GBLK:] + e3[24:rows - 8, 3 * GBLK:]


def _pool_counts(start, n):
    pos = (start + 1 + lax.broadcasted_iota(jnp.int32, (n, 1), 0)).astype(F32)
    return [jnp.minimum(pos, float(w)) for w in POOL_WINDOWS]


def _pooled_into(e_refs, pooled_ref, s, ts):
    cnt = _pool_counts(s * ts, ts)
    for g in range(4):
        cs = slice(g * GBLK, (g + 1) * GBLK)
        pooled_ref[:, cs] = e_refs[g + 1][HALO:HALO + ts, cs] / cnt[g] - e_refs[0][HALO:HALO + ts, cs]


def _odd_fwd(z3, wp, vec):
    bsz, s_len, _ = z3.shape
    ts = min(SEQ_TILE, s_len)
    ext_rows = ts + HALO

    def body(z_ref, zl_ref, wp_ref, vec_ref, y_ref, e0, e1, e2, e3, e4, pooled_ref, dext_ref, ec_ref):
        s = pl.program_id(1)
        hl = zl_ref[0]
        e0[0:HALO, :] = jnp.where(s > 0, hl[:, 0:W_BR], 0.0)
        e0[HALO:ext_rows, :] = z_ref[0, :, 0:W_BR]
        _pool_stages((e0, e1, e2, e3, e4), ext_rows)
        _pooled_into((e0, e1, e2, e3, e4), pooled_ref, s, ts)
        dext_ref[0:HALO, :] = jnp.where(s > 0, hl[:, 2 * W_BR:3 * W_BR] * hl[:, 4 * W_BR:5 * W_BR], 0.0)
        dext_ref[HALO:ext_rows, :] = z_ref[0, :, 2 * W_BR:3 * W_BR] * z_ref[0, :, 4 * W_BR:5 * W_BR]
        ec_ref[...] = (vec_ref[1:2, :] * dext_ref[HALO - 2:HALO - 2 + ts, :] + vec_ref[2:3, :] * dext_ref[HALO - 1:HALO - 1 + ts, :]
                       + vec_ref[3:4, :] * dext_ref[HALO:HALO + ts, :])

        def block(bi, carry):
            rows = pl.ds(pl.multiple_of(bi * GBLK, GBLK), GBLK)
            pb = _mm(pooled_ref[rows, :])
            cpre = jnp.concatenate([jnp.dot(pb[:, g * GBLK:(g + 1) * GBLK], wp_ref[g], preferred_element_type=F32)
                                    for g in range(4)], axis=1)
            c = cpre * vec_ref[0:1, :] * _silu(z_ref[0, rows, W_BR:2 * W_BR])
            d = z_ref[0, rows, 3 * W_BR:4 * W_BR] * ec_ref[rows, :] * _silu(z_ref[0, rows, 5 * W_BR:6 * W_BR])
            y_ref[0, rows, 0:W_BR] = c.astype(y_ref.dtype)
            y_ref[0, rows, W_BR:2 * W_BR] = d.astype(y_ref.dtype)
            return carry

        lax.fori_loop(0, ts // GBLK, block, 0)

    full = lambda shape: pl.BlockSpec(shape, lambda b, s: (0,) * len(shape))
    ebuf = pltpu.VMEM((ext_rows, W_BR), F32)
    return pl.pallas_call(
        body, name="odd_fwd", grid=(bsz, s_len // ts),
        in_specs=[pl.BlockSpec((1, ts, N_COLS), lambda b, s: (b, s, 0)), _halo_specs(ts, s_len, N_COLS),
                  full((4, GBLK, GBLK)), full((8, W_BR))],
        out_specs=pl.BlockSpec((1, ts, 2 * W_BR), lambda b, s: (b, s, 0)),
        out_shape=jax.ShapeDtypeStruct((bsz, s_len, 2 * W_BR), MM_DTYPE),
        scratch_shapes=[ebuf, ebuf, ebuf, ebuf, ebuf, pltpu.VMEM((ts, W_BR), F32), ebuf, pltpu.VMEM((ts, W_BR), F32)],
        compiler_params=_cparams(("parallel", "parallel")),
    )(z3, z3, wp, vec)


def _odd_bwd(z3, dy3, wp, vec):
    bsz, s_len, _ = z3.shape
    ts = min(SEQ_TILE, s_len)
    n_s = s_len // ts
    ext_rows = ts + HALO

    def body(z_ref, zl_ref, zr_ref, dy_ref, dyr_ref, wp_ref, vec_ref,
             dz_ref, dbin_ref, dwp_ref, dvec_ref,
             e0, e1, e2, e3, e4, pooled_ref, dext_ref, ec_ref, q_ref, dp_ref, de_ref, f1, f2, f3, f4):
        b = pl.program_id(0)
        s = pl.program_id(1)

        @pl.when((b == 0) & (s == 0))
        def _():
            dbin_ref[...] = jnp.zeros_like(dbin_ref)
            dwp_ref[...] = jnp.zeros_like(dwp_ref)
            dvec_ref[...] = jnp.zeros_like(dvec_ref)

        has_right = s < n_s - 1
        hl = zl_ref[0]
        e0[0:HALO, :] = jnp.where(s > 0, hl[:, 0:W_BR], 0.0)
        e0[HALO:ext_rows, :] = z_ref[0, :, 0:W_BR]
        _pool_stages((e0, e1, e2, e3, e4), ext_rows)
        _pooled_into((e0, e1, e2, e3, e4), pooled_ref, s, ts)
        dext_ref[0:HALO, :] = jnp.where(s > 0, hl[:, 2 * W_BR:3 * W_BR] * hl[:, 4 * W_BR:5 * W_BR], 0.0)
        dext_ref[HALO:ext_rows, :] = z_ref[0, :, 2 * W_BR:3 * W_BR] * z_ref[0, :, 4 * W_BR:5 * W_BR]
        ec_ref[...] = (vec_ref[1:2, :] * dext_ref[HALO - 2:HALO - 2 + ts, :] + vec_ref[2:3, :] * dext_ref[HALO - 1:HALO - 1 + ts, :]
                       + vec_ref[3:4, :] * dext_ref[HALO:HALO + ts, :])

        def grads(zc_gate, zd_b, zd_gate, dyc, dyd, rows_out, n, start, valid):
            sgt = _silu(zc_gate)
            dcpre = dyc * vec_ref[0:1, :] * sgt
            db = _mm(dcpre)
            dpool = jnp.concatenate([_dot_nt(db[:, g * GBLK:(g + 1) * GBLK], wp_ref[g]) for g in range(4)], axis=1)
            cnt = _pool_counts(start, n)
            q = jnp.concatenate([dpool[:, g * GBLK:(g + 1) * GBLK] / cnt[g] for g in range(4)], axis=1)
            de = dyd * zd_b * _silu(zd_gate)
            if valid is not None:
                q = jnp.where(valid, q, 0.0)
                de = jnp.where(valid, de, 0.0)
            q_ref[rows_out, :] = q
            dp_ref[rows_out, :] = dpool
            de_ref[rows_out, :] = de
            return dcpre

        def block(bi, carry):
            base = pl.multiple_of(bi * GBLK, GBLK)
            rows = pl.ds(base, GBLK)
            cg = z_ref[0, rows, W_BR:2 * W_BR]
            dyc = dy_ref[0, rows, 0:W_BR]
            dyd = dy_ref[0, rows, W_BR:2 * W_BR]
            d_b = z_ref[0, rows, 3 * W_BR:4 * W_BR]
            d_gate = z_ref[0, rows, 5 * W_BR:6 * W_BR]
            dcpre = grads(cg, d_b, d_gate, dyc, dyd, rows, GBLK, s * ts + base, None)
            pb = _mm(pooled_ref[rows, :])
            dcb = _mm(dcpre)
            cpre = jnp.concatenate([jnp.dot(pb[:, g * GBLK:(g + 1) * GBLK], wp_ref[g], preferred_element_type=F32)
                                    for g in range(4)], axis=1)
            for g in range(4):
                cs = slice(g * GBLK, (g + 1) * GBLK)
                dwp_ref[g] += _dot_tn(pb[:, cs], dcb[:, cs])
            sgt, dsgt = _silu_grad(cg)
            dvec_ref[0:1, :] += _rowsum(dyc * cpre * sgt)
            dcg = dyc * cpre * vec_ref[0:1, :] * dsgt
            sdt, dsdt = _silu_grad(d_gate)
            ec = ec_ref[rows, :]
            ddb = dyd * ec * sdt
            ddg = dyd * d_b * ec * dsdt
            dz_ref[0, rows, W_BR:2 * W_BR] = dcg.astype(dz_ref.dtype)
            dz_ref[0, rows, 3 * W_BR:4 * W_BR] = ddb.astype(dz_ref.dtype)
            dz_ref[0, rows, 5 * W_BR:6 * W_BR] = ddg.astype(dz_ref.dtype)
            dbin_ref[0:1, W_BR:2 * W_BR] += _rowsum(dcg)
            dbin_ref[0:1, 3 * W_BR:4 * W_BR] += _rowsum(ddb)
            dbin_ref[0:1, 5 * W_BR:6 * W_BR] += _rowsum(ddg)
            return carry

        lax.fori_loop(0, ts // GBLK, block, 0)
        hr = zr_ref[0]
        dyr = dyr_ref[0]
        grads(hr[:, W_BR:2 * W_BR], hr[:, 3 * W_BR:4 * W_BR], hr[:, 5 * W_BR:6 * W_BR], dyr[:, 0:W_BR], dyr[:, W_BR:2 * W_BR],
              slice(ts, ext_rows), HALO, (s + 1) * ts, has_right)

        r1, r2, r3 = ts + 24, ts + 16, ts + 8
        f1[0:r1, :] = q_ref[0:r1, :] + q_ref[1:r1 + 1, :]
        f2[0:r2, GBLK:] = f1[0:r2, GBLK:] + f1[2:r2 + 2, GBLK:]
        f3[0:r3, 2 * GBLK:] = f2[0:r3, 2 * GBLK:] + f2[4:r3 + 4, 2 * GBLK:]
        f4[0:ts, 3 * GBLK:] = f3[0:ts, 3 * GBLK:] + f3[8:ts + 8, 3 * GBLK:]
        for g, f in enumerate((f1, f2, f3, f4)):
            cs = slice(g * GBLK, (g + 1) * GBLK)
            dvg = f[0:ts, cs] - dp_ref[0:ts, cs]
            dz_ref[0, :, cs] = dvg.astype(dz_ref.dtype)
            dbin_ref[0:1, cs] += _rowsum(dvg)

        ddc = (vec_ref[1:2, :] * de_ref[2:ts + 2, :] + vec_ref[2:3, :] * de_ref[1:ts + 1, :] + vec_ref[3:4, :] * de_ref[0:ts, :])
        d_h = z_ref[0, :, 2 * W_BR:3 * W_BR]
        d_c = z_ref[0, :, 4 * W_BR:5 * W_BR]
        ddh = ddc * d_c
        ddcc = ddc * d_h
        dz_ref[0, :, 2 * W_BR:3 * W_BR] = ddh.astype(dz_ref.dtype)
        dz_ref[0, :, 4 * W_BR:5 * W_BR] = ddcc.astype(dz_ref.dtype)
        dbin_ref[0:1, 2 * W_BR:3 * W_BR] += _rowsum(ddh)
        dbin_ref[0:1, 4 * W_BR:5 * W_BR] += _rowsum(ddcc)
        de = de_ref[0:ts, :]
        for k in range(3):
            dvec_ref[1 + k:2 + k, :] += _rowsum(de * dext_ref[HALO - 2 + k:HALO - 2 + k + ts, :])

    full = lambda shape: pl.BlockSpec(shape, lambda b, s: (0,) * len(shape))
    acc_shapes = [(1, N_COLS), (4, GBLK, GBLK), (8, W_BR)]
    ebuf = pltpu.VMEM((ext_rows, W_BR), F32)
    tbuf = pltpu.VMEM((ts, W_BR), F32)
    return pl.pallas_call(
        body, name="odd_bwd", grid=(bsz, n_s),
        in_specs=[pl.BlockSpec((1, ts, N_COLS), lambda b, s: (b, s, 0)),
                  _halo_specs(ts, s_len, N_COLS, True), _halo_specs(ts, s_len, N_COLS, False),
                  pl.BlockSpec((1, ts, 2 * W_BR), lambda b, s: (b, s, 0)), _halo_specs(ts, s_len, 2 * W_BR, False),
                  full((4, GBLK, GBLK)), full((8, W_BR))],
        out_specs=[pl.BlockSpec((1, ts, N_COLS), lambda b, s: (b, s, 0))] + [full(sh) for sh in acc_shapes],
        out_shape=[jax.ShapeDtypeStruct((bsz, s_len, N_COLS), MM_DTYPE)] + [jax.ShapeDtypeStruct(sh, F32) for sh in acc_shapes],
        scratch_shapes=[ebuf, ebuf, ebuf, ebuf, ebuf, tbuf, ebuf, tbuf, ebuf, ebuf, ebuf, ebuf, ebuf, ebuf, tbuf],
        compiler_params=_cparams(("arbitrary", "arbitrary")),
    )(z3, z3, z3, dy3, dy3, wp, vec)


def _post_fwd(y2, x2, p2, w_out, wg, wple, vec):
    t, d = x2.shape
    tm = min(ROW_TILE, t)

    def body(y_ref, x_ref, p_ref, wo_ref, wg_ref, wp_ref, vec_ref, xn_ref, r_ref, gate_ref):
        r = ALPHA * x_ref[...] + jnp.dot(y_ref[...], wo_ref[...], preferred_element_type=F32) + vec_ref[0:1, :]
        r_ref[...] = r
        xh, _ = _ln_stats(r)
        h = xh * vec_ref[1:2, :] + vec_ref[2:3, :]
        gate = _sigmoid(_dot(h, wg_ref[...]) + vec_ref[3:4, :])
        gate_ref[...] = gate
        xn_ref[...] = h + gate * _dot(p_ref[...], wp_ref[...])

    row = lambda c: pl.BlockSpec((tm, c), lambda i: (i, 0))
    full = lambda shape: pl.BlockSpec(shape, lambda i: (0,) * len(shape))
    return pl.pallas_call(
        body, name="post_fwd", grid=(t // tm,),
        in_specs=[row(d), row(d), row(D_PLE), full((d, d)), full((d, d)), full((D_PLE, d)), full((8, d))],
        out_specs=[row(d), row(d), row(d)],
        out_shape=[jax.ShapeDtypeStruct((t, d), F32)] * 3,
        compiler_params=_cparams(("parallel",)),
    )(y2, x2, p2, w_out, wg, wple, vec)


def _post_bwd(dxn, r2, gate2, p2, y2, w_out, wg, wple, vec):
    t, d = r2.shape
    tm = min(BWD_ROW_TILE, t)

    def body(dxn_ref, r_ref, gate_ref, p_ref, y_ref, wo_ref, wg_ref, wp_ref, vec_ref,
             dxr_ref, dy_ref, dwo_ref, dwg_ref, dwp_ref, dvec_ref):
        @pl.when(pl.program_id(0) == 0)
        def _():
            dwo_ref[...] = jnp.zeros_like(dwo_ref)
            dwg_ref[...] = jnp.zeros_like(dwg_ref)
            dwp_ref[...] = jnp.zeros_like(dwp_ref)
            dvec_ref[...] = jnp.zeros_like(dvec_ref)

        dxn = dxn_ref[...]
        gate = gate_ref[...]
        xh, rs = _ln_stats(r_ref[...])
        hb = _mm(xh * vec_ref[1:2, :] + vec_ref[2:3, :])
        pb = _mm(p_ref[...])
        pe = jnp.dot(pb, wp_ref[...], preferred_element_type=F32)
        dpre = dxn * pe * gate * (1.0 - gate)
        dpb = _mm(dpre)
        dh = dxn + _dot_nt(dpb, wg_ref[...])
        dwg_ref[...] += _dot_tn(hb, dpb)
        dwp_ref[...] += _dot_tn(pb, dxn * gate)
        dr = _ln_bwd(dh * vec_ref[1:2, :], xh, rs)
        drb = _mm(dr)
        dxr_ref[...] = ALPHA * dr
        dy_ref[...] = _dot_nt(drb, wo_ref[...])
        dwo_ref[...] += _dot_tn(y_ref[...], drb)
        dvec_ref[0:1, :] += _rowsum(dr)
        dvec_ref[1:2, :] += _rowsum(dh * xh)
        dvec_ref[2:3, :] += _rowsum(dh)
        dvec_ref[3:4, :] += _rowsum(dpre)

    row = lambda c: pl.BlockSpec((tm, c), lambda i: (i, 0))
    full = lambda shape: pl.BlockSpec(shape, lambda i: (0,) * len(shape))
    acc_shapes = [(d, d), (d, d), (D_PLE, d), (8, d)]
    return pl.pallas_call(
        body, name="post_bwd", grid=(t // tm,),
        in_specs=[row(d), row(d), row(d), row(D_PLE), row(d), full((d, d)), full((d, d)), full((D_PLE, d)), full((8, d))],
        out_specs=[row(d), row(d)] + [full(sh) for sh in acc_shapes],
        out_shape=[jax.ShapeDtypeStruct((t, d), F32)] * 2 + [jax.ShapeDtypeStruct(sh, F32) for sh in acc_shapes],
        compiler_params=_cparams(("arbitrary",)),
    )(dxn, r2, gate2, p2, y2, w_out, wg, wple, vec)


def _loss_head(xf, tgt):
    t, d = xf.shape
    tm = min(ROW_TILE, t)

    def body(x_ref, t_ref, dx_ref, sq_ref):
        @pl.when(pl.program_id(0) == 0)
        def _():
            sq_ref[...] = jnp.zeros_like(sq_ref)
        e = x_ref[...] - t_ref[...]
        dx_ref[...] = e / float(d)
        sq_ref[...] += _rowsum(e * e)

    row = pl.BlockSpec((tm, d), lambda i: (i, 0))
    return pl.pallas_call(
        body, name="loss_head", grid=(t // tm,), in_specs=[row, row],
        out_specs=[row, pl.BlockSpec((1, d), lambda i: (0, 0))],
        out_shape=[jax.ShapeDtypeStruct((t, d), F32), jax.ShapeDtypeStruct((1, d), F32)],
        compiler_params=_cparams(("arbitrary",)),
    )(xf, tgt)


def _place():
    x, y, c = lax.axis_index("x"), lax.axis_index("y"), lax.axis_index("c")
    chips = [(1 - x, y), (x, 1 - y), (1 - x, 1 - y)]
    return x, y, c, chips


def _shard_of(ref, ax, k, width, lo=None, ln=None):
    idx = [slice(None)] * 3
    if lo is not None:
        idx[0] = pl.ds(lo, ln)
    idx[ax] = pl.ds(k * width, width)
    return ref.at[tuple(idx)]


def _remote(src, dst, ssem, rsem, dev):
    return pltpu.make_async_remote_copy(src_ref=src, dst_ref=dst, send_sem=ssem, recv_sem=rsem, device_id=dev, device_id_type=MESH)


def _place_shard(w, ax, chip):
    l_dim, a_dim, b_dim = w.shape
    tr = min(256, a_dim)
    per = a_dim // tr
    shape = list(w.shape)
    shape[ax] *= 4
    if ax == 2:
        out_spec = pl.BlockSpec((1, tr, b_dim), lambda l, i, k: (l, i, k[0]))
    else:
        out_spec = pl.BlockSpec((1, tr, b_dim), lambda l, i, k: (l, k[0] * per + i, 0))

    def body(k_ref, w_ref, o_ref):
        o_ref[...] = w_ref[...].astype(o_ref.dtype)

    return pl.pallas_call(
        body, name="place_shard",
        grid_spec=pltpu.PrefetchScalarGridSpec(
            num_scalar_prefetch=1, grid=(l_dim, per),
            in_specs=[pl.BlockSpec((1, tr, b_dim), lambda l, i, k: (l, i, 0))], out_specs=out_spec),
        out_shape=jax.ShapeDtypeStruct(tuple(shape), MM_DTYPE),
        compiler_params=_cparams(("parallel", "parallel")),
    )(chip, w)


def _gather_weights(fulls, axes, sv):
    n = len(fulls)

    def body(*refs):
        in_refs, sv_ref = refs[:n], refs[n]
        fu_refs, sva_ref = refs[n + 1:2 * n + 1], refs[2 * n + 1]
        ssem, rsem, fssem, frsem, lsem = refs[2 * n + 2:]
        x, y, c, chips = _place()
        j = 2 * x + y
        sends, fwds = [], []
        loc = pltpu.make_async_copy(sv_ref, sva_ref.at[j], lsem)
        loc.start()
        for a in range(n):
            lh = fu_refs[a].shape[0] // 2
            w = fu_refs[a].shape[axes[a]] // 4
            for q, (qx, qy) in enumerate(chips):
                cp = _remote(_shard_of(in_refs[a], axes[a], j, w, c * lh, lh), _shard_of(fu_refs[a], axes[a], j, w, c * lh, lh),
                             ssem.at[a * 3 + q], rsem.at[a * 3 + q], (qx, qy, c))
                cp.start()
                sends.append(cp)
        for q, (qx, qy) in enumerate(chips):
            cp = _remote(sv_ref, sva_ref.at[j], ssem.at[n * 3 + q], rsem.at[n * 3 + q], (qx, qy, c))
            cp.start()
            sends.append(cp)
        for a in range(n):
            lh = fu_refs[a].shape[0] // 2
            w = fu_refs[a].shape[axes[a]] // 4
            for q, (qx, qy) in enumerate(chips):
                piece = _shard_of(fu_refs[a], axes[a], 2 * qx + qy, w, c * lh, lh)
                _remote(piece, piece, ssem.at[a * 3 + q], rsem.at[a * 3 + q], (x, y, c)).wait_recv()
                cp = _remote(piece, piece, fssem.at[a * 3 + q], frsem.at[a * 3 + q], (x, y, 1 - c))
                cp.start()
                fwds.append(cp)
        for q, (qx, qy) in enumerate(chips):
            _remote(sv_ref, sva_ref.at[2 * qx + qy], ssem.at[n * 3 + q], rsem.at[n * 3 + q], (x, y, c)).wait_recv()
        for a in range(n):
            lh = fu_refs[a].shape[0] // 2
            w = fu_refs[a].shape[axes[a]] // 4
            for q, (qx, qy) in enumerate(chips):
                piece = _shard_of(fu_refs[a], axes[a], 2 * qx + qy, w, (1 - c) * lh, lh)
                _remote(piece, piece, fssem.at[a * 3 + q], frsem.at[a * 3 + q], (x, y, c)).wait_recv()
        for cp in sends + fwds:
            cp.wait_send()
        loc.wait()

    nd = 3 * n + 3
    return pl.pallas_call(
        body, name="gather_weights",
        in_specs=[ANY] * (n + 1), out_specs=[ANY] * (n + 1),
        out_shape=[jax.ShapeDtypeStruct(f.shape, f.dtype) for f in fulls] + [jax.ShapeDtypeStruct((4,) + sv.shape, sv.dtype)],
        input_output_aliases={a: a for a in range(n)},
        scratch_shapes=[pltpu.SemaphoreType.DMA((nd,)), pltpu.SemaphoreType.DMA((nd,)),
                        pltpu.SemaphoreType.DMA((3 * n,)), pltpu.SemaphoreType.DMA((3 * n,)), pltpu.SemaphoreType.DMA],
        compiler_params=pltpu.CompilerParams(has_side_effects=True),
    )(*fulls, sv)


def _pair_exchange(grads, small):
    n = len(grads)
    outs = [jax.ShapeDtypeStruct((g.shape[0] // 2,) + g.shape[1:], g.dtype) for g in grads]
    outs.append(jax.ShapeDtypeStruct((small.shape[0] // 2, small.shape[1]), small.dtype))

    def body(*refs):
        g_refs, o_refs = refs[:n + 1], refs[n + 1:2 * n + 2]
        ssem, rsem = refs[2 * n + 2:]
        x, y, c, _ = _place()
        cps = []
        for a in range(n + 1):
            lh = g_refs[a].shape[0] // 2
            cp = _remote(g_refs[a].at[pl.ds((1 - c) * lh, lh)], o_refs[a], ssem.at[a], rsem.at[a], (x, y, 1 - c))
            cp.start()
            cps.append(cp)
        for cp in cps:
            cp.wait()

    return pl.pallas_call(
        body, name="pair_exchange", in_specs=[ANY] * (n + 1), out_specs=[ANY] * (n + 1), out_shape=outs,
        scratch_shapes=[pltpu.SemaphoreType.DMA((n + 1,)), pltpu.SemaphoreType.DMA((n + 1,))],
        compiler_params=pltpu.CompilerParams(has_side_effects=True),
    )(*grads, small)


def _chip_scatter(sums, axes, small):
    n = len(sums)
    outs = []
    for g, ax in zip(sums, axes):
        sh = list(g.shape)
        sh[ax] //= 4
        outs.append(jax.ShapeDtypeStruct((3,) + tuple(sh), g.dtype))
    rq = small.shape[0] // 4
    outs.append(jax.ShapeDtypeStruct((3, rq, small.shape[1]), small.dtype))

    def body(*refs):
        g_refs, o_refs = refs[:n + 1], refs[n + 1:2 * n + 2]
        ssem, rsem = refs[2 * n + 2:]
        x, y, c, chips = _place()
        cps = []
        for a in range(n + 1):
            for q, (qx, qy) in enumerate(chips):
                k = 2 * qx + qy
                if a < n:
                    src = _shard_of(g_refs[a], axes[a], k, g_refs[a].shape[axes[a]] // 4)
                else:
                    src = g_refs[a].at[pl.ds(k * rq, rq)]
                cp = _remote(src, o_refs[a].at[q], ssem.at[a * 3 + q], rsem.at[a * 3 + q], (qx, qy, c))
                cp.start()
                cps.append(cp)
        for cp in cps:
            cp.wait()

    return pl.pallas_call(
        body, name="chip_scatter", in_specs=[ANY] * (n + 1), out_specs=[ANY] * (n + 1), out_shape=outs,
        scratch_shapes=[pltpu.SemaphoreType.DMA((3 * n + 3,)), pltpu.SemaphoreType.DMA((3 * n + 3,))],
        compiler_params=pltpu.CompilerParams(has_side_effects=True),
    )(*sums, small)


def _final_exchange(reds, small):
    n = len(reds)
    flips = [(fx, fy, fc) for fx in (0, 1) for fy in (0, 1) for fc in (0, 1)][1:]

    def body(*refs):
        g_refs, o_refs = refs[:n + 1], refs[n + 1:2 * n + 2]
        ssem, rsem = refs[2 * n + 2:]
        x, y, c, _ = _place()
        cps = []
        for a in range(n):
            lh = g_refs[a].shape[0] // 2
            cp = _remote(g_refs[a].at[pl.ds(c * lh, lh)], o_refs[a].at[pl.ds(c * lh, lh)], ssem.at[a], rsem.at[a], (x, y, 1 - c))
            cp.start()
            cps.append(cp)
        mine = 4 * c + 2 * x + y
        for f, (fx, fy, fc) in enumerate(flips):
            cp = _remote(g_refs[n].at[mine], o_refs[n].at[mine], ssem.at[n + f], rsem.at[n + f], (x ^ fx, y ^ fy, c ^ fc))
            cp.start()
            cps.append(cp)
        for cp in cps:
            cp.wait()

    return pl.pallas_call(
        body, name="final_exchange", in_specs=[ANY] * (n + 1), out_specs=[ANY] * (n + 1),
        out_shape=[jax.ShapeDtypeStruct(g.shape, g.dtype) for g in reds] + [jax.ShapeDtypeStruct(small.shape, small.dtype)],
        input_output_aliases={a: a for a in range(n + 1)},
        scratch_shapes=[pltpu.SemaphoreType.DMA((n + 7,)), pltpu.SemaphoreType.DMA((n + 7,))],
        compiler_params=pltpu.CompilerParams(has_side_effects=True),
    )(*reds, small)


def _pair_sum(g, got, half):
    lh, a_dim, b_dim = got.shape
    tr = min(256, a_dim)

    def body(half_ref, g_ref, o_ref, s_ref):
        s_ref[...] = (g_ref[...] + o_ref[...]).astype(s_ref.dtype)

    return pl.pallas_call(
        body, name="pair_sum",
        grid_spec=pltpu.PrefetchScalarGridSpec(
            num_scalar_prefetch=1, grid=(lh, a_dim // tr),
            in_specs=[pl.BlockSpec((1, tr, b_dim), lambda l, i, h: (h[0] * lh + l, i, 0)),
                      pl.BlockSpec((1, tr, b_dim), lambda l, i, h: (l, i, 0))],
            out_specs=pl.BlockSpec((1, tr, b_dim), lambda l, i, h: (l, i, 0))),
        out_shape=jax.ShapeDtypeStruct(got.shape, WIRE_DTYPE),
        compiler_params=_cparams(("parallel", "parallel")),
    )(half, g, got)


def _chip_sum(own, got, ax, pos):
    _, lh, a_dim, b_dim = got.shape
    tr = min(256, a_dim)
    if ax == 2:
        own_spec = pl.BlockSpec((1, tr, b_dim), lambda l, i, k: (l, i, k[0]))
    else:
        per = a_dim // tr
        own_spec = pl.BlockSpec((1, tr, b_dim), lambda l, i, k: (l, k[0] * per + i, 0))

    def body(k_ref, own_ref, got_ref, s_ref):
        s_ref[...] = ((own_ref[...].astype(F32) + got_ref[0].astype(F32)) + got_ref[1].astype(F32)) + got_ref[2].astype(F32)

    return pl.pallas_call(
        body, name="chip_sum",
        grid_spec=pltpu.PrefetchScalarGridSpec(
            num_scalar_prefetch=1, grid=(lh, a_dim // tr),
            in_specs=[own_spec, pl.BlockSpec((3, 1, tr, b_dim), lambda l, i, k: (0, l, i, 0))],
            out_specs=pl.BlockSpec((1, tr, b_dim), lambda l, i, k: (k[1] * lh + l, i, 0))),
        out_shape=jax.ShapeDtypeStruct((2 * lh, a_dim, b_dim), F32),
        compiler_params=_cparams(("parallel", "parallel")),
    )(pos, own, got)


def _small_pair_sum(small, got, half):
    rh = small.shape[0] // 2

    def body(h_ref, g_ref, o_ref, s_ref):
        s_ref[...] = g_ref[...] + o_ref[...]

    return pl.pallas_call(
        body, name="small_pair_sum",
        grid_spec=pltpu.PrefetchScalarGridSpec(
            num_scalar_prefetch=1, grid=(1,),
            in_specs=[pl.BlockSpec((rh, 128), lambda i, h: (h[0], 0)), pl.BlockSpec((rh, 128), lambda i, h: (0, 0))],
            out_specs=pl.BlockSpec((rh, 128), lambda i, h: (0, 0))),
        out_shape=jax.ShapeDtypeStruct((rh, 128), F32),
    )(half, small, got)


def _small_chip_sum(s1, got, pos):
    rq = got.shape[1]

    def body(k_ref, own_ref, got_ref, s_ref):
        s_ref[0] = ((own_ref[...] + got_ref[0]) + got_ref[1]) + got_ref[2]

    return pl.pallas_call(
        body, name="small_chip_sum",
        grid_spec=pltpu.PrefetchScalarGridSpec(
            num_scalar_prefetch=1, grid=(1,),
            in_specs=[pl.BlockSpec((rq, 128), lambda i, k: (k[0], 0)), pl.BlockSpec((3, rq, 128), lambda i, k: (0, 0, 0))],
            out_specs=pl.BlockSpec((1, rq, 128), lambda i, k: (4 * k[1] + k[0], 0, 0))),
        out_shape=jax.ShapeDtypeStruct((8, rq, 128), F32),
    )(pos, s1, got)


def _adam_math(w, g, m, v):
    m = ADAM_B1 * m + (1.0 - ADAM_B1) * g
    v = ADAM_B2 * v + (1.0 - ADAM_B2) * (g * g)
    m_hat = m / (1.0 - ADAM_B1 ** ADAM_STEP)
    v_hat = v / (1.0 - ADAM_B2 ** ADAM_STEP)
    return -ADAM_LR * (m_hat / (jnp.sqrt(v_hat) + ADAM_EPS) + ADAM_WD * w), m, v


def _adamw_big(w, g, m, v):
    l_dim, a_dim, b_dim = w.shape
    tr = min(256, a_dim)

    def body(w_ref, g_ref, m_ref, v_ref, d_ref, nm_ref, nv_ref):
        d_ref[...], nm_ref[...], nv_ref[...] = _adam_math(w_ref[...], g_ref[...], m_ref[...], v_ref[...])

    blk = pl.BlockSpec((1, tr, b_dim), lambda l, i: (l, i, 0))
    return pl.pallas_call(
        body, name="adamw_big", grid=(l_dim, a_dim // tr), in_specs=[blk] * 4, out_specs=[blk] * 3,
        out_shape=[jax.ShapeDtypeStruct(w.shape, F32)] * 3,
        compiler_params=_cparams(("parallel", "parallel")),
    )(w, g, m, v)


def _adamw_small(ws, gs, ms, vs):
    n = len(ws)

    def body(*refs):
        for i in range(n):
            w_ref, g_ref, m_ref, v_ref = refs[i], refs[n + i], refs[2 * n + i], refs[3 * n + i]
            d_ref, nm_ref, nv_ref = refs[4 * n + i], refs[5 * n + i], refs[6 * n + i]
            d_ref[...], nm_ref[...], nv_ref[...] = _adam_math(w_ref[...], g_ref[...], m_ref[...], v_ref[...])

    shapes = [jax.ShapeDtypeStruct(w.shape, F32) for w in ws]
    outs = pl.pallas_call(body, name="adamw_small", out_shape=shapes * 3,
                          compiler_params=_cparams())(*ws, *gs, *ms, *vs)
    return outs[:n], outs[n:2 * n], outs[2 * n:]


def _pack(arrs, row_mult):
    parts = []
    for a in arrs:
        flat = a.reshape(-1)
        pad = (-flat.shape[0]) % 1024
        parts.append(jnp.pad(flat, (0, pad)).reshape(-1, 128))
    buf = jnp.concatenate(parts, axis=0)
    pad = (-buf.shape[0]) % row_mult
    return jnp.pad(buf, ((0, pad), (0, 0)))


def _unpack(buf, shapes):
    out, row = [], 0
    for sh in shapes:
        n = math.prod(sh)
        rows = -(-n // 1024) * 8
        out.append(buf[row:row + rows].reshape(-1)[:n].reshape(sh))
        row += rows
    return out


_NAMES = ['w_in_e', 'b_in_e', 'conv_a_w', 'conv_a_b', 'ln_a_g', 'ln_a_b', 'ln_v_g', 'ln_v_b', 'w_s', 'b_s', 'w_out_e', 'b_out_e',
          'w_in_o', 'b_in_o', 'w_pool', 'pool_scale', 'conv_d_w', 'w_out_o', 'b_out_o', 'ln_g', 'ln_b', 'w_ple', 'w_ple_gate',
          'b_ple_gate']
_BIG = ['w_in_e', 'w_out_e', 'w_in_o', 'w_out_o', 'w_ple', 'w_ple_gate']
_BIG_AXES = [2, 1, 2, 1, 2, 1]
_SMALL_SHARDED = ['conv_a_w', 'b_in_o', 'pool_scale', 'conv_d_w', 'b_out_o']


def kernel(x, p, w_in_e, b_in_e, conv_a_w, conv_a_b, ln_a_g, ln_a_b, ln_v_g, ln_v_b, w_s, b_s, w_out_e, b_out_e, w_in_o, b_in_o, w_pool, pool_scale, conv_d_w, w_out_o, b_out_o, ln_g, ln_b, w_ple, w_ple_gate, b_ple_gate, loss_target, m_w_in_e, m_b_in_e, m_conv_a_w, m_conv_a_b, m_ln_a_g, m_ln_a_b, m_ln_v_g, m_ln_v_b, m_w_s, m_b_s, m_w_out_e, m_b_out_e, m_w_in_o, m_b_in_o, m_w_pool, m_pool_scale, m_conv_d_w, m_w_out_o, m_b_out_o, m_ln_g, m_ln_b, m_w_ple, m_w_ple_gate, m_b_ple_gate, v_w_in_e, v_b_in_e, v_conv_a_w, v_conv_a_b, v_ln_a_g, v_ln_a_b, v_ln_v_g, v_ln_v_b, v_w_s, v_b_s, v_w_out_e, v_b_out_e, v_w_in_o, v_b_in_o, v_w_pool, v_pool_scale, v_conv_d_w, v_w_out_o, v_b_out_o, v_ln_g, v_ln_b, v_w_ple, v_w_ple_gate, v_b_ple_gate):
    args = locals()
    wts = {n: args[n] for n in _NAMES}
    mom = {n: args["m_" + n] for n in _NAMES}
    var = {n: args["v_" + n] for n in _NAMES}
    bsz, s_len, d = x.shape
    t = bsz * s_len
    cx, cy, cc = lax.axis_index("x"), lax.axis_index("y"), lax.axis_index("c")
    chip = (2 * cx + cy).astype(jnp.int32).reshape(1)
    half = cc.astype(jnp.int32).reshape(1)
    pos = jnp.concatenate([chip, half])

    sv = _pack([wts[n] for n in _SMALL_SHARDED], 8)
    *full_big, sv_all = _gather_weights([_place_shard(wts[n], ax, chip) for n, ax in zip(_BIG, _BIG_AXES)], _BIG_AXES, sv)
    fw = dict(zip(_BIG, full_big))
    small_parts = [_unpack(sv_all[k], [wts[n].shape for n in _SMALL_SHARDED]) for k in range(4)]
    for i, n in enumerate(_SMALL_SHARDED):
        fw[n] = jnp.concatenate([small_parts[k][i] for k in range(4)], axis=-1)
    for n in _NAMES:
        fw.setdefault(n, wts[n])

    def row8(rows, width):
        rows = [r.reshape(1, width) for r in rows]
        return jnp.concatenate(rows + [jnp.zeros((8 - len(rows), width), F32)], axis=0)

    x2 = x.reshape(t, d)
    saved = []
    for i in range(DEPTH):
        j = i // 2
        even = i % 2 == 0
        w_in, b_in, w_out, b_out = ((fw['w_in_e'], fw['b_in_e'], fw['w_out_e'], fw['b_out_e']) if even else
                                    (fw['w_in_o'], fw['b_in_o'], fw['w_out_o'], fw['b_out_o']))
        z = _in_proj(x2, w_in[j], b_in[j].reshape(1, N_COLS))
        z3 = z.reshape(bsz, s_len, N_COLS)
        if even:
            cw = jnp.concatenate([fw['conv_a_w'][j], jnp.zeros((1, W_BR), F32)], axis=0)
            mvec = row8([fw['conv_a_b'][j], fw['ln_a_g'][j], fw['ln_a_b'][j], fw['ln_v_g'][j], fw['ln_v_b'][j]], W_BR)
            bsf = jnp.repeat(fw['b_s'][j].T, W_BR // 8, axis=1)
            mix = (cw, mvec, fw['w_s'][j], jnp.swapaxes(fw['w_s'][j], 1, 2), bsf)
            y3 = _even_fwd(z3, cw, mvec, fw['w_s'][j], bsf)
        else:
            mvec = row8([fw['pool_scale'][j]] + [fw['conv_d_w'][j][k] for k in range(3)], W_BR)
            mix = (fw['w_pool'][j].astype(MM_DTYPE), mvec)
            y3 = _odd_fwd(z3, mix[0], mvec)
        pvec = row8([b_out[j], fw['ln_g'][i], fw['ln_b'][i], fw['b_ple_gate'][i]], d)
        post_w = (w_out[j], fw['w_ple_gate'][i], fw['w_ple'][i], pvec)
        p2 = p[i].reshape(t, D_PLE)
        y2 = y3.reshape(t, 2 * W_BR)
        xn, r2, gate2 = _post_fwd(y2, x2, p2, *post_w)
        saved.append((x2, z3, y2, r2, gate2, p2, w_in[j], mix, post_w))
        x2 = xn

    dx, sq = _loss_head(x2, loss_target.reshape(t, d))
    loss = lax.psum(0.5 * jnp.sum(sq) / d, ("x", "y", "c"))

    gr = {n: [None] * wts[n].shape[0] for n in _NAMES}
    for i in reversed(range(DEPTH)):
        j = i // 2
        even = i % 2 == 0
        x_in, z3, y2, r2, gate2, p2, w_in, mix, post_w = saved[i]
        dxr, dy, dwo, dwg, dwp, dpv = _post_bwd(dx, r2, gate2, p2, y2, *post_w)
        dy3 = dy.reshape(bsz, s_len, 2 * W_BR)
        sfx = '_e' if even else '_o'
        gr['w_out' + sfx][j], gr['b_out' + sfx][j] = dwo, dpv[0]
        gr['w_ple_gate'][i], gr['w_ple'][i] = dwg, dwp
        gr['ln_g'][i], gr['ln_b'][i], gr['b_ple_gate'][i] = dpv[1], dpv[2], dpv[3]
        if even:
            dz3, dbin, dcw, dmv, dws, dbsf = _even_bwd(z3, dy3, *mix)
            gr['conv_a_w'][j], gr['conv_a_b'][j] = dcw[:KA], dmv[0]
            gr['ln_a_g'][j], gr['ln_a_b'][j], gr['ln_v_g'][j], gr['ln_v_b'][j] = dmv[1], dmv[2], dmv[3], dmv[4]
            gr['w_s'][j] = dws
            gr['b_s'][j] = jnp.sum(dbsf.reshape(GBLK, 8, W_BR // 8), axis=2).T
        else:
            dz3, dbin, dwpool, dmv = _odd_bwd(z3, dy3, *mix)
            gr['w_pool'][j], gr['pool_scale'][j], gr['conv_d_w'][j] = dwpool, dmv[0], dmv[1:4]
        gr['b_in' + sfx][j] = dbin[0]
        dz2 = dz3.reshape(t, N_COLS)
        gr['w_in' + sfx][j] = _in_proj_bwd_dw(x_in, dz2)
        dx = _in_proj_bwd_dx(dxr, dz2, w_in)
    grad_x = dx.reshape(bsz, s_len, d)

    small_names = [n for n in _NAMES if n not in _BIG]
    g_big = [jnp.stack(gr[n]) for n in _BIG]
    g_small_full = [jnp.stack(gr[n]) for n in small_names]
    small = _pack(g_small_full, 64)
    *got_pair, got_small = _pair_exchange(g_big, small)
    sums = [_pair_sum(g, o, half) for g, o in zip(g_big, got_pair)]
    s1 = _small_pair_sum(small, got_small, half)
    *got_chips, got_small2 = _chip_scatter(sums, _BIG_AXES, s1)
    reds = [_chip_sum(sm, o, ax, pos) for sm, o, ax in zip(sums, got_chips, _BIG_AXES)]
    s2 = _small_chip_sum(s1, got_small2, pos)
    *g_shards, small_all = _final_exchange(reds, s2)
    g_small = _unpack(small_all.reshape(small.shape), [g.shape for g in g_small_full])
    grads = dict(zip(_BIG, g_shards))
    for n, g in zip(small_names, g_small):
        if n in _SMALL_SHARDED:
            w = wts[n].shape[-1]
            g = lax.dynamic_slice_in_dim(g, (2 * cx + cy) * w, w, axis=g.ndim - 1)
        grads[n] = g

    delta, new_m, new_v = {}, {}, {}
    for n in _BIG:
        delta[n], new_m[n], new_v[n] = _adamw_big(wts[n], grads[n], mom[n], var[n])
    ds, ms, vs = _adamw_small([wts[n] for n in small_names], [grads[n] for n in small_names],
                              [mom[n] for n in small_names], [var[n] for n in small_names])
    for n, a, b, c_ in zip(small_names, ds, ms, vs):
        delta[n], new_m[n], new_v[n] = a, b, c_

    return (loss, grad_x, *[grads[n] for n in _NAMES], *[delta[n] for n in _NAMES],
            *[new_m[n] for n in _NAMES], *[new_v[n] for n in _NAMES])
```

```python
import functools
import math

import jax
import jax.numpy as jnp
from jax import lax
from jax.experimental import pallas as pl
from jax.experimental.pallas import tpu as pltpu

F32 = jnp.float32
MM_DTYPE = jnp.bfloat16
WIRE_DTYPE = jnp.bfloat16
SEQ_TILE = 512
ROW_TILE = 512
BWD_ROW_TILE = 256
HALO = 32
CONV_CHUNK = 32
GBLK = 128
VMEM_LIMIT = 56 * 1024 * 1024

D_MODEL = 1024
W_BR = 512
N_COLS = 6 * W_BR
D_PLE = 256
KA = 31
DEPTH = 4
POOL_WINDOWS = (2, 4, 8, 16)
ALPHA = (2.0 * DEPTH) ** 0.25
LN_EPS = 1e-5
GELU_C = math.sqrt(2.0 / math.pi)

ADAM_LR, ADAM_B1, ADAM_B2, ADAM_EPS, ADAM_WD, ADAM_STEP = 0.001, 0.9, 0.999, 1e-08, 0.01, 10

MESH = pl.DeviceIdType.MESH
ANY = pl.BlockSpec(memory_space=pl.ANY)


def _cparams(sem=None):
    return pltpu.CompilerParams(dimension_semantics=sem, vmem_limit_bytes=VMEM_LIMIT)


def _sigmoid(x):
    return 1.0 / (1.0 + jnp.exp(-x))


def _silu(x):
    return x * _sigmoid(x)


def _silu_grad(x):
    s = _sigmoid(x)
    return x * s, s * (1.0 + x * (1.0 - s))


def _gelu(x):
    return 0.5 * x * (1.0 + jnp.tanh(GELU_C * (x + 0.044715 * (x * x * x))))


def _gelu_grad(x):
    x2 = x * x
    th = jnp.tanh(GELU_C * (x + 0.044715 * (x * x2)))
    return 0.5 * x * (1.0 + th), 0.5 * (1.0 + th) + 0.5 * x * (1.0 - th * th) * (GELU_C * (1.0 + 3.0 * 0.044715 * x2))


def _ln_stats(x):
    mu = jnp.mean(x, axis=-1, keepdims=True)
    d = x - mu
    var = jnp.mean(d * d, axis=-1, keepdims=True)
    rs = lax.rsqrt(var + LN_EPS)
    return d * rs, rs


def _ln_bwd(dxh, xh, rs):
    return rs * (dxh - jnp.mean(dxh, axis=-1, keepdims=True) - xh * jnp.mean(dxh * xh, axis=-1, keepdims=True))


def _mm(a):
    return a.astype(MM_DTYPE)


def _dot(a, b):
    return jnp.dot(_mm(a), _mm(b), preferred_element_type=F32)


def _dot_nt(a, b):
    return lax.dot_general(_mm(a), _mm(b), (((1,), (1,)), ((), ())), preferred_element_type=F32)


def _dot_tn(a, b):
    return lax.dot_general(_mm(a), _mm(b), (((0,), (0,)), ((), ())), preferred_element_type=F32)


def _rowsum(x):
    return jnp.sum(x, axis=0, keepdims=True)


def _in_proj(x2, w, b):
    t, d = x2.shape
    n = w.shape[1]
    tm = min(ROW_TILE, t)
    nc = 768

    def body(x_ref, w_ref, b_ref, z_ref):
        xb = _mm(x_ref[...])
        for j in range(n // nc):
            cs = slice(j * nc, (j + 1) * nc)
            z_ref[:, cs] = jnp.dot(xb, w_ref[:, cs], preferred_element_type=F32) + b_ref[:, cs]

    return pl.pallas_call(
        body, name="in_proj", grid=(t // tm,),
        in_specs=[pl.BlockSpec((tm, d), lambda i: (i, 0)), pl.BlockSpec((d, n), lambda i: (0, 0)),
                  pl.BlockSpec((1, n), lambda i: (0, 0))],
        out_specs=pl.BlockSpec((tm, n), lambda i: (i, 0)),
        out_shape=jax.ShapeDtypeStruct((t, n), F32),
        compiler_params=_cparams(("parallel",)),
    )(x2, w, b)


def _in_proj_bwd_dx(dxr, dz, w):
    t, d = dxr.shape
    n = w.shape[1]
    tm = min(ROW_TILE, t)

    def body(dxr_ref, dz_ref, w_ref, dx_ref):
        dx_ref[...] = dxr_ref[...] + _dot_nt(dz_ref[...], w_ref[...])

    return pl.pallas_call(
        body, name="in_proj_bwd_dx", grid=(t // tm,),
        in_specs=[pl.BlockSpec((tm, d), lambda i: (i, 0)), pl.BlockSpec((tm, n), lambda i: (i, 0)),
                  pl.BlockSpec((d, n), lambda i: (0, 0))],
        out_specs=pl.BlockSpec((tm, d), lambda i: (i, 0)),
        out_shape=jax.ShapeDtypeStruct((t, d), F32),
        compiler_params=_cparams(("parallel",)),
    )(dxr, dz, w)


def _in_proj_bwd_dw(x2, dz):
    t, d = x2.shape
    n = dz.shape[1]
    tm = min(ROW_TILE, t)
    nc = 768

    def body(x_ref, dz_ref, dw_ref):
        @pl.when(pl.program_id(0) == 0)
        def _():
            dw_ref[...] = jnp.zeros_like(dw_ref)
        xb = _mm(x_ref[...])
        for j in range(n // nc):
            cs = slice(j * nc, (j + 1) * nc)
            dw_ref[:, cs] += _dot_tn(xb, dz_ref[:, cs])

    return pl.pallas_call(
        body, name="in_proj_bwd_dw", grid=(t // tm,),
        in_specs=[pl.BlockSpec((tm, d), lambda i: (i, 0)), pl.BlockSpec((tm, n), lambda i: (i, 0))],
        out_specs=pl.BlockSpec((d, n), lambda i: (0, 0)),
        out_shape=jax.ShapeDtypeStruct((d, n), F32),
        compiler_params=_cparams(("arbitrary",)),
    )(x2, dz)


def _halo_specs(ts, s_len, cols, left=True):
    per = ts // HALO
    last = s_len // HALO - 1
    if left:
        return pl.BlockSpec((1, HALO, cols), lambda b, s: (b, jnp.maximum(s * per - 1, 0), 0))
    return pl.BlockSpec((1, HALO, cols), lambda b, s: (b, jnp.minimum((s + 1) * per, last), 0))


def _build_shifts(src_ref, sh_ref, rows):
    for r in range(1, 8):
        sh_ref[r - 1, 0:rows, :] = src_ref[r:r + rows, :]


def _shifted(src_ref, sh_ref, r, start, n):
    if r == 0:
        return src_ref[pl.ds(start, n), :]
    return sh_ref[r - 1, pl.ds(start, n), :]


def _tril_masks():
    ri = lax.broadcasted_iota(jnp.int32, (GBLK, GBLK), 0)
    ci = lax.broadcasted_iota(jnp.int32, (GBLK, GBLK), 1)
    return ri >= ci, ci >= ri


def _spatial(w_ref, keep, vb):
    lane = lax.broadcasted_iota(jnp.int32, (GBLK, GBLK), 1)
    outs = []
    for p in range(4):
        xs = vb[:, p * GBLK:(p + 1) * GBLK]
        r0 = jnp.dot(_mm(jnp.where(keep, w_ref[2 * p], 0.0)), xs, preferred_element_type=F32)
        r1 = jnp.dot(_mm(jnp.where(keep, w_ref[2 * p + 1], 0.0)), xs, preferred_element_type=F32)
        outs.append(jnp.where(lane < 64, r0, r1))
    return jnp.concatenate(outs, axis=1)


def _even_fwd(z3, cw, vec, ws, bsf):
    bsz, s_len, _ = z3.shape
    ts = min(SEQ_TILE, s_len)
    ext_rows = ts + HALO

    def body(z_ref, zl_ref, cw_ref, vec_ref, ws_ref, bsf_ref, y_ref, ext_ref, sh_ref, a1_ref):
        s = pl.program_id(1)
        hl = zl_ref[0]
        a0h = hl[:, 0:W_BR] * _sigmoid(hl[:, W_BR:2 * W_BR])
        ext_ref[0:HALO, :] = jnp.where(s > 0, a0h, 0.0)
        ext_ref[HALO:ext_rows, :] = z_ref[0, :, 0:W_BR] * _sigmoid(z_ref[0, :, W_BR:2 * W_BR])
        ext_ref[ext_rows:ext_rows + 8, :] = jnp.zeros((8, W_BR), F32)
        _build_shifts(ext_ref, sh_ref, ext_rows)

        def conv_chunk(ci, carry):
            base = pl.multiple_of(ci * CONV_CHUNK, CONV_CHUNK)
            acc = jnp.zeros((CONV_CHUNK, W_BR), F32) + vec_ref[0:1, :]
            for k in range(KA):
                q, r = divmod(2 + k, 8)
                acc = acc + _shifted(ext_ref, sh_ref, r, base + 8 * q, CONV_CHUNK) * cw_ref[k:k + 1, :]
            a1_ref[pl.ds(base, CONV_CHUNK), :] = acc
            return carry

        lax.fori_loop(0, ts // CONV_CHUNK, conv_chunk, 0)
        keep, _ = _tril_masks()

        def block(bi, carry):
            rows = pl.ds(pl.multiple_of(bi * GBLK, GBLK), GBLK)
            xh, _ = _ln_stats(a1_ref[rows, :])
            a = _silu(xh * vec_ref[1:2, :] + vec_ref[2:3, :]) * _silu(z_ref[0, rows, 2 * W_BR:3 * W_BR])
            y_ref[0, rows, 0:W_BR] = a.astype(y_ref.dtype)
            ua = _gelu(z_ref[0, rows, 3 * W_BR:4 * W_BR])
            vh, _ = _ln_stats(_gelu(z_ref[0, rows, 4 * W_BR:5 * W_BR]))
            vl = vh * vec_ref[3:4, :] + vec_ref[4:5, :]
            sg = _spatial(ws_ref, keep, _mm(vl)) + bsf_ref[...]
            g = ua * sg * _silu(z_ref[0, rows, 5 * W_BR:6 * W_BR])
            y_ref[0, rows, W_BR:2 * W_BR] = g.astype(y_ref.dtype)
            return carry

        lax.fori_loop(0, ts // GBLK, block, 0)

    full = lambda shape: pl.BlockSpec(shape, lambda b, s: (0,) * len(shape))
    return pl.pallas_call(
        body, name="even_fwd", grid=(bsz, s_len // ts),
        in_specs=[pl.BlockSpec((1, ts, N_COLS), lambda b, s: (b, s, 0)), _halo_specs(ts, s_len, 2 * W_BR),
                  full((32, W_BR)), full((8, W_BR)), full((8, GBLK, GBLK)), full((GBLK, W_BR))],
        out_specs=pl.BlockSpec((1, ts, 2 * W_BR), lambda b, s: (b, s, 0)),
        out_shape=jax.ShapeDtypeStruct((bsz, s_len, 2 * W_BR), MM_DTYPE),
        scratch_shapes=[pltpu.VMEM((ext_rows + 8, W_BR), F32), pltpu.VMEM((7, ext_rows, W_BR), F32),
                        pltpu.VMEM((ts, W_BR), F32)],
        compiler_params=_cparams(("parallel", "parallel")),
    )(z3, z3, cw, vec, ws, bsf)


def _even_bwd(z3, dy3, cw, vec, ws, wst, bsf):
    bsz, s_len, _ = z3.shape
    ts = min(SEQ_TILE, s_len)
    n_s = s_len // ts
    ext_rows = ts + 2 * HALO
    a_rows = ts + HALO

    def body(z_ref, zl_ref, zr_ref, dy_ref, dyr_ref, cw_ref, vec_ref, ws_ref, wst_ref, bsf_ref,
             dz_ref, dbin_ref, dcw_ref, dvec_ref, dws_ref, dbsf_ref,
             ext_ref, sh_ref, a1_ref, ag_ref, dya_ref, d_ref, accw_ref):
        b = pl.program_id(0)
        s = pl.program_id(1)

        @pl.when((b == 0) & (s == 0))
        def _():
            dbin_ref[...] = jnp.zeros_like(dbin_ref)
            dcw_ref[...] = jnp.zeros_like(dcw_ref)
            dvec_ref[...] = jnp.zeros_like(dvec_ref)
            dws_ref[...] = jnp.zeros_like(dws_ref)
            dbsf_ref[...] = jnp.zeros_like(dbsf_ref)

        has_right = s < n_s - 1
        hl = zl_ref[0]
        hr = zr_ref[0]
        ext_ref[0:HALO, :] = jnp.where(s > 0, hl[:, 0:W_BR] * _sigmoid(hl[:, W_BR:2 * W_BR]), 0.0)
        ext_ref[HALO:HALO + ts, :] = z_ref[0, :, 0:W_BR] * _sigmoid(z_ref[0, :, W_BR:2 * W_BR])
        ext_ref[HALO + ts:ext_rows, :] = hr[:, 0:W_BR] * _sigmoid(hr[:, W_BR:2 * W_BR])
        ext_ref[ext_rows:ext_rows + 8, :] = jnp.zeros((8, W_BR), F32)
        ag_ref[0:ts, :] = z_ref[0, :, 2 * W_BR:3 * W_BR]
        ag_ref[ts:a_rows, :] = hr[:, 2 * W_BR:3 * W_BR]
        dya_ref[0:ts, :] = dy_ref[0, :, 0:W_BR]
        dya_ref[ts:a_rows, :] = dyr_ref[0][:, 0:W_BR]
        _build_shifts(ext_ref, sh_ref, ext_rows)

        def conv_chunk(ci, carry):
            base = pl.multiple_of(ci * CONV_CHUNK, CONV_CHUNK)
            acc = jnp.zeros((CONV_CHUNK, W_BR), F32) + vec_ref[0:1, :]
            for k in range(KA):
                q, r = divmod(2 + k, 8)
                acc = acc + _shifted(ext_ref, sh_ref, r, base + 8 * q, CONV_CHUNK) * cw_ref[k:k + 1, :]
            a1_ref[pl.ds(base, CONV_CHUNK), :] = acc
            return carry

        lax.fori_loop(0, a_rows // CONV_CHUNK, conv_chunk, 0)

        def a_chunk(base, n, main):
            rows = pl.ds(base, n)
            xh, rs = _ln_stats(a1_ref[rows, :])
            ln = xh * vec_ref[1:2, :] + vec_ref[2:3, :]
            sl, dsl = _silu_grad(ln)
            sgt, dsgt = _silu_grad(ag_ref[rows, :])
            dya = dya_ref[rows, :]
            dln = dya * sgt * dsl
            da1 = _ln_bwd(dln * vec_ref[1:2, :], xh, rs)
            if main:
                d_ref[rows, :] = da1
                dag = dya * sl * dsgt
                dz_ref[0, rows, 2 * W_BR:3 * W_BR] = dag.astype(dz_ref.dtype)
                dbin_ref[0:1, 2 * W_BR:3 * W_BR] += _rowsum(dag)
                dvec_ref[0:1, :] += _rowsum(da1)
                dvec_ref[1:2, :] += _rowsum(dln * xh)
                dvec_ref[2:3, :] += _rowsum(dln)
            else:
                d_ref[rows, :] = jnp.where(has_right, da1, 0.0)

        def a_main(ci, carry):
            a_chunk(pl.multiple_of(ci * GBLK, GBLK), GBLK, True)
            return carry

        lax.fori_loop(0, ts // GBLK, a_main, 0)
        a_chunk(ts, HALO, False)
        d_ref[a_rows:a_rows + 8, :] = jnp.zeros((8, W_BR), F32)

        accw_ref[...] = jnp.zeros_like(accw_ref)

        def dw_chunk(ci, carry):
            base = pl.multiple_of(ci * CONV_CHUNK, CONV_CHUNK)
            d = d_ref[pl.ds(base, CONV_CHUNK), :]
            for k in range(KA):
                q, r = divmod(2 + k, 8)
                prod = d * _shifted(ext_ref, sh_ref, r, base + 8 * q, CONV_CHUNK)
                accw_ref[k] += jnp.sum(prod.reshape(CONV_CHUNK // 8, 8, W_BR), axis=0)
            return carry

        lax.fori_loop(0, ts // CONV_CHUNK, dw_chunk, 0)
        dcw_ref[...] += jnp.sum(accw_ref[...], axis=1)

        _build_shifts(d_ref, sh_ref, a_rows)

        def dx_chunk(ci, carry):
            base = pl.multiple_of(ci * CONV_CHUNK, CONV_CHUNK)
            rows = pl.ds(base, CONV_CHUNK)
            acc = jnp.zeros((CONV_CHUNK, W_BR), F32)
            for m in range(KA):
                q, r = divmod(m, 8)
                acc = acc + _shifted(d_ref, sh_ref, r, base + 8 * q, CONV_CHUNK) * cw_ref[KA - 1 - m:KA - m, :]
            aval = z_ref[0, rows, 0:W_BR]
            sg = _sigmoid(z_ref[0, rows, W_BR:2 * W_BR])
            dval = acc * sg
            dglu = acc * aval * sg * (1.0 - sg)
            dz_ref[0, rows, 0:W_BR] = dval.astype(dz_ref.dtype)
            dz_ref[0, rows, W_BR:2 * W_BR] = dglu.astype(dz_ref.dtype)
            dbin_ref[0:1, 0:W_BR] += _rowsum(dval)
            dbin_ref[0:1, W_BR:2 * W_BR] += _rowsum(dglu)
            return carry

        lax.fori_loop(0, ts // CONV_CHUNK, dx_chunk, 0)

        keep, keep_t = _tril_masks()
        lane = lax.broadcasted_iota(jnp.int32, (GBLK, GBLK), 1)

        def block(bi, carry):
            rows = pl.ds(pl.multiple_of(bi * GBLK, GBLK), GBLK)
            ua, dua = _gelu_grad(z_ref[0, rows, 3 * W_BR:4 * W_BR])
            va, dva = _gelu_grad(z_ref[0, rows, 4 * W_BR:5 * W_BR])
            sgt, dsgt = _silu_grad(z_ref[0, rows, 5 * W_BR:6 * W_BR])
            vh, rs = _ln_stats(va)
            vlb = _mm(vh * vec_ref[3:4, :] + vec_ref[4:5, :])
            sg = _spatial(ws_ref, keep, vlb) + bsf_ref[...]
            dyg = dy_ref[0, rows, W_BR:2 * W_BR]
            du = dyg * sg * sgt * dua
            dsg = dyg * ua * sgt
            dgg = dyg * ua * sg * dsgt
            dvl = _spatial(wst_ref, keep_t, _mm(dsg))
            for p in range(4):
                dsp = dsg[:, p * GBLK:(p + 1) * GBLK]
                vlp = vlb[:, p * GBLK:(p + 1) * GBLK]
                dws_ref[2 * p] += jnp.where(keep, _dot_nt(jnp.where(lane < 64, dsp, 0.0), vlp), 0.0)
                dws_ref[2 * p + 1] += jnp.where(keep, _dot_nt(jnp.where(lane >= 64, dsp, 0.0), vlp), 0.0)
            dbsf_ref[...] += dsg
            dvec_ref[3:4, :] += _rowsum(dvl * vh)
            dvec_ref[4:5, :] += _rowsum(dvl)
            dv = _ln_bwd(dvl * vec_ref[3:4, :], vh, rs) * dva
            dz_ref[0, rows, 3 * W_BR:4 * W_BR] = du.astype(dz_ref.dtype)
            dz_ref[0, rows, 4 * W_BR:5 * W_BR] = dv.astype(dz_ref.dtype)
            dz_ref[0, rows, 5 * W_BR:6 * W_BR] = dgg.astype(dz_ref.dtype)
            dbin_ref[0:1, 3 * W_BR:4 * W_BR] += _rowsum(du)
            dbin_ref[0:1, 4 * W_BR:5 * W_BR] += _rowsum(dv)
            dbin_ref[0:1, 5 * W_BR:6 * W_BR] += _rowsum(dgg)
            return carry

        lax.fori_loop(0, ts // GBLK, block, 0)

    full = lambda shape: pl.BlockSpec(shape, lambda b, s: (0,) * len(shape))
    acc_shapes = [(1, N_COLS), (32, W_BR), (8, W_BR), (8, GBLK, GBLK), (GBLK, W_BR)]
    return pl.pallas_call(
        body, name="even_bwd", grid=(bsz, n_s),
        in_specs=[pl.BlockSpec((1, ts, N_COLS), lambda b, s: (b, s, 0)),
                  _halo_specs(ts, s_len, N_COLS, True), _halo_specs(ts, s_len, N_COLS, False),
                  pl.BlockSpec((1, ts, 2 * W_BR), lambda b, s: (b, s, 0)), _halo_specs(ts, s_len, 2 * W_BR, False),
                  full((32, W_BR)), full((8, W_BR)), full((8, GBLK, GBLK)), full((8, GBLK, GBLK)), full((GBLK, W_BR))],
        out_specs=[pl.BlockSpec((1, ts, N_COLS), lambda b, s: (b, s, 0))] + [full(sh) for sh in acc_shapes],
        out_shape=[jax.ShapeDtypeStruct((bsz, s_len, N_COLS), MM_DTYPE)] + [jax.ShapeDtypeStruct(sh, F32) for sh in acc_shapes],
        scratch_shapes=[pltpu.VMEM((ext_rows + 8, W_BR), F32), pltpu.VMEM((7, ext_rows, W_BR), F32),
                        pltpu.VMEM((a_rows, W_BR), F32), pltpu.VMEM((a_rows, W_BR), F32), pltpu.VMEM((a_rows, W_BR), F32),
                        pltpu.VMEM((a_rows + 8, W_BR), F32), pltpu.VMEM((32, 8, W_BR), F32)],
        compiler_params=_cparams(("arbitrary", "arbitrary")),
    )(z3, z3, z3, dy3, dy3, cw, vec, ws, wst, bsf)


def _pool_stages(e_refs, rows):
    e0, e1, e2, e3, e4 = e_refs
    e1[8:rows, :] = e0[8:rows, :] + e0[7:rows - 1, :]
    e2[16:rows, GBLK:] = e1[16:rows, GBLK:] + e1[14:rows - 2, GBLK:]
    e3[24:rows, 2 * GBLK:] = e2[24:rows, 2 * GBLK:] + e2[20:rows - 4, 2 * GBLK:]
    e4[32:rows, 3 * GBLK:] = e3[32:rows, 3 * GBLK:] + e3[24:rows - 8, 3 * GBLK:]


def _pool_counts(start, n):
    pos = (start + 1 + lax.broadcasted_iota(jnp.int32, (n, 1), 0)).astype(F32)
    return [jnp.minimum(pos, float(w)) for w in POOL_WINDOWS]


def _pooled_into(e_refs, pooled_ref, s, ts):
    cnt = _pool_counts(s * ts, ts)
    for g in range(4):
        cs = slice(g * GBLK, (g + 1) * GBLK)
        pooled_ref[:, cs] = e_refs[g + 1][HALO:HALO + ts, cs] / cnt[g] - e_refs[0][HALO:HALO + ts, cs]


def _odd_fwd(z3, wp, vec):
    bsz, s_len, _ = z3.shape
    ts = min(SEQ_TILE, s_len)
    ext_rows = ts + HALO

    def body(z_ref, zl_ref, wp_ref, vec_ref, y_ref, e0, e1, e2, e3, e4, pooled_ref, dext_ref, ec_ref):
        s = pl.program_id(1)
        hl = zl_ref[0]
        e0[0:HALO, :] = jnp.where(s > 0, hl[:, 0:W_BR], 0.0)
        e0[HALO:ext_rows, :] = z_ref[0, :, 0:W_BR]
        _pool_stages((e0, e1, e2, e3, e4), ext_rows)
        _pooled_into((e0, e1, e2, e3, e4), pooled_ref, s, ts)
        dext_ref[0:HALO, :] = jnp.where(s > 0, hl[:, 2 * W_BR:3 * W_BR] * hl[:, 4 * W_BR:5 * W_BR], 0.0)
        dext_ref[HALO:ext_rows, :] = z_ref[0, :, 2 * W_BR:3 * W_BR] * z_ref[0, :, 4 * W_BR:5 * W_BR]
        ec_ref[...] = (vec_ref[1:2, :] * dext_ref[HALO - 2:HALO - 2 + ts, :] + vec_ref[2:3, :] * dext_ref[HALO - 1:HALO - 1 + ts, :]
                       + vec_ref[3:4, :] * dext_ref[HALO:HALO + ts, :])

        def block(bi, carry):
            rows = pl.ds(pl.multiple_of(bi * GBLK, GBLK), GBLK)
            pb = _mm(pooled_ref[rows, :])
            cpre = jnp.concatenate([jnp.dot(pb[:, g * GBLK:(g + 1) * GBLK], wp_ref[g], preferred_element_type=F32)
                                    for g in range(4)], axis=1)
            c = cpre * vec_ref[0:1, :] * _silu(z_ref[0, rows, W_BR:2 * W_BR])
            d = z_ref[0, rows, 3 * W_BR:4 * W_BR] * ec_ref[rows, :] * _silu(z_ref[0, rows, 5 * W_BR:6 * W_BR])
            y_ref[0, rows, 0:W_BR] = c.astype(y_ref.dtype)
            y_ref[0, rows, W_BR:2 * W_BR] = d.astype(y_ref.dtype)
            return carry

        lax.fori_loop(0, ts // GBLK, block, 0)

    full = lambda shape: pl.BlockSpec(shape, lambda b, s: (0,) * len(shape))
    ebuf = pltpu.VMEM((ext_rows, W_BR), F32)
    return pl.pallas_call(
        body, name="odd_fwd", grid=(bsz, s_len // ts),
        in_specs=[pl.BlockSpec((1, ts, N_COLS), lambda b, s: (b, s, 0)), _halo_specs(ts, s_len, N_COLS),
                  full((4, GBLK, GBLK)), full((8, W_BR))],
        out_specs=pl.BlockSpec((1, ts, 2 * W_BR), lambda b, s: (b, s, 0)),
        out_shape=jax.ShapeDtypeStruct((bsz, s_len, 2 * W_BR), MM_DTYPE),
        scratch_shapes=[ebuf, ebuf, ebuf, ebuf, ebuf, pltpu.VMEM((ts, W_BR), F32), ebuf, pltpu.VMEM((ts, W_BR), F32)],
        compiler_params=_cparams(("parallel", "parallel")),
    )(z3, z3, wp, vec)


def _odd_bwd(z3, dy3, wp, vec):
    bsz, s_len, _ = z3.shape
    ts = min(SEQ_TILE, s_len)
    n_s = s_len // ts
    ext_rows = ts + HALO

    def body(z_ref, zl_ref, zr_ref, dy_ref, dyr_ref, wp_ref, vec_ref,
             dz_ref, dbin_ref, dwp_ref, dvec_ref,
             e0, e1, e2, e3, e4, pooled_ref, dext_ref, ec_ref, q_ref, dp_ref, de_ref, f1, f2, f3, f4):
        b = pl.program_id(0)
        s = pl.program_id(1)

        @pl.when((b == 0) & (s == 0))
        def _():
            dbin_ref[...] = jnp.zeros_like(dbin_ref)
            dwp_ref[...] = jnp.zeros_like(dwp_ref)
            dvec_ref[...] = jnp.zeros_like(dvec_ref)

        has_right = s < n_s - 1
        hl = zl_ref[0]
        e0[0:HALO, :] = jnp.where(s > 0, hl[:, 0:W_BR], 0.0)
        e0[HALO:ext_rows, :] = z_ref[0, :, 0:W_BR]
        _pool_stages((e0, e1, e2, e3, e4), ext_rows)
        _pooled_into((e0, e1, e2, e3, e4), pooled_ref, s, ts)
        dext_ref[0:HALO, :] = jnp.where(s > 0, hl[:, 2 * W_BR:3 * W_BR] * hl[:, 4 * W_BR:5 * W_BR], 0.0)
        dext_ref[HALO:ext_rows, :] = z_ref[0, :, 2 * W_BR:3 * W_BR] * z_ref[0, :, 4 * W_BR:5 * W_BR]
        ec_ref[...] = (vec_ref[1:2, :] * dext_ref[HALO - 2:HALO - 2 + ts, :] + vec_ref[2:3, :] * dext_ref[HALO - 1:HALO - 1 + ts, :]
                       + vec_ref[3:4, :] * dext_ref[HALO:HALO + ts, :])

        def grads(zc_gate, zd_b, zd_gate, dyc, dyd, rows_out, n, start, valid):
            sgt = _silu(zc_gate)
            dcpre = dyc * vec_ref[0:1, :] * sgt
            db = _mm(dcpre)
            dpool = jnp.concatenate([_dot_nt(db[:, g * GBLK:(g + 1) * GBLK], wp_ref[g]) for g in range(4)], axis=1)
            cnt = _pool_counts(start, n)
            q = jnp.concatenate([dpool[:, g * GBLK:(g + 1) * GBLK] / cnt[g] for g in range(4)], axis=1)
            de = dyd * zd_b * _silu(zd_gate)
            if valid is not None:
                q = jnp.where(valid, q, 0.0)
                de = jnp.where(valid, de, 0.0)
            q_ref[rows_out, :] = q
            dp_ref[rows_out, :] = dpool
            de_ref[rows_out, :] = de
            return dcpre

        def block(bi, carry):
            base = pl.multiple_of(bi * GBLK, GBLK)
            rows = pl.ds(base, GBLK)
            cg = z_ref[0, rows, W_BR:2 * W_BR]
            dyc = dy_ref[0, rows, 0:W_BR]
            dyd = dy_ref[0, rows, W_BR:2 * W_BR]
            d_b = z_ref[0, rows, 3 * W_BR:4 * W_BR]
            d_gate = z_ref[0, rows, 5 * W_BR:6 * W_BR]
            dcpre = grads(cg, d_b, d_gate, dyc, dyd, rows, GBLK, s * ts + base, None)
            pb = _mm(pooled_ref[rows, :])
            dcb = _mm(dcpre)
            cpre = jnp.concatenate([jnp.dot(pb[:, g * GBLK:(g + 1) * GBLK], wp_ref[g], preferred_element_type=F32)
                                    for g in range(4)], axis=1)
            for g in range(4):
                cs = slice(g * GBLK, (g + 1) * GBLK)
                dwp_ref[g] += _dot_tn(pb[:, cs], dcb[:, cs])
            sgt, dsgt = _silu_grad(cg)
            dvec_ref[0:1, :] += _rowsum(dyc * cpre * sgt)
            dcg = dyc * cpre * vec_ref[0:1, :] * dsgt
            sdt, dsdt = _silu_grad(d_gate)
            ec = ec_ref[rows, :]
            ddb = dyd * ec * sdt
            ddg = dyd * d_b * ec * dsdt
            dz_ref[0, rows, W_BR:2 * W_BR] = dcg.astype(dz_ref.dtype)
            dz_ref[0, rows, 3 * W_BR:4 * W_BR] = ddb.astype(dz_ref.dtype)
            dz_ref[0, rows, 5 * W_BR:6 * W_BR] = ddg.astype(dz_ref.dtype)
            dbin_ref[0:1, W_BR:2 * W_BR] += _rowsum(dcg)
            dbin_ref[0:1, 3 * W_BR:4 * W_BR] += _rowsum(ddb)
            dbin_ref[0:1, 5 * W_BR:6 * W_BR] += _rowsum(ddg)
            return carry

        lax.fori_loop(0, ts // GBLK, block, 0)
        hr = zr_ref[0]
        dyr = dyr_ref[0]
        grads(hr[:, W_BR:2 * W_BR], hr[:, 3 * W_BR:4 * W_BR], hr[:, 5 * W_BR:6 * W_BR], dyr[:, 0:W_BR], dyr[:, W_BR:2 * W_BR],
              slice(ts, ext_rows), HALO, (s + 1) * ts, has_right)

        r1, r2, r3 = ts + 24, ts + 16, ts + 8
        f1[0:r1, :] = q_ref[0:r1, :] + q_ref[1:r1 + 1, :]
        f2[0:r2, GBLK:] = f1[0:r2, GBLK:] + f1[2:r2 + 2, GBLK:]
        f3[0:r3, 2 * GBLK:] = f2[0:r3, 2 * GBLK:] + f2[4:r3 + 4, 2 * GBLK:]
        f4[0:ts, 3 * GBLK:] = f3[0:ts, 3 * GBLK:] + f3[8:ts + 8, 3 * GBLK:]
        for g, f in enumerate((f1, f2, f3, f4)):
            cs = slice(g * GBLK, (g + 1) * GBLK)
            dvg = f[0:ts, cs] - dp_ref[0:ts, cs]
            dz_ref[0, :, cs] = dvg.astype(dz_ref.dtype)
            dbin_ref[0:1, cs] += _rowsum(dvg)

        ddc = (vec_ref[1:2, :] * de_ref[2:ts + 2, :] + vec_ref[2:3, :] * de_ref[1:ts + 1, :] + vec_ref[3:4, :] * de_ref[0:ts, :])
        d_h = z_ref[0, :, 2 * W_BR:3 * W_BR]
        d_c = z_ref[0, :, 4 * W_BR:5 * W_BR]
        ddh = ddc * d_c
        ddcc = ddc * d_h
        dz_ref[0, :, 2 * W_BR:3 * W_BR] = ddh.astype(dz_ref.dtype)
        dz_ref[0, :, 4 * W_BR:5 * W_BR] = ddcc.astype(dz_ref.dtype)
        dbin_ref[0:1, 2 * W_BR:3 * W_BR] += _rowsum(ddh)
        dbin_ref[0:1, 4 * W_BR:5 * W_BR] += _rowsum(ddcc)
        de = de_ref[0:ts, :]
        for k in range(3):
            dvec_ref[1 + k:2 + k, :] += _rowsum(de * dext_ref[HALO - 2 + k:HALO - 2 + k + ts, :])

    full = lambda shape: pl.BlockSpec(shape, lambda b, s: (0,) * len(shape))
    acc_shapes = [(1, N_COLS), (4, GBLK, GBLK), (8, W_BR)]
    ebuf = pltpu.VMEM((ext_rows, W_BR), F32)
    tbuf = pltpu.VMEM((ts, W_BR), F32)
    return pl.pallas_call(
        body, name="odd_bwd", grid=(bsz, n_s),
        in_specs=[pl.BlockSpec((1, ts, N_COLS), lambda b, s: (b, s, 0)),
                  _halo_specs(ts, s_len, N_COLS, True), _halo_specs(ts, s_len, N_COLS, False),
                  pl.BlockSpec((1, ts, 2 * W_BR), lambda b, s: (b, s, 0)), _halo_specs(ts, s_len, 2 * W_BR, False),
                  full((4, GBLK, GBLK)), full((8, W_BR))],
        out_specs=[pl.BlockSpec((1, ts, N_COLS), lambda b, s: (b, s, 0))] + [full(sh) for sh in acc_shapes],
        out_shape=[jax.ShapeDtypeStruct((bsz, s_len, N_COLS), MM_DTYPE)] + [jax.ShapeDtypeStruct(sh, F32) for sh in acc_shapes],
        scratch_shapes=[ebuf, ebuf, ebuf, ebuf, ebuf, tbuf, ebuf, tbuf, ebuf, ebuf, ebuf, ebuf, ebuf, ebuf, tbuf],
        compiler_params=_cparams(("arbitrary", "arbitrary")),
    )(z3, z3, z3, dy3, dy3, wp, vec)


def _post_fwd(y2, x2, p2, w_out, wg, wple, vec, tgt=None):
    t, d = x2.shape
    tm = min(ROW_TILE, t)
    last = tgt is not None

    def body(*refs):
        y_ref, x_ref, p_ref, wo_ref, wg_ref, wp_ref, vec_ref = refs[:7]
        xn_ref, r_ref, gate_ref = refs[7 + last:10 + last]
        r = ALPHA * x_ref[...] + jnp.dot(y_ref[...], wo_ref[...], preferred_element_type=F32) + vec_ref[0:1, :]
        r_ref[...] = r
        xh, _ = _ln_stats(r)
        h = xh * vec_ref[1:2, :] + vec_ref[2:3, :]
        gate = _sigmoid(_dot(h, wg_ref[...]) + vec_ref[3:4, :])
        gate_ref[...] = gate
        xn = h + gate * _dot(p_ref[...], wp_ref[...])
        if last:
            sq_ref = refs[11]

            @pl.when(pl.program_id(0) == 0)
            def _():
                sq_ref[...] = jnp.zeros_like(sq_ref)
            e = xn - refs[7][...]
            xn_ref[...] = e / float(d)
            sq_ref[...] += _rowsum(e * e)
        else:
            xn_ref[...] = xn

    row = lambda c: pl.BlockSpec((tm, c), lambda i: (i, 0))
    full = lambda shape: pl.BlockSpec(shape, lambda i: (0,) * len(shape))
    return pl.pallas_call(
        body, name="post_fwd_loss" if last else "post_fwd", grid=(t // tm,),
        in_specs=[row(d), row(d), row(D_PLE), full((d, d)), full((d, d)), full((D_PLE, d)), full((8, d))] + [row(d)] * last,
        out_specs=[row(d), row(d), row(d)] + [full((1, d))] * last,
        out_shape=[jax.ShapeDtypeStruct((t, d), F32)] * 3 + [jax.ShapeDtypeStruct((1, d), F32)] * last,
        compiler_params=_cparams(("arbitrary",) if last else ("parallel",)),
    )(y2, x2, p2, w_out, wg, wple, vec, *([tgt] if last else []))


def _post_bwd(dxn, r2, gate2, p2, y2, w_out, wg, wple, vec):
    t, d = r2.shape
    tm = min(BWD_ROW_TILE, t)

    def body(dxn_ref, r_ref, gate_ref, p_ref, y_ref, wo_ref, wg_ref, wp_ref, vec_ref,
             dxr_ref, dy_ref, dwo_ref, dwg_ref, dwp_ref, dvec_ref):
        @pl.when(pl.program_id(0) == 0)
        def _():
            dwo_ref[...] = jnp.zeros_like(dwo_ref)
            dwg_ref[...] = jnp.zeros_like(dwg_ref)
            dwp_ref[...] = jnp.zeros_like(dwp_ref)
            dvec_ref[...] = jnp.zeros_like(dvec_ref)

        dxn = dxn_ref[...]
        gate = gate_ref[...]
        xh, rs = _ln_stats(r_ref[...])
        hb = _mm(xh * vec_ref[1:2, :] + vec_ref[2:3, :])
        pb = _mm(p_ref[...])
        pe = jnp.dot(pb, wp_ref[...], preferred_element_type=F32)
        dpre = dxn * pe * gate * (1.0 - gate)
        dpb = _mm(dpre)
        dh = dxn + _dot_nt(dpb, wg_ref[...])
        dwg_ref[...] += _dot_tn(hb, dpb)
        dwp_ref[...] += _dot_tn(pb, dxn * gate)
        dr = _ln_bwd(dh * vec_ref[1:2, :], xh, rs)
        drb = _mm(dr)
        dxr_ref[...] = ALPHA * dr
        dy_ref[...] = _dot_nt(drb, wo_ref[...])
        dwo_ref[...] += _dot_tn(y_ref[...], drb)
        dvec_ref[0:1, :] += _rowsum(dr)
        dvec_ref[1:2, :] += _rowsum(dh * xh)
        dvec_ref[2:3, :] += _rowsum(dh)
        dvec_ref[3:4, :] += _rowsum(dpre)

    row = lambda c: pl.BlockSpec((tm, c), lambda i: (i, 0))
    full = lambda shape: pl.BlockSpec(shape, lambda i: (0,) * len(shape))
    acc_shapes = [(d, d), (d, d), (D_PLE, d), (8, d)]
    return pl.pallas_call(
        body, name="post_bwd", grid=(t // tm,),
        in_specs=[row(d), row(d), row(d), row(D_PLE), row(d), full((d, d)), full((d, d)), full((D_PLE, d)), full((8, d))],
        out_specs=[row(d), row(d)] + [full(sh) for sh in acc_shapes],
        out_shape=[jax.ShapeDtypeStruct((t, d), F32)] * 2 + [jax.ShapeDtypeStruct(sh, F32) for sh in acc_shapes],
        compiler_params=_cparams(("arbitrary",)),
    )(dxn, r2, gate2, p2, y2, w_out, wg, wple, vec)


def _place():
    x, y, c = lax.axis_index("x"), lax.axis_index("y"), lax.axis_index("c")
    chips = [(1 - x, y), (x, 1 - y), (1 - x, 1 - y)]
    return x, y, c, chips


def _shard_of(ref, ax, k, width, lo=None, ln=None):
    idx = [slice(None)] * 3
    if lo is not None:
        idx[0] = pl.ds(lo, ln)
    idx[ax] = pl.ds(k * width, width)
    return ref.at[tuple(idx)]


def _remote(src, dst, ssem, rsem, dev):
    return pltpu.make_async_remote_copy(src_ref=src, dst_ref=dst, send_sem=ssem, recv_sem=rsem, device_id=dev, device_id_type=MESH)


def _place_shard(w, ax, chip):
    l_dim, a_dim, b_dim = w.shape
    tr = min(256, a_dim)
    per = a_dim // tr
    shape = list(w.shape)
    shape[ax] *= 4
    if ax == 2:
        out_spec = pl.BlockSpec((1, tr, b_dim), lambda l, i, k: (l, i, k[0]))
    else:
        out_spec = pl.BlockSpec((1, tr, b_dim), lambda l, i, k: (l, k[0] * per + i, 0))

    def body(k_ref, w_ref, o_ref):
        o_ref[...] = w_ref[...].astype(o_ref.dtype)

    return pl.pallas_call(
        body, name="place_shard",
        grid_spec=pltpu.PrefetchScalarGridSpec(
            num_scalar_prefetch=1, grid=(l_dim, per),
            in_specs=[pl.BlockSpec((1, tr, b_dim), lambda l, i, k: (l, i, 0))], out_specs=out_spec),
        out_shape=jax.ShapeDtypeStruct(tuple(shape), MM_DTYPE),
        compiler_params=_cparams(("parallel", "parallel")),
    )(chip, w)


def _gather_weights(fulls, axes, sv):
    n = len(fulls)

    def body(*refs):
        in_refs, sv_ref = refs[:n], refs[n]
        fu_refs, sva_ref = refs[n + 1:2 * n + 1], refs[2 * n + 1]
        ssem, rsem, fssem, frsem, lsem = refs[2 * n + 2:]
        x, y, c, chips = _place()
        j = 2 * x + y
        sends, fwds = [], []
        loc = pltpu.make_async_copy(sv_ref, sva_ref.at[j], lsem)
        loc.start()
        for a in range(n):
            lh = fu_refs[a].shape[0] // 2
            w = fu_refs[a].shape[axes[a]] // 4
            for q, (qx, qy) in enumerate(chips):
                cp = _remote(_shard_of(in_refs[a], axes[a], j, w, c * lh, lh), _shard_of(fu_refs[a], axes[a], j, w, c * lh, lh),
                             ssem.at[a * 3 + q], rsem.at[a * 3 + q], (qx, qy, c))
                cp.start()
                sends.append(cp)
        for q, (qx, qy) in enumerate(chips):
            cp = _remote(sv_ref, sva_ref.at[j], ssem.at[n * 3 + q], rsem.at[n * 3 + q], (qx, qy, c))
            cp.start()
            sends.append(cp)
        for a in range(n):
            lh = fu_refs[a].shape[0] // 2
            w = fu_refs[a].shape[axes[a]] // 4
            for q, (qx, qy) in enumerate(chips):
                piece = _shard_of(fu_refs[a], axes[a], 2 * qx + qy, w, c * lh, lh)
                _remote(piece, piece, ssem.at[a * 3 + q], rsem.at[a * 3 + q], (x, y, c)).wait_recv()
                cp = _remote(piece, piece, fssem.at[a * 3 + q], frsem.at[a * 3 + q], (x, y, 1 - c))
                cp.start()
                fwds.append(cp)
        for q, (qx, qy) in enumerate(chips):
            _remote(sv_ref, sva_ref.at[2 * qx + qy], ssem.at[n * 3 + q], rsem.at[n * 3 + q], (x, y, c)).wait_recv()
        for a in range(n):
            lh = fu_refs[a].shape[0] // 2
            w = fu_refs[a].shape[axes[a]] // 4
            for q, (qx, qy) in enumerate(chips):
                piece = _shard_of(fu_refs[a], axes[a], 2 * qx + qy, w, (1 - c) * lh, lh)
                _remote(piece, piece, fssem.at[a * 3 + q], frsem.at[a * 3 + q], (x, y, c)).wait_recv()
        for cp in sends + fwds:
            cp.wait_send()
        loc.wait()

    nd = 3 * n + 3
    return pl.pallas_call(
        body, name="gather_weights",
        in_specs=[ANY] * (n + 1), out_specs=[ANY] * (n + 1),
        out_shape=[jax.ShapeDtypeStruct(f.shape, f.dtype) for f in fulls] + [jax.ShapeDtypeStruct((4,) + sv.shape, sv.dtype)],
        input_output_aliases={a: a for a in range(n)},
        scratch_shapes=[pltpu.SemaphoreType.DMA((nd,)), pltpu.SemaphoreType.DMA((nd,)),
                        pltpu.SemaphoreType.DMA((3 * n,)), pltpu.SemaphoreType.DMA((3 * n,)), pltpu.SemaphoreType.DMA],
        compiler_params=pltpu.CompilerParams(has_side_effects=True),
    )(*fulls, sv)


def _pair_exchange(grads, small):
    n = len(grads)
    outs = [jax.ShapeDtypeStruct((g.shape[0] // 2,) + g.shape[1:], g.dtype) for g in grads]
    outs.append(jax.ShapeDtypeStruct((small.shape[0] // 2, small.shape[1]), small.dtype))

    def body(*refs):
        g_refs, o_refs = refs[:n + 1], refs[n + 1:2 * n + 2]
        ssem, rsem = refs[2 * n + 2:]
        x, y, c, _ = _place()
        cps = []
        for a in range(n + 1):
            lh = g_refs[a].shape[0] // 2
            cp = _remote(g_refs[a].at[pl.ds((1 - c) * lh, lh)], o_refs[a], ssem.at[a], rsem.at[a], (x, y, 1 - c))
            cp.start()
            cps.append(cp)
        for cp in cps:
            cp.wait()

    return pl.pallas_call(
        body, name="pair_exchange", in_specs=[ANY] * (n + 1), out_specs=[ANY] * (n + 1), out_shape=outs,
        scratch_shapes=[pltpu.SemaphoreType.DMA((n + 1,)), pltpu.SemaphoreType.DMA((n + 1,))],
        compiler_params=pltpu.CompilerParams(has_side_effects=True),
    )(*grads, small)


def _chip_scatter(sums, axes, small):
    n = len(sums)
    outs = []
    for g, ax in zip(sums, axes):
        sh = list(g.shape)
        sh[ax] //= 4
        outs.append(jax.ShapeDtypeStruct((3,) + tuple(sh), g.dtype))
    rq = small.shape[0] // 4
    outs.append(jax.ShapeDtypeStruct((3, rq, small.shape[1]), small.dtype))

    def body(*refs):
        g_refs, o_refs = refs[:n + 1], refs[n + 1:2 * n + 2]
        ssem, rsem = refs[2 * n + 2:]
        x, y, c, chips = _place()
        cps = []
        for a in range(n + 1):
            for q, (qx, qy) in enumerate(chips):
                k = 2 * qx + qy
                if a < n:
                    src = _shard_of(g_refs[a], axes[a], k, g_refs[a].shape[axes[a]] // 4)
                else:
                    src = g_refs[a].at[pl.ds(k * rq, rq)]
                cp = _remote(src, o_refs[a].at[q], ssem.at[a * 3 + q], rsem.at[a * 3 + q], (qx, qy, c))
                cp.start()
                cps.append(cp)
        for cp in cps:
            cp.wait()

    return pl.pallas_call(
        body, name="chip_scatter", in_specs=[ANY] * (n + 1), out_specs=[ANY] * (n + 1), out_shape=outs,
        scratch_shapes=[pltpu.SemaphoreType.DMA((3 * n + 3,)), pltpu.SemaphoreType.DMA((3 * n + 3,))],
        compiler_params=pltpu.CompilerParams(has_side_effects=True),
    )(*sums, small)


def _final_exchange(reds, small):
    n = len(reds)
    flips = [(fx, fy, fc) for fx in (0, 1) for fy in (0, 1) for fc in (0, 1)][1:]

    def body(*refs):
        g_refs, o_refs = refs[:n + 1], refs[n + 1:2 * n + 2]
        ssem, rsem = refs[2 * n + 2:]
        x, y, c, _ = _place()
        cps = []
        for a in range(n):
            lh = g_refs[a].shape[0] // 2
            cp = _remote(g_refs[a].at[pl.ds(c * lh, lh)], o_refs[a].at[pl.ds(c * lh, lh)], ssem.at[a], rsem.at[a], (x, y, 1 - c))
            cp.start()
            cps.append(cp)
        mine = 4 * c + 2 * x + y
        for f, (fx, fy, fc) in enumerate(flips):
            cp = _remote(g_refs[n].at[mine], o_refs[n].at[mine], ssem.at[n + f], rsem.at[n + f], (x ^ fx, y ^ fy, c ^ fc))
            cp.start()
            cps.append(cp)
        for cp in cps:
            cp.wait()

    return pl.pallas_call(
        body, name="final_exchange", in_specs=[ANY] * (n + 1), out_specs=[ANY] * (n + 1),
        out_shape=[jax.ShapeDtypeStruct(g.shape, g.dtype) for g in reds] + [jax.ShapeDtypeStruct(small.shape, small.dtype)],
        input_output_aliases={a: a for a in range(n + 1)},
        scratch_shapes=[pltpu.SemaphoreType.DMA((n + 7,)), pltpu.SemaphoreType.DMA((n + 7,))],
        compiler_params=pltpu.CompilerParams(has_side_effects=True),
    )(*reds, small)


def _pair_sum(g, got, half):
    lh, a_dim, b_dim = got.shape
    tr = min(256, a_dim)

    def body(half_ref, g_ref, o_ref, s_ref):
        s_ref[...] = (g_ref[...] + o_ref[...]).astype(s_ref.dtype)

    return pl.pallas_call(
        body, name="pair_sum",
        grid_spec=pltpu.PrefetchScalarGridSpec(
            num_scalar_prefetch=1, grid=(lh, a_dim // tr),
            in_specs=[pl.BlockSpec((1, tr, b_dim), lambda l, i, h: (h[0] * lh + l, i, 0)),
                      pl.BlockSpec((1, tr, b_dim), lambda l, i, h: (l, i, 0))],
            out_specs=pl.BlockSpec((1, tr, b_dim), lambda l, i, h: (l, i, 0))),
        out_shape=jax.ShapeDtypeStruct(got.shape, WIRE_DTYPE),
        compiler_params=_cparams(("parallel", "parallel")),
    )(half, g, got)


def _chip_sum(own, got, ax, pos):
    _, lh, a_dim, b_dim = got.shape
    tr = min(256, a_dim)
    if ax == 2:
        own_spec = pl.BlockSpec((1, tr, b_dim), lambda l, i, k: (l, i, k[0]))
    else:
        per = a_dim // tr
        own_spec = pl.BlockSpec((1, tr, b_dim), lambda l, i, k: (l, k[0] * per + i, 0))

    def body(k_ref, own_ref, got_ref, s_ref):
        s_ref[...] = ((own_ref[...].astype(F32) + got_ref[0].astype(F32)) + got_ref[1].astype(F32)) + got_ref[2].astype(F32)

    return pl.pallas_call(
        body, name="chip_sum",
        grid_spec=pltpu.PrefetchScalarGridSpec(
            num_scalar_prefetch=1, grid=(lh, a_dim // tr),
            in_specs=[own_spec, pl.BlockSpec((3, 1, tr, b_dim), lambda l, i, k: (0, l, i, 0))],
            out_specs=pl.BlockSpec((1, tr, b_dim), lambda l, i, k: (k[1] * lh + l, i, 0))),
        out_shape=jax.ShapeDtypeStruct((2 * lh, a_dim, b_dim), F32),
        compiler_params=_cparams(("parallel", "parallel")),
    )(pos, own, got)


def _small_pair_sum(small, got, half):
    rh = small.shape[0] // 2

    def body(h_ref, g_ref, o_ref, s_ref):
        s_ref[...] = g_ref[...] + o_ref[...]

    return pl.pallas_call(
        body, name="small_pair_sum",
        grid_spec=pltpu.PrefetchScalarGridSpec(
            num_scalar_prefetch=1, grid=(1,),
            in_specs=[pl.BlockSpec((rh, 128), lambda i, h: (h[0], 0)), pl.BlockSpec((rh, 128), lambda i, h: (0, 0))],
            out_specs=pl.BlockSpec((rh, 128), lambda i, h: (0, 0))),
        out_shape=jax.ShapeDtypeStruct((rh, 128), F32),
    )(half, small, got)


def _small_chip_sum(s1, got, pos):
    rq = got.shape[1]

    def body(k_ref, own_ref, got_ref, s_ref):
        s_ref[0] = ((own_ref[...] + got_ref[0]) + got_ref[1]) + got_ref[2]

    return pl.pallas_call(
        body, name="small_chip_sum",
        grid_spec=pltpu.PrefetchScalarGridSpec(
            num_scalar_prefetch=1, grid=(1,),
            in_specs=[pl.BlockSpec((rq, 128), lambda i, k: (k[0], 0)), pl.BlockSpec((3, rq, 128), lambda i, k: (0, 0, 0))],
            out_specs=pl.BlockSpec((1, rq, 128), lambda i, k: (4 * k[1] + k[0], 0, 0))),
        out_shape=jax.ShapeDtypeStruct((8, rq, 128), F32),
    )(pos, s1, got)


def _adam_math(w, g, m, v):
    m = ADAM_B1 * m + (1.0 - ADAM_B1) * g
    v = ADAM_B2 * v + (1.0 - ADAM_B2) * (g * g)
    m_hat = m / (1.0 - ADAM_B1 ** ADAM_STEP)
    v_hat = v / (1.0 - ADAM_B2 ** ADAM_STEP)
    return -ADAM_LR * (m_hat / (jnp.sqrt(v_hat) + ADAM_EPS) + ADAM_WD * w), m, v


def _adamw_big(w, g, m, v):
    l_dim, a_dim, b_dim = w.shape
    tr = min(256, a_dim)

    def body(w_ref, g_ref, m_ref, v_ref, d_ref, nm_ref, nv_ref):
        d_ref[...], nm_ref[...], nv_ref[...] = _adam_math(w_ref[...], g_ref[...], m_ref[...], v_ref[...])

    blk = pl.BlockSpec((1, tr, b_dim), lambda l, i: (l, i, 0))
    return pl.pallas_call(
        body, name="adamw_big", grid=(l_dim, a_dim // tr), in_specs=[blk] * 4, out_specs=[blk] * 3,
        out_shape=[jax.ShapeDtypeStruct(w.shape, F32)] * 3,
        compiler_params=_cparams(("parallel", "parallel")),
    )(w, g, m, v)


def _adamw_small(ws, gs, ms, vs):
    n = len(ws)

    def body(*refs):
        for i in range(n):
            w_ref, g_ref, m_ref, v_ref = refs[i], refs[n + i], refs[2 * n + i], refs[3 * n + i]
            d_ref, nm_ref, nv_ref = refs[4 * n + i], refs[5 * n + i], refs[6 * n + i]
            d_ref[...], nm_ref[...], nv_ref[...] = _adam_math(w_ref[...], g_ref[...], m_ref[...], v_ref[...])

    shapes = [jax.ShapeDtypeStruct(w.shape, F32) for w in ws]
    outs = pl.pallas_call(body, name="adamw_small", out_shape=shapes * 3,
                          compiler_params=_cparams())(*ws, *gs, *ms, *vs)
    return outs[:n], outs[n:2 * n], outs[2 * n:]


def _pack(arrs, row_mult):
    parts = []
    for a in arrs:
        flat = a.reshape(-1)
        pad = (-flat.shape[0]) % 1024
        parts.append(jnp.pad(flat, (0, pad)).reshape(-1, 128))
    buf = jnp.concatenate(parts, axis=0)
    pad = (-buf.shape[0]) % row_mult
    return jnp.pad(buf, ((0, pad), (0, 0)))


def _unpack(buf, shapes):
    out, row = [], 0
    for sh in shapes:
        n = math.prod(sh)
        rows = -(-n // 1024) * 8
        out.append(buf[row:row + rows].reshape(-1)[:n].reshape(sh))
        row += rows
    return out


_NAMES = ['w_in_e', 'b_in_e', 'conv_a_w', 'conv_a_b', 'ln_a_g', 'ln_a_b', 'ln_v_g', 'ln_v_b', 'w_s', 'b_s', 'w_out_e', 'b_out_e',
          'w_in_o', 'b_in_o', 'w_pool', 'pool_scale', 'conv_d_w', 'w_out_o', 'b_out_o', 'ln_g', 'ln_b', 'w_ple', 'w_ple_gate',
          'b_ple_gate']
_BIG = ['w_in_e', 'w_out_e', 'w_in_o', 'w_out_o', 'w_ple', 'w_ple_gate']
_BIG_AXES = [2, 1, 2, 1, 2, 1]
_SMALL_SHARDED = ['conv_a_w', 'b_in_o', 'pool_scale', 'conv_d_w', 'b_out_o']


def kernel(x, p, w_in_e, b_in_e, conv_a_w, conv_a_b, ln_a_g, ln_a_b, ln_v_g, ln_v_b, w_s, b_s, w_out_e, b_out_e, w_in_o, b_in_o, w_pool, pool_scale, conv_d_w, w_out_o, b_out_o, ln_g, ln_b, w_ple, w_ple_gate, b_ple_gate, loss_target, m_w_in_e, m_b_in_e, m_conv_a_w, m_conv_a_b, m_ln_a_g, m_ln_a_b, m_ln_v_g, m_ln_v_b, m_w_s, m_b_s, m_w_out_e, m_b_out_e, m_w_in_o, m_b_in_o, m_w_pool, m_pool_scale, m_conv_d_w, m_w_out_o, m_b_out_o, m_ln_g, m_ln_b, m_w_ple, m_w_ple_gate, m_b_ple_gate, v_w_in_e, v_b_in_e, v_conv_a_w, v_conv_a_b, v_ln_a_g, v_ln_a_b, v_ln_v_g, v_ln_v_b, v_w_s, v_b_s, v_w_out_e, v_b_out_e, v_w_in_o, v_b_in_o, v_w_pool, v_pool_scale, v_conv_d_w, v_w_out_o, v_b_out_o, v_ln_g, v_ln_b, v_w_ple, v_w_ple_gate, v_b_ple_gate):
    args = locals()
    wts = {n: args[n] for n in _NAMES}
    mom = {n: args["m_" + n] for n in _NAMES}
    var = {n: args["v_" + n] for n in _NAMES}
    bsz, s_len, d = x.shape
    t = bsz * s_len
    cx, cy, cc = lax.axis_index("x"), lax.axis_index("y"), lax.axis_index("c")
    chip = (2 * cx + cy).astype(jnp.int32).reshape(1)
    half = cc.astype(jnp.int32).reshape(1)
    pos = jnp.concatenate([chip, half])

    sv = _pack([wts[n] for n in _SMALL_SHARDED], 8)
    *full_big, sv_all = _gather_weights([_place_shard(wts[n], ax, chip) for n, ax in zip(_BIG, _BIG_AXES)], _BIG_AXES, sv)
    fw = dict(zip(_BIG, full_big))
    small_parts = [_unpack(sv_all[k], [wts[n].shape for n in _SMALL_SHARDED]) for k in range(4)]
    for i, n in enumerate(_SMALL_SHARDED):
        fw[n] = jnp.concatenate([small_parts[k][i] for k in range(4)], axis=-1)
    for n in _NAMES:
        fw.setdefault(n, wts[n])

    def row8(rows, width):
        rows = [r.reshape(1, width) for r in rows]
        return jnp.concatenate(rows + [jnp.zeros((8 - len(rows), width), F32)], axis=0)

    x2 = x.reshape(t, d)
    saved = []
    for i in range(DEPTH):
        j = i // 2
        even = i % 2 == 0
        w_in, b_in, w_out, b_out = ((fw['w_in_e'], fw['b_in_e'], fw['w_out_e'], fw['b_out_e']) if even else
                                    (fw['w_in_o'], fw['b_in_o'], fw['w_out_o'], fw['b_out_o']))
        z = _in_proj(x2, w_in[j], b_in[j].reshape(1, N_COLS))
        z3 = z.reshape(bsz, s_len, N_COLS)
        if even:
            cw = jnp.concatenate([fw['conv_a_w'][j], jnp.zeros((1, W_BR), F32)], axis=0)
            mvec = row8([fw['conv_a_b'][j], fw['ln_a_g'][j], fw['ln_a_b'][j], fw['ln_v_g'][j], fw['ln_v_b'][j]], W_BR)
            bsf = jnp.repeat(fw['b_s'][j].T, W_BR // 8, axis=1)
            mix = (cw, mvec, fw['w_s'][j], jnp.swapaxes(fw['w_s'][j], 1, 2), bsf)
            y3 = _even_fwd(z3, cw, mvec, fw['w_s'][j], bsf)
        else:
            mvec = row8([fw['pool_scale'][j]] + [fw['conv_d_w'][j][k] for k in range(3)], W_BR)
            mix = (fw['w_pool'][j].astype(MM_DTYPE), mvec)
            y3 = _odd_fwd(z3, mix[0], mvec)
        pvec = row8([b_out[j], fw['ln_g'][i], fw['ln_b'][i], fw['b_ple_gate'][i]], d)
        post_w = (w_out[j], fw['w_ple_gate'][i], fw['w_ple'][i], pvec)
        p2 = p[i].reshape(t, D_PLE)
        y2 = y3.reshape(t, 2 * W_BR)
        if i < DEPTH - 1:
            xn, r2, gate2 = _post_fwd(y2, x2, p2, *post_w)
        else:
            dx, r2, gate2, sq = _post_fwd(y2, x2, p2, *post_w, tgt=loss_target.reshape(t, d))
        saved.append((x2, z3, y2, r2, gate2, p2, w_in[j], mix, post_w))
        x2 = xn

    loss = lax.psum(0.5 * jnp.sum(sq) / d, ("x", "y", "c"))

    gr = {n: [None] * wts[n].shape[0] for n in _NAMES}
    for i in reversed(range(DEPTH)):
        j = i // 2
        even = i % 2 == 0
        x_in, z3, y2, r2, gate2, p2, w_in, mix, post_w = saved[i]
        dxr, dy, dwo, dwg, dwp, dpv = _post_bwd(dx, r2, gate2, p2, y2, *post_w)
        dy3 = dy.reshape(bsz, s_len, 2 * W_BR)
        sfx = '_e' if even else '_o'
        gr['w_out' + sfx][j], gr['b_out' + sfx][j] = dwo, dpv[0]
        gr['w_ple_gate'][i], gr['w_ple'][i] = dwg, dwp
        gr['ln_g'][i], gr['ln_b'][i], gr['b_ple_gate'][i] = dpv[1], dpv[2], dpv[3]
        if even:
            dz3, dbin, dcw, dmv, dws, dbsf = _even_bwd(z3, dy3, *mix)
            gr['conv_a_w'][j], gr['conv_a_b'][j] = dcw[:KA], dmv[0]
            gr['ln_a_g'][j], gr['ln_a_b'][j], gr['ln_v_g'][j], gr['ln_v_b'][j] = dmv[1], dmv[2], dmv[3], dmv[4]
            gr['w_s'][j] = dws
            gr['b_s'][j] = jnp.sum(dbsf.reshape(GBLK, 8, W_BR // 8), axis=2).T
        else:
            dz3, dbin, dwpool, dmv = _odd_bwd(z3, dy3, *mix)
            gr['w_pool'][j], gr['pool_scale'][j], gr['conv_d_w'][j] = dwpool, dmv[0], dmv[1:4]
        gr['b_in' + sfx][j] = dbin[0]
        dz2 = dz3.reshape(t, N_COLS)
        gr['w_in' + sfx][j] = _in_proj_bwd_dw(x_in, dz2)
        dx = _in_proj_bwd_dx(dxr, dz2, w_in)
    grad_x = dx.reshape(bsz, s_len, d)

    small_names = [n for n in _NAMES if n not in _BIG]
    g_big = [jnp.stack(gr[n]) for n in _BIG]
    g_small_full = [jnp.stack(gr[n]) for n in small_names]
    small = _pack(g_small_full, 64)
    *got_pair, got_small = _pair_exchange(g_big, small)
    sums = [_pair_sum(g, o, half) for g, o in zip(g_big, got_pair)]
    s1 = _small_pair_sum(small, got_small, half)
    *got_chips, got_small2 = _chip_scatter(sums, _BIG_AXES, s1)
    reds = [_chip_sum(sm, o, ax, pos) for sm, o, ax in zip(sums, got_chips, _BIG_AXES)]
    s2 = _small_chip_sum(s1, got_small2, pos)
    *g_shards, small_all = _final_exchange(reds, s2)
    g_small = _unpack(small_all.reshape(small.shape), [g.shape for g in g_small_full])
    grads = dict(zip(_BIG, g_shards))
    for n, g in zip(small_names, g_small):
        if n in _SMALL_SHARDED:
            w = wts[n].shape[-1]
            g = lax.dynamic_slice_in_dim(g, (2 * cx + cy) * w, w, axis=g.ndim - 1)
        grads[n] = g

    delta, new_m, new_v = {}, {}, {}
    for n in _BIG:
        delta[n], new_m[n], new_v[n] = _adamw_big(wts[n], grads[n], mom[n], var[n])
    ds, ms, vs = _adamw_small([wts[n] for n in small_names], [grads[n] for n in small_names],
                              [mom[n] for n in small_names], [var[n] for n in small_names])
    for n, a, b, c_ in zip(small_names, ds, ms, vs):
        delta[n], new_m[n], new_v[n] = a, b, c_

    return (loss, grad_x, *[grads[n] for n in _NAMES], *[delta[n] for n in _NAMES],
            *[new_m[n] for n in _NAMES], *[new_v[n] for n in _NAMES])
```

```python
import functools
import math

import jax
import jax.numpy as jnp
from jax import lax
from jax.experimental import pallas as pl
from jax.experimental.pallas import tpu as pltpu
from jax.experimental.pallas import tpu_sc as plsc

F32 = jnp.float32
MM_DTYPE = jnp.bfloat16
WIRE_DTYPE = jnp.bfloat16
SEQ_TILE = 512
ROW_TILE = 512
BWD_ROW_TILE = 256
HALO = 32
CONV_CHUNK = 32
GBLK = 128
VMEM_LIMIT = 56 * 1024 * 1024

D_MODEL = 1024
W_BR = 512
N_COLS = 6 * W_BR
D_PLE = 256
KA = 31
DEPTH = 4
POOL_WINDOWS = (2, 4, 8, 16)
ALPHA = (2.0 * DEPTH) ** 0.25
LN_EPS = 1e-5
GELU_C = math.sqrt(2.0 / math.pi)

ADAM_LR, ADAM_B1, ADAM_B2, ADAM_EPS, ADAM_WD, ADAM_STEP = 0.001, 0.9, 0.999, 1e-08, 0.01, 10

MESH = pl.DeviceIdType.MESH
ANY = pl.BlockSpec(memory_space=pl.ANY)


def _cparams(sem=None):
    return pltpu.CompilerParams(dimension_semantics=sem, vmem_limit_bytes=VMEM_LIMIT)


def _sigmoid(x):
    return 1.0 / (1.0 + jnp.exp(-x))


def _silu(x):
    return x * _sigmoid(x)


def _silu_grad(x):
    s = _sigmoid(x)
    return x * s, s * (1.0 + x * (1.0 - s))


def _gelu(x):
    return 0.5 * x * (1.0 + jnp.tanh(GELU_C * (x + 0.044715 * (x * x * x))))


def _gelu_grad(x):
    x2 = x * x
    th = jnp.tanh(GELU_C * (x + 0.044715 * (x * x2)))
    return 0.5 * x * (1.0 + th), 0.5 * (1.0 + th) + 0.5 * x * (1.0 - th * th) * (GELU_C * (1.0 + 3.0 * 0.044715 * x2))


def _ln_stats(x):
    mu = jnp.mean(x, axis=-1, keepdims=True)
    d = x - mu
    var = jnp.mean(d * d, axis=-1, keepdims=True)
    rs = lax.rsqrt(var + LN_EPS)
    return d * rs, rs


def _ln_bwd(dxh, xh, rs):
    return rs * (dxh - jnp.mean(dxh, axis=-1, keepdims=True) - xh * jnp.mean(dxh * xh, axis=-1, keepdims=True))


def _mm(a):
    return a.astype(MM_DTYPE)


def _dot(a, b):
    return jnp.dot(_mm(a), _mm(b), preferred_element_type=F32)


def _dot_nt(a, b):
    return lax.dot_general(_mm(a), _mm(b), (((1,), (1,)), ((), ())), preferred_element_type=F32)


def _dot_tn(a, b):
    return lax.dot_general(_mm(a), _mm(b), (((0,), (0,)), ((), ())), preferred_element_type=F32)


def _rowsum(x):
    return jnp.sum(x, axis=0, keepdims=True)


def _in_proj(x2, w, b):
    t, d = x2.shape
    n = w.shape[1]
    tm = min(ROW_TILE, t)
    nc = 768

    def body(x_ref, w_ref, b_ref, z_ref):
        xb = _mm(x_ref[...])
        for j in range(n // nc):
            cs = slice(j * nc, (j + 1) * nc)
            z_ref[:, cs] = jnp.dot(xb, w_ref[:, cs], preferred_element_type=F32) + b_ref[:, cs]

    return pl.pallas_call(
        body, name="in_proj", grid=(t // tm,),
        in_specs=[pl.BlockSpec((tm, d), lambda i: (i, 0)), pl.BlockSpec((d, n), lambda i: (0, 0)),
                  pl.BlockSpec((1, n), lambda i: (0, 0))],
        out_specs=pl.BlockSpec((tm, n), lambda i: (i, 0)),
        out_shape=jax.ShapeDtypeStruct((t, n), F32),
        compiler_params=_cparams(("parallel",)),
    )(x2, w, b)


def _in_proj_bwd_dx(dxr, dz, w):
    t, d = dxr.shape
    n = w.shape[1]
    tm = min(ROW_TILE, t)

    def body(dxr_ref, dz_ref, w_ref, dx_ref):
        dx_ref[...] = dxr_ref[...] + _dot_nt(dz_ref[...], w_ref[...])

    return pl.pallas_call(
        body, name="in_proj_bwd_dx", grid=(t // tm,),
        in_specs=[pl.BlockSpec((tm, d), lambda i: (i, 0)), pl.BlockSpec((tm, n), lambda i: (i, 0)),
                  pl.BlockSpec((d, n), lambda i: (0, 0))],
        out_specs=pl.BlockSpec((tm, d), lambda i: (i, 0)),
        out_shape=jax.ShapeDtypeStruct((t, d), F32),
        compiler_params=_cparams(("parallel",)),
    )(dxr, dz, w)


def _in_proj_bwd_dw(x2, dz):
    t, d = x2.shape
    n = dz.shape[1]
    tm = min(ROW_TILE, t)
    nc = 768

    def body(x_ref, dz_ref, dw_ref):
        @pl.when(pl.program_id(0) == 0)
        def _():
            dw_ref[...] = jnp.zeros_like(dw_ref)
        xb = _mm(x_ref[...])
        for j in range(n // nc):
            cs = slice(j * nc, (j + 1) * nc)
            dw_ref[:, cs] += _dot_tn(xb, dz_ref[:, cs])

    return pl.pallas_call(
        body, name="in_proj_bwd_dw", grid=(t // tm,),
        in_specs=[pl.BlockSpec((tm, d), lambda i: (i, 0)), pl.BlockSpec((tm, n), lambda i: (i, 0))],
        out_specs=pl.BlockSpec((d, n), lambda i: (0, 0)),
        out_shape=jax.ShapeDtypeStruct((d, n), F32),
        compiler_params=_cparams(("arbitrary",)),
    )(x2, dz)


def _halo_specs(ts, s_len, cols, left=True):
    per = ts // HALO
    last = s_len // HALO - 1
    if left:
        return pl.BlockSpec((1, HALO, cols), lambda b, s: (b, jnp.maximum(s * per - 1, 0), 0))
    return pl.BlockSpec((1, HALO, cols), lambda b, s: (b, jnp.minimum((s + 1) * per, last), 0))


def _build_shifts(src_ref, sh_ref, rows):
    for r in range(1, 8):
        sh_ref[r - 1, 0:rows, :] = src_ref[r:r + rows, :]


def _shifted(src_ref, sh_ref, r, start, n):
    if r == 0:
        return src_ref[pl.ds(start, n), :]
    return sh_ref[r - 1, pl.ds(start, n), :]


def _tril_masks():
    ri = lax.broadcasted_iota(jnp.int32, (GBLK, GBLK), 0)
    ci = lax.broadcasted_iota(jnp.int32, (GBLK, GBLK), 1)
    return ri >= ci, ci >= ri


def _spatial(w_ref, keep, vb):
    lane = lax.broadcasted_iota(jnp.int32, (GBLK, GBLK), 1)
    outs = []
    for p in range(4):
        xs = vb[:, p * GBLK:(p + 1) * GBLK]
        r0 = jnp.dot(_mm(jnp.where(keep, w_ref[2 * p], 0.0)), xs, preferred_element_type=F32)
        r1 = jnp.dot(_mm(jnp.where(keep, w_ref[2 * p + 1], 0.0)), xs, preferred_element_type=F32)
        outs.append(jnp.where(lane < 64, r0, r1))
    return jnp.concatenate(outs, axis=1)


def _even_fwd(z3, cw, vec, ws, bsf):
    bsz, s_len, _ = z3.shape
    ts = min(SEQ_TILE, s_len)
    ext_rows = ts + HALO

    def body(z_ref, zl_ref, cw_ref, vec_ref, ws_ref, bsf_ref, y_ref, ext_ref, sh_ref, a1_ref):
        s = pl.program_id(1)
        hl = zl_ref[0]
        a0h = hl[:, 0:W_BR] * _sigmoid(hl[:, W_BR:2 * W_BR])
        ext_ref[0:HALO, :] = jnp.where(s > 0, a0h, 0.0)
        ext_ref[HALO:ext_rows, :] = z_ref[0, :, 0:W_BR] * _sigmoid(z_ref[0, :, W_BR:2 * W_BR])
        ext_ref[ext_rows:ext_rows + 8, :] = jnp.zeros((8, W_BR), F32)
        _build_shifts(ext_ref, sh_ref, ext_rows)

        def conv_chunk(ci, carry):
            base = pl.multiple_of(ci * CONV_CHUNK, CONV_CHUNK)
            acc = jnp.zeros((CONV_CHUNK, W_BR), F32) + vec_ref[0:1, :]
            for k in range(KA):
                q, r = divmod(2 + k, 8)
                acc = acc + _shifted(ext_ref, sh_ref, r, base + 8 * q, CONV_CHUNK) * cw_ref[k:k + 1, :]
            a1_ref[pl.ds(base, CONV_CHUNK), :] = acc
            return carry

        lax.fori_loop(0, ts // CONV_CHUNK, conv_chunk, 0)
        keep, _ = _tril_masks()

        def block(bi, carry):
            rows = pl.ds(pl.multiple_of(bi * GBLK, GBLK), GBLK)
            xh, _ = _ln_stats(a1_ref[rows, :])
            a = _silu(xh * vec_ref[1:2, :] + vec_ref[2:3, :]) * _silu(z_ref[0, rows, 2 * W_BR:3 * W_BR])
            y_ref[0, rows, 0:W_BR] = a.astype(y_ref.dtype)
            ua = _gelu(z_ref[0, rows, 3 * W_BR:4 * W_BR])
            vh, _ = _ln_stats(_gelu(z_ref[0, rows, 4 * W_BR:5 * W_BR]))
            vl = vh * vec_ref[3:4, :] + vec_ref[4:5, :]
            sg = _spatial(ws_ref, keep, _mm(vl)) + bsf_ref[...]
            g = ua * sg * _silu(z_ref[0, rows, 5 * W_BR:6 * W_BR])
            y_ref[0, rows, W_BR:2 * W_BR] = g.astype(y_ref.dtype)
            return carry

        lax.fori_loop(0, ts // GBLK, block, 0)

    full = lambda shape: pl.BlockSpec(shape, lambda b, s: (0,) * len(shape))
    return pl.pallas_call(
        body, name="even_fwd", grid=(bsz, s_len // ts),
        in_specs=[pl.BlockSpec((1, ts, N_COLS), lambda b, s: (b, s, 0)), _halo_specs(ts, s_len, 2 * W_BR),
                  full((32, W_BR)), full((8, W_BR)), full((8, GBLK, GBLK)), full((GBLK, W_BR))],
        out_specs=pl.BlockSpec((1, ts, 2 * W_BR), lambda b, s: (b, s, 0)),
        out_shape=jax.ShapeDtypeStruct((bsz, s_len, 2 * W_BR), MM_DTYPE),
        scratch_shapes=[pltpu.VMEM((ext_rows + 8, W_BR), F32), pltpu.VMEM((7, ext_rows, W_BR), F32),
                        pltpu.VMEM((ts, W_BR), F32)],
        compiler_params=_cparams(("parallel", "parallel")),
    )(z3, z3, cw, vec, ws, bsf)


def _even_bwd(z3, dy3, cw, vec, ws, wst, bsf):
    bsz, s_len, _ = z3.shape
    ts = min(SEQ_TILE, s_len)
    n_s = s_len // ts
    ext_rows = ts + 2 * HALO
    a_rows = ts + HALO

    def body(z_ref, zl_ref, zr_ref, dy_ref, dyr_ref, cw_ref, vec_ref, ws_ref, wst_ref, bsf_ref,
             dz_ref, dbin_ref, dcw_ref, dvec_ref, dws_ref, dbsf_ref,
             ext_ref, sh_ref, a1_ref, ag_ref, dya_ref, d_ref, accw_ref):
        b = pl.program_id(0)
        s = pl.program_id(1)

        @pl.when((b == 0) & (s == 0))
        def _():
            dbin_ref[...] = jnp.zeros_like(dbin_ref)
            dcw_ref[...] = jnp.zeros_like(dcw_ref)
            dvec_ref[...] = jnp.zeros_like(dvec_ref)
            dws_ref[...] = jnp.zeros_like(dws_ref)
            dbsf_ref[...] = jnp.zeros_like(dbsf_ref)

        has_right = s < n_s - 1
        hl = zl_ref[0]
        hr = zr_ref[0]
        ext_ref[0:HALO, :] = jnp.where(s > 0, hl[:, 0:W_BR] * _sigmoid(hl[:, W_BR:2 * W_BR]), 0.0)
        ext_ref[HALO:HALO + ts, :] = z_ref[0, :, 0:W_BR] * _sigmoid(z_ref[0, :, W_BR:2 * W_BR])
        ext_ref[HALO + ts:ext_rows, :] = hr[:, 0:W_BR] * _sigmoid(hr[:, W_BR:2 * W_BR])
        ext_ref[ext_rows:ext_rows + 8, :] = jnp.zeros((8, W_BR), F32)
        ag_ref[0:ts, :] = z_ref[0, :, 2 * W_BR:3 * W_BR]
        ag_ref[ts:a_rows, :] = hr[:, 2 * W_BR:3 * W_BR]
        dya_ref[0:ts, :] = dy_ref[0, :, 0:W_BR]
        dya_ref[ts:a_rows, :] = dyr_ref[0][:, 0:W_BR]
        _build_shifts(ext_ref, sh_ref, ext_rows)

        def conv_chunk(ci, carry):
            base = pl.multiple_of(ci * CONV_CHUNK, CONV_CHUNK)
            acc = jnp.zeros((CONV_CHUNK, W_BR), F32) + vec_ref[0:1, :]
            for k in range(KA):
                q, r = divmod(2 + k, 8)
                acc = acc + _shifted(ext_ref, sh_ref, r, base + 8 * q, CONV_CHUNK) * cw_ref[k:k + 1, :]
            a1_ref[pl.ds(base, CONV_CHUNK), :] = acc
            return carry

        lax.fori_loop(0, a_rows // CONV_CHUNK, conv_chunk, 0)

        def a_chunk(base, n, main):
            rows = pl.ds(base, n)
            xh, rs = _ln_stats(a1_ref[rows, :])
            ln = xh * vec_ref[1:2, :] + vec_ref[2:3, :]
            sl, dsl = _silu_grad(ln)
            sgt, dsgt = _silu_grad(ag_ref[rows, :])
            dya = dya_ref[rows, :]
            dln = dya * sgt * dsl
            da1 = _ln_bwd(dln * vec_ref[1:2, :], xh, rs)
            if main:
                d_ref[rows, :] = da1
                dag = dya * sl * dsgt
                dz_ref[0, rows, 2 * W_BR:3 * W_BR] = dag.astype(dz_ref.dtype)
                dbin_ref[0:1, 2 * W_BR:3 * W_BR] += _rowsum(dag)
                dvec_ref[0:1, :] += _rowsum(da1)
                dvec_ref[1:2, :] += _rowsum(dln * xh)
                dvec_ref[2:3, :] += _rowsum(dln)
            else:
                d_ref[rows, :] = jnp.where(has_right, da1, 0.0)

        def a_main(ci, carry):
            a_chunk(pl.multiple_of(ci * GBLK, GBLK), GBLK, True)
            return carry

        lax.fori_loop(0, ts // GBLK, a_main, 0)
        a_chunk(ts, HALO, False)
        d_ref[a_rows:a_rows + 8, :] = jnp.zeros((8, W_BR), F32)

        accw_ref[...] = jnp.zeros_like(accw_ref)

        def dw_chunk(ci, carry):
            base = pl.multiple_of(ci * CONV_CHUNK, CONV_CHUNK)
            d = d_ref[pl.ds(base, CONV_CHUNK), :]
            for k in range(KA):
                q, r = divmod(2 + k, 8)
                prod = d * _shifted(ext_ref, sh_ref, r, base + 8 * q, CONV_CHUNK)
                accw_ref[k] += jnp.sum(prod.reshape(CONV_CHUNK // 8, 8, W_BR), axis=0)
            return carry

        lax.fori_loop(0, ts // CONV_CHUNK, dw_chunk, 0)
        dcw_ref[...] += jnp.sum(accw_ref[...], axis=1)

        _build_shifts(d_ref, sh_ref, a_rows)

        def dx_chunk(ci, carry):
            base = pl.multiple_of(ci * CONV_CHUNK, CONV_CHUNK)
            rows = pl.ds(base, CONV_CHUNK)
            acc = jnp.zeros((CONV_CHUNK, W_BR), F32)
            for m in range(KA):
                q, r = divmod(m, 8)
                acc = acc + _shifted(d_ref, sh_ref, r, base + 8 * q, CONV_CHUNK) * cw_ref[KA - 1 - m:KA - m, :]
            aval = z_ref[0, rows, 0:W_BR]
            sg = _sigmoid(z_ref[0, rows, W_BR:2 * W_BR])
            dval = acc * sg
            dglu = acc * aval * sg * (1.0 - sg)
            dz_ref[0, rows, 0:W_BR] = dval.astype(dz_ref.dtype)
            dz_ref[0, rows, W_BR:2 * W_BR] = dglu.astype(dz_ref.dtype)
            dbin_ref[0:1, 0:W_BR] += _rowsum(dval)
            dbin_ref[0:1, W_BR:2 * W_BR] += _rowsum(dglu)
            return carry

        lax.fori_loop(0, ts // CONV_CHUNK, dx_chunk, 0)

        keep, keep_t = _tril_masks()
        lane = lax.broadcasted_iota(jnp.int32, (GBLK, GBLK), 1)

        def block(bi, carry):
            rows = pl.ds(pl.multiple_of(bi * GBLK, GBLK), GBLK)
            ua, dua = _gelu_grad(z_ref[0, rows, 3 * W_BR:4 * W_BR])
            va, dva = _gelu_grad(z_ref[0, rows, 4 * W_BR:5 * W_BR])
            sgt, dsgt = _silu_grad(z_ref[0, rows, 5 * W_BR:6 * W_BR])
            vh, rs = _ln_stats(va)
            vlb = _mm(vh * vec_ref[3:4, :] + vec_ref[4:5, :])
            sg = _spatial(ws_ref, keep, vlb) + bsf_ref[...]
            dyg = dy_ref[0, rows, W_BR:2 * W_BR]
            du = dyg * sg * sgt * dua
            dsg = dyg * ua * sgt
            dgg = dyg * ua * sg * dsgt
            dvl = _spatial(wst_ref, keep_t, _mm(dsg))
            for p in range(4):
                dsp = dsg[:, p * GBLK:(p + 1) * GBLK]
                vlp = vlb[:, p * GBLK:(p + 1) * GBLK]
                dws_ref[2 * p] += jnp.where(keep, _dot_nt(jnp.where(lane < 64, dsp, 0.0), vlp), 0.0)
                dws_ref[2 * p + 1] += jnp.where(keep, _dot_nt(jnp.where(lane >= 64, dsp, 0.0), vlp), 0.0)
            dbsf_ref[...] += dsg
            dvec_ref[3:4, :] += _rowsum(dvl * vh)
            dvec_ref[4:5, :] += _rowsum(dvl)
            dv = _ln_bwd(dvl * vec_ref[3:4, :], vh, rs) * dva
            dz_ref[0, rows, 3 * W_BR:4 * W_BR] = du.astype(dz_ref.dtype)
            dz_ref[0, rows, 4 * W_BR:5 * W_BR] = dv.astype(dz_ref.dtype)
            dz_ref[0, rows, 5 * W_BR:6 * W_BR] = dgg.astype(dz_ref.dtype)
            dbin_ref[0:1, 3 * W_BR:4 * W_BR] += _rowsum(du)
            dbin_ref[0:1, 4 * W_BR:5 * W_BR] += _rowsum(dv)
            dbin_ref[0:1, 5 * W_BR:6 * W_BR] += _rowsum(dgg)
            return carry

        lax.fori_loop(0, ts // GBLK, block, 0)

    full = lambda shape: pl.BlockSpec(shape, lambda b, s: (0,) * len(shape))
    acc_shapes = [(1, N_COLS), (32, W_BR), (8, W_BR), (8, GBLK, GBLK), (GBLK, W_BR)]
    return pl.pallas_call(
        body, name="even_bwd", grid=(bsz, n_s),
        in_specs=[pl.BlockSpec((1, ts, N_COLS), lambda b, s: (b, s, 0)),
                  _halo_specs(ts, s_len, N_COLS, True), _halo_specs(ts, s_len, N_COLS, False),
                  pl.BlockSpec((1, ts, 2 * W_BR), lambda b, s: (b, s, 0)), _halo_specs(ts, s_len, 2 * W_BR, False),
                  full((32, W_BR)), full((8, W_BR)), full((8, GBLK, GBLK)), full((8, GBLK, GBLK)), full((GBLK, W_BR))],
        out_specs=[pl.BlockSpec((1, ts, N_COLS), lambda b, s: (b, s, 0))] + [full(sh) for sh in acc_shapes],
        out_shape=[jax.ShapeDtypeStruct((bsz, s_len, N_COLS), MM_DTYPE)] + [jax.ShapeDtypeStruct(sh, F32) for sh in acc_shapes],
        scratch_shapes=[pltpu.VMEM((ext_rows + 8, W_BR), F32), pltpu.VMEM((7, ext_rows, W_BR), F32),
                        pltpu.VMEM((a_rows, W_BR), F32), pltpu.VMEM((a_rows, W_BR), F32), pltpu.VMEM((a_rows, W_BR), F32),
                        pltpu.VMEM((a_rows + 8, W_BR), F32), pltpu.VMEM((32, 8, W_BR), F32)],
        compiler_params=_cparams(("arbitrary", "arbitrary")),
    )(z3, z3, z3, dy3, dy3, cw, vec, ws, wst, bsf)


def _pool_stages(e_refs, rows):
    e0, e1, e2, e3, e4 = e_refs
    e1[8:rows, :] = e0[8:rows, :] + e0[7:rows - 1, :]
    e2[16:rows, GBLK:] = e1[16:rows, GBLK:] + e1[14:rows - 2, GBLK:]
    e3[24:rows, 2 * GBLK:] = e2[24:rows, 2 * GBLK:] + e2[20:rows - 4, 2 * GBLK:]
    e4[32:rows, 3 * GBLK:] = e3[32:rows, 3 * GBLK:] + e3[24:rows - 8, 3 * GBLK:]


def _pool_counts(start, n):
    pos = (start + 1 + lax.broadcasted_iota(jnp.int32, (n, 1), 0)).astype(F32)
    return [jnp.minimum(pos, float(w)) for w in POOL_WINDOWS]


def _pooled_into(e_refs, pooled_ref, s, ts):
    cnt = _pool_counts(s * ts, ts)
    for g in range(4):
        cs = slice(g * GBLK, (g + 1) * GBLK)
        pooled_ref[:, cs] = e_refs[g + 1][HALO:HALO + ts, cs] / cnt[g] - e_refs[0][HALO:HALO + ts, cs]


def _odd_fwd(z3, wp, vec):
    bsz, s_len, _ = z3.shape
    ts = min(SEQ_TILE, s_len)
    ext_rows = ts + HALO

    def body(z_ref, zl_ref, wp_ref, vec_ref, y_ref, e0, e1, e2, e3, e4, pooled_ref, dext_ref, ec_ref):
        s = pl.program_id(1)
        hl = zl_ref[0]
        e0[0:HALO, :] = jnp.where(s > 0, hl[:, 0:W_BR], 0.0)
        e0[HALO:ext_rows, :] = z_ref[0, :, 0:W_BR]
        _pool_stages((e0, e1, e2, e3, e4), ext_rows)
        _pooled_into((e0, e1, e2, e3, e4), pooled_ref, s, ts)
        dext_ref[0:HALO, :] = jnp.where(s > 0, hl[:, 2 * W_BR:3 * W_BR] * hl[:, 4 * W_BR:5 * W_BR], 0.0)
        dext_ref[HALO:ext_rows, :] = z_ref[0, :, 2 * W_BR:3 * W_BR] * z_ref[0, :, 4 * W_BR:5 * W_BR]
        ec_ref[...] = (vec_ref[1:2, :] * dext_ref[HALO - 2:HALO - 2 + ts, :] + vec_ref[2:3, :] * dext_ref[HALO - 1:HALO - 1 + ts, :]
                       + vec_ref[3:4, :] * dext_ref[HALO:HALO + ts, :])

        def block(bi, carry):
            rows = pl.ds(pl.multiple_of(bi * GBLK, GBLK), GBLK)
            pb = _mm(pooled_ref[rows, :])
            cpre = jnp.concatenate([jnp.dot(pb[:, g * GBLK:(g + 1) * GBLK], wp_ref[g], preferred_element_type=F32)
                                    for g in range(4)], axis=1)
            c = cpre * vec_ref[0:1, :] * _silu(z_ref[0, rows, W_BR:2 * W_BR])
            d = z_ref[0, rows, 3 * W_BR:4 * W_BR] * ec_ref[rows, :] * _silu(z_ref[0, rows, 5 * W_BR:6 * W_BR])
            y_ref[0, rows, 0:W_BR] = c.astype(y_ref.dtype)
            y_ref[0, rows, W_BR:2 * W_BR] = d.astype(y_ref.dtype)
            return carry

        lax.fori_loop(0, ts // GBLK, block, 0)

    full = lambda shape: pl.BlockSpec(shape, lambda b, s: (0,) * len(shape))
    ebuf = pltpu.VMEM((ext_rows, W_BR), F32)
    return pl.pallas_call(
        body, name="odd_fwd", grid=(bsz, s_len // ts),
        in_specs=[pl.BlockSpec((1, ts, N_COLS), lambda b, s: (b, s, 0)), _halo_specs(ts, s_len, N_COLS),
                  full((4, GBLK, GBLK)), full((8, W_BR))],
        out_specs=pl.BlockSpec((1, ts, 2 * W_BR), lambda b, s: (b, s, 0)),
        out_shape=jax.ShapeDtypeStruct((bsz, s_len, 2 * W_BR), MM_DTYPE),
        scratch_shapes=[ebuf, ebuf, ebuf, ebuf, ebuf, pltpu.VMEM((ts, W_BR), F32), ebuf, pltpu.VMEM((ts, W_BR), F32)],
        compiler_params=_cparams(("parallel", "parallel")),
    )(z3, z3, wp, vec)


def _odd_bwd(z3, dy3, wp, vec):
    bsz, s_len, _ = z3.shape
    ts = min(SEQ_TILE, s_len)
    n_s = s_len // ts
    ext_rows = ts + HALO

    def body(z_ref, zl_ref, zr_ref, dy_ref, dyr_ref, wp_ref, vec_ref,
             dz_ref, dbin_ref, dwp_ref, dvec_ref,
             e0, e1, e2, e3, e4, pooled_ref, dext_ref, ec_ref, q_ref, dp_ref, de_ref, f1, f2, f3, f4):
        b = pl.program_id(0)
        s = pl.program_id(1)

        @pl.when((b == 0) & (s == 0))
        def _():
            dbin_ref[...] = jnp.zeros_like(dbin_ref)
            dwp_ref[...] = jnp.zeros_like(dwp_ref)
            dvec_ref[...] = jnp.zeros_like(dvec_ref)

        has_right = s < n_s - 1
        hl = zl_ref[0]
        e0[0:HALO, :] = jnp.where(s > 0, hl[:, 0:W_BR], 0.0)
        e0[HALO:ext_rows, :] = z_ref[0, :, 0:W_BR]
        _pool_stages((e0, e1, e2, e3, e4), ext_rows)
        _pooled_into((e0, e1, e2, e3, e4), pooled_ref, s, ts)
        dext_ref[0:HALO, :] = jnp.where(s > 0, hl[:, 2 * W_BR:3 * W_BR] * hl[:, 4 * W_BR:5 * W_BR], 0.0)
        dext_ref[HALO:ext_rows, :] = z_ref[0, :, 2 * W_BR:3 * W_BR] * z_ref[0, :, 4 * W_BR:5 * W_BR]
        ec_ref[...] = (vec_ref[1:2, :] * dext_ref[HALO - 2:HALO - 2 + ts, :] + vec_ref[2:3, :] * dext_ref[HALO - 1:HALO - 1 + ts, :]
                       + vec_ref[3:4, :] * dext_ref[HALO:HALO + ts, :])

        def grads(zc_gate, zd_b, zd_gate, dyc, dyd, rows_out, n, start, valid):
            sgt = _silu(zc_gate)
            dcpre = dyc * vec_ref[0:1, :] * sgt
            db = _mm(dcpre)
            dpool = jnp.concatenate([_dot_nt(db[:, g * GBLK:(g + 1) * GBLK], wp_ref[g]) for g in range(4)], axis=1)
            cnt = _pool_counts(start, n)
            q = jnp.concatenate([dpool[:, g * GBLK:(g + 1) * GBLK] / cnt[g] for g in range(4)], axis=1)
            de = dyd * zd_b * _silu(zd_gate)
            if valid is not None:
                q = jnp.where(valid, q, 0.0)
                de = jnp.where(valid, de, 0.0)
            q_ref[rows_out, :] = q
            dp_ref[rows_out, :] = dpool
            de_ref[rows_out, :] = de
            return dcpre

        def block(bi, carry):
            base = pl.multiple_of(bi * GBLK, GBLK)
            rows = pl.ds(base, GBLK)
            cg = z_ref[0, rows, W_BR:2 * W_BR]
            dyc = dy_ref[0, rows, 0:W_BR]
            dyd = dy_ref[0, rows, W_BR:2 * W_BR]
            d_b = z_ref[0, rows, 3 * W_BR:4 * W_BR]
            d_gate = z_ref[0, rows, 5 * W_BR:6 * W_BR]
            dcpre = grads(cg, d_b, d_gate, dyc, dyd, rows, GBLK, s * ts + base, None)
            pb = _mm(pooled_ref[rows, :])
            dcb = _mm(dcpre)
            cpre = jnp.concatenate([jnp.dot(pb[:, g * GBLK:(g + 1) * GBLK], wp_ref[g], preferred_element_type=F32)
                                    for g in range(4)], axis=1)
            for g in range(4):
                cs = slice(g * GBLK, (g + 1) * GBLK)
                dwp_ref[g] += _dot_tn(pb[:, cs], dcb[:, cs])
            sgt, dsgt = _silu_grad(cg)
            dvec_ref[0:1, :] += _rowsum(dyc * cpre * sgt)
            dcg = dyc * cpre * vec_ref[0:1, :] * dsgt
            sdt, dsdt = _silu_grad(d_gate)
            ec = ec_ref[rows, :]
            ddb = dyd * ec * sdt
            ddg = dyd * d_b * ec * dsdt
            dz_ref[0, rows, W_BR:2 * W_BR] = dcg.astype(dz_ref.dtype)
            dz_ref[0, rows, 3 * W_BR:4 * W_BR] = ddb.astype(dz_ref.dtype)
            dz_ref[0, rows, 5 * W_BR:6 * W_BR] = ddg.astype(dz_ref.dtype)
            dbin_ref[0:1, W_BR:2 * W_BR] += _rowsum(dcg)
            dbin_ref[0:1, 3 * W_BR:4 * W_BR] += _rowsum(ddb)
            dbin_ref[0:1, 5 * W_BR:6 * W_BR] += _rowsum(ddg)
            return carry

        lax.fori_loop(0, ts // GBLK, block, 0)
        hr = zr_ref[0]
        dyr = dyr_ref[0]
        grads(hr[:, W_BR:2 * W_BR], hr[:, 3 * W_BR:4 * W_BR], hr[:, 5 * W_BR:6 * W_BR], dyr[:, 0:W_BR], dyr[:, W_BR:2 * W_BR],
              slice(ts, ext_rows), HALO, (s + 1) * ts, has_right)

        r1, r2, r3 = ts + 24, ts + 16, ts + 8
        f1[0:r1, :] = q_ref[0:r1, :] + q_ref[1:r1 + 1, :]
        f2[0:r2, GBLK:] = f1[0:r2, GBLK:] + f1[2:r2 + 2, GBLK:]
        f3[0:r3, 2 * GBLK:] = f2[0:r3, 2 * GBLK:] + f2[4:r3 + 4, 2 * GBLK:]
        f4[0:ts, 3 * GBLK:] = f3[0:ts, 3 * GBLK:] + f3[8:ts + 8, 3 * GBLK:]
        for g, f in enumerate((f1, f2, f3, f4)):
            cs = slice(g * GBLK, (g + 1) * GBLK)
            dvg = f[0:ts, cs] - dp_ref[0:ts, cs]
            dz_ref[0, :, cs] = dvg.astype(dz_ref.dtype)
            dbin_ref[0:1, cs] += _rowsum(dvg)

        ddc = (vec_ref[1:2, :] * de_ref[2:ts + 2, :] + vec_ref[2:3, :] * de_ref[1:ts + 1, :] + vec_ref[3:4, :] * de_ref[0:ts, :])
        d_h = z_ref[0, :, 2 * W_BR:3 * W_BR]
        d_c = z_ref[0, :, 4 * W_BR:5 * W_BR]
        ddh = ddc * d_c
        ddcc = ddc * d_h
        dz_ref[0, :, 2 * W_BR:3 * W_BR] = ddh.astype(dz_ref.dtype)
        dz_ref[0, :, 4 * W_BR:5 * W_BR] = ddcc.astype(dz_ref.dtype)
        dbin_ref[0:1, 2 * W_BR:3 * W_BR] += _rowsum(ddh)
        dbin_ref[0:1, 4 * W_BR:5 * W_BR] += _rowsum(ddcc)
        de = de_ref[0:ts, :]
        for k in range(3):
            dvec_ref[1 + k:2 + k, :] += _rowsum(de * dext_ref[HALO - 2 + k:HALO - 2 + k + ts, :])

    full = lambda shape: pl.BlockSpec(shape, lambda b, s: (0,) * len(shape))
    acc_shapes = [(1, N_COLS), (4, GBLK, GBLK), (8, W_BR)]
    ebuf = pltpu.VMEM((ext_rows, W_BR), F32)
    tbuf = pltpu.VMEM((ts, W_BR), F32)
    return pl.pallas_call(
        body, name="odd_bwd", grid=(bsz, n_s),
        in_specs=[pl.BlockSpec((1, ts, N_COLS), lambda b, s: (b, s, 0)),
                  _halo_specs(ts, s_len, N_COLS, True), _halo_specs(ts, s_len, N_COLS, False),
                  pl.BlockSpec((1, ts, 2 * W_BR), lambda b, s: (b, s, 0)), _halo_specs(ts, s_len, 2 * W_BR, False),
                  full((4, GBLK, GBLK)), full((8, W_BR))],
        out_specs=[pl.BlockSpec((1, ts, N_COLS), lambda b, s: (b, s, 0))] + [full(sh) for sh in acc_shapes],
        out_shape=[jax.ShapeDtypeStruct((bsz, s_len, N_COLS), MM_DTYPE)] + [jax.ShapeDtypeStruct(sh, F32) for sh in acc_shapes],
        scratch_shapes=[ebuf, ebuf, ebuf, ebuf, ebuf, tbuf, ebuf, tbuf, ebuf, ebuf, ebuf, ebuf, ebuf, ebuf, tbuf],
        compiler_params=_cparams(("arbitrary", "arbitrary")),
    )(z3, z3, z3, dy3, dy3, wp, vec)


def _post_fwd(y2, x2, p2, w_out, wg, wple, vec, tgt=None):
    t, d = x2.shape
    tm = min(ROW_TILE, t)
    last = tgt is not None

    def body(*refs):
        y_ref, x_ref, p_ref, wo_ref, wg_ref, wp_ref, vec_ref = refs[:7]
        xn_ref, r_ref, gate_ref = refs[7 + last:10 + last]
        r = ALPHA * x_ref[...] + jnp.dot(y_ref[...], wo_ref[...], preferred_element_type=F32) + vec_ref[0:1, :]
        r_ref[...] = r
        xh, _ = _ln_stats(r)
        h = xh * vec_ref[1:2, :] + vec_ref[2:3, :]
        gate = _sigmoid(_dot(h, wg_ref[...]) + vec_ref[3:4, :])
        gate_ref[...] = gate
        xn = h + gate * _dot(p_ref[...], wp_ref[...])
        if last:
            sq_ref = refs[11]

            @pl.when(pl.program_id(0) == 0)
            def _():
                sq_ref[...] = jnp.zeros_like(sq_ref)
            e = xn - refs[7][...]
            xn_ref[...] = e / float(d)
            sq_ref[...] += _rowsum(e * e)
        else:
            xn_ref[...] = xn

    row = lambda c: pl.BlockSpec((tm, c), lambda i: (i, 0))
    full = lambda shape: pl.BlockSpec(shape, lambda i: (0,) * len(shape))
    return pl.pallas_call(
        body, name="post_fwd_loss" if last else "post_fwd", grid=(t // tm,),
        in_specs=[row(d), row(d), row(D_PLE), full((d, d)), full((d, d)), full((D_PLE, d)), full((8, d))] + [row(d)] * last,
        out_specs=[row(d), row(d), row(d)] + [full((1, d))] * last,
        out_shape=[jax.ShapeDtypeStruct((t, d), F32)] * 3 + [jax.ShapeDtypeStruct((1, d), F32)] * last,
        compiler_params=_cparams(("arbitrary",) if last else ("parallel",)),
    )(y2, x2, p2, w_out, wg, wple, vec, *([tgt] if last else []))


def _post_bwd(dxn, r2, gate2, p2, y2, w_out, wg, wple, vec):
    t, d = r2.shape
    tm = min(BWD_ROW_TILE, t)

    def body(dxn_ref, r_ref, gate_ref, p_ref, y_ref, wo_ref, wg_ref, wp_ref, vec_ref,
             dxr_ref, dy_ref, dwo_ref, dwg_ref, dwp_ref, dvec_ref):
        @pl.when(pl.program_id(0) == 0)
        def _():
            dwo_ref[...] = jnp.zeros_like(dwo_ref)
            dwg_ref[...] = jnp.zeros_like(dwg_ref)
            dwp_ref[...] = jnp.zeros_like(dwp_ref)
            dvec_ref[...] = jnp.zeros_like(dvec_ref)

        dxn = dxn_ref[...]
        gate = gate_ref[...]
        xh, rs = _ln_stats(r_ref[...])
        hb = _mm(xh * vec_ref[1:2, :] + vec_ref[2:3, :])
        pb = _mm(p_ref[...])
        pe = jnp.dot(pb, wp_ref[...], preferred_element_type=F32)
        dpre = dxn * pe * gate * (1.0 - gate)
        dpb = _mm(dpre)
        dh = dxn + _dot_nt(dpb, wg_ref[...])
        dwg_ref[...] += _dot_tn(hb, dpb)
        dwp_ref[...] += _dot_tn(pb, dxn * gate)
        dr = _ln_bwd(dh * vec_ref[1:2, :], xh, rs)
        drb = _mm(dr)
        dxr_ref[...] = ALPHA * dr
        dy_ref[...] = _dot_nt(drb, wo_ref[...])
        dwo_ref[...] += _dot_tn(y_ref[...], drb)
        dvec_ref[0:1, :] += _rowsum(dr)
        dvec_ref[1:2, :] += _rowsum(dh * xh)
        dvec_ref[2:3, :] += _rowsum(dh)
        dvec_ref[3:4, :] += _rowsum(dpre)

    row = lambda c: pl.BlockSpec((tm, c), lambda i: (i, 0))
    full = lambda shape: pl.BlockSpec(shape, lambda i: (0,) * len(shape))
    acc_shapes = [(d, d), (d, d), (D_PLE, d), (8, d)]
    return pl.pallas_call(
        body, name="post_bwd", grid=(t // tm,),
        in_specs=[row(d), row(d), row(d), row(D_PLE), row(d), full((d, d)), full((d, d)), full((D_PLE, d)), full((8, d))],
        out_specs=[row(d), row(d)] + [full(sh) for sh in acc_shapes],
        out_shape=[jax.ShapeDtypeStruct((t, d), F32)] * 2 + [jax.ShapeDtypeStruct(sh, F32) for sh in acc_shapes],
        compiler_params=_cparams(("arbitrary",)),
    )(dxn, r2, gate2, p2, y2, w_out, wg, wple, vec)


def _place():
    x, y, c = lax.axis_index("x"), lax.axis_index("y"), lax.axis_index("c")
    chips = [(1 - x, y), (x, 1 - y), (1 - x, 1 - y)]
    return x, y, c, chips


def _shard_of(ref, ax, k, width, lo=None, ln=None):
    idx = [slice(None)] * 3
    if lo is not None:
        idx[0] = pl.ds(lo, ln)
    idx[ax] = pl.ds(k * width, width)
    return ref.at[tuple(idx)]


def _remote(src, dst, ssem, rsem, dev):
    return pltpu.make_async_remote_copy(src_ref=src, dst_ref=dst, send_sem=ssem, recv_sem=rsem, device_id=dev, device_id_type=MESH)


def _place_shard(w, ax, chip):
    l_dim, a_dim, b_dim = w.shape
    tr = min(256, a_dim)
    per = a_dim // tr
    shape = list(w.shape)
    shape[ax] *= 4
    if ax == 2:
        out_spec = pl.BlockSpec((1, tr, b_dim), lambda l, i, k: (l, i, k[0]))
    else:
        out_spec = pl.BlockSpec((1, tr, b_dim), lambda l, i, k: (l, k[0] * per + i, 0))

    def body(k_ref, w_ref, o_ref):
        o_ref[...] = w_ref[...].astype(o_ref.dtype)

    return pl.pallas_call(
        body, name="place_shard",
        grid_spec=pltpu.PrefetchScalarGridSpec(
            num_scalar_prefetch=1, grid=(l_dim, per),
            in_specs=[pl.BlockSpec((1, tr, b_dim), lambda l, i, k: (l, i, 0))], out_specs=out_spec),
        out_shape=jax.ShapeDtypeStruct(tuple(shape), MM_DTYPE),
        compiler_params=_cparams(("parallel", "parallel")),
    )(chip, w)


def _shard_copies(src_refs, dst_refs, axes, sems):
    x, y, c, chips = _place()
    j = 2 * x + y
    cps = []
    for a, (s_ref, d_ref) in enumerate(zip(src_refs, dst_refs)):
        w = d_ref.shape[axes[a]] // 4
        for q, (qx, qy) in enumerate(chips):
            ssem, rsem = sems[3 * a + q]
            cps.append(_remote(_shard_of(s_ref, axes[a], j, w), _shard_of(d_ref, axes[a], j, w), ssem, rsem, (qx, qy, c)))
    return cps


def _gather_sync(fulls, axes, sv):
    n = len(fulls)

    def body(*refs):
        in_refs, sv_ref = refs[:n], refs[n]
        fu_refs, sva_ref = refs[n + 1:2 * n + 1], refs[2 * n + 1]
        ssem, rsem, lsem = refs[2 * n + 2:]
        x, y, c, chips = _place()
        j = 2 * x + y
        loc = pltpu.make_async_copy(sv_ref, sva_ref.at[j], lsem)
        loc.start()
        cps = _shard_copies(in_refs, fu_refs, axes, [(ssem.at[k], rsem.at[k]) for k in range(3 * n)])
        for q, (qx, qy) in enumerate(chips):
            cps.append(_remote(sv_ref, sva_ref.at[j], ssem.at[n * 3 + q], rsem.at[n * 3 + q], (qx, qy, c)))
        for cp in cps:
            cp.start()
        for cp in cps:
            cp.wait()
        loc.wait()

    nd = 3 * n + 3
    return pl.pallas_call(
        body, name="gather_sync",
        in_specs=[ANY] * (n + 1), out_specs=[ANY] * (n + 1),
        out_shape=[jax.ShapeDtypeStruct(f.shape, f.dtype) for f in fulls] + [jax.ShapeDtypeStruct((4,) + sv.shape, sv.dtype)],
        input_output_aliases={a: a for a in range(n)},
        scratch_shapes=[pltpu.SemaphoreType.DMA((nd,)), pltpu.SemaphoreType.DMA((nd,)), pltpu.SemaphoreType.DMA],
        compiler_params=pltpu.CompilerParams(has_side_effects=True),
    )(*fulls, sv)


def _chip_handshake():
    x, y, c, chips = _place()
    barrier = pltpu.get_barrier_semaphore()
    for qx, qy in chips:
        pl.semaphore_signal(barrier, inc=1, device_id=(qx, qy, c), device_id_type=MESH)
    pl.semaphore_wait(barrier, 3)


def _gather_async(name, collective_id, fulls, axes):
    n = len(fulls)
    refs = [jax.new_ref(f, memory_space=pltpu.MemorySpace.HBM) for f in fulls]

    @pl.kernel(mesh=plsc.ScalarSubcoreMesh(axis_name="seq", num_cores=1), name=name,
               scratch_types=(pltpu.SemaphoreType.DMA,) * (6 * n),
               compiler_params=pltpu.CompilerParams(collective_id=collective_id))
    def launch(*sems):
        _chip_handshake()
        cps = _shard_copies(refs, refs, axes, [(sems[2 * k], sems[2 * k + 1]) for k in range(3 * n)])
        for cp in cps:
            cp.start()
        for cp in cps:
            cp.wait()

    launch()
    return refs


def _pair_exchange(grads, small):
    n = len(grads)
    outs = [jax.ShapeDtypeStruct((g.shape[0] // 2,) + g.shape[1:], g.dtype) for g in grads]
    outs.append(jax.ShapeDtypeStruct((small.shape[0] // 2, small.shape[1]), small.dtype))

    def body(*refs):
        g_refs, o_refs = refs[:n + 1], refs[n + 1:2 * n + 2]
        ssem, rsem = refs[2 * n + 2:]
        x, y, c, _ = _place()
        cps = []
        for a in range(n + 1):
            lh = g_refs[a].shape[0] // 2
            cp = _remote(g_refs[a].at[pl.ds((1 - c) * lh, lh)], o_refs[a], ssem.at[a], rsem.at[a], (x, y, 1 - c))
            cp.start()
            cps.append(cp)
        for cp in cps:
            cp.wait()

    return pl.pallas_call(
        body, name="pair_exchange", in_specs=[ANY] * (n + 1), out_specs=[ANY] * (n + 1), out_shape=outs,
        scratch_shapes=[pltpu.SemaphoreType.DMA((n + 1,)), pltpu.SemaphoreType.DMA((n + 1,))],
        compiler_params=pltpu.CompilerParams(has_side_effects=True),
    )(*grads, small)


def _chip_scatter(sums, axes, small):
    n = len(sums)
    outs = []
    for g, ax in zip(sums, axes):
        sh = list(g.shape)
        sh[ax] //= 4
        outs.append(jax.ShapeDtypeStruct((3,) + tuple(sh), g.dtype))
    rq = small.shape[0] // 4
    outs.append(jax.ShapeDtypeStruct((3, rq, small.shape[1]), small.dtype))

    def body(*refs):
        g_refs, o_refs = refs[:n + 1], refs[n + 1:2 * n + 2]
        ssem, rsem = refs[2 * n + 2:]
        x, y, c, chips = _place()
        cps = []
        for a in range(n + 1):
            for q, (qx, qy) in enumerate(chips):
                k = 2 * qx + qy
                if a < n:
                    src = _shard_of(g_refs[a], axes[a], k, g_refs[a].shape[axes[a]] // 4)
                else:
                    src = g_refs[a].at[pl.ds(k * rq, rq)]
                cp = _remote(src, o_refs[a].at[q], ssem.at[a * 3 + q], rsem.at[a * 3 + q], (qx, qy, c))
                cp.start()
                cps.append(cp)
        for cp in cps:
            cp.wait()

    return pl.pallas_call(
        body, name="chip_scatter", in_specs=[ANY] * (n + 1), out_specs=[ANY] * (n + 1), out_shape=outs,
        scratch_shapes=[pltpu.SemaphoreType.DMA((3 * n + 3,)), pltpu.SemaphoreType.DMA((3 * n + 3,))],
        compiler_params=pltpu.CompilerParams(has_side_effects=True),
    )(*sums, small)


def _final_exchange(reds, small):
    n = len(reds)
    flips = [(fx, fy, fc) for fx in (0, 1) for fy in (0, 1) for fc in (0, 1)][1:]

    def body(*refs):
        g_refs, o_refs = refs[:n + 1], refs[n + 1:2 * n + 2]
        ssem, rsem = refs[2 * n + 2:]
        x, y, c, _ = _place()
        cps = []
        for a in range(n):
            lh = g_refs[a].shape[0] // 2
            cp = _remote(g_refs[a].at[pl.ds(c * lh, lh)], o_refs[a].at[pl.ds(c * lh, lh)], ssem.at[a], rsem.at[a], (x, y, 1 - c))
            cp.start()
            cps.append(cp)
        mine = 4 * c + 2 * x + y
        for f, (fx, fy, fc) in enumerate(flips):
            cp = _remote(g_refs[n].at[mine], o_refs[n].at[mine], ssem.at[n + f], rsem.at[n + f], (x ^ fx, y ^ fy, c ^ fc))
            cp.start()
            cps.append(cp)
        for cp in cps:
            cp.wait()

    return pl.pallas_call(
        body, name="final_exchange", in_specs=[ANY] * (n + 1), out_specs=[ANY] * (n + 1),
        out_shape=[jax.ShapeDtypeStruct(g.shape, g.dtype) for g in reds] + [jax.ShapeDtypeStruct(small.shape, small.dtype)],
        input_output_aliases={a: a for a in range(n + 1)},
        scratch_shapes=[pltpu.SemaphoreType.DMA((n + 7,)), pltpu.SemaphoreType.DMA((n + 7,))],
        compiler_params=pltpu.CompilerParams(has_side_effects=True),
    )(*reds, small)


def _pair_sum(g, got, half):
    lh, a_dim, b_dim = got.shape
    tr = min(256, a_dim)

    def body(half_ref, g_ref, o_ref, s_ref):
        s_ref[...] = (g_ref[...] + o_ref[...]).astype(s_ref.dtype)

    return pl.pallas_call(
        body, name="pair_sum",
        grid_spec=pltpu.PrefetchScalarGridSpec(
            num_scalar_prefetch=1, grid=(lh, a_dim // tr),
            in_specs=[pl.BlockSpec((1, tr, b_dim), lambda l, i, h: (h[0] * lh + l, i, 0)),
                      pl.BlockSpec((1, tr, b_dim), lambda l, i, h: (l, i, 0))],
            out_specs=pl.BlockSpec((1, tr, b_dim), lambda l, i, h: (l, i, 0))),
        out_shape=jax.ShapeDtypeStruct(got.shape, WIRE_DTYPE),
        compiler_params=_cparams(("parallel", "parallel")),
    )(half, g, got)


def _chip_sum(own, got, ax, pos):
    _, lh, a_dim, b_dim = got.shape
    tr = min(256, a_dim)
    if ax == 2:
        own_spec = pl.BlockSpec((1, tr, b_dim), lambda l, i, k: (l, i, k[0]))
    else:
        per = a_dim // tr
        own_spec = pl.BlockSpec((1, tr, b_dim), lambda l, i, k: (l, k[0] * per + i, 0))

    def body(k_ref, own_ref, got_ref, s_ref):
        s_ref[...] = ((own_ref[...].astype(F32) + got_ref[0].astype(F32)) + got_ref[1].astype(F32)) + got_ref[2].astype(F32)

    return pl.pallas_call(
        body, name="chip_sum",
        grid_spec=pltpu.PrefetchScalarGridSpec(
            num_scalar_prefetch=1, grid=(lh, a_dim // tr),
            in_specs=[own_spec, pl.BlockSpec((3, 1, tr, b_dim), lambda l, i, k: (0, l, i, 0))],
            out_specs=pl.BlockSpec((1, tr, b_dim), lambda l, i, k: (k[1] * lh + l, i, 0))),
        out_shape=jax.ShapeDtypeStruct((2 * lh, a_dim, b_dim), F32),
        compiler_params=_cparams(("parallel", "parallel")),
    )(pos, own, got)


def _small_pair_sum(small, got, half):
    rh = small.shape[0] // 2

    def body(h_ref, g_ref, o_ref, s_ref):
        s_ref[...] = g_ref[...] + o_ref[...]

    return pl.pallas_call(
        body, name="small_pair_sum",
        grid_spec=pltpu.PrefetchScalarGridSpec(
            num_scalar_prefetch=1, grid=(1,),
            in_specs=[pl.BlockSpec((rh, 128), lambda i, h: (h[0], 0)), pl.BlockSpec((rh, 128), lambda i, h: (0, 0))],
            out_specs=pl.BlockSpec((rh, 128), lambda i, h: (0, 0))),
        out_shape=jax.ShapeDtypeStruct((rh, 128), F32),
    )(half, small, got)


def _small_chip_sum(s1, got, pos):
    rq = got.shape[1]

    def body(k_ref, own_ref, got_ref, s_ref):
        s_ref[0] = ((own_ref[...] + got_ref[0]) + got_ref[1]) + got_ref[2]

    return pl.pallas_call(
        body, name="small_chip_sum",
        grid_spec=pltpu.PrefetchScalarGridSpec(
            num_scalar_prefetch=1, grid=(1,),
            in_specs=[pl.BlockSpec((rq, 128), lambda i, k: (k[0], 0)), pl.BlockSpec((3, rq, 128), lambda i, k: (0, 0, 0))],
            out_specs=pl.BlockSpec((1, rq, 128), lambda i, k: (4 * k[1] + k[0], 0, 0))),
        out_shape=jax.ShapeDtypeStruct((8, rq, 128), F32),
    )(pos, s1, got)


def _adam_math(w, g, m, v):
    m = ADAM_B1 * m + (1.0 - ADAM_B1) * g
    v = ADAM_B2 * v + (1.0 - ADAM_B2) * (g * g)
    m_hat = m / (1.0 - ADAM_B1 ** ADAM_STEP)
    v_hat = v / (1.0 - ADAM_B2 ** ADAM_STEP)
    return -ADAM_LR * (m_hat / (jnp.sqrt(v_hat) + ADAM_EPS) + ADAM_WD * w), m, v


def _adamw_big(w, g, m, v):
    l_dim, a_dim, b_dim = w.shape
    tr = min(256, a_dim)

    def body(w_ref, g_ref, m_ref, v_ref, d_ref, nm_ref, nv_ref):
        d_ref[...], nm_ref[...], nv_ref[...] = _adam_math(w_ref[...], g_ref[...], m_ref[...], v_ref[...])

    blk = pl.BlockSpec((1, tr, b_dim), lambda l, i: (l, i, 0))
    return pl.pallas_call(
        body, name="adamw_big", grid=(l_dim, a_dim // tr), in_specs=[blk] * 4, out_specs=[blk] * 3,
        out_shape=[jax.ShapeDtypeStruct(w.shape, F32)] * 3,
        compiler_params=_cparams(("parallel", "parallel")),
    )(w, g, m, v)


def _adamw_small(ws, gs, ms, vs):
    n = len(ws)

    def body(*refs):
        for i in range(n):
            w_ref, g_ref, m_ref, v_ref = refs[i], refs[n + i], refs[2 * n + i], refs[3 * n + i]
            d_ref, nm_ref, nv_ref = refs[4 * n + i], refs[5 * n + i], refs[6 * n + i]
            d_ref[...], nm_ref[...], nv_ref[...] = _adam_math(w_ref[...], g_ref[...], m_ref[...], v_ref[...])

    shapes = [jax.ShapeDtypeStruct(w.shape, F32) for w in ws]
    outs = pl.pallas_call(body, name="adamw_small", out_shape=shapes * 3,
                          compiler_params=_cparams())(*ws, *gs, *ms, *vs)
    return outs[:n], outs[n:2 * n], outs[2 * n:]


def _pack(arrs, row_mult):
    parts = []
    for a in arrs:
        flat = a.reshape(-1)
        pad = (-flat.shape[0]) % 1024
        parts.append(jnp.pad(flat, (0, pad)).reshape(-1, 128))
    buf = jnp.concatenate(parts, axis=0)
    pad = (-buf.shape[0]) % row_mult
    return jnp.pad(buf, ((0, pad), (0, 0)))


def _unpack(buf, shapes):
    out, row = [], 0
    for sh in shapes:
        n = math.prod(sh)
        rows = -(-n // 1024) * 8
        out.append(buf[row:row + rows].reshape(-1)[:n].reshape(sh))
        row += rows
    return out


_NAMES = ['w_in_e', 'b_in_e', 'conv_a_w', 'conv_a_b', 'ln_a_g', 'ln_a_b', 'ln_v_g', 'ln_v_b', 'w_s', 'b_s', 'w_out_e', 'b_out_e',
          'w_in_o', 'b_in_o', 'w_pool', 'pool_scale', 'conv_d_w', 'w_out_o', 'b_out_o', 'ln_g', 'ln_b', 'w_ple', 'w_ple_gate',
          'b_ple_gate']
_BIG = ['w_in_e', 'w_out_e', 'w_in_o', 'w_out_o', 'w_ple', 'w_ple_gate']
_BIG_AXES = [2, 1, 2, 1, 2, 1]
_SMALL_SHARDED = ['conv_a_w', 'b_in_o', 'pool_scale', 'conv_d_w', 'b_out_o']


def kernel(x, p, w_in_e, b_in_e, conv_a_w, conv_a_b, ln_a_g, ln_a_b, ln_v_g, ln_v_b, w_s, b_s, w_out_e, b_out_e, w_in_o, b_in_o, w_pool, pool_scale, conv_d_w, w_out_o, b_out_o, ln_g, ln_b, w_ple, w_ple_gate, b_ple_gate, loss_target, m_w_in_e, m_b_in_e, m_conv_a_w, m_conv_a_b, m_ln_a_g, m_ln_a_b, m_ln_v_g, m_ln_v_b, m_w_s, m_b_s, m_w_out_e, m_b_out_e, m_w_in_o, m_b_in_o, m_w_pool, m_pool_scale, m_conv_d_w, m_w_out_o, m_b_out_o, m_ln_g, m_ln_b, m_w_ple, m_w_ple_gate, m_b_ple_gate, v_w_in_e, v_b_in_e, v_conv_a_w, v_conv_a_b, v_ln_a_g, v_ln_a_b, v_ln_v_g, v_ln_v_b, v_w_s, v_b_s, v_w_out_e, v_b_out_e, v_w_in_o, v_b_in_o, v_w_pool, v_pool_scale, v_conv_d_w, v_w_out_o, v_b_out_o, v_ln_g, v_ln_b, v_w_ple, v_w_ple_gate, v_b_ple_gate):
    args = locals()
    wts = {n: args[n] for n in _NAMES}
    mom = {n: args["m_" + n] for n in _NAMES}
    var = {n: args["v_" + n] for n in _NAMES}
    bsz, s_len, d = x.shape
    t = bsz * s_len
    cx, cy, cc = lax.axis_index("x"), lax.axis_index("y"), lax.axis_index("c")
    chip = (2 * cx + cy).astype(jnp.int32).reshape(1)
    half = cc.astype(jnp.int32).reshape(1)
    pos = jnp.concatenate([chip, half])

    def placed(name, ax, layer):
        return _place_shard(wts[name][layer:layer + 1], ax, chip)

    sv = _pack([wts[n] for n in _SMALL_SHARDED], 8)
    w_in_first, sv_all = _gather_sync([placed('w_in_e', 2, 0)], [2], sv)
    layer_refs = []
    for i in range(DEPTH):
        sfx = '_e' if i % 2 == 0 else '_o'
        items = [('w_out' + sfx, 1, i // 2), ('w_ple_gate', 1, i), ('w_ple', 2, i)] + ([('w_in' + sfx, 2, i // 2)] if i else [])
        layer_refs.append(_gather_async("gather_layer%d" % i, i, [placed(*it) for it in items], [it[1] for it in items]))
    fw = {}
    small_parts = [_unpack(sv_all[k], [wts[n].shape for n in _SMALL_SHARDED]) for k in range(4)]
    for i, n in enumerate(_SMALL_SHARDED):
        fw[n] = jnp.concatenate([small_parts[k][i] for k in range(4)], axis=-1)
    for n in _NAMES:
        fw.setdefault(n, wts[n])

    def row8(rows, width):
        rows = [r.reshape(1, width) for r in rows]
        return jnp.concatenate(rows + [jnp.zeros((8 - len(rows), width), F32)], axis=0)

    x2 = x.reshape(t, d)
    saved = []
    for i in range(DEPTH):
        j = i // 2
        even = i % 2 == 0
        b_in, b_out = (fw['b_in_e'], fw['b_out_e']) if even else (fw['b_in_o'], fw['b_out_o'])
        w_in = layer_refs[i][3][...][0] if i else w_in_first[0]
        z = _in_proj(x2, w_in, b_in[j].reshape(1, N_COLS))
        z3 = z.reshape(bsz, s_len, N_COLS)
        if even:
            cw = jnp.concatenate([fw['conv_a_w'][j], jnp.zeros((1, W_BR), F32)], axis=0)
            mvec = row8([fw['conv_a_b'][j], fw['ln_a_g'][j], fw['ln_a_b'][j], fw['ln_v_g'][j], fw['ln_v_b'][j]], W_BR)
            bsf = jnp.repeat(fw['b_s'][j].T, W_BR // 8, axis=1)
            mix = (cw, mvec, fw['w_s'][j], jnp.swapaxes(fw['w_s'][j], 1, 2), bsf)
            y3 = _even_fwd(z3, cw, mvec, fw['w_s'][j], bsf)
        else:
            mvec = row8([fw['pool_scale'][j]] + [fw['conv_d_w'][j][k] for k in range(3)], W_BR)
            mix = (fw['w_pool'][j].astype(MM_DTYPE), mvec)
            y3 = _odd_fwd(z3, mix[0], mvec)
        pvec = row8([b_out[j], fw['ln_g'][i], fw['ln_b'][i], fw['b_ple_gate'][i]], d)
        post_w = (layer_refs[i][0][...][0], layer_refs[i][1][...][0], layer_refs[i][2][...][0], pvec)
        p2 = p[i].reshape(t, D_PLE)
        y2 = y3.reshape(t, 2 * W_BR)
        if i < DEPTH - 1:
            xn, r2, gate2 = _post_fwd(y2, x2, p2, *post_w)
        else:
            dx, r2, gate2, sq = _post_fwd(y2, x2, p2, *post_w, tgt=loss_target.reshape(t, d))
        saved.append((x2, z3, y2, r2, gate2, p2, w_in, mix, post_w))
        x2 = xn

    loss = lax.psum(0.5 * jnp.sum(sq) / d, ("x", "y", "c"))

    gr = {n: [None] * wts[n].shape[0] for n in _NAMES}
    for i in reversed(range(DEPTH)):
        j = i // 2
        even = i % 2 == 0
        x_in, z3, y2, r2, gate2, p2, w_in, mix, post_w = saved[i]
        dxr, dy, dwo, dwg, dwp, dpv = _post_bwd(dx, r2, gate2, p2, y2, *post_w)
        dy3 = dy.reshape(bsz, s_len, 2 * W_BR)
        sfx = '_e' if even else '_o'
        gr['w_out' + sfx][j], gr['b_out' + sfx][j] = dwo, dpv[0]
        gr['w_ple_gate'][i], gr['w_ple'][i] = dwg, dwp
        gr['ln_g'][i], gr['ln_b'][i], gr['b_ple_gate'][i] = dpv[1], dpv[2], dpv[3]
        if even:
            dz3, dbin, dcw, dmv, dws, dbsf = _even_bwd(z3, dy3, *mix)
            gr['conv_a_w'][j], gr['conv_a_b'][j] = dcw[:KA], dmv[0]
            gr['ln_a_g'][j], gr['ln_a_b'][j], gr['ln_v_g'][j], gr['ln_v_b'][j] = dmv[1], dmv[2], dmv[3], dmv[4]
            gr['w_s'][j] = dws
            gr['b_s'][j] = jnp.sum(dbsf.reshape(GBLK, 8, W_BR // 8), axis=2).T
        else:
            dz3, dbin, dwpool, dmv = _odd_bwd(z3, dy3, *mix)
            gr['w_pool'][j], gr['pool_scale'][j], gr['conv_d_w'][j] = dwpool, dmv[0], dmv[1:4]
        gr['b_in' + sfx][j] = dbin[0]
        dz2 = dz3.reshape(t, N_COLS)
        gr['w_in' + sfx][j] = _in_proj_bwd_dw(x_in, dz2)
        dx = _in_proj_bwd_dx(dxr, dz2, w_in)
    grad_x = dx.reshape(bsz, s_len, d)

    small_names = [n for n in _NAMES if n not in _BIG]
    g_big = [jnp.stack(gr[n]) for n in _BIG]
    g_small_full = [jnp.stack(gr[n]) for n in small_names]
    small = _pack(g_small_full, 64)
    *got_pair, got_small = _pair_exchange(g_big, small)
    sums = [_pair_sum(g, o, half) for g, o in zip(g_big, got_pair)]
    s1 = _small_pair_sum(small, got_small, half)
    *got_chips, got_small2 = _chip_scatter(sums, _BIG_AXES, s1)
    reds = [_chip_sum(sm, o, ax, pos) for sm, o, ax in zip(sums, got_chips, _BIG_AXES)]
    s2 = _small_chip_sum(s1, got_small2, pos)
    *g_shards, small_all = _final_exchange(reds, s2)
    g_small = _unpack(small_all.reshape(small.shape), [g.shape for g in g_small_full])
    grads = dict(zip(_BIG, g_shards))
    for n, g in zip(small_names, g_small):
        if n in _SMALL_SHARDED:
            w = wts[n].shape[-1]
            g = lax.dynamic_slice_in_dim(g, (2 * cx + cy) * w, w, axis=g.ndim - 1)
        grads[n] = g

    delta, new_m, new_v = {}, {}, {}
    for n in _BIG:
        delta[n], new_m[n], new_v[n] = _adamw_big(wts[n], grads[n], mom[n], var[n])
    ds, ms, vs = _adamw_small([wts[n] for n in small_names], [grads[n] for n in small_names],
                              [mom[n] for n in small_names], [var[n] for n in small_names])
    for n, a, b, c_ in zip(small_names, ds, ms, vs):
        delta[n], new_m[n], new_v[n] = a, b, c_

    return (loss, grad_x, *[grads[n] for n in _NAMES], *[delta[n] for n in _NAMES],
            *[new_m[n] for n in _NAMES], *[new_v[n] for n in _NAMES])
```

```python
import functools
import math

import jax
import jax.numpy as jnp
from jax import lax
from jax.experimental import pallas as pl
from jax.experimental.pallas import tpu as pltpu
from jax.experimental.pallas import tpu_sc as plsc

F32 = jnp.float32
MM_DTYPE = jnp.bfloat16
WIRE_DTYPE = jnp.bfloat16
SEQ_TILE = 512
ROW_TILE = 512
BWD_ROW_TILE = 256
HALO = 32
CONV_CHUNK = 32
GBLK = 128
VMEM_LIMIT = 56 * 1024 * 1024

D_MODEL = 1024
W_BR = 512
N_COLS = 6 * W_BR
D_PLE = 256
KA = 31
DEPTH = 4
POOL_WINDOWS = (2, 4, 8, 16)
ALPHA = (2.0 * DEPTH) ** 0.25
LN_EPS = 1e-5
GELU_C = math.sqrt(2.0 / math.pi)

ADAM_LR, ADAM_B1, ADAM_B2, ADAM_EPS, ADAM_WD, ADAM_STEP = 0.001, 0.9, 0.999, 1e-08, 0.01, 10

MESH = pl.DeviceIdType.MESH
ANY = pl.BlockSpec(memory_space=pl.ANY)


def _cparams(sem=None):
    return pltpu.CompilerParams(dimension_semantics=sem, vmem_limit_bytes=VMEM_LIMIT)


def _sigmoid(x):
    return 1.0 / (1.0 + jnp.exp(-x))


def _silu(x):
    return x * _sigmoid(x)


def _silu_grad(x):
    s = _sigmoid(x)
    return x * s, s * (1.0 + x * (1.0 - s))


def _gelu(x):
    return 0.5 * x * (1.0 + jnp.tanh(GELU_C * (x + 0.044715 * (x * x * x))))


def _gelu_grad(x):
    x2 = x * x
    th = jnp.tanh(GELU_C * (x + 0.044715 * (x * x2)))
    return 0.5 * x * (1.0 + th), 0.5 * (1.0 + th) + 0.5 * x * (1.0 - th * th) * (GELU_C * (1.0 + 3.0 * 0.044715 * x2))


def _ln_stats(x):
    mu = jnp.mean(x, axis=-1, keepdims=True)
    d = x - mu
    var = jnp.mean(d * d, axis=-1, keepdims=True)
    rs = lax.rsqrt(var + LN_EPS)
    return d * rs, rs


def _ln_bwd(dxh, xh, rs):
    return rs * (dxh - jnp.mean(dxh, axis=-1, keepdims=True) - xh * jnp.mean(dxh * xh, axis=-1, keepdims=True))


def _mm(a):
    return a.astype(MM_DTYPE)


def _dot(a, b):
    return jnp.dot(_mm(a), _mm(b), preferred_element_type=F32)


def _dot_nt(a, b):
    return lax.dot_general(_mm(a), _mm(b), (((1,), (1,)), ((), ())), preferred_element_type=F32)


def _dot_tn(a, b):
    return lax.dot_general(_mm(a), _mm(b), (((0,), (0,)), ((), ())), preferred_element_type=F32)


def _rowsum(x):
    return jnp.sum(x, axis=0, keepdims=True)


def _in_proj(x2, w, b):
    t, d = x2.shape
    n = w.shape[1]
    tm = min(ROW_TILE, t)
    nc = 768

    def body(x_ref, w_ref, b_ref, z_ref):
        xb = _mm(x_ref[...])
        for j in range(n // nc):
            cs = slice(j * nc, (j + 1) * nc)
            z_ref[:, cs] = jnp.dot(xb, w_ref[:, cs], preferred_element_type=F32) + b_ref[:, cs]

    return pl.pallas_call(
        body, name="in_proj", grid=(t // tm,),
        in_specs=[pl.BlockSpec((tm, d), lambda i: (i, 0)), pl.BlockSpec((d, n), lambda i: (0, 0)),
                  pl.BlockSpec((1, n), lambda i: (0, 0))],
        out_specs=pl.BlockSpec((tm, n), lambda i: (i, 0)),
        out_shape=jax.ShapeDtypeStruct((t, n), F32),
        compiler_params=_cparams(("parallel",)),
    )(x2, w, b)


def _in_proj_bwd_dx(dxr, dz, w):
    t, d = dxr.shape
    n = w.shape[1]
    tm = min(ROW_TILE, t)

    def body(dxr_ref, dz_ref, w_ref, dx_ref):
        dx_ref[...] = dxr_ref[...] + _dot_nt(dz_ref[...], w_ref[...])

    return pl.pallas_call(
        body, name="in_proj_bwd_dx", grid=(t // tm,),
        in_specs=[pl.BlockSpec((tm, d), lambda i: (i, 0)), pl.BlockSpec((tm, n), lambda i: (i, 0)),
                  pl.BlockSpec((d, n), lambda i: (0, 0))],
        out_specs=pl.BlockSpec((tm, d), lambda i: (i, 0)),
        out_shape=jax.ShapeDtypeStruct((t, d), F32),
        compiler_params=_cparams(("parallel",)),
    )(dxr, dz, w)


def _in_proj_bwd_dw(x2, dz):
    t, d = x2.shape
    n = dz.shape[1]
    tm = min(ROW_TILE, t)
    nc = 768

    def body(x_ref, dz_ref, dw_ref):
        @pl.when(pl.program_id(0) == 0)
        def _():
            dw_ref[...] = jnp.zeros_like(dw_ref)
        xb = _mm(x_ref[...])
        for j in range(n // nc):
            cs = slice(j * nc, (j + 1) * nc)
            dw_ref[:, cs] += _dot_tn(xb, dz_ref[:, cs])

    return pl.pallas_call(
        body, name="in_proj_bwd_dw", grid=(t // tm,),
        in_specs=[pl.BlockSpec((tm, d), lambda i: (i, 0)), pl.BlockSpec((tm, n), lambda i: (i, 0))],
        out_specs=pl.BlockSpec((d, n), lambda i: (0, 0)),
        out_shape=jax.ShapeDtypeStruct((d, n), F32),
        compiler_params=_cparams(("arbitrary",)),
    )(x2, dz)


def _halo_specs(ts, s_len, cols, left=True):
    per = ts // HALO
    last = s_len // HALO - 1
    if left:
        return pl.BlockSpec((1, HALO, cols), lambda b, s: (b, jnp.maximum(s * per - 1, 0), 0))
    return pl.BlockSpec((1, HALO, cols), lambda b, s: (b, jnp.minimum((s + 1) * per, last), 0))


def _build_shifts(src_ref, sh_ref, rows):
    for r in range(1, 8):
        sh_ref[r - 1, 0:rows, :] = src_ref[r:r + rows, :]


def _shifted(src_ref, sh_ref, r, start, n):
    if r == 0:
        return src_ref[pl.ds(start, n), :]
    return sh_ref[r - 1, pl.ds(start, n), :]


def _tril_masks():
    ri = lax.broadcasted_iota(jnp.int32, (GBLK, GBLK), 0)
    ci = lax.broadcasted_iota(jnp.int32, (GBLK, GBLK), 1)
    return ri >= ci, ci >= ri


def _spatial(w_ref, keep, vb):
    lane = lax.broadcasted_iota(jnp.int32, (GBLK, GBLK), 1)
    outs = []
    for p in range(4):
        xs = vb[:, p * GBLK:(p + 1) * GBLK]
        r0 = jnp.dot(_mm(jnp.where(keep, w_ref[2 * p], 0.0)), xs, preferred_element_type=F32)
        r1 = jnp.dot(_mm(jnp.where(keep, w_ref[2 * p + 1], 0.0)), xs, preferred_element_type=F32)
        outs.append(jnp.where(lane < 64, r0, r1))
    return jnp.concatenate(outs, axis=1)


def _even_fwd(z3, cw, vec, ws, bsf):
    bsz, s_len, _ = z3.shape
    ts = min(SEQ_TILE, s_len)
    ext_rows = ts + HALO

    def body(z_ref, zl_ref, cw_ref, vec_ref, ws_ref, bsf_ref, y_ref, ext_ref, sh_ref, a1_ref):
        s = pl.program_id(1)
        hl = zl_ref[0]
        a0h = hl[:, 0:W_BR] * _sigmoid(hl[:, W_BR:2 * W_BR])
        ext_ref[0:HALO, :] = jnp.where(s > 0, a0h, 0.0)
        ext_ref[HALO:ext_rows, :] = z_ref[0, :, 0:W_BR] * _sigmoid(z_ref[0, :, W_BR:2 * W_BR])
        ext_ref[ext_rows:ext_rows + 8, :] = jnp.zeros((8, W_BR), F32)
        _build_shifts(ext_ref, sh_ref, ext_rows)

        def conv_chunk(ci, carry):
            base = pl.multiple_of(ci * CONV_CHUNK, CONV_CHUNK)
            acc = jnp.zeros((CONV_CHUNK, W_BR), F32) + vec_ref[0:1, :]
            for k in range(KA):
                q, r = divmod(2 + k, 8)
                acc = acc + _shifted(ext_ref, sh_ref, r, base + 8 * q, CONV_CHUNK) * cw_ref[k:k + 1, :]
            a1_ref[pl.ds(base, CONV_CHUNK), :] = acc
            return carry

        lax.fori_loop(0, ts // CONV_CHUNK, conv_chunk, 0)
        keep, _ = _tril_masks()

        def block(bi, carry):
            rows = pl.ds(pl.multiple_of(bi * GBLK, GBLK), GBLK)
            xh, _ = _ln_stats(a1_ref[rows, :])
            a = _silu(xh * vec_ref[1:2, :] + vec_ref[2:3, :]) * _silu(z_ref[0, rows, 2 * W_BR:3 * W_BR])
            y_ref[0, rows, 0:W_BR] = a.astype(y_ref.dtype)
            ua = _gelu(z_ref[0, rows, 3 * W_BR:4 * W_BR])
            vh, _ = _ln_stats(_gelu(z_ref[0, rows, 4 * W_BR:5 * W_BR]))
            vl = vh * vec_ref[3:4, :] + vec_ref[4:5, :]
            sg = _spatial(ws_ref, keep, _mm(vl)) + bsf_ref[...]
            g = ua * sg * _silu(z_ref[0, rows, 5 * W_BR:6 * W_BR])
            y_ref[0, rows, W_BR:2 * W_BR] = g.astype(y_ref.dtype)
            return carry

        lax.fori_loop(0, ts // GBLK, block, 0)

    full = lambda shape: pl.BlockSpec(shape, lambda b, s: (0,) * len(shape))
    return pl.pallas_call(
        body, name="even_fwd", grid=(bsz, s_len // ts),
        in_specs=[pl.BlockSpec((1, ts, N_COLS), lambda b, s: (b, s, 0)), _halo_specs(ts, s_len, 2 * W_BR),
                  full((32, W_BR)), full((8, W_BR)), full((8, GBLK, GBLK)), full((GBLK, W_BR))],
        out_specs=pl.BlockSpec((1, ts, 2 * W_BR), lambda b, s: (b, s, 0)),
        out_shape=jax.ShapeDtypeStruct((bsz, s_len, 2 * W_BR), MM_DTYPE),
        scratch_shapes=[pltpu.VMEM((ext_rows + 8, W_BR), F32), pltpu.VMEM((7, ext_rows, W_BR), F32),
                        pltpu.VMEM((ts, W_BR), F32)],
        compiler_params=_cparams(("parallel", "parallel")),
    )(z3, z3, cw, vec, ws, bsf)


def _even_bwd(z3, dy3, cw, vec, ws, wst, bsf):
    bsz, s_len, _ = z3.shape
    ts = min(SEQ_TILE, s_len)
    n_s = s_len // ts
    ext_rows = ts + 2 * HALO
    a_rows = ts + HALO

    def body(z_ref, zl_ref, zr_ref, dy_ref, dyr_ref, cw_ref, vec_ref, ws_ref, wst_ref, bsf_ref,
             dz_ref, dbin_ref, dcw_ref, dvec_ref, dws_ref, dbsf_ref,
             ext_ref, sh_ref, a1_ref, ag_ref, dya_ref, d_ref, accw_ref):
        b = pl.program_id(0)
        s = pl.program_id(1)

        @pl.when((b == 0) & (s == 0))
        def _():
            dbin_ref[...] = jnp.zeros_like(dbin_ref)
            dcw_ref[...] = jnp.zeros_like(dcw_ref)
            dvec_ref[...] = jnp.zeros_like(dvec_ref)
            dws_ref[...] = jnp.zeros_like(dws_ref)
            dbsf_ref[...] = jnp.zeros_like(dbsf_ref)

        has_right = s < n_s - 1
        hl = zl_ref[0]
        hr = zr_ref[0]
        ext_ref[0:HALO, :] = jnp.where(s > 0, hl[:, 0:W_BR] * _sigmoid(hl[:, W_BR:2 * W_BR]), 0.0)
        ext_ref[HALO:HALO + ts, :] = z_ref[0, :, 0:W_BR] * _sigmoid(z_ref[0, :, W_BR:2 * W_BR])
        ext_ref[HALO + ts:ext_rows, :] = hr[:, 0:W_BR] * _sigmoid(hr[:, W_BR:2 * W_BR])
        ext_ref[ext_rows:ext_rows + 8, :] = jnp.zeros((8, W_BR), F32)
        ag_ref[0:ts, :] = z_ref[0, :, 2 * W_BR:3 * W_BR]
        ag_ref[ts:a_rows, :] = hr[:, 2 * W_BR:3 * W_BR]
        dya_ref[0:ts, :] = dy_ref[0, :, 0:W_BR]
        dya_ref[ts:a_rows, :] = dyr_ref[0][:, 0:W_BR]
        _build_shifts(ext_ref, sh_ref, ext_rows)

        def conv_chunk(ci, carry):
            base = pl.multiple_of(ci * CONV_CHUNK, CONV_CHUNK)
            acc = jnp.zeros((CONV_CHUNK, W_BR), F32) + vec_ref[0:1, :]
            for k in range(KA):
                q, r = divmod(2 + k, 8)
                acc = acc + _shifted(ext_ref, sh_ref, r, base + 8 * q, CONV_CHUNK) * cw_ref[k:k + 1, :]
            a1_ref[pl.ds(base, CONV_CHUNK), :] = acc
            return carry

        lax.fori_loop(0, a_rows // CONV_CHUNK, conv_chunk, 0)

        def a_chunk(base, n, main):
            rows = pl.ds(base, n)
            xh, rs = _ln_stats(a1_ref[rows, :])
            ln = xh * vec_ref[1:2, :] + vec_ref[2:3, :]
            sl, dsl = _silu_grad(ln)
            sgt, dsgt = _silu_grad(ag_ref[rows, :])
            dya = dya_ref[rows, :]
            dln = dya * sgt * dsl
            da1 = _ln_bwd(dln * vec_ref[1:2, :], xh, rs)
            if main:
                d_ref[rows, :] = da1
                dag = dya * sl * dsgt
                dz_ref[0, rows, 2 * W_BR:3 * W_BR] = dag.astype(dz_ref.dtype)
                dbin_ref[0:1, 2 * W_BR:3 * W_BR] += _rowsum(dag)
                dvec_ref[0:1, :] += _rowsum(da1)
                dvec_ref[1:2, :] += _rowsum(dln * xh)
                dvec_ref[2:3, :] += _rowsum(dln)
            else:
                d_ref[rows, :] = jnp.where(has_right, da1, 0.0)

        def a_main(ci, carry):
            a_chunk(pl.multiple_of(ci * GBLK, GBLK), GBLK, True)
            return carry

        lax.fori_loop(0, ts // GBLK, a_main, 0)
        a_chunk(ts, HALO, False)
        d_ref[a_rows:a_rows + 8, :] = jnp.zeros((8, W_BR), F32)

        accw_ref[...] = jnp.zeros_like(accw_ref)

        def dw_chunk(ci, carry):
            base = pl.multiple_of(ci * CONV_CHUNK, CONV_CHUNK)
            d = d_ref[pl.ds(base, CONV_CHUNK), :]
            for k in range(KA):
                q, r = divmod(2 + k, 8)
                prod = d * _shifted(ext_ref, sh_ref, r, base + 8 * q, CONV_CHUNK)
                accw_ref[k] += jnp.sum(prod.reshape(CONV_CHUNK // 8, 8, W_BR), axis=0)
            return carry

        lax.fori_loop(0, ts // CONV_CHUNK, dw_chunk, 0)
        dcw_ref[...] += jnp.sum(accw_ref[...], axis=1)

        _build_shifts(d_ref, sh_ref, a_rows)

        def dx_chunk(ci, carry):
            base = pl.multiple_of(ci * CONV_CHUNK, CONV_CHUNK)
            rows = pl.ds(base, CONV_CHUNK)
            acc = jnp.zeros((CONV_CHUNK, W_BR), F32)
            for m in range(KA):
                q, r = divmod(m, 8)
                acc = acc + _shifted(d_ref, sh_ref, r, base + 8 * q, CONV_CHUNK) * cw_ref[KA - 1 - m:KA - m, :]
            aval = z_ref[0, rows, 0:W_BR]
            sg = _sigmoid(z_ref[0, rows, W_BR:2 * W_BR])
            dval = acc * sg
            dglu = acc * aval * sg * (1.0 - sg)
            dz_ref[0, rows, 0:W_BR] = dval.astype(dz_ref.dtype)
            dz_ref[0, rows, W_BR:2 * W_BR] = dglu.astype(dz_ref.dtype)
            dbin_ref[0:1, 0:W_BR] += _rowsum(dval)
            dbin_ref[0:1, W_BR:2 * W_BR] += _rowsum(dglu)
            return carry

        lax.fori_loop(0, ts // CONV_CHUNK, dx_chunk, 0)

        keep, keep_t = _tril_masks()
        lane = lax.broadcasted_iota(jnp.int32, (GBLK, GBLK), 1)

        def block(bi, carry):
            rows = pl.ds(pl.multiple_of(bi * GBLK, GBLK), GBLK)
            ua, dua = _gelu_grad(z_ref[0, rows, 3 * W_BR:4 * W_BR])
            va, dva = _gelu_grad(z_ref[0, rows, 4 * W_BR:5 * W_BR])
            sgt, dsgt = _silu_grad(z_ref[0, rows, 5 * W_BR:6 * W_BR])
            vh, rs = _ln_stats(va)
            vlb = _mm(vh * vec_ref[3:4, :] + vec_ref[4:5, :])
            sg = _spatial(ws_ref, keep, vlb) + bsf_ref[...]
            dyg = dy_ref[0, rows, W_BR:2 * W_BR]
            du = dyg * sg * sgt * dua
            dsg = dyg * ua * sgt
            dgg = dyg * ua * sg * dsgt
            dvl = _spatial(wst_ref, keep_t, _mm(dsg))
            for p in range(4):
                dsp = dsg[:, p * GBLK:(p + 1) * GBLK]
                vlp = vlb[:, p * GBLK:(p + 1) * GBLK]
                dws_ref[2 * p] += jnp.where(keep, _dot_nt(jnp.where(lane < 64, dsp, 0.0), vlp), 0.0)
                dws_ref[2 * p + 1] += jnp.where(keep, _dot_nt(jnp.where(lane >= 64, dsp, 0.0), vlp), 0.0)
            dbsf_ref[...] += dsg
            dvec_ref[3:4, :] += _rowsum(dvl * vh)
            dvec_ref[4:5, :] += _rowsum(dvl)
            dv = _ln_bwd(dvl * vec_ref[3:4, :], vh, rs) * dva
            dz_ref[0, rows, 3 * W_BR:4 * W_BR] = du.astype(dz_ref.dtype)
            dz_ref[0, rows, 4 * W_BR:5 * W_BR] = dv.astype(dz_ref.dtype)
            dz_ref[0, rows, 5 * W_BR:6 * W_BR] = dgg.astype(dz_ref.dtype)
            dbin_ref[0:1, 3 * W_BR:4 * W_BR] += _rowsum(du)
            dbin_ref[0:1, 4 * W_BR:5 * W_BR] += _rowsum(dv)
            dbin_ref[0:1, 5 * W_BR:6 * W_BR] += _rowsum(dgg)
            return carry

        lax.fori_loop(0, ts // GBLK, block, 0)

    full = lambda shape: pl.BlockSpec(shape, lambda b, s: (0,) * len(shape))
    acc_shapes = [(1, N_COLS), (32, W_BR), (8, W_BR), (8, GBLK, GBLK), (GBLK, W_BR)]
    return pl.pallas_call(
        body, name="even_bwd", grid=(bsz, n_s),
        in_specs=[pl.BlockSpec((1, ts, N_COLS), lambda b, s: (b, s, 0)),
                  _halo_specs(ts, s_len, N_COLS, True), _halo_specs(ts, s_len, N_COLS, False),
                  pl.BlockSpec((1, ts, 2 * W_BR), lambda b, s: (b, s, 0)), _halo_specs(ts, s_len, 2 * W_BR, False),
                  full((32, W_BR)), full((8, W_BR)), full((8, GBLK, GBLK)), full((8, GBLK, GBLK)), full((GBLK, W_BR))],
        out_specs=[pl.BlockSpec((1, ts, N_COLS), lambda b, s: (b, s, 0))] + [full(sh) for sh in acc_shapes],
        out_shape=[jax.ShapeDtypeStruct((bsz, s_len, N_COLS), MM_DTYPE)] + [jax.ShapeDtypeStruct(sh, F32) for sh in acc_shapes],
        scratch_shapes=[pltpu.VMEM((ext_rows + 8, W_BR), F32), pltpu.VMEM((7, ext_rows, W_BR), F32),
                        pltpu.VMEM((a_rows, W_BR), F32), pltpu.VMEM((a_rows, W_BR), F32), pltpu.VMEM((a_rows, W_BR), F32),
                        pltpu.VMEM((a_rows + 8, W_BR), F32), pltpu.VMEM((32, 8, W_BR), F32)],
        compiler_params=_cparams(("arbitrary", "arbitrary")),
    )(z3, z3, z3, dy3, dy3, cw, vec, ws, wst, bsf)


def _pool_stages(e_refs, rows):
    e0, e1, e2, e3, e4 = e_refs
    e1[8:rows, :] = e0[8:rows, :] + e0[7:rows - 1, :]
    e2[16:rows, GBLK:] = e1[16:rows, GBLK:] + e1[14:rows - 2, GBLK:]
    e3[24:rows, 2 * GBLK:] = e2[24:rows, 2 * GBLK:] + e2[20:rows - 4, 2 * GBLK:]
    e4[32:rows, 3 * GBLK:] = e3[32:rows, 3 * GBLK:] + e3[24:rows - 8, 3 * GBLK:]


def _pool_counts(start, n):
    pos = (start + 1 + lax.broadcasted_iota(jnp.int32, (n, 1), 0)).astype(F32)
    return [jnp.minimum(pos, float(w)) for w in POOL_WINDOWS]


def _pooled_into(e_refs, pooled_ref, s, ts):
    cnt = _pool_counts(s * ts, ts)
    for g in range(4):
        cs = slice(g * GBLK, (g + 1) * GBLK)
        pooled_ref[:, cs] = e_refs[g + 1][HALO:HALO + ts, cs] / cnt[g] - e_refs[0][HALO:HALO + ts, cs]


def _odd_fwd(z3, wp, vec):
    bsz, s_len, _ = z3.shape
    ts = min(SEQ_TILE, s_len)
    ext_rows = ts + HALO

    def body(z_ref, zl_ref, wp_ref, vec_ref, y_ref, e0, e1, e2, e3, e4, pooled_ref, dext_ref, ec_ref):
        s = pl.program_id(1)
        hl = zl_ref[0]
        e0[0:HALO, :] = jnp.where(s > 0, hl[:, 0:W_BR], 0.0)
        e0[HALO:ext_rows, :] = z_ref[0, :, 0:W_BR]
        _pool_stages((e0, e1, e2, e3, e4), ext_rows)
        _pooled_into((e0, e1, e2, e3, e4), pooled_ref, s, ts)
        dext_ref[0:HALO, :] = jnp.where(s > 0, hl[:, 2 * W_BR:3 * W_BR] * hl[:, 4 * W_BR:5 * W_BR], 0.0)
        dext_ref[HALO:ext_rows, :] = z_ref[0, :, 2 * W_BR:3 * W_BR] * z_ref[0, :, 4 * W_BR:5 * W_BR]
        ec_ref[...] = (vec_ref[1:2, :] * dext_ref[HALO - 2:HALO - 2 + ts, :] + vec_ref[2:3, :] * dext_ref[HALO - 1:HALO - 1 + ts, :]
                       + vec_ref[3:4, :] * dext_ref[HALO:HALO + ts, :])

        def block(bi, carry):
            rows = pl.ds(pl.multiple_of(bi * GBLK, GBLK), GBLK)
            pb = _mm(pooled_ref[rows, :])
            cpre = jnp.concatenate([jnp.dot(pb[:, g * GBLK:(g + 1) * GBLK], wp_ref[g], preferred_element_type=F32)
                                    for g in range(4)], axis=1)
            c = cpre * vec_ref[0:1, :] * _silu(z_ref[0, rows, W_BR:2 * W_BR])
            d = z_ref[0, rows, 3 * W_BR:4 * W_BR] * ec_ref[rows, :] * _silu(z_ref[0, rows, 5 * W_BR:6 * W_BR])
            y_ref[0, rows, 0:W_BR] = c.astype(y_ref.dtype)
            y_ref[0, rows, W_BR:2 * W_BR] = d.astype(y_ref.dtype)
            return carry

        lax.fori_loop(0, ts // GBLK, block, 0)

    full = lambda shape: pl.BlockSpec(shape, lambda b, s: (0,) * len(shape))
    ebuf = pltpu.VMEM((ext_rows, W_BR), F32)
    return pl.pallas_call(
        body, name="odd_fwd", grid=(bsz, s_len // ts),
        in_specs=[pl.BlockSpec((1, ts, N_COLS), lambda b, s: (b, s, 0)), _halo_specs(ts, s_len, N_COLS),
                  full((4, GBLK, GBLK)), full((8, W_BR))],
        out_specs=pl.BlockSpec((1, ts, 2 * W_BR), lambda b, s: (b, s, 0)),
        out_shape=jax.ShapeDtypeStruct((bsz, s_len, 2 * W_BR), MM_DTYPE),
        scratch_shapes=[ebuf, ebuf, ebuf, ebuf, ebuf, pltpu.VMEM((ts, W_BR), F32), ebuf, pltpu.VMEM((ts, W_BR), F32)],
        compiler_params=_cparams(("parallel", "parallel")),
    )(z3, z3, wp, vec)


def _odd_bwd(z3, dy3, wp, vec):
    bsz, s_len, _ = z3.shape
    ts = min(SEQ_TILE, s_len)
    n_s = s_len // ts
    ext_rows = ts + HALO

    def body(z_ref, zl_ref, zr_ref, dy_ref, dyr_ref, wp_ref, vec_ref,
             dz_ref, dbin_ref, dwp_ref, dvec_ref,
             e0, e1, e2, e3, e4, pooled_ref, dext_ref, ec_ref, q_ref, dp_ref, de_ref, f1, f2, f3, f4):
        b = pl.program_id(0)
        s = pl.program_id(1)

        @pl.when((b == 0) & (s == 0))
        def _():
            dbin_ref[...] = jnp.zeros_like(dbin_ref)
            dwp_ref[...] = jnp.zeros_like(dwp_ref)
            dvec_ref[...] = jnp.zeros_like(dvec_ref)

        has_right = s < n_s - 1
        hl = zl_ref[0]
        e0[0:HALO, :] = jnp.where(s > 0, hl[:, 0:W_BR], 0.0)
        e0[HALO:ext_rows, :] = z_ref[0, :, 0:W_BR]
        _pool_stages((e0, e1, e2, e3, e4), ext_rows)
        _pooled_into((e0, e1, e2, e3, e4), pooled_ref, s, ts)
        dext_ref[0:HALO, :] = jnp.where(s > 0, hl[:, 2 * W_BR:3 * W_BR] * hl[:, 4 * W_BR:5 * W_BR], 0.0)
        dext_ref[HALO:ext_rows, :] = z_ref[0, :, 2 * W_BR:3 * W_BR] * z_ref[0, :, 4 * W_BR:5 * W_BR]
        ec_ref[...] = (vec_ref[1:2, :] * dext_ref[HALO - 2:HALO - 2 + ts, :] + vec_ref[2:3, :] * dext_ref[HALO - 1:HALO - 1 + ts, :]
                       + vec_ref[3:4, :] * dext_ref[HALO:HALO + ts, :])

        def grads(zc_gate, zd_b, zd_gate, dyc, dyd, rows_out, n, start, valid):
            sgt = _silu(zc_gate)
            dcpre = dyc * vec_ref[0:1, :] * sgt
            db = _mm(dcpre)
            dpool = jnp.concatenate([_dot_nt(db[:, g * GBLK:(g + 1) * GBLK], wp_ref[g]) for g in range(4)], axis=1)
            cnt = _pool_counts(start, n)
            q = jnp.concatenate([dpool[:, g * GBLK:(g + 1) * GBLK] / cnt[g] for g in range(4)], axis=1)
            de = dyd * zd_b * _silu(zd_gate)
            if valid is not None:
                q = jnp.where(valid, q, 0.0)
                de = jnp.where(valid, de, 0.0)
            q_ref[rows_out, :] = q
            dp_ref[rows_out, :] = dpool
            de_ref[rows_out, :] = de
            return dcpre

        def block(bi, carry):
            base = pl.multiple_of(bi * GBLK, GBLK)
            rows = pl.ds(base, GBLK)
            cg = z_ref[0, rows, W_BR:2 * W_BR]
            dyc = dy_ref[0, rows, 0:W_BR]
            dyd = dy_ref[0, rows, W_BR:2 * W_BR]
            d_b = z_ref[0, rows, 3 * W_BR:4 * W_BR]
            d_gate = z_ref[0, rows, 5 * W_BR:6 * W_BR]
            dcpre = grads(cg, d_b, d_gate, dyc, dyd, rows, GBLK, s * ts + base, None)
            pb = _mm(pooled_ref[rows, :])
            dcb = _mm(dcpre)
            cpre = jnp.concatenate([jnp.dot(pb[:, g * GBLK:(g + 1) * GBLK], wp_ref[g], preferred_element_type=F32)
                                    for g in range(4)], axis=1)
            for g in range(4):
                cs = slice(g * GBLK, (g + 1) * GBLK)
                dwp_ref[g] += _dot_tn(pb[:, cs], dcb[:, cs])
            sgt, dsgt = _silu_grad(cg)
            dvec_ref[0:1, :] += _rowsum(dyc * cpre * sgt)
            dcg = dyc * cpre * vec_ref[0:1, :] * dsgt
            sdt, dsdt = _silu_grad(d_gate)
            ec = ec_ref[rows, :]
            ddb = dyd * ec * sdt
            ddg = dyd * d_b * ec * dsdt
            dz_ref[0, rows, W_BR:2 * W_BR] = dcg.astype(dz_ref.dtype)
            dz_ref[0, rows, 3 * W_BR:4 * W_BR] = ddb.astype(dz_ref.dtype)
            dz_ref[0, rows, 5 * W_BR:6 * W_BR] = ddg.astype(dz_ref.dtype)
            dbin_ref[0:1, W_BR:2 * W_BR] += _rowsum(dcg)
            dbin_ref[0:1, 3 * W_BR:4 * W_BR] += _rowsum(ddb)
            dbin_ref[0:1, 5 * W_BR:6 * W_BR] += _rowsum(ddg)
            return carry

        lax.fori_loop(0, ts // GBLK, block, 0)
        hr = zr_ref[0]
        dyr = dyr_ref[0]
        grads(hr[:, W_BR:2 * W_BR], hr[:, 3 * W_BR:4 * W_BR], hr[:, 5 * W_BR:6 * W_BR], dyr[:, 0:W_BR], dyr[:, W_BR:2 * W_BR],
              slice(ts, ext_rows), HALO, (s + 1) * ts, has_right)

        r1, r2, r3 = ts + 24, ts + 16, ts + 8
        f1[0:r1, :] = q_ref[0:r1, :] + q_ref[1:r1 + 1, :]
        f2[0:r2, GBLK:] = f1[0:r2, GBLK:] + f1[2:r2 + 2, GBLK:]
        f3[0:r3, 2 * GBLK:] = f2[0:r3, 2 * GBLK:] + f2[4:r3 + 4, 2 * GBLK:]
        f4[0:ts, 3 * GBLK:] = f3[0:ts, 3 * GBLK:] + f3[8:ts + 8, 3 * GBLK:]
        for g, f in enumerate((f1, f2, f3, f4)):
            cs = slice(g * GBLK, (g + 1) * GBLK)
            dvg = f[0:ts, cs] - dp_ref[0:ts, cs]
            dz_ref[0, :, cs] = dvg.astype(dz_ref.dtype)
            dbin_ref[0:1, cs] += _rowsum(dvg)

        ddc = (vec_ref[1:2, :] * de_ref[2:ts + 2, :] + vec_ref[2:3, :] * de_ref[1:ts + 1, :] + vec_ref[3:4, :] * de_ref[0:ts, :])
        d_h = z_ref[0, :, 2 * W_BR:3 * W_BR]
        d_c = z_ref[0, :, 4 * W_BR:5 * W_BR]
        ddh = ddc * d_c
        ddcc = ddc * d_h
        dz_ref[0, :, 2 * W_BR:3 * W_BR] = ddh.astype(dz_ref.dtype)
        dz_ref[0, :, 4 * W_BR:5 * W_BR] = ddcc.astype(dz_ref.dtype)
        dbin_ref[0:1, 2 * W_BR:3 * W_BR] += _rowsum(ddh)
        dbin_ref[0:1, 4 * W_BR:5 * W_BR] += _rowsum(ddcc)
        de = de_ref[0:ts, :]
        for k in range(3):
            dvec_ref[1 + k:2 + k, :] += _rowsum(de * dext_ref[HALO - 2 + k:HALO - 2 + k + ts, :])

    full = lambda shape: pl.BlockSpec(shape, lambda b, s: (0,) * len(shape))
    acc_shapes = [(1, N_COLS), (4, GBLK, GBLK), (8, W_BR)]
    ebuf = pltpu.VMEM((ext_rows, W_BR), F32)
    tbuf = pltpu.VMEM((ts, W_BR), F32)
    return pl.pallas_call(
        body, name="odd_bwd", grid=(bsz, n_s),
        in_specs=[pl.BlockSpec((1, ts, N_COLS), lambda b, s: (b, s, 0)),
                  _halo_specs(ts, s_len, N_COLS, True), _halo_specs(ts, s_len, N_COLS, False),
                  pl.BlockSpec((1, ts, 2 * W_BR), lambda b, s: (b, s, 0)), _halo_specs(ts, s_len, 2 * W_BR, False),
                  full((4, GBLK, GBLK)), full((8, W_BR))],
        out_specs=[pl.BlockSpec((1, ts, N_COLS), lambda b, s: (b, s, 0))] + [full(sh) for sh in acc_shapes],
        out_shape=[jax.ShapeDtypeStruct((bsz, s_len, N_COLS), MM_DTYPE)] + [jax.ShapeDtypeStruct(sh, F32) for sh in acc_shapes],
        scratch_shapes=[ebuf, ebuf, ebuf, ebuf, ebuf, tbuf, ebuf, tbuf, ebuf, ebuf, ebuf, ebuf, ebuf, ebuf, tbuf],
        compiler_params=_cparams(("arbitrary", "arbitrary")),
    )(z3, z3, z3, dy3, dy3, wp, vec)


def _post_fwd(y2, x2, p2, w_out, wg, wple, vec, tgt=None):
    t, d = x2.shape
    tm = min(ROW_TILE, t)
    last = tgt is not None

    def body(*refs):
        y_ref, x_ref, p_ref, wo_ref, wg_ref, wp_ref, vec_ref = refs[:7]
        xn_ref, r_ref, gate_ref = refs[7 + last:10 + last]
        r = ALPHA * x_ref[...] + jnp.dot(y_ref[...], wo_ref[...], preferred_element_type=F32) + vec_ref[0:1, :]
        r_ref[...] = r
        xh, _ = _ln_stats(r)
        h = xh * vec_ref[1:2, :] + vec_ref[2:3, :]
        gate = _sigmoid(_dot(h, wg_ref[...]) + vec_ref[3:4, :])
        gate_ref[...] = gate
        xn = h + gate * _dot(p_ref[...], wp_ref[...])
        if last:
            sq_ref = refs[11]

            @pl.when(pl.program_id(0) == 0)
            def _():
                sq_ref[...] = jnp.zeros_like(sq_ref)
            e = xn - refs[7][...]
            xn_ref[...] = e / float(d)
            sq_ref[...] += _rowsum(e * e)
        else:
            xn_ref[...] = xn

    row = lambda c: pl.BlockSpec((tm, c), lambda i: (i, 0))
    full = lambda shape: pl.BlockSpec(shape, lambda i: (0,) * len(shape))
    return pl.pallas_call(
        body, name="post_fwd_loss" if last else "post_fwd", grid=(t // tm,),
        in_specs=[row(d), row(d), row(D_PLE), full((d, d)), full((d, d)), full((D_PLE, d)), full((8, d))] + [row(d)] * last,
        out_specs=[row(d), row(d), row(d)] + [full((1, d))] * last,
        out_shape=[jax.ShapeDtypeStruct((t, d), F32)] * 3 + [jax.ShapeDtypeStruct((1, d), F32)] * last,
        compiler_params=_cparams(("arbitrary",) if last else ("parallel",)),
    )(y2, x2, p2, w_out, wg, wple, vec, *([tgt] if last else []))


def _post_bwd(dxn, r2, gate2, p2, y2, w_out, wg, wple, vec):
    t, d = r2.shape
    tm = min(BWD_ROW_TILE, t)

    def body(dxn_ref, r_ref, gate_ref, p_ref, y_ref, wo_ref, wg_ref, wp_ref, vec_ref,
             dxr_ref, dy_ref, dwo_ref, dwg_ref, dwp_ref, dvec_ref):
        @pl.when(pl.program_id(0) == 0)
        def _():
            dwo_ref[...] = jnp.zeros_like(dwo_ref)
            dwg_ref[...] = jnp.zeros_like(dwg_ref)
            dwp_ref[...] = jnp.zeros_like(dwp_ref)
            dvec_ref[...] = jnp.zeros_like(dvec_ref)

        dxn = dxn_ref[...]
        gate = gate_ref[...]
        xh, rs = _ln_stats(r_ref[...])
        hb = _mm(xh * vec_ref[1:2, :] + vec_ref[2:3, :])
        pb = _mm(p_ref[...])
        pe = jnp.dot(pb, wp_ref[...], preferred_element_type=F32)
        dpre = dxn * pe * gate * (1.0 - gate)
        dpb = _mm(dpre)
        dh = dxn + _dot_nt(dpb, wg_ref[...])
        dwg_ref[...] += _dot_tn(hb, dpb)
        dwp_ref[...] += _dot_tn(pb, dxn * gate)
        dr = _ln_bwd(dh * vec_ref[1:2, :], xh, rs)
        drb = _mm(dr)
        dxr_ref[...] = ALPHA * dr
        dy_ref[...] = _dot_nt(drb, wo_ref[...])
        dwo_ref[...] += _dot_tn(y_ref[...], drb)
        dvec_ref[0:1, :] += _rowsum(dr)
        dvec_ref[1:2, :] += _rowsum(dh * xh)
        dvec_ref[2:3, :] += _rowsum(dh)
        dvec_ref[3:4, :] += _rowsum(dpre)

    row = lambda c: pl.BlockSpec((tm, c), lambda i: (i, 0))
    full = lambda shape: pl.BlockSpec(shape, lambda i: (0,) * len(shape))
    acc_shapes = [(d, d), (d, d), (D_PLE, d), (8, d)]
    return pl.pallas_call(
        body, name="post_bwd", grid=(t // tm,),
        in_specs=[row(d), row(d), row(d), row(D_PLE), row(d), full((d, d)), full((d, d)), full((D_PLE, d)), full((8, d))],
        out_specs=[row(d), row(d)] + [full(sh) for sh in acc_shapes],
        out_shape=[jax.ShapeDtypeStruct((t, d), F32)] * 2 + [jax.ShapeDtypeStruct(sh, F32) for sh in acc_shapes],
        compiler_params=_cparams(("arbitrary",)),
    )(dxn, r2, gate2, p2, y2, w_out, wg, wple, vec)


def _place():
    x, y, c = lax.axis_index("x"), lax.axis_index("y"), lax.axis_index("c")
    chips = [(1 - x, y), (x, 1 - y), (1 - x, 1 - y)]
    return x, y, c, chips


def _shard_of(ref, ax, k, width, lo=None, ln=None):
    idx = [slice(None)] * 3
    if lo is not None:
        idx[0] = pl.ds(lo, ln)
    idx[ax] = pl.ds(k * width, width)
    return ref.at[tuple(idx)]


def _remote(src, dst, ssem, rsem, dev):
    return pltpu.make_async_remote_copy(src_ref=src, dst_ref=dst, send_sem=ssem, recv_sem=rsem, device_id=dev, device_id_type=MESH)


def _place_shard(w, ax, chip, dtype=MM_DTYPE):
    l_dim, a_dim, b_dim = w.shape
    tr = min(256, a_dim)
    per = a_dim // tr
    shape = list(w.shape)
    shape[ax] *= 4
    if ax == 2:
        out_spec = pl.BlockSpec((1, tr, b_dim), lambda l, i, k: (l, i, k[0]))
    else:
        out_spec = pl.BlockSpec((1, tr, b_dim), lambda l, i, k: (l, k[0] * per + i, 0))

    def body(k_ref, w_ref, o_ref):
        o_ref[...] = w_ref[...].astype(o_ref.dtype)

    return pl.pallas_call(
        body, name="place_shard",
        grid_spec=pltpu.PrefetchScalarGridSpec(
            num_scalar_prefetch=1, grid=(l_dim, per),
            in_specs=[pl.BlockSpec((1, tr, b_dim), lambda l, i, k: (l, i, 0))], out_specs=out_spec),
        out_shape=jax.ShapeDtypeStruct(tuple(shape), dtype),
        compiler_params=_cparams(("parallel", "parallel")),
    )(chip, w)


def _shard_copies(src_refs, dst_refs, axes, sems):
    x, y, c, chips = _place()
    j = 2 * x + y
    cps = []
    for a, (s_ref, d_ref) in enumerate(zip(src_refs, dst_refs)):
        w = d_ref.shape[axes[a]] // 4
        for q, (qx, qy) in enumerate(chips):
            ssem, rsem = sems[3 * a + q]
            cps.append(_remote(_shard_of(s_ref, axes[a], j, w), _shard_of(d_ref, axes[a], j, w), ssem, rsem, (qx, qy, c)))
    return cps


def _chip_handshake():
    x, y, c, chips = _place()
    barrier = pltpu.get_barrier_semaphore()
    for qx, qy in chips:
        pl.semaphore_signal(barrier, inc=1, device_id=(qx, qy, c), device_id_type=MESH)
    pl.semaphore_wait(barrier, 3)


def _gather_async(name, collective_id, fulls, axes):
    n = len(fulls)
    refs = [jax.new_ref(f, memory_space=pltpu.MemorySpace.HBM) for f in fulls]

    @pl.kernel(mesh=plsc.ScalarSubcoreMesh(axis_name="seq", num_cores=1), name=name,
               scratch_types=(pltpu.SemaphoreType.DMA,) * (6 * n),
               compiler_params=pltpu.CompilerParams(collective_id=collective_id))
    def launch(*sems):
        _chip_handshake()
        cps = _shard_copies(refs, refs, axes, [(sems[2 * k], sems[2 * k + 1]) for k in range(3 * n)])
        for cp in cps:
            cp.start()
        for cp in cps:
            cp.wait()

    launch()
    return refs


def _scatter_async(name, collective_id, grads, axes):
    n = len(grads)
    outs = []
    for g, ax in zip(grads, axes):
        sh = list(g.shape)
        sh[ax] //= 4
        outs.append(jax.ShapeDtypeStruct((3,) + tuple(sh), g.dtype))

    def body(*refs):
        srcs, lands, sems = refs[:n], refs[n:2 * n], refs[2 * n:]
        _chip_handshake()
        x, y, c, chips = _place()
        cps = []
        for a in range(n):
            w = srcs[a].shape[axes[a]] // 4
            for q, (qx, qy) in enumerate(chips):
                k = 3 * a + q
                cps.append(_remote(_shard_of(srcs[a], axes[a], 2 * qx + qy, w), lands[a].at[q], sems[2 * k], sems[2 * k + 1],
                                   (qx, qy, c)))
        for cp in cps:
            cp.start()
        for cp in cps:
            cp.wait()

    return pl.kernel(body, out_type=outs, mesh=plsc.ScalarSubcoreMesh(axis_name="seq", num_cores=1), name=name,
                     scratch_types=(pltpu.SemaphoreType.DMA,) * (6 * n),
                     compiler_params=pltpu.CompilerParams(collective_id=collective_id))(*grads)


def _pair_swap(sums):
    n = len(sums)

    def body(*refs):
        g_refs, o_refs = refs[:n], refs[n:2 * n]
        ssem, rsem = refs[2 * n:]
        x, y, c, _ = _place()
        cps = [_remote(g_refs[a], o_refs[a], ssem.at[a], rsem.at[a], (x, y, 1 - c)) for a in range(n)]
        for cp in cps:
            cp.start()
        for cp in cps:
            cp.wait()

    return pl.pallas_call(
        body, name="pair_swap", in_specs=[ANY] * n, out_specs=[ANY] * n,
        out_shape=[jax.ShapeDtypeStruct(g.shape, g.dtype) for g in sums],
        scratch_shapes=[pltpu.SemaphoreType.DMA((n,)), pltpu.SemaphoreType.DMA((n,))],
        compiler_params=pltpu.CompilerParams(has_side_effects=True),
    )(*sums)


def _pair_exchange(grads, small):
    n = len(grads)
    outs = [jax.ShapeDtypeStruct((g.shape[0] // 2,) + g.shape[1:], g.dtype) for g in grads]
    outs.append(jax.ShapeDtypeStruct((small.shape[0] // 2, small.shape[1]), small.dtype))

    def body(*refs):
        g_refs, o_refs = refs[:n + 1], refs[n + 1:2 * n + 2]
        ssem, rsem = refs[2 * n + 2:]
        x, y, c, _ = _place()
        cps = []
        for a in range(n + 1):
            lh = g_refs[a].shape[0] // 2
            cp = _remote(g_refs[a].at[pl.ds((1 - c) * lh, lh)], o_refs[a], ssem.at[a], rsem.at[a], (x, y, 1 - c))
            cp.start()
            cps.append(cp)
        for cp in cps:
            cp.wait()

    return pl.pallas_call(
        body, name="pair_exchange", in_specs=[ANY] * (n + 1), out_specs=[ANY] * (n + 1), out_shape=outs,
        scratch_shapes=[pltpu.SemaphoreType.DMA((n + 1,)), pltpu.SemaphoreType.DMA((n + 1,))],
        compiler_params=pltpu.CompilerParams(has_side_effects=True),
    )(*grads, small)


def _chip_scatter(sums, axes, small):
    n = len(sums)
    outs = []
    for g, ax in zip(sums, axes):
        sh = list(g.shape)
        sh[ax] //= 4
        outs.append(jax.ShapeDtypeStruct((3,) + tuple(sh), g.dtype))
    rq = small.shape[0] // 4
    outs.append(jax.ShapeDtypeStruct((3, rq, small.shape[1]), small.dtype))

    def body(*refs):
        g_refs, o_refs = refs[:n + 1], refs[n + 1:2 * n + 2]
        ssem, rsem = refs[2 * n + 2:]
        x, y, c, chips = _place()
        cps = []
        for a in range(n + 1):
            for q, (qx, qy) in enumerate(chips):
                k = 2 * qx + qy
                if a < n:
                    src = _shard_of(g_refs[a], axes[a], k, g_refs[a].shape[axes[a]] // 4)
                else:
                    src = g_refs[a].at[pl.ds(k * rq, rq)]
                cp = _remote(src, o_refs[a].at[q], ssem.at[a * 3 + q], rsem.at[a * 3 + q], (qx, qy, c))
                cp.start()
                cps.append(cp)
        for cp in cps:
            cp.wait()

    return pl.pallas_call(
        body, name="chip_scatter", in_specs=[ANY] * (n + 1), out_specs=[ANY] * (n + 1), out_shape=outs,
        scratch_shapes=[pltpu.SemaphoreType.DMA((3 * n + 3,)), pltpu.SemaphoreType.DMA((3 * n + 3,))],
        compiler_params=pltpu.CompilerParams(has_side_effects=True),
    )(*sums, small)


def _final_exchange(reds, small):
    n = len(reds)
    flips = [(fx, fy, fc) for fx in (0, 1) for fy in (0, 1) for fc in (0, 1)][1:]

    def body(*refs):
        g_refs, o_refs = refs[:n + 1], refs[n + 1:2 * n + 2]
        ssem, rsem = refs[2 * n + 2:]
        x, y, c, _ = _place()
        cps = []
        for a in range(n):
            lh = g_refs[a].shape[0] // 2
            cp = _remote(g_refs[a].at[pl.ds(c * lh, lh)], o_refs[a].at[pl.ds(c * lh, lh)], ssem.at[a], rsem.at[a], (x, y, 1 - c))
            cp.start()
            cps.append(cp)
        mine = 4 * c + 2 * x + y
        for f, (fx, fy, fc) in enumerate(flips):
            cp = _remote(g_refs[n].at[mine], o_refs[n].at[mine], ssem.at[n + f], rsem.at[n + f], (x ^ fx, y ^ fy, c ^ fc))
            cp.start()
            cps.append(cp)
        for cp in cps:
            cp.wait()

    return pl.pallas_call(
        body, name="final_exchange", in_specs=[ANY] * (n + 1), out_specs=[ANY] * (n + 1),
        out_shape=[jax.ShapeDtypeStruct(g.shape, g.dtype) for g in reds] + [jax.ShapeDtypeStruct(small.shape, small.dtype)],
        input_output_aliases={a: a for a in range(n + 1)},
        scratch_shapes=[pltpu.SemaphoreType.DMA((n + 7,)), pltpu.SemaphoreType.DMA((n + 7,))],
        compiler_params=pltpu.CompilerParams(has_side_effects=True),
    )(*reds, small)


def _to_wire(g):
    _, a_dim, b_dim = g.shape
    tr = min(256, a_dim)

    def body(g_ref, o_ref):
        o_ref[...] = g_ref[...].astype(o_ref.dtype)

    blk = pl.BlockSpec((1, tr, b_dim), lambda i: (0, i, 0))
    return pl.pallas_call(
        body, name="to_wire", grid=(a_dim // tr,), in_specs=[blk], out_specs=blk,
        out_shape=jax.ShapeDtypeStruct(g.shape, WIRE_DTYPE), compiler_params=_cparams(("parallel",)),
    )(g)


def _chip_sum(own, got, ax, chip, layer, n_layers, acc):
    _, _, a_dim, b_dim = got.shape
    tr = min(256, a_dim)
    per = a_dim // tr
    if ax == 2:
        own_spec = pl.BlockSpec((1, tr, b_dim), lambda i, k: (0, i, k[0]))
    else:
        own_spec = pl.BlockSpec((1, tr, b_dim), lambda i, k: (0, k[0] * per + i, 0))
    chained = acc is not None

    def body(k_ref, own_ref, got_ref, *rest):
        rest[-1][...] = ((own_ref[...].astype(F32) + got_ref[0].astype(F32)) + got_ref[1].astype(F32)) + got_ref[2].astype(F32)

    return pl.pallas_call(
        body, name="chip_sum",
        grid_spec=pltpu.PrefetchScalarGridSpec(
            num_scalar_prefetch=1, grid=(per,),
            in_specs=[own_spec, pl.BlockSpec((3, 1, tr, b_dim), lambda i, k: (0, 0, i, 0))] + [ANY] * chained,
            out_specs=pl.BlockSpec((1, tr, b_dim), lambda i, k: (layer, i, 0))),
        out_shape=jax.ShapeDtypeStruct((n_layers, a_dim, b_dim), F32),
        input_output_aliases={3: 0} if chained else {},
        compiler_params=_cparams(("parallel",)),
    )(chip, own, got, *([acc] if chained else []))


def _small_pair_sum(small, got, half):
    rh = small.shape[0] // 2

    def body(h_ref, g_ref, o_ref, s_ref):
        s_ref[...] = g_ref[...] + o_ref[...]

    return pl.pallas_call(
        body, name="small_pair_sum",
        grid_spec=pltpu.PrefetchScalarGridSpec(
            num_scalar_prefetch=1, grid=(1,),
            in_specs=[pl.BlockSpec((rh, 128), lambda i, h: (h[0], 0)), pl.BlockSpec((rh, 128), lambda i, h: (0, 0))],
            out_specs=pl.BlockSpec((rh, 128), lambda i, h: (0, 0))),
        out_shape=jax.ShapeDtypeStruct((rh, 128), F32),
    )(half, small, got)


def _small_chip_sum(s1, got, pos):
    rq = got.shape[1]

    def body(k_ref, own_ref, got_ref, s_ref):
        s_ref[0] = ((own_ref[...] + got_ref[0]) + got_ref[1]) + got_ref[2]

    return pl.pallas_call(
        body, name="small_chip_sum",
        grid_spec=pltpu.PrefetchScalarGridSpec(
            num_scalar_prefetch=1, grid=(1,),
            in_specs=[pl.BlockSpec((rq, 128), lambda i, k: (k[0], 0)), pl.BlockSpec((3, rq, 128), lambda i, k: (0, 0, 0))],
            out_specs=pl.BlockSpec((1, rq, 128), lambda i, k: (4 * k[1] + k[0], 0, 0))),
        out_shape=jax.ShapeDtypeStruct((8, rq, 128), F32),
    )(pos, s1, got)


def _adam_math(w, g, m, v):
    m = ADAM_B1 * m + (1.0 - ADAM_B1) * g
    v = ADAM_B2 * v + (1.0 - ADAM_B2) * (g * g)
    m_hat = m / (1.0 - ADAM_B1 ** ADAM_STEP)
    v_hat = v / (1.0 - ADAM_B2 ** ADAM_STEP)
    return -ADAM_LR * (m_hat / (jnp.sqrt(v_hat) + ADAM_EPS) + ADAM_WD * w), m, v


def _adamw_big(w, g_mine, g_other, m, v):
    l_dim, a_dim, b_dim = w.shape
    tr = min(256, a_dim)

    def body(w_ref, g1_ref, g2_ref, m_ref, v_ref, g_ref, d_ref, nm_ref, nv_ref):
        g = g1_ref[...] + g2_ref[...]
        g_ref[...] = g
        d_ref[...], nm_ref[...], nv_ref[...] = _adam_math(w_ref[...], g, m_ref[...], v_ref[...])

    blk = pl.BlockSpec((1, tr, b_dim), lambda l, i: (l, i, 0))
    return pl.pallas_call(
        body, name="adamw_big", grid=(l_dim, a_dim // tr), in_specs=[blk] * 5, out_specs=[blk] * 4,
        out_shape=[jax.ShapeDtypeStruct(w.shape, F32)] * 4,
        compiler_params=_cparams(("parallel", "parallel")),
    )(w, g_mine, g_other, m, v)


def _adamw_small(ws, gs, ms, vs):
    n = len(ws)

    def body(*refs):
        for i in range(n):
            w_ref, g_ref, m_ref, v_ref = refs[i], refs[n + i], refs[2 * n + i], refs[3 * n + i]
            d_ref, nm_ref, nv_ref = refs[4 * n + i], refs[5 * n + i], refs[6 * n + i]
            d_ref[...], nm_ref[...], nv_ref[...] = _adam_math(w_ref[...], g_ref[...], m_ref[...], v_ref[...])

    shapes = [jax.ShapeDtypeStruct(w.shape, F32) for w in ws]
    outs = pl.pallas_call(body, name="adamw_small", out_shape=shapes * 3,
                          compiler_params=_cparams())(*ws, *gs, *ms, *vs)
    return outs[:n], outs[n:2 * n], outs[2 * n:]


def _pack(arrs, row_mult):
    parts = []
    for a in arrs:
        flat = a.reshape(-1)
        pad = (-flat.shape[0]) % 1024
        parts.append(jnp.pad(flat, (0, pad)).reshape(-1, 128))
    buf = jnp.concatenate(parts, axis=0)
    pad = (-buf.shape[0]) % row_mult
    return jnp.pad(buf, ((0, pad), (0, 0)))


def _unpack(buf, shapes):
    out, row = [], 0
    for sh in shapes:
        n = math.prod(sh)
        rows = -(-n // 1024) * 8
        out.append(buf[row:row + rows].reshape(-1)[:n].reshape(sh))
        row += rows
    return out


_NAMES = ['w_in_e', 'b_in_e', 'conv_a_w', 'conv_a_b', 'ln_a_g', 'ln_a_b', 'ln_v_g', 'ln_v_b', 'w_s', 'b_s', 'w_out_e', 'b_out_e',
          'w_in_o', 'b_in_o', 'w_pool', 'pool_scale', 'conv_d_w', 'w_out_o', 'b_out_o', 'ln_g', 'ln_b', 'w_ple', 'w_ple_gate',
          'b_ple_gate']
_BIG = ['w_in_e', 'w_out_e', 'w_in_o', 'w_out_o', 'w_ple', 'w_ple_gate']
_BIG_AXES = [2, 1, 2, 1, 2, 1]
_SMALL_SHARDED = ['conv_a_w', 'b_in_o', 'pool_scale', 'conv_d_w', 'b_out_o']


def kernel(x, p, w_in_e, b_in_e, conv_a_w, conv_a_b, ln_a_g, ln_a_b, ln_v_g, ln_v_b, w_s, b_s, w_out_e, b_out_e, w_in_o, b_in_o, w_pool, pool_scale, conv_d_w, w_out_o, b_out_o, ln_g, ln_b, w_ple, w_ple_gate, b_ple_gate, loss_target, m_w_in_e, m_b_in_e, m_conv_a_w, m_conv_a_b, m_ln_a_g, m_ln_a_b, m_ln_v_g, m_ln_v_b, m_w_s, m_b_s, m_w_out_e, m_b_out_e, m_w_in_o, m_b_in_o, m_w_pool, m_pool_scale, m_conv_d_w, m_w_out_o, m_b_out_o, m_ln_g, m_ln_b, m_w_ple, m_w_ple_gate, m_b_ple_gate, v_w_in_e, v_b_in_e, v_conv_a_w, v_conv_a_b, v_ln_a_g, v_ln_a_b, v_ln_v_g, v_ln_v_b, v_w_s, v_b_s, v_w_out_e, v_b_out_e, v_w_in_o, v_b_in_o, v_w_pool, v_pool_scale, v_conv_d_w, v_w_out_o, v_b_out_o, v_ln_g, v_ln_b, v_w_ple, v_w_ple_gate, v_b_ple_gate):
    args = locals()
    wts = {n: args[n] for n in _NAMES}
    mom = {n: args["m_" + n] for n in _NAMES}
    var = {n: args["v_" + n] for n in _NAMES}
    bsz, s_len, d = x.shape
    t = bsz * s_len
    cx, cy, cc = lax.axis_index("x"), lax.axis_index("y"), lax.axis_index("c")
    chip = (2 * cx + cy).astype(jnp.int32).reshape(1)
    half = cc.astype(jnp.int32).reshape(1)
    pos = jnp.concatenate([chip, half])

    def placed(name, ax, layer):
        return _place_shard(wts[name][layer:layer + 1], ax, chip)

    sv = _pack([wts[n] for n in _SMALL_SHARDED], 8)
    first_refs = _gather_async("gather_first", DEPTH, [placed('w_in_e', 2, 0), _place_shard(sv[None], 1, chip, F32)], [2, 1])
    layer_refs = []
    for i in range(DEPTH):
        sfx = '_e' if i % 2 == 0 else '_o'
        items = [('w_out' + sfx, 1, i // 2), ('w_ple_gate', 1, i), ('w_ple', 2, i)] + ([('w_in' + sfx, 2, i // 2)] if i else [])
        layer_refs.append(_gather_async("gather_layer%d" % i, i, [placed(*it) for it in items], [it[1] for it in items]))
    fw = {}
    w_in_first = first_refs[0][...]
    sv_all = first_refs[1][...].reshape((4,) + sv.shape)
    small_parts = [_unpack(sv_all[k], [wts[n].shape for n in _SMALL_SHARDED]) for k in range(4)]
    for i, n in enumerate(_SMALL_SHARDED):
        fw[n] = jnp.concatenate([small_parts[k][i] for k in range(4)], axis=-1)
    for n in _NAMES:
        fw.setdefault(n, wts[n])

    def row8(rows, width):
        rows = [r.reshape(1, width) for r in rows]
        return jnp.concatenate(rows + [jnp.zeros((8 - len(rows), width), F32)], axis=0)

    x2 = x.reshape(t, d)
    saved = []
    for i in range(DEPTH):
        j = i // 2
        even = i % 2 == 0
        b_in, b_out = (fw['b_in_e'], fw['b_out_e']) if even else (fw['b_in_o'], fw['b_out_o'])
        w_in = layer_refs[i][3][...][0] if i else w_in_first[0]
        z = _in_proj(x2, w_in, b_in[j].reshape(1, N_COLS))
        z3 = z.reshape(bsz, s_len, N_COLS)
        if even:
            cw = jnp.concatenate([fw['conv_a_w'][j], jnp.zeros((1, W_BR), F32)], axis=0)
            mvec = row8([fw['conv_a_b'][j], fw['ln_a_g'][j], fw['ln_a_b'][j], fw['ln_v_g'][j], fw['ln_v_b'][j]], W_BR)
            bsf = jnp.repeat(fw['b_s'][j].T, W_BR // 8, axis=1)
            mix = (cw, mvec, fw['w_s'][j], jnp.swapaxes(fw['w_s'][j], 1, 2), bsf)
            y3 = _even_fwd(z3, cw, mvec, fw['w_s'][j], bsf)
        else:
            mvec = row8([fw['pool_scale'][j]] + [fw['conv_d_w'][j][k] for k in range(3)], W_BR)
            mix = (fw['w_pool'][j].astype(MM_DTYPE), mvec)
            y3 = _odd_fwd(z3, mix[0], mvec)
        pvec = row8([b_out[j], fw['ln_g'][i], fw['ln_b'][i], fw['b_ple_gate'][i]], d)
        post_w = (layer_refs[i][0][...][0], layer_refs[i][1][...][0], layer_refs[i][2][...][0], pvec)
        p2 = p[i].reshape(t, D_PLE)
        y2 = y3.reshape(t, 2 * W_BR)
        if i < DEPTH - 1:
            xn, r2, gate2 = _post_fwd(y2, x2, p2, *post_w)
        else:
            dx, r2, gate2, sq = _post_fwd(y2, x2, p2, *post_w, tgt=loss_target.reshape(t, d))
        saved.append((x2, z3, y2, r2, gate2, p2, w_in, mix, post_w))
        x2 = xn

    loss = lax.psum(0.5 * jnp.sum(sq) / d, ("x", "y", "c"))

    gr = {n: [None] * wts[n].shape[0] for n in _NAMES}
    for i in reversed(range(DEPTH)):
        j = i // 2
        even = i % 2 == 0
        x_in, z3, y2, r2, gate2, p2, w_in, mix, post_w = saved[i]
        dxr, dy, dwo, dwg, dwp, dpv = _post_bwd(dx, r2, gate2, p2, y2, *post_w)
        dy3 = dy.reshape(bsz, s_len, 2 * W_BR)
        sfx = '_e' if even else '_o'
        gr['w_out' + sfx][j], gr['b_out' + sfx][j] = dwo, dpv[0]
        gr['w_ple_gate'][i], gr['w_ple'][i] = dwg, dwp
        gr['ln_g'][i], gr['ln_b'][i], gr['b_ple_gate'][i] = dpv[1], dpv[2], dpv[3]
        if even:
            dz3, dbin, dcw, dmv, dws, dbsf = _even_bwd(z3, dy3, *mix)
            gr['conv_a_w'][j], gr['conv_a_b'][j] = dcw[:KA], dmv[0]
            gr['ln_a_g'][j], gr['ln_a_b'][j], gr['ln_v_g'][j], gr['ln_v_b'][j] = dmv[1], dmv[2], dmv[3], dmv[4]
            gr['w_s'][j] = dws
            gr['b_s'][j] = jnp.sum(dbsf.reshape(GBLK, 8, W_BR // 8), axis=2).T
        else:
            dz3, dbin, dwpool, dmv = _odd_bwd(z3, dy3, *mix)
            gr['w_pool'][j], gr['pool_scale'][j], gr['conv_d_w'][j] = dwpool, dmv[0], dmv[1:4]
        gr['b_in' + sfx][j] = dbin[0]
        dz2 = dz3.reshape(t, N_COLS)
        dwi = _in_proj_bwd_dw(x_in, dz2)
        post_items = [('w_out' + sfx, j, dwo[None], 1), ('w_ple_gate', i, dwg[None], 1), ('w_ple', i, dwp[None], 2)]
        in_item = ('w_in' + sfx, j, dwi[None], 2)
        batches = [post_items + [in_item]] if i else [post_items, [in_item]]
        for bi, items in enumerate(batches):
            sent = [_to_wire(it[2]) if (i == 0 and bi == 1) else it[2] for it in items]
            lands = _scatter_async("scatter_layer%d_%d" % (i, bi), DEPTH + 1 + 2 * i + bi, sent, [it[3] for it in items])
            for it, land in zip(items, lands):
                gr[it[0]][it[1]] = (it[2], land, it[3])
        dx = _in_proj_bwd_dx(dxr, dz2, w_in)
    grad_x = dx.reshape(bsz, s_len, d)

    sums = []
    for n in _BIG:
        acc = None
        for l, (own, land, ax) in enumerate(gr[n]):
            acc = _chip_sum(own, land, ax, chip, l, len(gr[n]), acc)
        sums.append(acc)
    others = _pair_swap(sums)
    small_names = [n for n in _NAMES if n not in _BIG]
    g_small_full = [jnp.stack(gr[n]) for n in small_names]
    small = _pack(g_small_full, 64)
    (got_small,) = _pair_exchange([], small)
    s1 = _small_pair_sum(small, got_small, half)
    (got_small2,) = _chip_scatter([], [], s1)
    s2 = _small_chip_sum(s1, got_small2, pos)
    (small_all,) = _final_exchange([], s2)
    g_small = _unpack(small_all.reshape(small.shape), [g.shape for g in g_small_full])
    grads = {}
    for n, g in zip(small_names, g_small):
        if n in _SMALL_SHARDED:
            w = wts[n].shape[-1]
            g = lax.dynamic_slice_in_dim(g, (2 * cx + cy) * w, w, axis=g.ndim - 1)
        grads[n] = g

    delta, new_m, new_v = {}, {}, {}
    for n, mine, other in zip(_BIG, sums, others):
        grads[n], delta[n], new_m[n], new_v[n] = _adamw_big(wts[n], mine, other, mom[n], var[n])
    ds, ms, vs = _adamw_small([wts[n] for n in small_names], [grads[n] for n in small_names],
                              [mom[n] for n in small_names], [var[n] for n in small_names])
    for n, a, b, c_ in zip(small_names, ds, ms, vs):
        delta[n], new_m[n], new_v[n] = a, b, c_

    return (loss, grad_x, *[grads[n] for n in _NAMES], *[delta[n] for n in _NAMES],
            *[new_m[n] for n in _NAMES], *[new_v[n] for n in _NAMES])
```

```python
import functools
import math

import jax
import jax.numpy as jnp
from jax import lax
from jax.experimental import pallas as pl
from jax.experimental.pallas import tpu as pltpu
from jax.experimental.pallas import tpu_sc as plsc

F32 = jnp.float32
MM_DTYPE = jnp.bfloat16
WIRE_DTYPE = jnp.bfloat16
SEQ_TILE = 512
ROW_TILE = 512
BWD_ROW_TILE = 256
HALO = 32
CONV_CHUNK = 32
GBLK = 128
VMEM_LIMIT = 56 * 1024 * 1024

D_MODEL = 1024
W_BR = 512
N_COLS = 6 * W_BR
D_PLE = 256
KA = 31
DEPTH = 4
POOL_WINDOWS = (2, 4, 8, 16)
ALPHA = (2.0 * DEPTH) ** 0.25
LN_EPS = 1e-5
GELU_C = math.sqrt(2.0 / math.pi)

ADAM_LR, ADAM_B1, ADAM_B2, ADAM_EPS, ADAM_WD, ADAM_STEP = 0.001, 0.9, 0.999, 1e-08, 0.01, 10

MESH = pl.DeviceIdType.MESH
ANY = pl.BlockSpec(memory_space=pl.ANY)


def _cparams(sem=None):
    return pltpu.CompilerParams(dimension_semantics=sem, vmem_limit_bytes=VMEM_LIMIT)


def _sigmoid(x):
    return 1.0 / (1.0 + jnp.exp(-x))


def _silu(x):
    return x * _sigmoid(x)


def _silu_grad(x):
    s = _sigmoid(x)
    return x * s, s * (1.0 + x * (1.0 - s))


def _gelu(x):
    return 0.5 * x * (1.0 + jnp.tanh(GELU_C * (x + 0.044715 * (x * x * x))))


def _gelu_grad(x):
    x2 = x * x
    th = jnp.tanh(GELU_C * (x + 0.044715 * (x * x2)))
    return 0.5 * x * (1.0 + th), 0.5 * (1.0 + th) + 0.5 * x * (1.0 - th * th) * (GELU_C * (1.0 + 3.0 * 0.044715 * x2))


def _ln_stats(x):
    mu = jnp.mean(x, axis=-1, keepdims=True)
    d = x - mu
    var = jnp.mean(d * d, axis=-1, keepdims=True)
    rs = lax.rsqrt(var + LN_EPS)
    return d * rs, rs


def _ln_bwd(dxh, xh, rs):
    return rs * (dxh - jnp.mean(dxh, axis=-1, keepdims=True) - xh * jnp.mean(dxh * xh, axis=-1, keepdims=True))


def _mm(a):
    return a.astype(MM_DTYPE)


def _dot(a, b):
    return jnp.dot(_mm(a), _mm(b), preferred_element_type=F32)


def _dot_nt(a, b):
    return lax.dot_general(_mm(a), _mm(b), (((1,), (1,)), ((), ())), preferred_element_type=F32)


def _dot_tn(a, b):
    return lax.dot_general(_mm(a), _mm(b), (((0,), (0,)), ((), ())), preferred_element_type=F32)


def _rowsum(x):
    return jnp.sum(x, axis=0, keepdims=True)


def _in_proj(x2, w, b):
    t, d = x2.shape
    n = w.shape[1]
    tm = min(ROW_TILE, t)
    nc = 768

    def body(x_ref, w_ref, b_ref, z_ref):
        xb = _mm(x_ref[...])
        for j in range(n // nc):
            cs = slice(j * nc, (j + 1) * nc)
            z_ref[:, cs] = jnp.dot(xb, w_ref[:, cs], preferred_element_type=F32) + b_ref[:, cs]

    return pl.pallas_call(
        body, name="in_proj", grid=(t // tm,),
        in_specs=[pl.BlockSpec((tm, d), lambda i: (i, 0)), pl.BlockSpec((d, n), lambda i: (0, 0)),
                  pl.BlockSpec((1, n), lambda i: (0, 0))],
        out_specs=pl.BlockSpec((tm, n), lambda i: (i, 0)),
        out_shape=jax.ShapeDtypeStruct((t, n), F32),
        compiler_params=_cparams(("parallel",)),
    )(x2, w, b)


def _in_proj_bwd_dx(dxr, dz, w, after):
    t, d = dxr.shape
    n = w.shape[1]
    tm = min(ROW_TILE, t)

    def body(dxr_ref, dz_ref, w_ref, after_ref, dx_ref):
        dx_ref[...] = dxr_ref[...] + _dot_nt(dz_ref[...], w_ref[...])

    return pl.pallas_call(
        body, name="in_proj_bwd_dx", grid=(t // tm,),
        in_specs=[pl.BlockSpec((tm, d), lambda i: (i, 0)), pl.BlockSpec((tm, n), lambda i: (i, 0)),
                  pl.BlockSpec((d, n), lambda i: (0, 0)), pl.BlockSpec((8, 128), lambda i: (0, 0))],
        out_specs=pl.BlockSpec((tm, d), lambda i: (i, 0)),
        out_shape=jax.ShapeDtypeStruct((t, d), F32),
        compiler_params=_cparams(("parallel",)),
    )(dxr, dz, w, after)


def _in_proj_bwd_dw(x2, dz):
    t, d = x2.shape
    n = dz.shape[1]
    tm = min(ROW_TILE, t)
    nc = 768

    def body(x_ref, dz_ref, dw_ref, done_ref):
        @pl.when(pl.program_id(0) == 0)
        def _():
            dw_ref[...] = jnp.zeros_like(dw_ref)
            done_ref[...] = jnp.zeros_like(done_ref)
        xb = _mm(x_ref[...])
        for j in range(n // nc):
            cs = slice(j * nc, (j + 1) * nc)
            dw_ref[:, cs] += _dot_tn(xb, dz_ref[:, cs])

    return pl.pallas_call(
        body, name="in_proj_bwd_dw", grid=(t // tm,),
        in_specs=[pl.BlockSpec((tm, d), lambda i: (i, 0)), pl.BlockSpec((tm, n), lambda i: (i, 0))],
        out_specs=[pl.BlockSpec((d, n), lambda i: (0, 0)), pl.BlockSpec((8, 128), lambda i: (0, 0))],
        out_shape=[jax.ShapeDtypeStruct((d, n), F32), jax.ShapeDtypeStruct((8, 128), F32)],
        compiler_params=_cparams(("arbitrary",)),
    )(x2, dz)


def _halo_specs(ts, s_len, cols, left=True):
    per = ts // HALO
    last = s_len // HALO - 1
    if left:
        return pl.BlockSpec((1, HALO, cols), lambda b, s: (b, jnp.maximum(s * per - 1, 0), 0))
    return pl.BlockSpec((1, HALO, cols), lambda b, s: (b, jnp.minimum((s + 1) * per, last), 0))


def _build_shifts(src_ref, sh_ref, rows):
    for r in range(1, 8):
        sh_ref[r - 1, 0:rows, :] = src_ref[r:r + rows, :]


def _shifted(src_ref, sh_ref, r, start, n):
    if r == 0:
        return src_ref[pl.ds(start, n), :]
    return sh_ref[r - 1, pl.ds(start, n), :]


def _tril_masks():
    ri = lax.broadcasted_iota(jnp.int32, (GBLK, GBLK), 0)
    ci = lax.broadcasted_iota(jnp.int32, (GBLK, GBLK), 1)
    return ri >= ci, ci >= ri


def _spatial(w_ref, keep, vb):
    lane = lax.broadcasted_iota(jnp.int32, (GBLK, GBLK), 1)
    outs = []
    for p in range(4):
        xs = vb[:, p * GBLK:(p + 1) * GBLK]
        r0 = jnp.dot(_mm(jnp.where(keep, w_ref[2 * p], 0.0)), xs, preferred_element_type=F32)
        r1 = jnp.dot(_mm(jnp.where(keep, w_ref[2 * p + 1], 0.0)), xs, preferred_element_type=F32)
        outs.append(jnp.where(lane < 64, r0, r1))
    return jnp.concatenate(outs, axis=1)


def _even_fwd(z3, cw, vec, ws, bsf):
    bsz, s_len, _ = z3.shape
    ts = min(SEQ_TILE, s_len)
    ext_rows = ts + HALO

    def body(z_ref, zl_ref, cw_ref, vec_ref, ws_ref, bsf_ref, y_ref, ext_ref, sh_ref, a1_ref):
        s = pl.program_id(1)
        hl = zl_ref[0]
        a0h = hl[:, 0:W_BR] * _sigmoid(hl[:, W_BR:2 * W_BR])
        ext_ref[0:HALO, :] = jnp.where(s > 0, a0h, 0.0)
        ext_ref[HALO:ext_rows, :] = z_ref[0, :, 0:W_BR] * _sigmoid(z_ref[0, :, W_BR:2 * W_BR])
        ext_ref[ext_rows:ext_rows + 8, :] = jnp.zeros((8, W_BR), F32)
        _build_shifts(ext_ref, sh_ref, ext_rows)

        def conv_chunk(ci, carry):
            base = pl.multiple_of(ci * CONV_CHUNK, CONV_CHUNK)
            acc = jnp.zeros((CONV_CHUNK, W_BR), F32) + vec_ref[0:1, :]
            for k in range(KA):
                q, r = divmod(2 + k, 8)
                acc = acc + _shifted(ext_ref, sh_ref, r, base + 8 * q, CONV_CHUNK) * cw_ref[k:k + 1, :]
            a1_ref[pl.ds(base, CONV_CHUNK), :] = acc
            return carry

        lax.fori_loop(0, ts // CONV_CHUNK, conv_chunk, 0)
        keep, _ = _tril_masks()

        def block(bi, carry):
            rows = pl.ds(pl.multiple_of(bi * GBLK, GBLK), GBLK)
            xh, _ = _ln_stats(a1_ref[rows, :])
            a = _silu(xh * vec_ref[1:2, :] + vec_ref[2:3, :]) * _silu(z_ref[0, rows, 2 * W_BR:3 * W_BR])
            y_ref[0, rows, 0:W_BR] = a.astype(y_ref.dtype)
            ua = _gelu(z_ref[0, rows, 3 * W_BR:4 * W_BR])
            vh, _ = _ln_stats(_gelu(z_ref[0, rows, 4 * W_BR:5 * W_BR]))
            vl = vh * vec_ref[3:4, :] + vec_ref[4:5, :]
            sg = _spatial(ws_ref, keep, _mm(vl)) + bsf_ref[...]
            g = ua * sg * _silu(z_ref[0, rows, 5 * W_BR:6 * W_BR])
            y_ref[0, rows, W_BR:2 * W_BR] = g.astype(y_ref.dtype)
            return carry

        lax.fori_loop(0, ts // GBLK, block, 0)

    full = lambda shape: pl.BlockSpec(shape, lambda b, s: (0,) * len(shape))
    return pl.pallas_call(
        body, name="even_fwd", grid=(bsz, s_len // ts),
        in_specs=[pl.BlockSpec((1, ts, N_COLS), lambda b, s: (b, s, 0)), _halo_specs(ts, s_len, 2 * W_BR),
                  full((32, W_BR)), full((8, W_BR)), full((8, GBLK, GBLK)), full((GBLK, W_BR))],
        out_specs=pl.BlockSpec((1, ts, 2 * W_BR), lambda b, s: (b, s, 0)),
        out_shape=jax.ShapeDtypeStruct((bsz, s_len, 2 * W_BR), MM_DTYPE),
        scratch_shapes=[pltpu.VMEM((ext_rows + 8, W_BR), F32), pltpu.VMEM((7, ext_rows, W_BR), F32),
                        pltpu.VMEM((ts, W_BR), F32)],
        compiler_params=_cparams(("parallel", "parallel")),
    )(z3, z3, cw, vec, ws, bsf)


def _even_bwd(z3, dy3, cw, vec, ws, wst, bsf):
    bsz, s_len, _ = z3.shape
    ts = min(SEQ_TILE, s_len)
    n_s = s_len // ts
    ext_rows = ts + 2 * HALO
    a_rows = ts + HALO

    def body(z_ref, zl_ref, zr_ref, dy_ref, dyr_ref, cw_ref, vec_ref, ws_ref, wst_ref, bsf_ref,
             dz_ref, dbin_ref, dcw_ref, dvec_ref, dws_ref, dbsf_ref,
             ext_ref, sh_ref, a1_ref, ag_ref, dya_ref, d_ref, accw_ref):
        b = pl.program_id(0)
        s = pl.program_id(1)

        @pl.when((b == 0) & (s == 0))
        def _():
            dbin_ref[...] = jnp.zeros_like(dbin_ref)
            dcw_ref[...] = jnp.zeros_like(dcw_ref)
            dvec_ref[...] = jnp.zeros_like(dvec_ref)
            dws_ref[...] = jnp.zeros_like(dws_ref)
            dbsf_ref[...] = jnp.zeros_like(dbsf_ref)

        has_right = s < n_s - 1
        hl = zl_ref[0]
        hr = zr_ref[0]
        ext_ref[0:HALO, :] = jnp.where(s > 0, hl[:, 0:W_BR] * _sigmoid(hl[:, W_BR:2 * W_BR]), 0.0)
        ext_ref[HALO:HALO + ts, :] = z_ref[0, :, 0:W_BR] * _sigmoid(z_ref[0, :, W_BR:2 * W_BR])
        ext_ref[HALO + ts:ext_rows, :] = hr[:, 0:W_BR] * _sigmoid(hr[:, W_BR:2 * W_BR])
        ext_ref[ext_rows:ext_rows + 8, :] = jnp.zeros((8, W_BR), F32)
        ag_ref[0:ts, :] = z_ref[0, :, 2 * W_BR:3 * W_BR]
        ag_ref[ts:a_rows, :] = hr[:, 2 * W_BR:3 * W_BR]
        dya_ref[0:ts, :] = dy_ref[0, :, 0:W_BR]
        dya_ref[ts:a_rows, :] = dyr_ref[0][:, 0:W_BR]
        _build_shifts(ext_ref, sh_ref, ext_rows)

        def conv_chunk(ci, carry):
            base = pl.multiple_of(ci * CONV_CHUNK, CONV_CHUNK)
            acc = jnp.zeros((CONV_CHUNK, W_BR), F32) + vec_ref[0:1, :]
            for k in range(KA):
                q, r = divmod(2 + k, 8)
                acc = acc + _shifted(ext_ref, sh_ref, r, base + 8 * q, CONV_CHUNK) * cw_ref[k:k + 1, :]
            a1_ref[pl.ds(base, CONV_CHUNK), :] = acc
            return carry

        lax.fori_loop(0, a_rows // CONV_CHUNK, conv_chunk, 0)

        def a_chunk(base, n, main):
            rows = pl.ds(base, n)
            xh, rs = _ln_stats(a1_ref[rows, :])
            ln = xh * vec_ref[1:2, :] + vec_ref[2:3, :]
            sl, dsl = _silu_grad(ln)
            sgt, dsgt = _silu_grad(ag_ref[rows, :])
            dya = dya_ref[rows, :]
            dln = dya * sgt * dsl
            da1 = _ln_bwd(dln * vec_ref[1:2, :], xh, rs)
            if main:
                d_ref[rows, :] = da1
                dag = dya * sl * dsgt
                dz_ref[0, rows, 2 * W_BR:3 * W_BR] = dag.astype(dz_ref.dtype)
                dbin_ref[0:1, 2 * W_BR:3 * W_BR] += _rowsum(dag)
                dvec_ref[0:1, :] += _rowsum(da1)
                dvec_ref[1:2, :] += _rowsum(dln * xh)
                dvec_ref[2:3, :] += _rowsum(dln)
            else:
                d_ref[rows, :] = jnp.where(has_right, da1, 0.0)

        def a_main(ci, carry):
            a_chunk(pl.multiple_of(ci * GBLK, GBLK), GBLK, True)
            return carry

        lax.fori_loop(0, ts // GBLK, a_main, 0)
        a_chunk(ts, HALO, False)
        d_ref[a_rows:a_rows + 8, :] = jnp.zeros((8, W_BR), F32)

        accw_ref[...] = jnp.zeros_like(accw_ref)

        def dw_chunk(ci, carry):
            base = pl.multiple_of(ci * CONV_CHUNK, CONV_CHUNK)
            d = d_ref[pl.ds(base, CONV_CHUNK), :]
            for k in range(KA):
                q, r = divmod(2 + k, 8)
                prod = d * _shifted(ext_ref, sh_ref, r, base + 8 * q, CONV_CHUNK)
                accw_ref[k] += jnp.sum(prod.reshape(CONV_CHUNK // 8, 8, W_BR), axis=0)
            return carry

        lax.fori_loop(0, ts // CONV_CHUNK, dw_chunk, 0)
        dcw_ref[...] += jnp.sum(accw_ref[...], axis=1)

        _build_shifts(d_ref, sh_ref, a_rows)

        def dx_chunk(ci, carry):
            base = pl.multiple_of(ci * CONV_CHUNK, CONV_CHUNK)
            rows = pl.ds(base, CONV_CHUNK)
            acc = jnp.zeros((CONV_CHUNK, W_BR), F32)
            for m in range(KA):
                q, r = divmod(m, 8)
                acc = acc + _shifted(d_ref, sh_ref, r, base + 8 * q, CONV_CHUNK) * cw_ref[KA - 1 - m:KA - m, :]
            aval = z_ref[0, rows, 0:W_BR]
            sg = _sigmoid(z_ref[0, rows, W_BR:2 * W_BR])
            dval = acc * sg
            dglu = acc * aval * sg * (1.0 - sg)
            dz_ref[0, rows, 0:W_BR] = dval.astype(dz_ref.dtype)
            dz_ref[0, rows, W_BR:2 * W_BR] = dglu.astype(dz_ref.dtype)
            dbin_ref[0:1, 0:W_BR] += _rowsum(dval)
            dbin_ref[0:1, W_BR:2 * W_BR] += _rowsum(dglu)
            return carry

        lax.fori_loop(0, ts // CONV_CHUNK, dx_chunk, 0)

        keep, keep_t = _tril_masks()
        lane = lax.broadcasted_iota(jnp.int32, (GBLK, GBLK), 1)

        def block(bi, carry):
            rows = pl.ds(pl.multiple_of(bi * GBLK, GBLK), GBLK)
            ua, dua = _gelu_grad(z_ref[0, rows, 3 * W_BR:4 * W_BR])
            va, dva = _gelu_grad(z_ref[0, rows, 4 * W_BR:5 * W_BR])
            sgt, dsgt = _silu_grad(z_ref[0, rows, 5 * W_BR:6 * W_BR])
            vh, rs = _ln_stats(va)
            vlb = _mm(vh * vec_ref[3:4, :] + vec_ref[4:5, :])
            sg = _spatial(ws_ref, keep, vlb) + bsf_ref[...]
            dyg = dy_ref[0, rows, W_BR:2 * W_BR]
            du = dyg * sg * sgt * dua
            dsg = dyg * ua * sgt
            dgg = dyg * ua * sg * dsgt
            dvl = _spatial(wst_ref, keep_t, _mm(dsg))
            for p in range(4):
                dsp = dsg[:, p * GBLK:(p + 1) * GBLK]
                vlp = vlb[:, p * GBLK:(p + 1) * GBLK]
                dws_ref[2 * p] += jnp.where(keep, _dot_nt(jnp.where(lane < 64, dsp, 0.0), vlp), 0.0)
                dws_ref[2 * p + 1] += jnp.where(keep, _dot_nt(jnp.where(lane >= 64, dsp, 0.0), vlp), 0.0)
            dbsf_ref[...] += dsg
            dvec_ref[3:4, :] += _rowsum(dvl * vh)
            dvec_ref[4:5, :] += _rowsum(dvl)
            dv = _ln_bwd(dvl * vec_ref[3:4, :], vh, rs) * dva
            dz_ref[0, rows, 3 * W_BR:4 * W_BR] = du.astype(dz_ref.dtype)
            dz_ref[0, rows, 4 * W_BR:5 * W_BR] = dv.astype(dz_ref.dtype)
            dz_ref[0, rows, 5 * W_BR:6 * W_BR] = dgg.astype(dz_ref.dtype)
            dbin_ref[0:1, 3 * W_BR:4 * W_BR] += _rowsum(du)
            dbin_ref[0:1, 4 * W_BR:5 * W_BR] += _rowsum(dv)
            dbin_ref[0:1, 5 * W_BR:6 * W_BR] += _rowsum(dgg)
            return carry

        lax.fori_loop(0, ts // GBLK, block, 0)

    full = lambda shape: pl.BlockSpec(shape, lambda b, s: (0,) * len(shape))
    acc_shapes = [(1, N_COLS), (32, W_BR), (8, W_BR), (8, GBLK, GBLK), (GBLK, W_BR)]
    return pl.pallas_call(
        body, name="even_bwd", grid=(bsz, n_s),
        in_specs=[pl.BlockSpec((1, ts, N_COLS), lambda b, s: (b, s, 0)),
                  _halo_specs(ts, s_len, N_COLS, True), _halo_specs(ts, s_len, N_COLS, False),
                  pl.BlockSpec((1, ts, 2 * W_BR), lambda b, s: (b, s, 0)), _halo_specs(ts, s_len, 2 * W_BR, False),
                  full((32, W_BR)), full((8, W_BR)), full((8, GBLK, GBLK)), full((8, GBLK, GBLK)), full((GBLK, W_BR))],
        out_specs=[pl.BlockSpec((1, ts, N_COLS), lambda b, s: (b, s, 0))] + [full(sh) for sh in acc_shapes],
        out_shape=[jax.ShapeDtypeStruct((bsz, s_len, N_COLS), MM_DTYPE)] + [jax.ShapeDtypeStruct(sh, F32) for sh in acc_shapes],
        scratch_shapes=[pltpu.VMEM((ext_rows + 8, W_BR), F32), pltpu.VMEM((7, ext_rows, W_BR), F32),
                        pltpu.VMEM((a_rows, W_BR), F32), pltpu.VMEM((a_rows, W_BR), F32), pltpu.VMEM((a_rows, W_BR), F32),
                        pltpu.VMEM((a_rows + 8, W_BR), F32), pltpu.VMEM((32, 8, W_BR), F32)],
        compiler_params=_cparams(("arbitrary", "arbitrary")),
    )(z3, z3, z3, dy3, dy3, cw, vec, ws, wst, bsf)


def _pool_stages(e_refs, rows):
    e0, e1, e2, e3, e4 = e_refs
    e1[8:rows, :] = e0[8:rows, :] + e0[7:rows - 1, :]
    e2[16:rows, GBLK:] = e1[16:rows, GBLK:] + e1[14:rows - 2, GBLK:]
    e3[24:rows, 2 * GBLK:] = e2[24:rows, 2 * GBLK:] + e2[20:rows - 4, 2 * GBLK:]
    e4[32:rows, 3 * GBLK:] = e3[32:rows, 3 * GBLK:] + e3[24:rows - 8, 3 * GBLK:]


def _pool_counts(start, n):
    pos = (start + 1 + lax.broadcasted_iota(jnp.int32, (n, 1), 0)).astype(F32)
    return [jnp.minimum(pos, float(w)) for w in POOL_WINDOWS]


def _pooled_into(e_refs, pooled_ref, s, ts):
    cnt = _pool_counts(s * ts, ts)
    for g in range(4):
        cs = slice(g * GBLK, (g + 1) * GBLK)
        pooled_ref[:, cs] = e_refs[g + 1][HALO:HALO + ts, cs] / cnt[g] - e_refs[0][HALO:HALO + ts, cs]


def _odd_fwd(z3, wp, vec):
    bsz, s_len, _ = z3.shape
    ts = min(SEQ_TILE, s_len)
    ext_rows = ts + HALO

    def body(z_ref, zl_ref, wp_ref, vec_ref, y_ref, e0, e1, e2, e3, e4, pooled_ref, dext_ref, ec_ref):
        s = pl.program_id(1)
        hl = zl_ref[0]
        e0[0:HALO, :] = jnp.where(s > 0, hl[:, 0:W_BR], 0.0)
        e0[HALO:ext_rows, :] = z_ref[0, :, 0:W_BR]
        _pool_stages((e0, e1, e2, e3, e4), ext_rows)
        _pooled_into((e0, e1, e2, e3, e4), pooled_ref, s, ts)
        dext_ref[0:HALO, :] = jnp.where(s > 0, hl[:, 2 * W_BR:3 * W_BR] * hl[:, 4 * W_BR:5 * W_BR], 0.0)
        dext_ref[HALO:ext_rows, :] = z_ref[0, :, 2 * W_BR:3 * W_BR] * z_ref[0, :, 4 * W_BR:5 * W_BR]
        ec_ref[...] = (vec_ref[1:2, :] * dext_ref[HALO - 2:HALO - 2 + ts, :] + vec_ref[2:3, :] * dext_ref[HALO - 1:HALO - 1 + ts, :]
                       + vec_ref[3:4, :] * dext_ref[HALO:HALO + ts, :])

        def block(bi, carry):
            rows = pl.ds(pl.multiple_of(bi * GBLK, GBLK), GBLK)
            pb = _mm(pooled_ref[rows, :])
            cpre = jnp.concatenate([jnp.dot(pb[:, g * GBLK:(g + 1) * GBLK], wp_ref[g], preferred_element_type=F32)
                                    for g in range(4)], axis=1)
            c = cpre * vec_ref[0:1, :] * _silu(z_ref[0, rows, W_BR:2 * W_BR])
            d = z_ref[0, rows, 3 * W_BR:4 * W_BR] * ec_ref[rows, :] * _silu(z_ref[0, rows, 5 * W_BR:6 * W_BR])
            y_ref[0, rows, 0:W_BR] = c.astype(y_ref.dtype)
            y_ref[0, rows, W_BR:2 * W_BR] = d.astype(y_ref.dtype)
            return carry

        lax.fori_loop(0, ts // GBLK, block, 0)

    full = lambda shape: pl.BlockSpec(shape, lambda b, s: (0,) * len(shape))
    ebuf = pltpu.VMEM((ext_rows, W_BR), F32)
    return pl.pallas_call(
        body, name="odd_fwd", grid=(bsz, s_len // ts),
        in_specs=[pl.BlockSpec((1, ts, N_COLS), lambda b, s: (b, s, 0)), _halo_specs(ts, s_len, N_COLS),
                  full((4, GBLK, GBLK)), full((8, W_BR))],
        out_specs=pl.BlockSpec((1, ts, 2 * W_BR), lambda b, s: (b, s, 0)),
        out_shape=jax.ShapeDtypeStruct((bsz, s_len, 2 * W_BR), MM_DTYPE),
        scratch_shapes=[ebuf, ebuf, ebuf, ebuf, ebuf, pltpu.VMEM((ts, W_BR), F32), ebuf, pltpu.VMEM((ts, W_BR), F32)],
        compiler_params=_cparams(("parallel", "parallel")),
    )(z3, z3, wp, vec)


def _odd_bwd(z3, dy3, wp, vec):
    bsz, s_len, _ = z3.shape
    ts = min(SEQ_TILE, s_len)
    n_s = s_len // ts
    ext_rows = ts + HALO

    def body(z_ref, zl_ref, zr_ref, dy_ref, dyr_ref, wp_ref, vec_ref,
             dz_ref, dbin_ref, dwp_ref, dvec_ref,
             e0, e1, e2, e3, e4, pooled_ref, dext_ref, ec_ref, q_ref, dp_ref, de_ref, f1, f2, f3, f4):
        b = pl.program_id(0)
        s = pl.program_id(1)

        @pl.when((b == 0) & (s == 0))
        def _():
            dbin_ref[...] = jnp.zeros_like(dbin_ref)
            dwp_ref[...] = jnp.zeros_like(dwp_ref)
            dvec_ref[...] = jnp.zeros_like(dvec_ref)

        has_right = s < n_s - 1
        hl = zl_ref[0]
        e0[0:HALO, :] = jnp.where(s > 0, hl[:, 0:W_BR], 0.0)
        e0[HALO:ext_rows, :] = z_ref[0, :, 0:W_BR]
        _pool_stages((e0, e1, e2, e3, e4), ext_rows)
        _pooled_into((e0, e1, e2, e3, e4), pooled_ref, s, ts)
        dext_ref[0:HALO, :] = jnp.where(s > 0, hl[:, 2 * W_BR:3 * W_BR] * hl[:, 4 * W_BR:5 * W_BR], 0.0)
        dext_ref[HALO:ext_rows, :] = z_ref[0, :, 2 * W_BR:3 * W_BR] * z_ref[0, :, 4 * W_BR:5 * W_BR]
        ec_ref[...] = (vec_ref[1:2, :] * dext_ref[HALO - 2:HALO - 2 + ts, :] + vec_ref[2:3, :] * dext_ref[HALO - 1:HALO - 1 + ts, :]
                       + vec_ref[3:4, :] * dext_ref[HALO:HALO + ts, :])

        def grads(zc_gate, zd_b, zd_gate, dyc, dyd, rows_out, n, start, valid):
            sgt = _silu(zc_gate)
            dcpre = dyc * vec_ref[0:1, :] * sgt
            db = _mm(dcpre)
            dpool = jnp.concatenate([_dot_nt(db[:, g * GBLK:(g + 1) * GBLK], wp_ref[g]) for g in range(4)], axis=1)
            cnt = _pool_counts(start, n)
            q = jnp.concatenate([dpool[:, g * GBLK:(g + 1) * GBLK] / cnt[g] for g in range(4)], axis=1)
            de = dyd * zd_b * _silu(zd_gate)
            if valid is not None:
                q = jnp.where(valid, q, 0.0)
                de = jnp.where(valid, de, 0.0)
            q_ref[rows_out, :] = q
            dp_ref[rows_out, :] = dpool
            de_ref[rows_out, :] = de
            return dcpre

        def block(bi, carry):
            base = pl.multiple_of(bi * GBLK, GBLK)
            rows = pl.ds(base, GBLK)
            cg = z_ref[0, rows, W_BR:2 * W_BR]
            dyc = dy_ref[0, rows, 0:W_BR]
            dyd = dy_ref[0, rows, W_BR:2 * W_BR]
            d_b = z_ref[0, rows, 3 * W_BR:4 * W_BR]
            d_gate = z_ref[0, rows, 5 * W_BR:6 * W_BR]
            dcpre = grads(cg, d_b, d_gate, dyc, dyd, rows, GBLK, s * ts + base, None)
            pb = _mm(pooled_ref[rows, :])
            dcb = _mm(dcpre)
            cpre = jnp.concatenate([jnp.dot(pb[:, g * GBLK:(g + 1) * GBLK], wp_ref[g], preferred_element_type=F32)
                                    for g in range(4)], axis=1)
            for g in range(4):
                cs = slice(g * GBLK, (g + 1) * GBLK)
                dwp_ref[g] += _dot_tn(pb[:, cs], dcb[:, cs])
            sgt, dsgt = _silu_grad(cg)
            dvec_ref[0:1, :] += _rowsum(dyc * cpre * sgt)
            dcg = dyc * cpre * vec_ref[0:1, :] * dsgt
            sdt, dsdt = _silu_grad(d_gate)
            ec = ec_ref[rows, :]
            ddb = dyd * ec * sdt
            ddg = dyd * d_b * ec * dsdt
            dz_ref[0, rows, W_BR:2 * W_BR] = dcg.astype(dz_ref.dtype)
            dz_ref[0, rows, 3 * W_BR:4 * W_BR] = ddb.astype(dz_ref.dtype)
            dz_ref[0, rows, 5 * W_BR:6 * W_BR] = ddg.astype(dz_ref.dtype)
            dbin_ref[0:1, W_BR:2 * W_BR] += _rowsum(dcg)
            dbin_ref[0:1, 3 * W_BR:4 * W_BR] += _rowsum(ddb)
            dbin_ref[0:1, 5 * W_BR:6 * W_BR] += _rowsum(ddg)
            return carry

        lax.fori_loop(0, ts // GBLK, block, 0)
        hr = zr_ref[0]
        dyr = dyr_ref[0]
        grads(hr[:, W_BR:2 * W_BR], hr[:, 3 * W_BR:4 * W_BR], hr[:, 5 * W_BR:6 * W_BR], dyr[:, 0:W_BR], dyr[:, W_BR:2 * W_BR],
              slice(ts, ext_rows), HALO, (s + 1) * ts, has_right)

        r1, r2, r3 = ts + 24, ts + 16, ts + 8
        f1[0:r1, :] = q_ref[0:r1, :] + q_ref[1:r1 + 1, :]
        f2[0:r2, GBLK:] = f1[0:r2, GBLK:] + f1[2:r2 + 2, GBLK:]
        f3[0:r3, 2 * GBLK:] = f2[0:r3, 2 * GBLK:] + f2[4:r3 + 4, 2 * GBLK:]
        f4[0:ts, 3 * GBLK:] = f3[0:ts, 3 * GBLK:] + f3[8:ts + 8, 3 * GBLK:]
        for g, f in enumerate((f1, f2, f3, f4)):
            cs = slice(g * GBLK, (g + 1) * GBLK)
            dvg = f[0:ts, cs] - dp_ref[0:ts, cs]
            dz_ref[0, :, cs] = dvg.astype(dz_ref.dtype)
            dbin_ref[0:1, cs] += _rowsum(dvg)

        ddc = (vec_ref[1:2, :] * de_ref[2:ts + 2, :] + vec_ref[2:3, :] * de_ref[1:ts + 1, :] + vec_ref[3:4, :] * de_ref[0:ts, :])
        d_h = z_ref[0, :, 2 * W_BR:3 * W_BR]
        d_c = z_ref[0, :, 4 * W_BR:5 * W_BR]
        ddh = ddc * d_c
        ddcc = ddc * d_h
        dz_ref[0, :, 2 * W_BR:3 * W_BR] = ddh.astype(dz_ref.dtype)
        dz_ref[0, :, 4 * W_BR:5 * W_BR] = ddcc.astype(dz_ref.dtype)
        dbin_ref[0:1, 2 * W_BR:3 * W_BR] += _rowsum(ddh)
        dbin_ref[0:1, 4 * W_BR:5 * W_BR] += _rowsum(ddcc)
        de = de_ref[0:ts, :]
        for k in range(3):
            dvec_ref[1 + k:2 + k, :] += _rowsum(de * dext_ref[HALO - 2 + k:HALO - 2 + k + ts, :])

    full = lambda shape: pl.BlockSpec(shape, lambda b, s: (0,) * len(shape))
    acc_shapes = [(1, N_COLS), (4, GBLK, GBLK), (8, W_BR)]
    ebuf = pltpu.VMEM((ext_rows, W_BR), F32)
    tbuf = pltpu.VMEM((ts, W_BR), F32)
    return pl.pallas_call(
        body, name="odd_bwd", grid=(bsz, n_s),
        in_specs=[pl.BlockSpec((1, ts, N_COLS), lambda b, s: (b, s, 0)),
                  _halo_specs(ts, s_len, N_COLS, True), _halo_specs(ts, s_len, N_COLS, False),
                  pl.BlockSpec((1, ts, 2 * W_BR), lambda b, s: (b, s, 0)), _halo_specs(ts, s_len, 2 * W_BR, False),
                  full((4, GBLK, GBLK)), full((8, W_BR))],
        out_specs=[pl.BlockSpec((1, ts, N_COLS), lambda b, s: (b, s, 0))] + [full(sh) for sh in acc_shapes],
        out_shape=[jax.ShapeDtypeStruct((bsz, s_len, N_COLS), MM_DTYPE)] + [jax.ShapeDtypeStruct(sh, F32) for sh in acc_shapes],
        scratch_shapes=[ebuf, ebuf, ebuf, ebuf, ebuf, tbuf, ebuf, tbuf, ebuf, ebuf, ebuf, ebuf, ebuf, ebuf, tbuf],
        compiler_params=_cparams(("arbitrary", "arbitrary")),
    )(z3, z3, z3, dy3, dy3, wp, vec)


def _post_fwd(y2, x2, p2, w_out, wg, wple, vec, tgt=None):
    t, d = x2.shape
    tm = min(ROW_TILE, t)
    last = tgt is not None

    def body(*refs):
        y_ref, x_ref, p_ref, wo_ref, wg_ref, wp_ref, vec_ref = refs[:7]
        xn_ref, r_ref, gate_ref = refs[7 + last:10 + last]
        r = ALPHA * x_ref[...] + jnp.dot(y_ref[...], wo_ref[...], preferred_element_type=F32) + vec_ref[0:1, :]
        r_ref[...] = r
        xh, _ = _ln_stats(r)
        h = xh * vec_ref[1:2, :] + vec_ref[2:3, :]
        gate = _sigmoid(_dot(h, wg_ref[...]) + vec_ref[3:4, :])
        gate_ref[...] = gate
        xn = h + gate * _dot(p_ref[...], wp_ref[...])
        if last:
            sq_ref = refs[11]

            @pl.when(pl.program_id(0) == 0)
            def _():
                sq_ref[...] = jnp.zeros_like(sq_ref)
            e = xn - refs[7][...]
            xn_ref[...] = e / float(d)
            sq_ref[...] += _rowsum(e * e)
        else:
            xn_ref[...] = xn

    row = lambda c: pl.BlockSpec((tm, c), lambda i: (i, 0))
    full = lambda shape: pl.BlockSpec(shape, lambda i: (0,) * len(shape))
    return pl.pallas_call(
        body, name="post_fwd_loss" if last else "post_fwd", grid=(t // tm,),
        in_specs=[row(d), row(d), row(D_PLE), full((d, d)), full((d, d)), full((D_PLE, d)), full((8, d))] + [row(d)] * last,
        out_specs=[row(d), row(d), row(d)] + [full((1, d))] * last,
        out_shape=[jax.ShapeDtypeStruct((t, d), F32)] * 3 + [jax.ShapeDtypeStruct((1, d), F32)] * last,
        compiler_params=_cparams(("arbitrary",) if last else ("parallel",)),
    )(y2, x2, p2, w_out, wg, wple, vec, *([tgt] if last else []))


def _post_bwd(dxn, r2, gate2, p2, y2, w_out, wg, wple, vec):
    t, d = r2.shape
    tm = min(BWD_ROW_TILE, t)

    def body(dxn_ref, r_ref, gate_ref, p_ref, y_ref, wo_ref, wg_ref, wp_ref, vec_ref,
             dxr_ref, dy_ref, dwo_ref, dwg_ref, dwp_ref, dvec_ref):
        @pl.when(pl.program_id(0) == 0)
        def _():
            dwo_ref[...] = jnp.zeros_like(dwo_ref)
            dwg_ref[...] = jnp.zeros_like(dwg_ref)
            dwp_ref[...] = jnp.zeros_like(dwp_ref)
            dvec_ref[...] = jnp.zeros_like(dvec_ref)

        dxn = dxn_ref[...]
        gate = gate_ref[...]
        xh, rs = _ln_stats(r_ref[...])
        hb = _mm(xh * vec_ref[1:2, :] + vec_ref[2:3, :])
        pb = _mm(p_ref[...])
        pe = jnp.dot(pb, wp_ref[...], preferred_element_type=F32)
        dpre = dxn * pe * gate * (1.0 - gate)
        dpb = _mm(dpre)
        dh = dxn + _dot_nt(dpb, wg_ref[...])
        dwg_ref[...] += _dot_tn(hb, dpb)
        dwp_ref[...] += _dot_tn(pb, dxn * gate)
        dr = _ln_bwd(dh * vec_ref[1:2, :], xh, rs)
        drb = _mm(dr)
        dxr_ref[...] = ALPHA * dr
        dy_ref[...] = _dot_nt(drb, wo_ref[...])
        dwo_ref[...] += _dot_tn(y_ref[...], drb)
        dvec_ref[0:1, :] += _rowsum(dr)
        dvec_ref[1:2, :] += _rowsum(dh * xh)
        dvec_ref[2:3, :] += _rowsum(dh)
        dvec_ref[3:4, :] += _rowsum(dpre)

    row = lambda c: pl.BlockSpec((tm, c), lambda i: (i, 0))
    full = lambda shape: pl.BlockSpec(shape, lambda i: (0,) * len(shape))
    acc_shapes = [(d, d), (d, d), (D_PLE, d), (8, d)]
    return pl.pallas_call(
        body, name="post_bwd", grid=(t // tm,),
        in_specs=[row(d), row(d), row(d), row(D_PLE), row(d), full((d, d)), full((d, d)), full((D_PLE, d)), full((8, d))],
        out_specs=[row(d), row(d)] + [full(sh) for sh in acc_shapes],
        out_shape=[jax.ShapeDtypeStruct((t, d), F32)] * 2 + [jax.ShapeDtypeStruct(sh, F32) for sh in acc_shapes],
        compiler_params=_cparams(("arbitrary",)),
    )(dxn, r2, gate2, p2, y2, w_out, wg, wple, vec)


def _place():
    x, y, c = lax.axis_index("x"), lax.axis_index("y"), lax.axis_index("c")
    chips = [(1 - x, y), (x, 1 - y), (1 - x, 1 - y)]
    return x, y, c, chips


def _shard_of(ref, ax, k, width, lo=None, ln=None):
    idx = [slice(None)] * 3
    if lo is not None:
        idx[0] = pl.ds(lo, ln)
    idx[ax] = pl.ds(k * width, width)
    return ref.at[tuple(idx)]


def _remote(src, dst, ssem, rsem, dev):
    return pltpu.make_async_remote_copy(src_ref=src, dst_ref=dst, send_sem=ssem, recv_sem=rsem, device_id=dev, device_id_type=MESH)


def _place_shard(w, ax, chip, dtype=MM_DTYPE, layer=None):
    l_dim, a_dim, b_dim = w.shape
    tr = min(256, a_dim)
    per = a_dim // tr
    first = 0
    if layer is not None:
        l_dim, first = 1, layer
    shape = [l_dim, a_dim, b_dim]
    shape[ax] *= 4
    if ax == 2:
        out_spec = pl.BlockSpec((1, tr, b_dim), lambda l, i, k: (l, i, k[0]))
    else:
        out_spec = pl.BlockSpec((1, tr, b_dim), lambda l, i, k: (l, k[0] * per + i, 0))

    def body(k_ref, w_ref, o_ref):
        o_ref[...] = w_ref[...].astype(o_ref.dtype)

    return pl.pallas_call(
        body, name="place_shard",
        grid_spec=pltpu.PrefetchScalarGridSpec(
            num_scalar_prefetch=1, grid=(l_dim, per),
            in_specs=[pl.BlockSpec((1, tr, b_dim), lambda l, i, k: (first + l, i, 0))], out_specs=out_spec),
        out_shape=jax.ShapeDtypeStruct(tuple(shape), dtype),
        compiler_params=_cparams(("parallel", "parallel")),
    )(chip, w)


def _shard_copies(src_refs, dst_refs, axes, sems):
    x, y, c, chips = _place()
    j = 2 * x + y
    cps = []
    for a, (s_ref, d_ref) in enumerate(zip(src_refs, dst_refs)):
        w = d_ref.shape[axes[a]] // 4
        for q, (qx, qy) in enumerate(chips):
            ssem, rsem = sems[3 * a + q]
            cps.append(_remote(_shard_of(s_ref, axes[a], j, w), _shard_of(d_ref, axes[a], j, w), ssem, rsem, (qx, qy, c)))
    return cps


def _chip_handshake():
    x, y, c, chips = _place()
    barrier = pltpu.get_barrier_semaphore()
    for qx, qy in chips:
        pl.semaphore_signal(barrier, inc=1, device_id=(qx, qy, c), device_id_type=MESH)
    pl.semaphore_wait(barrier, 3)


def _gather_async(name, collective_id, fulls, axes):
    n = len(fulls)
    refs = [jax.new_ref(f, memory_space=pltpu.MemorySpace.HBM) for f in fulls]

    @pl.kernel(mesh=plsc.ScalarSubcoreMesh(axis_name="seq", num_cores=1), name=name,
               scratch_types=(pltpu.SemaphoreType.DMA,) * (6 * n),
               compiler_params=pltpu.CompilerParams(collective_id=collective_id))
    def launch(*sems):
        _chip_handshake()
        cps = _shard_copies(refs, refs, axes, [(sems[2 * k], sems[2 * k + 1]) for k in range(3 * n)])
        for cp in cps:
            cp.start()
        for cp in cps:
            cp.wait()

    launch()
    return refs


def _scatter_async(name, collective_id, grads, axes):
    n = len(grads)
    outs = []
    for g, ax in zip(grads, axes):
        sh = list(g.shape)
        sh[ax] //= 4
        outs.append(jax.ShapeDtypeStruct((3,) + tuple(sh), g.dtype))

    def body(*refs):
        srcs, lands, sems = refs[:n], refs[n:2 * n], refs[2 * n:]
        _chip_handshake()
        x, y, c, chips = _place()
        cps = []
        for a in range(n):
            w = srcs[a].shape[axes[a]] // 4
            for q, (qx, qy) in enumerate(chips):
                k = 3 * a + q
                cps.append(_remote(_shard_of(srcs[a], axes[a], 2 * qx + qy, w), lands[a].at[q], sems[2 * k], sems[2 * k + 1],
                                   (qx, qy, c)))
        for cp in cps:
            cp.start()
        for cp in cps:
            cp.wait()

    return pl.kernel(body, out_type=outs, mesh=plsc.ScalarSubcoreMesh(axis_name="seq", num_cores=1), name=name,
                     scratch_types=(pltpu.SemaphoreType.DMA,) * (6 * n),
                     compiler_params=pltpu.CompilerParams(collective_id=collective_id))(*grads)


def _pair_swap(sums):
    n = len(sums)

    def body(*refs):
        g_refs, o_refs = refs[:n], refs[n:2 * n]
        ssem, rsem = refs[2 * n:]
        x, y, c, _ = _place()
        cps = [_remote(g_refs[a], o_refs[a], ssem.at[a], rsem.at[a], (x, y, 1 - c)) for a in range(n)]
        for cp in cps:
            cp.start()
        for cp in cps:
            cp.wait()

    return pl.pallas_call(
        body, name="pair_swap", in_specs=[ANY] * n, out_specs=[ANY] * n,
        out_shape=[jax.ShapeDtypeStruct(g.shape, g.dtype) for g in sums],
        scratch_shapes=[pltpu.SemaphoreType.DMA((n,)), pltpu.SemaphoreType.DMA((n,))],
        compiler_params=pltpu.CompilerParams(has_side_effects=True),
    )(*sums)


def _pair_exchange(grads, small):
    n = len(grads)
    outs = [jax.ShapeDtypeStruct((g.shape[0] // 2,) + g.shape[1:], g.dtype) for g in grads]
    outs.append(jax.ShapeDtypeStruct((small.shape[0] // 2, small.shape[1]), small.dtype))

    def body(*refs):
        g_refs, o_refs = refs[:n + 1], refs[n + 1:2 * n + 2]
        ssem, rsem = refs[2 * n + 2:]
        x, y, c, _ = _place()
        cps = []
        for a in range(n + 1):
            lh = g_refs[a].shape[0] // 2
            cp = _remote(g_refs[a].at[pl.ds((1 - c) * lh, lh)], o_refs[a], ssem.at[a], rsem.at[a], (x, y, 1 - c))
            cp.start()
            cps.append(cp)
        for cp in cps:
            cp.wait()

    return pl.pallas_call(
        body, name="pair_exchange", in_specs=[ANY] * (n + 1), out_specs=[ANY] * (n + 1), out_shape=outs,
        scratch_shapes=[pltpu.SemaphoreType.DMA((n + 1,)), pltpu.SemaphoreType.DMA((n + 1,))],
        compiler_params=pltpu.CompilerParams(has_side_effects=True),
    )(*grads, small)


def _chip_scatter(sums, axes, small):
    n = len(sums)
    outs = []
    for g, ax in zip(sums, axes):
        sh = list(g.shape)
        sh[ax] //= 4
        outs.append(jax.ShapeDtypeStruct((3,) + tuple(sh), g.dtype))
    rq = small.shape[0] // 4
    outs.append(jax.ShapeDtypeStruct((3, rq, small.shape[1]), small.dtype))

    def body(*refs):
        g_refs, o_refs = refs[:n + 1], refs[n + 1:2 * n + 2]
        ssem, rsem = refs[2 * n + 2:]
        x, y, c, chips = _place()
        cps = []
        for a in range(n + 1):
            for q, (qx, qy) in enumerate(chips):
                k = 2 * qx + qy
                if a < n:
                    src = _shard_of(g_refs[a], axes[a], k, g_refs[a].shape[axes[a]] // 4)
                else:
                    src = g_refs[a].at[pl.ds(k * rq, rq)]
                cp = _remote(src, o_refs[a].at[q], ssem.at[a * 3 + q], rsem.at[a * 3 + q], (qx, qy, c))
                cp.start()
                cps.append(cp)
        for cp in cps:
            cp.wait()

    return pl.pallas_call(
        body, name="chip_scatter", in_specs=[ANY] * (n + 1), out_specs=[ANY] * (n + 1), out_shape=outs,
        scratch_shapes=[pltpu.SemaphoreType.DMA((3 * n + 3,)), pltpu.SemaphoreType.DMA((3 * n + 3,))],
        compiler_params=pltpu.CompilerParams(has_side_effects=True),
    )(*sums, small)


def _final_exchange(reds, small):
    n = len(reds)
    flips = [(fx, fy, fc) for fx in (0, 1) for fy in (0, 1) for fc in (0, 1)][1:]

    def body(*refs):
        g_refs, o_refs = refs[:n + 1], refs[n + 1:2 * n + 2]
        ssem, rsem = refs[2 * n + 2:]
        x, y, c, _ = _place()
        cps = []
        for a in range(n):
            lh = g_refs[a].shape[0] // 2
            cp = _remote(g_refs[a].at[pl.ds(c * lh, lh)], o_refs[a].at[pl.ds(c * lh, lh)], ssem.at[a], rsem.at[a], (x, y, 1 - c))
            cp.start()
            cps.append(cp)
        mine = 4 * c + 2 * x + y
        for f, (fx, fy, fc) in enumerate(flips):
            cp = _remote(g_refs[n].at[mine], o_refs[n].at[mine], ssem.at[n + f], rsem.at[n + f], (x ^ fx, y ^ fy, c ^ fc))
            cp.start()
            cps.append(cp)
        for cp in cps:
            cp.wait()

    return pl.pallas_call(
        body, name="final_exchange", in_specs=[ANY] * (n + 1), out_specs=[ANY] * (n + 1),
        out_shape=[jax.ShapeDtypeStruct(g.shape, g.dtype) for g in reds] + [jax.ShapeDtypeStruct(small.shape, small.dtype)],
        input_output_aliases={a: a for a in range(n + 1)},
        scratch_shapes=[pltpu.SemaphoreType.DMA((n + 7,)), pltpu.SemaphoreType.DMA((n + 7,))],
        compiler_params=pltpu.CompilerParams(has_side_effects=True),
    )(*reds, small)


def _small_allreduce(small):
    r = small.shape[0]
    rh, rq = r // 2, r // 8
    flips = [(fx, fy, fc) for fx in (0, 1) for fy in (0, 1) for fc in (0, 1)][1:]

    def body(g_ref, out_ref, pair_ref, chip_ref, s1_ref, ssem, rsem):
        x, y, c, chips = _place()
        cp = _remote(g_ref.at[pl.ds((1 - c) * rh, rh)], pair_ref, ssem.at[0], rsem.at[0], (x, y, 1 - c))
        cp.start()
        cp.wait()
        s1_ref[...] = g_ref[pl.ds(pl.multiple_of(c * rh, 8), rh), :] + pair_ref[...]
        cps = [_remote(s1_ref.at[pl.ds((2 * qx + qy) * rq, rq)], chip_ref.at[q], ssem.at[1 + q], rsem.at[1 + q], (qx, qy, c))
               for q, (qx, qy) in enumerate(chips)]
        for cp in cps:
            cp.start()
        for cp in cps:
            cp.wait()
        mine = out_ref.at[pl.ds(pl.multiple_of((4 * c + 2 * x + y) * rq, 8), rq)]
        mine[...] = ((s1_ref[pl.ds(pl.multiple_of((2 * x + y) * rq, 8), rq), :] + chip_ref[0]) + chip_ref[1]) + chip_ref[2]
        cps = [_remote(mine, mine, ssem.at[4 + f], rsem.at[4 + f], (x ^ fx, y ^ fy, c ^ fc)) for f, (fx, fy, fc) in enumerate(flips)]
        for cp in cps:
            cp.start()
        for cp in cps:
            cp.wait()

    vm = pl.BlockSpec(memory_space=pltpu.VMEM)
    return pl.pallas_call(
        body, name="small_allreduce", in_specs=[vm], out_specs=vm, out_shape=jax.ShapeDtypeStruct(small.shape, F32),
        scratch_shapes=[pltpu.VMEM((rh, 128), F32), pltpu.VMEM((3, rq, 128), F32), pltpu.VMEM((rh, 128), F32),
                        pltpu.SemaphoreType.DMA((11,)), pltpu.SemaphoreType.DMA((11,))],
        compiler_params=pltpu.CompilerParams(has_side_effects=True, vmem_limit_bytes=VMEM_LIMIT),
    )(small)


def _to_wire(g):
    _, a_dim, b_dim = g.shape
    tr = min(256, a_dim)

    def body(g_ref, o_ref):
        o_ref[...] = g_ref[...].astype(o_ref.dtype)

    blk = pl.BlockSpec((1, tr, b_dim), lambda i: (0, i, 0))
    return pl.pallas_call(
        body, name="to_wire", grid=(a_dim // tr,), in_specs=[blk], out_specs=blk,
        out_shape=jax.ShapeDtypeStruct(g.shape, WIRE_DTYPE), compiler_params=_cparams(("parallel",)),
    )(g)


def _chip_sum(owns, gots, ax, chip):
    n_layers = len(owns)
    _, _, a_dim, b_dim = gots[0].shape
    tr = min(256, a_dim)
    per = a_dim // tr

    def own_spec(layer):
        if ax == 2:
            return pl.BlockSpec((1, tr, b_dim), lambda l, i, k: (0, jnp.where(l == layer, i, 0), k[0]))
        return pl.BlockSpec((1, tr, b_dim), lambda l, i, k: (0, k[0] * per + jnp.where(l == layer, i, 0), 0))

    def got_spec(layer):
        return pl.BlockSpec((3, 1, tr, b_dim), lambda l, i, k: (0, 0, jnp.where(l == layer, i, 0), 0))

    def body(k_ref, *refs):
        s_ref = refs[-1]
        for layer in range(n_layers):
            @pl.when(pl.program_id(0) == layer)
            def _(own_ref=refs[layer], got_ref=refs[n_layers + layer]):
                s_ref[...] = ((own_ref[...].astype(F32) + got_ref[0].astype(F32)) + got_ref[1].astype(F32)) + got_ref[2].astype(F32)

    return pl.pallas_call(
        body, name="chip_sum",
        grid_spec=pltpu.PrefetchScalarGridSpec(
            num_scalar_prefetch=1, grid=(n_layers, per),
            in_specs=[own_spec(l) for l in range(n_layers)] + [got_spec(l) for l in range(n_layers)],
            out_specs=pl.BlockSpec((1, tr, b_dim), lambda l, i, k: (l, i, 0))),
        out_shape=jax.ShapeDtypeStruct((n_layers, a_dim, b_dim), F32),
        compiler_params=_cparams(("arbitrary", "arbitrary")),
    )(chip, *owns, *gots)


def _small_pair_sum(small, got, half):
    rh = small.shape[0] // 2

    def body(h_ref, g_ref, o_ref, s_ref):
        s_ref[...] = g_ref[...] + o_ref[...]

    return pl.pallas_call(
        body, name="small_pair_sum",
        grid_spec=pltpu.PrefetchScalarGridSpec(
            num_scalar_prefetch=1, grid=(1,),
            in_specs=[pl.BlockSpec((rh, 128), lambda i, h: (h[0], 0)), pl.BlockSpec((rh, 128), lambda i, h: (0, 0))],
            out_specs=pl.BlockSpec((rh, 128), lambda i, h: (0, 0))),
        out_shape=jax.ShapeDtypeStruct((rh, 128), F32),
    )(half, small, got)


def _small_chip_sum(s1, got, pos):
    rq = got.shape[1]

    def body(k_ref, own_ref, got_ref, s_ref):
        s_ref[0] = ((own_ref[...] + got_ref[0]) + got_ref[1]) + got_ref[2]

    return pl.pallas_call(
        body, name="small_chip_sum",
        grid_spec=pltpu.PrefetchScalarGridSpec(
            num_scalar_prefetch=1, grid=(1,),
            in_specs=[pl.BlockSpec((rq, 128), lambda i, k: (k[0], 0)), pl.BlockSpec((3, rq, 128), lambda i, k: (0, 0, 0))],
            out_specs=pl.BlockSpec((1, rq, 128), lambda i, k: (4 * k[1] + k[0], 0, 0))),
        out_shape=jax.ShapeDtypeStruct((8, rq, 128), F32),
    )(pos, s1, got)


def _adam_math(w, g, m, v):
    m = ADAM_B1 * m + (1.0 - ADAM_B1) * g
    v = ADAM_B2 * v + (1.0 - ADAM_B2) * (g * g)
    m_hat = m / (1.0 - ADAM_B1 ** ADAM_STEP)
    v_hat = v / (1.0 - ADAM_B2 ** ADAM_STEP)
    return -ADAM_LR * (m_hat / (jnp.sqrt(v_hat) + ADAM_EPS) + ADAM_WD * w), m, v


def _adamw_big(w, g_mine, g_other, m, v):
    l_dim, a_dim, b_dim = w.shape
    tr = min(256, a_dim)

    def body(w_ref, g1_ref, g2_ref, m_ref, v_ref, g_ref, d_ref, nm_ref, nv_ref):
        g = g1_ref[...] + g2_ref[...]
        g_ref[...] = g
        d_ref[...], nm_ref[...], nv_ref[...] = _adam_math(w_ref[...], g, m_ref[...], v_ref[...])

    blk = pl.BlockSpec((1, tr, b_dim), lambda l, i: (l, i, 0))
    return pl.pallas_call(
        body, name="adamw_big", grid=(l_dim, a_dim // tr), in_specs=[blk] * 5, out_specs=[blk] * 4,
        out_shape=[jax.ShapeDtypeStruct(w.shape, F32)] * 4,
        compiler_params=_cparams(("parallel", "parallel")),
    )(w, g_mine, g_other, m, v)


def _adamw_small(ws, gs, ms, vs):
    n = len(ws)

    def body(*refs):
        for i in range(n):
            w_ref, g_ref, m_ref, v_ref = refs[i], refs[n + i], refs[2 * n + i], refs[3 * n + i]
            d_ref, nm_ref, nv_ref = refs[4 * n + i], refs[5 * n + i], refs[6 * n + i]
            d_ref[...], nm_ref[...], nv_ref[...] = _adam_math(w_ref[...], g_ref[...], m_ref[...], v_ref[...])

    shapes = [jax.ShapeDtypeStruct(w.shape, F32) for w in ws]
    outs = pl.pallas_call(body, name="adamw_small", out_shape=shapes * 3,
                          compiler_params=_cparams())(*ws, *gs, *ms, *vs)
    return outs[:n], outs[n:2 * n], outs[2 * n:]


def _pack(arrs, row_mult):
    parts = []
    for a in arrs:
        flat = a.reshape(-1)
        pad = (-flat.shape[0]) % 1024
        parts.append(jnp.pad(flat, (0, pad)).reshape(-1, 128))
    buf = jnp.concatenate(parts, axis=0)
    pad = (-buf.shape[0]) % row_mult
    return jnp.pad(buf, ((0, pad), (0, 0)))


def _unpack(buf, shapes):
    out, row = [], 0
    for sh in shapes:
        n = math.prod(sh)
        rows = -(-n // 1024) * 8
        out.append(buf[row:row + rows].reshape(-1)[:n].reshape(sh))
        row += rows
    return out


_NAMES = ['w_in_e', 'b_in_e', 'conv_a_w', 'conv_a_b', 'ln_a_g', 'ln_a_b', 'ln_v_g', 'ln_v_b', 'w_s', 'b_s', 'w_out_e', 'b_out_e',
          'w_in_o', 'b_in_o', 'w_pool', 'pool_scale', 'conv_d_w', 'w_out_o', 'b_out_o', 'ln_g', 'ln_b', 'w_ple', 'w_ple_gate',
          'b_ple_gate']
_BIG = ['w_in_e', 'w_out_e', 'w_in_o', 'w_out_o', 'w_ple', 'w_ple_gate']
_BIG_AXES = [2, 1, 2, 1, 2, 1]
_SMALL_SHARDED = ['conv_a_w', 'b_in_o', 'pool_scale', 'conv_d_w', 'b_out_o']


def kernel(x, p, w_in_e, b_in_e, conv_a_w, conv_a_b, ln_a_g, ln_a_b, ln_v_g, ln_v_b, w_s, b_s, w_out_e, b_out_e, w_in_o, b_in_o, w_pool, pool_scale, conv_d_w, w_out_o, b_out_o, ln_g, ln_b, w_ple, w_ple_gate, b_ple_gate, loss_target, m_w_in_e, m_b_in_e, m_conv_a_w, m_conv_a_b, m_ln_a_g, m_ln_a_b, m_ln_v_g, m_ln_v_b, m_w_s, m_b_s, m_w_out_e, m_b_out_e, m_w_in_o, m_b_in_o, m_w_pool, m_pool_scale, m_conv_d_w, m_w_out_o, m_b_out_o, m_ln_g, m_ln_b, m_w_ple, m_w_ple_gate, m_b_ple_gate, v_w_in_e, v_b_in_e, v_conv_a_w, v_conv_a_b, v_ln_a_g, v_ln_a_b, v_ln_v_g, v_ln_v_b, v_w_s, v_b_s, v_w_out_e, v_b_out_e, v_w_in_o, v_b_in_o, v_w_pool, v_pool_scale, v_conv_d_w, v_w_out_o, v_b_out_o, v_ln_g, v_ln_b, v_w_ple, v_w_ple_gate, v_b_ple_gate):
    args = locals()
    wts = {n: args[n] for n in _NAMES}
    mom = {n: args["m_" + n] for n in _NAMES}
    var = {n: args["v_" + n] for n in _NAMES}
    bsz, s_len, d = x.shape
    t = bsz * s_len
    cx, cy, cc = lax.axis_index("x"), lax.axis_index("y"), lax.axis_index("c")
    chip = (2 * cx + cy).astype(jnp.int32).reshape(1)
    half = cc.astype(jnp.int32).reshape(1)
    pos = jnp.concatenate([chip, half])

    def placed(name, ax, layer):
        return _place_shard(wts[name], ax, chip, layer=layer)

    sv = _pack([wts[n] for n in _SMALL_SHARDED], 8)
    first_refs = _gather_async("gather_first", DEPTH, [placed('w_in_e', 2, 0), _place_shard(sv[None], 1, chip, F32)], [2, 1])
    layer_refs = []
    for i in range(DEPTH):
        sfx = '_e' if i % 2 == 0 else '_o'
        items = [('w_out' + sfx, 1, i // 2), ('w_ple_gate', 1, i), ('w_ple', 2, i)] + ([('w_in' + sfx, 2, i // 2)] if i else [])
        layer_refs.append(_gather_async("gather_layer%d" % i, i, [placed(*it) for it in items], [it[1] for it in items]))
    fw = {}
    w_in_first = first_refs[0][...]
    sv_all = first_refs[1][...].reshape((4,) + sv.shape)
    small_parts = [_unpack(sv_all[k], [wts[n].shape for n in _SMALL_SHARDED]) for k in range(4)]
    for i, n in enumerate(_SMALL_SHARDED):
        fw[n] = jnp.concatenate([small_parts[k][i] for k in range(4)], axis=-1)
    for n in _NAMES:
        fw.setdefault(n, wts[n])

    def row8(rows, width):
        rows = [r.reshape(1, width) for r in rows]
        return jnp.concatenate(rows + [jnp.zeros((8 - len(rows), width), F32)], axis=0)

    x2 = x.reshape(t, d)
    saved = []
    for i in range(DEPTH):
        j = i // 2
        even = i % 2 == 0
        b_in, b_out = (fw['b_in_e'], fw['b_out_e']) if even else (fw['b_in_o'], fw['b_out_o'])
        w_in = layer_refs[i][3][...][0] if i else w_in_first[0]
        z = _in_proj(x2, w_in, b_in[j].reshape(1, N_COLS))
        z3 = z.reshape(bsz, s_len, N_COLS)
        if even:
            cw = jnp.concatenate([fw['conv_a_w'][j], jnp.zeros((1, W_BR), F32)], axis=0)
            mvec = row8([fw['conv_a_b'][j], fw['ln_a_g'][j], fw['ln_a_b'][j], fw['ln_v_g'][j], fw['ln_v_b'][j]], W_BR)
            bsf = jnp.repeat(fw['b_s'][j].T, W_BR // 8, axis=1)
            mix = (cw, mvec, fw['w_s'][j], jnp.swapaxes(fw['w_s'][j], 1, 2), bsf)
            y3 = _even_fwd(z3, cw, mvec, fw['w_s'][j], bsf)
        else:
            mvec = row8([fw['pool_scale'][j]] + [fw['conv_d_w'][j][k] for k in range(3)], W_BR)
            mix = (fw['w_pool'][j].astype(MM_DTYPE), mvec)
            y3 = _odd_fwd(z3, mix[0], mvec)
        pvec = row8([b_out[j], fw['ln_g'][i], fw['ln_b'][i], fw['b_ple_gate'][i]], d)
        post_w = (layer_refs[i][0][...][0], layer_refs[i][1][...][0], layer_refs[i][2][...][0], pvec)
        p2 = p[i].reshape(t, D_PLE)
        y2 = y3.reshape(t, 2 * W_BR)
        if i < DEPTH - 1:
            xn, r2, gate2 = _post_fwd(y2, x2, p2, *post_w)
        else:
            dx, r2, gate2, sq = _post_fwd(y2, x2, p2, *post_w, tgt=loss_target.reshape(t, d))
        saved.append((x2, z3, y2, r2, gate2, p2, w_in, mix, post_w))
        x2 = xn

    gr = {n: [None] * wts[n].shape[0] for n in _NAMES}
    for i in reversed(range(DEPTH)):
        j = i // 2
        even = i % 2 == 0
        x_in, z3, y2, r2, gate2, p2, w_in, mix, post_w = saved[i]
        dxr, dy, dwo, dwg, dwp, dpv = _post_bwd(dx, r2, gate2, p2, y2, *post_w)
        dy3 = dy.reshape(bsz, s_len, 2 * W_BR)
        sfx = '_e' if even else '_o'
        gr['w_out' + sfx][j], gr['b_out' + sfx][j] = dwo, dpv[0]
        gr['w_ple_gate'][i], gr['w_ple'][i] = dwg, dwp
        gr['ln_g'][i], gr['ln_b'][i], gr['b_ple_gate'][i] = dpv[1], dpv[2], dpv[3]
        if even:
            dz3, dbin, dcw, dmv, dws, dbsf = _even_bwd(z3, dy3, *mix)
            gr['conv_a_w'][j], gr['conv_a_b'][j] = dcw[:KA], dmv[0]
            gr['ln_a_g'][j], gr['ln_a_b'][j], gr['ln_v_g'][j], gr['ln_v_b'][j] = dmv[1], dmv[2], dmv[3], dmv[4]
            gr['w_s'][j] = dws
            gr['b_s'][j] = jnp.sum(dbsf.reshape(GBLK, 8, W_BR // 8), axis=2).T
        else:
            dz3, dbin, dwpool, dmv = _odd_bwd(z3, dy3, *mix)
            gr['w_pool'][j], gr['pool_scale'][j], gr['conv_d_w'][j] = dwpool, dmv[0], dmv[1:4]
        gr['b_in' + sfx][j] = dbin[0]
        dz2 = dz3.reshape(t, N_COLS)
        dwi, dw_done = _in_proj_bwd_dw(x_in, dz2)
        post_items = [('w_out' + sfx, j, dwo[None], 1), ('w_ple_gate', i, dwg[None], 1), ('w_ple', i, dwp[None], 2)]
        in_item = ('w_in' + sfx, j, dwi[None], 2)
        batches = [post_items + [in_item]] if i else [post_items, [in_item]]
        for bi, items in enumerate(batches):
            sent = [_to_wire(it[2]) if (i == 0 and bi == 1) else it[2] for it in items]
            lands = _scatter_async("scatter_layer%d_%d" % (i, bi), DEPTH + 1 + 2 * i + bi, sent, [it[3] for it in items])
            for it, land in zip(items, lands):
                gr[it[0]][it[1]] = (it[2], land, it[3])
        dx = _in_proj_bwd_dx(dxr, dz2, w_in, dw_done)
    grad_x = dx.reshape(bsz, s_len, d)

    sums = []
    for n in _BIG:
        sums.append(_chip_sum([g[0] for g in gr[n]], [g[1] for g in gr[n]], gr[n][0][2], chip))
    others = _pair_swap(sums)
    small_names = [n for n in _NAMES if n not in _BIG]
    g_small_full = [jnp.stack(gr[n]) for n in small_names] + [sq]
    small_all = _small_allreduce(_pack(g_small_full, 64))
    *g_small, sq_all = _unpack(small_all, [g.shape for g in g_small_full])
    loss = 0.5 * jnp.sum(sq_all) / d
    grads = {}
    for n, g in zip(small_names, g_small):
        if n in _SMALL_SHARDED:
            w = wts[n].shape[-1]
            g = lax.dynamic_slice_in_dim(g, (2 * cx + cy) * w, w, axis=g.ndim - 1)
        grads[n] = g

    delta, new_m, new_v = {}, {}, {}
    for n, mine, other in zip(_BIG, sums, others):
        grads[n], delta[n], new_m[n], new_v[n] = _adamw_big(wts[n], mine, other, mom[n], var[n])
    ds, ms, vs = _adamw_small([wts[n] for n in small_names], [grads[n] for n in small_names],
                              [mom[n] for n in small_names], [var[n] for n in small_names])
    for n, a, b, c_ in zip(small_names, ds, ms, vs):
        delta[n], new_m[n], new_v[n] = a, b, c_

    return (loss, grad_x, *[grads[n] for n in _NAMES], *[delta[n] for n in _NAMES],
            *[new_m[n] for n in _NAMES], *[new_v[n] for n in _NAMES])
```

```python
import functools
import math

import jax
import jax.numpy as jnp
from jax import lax
from jax.experimental import pallas as pl
from jax.experimental.pallas import tpu as pltpu
from jax.experimental.pallas import tpu_sc as plsc

F32 = jnp.float32
MM_DTYPE = jnp.bfloat16
WIRE_DTYPE = jnp.bfloat16
SEQ_TILE = 512
ROW_TILE = 512
BWD_ROW_TILE = 512
HALO = 32
CONV_CHUNK = 32
GBLK = 128
VMEM_LIMIT = 56 * 1024 * 1024

D_MODEL = 1024
W_BR = 512
N_COLS = 6 * W_BR
D_PLE = 256
KA = 31
DEPTH = 4
POOL_WINDOWS = (2, 4, 8, 16)
ALPHA = (2.0 * DEPTH) ** 0.25
LN_EPS = 1e-5
GELU_C = math.sqrt(2.0 / math.pi)

ADAM_LR, ADAM_B1, ADAM_B2, ADAM_EPS, ADAM_WD, ADAM_STEP = 0.001, 0.9, 0.999, 1e-08, 0.01, 10

MESH = pl.DeviceIdType.MESH
ANY = pl.BlockSpec(memory_space=pl.ANY)


def _cparams(sem=None):
    return pltpu.CompilerParams(dimension_semantics=sem, vmem_limit_bytes=VMEM_LIMIT)


def _sigmoid(x):
    return 1.0 / (1.0 + jnp.exp(-x))


def _silu(x):
    return x * _sigmoid(x)


def _silu_grad(x):
    s = _sigmoid(x)
    return x * s, s * (1.0 + x * (1.0 - s))


def _gelu(x):
    return 0.5 * x * (1.0 + jnp.tanh(GELU_C * (x + 0.044715 * (x * x * x))))


def _gelu_grad(x):
    x2 = x * x
    th = jnp.tanh(GELU_C * (x + 0.044715 * (x * x2)))
    return 0.5 * x * (1.0 + th), 0.5 * (1.0 + th) + 0.5 * x * (1.0 - th * th) * (GELU_C * (1.0 + 3.0 * 0.044715 * x2))


def _ln_stats(x):
    mu = jnp.mean(x, axis=-1, keepdims=True)
    d = x - mu
    var = jnp.mean(d * d, axis=-1, keepdims=True)
    rs = lax.rsqrt(var + LN_EPS)
    return d * rs, rs


def _ln_bwd(dxh, xh, rs):
    return rs * (dxh - jnp.mean(dxh, axis=-1, keepdims=True) - xh * jnp.mean(dxh * xh, axis=-1, keepdims=True))


def _mm(a):
    return a.astype(MM_DTYPE)


def _dot(a, b):
    return jnp.dot(_mm(a), _mm(b), preferred_element_type=F32)


def _dot_nt(a, b):
    return lax.dot_general(_mm(a), _mm(b), (((1,), (1,)), ((), ())), preferred_element_type=F32)


def _dot_tn(a, b):
    return lax.dot_general(_mm(a), _mm(b), (((0,), (0,)), ((), ())), preferred_element_type=F32)


def _rowsum(x):
    return jnp.sum(x, axis=0, keepdims=True)


def _in_proj(x2, w, b):
    t, d = x2.shape
    n = w.shape[1]
    tm = min(ROW_TILE, t)
    nc = 768

    def body(x_ref, w_ref, b_ref, z_ref):
        xb = _mm(x_ref[...])
        for j in range(n // nc):
            cs = slice(j * nc, (j + 1) * nc)
            z_ref[:, cs] = jnp.dot(xb, w_ref[:, cs], preferred_element_type=F32) + b_ref[:, cs]

    return pl.pallas_call(
        body, name="in_proj", grid=(t // tm,),
        in_specs=[pl.BlockSpec((tm, d), lambda i: (i, 0)), pl.BlockSpec((d, n), lambda i: (0, 0)),
                  pl.BlockSpec((1, n), lambda i: (0, 0))],
        out_specs=pl.BlockSpec((tm, n), lambda i: (i, 0)),
        out_shape=jax.ShapeDtypeStruct((t, n), F32),
        compiler_params=_cparams(("parallel",)),
    )(x2, w, b)


def _in_proj_bwd_dx(dxr, dz, w, after):
    t, d = dxr.shape
    n = w.shape[1]
    tm = min(ROW_TILE, t)

    def body(dxr_ref, dz_ref, w_ref, after_ref, dx_ref):
        dx_ref[...] = dxr_ref[...] + _dot_nt(dz_ref[...], w_ref[...])

    return pl.pallas_call(
        body, name="in_proj_bwd_dx", grid=(t // tm,),
        in_specs=[pl.BlockSpec((tm, d), lambda i: (i, 0)), pl.BlockSpec((tm, n), lambda i: (i, 0)),
                  pl.BlockSpec((d, n), lambda i: (0, 0)), pl.BlockSpec((8, 128), lambda i: (0, 0))],
        out_specs=pl.BlockSpec((tm, d), lambda i: (i, 0)),
        out_shape=jax.ShapeDtypeStruct((t, d), F32),
        compiler_params=_cparams(("parallel",)),
    )(dxr, dz, w, after)


def _in_proj_bwd_dw(x2, dz):
    t, d = x2.shape
    n = dz.shape[1]
    tm = min(ROW_TILE, t)
    nc = 768

    def body(x_ref, dz_ref, dw_ref, done_ref):
        @pl.when(pl.program_id(0) == 0)
        def _():
            dw_ref[...] = jnp.zeros_like(dw_ref)
            done_ref[...] = jnp.zeros_like(done_ref)
        xb = _mm(x_ref[...])
        for j in range(n // nc):
            cs = slice(j * nc, (j + 1) * nc)
            dw_ref[:, cs] += _dot_tn(xb, dz_ref[:, cs])

    return pl.pallas_call(
        body, name="in_proj_bwd_dw", grid=(t // tm,),
        in_specs=[pl.BlockSpec((tm, d), lambda i: (i, 0)), pl.BlockSpec((tm, n), lambda i: (i, 0))],
        out_specs=[pl.BlockSpec((d, n), lambda i: (0, 0)), pl.BlockSpec((8, 128), lambda i: (0, 0))],
        out_shape=[jax.ShapeDtypeStruct((d, n), F32), jax.ShapeDtypeStruct((8, 128), F32)],
        compiler_params=_cparams(("arbitrary",)),
    )(x2, dz)


def _halo_specs(ts, s_len, cols, left=True):
    per = ts // HALO
    last = s_len // HALO - 1
    if left:
        return pl.BlockSpec((1, HALO, cols), lambda b, s: (b, jnp.maximum(s * per - 1, 0), 0))
    return pl.BlockSpec((1, HALO, cols), lambda b, s: (b, jnp.minimum((s + 1) * per, last), 0))


def _build_shifts(src_ref, sh_ref, rows):
    for r in range(1, 8):
        sh_ref[r - 1, 0:rows, :] = src_ref[r:r + rows, :]


def _shifted(src_ref, sh_ref, r, start, n):
    if r == 0:
        return src_ref[pl.ds(start, n), :]
    return sh_ref[r - 1, pl.ds(start, n), :]


def _tril_masks():
    ri = lax.broadcasted_iota(jnp.int32, (GBLK, GBLK), 0)
    ci = lax.broadcasted_iota(jnp.int32, (GBLK, GBLK), 1)
    return ri >= ci, ci >= ri


def _spatial(w_ref, keep, vb):
    lane = lax.broadcasted_iota(jnp.int32, (GBLK, GBLK), 1)
    outs = []
    for p in range(4):
        xs = vb[:, p * GBLK:(p + 1) * GBLK]
        r0 = jnp.dot(_mm(jnp.where(keep, w_ref[2 * p], 0.0)), xs, preferred_element_type=F32)
        r1 = jnp.dot(_mm(jnp.where(keep, w_ref[2 * p + 1], 0.0)), xs, preferred_element_type=F32)
        outs.append(jnp.where(lane < 64, r0, r1))
    return jnp.concatenate(outs, axis=1)


def _even_fwd(z3, cw, vec, ws, bsf):
    bsz, s_len, _ = z3.shape
    ts = min(SEQ_TILE, s_len)
    ext_rows = ts + HALO

    def body(z_ref, zl_ref, cw_ref, vec_ref, ws_ref, bsf_ref, y_ref, ext_ref, sh_ref, a1_ref):
        s = pl.program_id(1)
        hl = zl_ref[0]
        a0h = hl[:, 0:W_BR] * _sigmoid(hl[:, W_BR:2 * W_BR])
        ext_ref[0:HALO, :] = jnp.where(s > 0, a0h, 0.0)
        ext_ref[HALO:ext_rows, :] = z_ref[0, :, 0:W_BR] * _sigmoid(z_ref[0, :, W_BR:2 * W_BR])
        ext_ref[ext_rows:ext_rows + 8, :] = jnp.zeros((8, W_BR), F32)
        _build_shifts(ext_ref, sh_ref, ext_rows)

        def conv_chunk(ci, carry):
            base = pl.multiple_of(ci * CONV_CHUNK, CONV_CHUNK)
            acc = jnp.zeros((CONV_CHUNK, W_BR), F32) + vec_ref[0:1, :]
            for k in range(KA):
                q, r = divmod(2 + k, 8)
                acc = acc + _shifted(ext_ref, sh_ref, r, base + 8 * q, CONV_CHUNK) * cw_ref[k:k + 1, :]
            a1_ref[pl.ds(base, CONV_CHUNK), :] = acc
            return carry

        lax.fori_loop(0, ts // CONV_CHUNK, conv_chunk, 0)
        keep, _ = _tril_masks()

        def block(bi, carry):
            rows = pl.ds(pl.multiple_of(bi * GBLK, GBLK), GBLK)
            xh, _ = _ln_stats(a1_ref[rows, :])
            a = _silu(xh * vec_ref[1:2, :] + vec_ref[2:3, :]) * _silu(z_ref[0, rows, 2 * W_BR:3 * W_BR])
            y_ref[0, rows, 0:W_BR] = a.astype(y_ref.dtype)
            ua = _gelu(z_ref[0, rows, 3 * W_BR:4 * W_BR])
            vh, _ = _ln_stats(_gelu(z_ref[0, rows, 4 * W_BR:5 * W_BR]))
            vl = vh * vec_ref[3:4, :] + vec_ref[4:5, :]
            sg = _spatial(ws_ref, keep, _mm(vl)) + bsf_ref[...]
            g = ua * sg * _silu(z_ref[0, rows, 5 * W_BR:6 * W_BR])
            y_ref[0, rows, W_BR:2 * W_BR] = g.astype(y_ref.dtype)
            return carry

        lax.fori_loop(0, ts // GBLK, block, 0)

    full = lambda shape: pl.BlockSpec(shape, lambda b, s: (0,) * len(shape))
    return pl.pallas_call(
        body, name="even_fwd", grid=(bsz, s_len // ts),
        in_specs=[pl.BlockSpec((1, ts, N_COLS), lambda b, s: (b, s, 0)), _halo_specs(ts, s_len, 2 * W_BR),
                  full((32, W_BR)), full((8, W_BR)), full((8, GBLK, GBLK)), full((GBLK, W_BR))],
        out_specs=pl.BlockSpec((1, ts, 2 * W_BR), lambda b, s: (b, s, 0)),
        out_shape=jax.ShapeDtypeStruct((bsz, s_len, 2 * W_BR), MM_DTYPE),
        scratch_shapes=[pltpu.VMEM((ext_rows + 8, W_BR), F32), pltpu.VMEM((7, ext_rows, W_BR), F32),
                        pltpu.VMEM((ts, W_BR), F32)],
        compiler_params=_cparams(("parallel", "parallel")),
    )(z3, z3, cw, vec, ws, bsf)


def _even_bwd(z3, dy3, cw, vec, ws, wst, bsf):
    bsz, s_len, _ = z3.shape
    ts = min(SEQ_TILE, s_len)
    n_s = s_len // ts
    ext_rows = ts + 2 * HALO
    a_rows = ts + HALO

    def body(z_ref, zl_ref, zr_ref, dy_ref, dyr_ref, cw_ref, vec_ref, ws_ref, wst_ref, bsf_ref,
             dz_ref, dbin_ref, dcw_ref, dvec_ref, dws_ref, dbsf_ref,
             ext_ref, sh_ref, a1_ref, ag_ref, dya_ref, d_ref, accw_ref):
        b = pl.program_id(0)
        s = pl.program_id(1)

        @pl.when((b == 0) & (s == 0))
        def _():
            dbin_ref[...] = jnp.zeros_like(dbin_ref)
            dcw_ref[...] = jnp.zeros_like(dcw_ref)
            dvec_ref[...] = jnp.zeros_like(dvec_ref)
            dws_ref[...] = jnp.zeros_like(dws_ref)
            dbsf_ref[...] = jnp.zeros_like(dbsf_ref)

        has_right = s < n_s - 1
        hl = zl_ref[0]
        hr = zr_ref[0]
        ext_ref[0:HALO, :] = jnp.where(s > 0, hl[:, 0:W_BR] * _sigmoid(hl[:, W_BR:2 * W_BR]), 0.0)
        ext_ref[HALO:HALO + ts, :] = z_ref[0, :, 0:W_BR] * _sigmoid(z_ref[0, :, W_BR:2 * W_BR])
        ext_ref[HALO + ts:ext_rows, :] = hr[:, 0:W_BR] * _sigmoid(hr[:, W_BR:2 * W_BR])
        ext_ref[ext_rows:ext_rows + 8, :] = jnp.zeros((8, W_BR), F32)
        ag_ref[0:ts, :] = z_ref[0, :, 2 * W_BR:3 * W_BR]
        ag_ref[ts:a_rows, :] = hr[:, 2 * W_BR:3 * W_BR]
        dya_ref[0:ts, :] = dy_ref[0, :, 0:W_BR]
        dya_ref[ts:a_rows, :] = dyr_ref[0][:, 0:W_BR]
        _build_shifts(ext_ref, sh_ref, ext_rows)

        def conv_chunk(ci, carry):
            base = pl.multiple_of(ci * CONV_CHUNK, CONV_CHUNK)
            acc = jnp.zeros((CONV_CHUNK, W_BR), F32) + vec_ref[0:1, :]
            for k in range(KA):
                q, r = divmod(2 + k, 8)
                acc = acc + _shifted(ext_ref, sh_ref, r, base + 8 * q, CONV_CHUNK) * cw_ref[k:k + 1, :]
            a1_ref[pl.ds(base, CONV_CHUNK), :] = acc
            return carry

        lax.fori_loop(0, a_rows // CONV_CHUNK, conv_chunk, 0)

        def a_chunk(base, n, main):
            rows = pl.ds(base, n)
            xh, rs = _ln_stats(a1_ref[rows, :])
            ln = xh * vec_ref[1:2, :] + vec_ref[2:3, :]
            sl, dsl = _silu_grad(ln)
            sgt, dsgt = _silu_grad(ag_ref[rows, :])
            dya = dya_ref[rows, :]
            dln = dya * sgt * dsl
            da1 = _ln_bwd(dln * vec_ref[1:2, :], xh, rs)
            if main:
                d_ref[rows, :] = da1
                dag = dya * sl * dsgt
                dz_ref[0, rows, 2 * W_BR:3 * W_BR] = dag.astype(dz_ref.dtype)
                dbin_ref[0:1, 2 * W_BR:3 * W_BR] += _rowsum(dag)
                dvec_ref[0:1, :] += _rowsum(da1)
                dvec_ref[1:2, :] += _rowsum(dln * xh)
                dvec_ref[2:3, :] += _rowsum(dln)
            else:
                d_ref[rows, :] = jnp.where(has_right, da1, 0.0)

        def a_main(ci, carry):
            a_chunk(pl.multiple_of(ci * GBLK, GBLK), GBLK, True)
            return carry

        lax.fori_loop(0, ts // GBLK, a_main, 0)
        a_chunk(ts, HALO, False)
        d_ref[a_rows:a_rows + 8, :] = jnp.zeros((8, W_BR), F32)

        accw_ref[...] = jnp.zeros_like(accw_ref)

        def dw_chunk(ci, carry):
            base = pl.multiple_of(ci * CONV_CHUNK, CONV_CHUNK)
            d = d_ref[pl.ds(base, CONV_CHUNK), :]
            for k in range(KA):
                q, r = divmod(2 + k, 8)
                prod = d * _shifted(ext_ref, sh_ref, r, base + 8 * q, CONV_CHUNK)
                accw_ref[k] += jnp.sum(prod.reshape(CONV_CHUNK // 8, 8, W_BR), axis=0)
            return carry

        lax.fori_loop(0, ts // CONV_CHUNK, dw_chunk, 0)
        dcw_ref[...] += jnp.sum(accw_ref[...], axis=1)

        _build_shifts(d_ref, sh_ref, a_rows)

        def dx_chunk(ci, carry):
            base = pl.multiple_of(ci * CONV_CHUNK, CONV_CHUNK)
            rows = pl.ds(base, CONV_CHUNK)
            acc = jnp.zeros((CONV_CHUNK, W_BR), F32)
            for m in range(KA):
                q, r = divmod(m, 8)
                acc = acc + _shifted(d_ref, sh_ref, r, base + 8 * q, CONV_CHUNK) * cw_ref[KA - 1 - m:KA - m, :]
            aval = z_ref[0, rows, 0:W_BR]
            sg = _sigmoid(z_ref[0, rows, W_BR:2 * W_BR])
            dval = acc * sg
            dglu = acc * aval * sg * (1.0 - sg)
            dz_ref[0, rows, 0:W_BR] = dval.astype(dz_ref.dtype)
            dz_ref[0, rows, W_BR:2 * W_BR] = dglu.astype(dz_ref.dtype)
            dbin_ref[0:1, 0:W_BR] += _rowsum(dval)
            dbin_ref[0:1, W_BR:2 * W_BR] += _rowsum(dglu)
            return carry

        lax.fori_loop(0, ts // CONV_CHUNK, dx_chunk, 0)

        keep, keep_t = _tril_masks()
        lane = lax.broadcasted_iota(jnp.int32, (GBLK, GBLK), 1)

        def block(bi, carry):
            rows = pl.ds(pl.multiple_of(bi * GBLK, GBLK), GBLK)
            ua, dua = _gelu_grad(z_ref[0, rows, 3 * W_BR:4 * W_BR])
            va, dva = _gelu_grad(z_ref[0, rows, 4 * W_BR:5 * W_BR])
            sgt, dsgt = _silu_grad(z_ref[0, rows, 5 * W_BR:6 * W_BR])
            vh, rs = _ln_stats(va)
            vlb = _mm(vh * vec_ref[3:4, :] + vec_ref[4:5, :])
            sg = _spatial(ws_ref, keep, vlb) + bsf_ref[...]
            dyg = dy_ref[0, rows, W_BR:2 * W_BR]
            du = dyg * sg * sgt * dua
            dsg = dyg * ua * sgt
            dgg = dyg * ua * sg * dsgt
            dvl = _spatial(wst_ref, keep_t, _mm(dsg))
            for p in range(4):
                dsp = dsg[:, p * GBLK:(p + 1) * GBLK]
                vlp = vlb[:, p * GBLK:(p + 1) * GBLK]
                dws_ref[2 * p] += jnp.where(keep, _dot_nt(jnp.where(lane < 64, dsp, 0.0), vlp), 0.0)
                dws_ref[2 * p + 1] += jnp.where(keep, _dot_nt(jnp.where(lane >= 64, dsp, 0.0), vlp), 0.0)
            dbsf_ref[...] += dsg
            dvec_ref[3:4, :] += _rowsum(dvl * vh)
            dvec_ref[4:5, :] += _rowsum(dvl)
            dv = _ln_bwd(dvl * vec_ref[3:4, :], vh, rs) * dva
            dz_ref[0, rows, 3 * W_BR:4 * W_BR] = du.astype(dz_ref.dtype)
            dz_ref[0, rows, 4 * W_BR:5 * W_BR] = dv.astype(dz_ref.dtype)
            dz_ref[0, rows, 5 * W_BR:6 * W_BR] = dgg.astype(dz_ref.dtype)
            dbin_ref[0:1, 3 * W_BR:4 * W_BR] += _rowsum(du)
            dbin_ref[0:1, 4 * W_BR:5 * W_BR] += _rowsum(dv)
            dbin_ref[0:1, 5 * W_BR:6 * W_BR] += _rowsum(dgg)
            return carry

        lax.fori_loop(0, ts // GBLK, block, 0)

    full = lambda shape: pl.BlockSpec(shape, lambda b, s: (0,) * len(shape))
    acc_shapes = [(1, N_COLS), (32, W_BR), (8, W_BR), (8, GBLK, GBLK), (GBLK, W_BR)]
    return pl.pallas_call(
        body, name="even_bwd", grid=(bsz, n_s),
        in_specs=[pl.BlockSpec((1, ts, N_COLS), lambda b, s: (b, s, 0)),
                  _halo_specs(ts, s_len, N_COLS, True), _halo_specs(ts, s_len, N_COLS, False),
                  pl.BlockSpec((1, ts, 2 * W_BR), lambda b, s: (b, s, 0)), _halo_specs(ts, s_len, 2 * W_BR, False),
                  full((32, W_BR)), full((8, W_BR)), full((8, GBLK, GBLK)), full((8, GBLK, GBLK)), full((GBLK, W_BR))],
        out_specs=[pl.BlockSpec((1, ts, N_COLS), lambda b, s: (b, s, 0))] + [full(sh) for sh in acc_shapes],
        out_shape=[jax.ShapeDtypeStruct((bsz, s_len, N_COLS), MM_DTYPE)] + [jax.ShapeDtypeStruct(sh, F32) for sh in acc_shapes],
        scratch_shapes=[pltpu.VMEM((ext_rows + 8, W_BR), F32), pltpu.VMEM((7, ext_rows, W_BR), F32),
                        pltpu.VMEM((a_rows, W_BR), F32), pltpu.VMEM((a_rows, W_BR), F32), pltpu.VMEM((a_rows, W_BR), F32),
                        pltpu.VMEM((a_rows + 8, W_BR), F32), pltpu.VMEM((32, 8, W_BR), F32)],
        compiler_params=_cparams(("arbitrary", "arbitrary")),
    )(z3, z3, z3, dy3, dy3, cw, vec, ws, wst, bsf)


def _pool_stages(e_refs, rows):
    e0, e1, e2, e3, e4 = e_refs
    e1[8:rows, :] = e0[8:rows, :] + e0[7:rows - 1, :]
    e2[16:rows, GBLK:] = e1[16:rows, GBLK:] + e1[14:rows - 2, GBLK:]
    e3[24:rows, 2 * GBLK:] = e2[24:rows, 2 * GBLK:] + e2[20:rows - 4, 2 * GBLK:]
    e4[32:rows, 3 * GBLK:] = e3[32:rows, 3 * GBLK:] + e3[24:rows - 8, 3 * GBLK:]


def _pool_counts(start, n):
    pos = (start + 1 + lax.broadcasted_iota(jnp.int32, (n, 1), 0)).astype(F32)
    return [jnp.minimum(pos, float(w)) for w in POOL_WINDOWS]


def _pooled_into(e_refs, pooled_ref, s, ts):
    cnt = _pool_counts(s * ts, ts)
    for g in range(4):
        cs = slice(g * GBLK, (g + 1) * GBLK)
        pooled_ref[:, cs] = e_refs[g + 1][HALO:HALO + ts, cs] / cnt[g] - e_refs[0][HALO:HALO + ts, cs]


def _odd_fwd(z3, wp, vec):
    bsz, s_len, _ = z3.shape
    ts = min(SEQ_TILE, s_len)
    ext_rows = ts + HALO

    def body(z_ref, zl_ref, wp_ref, vec_ref, y_ref, e0, e1, e2, e3, e4, pooled_ref, dext_ref, ec_ref):
        s = pl.program_id(1)
        hl = zl_ref[0]
        e0[0:HALO, :] = jnp.where(s > 0, hl[:, 0:W_BR], 0.0)
        e0[HALO:ext_rows, :] = z_ref[0, :, 0:W_BR]
        _pool_stages((e0, e1, e2, e3, e4), ext_rows)
        _pooled_into((e0, e1, e2, e3, e4), pooled_ref, s, ts)
        dext_ref[0:HALO, :] = jnp.where(s > 0, hl[:, 2 * W_BR:3 * W_BR] * hl[:, 4 * W_BR:5 * W_BR], 0.0)
        dext_ref[HALO:ext_rows, :] = z_ref[0, :, 2 * W_BR:3 * W_BR] * z_ref[0, :, 4 * W_BR:5 * W_BR]
        ec_ref[...] = (vec_ref[1:2, :] * dext_ref[HALO - 2:HALO - 2 + ts, :] + vec_ref[2:3, :] * dext_ref[HALO - 1:HALO - 1 + ts, :]
                       + vec_ref[3:4, :] * dext_ref[HALO:HALO + ts, :])

        def block(bi, carry):
            rows = pl.ds(pl.multiple_of(bi * GBLK, GBLK), GBLK)
            pb = _mm(pooled_ref[rows, :])
            cpre = jnp.concatenate([jnp.dot(pb[:, g * GBLK:(g + 1) * GBLK], wp_ref[g], preferred_element_type=F32)
                                    for g in range(4)], axis=1)
            c = cpre * vec_ref[0:1, :] * _silu(z_ref[0, rows, W_BR:2 * W_BR])
            d = z_ref[0, rows, 3 * W_BR:4 * W_BR] * ec_ref[rows, :] * _silu(z_ref[0, rows, 5 * W_BR:6 * W_BR])
            y_ref[0, rows, 0:W_BR] = c.astype(y_ref.dtype)
            y_ref[0, rows, W_BR:2 * W_BR] = d.astype(y_ref.dtype)
            return carry

        lax.fori_loop(0, ts // GBLK, block, 0)

    full = lambda shape: pl.BlockSpec(shape, lambda b, s: (0,) * len(shape))
    ebuf = pltpu.VMEM((ext_rows, W_BR), F32)
    return pl.pallas_call(
        body, name="odd_fwd", grid=(bsz, s_len // ts),
        in_specs=[pl.BlockSpec((1, ts, N_COLS), lambda b, s: (b, s, 0)), _halo_specs(ts, s_len, N_COLS),
                  full((4, GBLK, GBLK)), full((8, W_BR))],
        out_specs=pl.BlockSpec((1, ts, 2 * W_BR), lambda b, s: (b, s, 0)),
        out_shape=jax.ShapeDtypeStruct((bsz, s_len, 2 * W_BR), MM_DTYPE),
        scratch_shapes=[ebuf, ebuf, ebuf, ebuf, ebuf, pltpu.VMEM((ts, W_BR), F32), ebuf, pltpu.VMEM((ts, W_BR), F32)],
        compiler_params=_cparams(("parallel", "parallel")),
    )(z3, z3, wp, vec)


def _odd_bwd(z3, dy3, wp, vec):
    bsz, s_len, _ = z3.shape
    ts = min(SEQ_TILE, s_len)
    n_s = s_len // ts
    ext_rows = ts + HALO

    def body(z_ref, zl_ref, zr_ref, dy_ref, dyr_ref, wp_ref, vec_ref,
             dz_ref, dbin_ref, dwp_ref, dvec_ref,
             e0, e1, e2, e3, e4, pooled_ref, dext_ref, ec_ref, q_ref, dp_ref, de_ref, f1, f2, f3, f4):
        b = pl.program_id(0)
        s = pl.program_id(1)

        @pl.when((b == 0) & (s == 0))
        def _():
            dbin_ref[...] = jnp.zeros_like(dbin_ref)
            dwp_ref[...] = jnp.zeros_like(dwp_ref)
            dvec_ref[...] = jnp.zeros_like(dvec_ref)

        has_right = s < n_s - 1
        hl = zl_ref[0]
        e0[0:HALO, :] = jnp.where(s > 0, hl[:, 0:W_BR], 0.0)
        e0[HALO:ext_rows, :] = z_ref[0, :, 0:W_BR]
        _pool_stages((e0, e1, e2, e3, e4), ext_rows)
        _pooled_into((e0, e1, e2, e3, e4), pooled_ref, s, ts)
        dext_ref[0:HALO, :] = jnp.where(s > 0, hl[:, 2 * W_BR:3 * W_BR] * hl[:, 4 * W_BR:5 * W_BR], 0.0)
        dext_ref[HALO:ext_rows, :] = z_ref[0, :, 2 * W_BR:3 * W_BR] * z_ref[0, :, 4 * W_BR:5 * W_BR]
        ec_ref[...] = (vec_ref[1:2, :] * dext_ref[HALO - 2:HALO - 2 + ts, :] + vec_ref[2:3, :] * dext_ref[HALO - 1:HALO - 1 + ts, :]
                       + vec_ref[3:4, :] * dext_ref[HALO:HALO + ts, :])

        def grads(zc_gate, zd_b, zd_gate, dyc, dyd, rows_out, n, start, valid):
            sgt = _silu(zc_gate)
            dcpre = dyc * vec_ref[0:1, :] * sgt
            db = _mm(dcpre)
            dpool = jnp.concatenate([_dot_nt(db[:, g * GBLK:(g + 1) * GBLK], wp_ref[g]) for g in range(4)], axis=1)
            cnt = _pool_counts(start, n)
            q = jnp.concatenate([dpool[:, g * GBLK:(g + 1) * GBLK] / cnt[g] for g in range(4)], axis=1)
            de = dyd * zd_b * _silu(zd_gate)
            if valid is not None:
                q = jnp.where(valid, q, 0.0)
                de = jnp.where(valid, de, 0.0)
            q_ref[rows_out, :] = q
            dp_ref[rows_out, :] = dpool
            de_ref[rows_out, :] = de
            return dcpre

        def block(bi, carry):
            base = pl.multiple_of(bi * GBLK, GBLK)
            rows = pl.ds(base, GBLK)
            cg = z_ref[0, rows, W_BR:2 * W_BR]
            dyc = dy_ref[0, rows, 0:W_BR]
            dyd = dy_ref[0, rows, W_BR:2 * W_BR]
            d_b = z_ref[0, rows, 3 * W_BR:4 * W_BR]
            d_gate = z_ref[0, rows, 5 * W_BR:6 * W_BR]
            dcpre = grads(cg, d_b, d_gate, dyc, dyd, rows, GBLK, s * ts + base, None)
            pb = _mm(pooled_ref[rows, :])
            dcb = _mm(dcpre)
            cpre = jnp.concatenate([jnp.dot(pb[:, g * GBLK:(g + 1) * GBLK], wp_ref[g], preferred_element_type=F32)
                                    for g in range(4)], axis=1)
            for g in range(4):
                cs = slice(g * GBLK, (g + 1) * GBLK)
                dwp_ref[g] += _dot_tn(pb[:, cs], dcb[:, cs])
            sgt, dsgt = _silu_grad(cg)
            dvec_ref[0:1, :] += _rowsum(dyc * cpre * sgt)
            dcg = dyc * cpre * vec_ref[0:1, :] * dsgt
            sdt, dsdt = _silu_grad(d_gate)
            ec = ec_ref[rows, :]
            ddb = dyd * ec * sdt
            ddg = dyd * d_b * ec * dsdt
            dz_ref[0, rows, W_BR:2 * W_BR] = dcg.astype(dz_ref.dtype)
            dz_ref[0, rows, 3 * W_BR:4 * W_BR] = ddb.astype(dz_ref.dtype)
            dz_ref[0, rows, 5 * W_BR:6 * W_BR] = ddg.astype(dz_ref.dtype)
            dbin_ref[0:1, W_BR:2 * W_BR] += _rowsum(dcg)
            dbin_ref[0:1, 3 * W_BR:4 * W_BR] += _rowsum(ddb)
            dbin_ref[0:1, 5 * W_BR:6 * W_BR] += _rowsum(ddg)
            return carry

        lax.fori_loop(0, ts // GBLK, block, 0)
        hr = zr_ref[0]
        dyr = dyr_ref[0]
        grads(hr[:, W_BR:2 * W_BR], hr[:, 3 * W_BR:4 * W_BR], hr[:, 5 * W_BR:6 * W_BR], dyr[:, 0:W_BR], dyr[:, W_BR:2 * W_BR],
              slice(ts, ext_rows), HALO, (s + 1) * ts, has_right)

        r1, r2, r3 = ts + 24, ts + 16, ts + 8
        f1[0:r1, :] = q_ref[0:r1, :] + q_ref[1:r1 + 1, :]
        f2[0:r2, GBLK:] = f1[0:r2, GBLK:] + f1[2:r2 + 2, GBLK:]
        f3[0:r3, 2 * GBLK:] = f2[0:r3, 2 * GBLK:] + f2[4:r3 + 4, 2 * GBLK:]
        f4[0:ts, 3 * GBLK:] = f3[0:ts, 3 * GBLK:] + f3[8:ts + 8, 3 * GBLK:]
        for g, f in enumerate((f1, f2, f3, f4)):
            cs = slice(g * GBLK, (g + 1) * GBLK)
            dvg = f[0:ts, cs] - dp_ref[0:ts, cs]
            dz_ref[0, :, cs] = dvg.astype(dz_ref.dtype)
            dbin_ref[0:1, cs] += _rowsum(dvg)

        ddc = (vec_ref[1:2, :] * de_ref[2:ts + 2, :] + vec_ref[2:3, :] * de_ref[1:ts + 1, :] + vec_ref[3:4, :] * de_ref[0:ts, :])
        d_h = z_ref[0, :, 2 * W_BR:3 * W_BR]
        d_c = z_ref[0, :, 4 * W_BR:5 * W_BR]
        ddh = ddc * d_c
        ddcc = ddc * d_h
        dz_ref[0, :, 2 * W_BR:3 * W_BR] = ddh.astype(dz_ref.dtype)
        dz_ref[0, :, 4 * W_BR:5 * W_BR] = ddcc.astype(dz_ref.dtype)
        dbin_ref[0:1, 2 * W_BR:3 * W_BR] += _rowsum(ddh)
        dbin_ref[0:1, 4 * W_BR:5 * W_BR] += _rowsum(ddcc)
        de = de_ref[0:ts, :]
        for k in range(3):
            dvec_ref[1 + k:2 + k, :] += _rowsum(de * dext_ref[HALO - 2 + k:HALO - 2 + k + ts, :])

    full = lambda shape: pl.BlockSpec(shape, lambda b, s: (0,) * len(shape))
    acc_shapes = [(1, N_COLS), (4, GBLK, GBLK), (8, W_BR)]
    ebuf = pltpu.VMEM((ext_rows, W_BR), F32)
    tbuf = pltpu.VMEM((ts, W_BR), F32)
    return pl.pallas_call(
        body, name="odd_bwd", grid=(bsz, n_s),
        in_specs=[pl.BlockSpec((1, ts, N_COLS), lambda b, s: (b, s, 0)),
                  _halo_specs(ts, s_len, N_COLS, True), _halo_specs(ts, s_len, N_COLS, False),
                  pl.BlockSpec((1, ts, 2 * W_BR), lambda b, s: (b, s, 0)), _halo_specs(ts, s_len, 2 * W_BR, False),
                  full((4, GBLK, GBLK)), full((8, W_BR))],
        out_specs=[pl.BlockSpec((1, ts, N_COLS), lambda b, s: (b, s, 0))] + [full(sh) for sh in acc_shapes],
        out_shape=[jax.ShapeDtypeStruct((bsz, s_len, N_COLS), MM_DTYPE)] + [jax.ShapeDtypeStruct(sh, F32) for sh in acc_shapes],
        scratch_shapes=[ebuf, ebuf, ebuf, ebuf, ebuf, tbuf, ebuf, tbuf, ebuf, ebuf, ebuf, ebuf, ebuf, ebuf, tbuf],
        compiler_params=_cparams(("arbitrary", "arbitrary")),
    )(z3, z3, z3, dy3, dy3, wp, vec)


def _post_fwd(y2, x2, p_layer, w_out, wg, wple, vec, tgt=None):
    p_all, layer = p_layer
    t, d = x2.shape
    tm = min(ROW_TILE, t)
    last = tgt is not None

    def body(*refs):
        y_ref, x_ref, p_ref, wo_ref, wg_ref, wp_ref, vec_ref = refs[:7]
        xn_ref, r_ref, gate_ref = refs[7 + last:10 + last]
        r = ALPHA * x_ref[...] + jnp.dot(y_ref[...], wo_ref[...], preferred_element_type=F32) + vec_ref[0:1, :]
        r_ref[...] = r
        xh, _ = _ln_stats(r)
        h = xh * vec_ref[1:2, :] + vec_ref[2:3, :]
        gate = _sigmoid(_dot(h, wg_ref[...]) + vec_ref[3:4, :])
        gate_ref[...] = gate
        xn = h + gate * _dot(p_ref[...], wp_ref[...])
        if last:
            sq_ref = refs[11]

            @pl.when(pl.program_id(0) == 0)
            def _():
                sq_ref[...] = jnp.zeros_like(sq_ref)
            e = xn - refs[7][...]
            xn_ref[...] = e / float(d)
            sq_ref[...] += _rowsum(e * e)
        else:
            xn_ref[...] = xn

    row = lambda c: pl.BlockSpec((tm, c), lambda i: (i, 0))
    full = lambda shape: pl.BlockSpec(shape, lambda i: (0,) * len(shape))
    return pl.pallas_call(
        body, name="post_fwd_loss" if last else "post_fwd", grid=(t // tm,),
        in_specs=[row(d), row(d), pl.BlockSpec((None, tm, D_PLE), lambda i: (layer, i, 0)),
                  full((d, d)), full((d, d)), full((D_PLE, d)), full((8, d))] + [row(d)] * last,
        out_specs=[row(d), row(d), row(d)] + [full((1, d))] * last,
        out_shape=[jax.ShapeDtypeStruct((t, d), F32)] * 3 + [jax.ShapeDtypeStruct((1, d), F32)] * last,
        compiler_params=_cparams(("arbitrary",) if last else ("parallel",)),
    )(y2, x2, p_all, w_out, wg, wple, vec, *([tgt] if last else []))


def _post_bwd(dxn, r2, gate2, p_layer, y2, w_out, wg, wple, vec):
    p_all, layer = p_layer
    t, d = r2.shape
    tm = min(BWD_ROW_TILE, t)

    def body(dxn_ref, r_ref, gate_ref, p_ref, y_ref, wo_ref, wg_ref, wp_ref, vec_ref,
             dxr_ref, dy_ref, dwo_ref, dwg_ref, dwp_ref, dvec_ref):
        @pl.when(pl.program_id(0) == 0)
        def _():
            dwo_ref[...] = jnp.zeros_like(dwo_ref)
            dwg_ref[...] = jnp.zeros_like(dwg_ref)
            dwp_ref[...] = jnp.zeros_like(dwp_ref)
            dvec_ref[...] = jnp.zeros_like(dvec_ref)

        dxn = dxn_ref[...]
        gate = gate_ref[...]
        xh, rs = _ln_stats(r_ref[...])
        hb = _mm(xh * vec_ref[1:2, :] + vec_ref[2:3, :])
        pb = _mm(p_ref[...])
        pe = jnp.dot(pb, wp_ref[...], preferred_element_type=F32)
        dpre = dxn * pe * gate * (1.0 - gate)
        dpb = _mm(dpre)
        dh = dxn + _dot_nt(dpb, wg_ref[...])
        dwg_ref[...] += _dot_tn(hb, dpb)
        dwp_ref[...] += _dot_tn(pb, dxn * gate)
        dr = _ln_bwd(dh * vec_ref[1:2, :], xh, rs)
        drb = _mm(dr)
        dxr_ref[...] = ALPHA * dr
        dy_ref[...] = _dot_nt(drb, wo_ref[...])
        dwo_ref[...] += _dot_tn(y_ref[...], drb)
        dvec_ref[0:1, :] += _rowsum(dr)
        dvec_ref[1:2, :] += _rowsum(dh * xh)
        dvec_ref[2:3, :] += _rowsum(dh)
        dvec_ref[3:4, :] += _rowsum(dpre)

    row = lambda c: pl.BlockSpec((tm, c), lambda i: (i, 0))
    full = lambda shape: pl.BlockSpec(shape, lambda i: (0,) * len(shape), pipeline_mode=pl.Buffered(1))
    acc_shapes = [(d, d), (d, d), (D_PLE, d), (8, d)]
    return pl.pallas_call(
        body, name="post_bwd", grid=(t // tm,),
        in_specs=[row(d), row(d), row(d), pl.BlockSpec((None, tm, D_PLE), lambda i: (layer, i, 0)), row(d),
                  full((d, d)), full((d, d)), full((D_PLE, d)), full((8, d))],
        out_specs=[row(d), row(d)] + [full(sh) for sh in acc_shapes],
        out_shape=[jax.ShapeDtypeStruct((t, d), F32)] * 2 + [jax.ShapeDtypeStruct(sh, F32) for sh in acc_shapes],
        compiler_params=_cparams(("arbitrary",)),
    )(dxn, r2, gate2, p_all, y2, w_out, wg, wple, vec)


def _place():
    x, y, c = lax.axis_index("x"), lax.axis_index("y"), lax.axis_index("c")
    chips = [(1 - x, y), (x, 1 - y), (1 - x, 1 - y)]
    return x, y, c, chips


def _shard_of(ref, ax, k, width, lo=None, ln=None):
    idx = [slice(None)] * 3
    if lo is not None:
        idx[0] = pl.ds(lo, ln)
    idx[ax] = pl.ds(k * width, width)
    return ref.at[tuple(idx)]


def _remote(src, dst, ssem, rsem, dev):
    return pltpu.make_async_remote_copy(src_ref=src, dst_ref=dst, send_sem=ssem, recv_sem=rsem, device_id=dev, device_id_type=MESH)


def _place_shard(w, ax, chip, dtype=MM_DTYPE, layer=None):
    l_dim, a_dim, b_dim = w.shape
    tr = min(256, a_dim)
    per = a_dim // tr
    first = 0
    if layer is not None:
        l_dim, first = 1, layer
    shape = [l_dim, a_dim, b_dim]
    shape[ax] *= 4
    if ax == 2:
        out_spec = pl.BlockSpec((1, tr, b_dim), lambda l, i, k: (l, i, k[0]))
    else:
        out_spec = pl.BlockSpec((1, tr, b_dim), lambda l, i, k: (l, k[0] * per + i, 0))

    def body(k_ref, w_ref, o_ref):
        o_ref[...] = w_ref[...].astype(o_ref.dtype)

    return pl.pallas_call(
        body, name="place_shard",
        grid_spec=pltpu.PrefetchScalarGridSpec(
            num_scalar_prefetch=1, grid=(l_dim, per),
            in_specs=[pl.BlockSpec((1, tr, b_dim), lambda l, i, k: (first + l, i, 0))], out_specs=out_spec),
        out_shape=jax.ShapeDtypeStruct(tuple(shape), dtype),
        compiler_params=_cparams(("parallel", "parallel")),
    )(chip, w)


def _shard_copies(src_refs, dst_refs, axes, sems):
    x, y, c, chips = _place()
    j = 2 * x + y
    cps = []
    for a, (s_ref, d_ref) in enumerate(zip(src_refs, dst_refs)):
        w = d_ref.shape[axes[a]] // 4
        for q, (qx, qy) in enumerate(chips):
            ssem, rsem = sems[3 * a + q]
            cps.append(_remote(_shard_of(s_ref, axes[a], j, w), _shard_of(d_ref, axes[a], j, w), ssem, rsem, (qx, qy, c)))
    return cps


def _chip_handshake():
    x, y, c, chips = _place()
    barrier = pltpu.get_barrier_semaphore()
    for qx, qy in chips:
        pl.semaphore_signal(barrier, inc=1, device_id=(qx, qy, c), device_id_type=MESH)
    pl.semaphore_wait(barrier, 3)


def _gather_async(name, collective_id, fulls, axes):
    n = len(fulls)
    refs = [jax.new_ref(f, memory_space=pltpu.MemorySpace.HBM) for f in fulls]

    @pl.kernel(mesh=plsc.ScalarSubcoreMesh(axis_name="seq", num_cores=1), name=name,
               scratch_types=(pltpu.SemaphoreType.DMA,) * (6 * n),
               compiler_params=pltpu.CompilerParams(collective_id=collective_id))
    def launch(*sems):
        _chip_handshake()
        cps = _shard_copies(refs, refs, axes, [(sems[2 * k], sems[2 * k + 1]) for k in range(3 * n)])
        for cp in cps:
            cp.start()
        for cp in cps:
            cp.wait()

    launch()
    return refs


def _scatter_async(name, collective_id, grads, axes):
    n = len(grads)
    outs = []
    for g, ax in zip(grads, axes):
        sh = list(g.shape)
        sh[ax] //= 4
        outs.append(jax.ShapeDtypeStruct((3,) + tuple(sh), g.dtype))

    def body(*refs):
        srcs, lands, sems = refs[:n], refs[n:2 * n], refs[2 * n:]
        _chip_handshake()
        x, y, c, chips = _place()
        cps = []
        for a in range(n):
            w = srcs[a].shape[axes[a]] // 4
            for q, (qx, qy) in enumerate(chips):
                k = 3 * a + q
                cps.append(_remote(_shard_of(srcs[a], axes[a], 2 * qx + qy, w), lands[a].at[q], sems[2 * k], sems[2 * k + 1],
                                   (qx, qy, c)))
        for cp in cps:
            cp.start()
        for cp in cps:
            cp.wait()

    return pl.kernel(body, out_type=outs, mesh=plsc.ScalarSubcoreMesh(axis_name="seq", num_cores=1), name=name,
                     scratch_types=(pltpu.SemaphoreType.DMA,) * (6 * n),
                     compiler_params=pltpu.CompilerParams(collective_id=collective_id))(*grads)


def _pair_swap_async(collective_id, sums):
    n = len(sums)

    def body(*refs):
        g_refs, o_refs, sems = refs[:n], refs[n:2 * n], refs[2 * n:]
        x, y, c, _ = _place()
        barrier = pltpu.get_barrier_semaphore()
        pl.semaphore_signal(barrier, inc=1, device_id=(x, y, 1 - c), device_id_type=MESH)
        pl.semaphore_wait(barrier, 1)
        cps = [_remote(g_refs[a], o_refs[a], sems[2 * a], sems[2 * a + 1], (x, y, 1 - c)) for a in range(n)]
        for cp in cps:
            cp.start()
        for cp in cps:
            cp.wait()

    return pl.kernel(body, out_type=[jax.ShapeDtypeStruct(g.shape, g.dtype) for g in sums],
                     mesh=plsc.ScalarSubcoreMesh(axis_name="seq", num_cores=1), name="pair_swap",
                     scratch_types=(pltpu.SemaphoreType.DMA,) * (2 * n),
                     compiler_params=pltpu.CompilerParams(collective_id=collective_id))(*sums)


def _pair_exchange(grads, small):
    n = len(grads)
    outs = [jax.ShapeDtypeStruct((g.shape[0] // 2,) + g.shape[1:], g.dtype) for g in grads]
    outs.append(jax.ShapeDtypeStruct((small.shape[0] // 2, small.shape[1]), small.dtype))

    def body(*refs):
        g_refs, o_refs = refs[:n + 1], refs[n + 1:2 * n + 2]
        ssem, rsem = refs[2 * n + 2:]
        x, y, c, _ = _place()
        cps = []
        for a in range(n + 1):
            lh = g_refs[a].shape[0] // 2
            cp = _remote(g_refs[a].at[pl.ds((1 - c) * lh, lh)], o_refs[a], ssem.at[a], rsem.at[a], (x, y, 1 - c))
            cp.start()
            cps.append(cp)
        for cp in cps:
            cp.wait()

    return pl.pallas_call(
        body, name="pair_exchange", in_specs=[ANY] * (n + 1), out_specs=[ANY] * (n + 1), out_shape=outs,
        scratch_shapes=[pltpu.SemaphoreType.DMA((n + 1,)), pltpu.SemaphoreType.DMA((n + 1,))],
        compiler_params=pltpu.CompilerParams(has_side_effects=True),
    )(*grads, small)


def _chip_scatter(sums, axes, small):
    n = len(sums)
    outs = []
    for g, ax in zip(sums, axes):
        sh = list(g.shape)
        sh[ax] //= 4
        outs.append(jax.ShapeDtypeStruct((3,) + tuple(sh), g.dtype))
    rq = small.shape[0] // 4
    outs.append(jax.ShapeDtypeStruct((3, rq, small.shape[1]), small.dtype))

    def body(*refs):
        g_refs, o_refs = refs[:n + 1], refs[n + 1:2 * n + 2]
        ssem, rsem = refs[2 * n + 2:]
        x, y, c, chips = _place()
        cps = []
        for a in range(n + 1):
            for q, (qx, qy) in enumerate(chips):
                k = 2 * qx + qy
                if a < n:
                    src = _shard_of(g_refs[a], axes[a], k, g_refs[a].shape[axes[a]] // 4)
                else:
                    src = g_refs[a].at[pl.ds(k * rq, rq)]
                cp = _remote(src, o_refs[a].at[q], ssem.at[a * 3 + q], rsem.at[a * 3 + q], (qx, qy, c))
                cp.start()
                cps.append(cp)
        for cp in cps:
            cp.wait()

    return pl.pallas_call(
        body, name="chip_scatter", in_specs=[ANY] * (n + 1), out_specs=[ANY] * (n + 1), out_shape=outs,
        scratch_shapes=[pltpu.SemaphoreType.DMA((3 * n + 3,)), pltpu.SemaphoreType.DMA((3 * n + 3,))],
        compiler_params=pltpu.CompilerParams(has_side_effects=True),
    )(*sums, small)


def _final_exchange(reds, small):
    n = len(reds)
    flips = [(fx, fy, fc) for fx in (0, 1) for fy in (0, 1) for fc in (0, 1)][1:]

    def body(*refs):
        g_refs, o_refs = refs[:n + 1], refs[n + 1:2 * n + 2]
        ssem, rsem = refs[2 * n + 2:]
        x, y, c, _ = _place()
        cps = []
        for a in range(n):
            lh = g_refs[a].shape[0] // 2
            cp = _remote(g_refs[a].at[pl.ds(c * lh, lh)], o_refs[a].at[pl.ds(c * lh, lh)], ssem.at[a], rsem.at[a], (x, y, 1 - c))
            cp.start()
            cps.append(cp)
        mine = 4 * c + 2 * x + y
        for f, (fx, fy, fc) in enumerate(flips):
            cp = _remote(g_refs[n].at[mine], o_refs[n].at[mine], ssem.at[n + f], rsem.at[n + f], (x ^ fx, y ^ fy, c ^ fc))
            cp.start()
            cps.append(cp)
        for cp in cps:
            cp.wait()

    return pl.pallas_call(
        body, name="final_exchange", in_specs=[ANY] * (n + 1), out_specs=[ANY] * (n + 1),
        out_shape=[jax.ShapeDtypeStruct(g.shape, g.dtype) for g in reds] + [jax.ShapeDtypeStruct(small.shape, small.dtype)],
        input_output_aliases={a: a for a in range(n + 1)},
        scratch_shapes=[pltpu.SemaphoreType.DMA((n + 7,)), pltpu.SemaphoreType.DMA((n + 7,))],
        compiler_params=pltpu.CompilerParams(has_side_effects=True),
    )(*reds, small)


def _small_allreduce(small):
    r = small.shape[0]
    rh, rq = r // 2, r // 8
    flips = [(fx, fy, fc) for fx in (0, 1) for fy in (0, 1) for fc in (0, 1)][1:]

    def body(g_ref, out_ref, pair_ref, chip_ref, s1_ref, ssem, rsem):
        x, y, c, chips = _place()
        cp = _remote(g_ref.at[pl.ds((1 - c) * rh, rh)], pair_ref, ssem.at[0], rsem.at[0], (x, y, 1 - c))
        cp.start()
        cp.wait()
        s1_ref[...] = g_ref[pl.ds(pl.multiple_of(c * rh, 8), rh), :] + pair_ref[...]
        cps = [_remote(s1_ref.at[pl.ds((2 * qx + qy) * rq, rq)], chip_ref.at[q], ssem.at[1 + q], rsem.at[1 + q], (qx, qy, c))
               for q, (qx, qy) in enumerate(chips)]
        for cp in cps:
            cp.start()
        for cp in cps:
            cp.wait()
        mine = out_ref.at[pl.ds(pl.multiple_of((4 * c + 2 * x + y) * rq, 8), rq)]
        mine[...] = ((s1_ref[pl.ds(pl.multiple_of((2 * x + y) * rq, 8), rq), :] + chip_ref[0]) + chip_ref[1]) + chip_ref[2]
        cps = [_remote(mine, mine, ssem.at[4 + f], rsem.at[4 + f], (x ^ fx, y ^ fy, c ^ fc)) for f, (fx, fy, fc) in enumerate(flips)]
        for cp in cps:
            cp.start()
        for cp in cps:
            cp.wait()

    vm = pl.BlockSpec(memory_space=pltpu.VMEM)
    return pl.pallas_call(
        body, name="small_allreduce", in_specs=[vm], out_specs=vm, out_shape=jax.ShapeDtypeStruct(small.shape, F32),
        scratch_shapes=[pltpu.VMEM((rh, 128), F32), pltpu.VMEM((3, rq, 128), F32), pltpu.VMEM((rh, 128), F32),
                        pltpu.SemaphoreType.DMA((11,)), pltpu.SemaphoreType.DMA((11,))],
        compiler_params=pltpu.CompilerParams(has_side_effects=True, vmem_limit_bytes=VMEM_LIMIT),
    )(small)


def _to_wire(g):
    _, a_dim, b_dim = g.shape
    tr = min(256, a_dim)

    def body(g_ref, o_ref):
        o_ref[...] = g_ref[...].astype(o_ref.dtype)

    blk = pl.BlockSpec((1, tr, b_dim), lambda i: (0, i, 0))
    return pl.pallas_call(
        body, name="to_wire", grid=(a_dim // tr,), in_specs=[blk], out_specs=blk,
        out_shape=jax.ShapeDtypeStruct(g.shape, WIRE_DTYPE), compiler_params=_cparams(("parallel",)),
    )(g)


def _chip_sum(owns, gots, ax, chip):
    n_layers = len(owns)
    _, _, a_dim, b_dim = gots[0].shape
    tr = min(256, a_dim)
    per = a_dim // tr

    def own_spec(layer):
        if ax == 2:
            return pl.BlockSpec((1, tr, b_dim), lambda l, i, k: (0, jnp.where(l == layer, i, 0), k[0]))
        return pl.BlockSpec((1, tr, b_dim), lambda l, i, k: (0, k[0] * per + jnp.where(l == layer, i, 0), 0))

    def got_spec(layer):
        return pl.BlockSpec((3, 1, tr, b_dim), lambda l, i, k: (0, 0, jnp.where(l == layer, i, 0), 0))

    def body(k_ref, *refs):
        s_ref = refs[-1]
        for layer in range(n_layers):
            @pl.when(pl.program_id(0) == layer)
            def _(own_ref=refs[layer], got_ref=refs[n_layers + layer]):
                s_ref[...] = ((own_ref[...].astype(F32) + got_ref[0].astype(F32)) + got_ref[1].astype(F32)) + got_ref[2].astype(F32)

    return pl.pallas_call(
        body, name="chip_sum",
        grid_spec=pltpu.PrefetchScalarGridSpec(
            num_scalar_prefetch=1, grid=(n_layers, per),
            in_specs=[own_spec(l) for l in range(n_layers)] + [got_spec(l) for l in range(n_layers)],
            out_specs=pl.BlockSpec((1, tr, b_dim), lambda l, i, k: (l, i, 0))),
        out_shape=jax.ShapeDtypeStruct((n_layers, a_dim, b_dim), F32),
        compiler_params=_cparams(("arbitrary", "arbitrary")),
    )(chip, *owns, *gots)


def _small_pair_sum(small, got, half):
    rh = small.shape[0] // 2

    def body(h_ref, g_ref, o_ref, s_ref):
        s_ref[...] = g_ref[...] + o_ref[...]

    return pl.pallas_call(
        body, name="small_pair_sum",
        grid_spec=pltpu.PrefetchScalarGridSpec(
            num_scalar_prefetch=1, grid=(1,),
            in_specs=[pl.BlockSpec((rh, 128), lambda i, h: (h[0], 0)), pl.BlockSpec((rh, 128), lambda i, h: (0, 0))],
            out_specs=pl.BlockSpec((rh, 128), lambda i, h: (0, 0))),
        out_shape=jax.ShapeDtypeStruct((rh, 128), F32),
    )(half, small, got)


def _small_chip_sum(s1, got, pos):
    rq = got.shape[1]

    def body(k_ref, own_ref, got_ref, s_ref):
        s_ref[0] = ((own_ref[...] + got_ref[0]) + got_ref[1]) + got_ref[2]

    return pl.pallas_call(
        body, name="small_chip_sum",
        grid_spec=pltpu.PrefetchScalarGridSpec(
            num_scalar_prefetch=1, grid=(1,),
            in_specs=[pl.BlockSpec((rq, 128), lambda i, k: (k[0], 0)), pl.BlockSpec((3, rq, 128), lambda i, k: (0, 0, 0))],
            out_specs=pl.BlockSpec((1, rq, 128), lambda i, k: (4 * k[1] + k[0], 0, 0))),
        out_shape=jax.ShapeDtypeStruct((8, rq, 128), F32),
    )(pos, s1, got)


def _adam_math(w, g, m, v):
    m = ADAM_B1 * m + (1.0 - ADAM_B1) * g
    v = ADAM_B2 * v + (1.0 - ADAM_B2) * (g * g)
    m_hat = m / (1.0 - ADAM_B1 ** ADAM_STEP)
    v_hat = v / (1.0 - ADAM_B2 ** ADAM_STEP)
    return -ADAM_LR * (m_hat / (jnp.sqrt(v_hat) + ADAM_EPS) + ADAM_WD * w), m, v


def _adamw_big(w, g_mine, g_other, m, v):
    l_dim, a_dim, b_dim = w.shape
    tr = min(256, a_dim)

    def body(w_ref, g1_ref, g2_ref, m_ref, v_ref, g_ref, d_ref, nm_ref, nv_ref):
        g = g1_ref[...] + g2_ref[...]
        g_ref[...] = g
        d_ref[...], nm_ref[...], nv_ref[...] = _adam_math(w_ref[...], g, m_ref[...], v_ref[...])

    blk = pl.BlockSpec((1, tr, b_dim), lambda l, i: (l, i, 0))
    return pl.pallas_call(
        body, name="adamw_big", grid=(l_dim, a_dim // tr), in_specs=[blk] * 5, out_specs=[blk] * 4,
        out_shape=[jax.ShapeDtypeStruct(w.shape, F32)] * 4,
        compiler_params=_cparams(("parallel", "parallel")),
    )(w, g_mine, g_other, m, v)


def _adamw_small(ws, gs, ms, vs):
    n = len(ws)

    def body(*refs):
        for i in range(n):
            w_ref, g_ref, m_ref, v_ref = refs[i], refs[n + i], refs[2 * n + i], refs[3 * n + i]
            d_ref, nm_ref, nv_ref = refs[4 * n + i], refs[5 * n + i], refs[6 * n + i]
            d_ref[...], nm_ref[...], nv_ref[...] = _adam_math(w_ref[...], g_ref[...], m_ref[...], v_ref[...])

    shapes = [jax.ShapeDtypeStruct(w.shape, F32) for w in ws]
    outs = pl.pallas_call(body, name="adamw_small", out_shape=shapes * 3,
                          compiler_params=_cparams())(*ws, *gs, *ms, *vs)
    return outs[:n], outs[n:2 * n], outs[2 * n:]


def _pack(arrs, row_mult):
    parts = []
    for a in arrs:
        flat = a.reshape(-1)
        pad = (-flat.shape[0]) % 1024
        parts.append(jnp.pad(flat, (0, pad)).reshape(-1, 128))
    buf = jnp.concatenate(parts, axis=0)
    pad = (-buf.shape[0]) % row_mult
    return jnp.pad(buf, ((0, pad), (0, 0)))


def _unpack(buf, shapes):
    out, row = [], 0
    for sh in shapes:
        n = math.prod(sh)
        rows = -(-n // 1024) * 8
        out.append(buf[row:row + rows].reshape(-1)[:n].reshape(sh))
        row += rows
    return out


_NAMES = ['w_in_e', 'b_in_e', 'conv_a_w', 'conv_a_b', 'ln_a_g', 'ln_a_b', 'ln_v_g', 'ln_v_b', 'w_s', 'b_s', 'w_out_e', 'b_out_e',
          'w_in_o', 'b_in_o', 'w_pool', 'pool_scale', 'conv_d_w', 'w_out_o', 'b_out_o', 'ln_g', 'ln_b', 'w_ple', 'w_ple_gate',
          'b_ple_gate']
_BIG = ['w_in_e', 'w_out_e', 'w_in_o', 'w_out_o', 'w_ple', 'w_ple_gate']
_BIG_AXES = [2, 1, 2, 1, 2, 1]
_SMALL_SHARDED = ['conv_a_w', 'b_in_o', 'pool_scale', 'conv_d_w', 'b_out_o']


def kernel(x, p, w_in_e, b_in_e, conv_a_w, conv_a_b, ln_a_g, ln_a_b, ln_v_g, ln_v_b, w_s, b_s, w_out_e, b_out_e, w_in_o, b_in_o, w_pool, pool_scale, conv_d_w, w_out_o, b_out_o, ln_g, ln_b, w_ple, w_ple_gate, b_ple_gate, loss_target, m_w_in_e, m_b_in_e, m_conv_a_w, m_conv_a_b, m_ln_a_g, m_ln_a_b, m_ln_v_g, m_ln_v_b, m_w_s, m_b_s, m_w_out_e, m_b_out_e, m_w_in_o, m_b_in_o, m_w_pool, m_pool_scale, m_conv_d_w, m_w_out_o, m_b_out_o, m_ln_g, m_ln_b, m_w_ple, m_w_ple_gate, m_b_ple_gate, v_w_in_e, v_b_in_e, v_conv_a_w, v_conv_a_b, v_ln_a_g, v_ln_a_b, v_ln_v_g, v_ln_v_b, v_w_s, v_b_s, v_w_out_e, v_b_out_e, v_w_in_o, v_b_in_o, v_w_pool, v_pool_scale, v_conv_d_w, v_w_out_o, v_b_out_o, v_ln_g, v_ln_b, v_w_ple, v_w_ple_gate, v_b_ple_gate):
    args = locals()
    wts = {n: args[n] for n in _NAMES}
    mom = {n: args["m_" + n] for n in _NAMES}
    var = {n: args["v_" + n] for n in _NAMES}
    bsz, s_len, d = x.shape
    t = bsz * s_len
    cx, cy, cc = lax.axis_index("x"), lax.axis_index("y"), lax.axis_index("c")
    chip = (2 * cx + cy).astype(jnp.int32).reshape(1)
    half = cc.astype(jnp.int32).reshape(1)
    pos = jnp.concatenate([chip, half])

    def placed(name, ax, layer):
        return _place_shard(wts[name], ax, chip, layer=layer)

    sv = _pack([wts[n] for n in _SMALL_SHARDED], 8)
    first_refs = _gather_async("gather_first", DEPTH, [placed('w_in_e', 2, 0), _place_shard(sv[None], 1, chip, F32)], [2, 1])
    layer_refs = []
    for i in range(DEPTH):
        sfx = '_e' if i % 2 == 0 else '_o'
        items = [('w_out' + sfx, 1, i // 2), ('w_ple_gate', 1, i), ('w_ple', 2, i)] + ([('w_in' + sfx, 2, i // 2)] if i else [])
        layer_refs.append(_gather_async("gather_layer%d" % i, i, [placed(*it) for it in items], [it[1] for it in items]))
    fw = {}
    w_in_first = first_refs[0][...]
    sv_all = first_refs[1][...].reshape((4,) + sv.shape)
    small_parts = [_unpack(sv_all[k], [wts[n].shape for n in _SMALL_SHARDED]) for k in range(4)]
    for i, n in enumerate(_SMALL_SHARDED):
        fw[n] = jnp.concatenate([small_parts[k][i] for k in range(4)], axis=-1)
    for n in _NAMES:
        fw.setdefault(n, wts[n])

    def row8(rows, width):
        rows = [r.reshape(1, width) for r in rows]
        return jnp.concatenate(rows + [jnp.zeros((8 - len(rows), width), F32)], axis=0)

    x2 = x.reshape(t, d)
    saved = []
    for i in range(DEPTH):
        j = i // 2
        even = i % 2 == 0
        b_in, b_out = (fw['b_in_e'], fw['b_out_e']) if even else (fw['b_in_o'], fw['b_out_o'])
        w_in = layer_refs[i][3][...][0] if i else w_in_first[0]
        z = _in_proj(x2, w_in, b_in[j].reshape(1, N_COLS))
        z3 = z.reshape(bsz, s_len, N_COLS)
        if even:
            cw = jnp.concatenate([fw['conv_a_w'][j], jnp.zeros((1, W_BR), F32)], axis=0)
            mvec = row8([fw['conv_a_b'][j], fw['ln_a_g'][j], fw['ln_a_b'][j], fw['ln_v_g'][j], fw['ln_v_b'][j]], W_BR)
            bsf = jnp.repeat(fw['b_s'][j].T, W_BR // 8, axis=1)
            mix = (cw, mvec, fw['w_s'][j], jnp.swapaxes(fw['w_s'][j], 1, 2), bsf)
            y3 = _even_fwd(z3, cw, mvec, fw['w_s'][j], bsf)
        else:
            mvec = row8([fw['pool_scale'][j]] + [fw['conv_d_w'][j][k] for k in range(3)], W_BR)
            mix = (fw['w_pool'][j].astype(MM_DTYPE), mvec)
            y3 = _odd_fwd(z3, mix[0], mvec)
        pvec = row8([b_out[j], fw['ln_g'][i], fw['ln_b'][i], fw['b_ple_gate'][i]], d)
        post_w = (layer_refs[i][0][...][0], layer_refs[i][1][...][0], layer_refs[i][2][...][0], pvec)
        p2 = (p.reshape(DEPTH, t, D_PLE), i)
        y2 = y3.reshape(t, 2 * W_BR)
        if i < DEPTH - 1:
            xn, r2, gate2 = _post_fwd(y2, x2, p2, *post_w)
        else:
            dx, r2, gate2, sq = _post_fwd(y2, x2, p2, *post_w, tgt=loss_target.reshape(t, d))
        saved.append((x2, z3, y2, r2, gate2, p2, w_in, mix, post_w))
        x2 = xn

    gr = {n: [None] * wts[n].shape[0] for n in _NAMES}
    for i in reversed(range(DEPTH)):
        j = i // 2
        even = i % 2 == 0
        x_in, z3, y2, r2, gate2, p2, w_in, mix, post_w = saved[i]
        dxr, dy, dwo, dwg, dwp, dpv = _post_bwd(dx, r2, gate2, p2, y2, *post_w)
        dy3 = dy.reshape(bsz, s_len, 2 * W_BR)
        sfx = '_e' if even else '_o'
        gr['w_out' + sfx][j], gr['b_out' + sfx][j] = dwo, dpv[0]
        gr['w_ple_gate'][i], gr['w_ple'][i] = dwg, dwp
        gr['ln_g'][i], gr['ln_b'][i], gr['b_ple_gate'][i] = dpv[1], dpv[2], dpv[3]
        if even:
            dz3, dbin, dcw, dmv, dws, dbsf = _even_bwd(z3, dy3, *mix)
            gr['conv_a_w'][j], gr['conv_a_b'][j] = dcw[:KA], dmv[0]
            gr['ln_a_g'][j], gr['ln_a_b'][j], gr['ln_v_g'][j], gr['ln_v_b'][j] = dmv[1], dmv[2], dmv[3], dmv[4]
            gr['w_s'][j] = dws
            gr['b_s'][j] = jnp.sum(dbsf.reshape(GBLK, 8, W_BR // 8), axis=2).T
        else:
            dz3, dbin, dwpool, dmv = _odd_bwd(z3, dy3, *mix)
            gr['w_pool'][j], gr['pool_scale'][j], gr['conv_d_w'][j] = dwpool, dmv[0], dmv[1:4]
        gr['b_in' + sfx][j] = dbin[0]
        dz2 = dz3.reshape(t, N_COLS)
        dwi, dw_done = _in_proj_bwd_dw(x_in, dz2)
        post_items = [('w_out' + sfx, j, dwo[None], 1), ('w_ple_gate', i, dwg[None], 1), ('w_ple', i, dwp[None], 2)]
        in_item = ('w_in' + sfx, j, dwi[None], 2)
        batches = [post_items + [in_item]] if i else [post_items, [in_item]]
        for bi, items in enumerate(batches):
            sent = [_to_wire(it[2]) if (i == 0 and bi == 1) else it[2] for it in items]
            lands = _scatter_async("scatter_layer%d_%d" % (i, bi), DEPTH + 1 + 2 * i + bi, sent, [it[3] for it in items])
            for it, land in zip(items, lands):
                gr[it[0]][it[1]] = (it[2], land, it[3])
        dx = _in_proj_bwd_dx(dxr, dz2, w_in, dw_done)
    grad_x = dx.reshape(bsz, s_len, d)

    sums = []
    for n in _BIG:
        sums.append(_chip_sum([g[0] for g in gr[n]], [g[1] for g in gr[n]], gr[n][0][2], chip))
    others = _pair_swap_async(3 * DEPTH + 1, sums)
    small_names = [n for n in _NAMES if n not in _BIG]
    g_small_full = [jnp.stack(gr[n]) for n in small_names] + [sq]
    small_all = _small_allreduce(_pack(g_small_full, 64))
    *g_small, sq_all = _unpack(small_all, [g.shape for g in g_small_full])
    loss = 0.5 * jnp.sum(sq_all) / d
    grads = {}
    for n, g in zip(small_names, g_small):
        if n in _SMALL_SHARDED:
            w = wts[n].shape[-1]
            g = lax.dynamic_slice_in_dim(g, (2 * cx + cy) * w, w, axis=g.ndim - 1)
        grads[n] = g

    delta, new_m, new_v = {}, {}, {}
    for n, mine, other in zip(_BIG, sums, others):
        grads[n], delta[n], new_m[n], new_v[n] = _adamw_big(wts[n], mine, other, mom[n], var[n])
    ds, ms, vs = _adamw_small([wts[n] for n in small_names], [grads[n] for n in small_names],
                              [mom[n] for n in small_names], [var[n] for n in small_names])
    for n, a, b, c_ in zip(small_names, ds, ms, vs):
        delta[n], new_m[n], new_v[n] = a, b, c_

    return (loss, grad_x, *[grads[n] for n in _NAMES], *[delta[n] for n in _NAMES],
            *[new_m[n] for n in _NAMES], *[new_v[n] for n in _NAMES])
```

```python
import functools
import math

import jax
import jax.numpy as jnp
from jax import lax
from jax.experimental import pallas as pl
from jax.experimental.pallas import tpu as pltpu
from jax.experimental.pallas import tpu_sc as plsc

F32 = jnp.float32
MM_DTYPE = jnp.bfloat16
WIRE_DTYPE = jnp.bfloat16
SEQ_TILE = 512
ROW_TILE = 512
BWD_ROW_TILE = 512
HALO = 32
CONV_CHUNK = 32
ROW_CHUNK = 64
GBLK = 128
VMEM_LIMIT = 56 * 1024 * 1024

D_MODEL = 1024
W_BR = 512
N_COLS = 6 * W_BR
D_PLE = 256
KA = 31
DEPTH = 4
POOL_WINDOWS = (2, 4, 8, 16)
ALPHA = (2.0 * DEPTH) ** 0.25
LN_EPS = 1e-5
GELU_C = math.sqrt(2.0 / math.pi)

ADAM_LR, ADAM_B1, ADAM_B2, ADAM_EPS, ADAM_WD, ADAM_STEP = 0.001, 0.9, 0.999, 1e-08, 0.01, 10

MESH = pl.DeviceIdType.MESH
ANY = pl.BlockSpec(memory_space=pl.ANY)


def _cparams(sem=None):
    return pltpu.CompilerParams(dimension_semantics=sem, vmem_limit_bytes=VMEM_LIMIT)


def _sigmoid(x):
    return 1.0 / (1.0 + jnp.exp(-x))


def _silu(x):
    return x * _sigmoid(x)


def _silu_grad(x):
    s = _sigmoid(x)
    return x * s, s * (1.0 + x * (1.0 - s))


def _gelu(x):
    return 0.5 * x * (1.0 + jnp.tanh(GELU_C * (x + 0.044715 * (x * x * x))))


def _gelu_grad(x):
    x2 = x * x
    th = jnp.tanh(GELU_C * (x + 0.044715 * (x * x2)))
    return 0.5 * x * (1.0 + th), 0.5 * (1.0 + th) + 0.5 * x * (1.0 - th * th) * (GELU_C * (1.0 + 3.0 * 0.044715 * x2))


def _ln_stats(x):
    mu = jnp.mean(x, axis=-1, keepdims=True)
    d = x - mu
    var = jnp.mean(d * d, axis=-1, keepdims=True)
    rs = lax.rsqrt(var + LN_EPS)
    return d * rs, rs


def _ln_bwd(dxh, xh, rs):
    return rs * (dxh - jnp.mean(dxh, axis=-1, keepdims=True) - xh * jnp.mean(dxh * xh, axis=-1, keepdims=True))


def _mm(a):
    return a.astype(MM_DTYPE)


def _dot(a, b):
    return jnp.dot(_mm(a), _mm(b), preferred_element_type=F32)


def _dot_nt(a, b):
    return lax.dot_general(_mm(a), _mm(b), (((1,), (1,)), ((), ())), preferred_element_type=F32)


def _dot_tn(a, b):
    return lax.dot_general(_mm(a), _mm(b), (((0,), (0,)), ((), ())), preferred_element_type=F32)


def _rowsum(x):
    return jnp.sum(x, axis=0, keepdims=True)


def _in_proj(x2, w, b):
    t, d = x2.shape
    n = w.shape[1]
    tm = min(ROW_TILE, t)
    nc = 768

    def body(x_ref, w_ref, b_ref, z_ref):
        xb = _mm(x_ref[...])
        for j in range(n // nc):
            cs = slice(j * nc, (j + 1) * nc)
            z_ref[:, cs] = jnp.dot(xb, w_ref[:, cs], preferred_element_type=F32) + b_ref[:, cs]

    return pl.pallas_call(
        body, name="in_proj", grid=(t // tm,),
        in_specs=[pl.BlockSpec((tm, d), lambda i: (i, 0)), pl.BlockSpec((d, n), lambda i: (0, 0)),
                  pl.BlockSpec((1, n), lambda i: (0, 0))],
        out_specs=pl.BlockSpec((tm, n), lambda i: (i, 0)),
        out_shape=jax.ShapeDtypeStruct((t, n), F32),
        compiler_params=_cparams(("parallel",)),
    )(x2, w, b)


def _in_proj_bwd_dx(dxr, dz, w, after):
    t, d = dxr.shape
    n = w.shape[1]
    tm = min(ROW_TILE, t)

    def body(dxr_ref, dz_ref, w_ref, after_ref, dx_ref):
        dx_ref[...] = dxr_ref[...] + _dot_nt(dz_ref[...], w_ref[...])

    return pl.pallas_call(
        body, name="in_proj_bwd_dx", grid=(t // tm,),
        in_specs=[pl.BlockSpec((tm, d), lambda i: (i, 0)), pl.BlockSpec((tm, n), lambda i: (i, 0)),
                  pl.BlockSpec((d, n), lambda i: (0, 0)), pl.BlockSpec((8, 128), lambda i: (0, 0))],
        out_specs=pl.BlockSpec((tm, d), lambda i: (i, 0)),
        out_shape=jax.ShapeDtypeStruct((t, d), F32),
        compiler_params=_cparams(("parallel",)),
    )(dxr, dz, w, after)


def _in_proj_bwd_dw(x2, dz):
    t, d = x2.shape
    n = dz.shape[1]
    tm = min(ROW_TILE, t)
    nc = 768

    def body(x_ref, dz_ref, dw_ref, done_ref):
        @pl.when(pl.program_id(0) == 0)
        def _():
            dw_ref[...] = jnp.zeros_like(dw_ref)
            done_ref[...] = jnp.zeros_like(done_ref)
        xb = _mm(x_ref[...])
        for j in range(n // nc):
            cs = slice(j * nc, (j + 1) * nc)
            dw_ref[:, cs] += _dot_tn(xb, dz_ref[:, cs])

    return pl.pallas_call(
        body, name="in_proj_bwd_dw", grid=(t // tm,),
        in_specs=[pl.BlockSpec((tm, d), lambda i: (i, 0)), pl.BlockSpec((tm, n), lambda i: (i, 0))],
        out_specs=[pl.BlockSpec((d, n), lambda i: (0, 0)), pl.BlockSpec((8, 128), lambda i: (0, 0))],
        out_shape=[jax.ShapeDtypeStruct((d, n), F32), jax.ShapeDtypeStruct((8, 128), F32)],
        compiler_params=_cparams(("arbitrary",)),
    )(x2, dz)


def _halo_specs(ts, s_len, cols, left=True):
    per = ts // HALO
    last = s_len // HALO - 1
    if left:
        return pl.BlockSpec((1, HALO, cols), lambda b, s: (b, jnp.maximum(s * per - 1, 0), 0))
    return pl.BlockSpec((1, HALO, cols), lambda b, s: (b, jnp.minimum((s + 1) * per, last), 0))


def _build_shifts(src_ref, sh_ref, rows):
    for r in range(1, 8):
        sh_ref[r - 1, 0:rows, :] = src_ref[r:r + rows, :]


def _shifted(src_ref, sh_ref, r, start, n):
    if r == 0:
        return src_ref[pl.ds(start, n), :]
    return sh_ref[r - 1, pl.ds(start, n), :]


def _tril_masks():
    ri = lax.broadcasted_iota(jnp.int32, (GBLK, GBLK), 0)
    ci = lax.broadcasted_iota(jnp.int32, (GBLK, GBLK), 1)
    return ri >= ci, ci >= ri


def _spatial(w_ref, keep, vb):
    lane = lax.broadcasted_iota(jnp.int32, (GBLK, GBLK), 1)
    outs = []
    for p in range(4):
        xs = vb[:, p * GBLK:(p + 1) * GBLK]
        r0 = jnp.dot(_mm(jnp.where(keep, w_ref[2 * p], 0.0)), xs, preferred_element_type=F32)
        r1 = jnp.dot(_mm(jnp.where(keep, w_ref[2 * p + 1], 0.0)), xs, preferred_element_type=F32)
        outs.append(jnp.where(lane < 64, r0, r1))
    return jnp.concatenate(outs, axis=1)


def _even_fwd(z3, cw, vec, ws, bsf):
    bsz, s_len, _ = z3.shape
    ts = min(SEQ_TILE, s_len)
    ext_rows = ts + HALO

    def body(z_ref, zl_ref, cw_ref, vec_ref, ws_ref, bsf_ref, y_ref, ext_ref, sh_ref, a1_ref):
        s = pl.program_id(1)
        hl = zl_ref[0]
        a0h = hl[:, 0:W_BR] * _sigmoid(hl[:, W_BR:2 * W_BR])
        ext_ref[0:HALO, :] = jnp.where(s > 0, a0h, 0.0)
        for a, b in _row_chunks(0, ts):
            ext_ref[HALO + a:HALO + b, :] = z_ref[0, a:b, 0:W_BR] * _sigmoid(z_ref[0, a:b, W_BR:2 * W_BR])
        ext_ref[ext_rows:ext_rows + 8, :] = jnp.zeros((8, W_BR), F32)
        _build_shifts(ext_ref, sh_ref, ext_rows)

        def conv_chunk(ci, carry):
            base = pl.multiple_of(ci * CONV_CHUNK, CONV_CHUNK)
            acc = jnp.zeros((CONV_CHUNK, W_BR), F32) + vec_ref[0:1, :]
            for k in range(KA):
                q, r = divmod(2 + k, 8)
                acc = acc + _shifted(ext_ref, sh_ref, r, base + 8 * q, CONV_CHUNK) * cw_ref[k:k + 1, :]
            a1_ref[pl.ds(base, CONV_CHUNK), :] = acc
            return carry

        lax.fori_loop(0, ts // CONV_CHUNK, conv_chunk, 0)
        keep, _ = _tril_masks()

        def block(bi, carry):
            rows = pl.ds(pl.multiple_of(bi * GBLK, GBLK), GBLK)
            xh, _ = _ln_stats(a1_ref[rows, :])
            a = _silu(xh * vec_ref[1:2, :] + vec_ref[2:3, :]) * _silu(z_ref[0, rows, 2 * W_BR:3 * W_BR])
            y_ref[0, rows, 0:W_BR] = a.astype(y_ref.dtype)
            ua = _gelu(z_ref[0, rows, 3 * W_BR:4 * W_BR])
            vh, _ = _ln_stats(_gelu(z_ref[0, rows, 4 * W_BR:5 * W_BR]))
            vl = vh * vec_ref[3:4, :] + vec_ref[4:5, :]
            sg = _spatial(ws_ref, keep, _mm(vl)) + bsf_ref[...]
            g = ua * sg * _silu(z_ref[0, rows, 5 * W_BR:6 * W_BR])
            y_ref[0, rows, W_BR:2 * W_BR] = g.astype(y_ref.dtype)
            return carry

        lax.fori_loop(0, ts // GBLK, block, 0)

    full = lambda shape: pl.BlockSpec(shape, lambda b, s: (0,) * len(shape))
    return pl.pallas_call(
        body, name="even_fwd", grid=(bsz, s_len // ts),
        in_specs=[pl.BlockSpec((1, ts, N_COLS), lambda b, s: (b, s, 0)), _halo_specs(ts, s_len, 2 * W_BR),
                  full((32, W_BR)), full((8, W_BR)), full((8, GBLK, GBLK)), full((GBLK, W_BR))],
        out_specs=pl.BlockSpec((1, ts, 2 * W_BR), lambda b, s: (b, s, 0)),
        out_shape=jax.ShapeDtypeStruct((bsz, s_len, 2 * W_BR), MM_DTYPE),
        scratch_shapes=[pltpu.VMEM((ext_rows + 8, W_BR), F32), pltpu.VMEM((7, ext_rows, W_BR), F32),
                        pltpu.VMEM((ts, W_BR), F32)],
        compiler_params=_cparams(("parallel", "parallel")),
    )(z3, z3, cw, vec, ws, bsf)


def _even_bwd(z3, dy3, cw, vec, ws, wst, bsf):
    bsz, s_len, _ = z3.shape
    ts = min(SEQ_TILE, s_len)
    n_s = s_len // ts
    ext_rows = ts + 2 * HALO
    a_rows = ts + HALO

    def body(z_ref, zl_ref, zr_ref, dy_ref, dyr_ref, cw_ref, vec_ref, ws_ref, wst_ref, bsf_ref,
             dz_ref, dbin_ref, dcw_ref, dvec_ref, dws_ref, dbsf_ref,
             ext_ref, sh_ref, a1_ref, ag_ref, dya_ref, d_ref, accw_ref):
        b = pl.program_id(0)
        s = pl.program_id(1)

        @pl.when((b == 0) & (s == 0))
        def _():
            dbin_ref[...] = jnp.zeros_like(dbin_ref)
            dcw_ref[...] = jnp.zeros_like(dcw_ref)
            dvec_ref[...] = jnp.zeros_like(dvec_ref)
            dws_ref[...] = jnp.zeros_like(dws_ref)
            dbsf_ref[...] = jnp.zeros_like(dbsf_ref)

        has_right = s < n_s - 1
        hl = zl_ref[0]
        hr = zr_ref[0]
        ext_ref[0:HALO, :] = jnp.where(s > 0, hl[:, 0:W_BR] * _sigmoid(hl[:, W_BR:2 * W_BR]), 0.0)
        for a, b in _row_chunks(0, ts):
            ext_ref[HALO + a:HALO + b, :] = z_ref[0, a:b, 0:W_BR] * _sigmoid(z_ref[0, a:b, W_BR:2 * W_BR])
            ag_ref[a:b, :] = z_ref[0, a:b, 2 * W_BR:3 * W_BR]
            dya_ref[a:b, :] = dy_ref[0, a:b, 0:W_BR]
        ext_ref[HALO + ts:ext_rows, :] = hr[:, 0:W_BR] * _sigmoid(hr[:, W_BR:2 * W_BR])
        ext_ref[ext_rows:ext_rows + 8, :] = jnp.zeros((8, W_BR), F32)
        ag_ref[ts:a_rows, :] = hr[:, 2 * W_BR:3 * W_BR]
        dya_ref[ts:a_rows, :] = dyr_ref[0][:, 0:W_BR]
        _build_shifts(ext_ref, sh_ref, ext_rows)

        def conv_chunk(ci, carry):
            base = pl.multiple_of(ci * CONV_CHUNK, CONV_CHUNK)
            acc = jnp.zeros((CONV_CHUNK, W_BR), F32) + vec_ref[0:1, :]
            for k in range(KA):
                q, r = divmod(2 + k, 8)
                acc = acc + _shifted(ext_ref, sh_ref, r, base + 8 * q, CONV_CHUNK) * cw_ref[k:k + 1, :]
            a1_ref[pl.ds(base, CONV_CHUNK), :] = acc
            return carry

        lax.fori_loop(0, a_rows // CONV_CHUNK, conv_chunk, 0)

        def a_chunk(base, n, main):
            rows = pl.ds(base, n)
            xh, rs = _ln_stats(a1_ref[rows, :])
            ln = xh * vec_ref[1:2, :] + vec_ref[2:3, :]
            sl, dsl = _silu_grad(ln)
            sgt, dsgt = _silu_grad(ag_ref[rows, :])
            dya = dya_ref[rows, :]
            dln = dya * sgt * dsl
            da1 = _ln_bwd(dln * vec_ref[1:2, :], xh, rs)
            if main:
                d_ref[rows, :] = da1
                dag = dya * sl * dsgt
                dz_ref[0, rows, 2 * W_BR:3 * W_BR] = dag.astype(dz_ref.dtype)
                dbin_ref[0:1, 2 * W_BR:3 * W_BR] += _rowsum(dag)
                dvec_ref[0:1, :] += _rowsum(da1)
                dvec_ref[1:2, :] += _rowsum(dln * xh)
                dvec_ref[2:3, :] += _rowsum(dln)
            else:
                d_ref[rows, :] = jnp.where(has_right, da1, 0.0)

        def a_main(ci, carry):
            a_chunk(pl.multiple_of(ci * GBLK, GBLK), GBLK, True)
            return carry

        lax.fori_loop(0, ts // GBLK, a_main, 0)
        a_chunk(ts, HALO, False)
        d_ref[a_rows:a_rows + 8, :] = jnp.zeros((8, W_BR), F32)

        accw_ref[...] = jnp.zeros_like(accw_ref)

        def dw_chunk(ci, carry):
            base = pl.multiple_of(ci * CONV_CHUNK, CONV_CHUNK)
            d = d_ref[pl.ds(base, CONV_CHUNK), :]
            for k in range(KA):
                q, r = divmod(2 + k, 8)
                prod = d * _shifted(ext_ref, sh_ref, r, base + 8 * q, CONV_CHUNK)
                accw_ref[k] += jnp.sum(prod.reshape(CONV_CHUNK // 8, 8, W_BR), axis=0)
            return carry

        lax.fori_loop(0, ts // CONV_CHUNK, dw_chunk, 0)
        dcw_ref[...] += jnp.sum(accw_ref[...], axis=1)

        _build_shifts(d_ref, sh_ref, a_rows)

        def dx_chunk(ci, carry):
            base = pl.multiple_of(ci * CONV_CHUNK, CONV_CHUNK)
            rows = pl.ds(base, CONV_CHUNK)
            acc = jnp.zeros((CONV_CHUNK, W_BR), F32)
            for m in range(KA):
                q, r = divmod(m, 8)
                acc = acc + _shifted(d_ref, sh_ref, r, base + 8 * q, CONV_CHUNK) * cw_ref[KA - 1 - m:KA - m, :]
            aval = z_ref[0, rows, 0:W_BR]
            sg = _sigmoid(z_ref[0, rows, W_BR:2 * W_BR])
            dval = acc * sg
            dglu = acc * aval * sg * (1.0 - sg)
            dz_ref[0, rows, 0:W_BR] = dval.astype(dz_ref.dtype)
            dz_ref[0, rows, W_BR:2 * W_BR] = dglu.astype(dz_ref.dtype)
            dbin_ref[0:1, 0:W_BR] += _rowsum(dval)
            dbin_ref[0:1, W_BR:2 * W_BR] += _rowsum(dglu)
            return carry

        lax.fori_loop(0, ts // CONV_CHUNK, dx_chunk, 0)

        keep, keep_t = _tril_masks()
        lane = lax.broadcasted_iota(jnp.int32, (GBLK, GBLK), 1)

        def block(bi, carry):
            rows = pl.ds(pl.multiple_of(bi * GBLK, GBLK), GBLK)
            ua, dua = _gelu_grad(z_ref[0, rows, 3 * W_BR:4 * W_BR])
            va, dva = _gelu_grad(z_ref[0, rows, 4 * W_BR:5 * W_BR])
            sgt, dsgt = _silu_grad(z_ref[0, rows, 5 * W_BR:6 * W_BR])
            vh, rs = _ln_stats(va)
            vlb = _mm(vh * vec_ref[3:4, :] + vec_ref[4:5, :])
            sg = _spatial(ws_ref, keep, vlb) + bsf_ref[...]
            dyg = dy_ref[0, rows, W_BR:2 * W_BR]
            du = dyg * sg * sgt * dua
            dsg = dyg * ua * sgt
            dgg = dyg * ua * sg * dsgt
            dvl = _spatial(wst_ref, keep_t, _mm(dsg))
            for p in range(4):
                dsp = dsg[:, p * GBLK:(p + 1) * GBLK]
                vlp = vlb[:, p * GBLK:(p + 1) * GBLK]
                dws_ref[2 * p] += jnp.where(keep, _dot_nt(jnp.where(lane < 64, dsp, 0.0), vlp), 0.0)
                dws_ref[2 * p + 1] += jnp.where(keep, _dot_nt(jnp.where(lane >= 64, dsp, 0.0), vlp), 0.0)
            dbsf_ref[...] += dsg
            dvec_ref[3:4, :] += _rowsum(dvl * vh)
            dvec_ref[4:5, :] += _rowsum(dvl)
            dv = _ln_bwd(dvl * vec_ref[3:4, :], vh, rs) * dva
            dz_ref[0, rows, 3 * W_BR:4 * W_BR] = du.astype(dz_ref.dtype)
            dz_ref[0, rows, 4 * W_BR:5 * W_BR] = dv.astype(dz_ref.dtype)
            dz_ref[0, rows, 5 * W_BR:6 * W_BR] = dgg.astype(dz_ref.dtype)
            dbin_ref[0:1, 3 * W_BR:4 * W_BR] += _rowsum(du)
            dbin_ref[0:1, 4 * W_BR:5 * W_BR] += _rowsum(dv)
            dbin_ref[0:1, 5 * W_BR:6 * W_BR] += _rowsum(dgg)
            return carry

        lax.fori_loop(0, ts // GBLK, block, 0)

    full = lambda shape: pl.BlockSpec(shape, lambda b, s: (0,) * len(shape))
    acc_shapes = [(1, N_COLS), (32, W_BR), (8, W_BR), (8, GBLK, GBLK), (GBLK, W_BR)]
    return pl.pallas_call(
        body, name="even_bwd", grid=(bsz, n_s),
        in_specs=[pl.BlockSpec((1, ts, N_COLS), lambda b, s: (b, s, 0)),
                  _halo_specs(ts, s_len, N_COLS, True), _halo_specs(ts, s_len, N_COLS, False),
                  pl.BlockSpec((1, ts, 2 * W_BR), lambda b, s: (b, s, 0)), _halo_specs(ts, s_len, 2 * W_BR, False),
                  full((32, W_BR)), full((8, W_BR)), full((8, GBLK, GBLK)), full((8, GBLK, GBLK)), full((GBLK, W_BR))],
        out_specs=[pl.BlockSpec((1, ts, N_COLS), lambda b, s: (b, s, 0))] + [full(sh) for sh in acc_shapes],
        out_shape=[jax.ShapeDtypeStruct((bsz, s_len, N_COLS), MM_DTYPE)] + [jax.ShapeDtypeStruct(sh, F32) for sh in acc_shapes],
        scratch_shapes=[pltpu.VMEM((ext_rows + 8, W_BR), F32), pltpu.VMEM((7, ext_rows, W_BR), F32),
                        pltpu.VMEM((a_rows, W_BR), F32), pltpu.VMEM((a_rows, W_BR), F32), pltpu.VMEM((a_rows, W_BR), F32),
                        pltpu.VMEM((a_rows + 8, W_BR), F32), pltpu.VMEM((32, 8, W_BR), F32)],
        compiler_params=_cparams(("arbitrary", "arbitrary")),
    )(z3, z3, z3, dy3, dy3, cw, vec, ws, wst, bsf)


def _row_chunks(lo, hi):
    return [(a, min(a + ROW_CHUNK, hi)) for a in range(lo, hi, ROW_CHUNK)]


def _pool_stages(e_refs, rows):
    e0, e1, e2, e3, e4 = e_refs
    for a, b in _row_chunks(8, rows):
        e1[a:b, :] = e0[a:b, :] + e0[a - 1:b - 1, :]
    for a, b in _row_chunks(16, rows):
        e2[a:b, GBLK:] = e1[a:b, GBLK:] + e1[a - 2:b - 2, GBLK:]
    for a, b in _row_chunks(24, rows):
        e3[a:b, 2 * GBLK:] = e2[a:b, 2 * GBLK:] + e2[a - 4:b - 4, 2 * GBLK:]
    for a, b in _row_chunks(32, rows):
        e4[a:b, 3 * GBLK:] = e3[a:b, 3 * GBLK:] + e3[a - 8:b - 8, 3 * GBLK:]


def _pool_counts(start, n):
    pos = (start + 1 + lax.broadcasted_iota(jnp.int32, (n, 1), 0)).astype(F32)
    return [jnp.minimum(pos, float(w)) for w in POOL_WINDOWS]


def _pooled_into(e_refs, pooled_ref, s, ts):
    for a, b in _row_chunks(0, ts):
        cnt = _pool_counts(s * ts + a, b - a)
        for g in range(4):
            cs = slice(g * GBLK, (g + 1) * GBLK)
            pooled_ref[a:b, cs] = e_refs[g + 1][HALO + a:HALO + b, cs] / cnt[g] - e_refs[0][HALO + a:HALO + b, cs]


def _odd_prologue(z_ref, hl, s, ts, e_refs, pooled_ref, dext_ref, ec_ref, vec_ref):
    e0 = e_refs[0]
    e0[0:HALO, :] = jnp.where(s > 0, hl[:, 0:W_BR], 0.0)
    dext_ref[0:HALO, :] = jnp.where(s > 0, hl[:, 2 * W_BR:3 * W_BR] * hl[:, 4 * W_BR:5 * W_BR], 0.0)
    for a, b in _row_chunks(0, ts):
        e0[HALO + a:HALO + b, :] = z_ref[0, a:b, 0:W_BR]
        dext_ref[HALO + a:HALO + b, :] = z_ref[0, a:b, 2 * W_BR:3 * W_BR] * z_ref[0, a:b, 4 * W_BR:5 * W_BR]
    _pool_stages(e_refs, ts + HALO)
    _pooled_into(e_refs, pooled_ref, s, ts)
    for a, b in _row_chunks(0, ts):
        ec_ref[a:b, :] = (vec_ref[1:2, :] * dext_ref[HALO - 2 + a:HALO - 2 + b, :] + vec_ref[2:3, :] * dext_ref[HALO - 1 + a:HALO - 1 + b, :]
                          + vec_ref[3:4, :] * dext_ref[HALO + a:HALO + b, :])


def _odd_fwd(z3, wp, vec):
    bsz, s_len, _ = z3.shape
    ts = min(SEQ_TILE, s_len)
    ext_rows = ts + HALO

    def body(z_ref, zl_ref, wp_ref, vec_ref, y_ref, e0, e1, e2, e3, e4, pooled_ref, dext_ref, ec_ref):
        s = pl.program_id(1)
        _odd_prologue(z_ref, zl_ref[0], s, ts, (e0, e1, e2, e3, e4), pooled_ref, dext_ref, ec_ref, vec_ref)

        def block(bi, carry):
            rows = pl.ds(pl.multiple_of(bi * GBLK, GBLK), GBLK)
            pb = _mm(pooled_ref[rows, :])
            cpre = jnp.concatenate([jnp.dot(pb[:, g * GBLK:(g + 1) * GBLK], wp_ref[g], preferred_element_type=F32)
                                    for g in range(4)], axis=1)
            c = cpre * vec_ref[0:1, :] * _silu(z_ref[0, rows, W_BR:2 * W_BR])
            d = z_ref[0, rows, 3 * W_BR:4 * W_BR] * ec_ref[rows, :] * _silu(z_ref[0, rows, 5 * W_BR:6 * W_BR])
            y_ref[0, rows, 0:W_BR] = c.astype(y_ref.dtype)
            y_ref[0, rows, W_BR:2 * W_BR] = d.astype(y_ref.dtype)
            return carry

        lax.fori_loop(0, ts // GBLK, block, 0)

    full = lambda shape: pl.BlockSpec(shape, lambda b, s: (0,) * len(shape))
    ebuf = pltpu.VMEM((ext_rows, W_BR), F32)
    return pl.pallas_call(
        body, name="odd_fwd", grid=(bsz, s_len // ts),
        in_specs=[pl.BlockSpec((1, ts, N_COLS), lambda b, s: (b, s, 0)), _halo_specs(ts, s_len, N_COLS),
                  full((4, GBLK, GBLK)), full((8, W_BR))],
        out_specs=pl.BlockSpec((1, ts, 2 * W_BR), lambda b, s: (b, s, 0)),
        out_shape=jax.ShapeDtypeStruct((bsz, s_len, 2 * W_BR), MM_DTYPE),
        scratch_shapes=[ebuf, ebuf, ebuf, ebuf, ebuf, pltpu.VMEM((ts, W_BR), F32), ebuf, pltpu.VMEM((ts, W_BR), F32)],
        compiler_params=_cparams(("parallel", "parallel")),
    )(z3, z3, wp, vec)


def _odd_bwd(z3, dy3, wp, vec):
    bsz, s_len, _ = z3.shape
    ts = min(SEQ_TILE, s_len)
    n_s = s_len // ts
    ext_rows = ts + HALO

    def body(z_ref, zl_ref, zr_ref, dy_ref, dyr_ref, wp_ref, vec_ref,
             dz_ref, dbin_ref, dwp_ref, dvec_ref,
             e0, e1, e2, e3, e4, pooled_ref, dext_ref, ec_ref, q_ref, dp_ref, de_ref, f1, f2, f3, f4):
        b = pl.program_id(0)
        s = pl.program_id(1)

        @pl.when((b == 0) & (s == 0))
        def _():
            dbin_ref[...] = jnp.zeros_like(dbin_ref)
            dwp_ref[...] = jnp.zeros_like(dwp_ref)
            dvec_ref[...] = jnp.zeros_like(dvec_ref)

        has_right = s < n_s - 1
        _odd_prologue(z_ref, zl_ref[0], s, ts, (e0, e1, e2, e3, e4), pooled_ref, dext_ref, ec_ref, vec_ref)

        def grads(zc_gate, zd_b, zd_gate, dyc, dyd, rows_out, n, start, valid):
            sgt = _silu(zc_gate)
            dcpre = dyc * vec_ref[0:1, :] * sgt
            db = _mm(dcpre)
            dpool = jnp.concatenate([_dot_nt(db[:, g * GBLK:(g + 1) * GBLK], wp_ref[g]) for g in range(4)], axis=1)
            cnt = _pool_counts(start, n)
            q = jnp.concatenate([dpool[:, g * GBLK:(g + 1) * GBLK] / cnt[g] for g in range(4)], axis=1)
            de = dyd * zd_b * _silu(zd_gate)
            if valid is not None:
                q = jnp.where(valid, q, 0.0)
                de = jnp.where(valid, de, 0.0)
            q_ref[rows_out, :] = q
            dp_ref[rows_out, :] = dpool
            de_ref[rows_out, :] = de
            return dcpre

        def block(bi, carry):
            base = pl.multiple_of(bi * GBLK, GBLK)
            rows = pl.ds(base, GBLK)
            cg = z_ref[0, rows, W_BR:2 * W_BR]
            dyc = dy_ref[0, rows, 0:W_BR]
            dyd = dy_ref[0, rows, W_BR:2 * W_BR]
            d_b = z_ref[0, rows, 3 * W_BR:4 * W_BR]
            d_gate = z_ref[0, rows, 5 * W_BR:6 * W_BR]
            dcpre = grads(cg, d_b, d_gate, dyc, dyd, rows, GBLK, s * ts + base, None)
            pb = _mm(pooled_ref[rows, :])
            dcb = _mm(dcpre)
            cpre = jnp.concatenate([jnp.dot(pb[:, g * GBLK:(g + 1) * GBLK], wp_ref[g], preferred_element_type=F32)
                                    for g in range(4)], axis=1)
            for g in range(4):
                cs = slice(g * GBLK, (g + 1) * GBLK)
                dwp_ref[g] += _dot_tn(pb[:, cs], dcb[:, cs])
            sgt, dsgt = _silu_grad(cg)
            dvec_ref[0:1, :] += _rowsum(dyc * cpre * sgt)
            dcg = dyc * cpre * vec_ref[0:1, :] * dsgt
            sdt, dsdt = _silu_grad(d_gate)
            ec = ec_ref[rows, :]
            ddb = dyd * ec * sdt
            ddg = dyd * d_b * ec * dsdt
            dz_ref[0, rows, W_BR:2 * W_BR] = dcg.astype(dz_ref.dtype)
            dz_ref[0, rows, 3 * W_BR:4 * W_BR] = ddb.astype(dz_ref.dtype)
            dz_ref[0, rows, 5 * W_BR:6 * W_BR] = ddg.astype(dz_ref.dtype)
            dbin_ref[0:1, W_BR:2 * W_BR] += _rowsum(dcg)
            dbin_ref[0:1, 3 * W_BR:4 * W_BR] += _rowsum(ddb)
            dbin_ref[0:1, 5 * W_BR:6 * W_BR] += _rowsum(ddg)
            return carry

        lax.fori_loop(0, ts // GBLK, block, 0)
        hr = zr_ref[0]
        dyr = dyr_ref[0]
        grads(hr[:, W_BR:2 * W_BR], hr[:, 3 * W_BR:4 * W_BR], hr[:, 5 * W_BR:6 * W_BR], dyr[:, 0:W_BR], dyr[:, W_BR:2 * W_BR],
              slice(ts, ext_rows), HALO, (s + 1) * ts, has_right)

        for a, b in _row_chunks(0, ts + 24):
            f1[a:b, :] = q_ref[a:b, :] + q_ref[a + 1:b + 1, :]
        for a, b in _row_chunks(0, ts + 16):
            f2[a:b, GBLK:] = f1[a:b, GBLK:] + f1[a + 2:b + 2, GBLK:]
        for a, b in _row_chunks(0, ts + 8):
            f3[a:b, 2 * GBLK:] = f2[a:b, 2 * GBLK:] + f2[a + 4:b + 4, 2 * GBLK:]
        for a, b in _row_chunks(0, ts):
            f4[a:b, 3 * GBLK:] = f3[a:b, 3 * GBLK:] + f3[a + 8:b + 8, 3 * GBLK:]

        for a, b in _row_chunks(0, ts):
            for g, f in enumerate((f1, f2, f3, f4)):
                cs = slice(g * GBLK, (g + 1) * GBLK)
                dvg = f[a:b, cs] - dp_ref[a:b, cs]
                dz_ref[0, a:b, cs] = dvg.astype(dz_ref.dtype)
                dbin_ref[0:1, cs] += _rowsum(dvg)
            de = de_ref[a:b, :]
            ddc = vec_ref[1:2, :] * de_ref[a + 2:b + 2, :] + vec_ref[2:3, :] * de_ref[a + 1:b + 1, :] + vec_ref[3:4, :] * de
            ddh = ddc * z_ref[0, a:b, 4 * W_BR:5 * W_BR]
            ddcc = ddc * z_ref[0, a:b, 2 * W_BR:3 * W_BR]
            dz_ref[0, a:b, 2 * W_BR:3 * W_BR] = ddh.astype(dz_ref.dtype)
            dz_ref[0, a:b, 4 * W_BR:5 * W_BR] = ddcc.astype(dz_ref.dtype)
            dbin_ref[0:1, 2 * W_BR:3 * W_BR] += _rowsum(ddh)
            dbin_ref[0:1, 4 * W_BR:5 * W_BR] += _rowsum(ddcc)
            for k in range(3):
                dvec_ref[1 + k:2 + k, :] += _rowsum(de * dext_ref[HALO - 2 + k + a:HALO - 2 + k + b, :])

    full = lambda shape: pl.BlockSpec(shape, lambda b, s: (0,) * len(shape))
    acc_shapes = [(1, N_COLS), (4, GBLK, GBLK), (8, W_BR)]
    ebuf = pltpu.VMEM((ext_rows, W_BR), F32)
    tbuf = pltpu.VMEM((ts, W_BR), F32)
    return pl.pallas_call(
        body, name="odd_bwd", grid=(bsz, n_s),
        in_specs=[pl.BlockSpec((1, ts, N_COLS), lambda b, s: (b, s, 0)),
                  _halo_specs(ts, s_len, N_COLS, True), _halo_specs(ts, s_len, N_COLS, False),
                  pl.BlockSpec((1, ts, 2 * W_BR), lambda b, s: (b, s, 0)), _halo_specs(ts, s_len, 2 * W_BR, False),
                  full((4, GBLK, GBLK)), full((8, W_BR))],
        out_specs=[pl.BlockSpec((1, ts, N_COLS), lambda b, s: (b, s, 0))] + [full(sh) for sh in acc_shapes],
        out_shape=[jax.ShapeDtypeStruct((bsz, s_len, N_COLS), MM_DTYPE)] + [jax.ShapeDtypeStruct(sh, F32) for sh in acc_shapes],
        scratch_shapes=[ebuf, ebuf, ebuf, ebuf, ebuf, tbuf, ebuf, tbuf, ebuf, ebuf, ebuf, ebuf, ebuf, ebuf, tbuf],
        compiler_params=_cparams(("arbitrary", "arbitrary")),
    )(z3, z3, z3, dy3, dy3, wp, vec)


def _post_fwd(y2, x2, p_layer, w_out, wg, wple, vec, tgt=None):
    p_all, layer = p_layer
    t, d = x2.shape
    tm = min(ROW_TILE, t)
    last = tgt is not None

    def body(*refs):
        y_ref, x_ref, p_ref, wo_ref, wg_ref, wp_ref, vec_ref = refs[:7]
        xn_ref, r_ref, gate_ref = refs[7 + last:10 + last]
        r = ALPHA * x_ref[...] + jnp.dot(y_ref[...], wo_ref[...], preferred_element_type=F32) + vec_ref[0:1, :]
        r_ref[...] = r
        xh, _ = _ln_stats(r)
        h = xh * vec_ref[1:2, :] + vec_ref[2:3, :]
        gate = _sigmoid(_dot(h, wg_ref[...]) + vec_ref[3:4, :])
        gate_ref[...] = gate
        xn = h + gate * _dot(p_ref[...], wp_ref[...])
        if last:
            sq_ref = refs[11]

            @pl.when(pl.program_id(0) == 0)
            def _():
                sq_ref[...] = jnp.zeros_like(sq_ref)
            e = xn - refs[7][...]
            xn_ref[...] = e / float(d)
            sq_ref[...] += _rowsum(e * e)
        else:
            xn_ref[...] = xn

    row = lambda c: pl.BlockSpec((tm, c), lambda i: (i, 0))
    full = lambda shape: pl.BlockSpec(shape, lambda i: (0,) * len(shape))
    return pl.pallas_call(
        body, name="post_fwd_loss" if last else "post_fwd", grid=(t // tm,),
        in_specs=[row(d), row(d), pl.BlockSpec((None, tm, D_PLE), lambda i: (layer, i, 0)),
                  full((d, d)), full((d, d)), full((D_PLE, d)), full((8, d))] + [row(d)] * last,
        out_specs=[row(d), row(d), row(d)] + [full((1, d))] * last,
        out_shape=[jax.ShapeDtypeStruct((t, d), F32)] * 3 + [jax.ShapeDtypeStruct((1, d), F32)] * last,
        compiler_params=_cparams(("arbitrary",) if last else ("parallel",)),
    )(y2, x2, p_all, w_out, wg, wple, vec, *([tgt] if last else []))


def _post_bwd(dxn, r2, gate2, p_layer, y2, w_out, wg, wple, vec):
    p_all, layer = p_layer
    t, d = r2.shape
    tm = min(BWD_ROW_TILE, t)

    def body(dxn_ref, r_ref, gate_ref, p_ref, y_ref, wo_ref, wg_ref, wp_ref, vec_ref,
             dxr_ref, dy_ref, dwo_ref, dwg_ref, dwp_ref, dvec_ref):
        @pl.when(pl.program_id(0) == 0)
        def _():
            dwo_ref[...] = jnp.zeros_like(dwo_ref)
            dwg_ref[...] = jnp.zeros_like(dwg_ref)
            dwp_ref[...] = jnp.zeros_like(dwp_ref)
            dvec_ref[...] = jnp.zeros_like(dvec_ref)

        dxn = dxn_ref[...]
        gate = gate_ref[...]
        xh, rs = _ln_stats(r_ref[...])
        hb = _mm(xh * vec_ref[1:2, :] + vec_ref[2:3, :])
        pb = _mm(p_ref[...])
        pe = jnp.dot(pb, wp_ref[...], preferred_element_type=F32)
        dpre = dxn * pe * gate * (1.0 - gate)
        dpb = _mm(dpre)
        dh = dxn + _dot_nt(dpb, wg_ref[...])
        dwg_ref[...] += _dot_tn(hb, dpb)
        dwp_ref[...] += _dot_tn(pb, dxn * gate)
        dr = _ln_bwd(dh * vec_ref[1:2, :], xh, rs)
        drb = _mm(dr)
        dxr_ref[...] = ALPHA * dr
        dy_ref[...] = _dot_nt(drb, wo_ref[...])
        dwo_ref[...] += _dot_tn(y_ref[...], drb)
        dvec_ref[0:1, :] += _rowsum(dr)
        dvec_ref[1:2, :] += _rowsum(dh * xh)
        dvec_ref[2:3, :] += _rowsum(dh)
        dvec_ref[3:4, :] += _rowsum(dpre)

    row = lambda c: pl.BlockSpec((tm, c), lambda i: (i, 0))
    full = lambda shape: pl.BlockSpec(shape, lambda i: (0,) * len(shape), pipeline_mode=pl.Buffered(1))
    acc_shapes = [(d, d), (d, d), (D_PLE, d), (8, d)]
    return pl.pallas_call(
        body, name="post_bwd", grid=(t // tm,),
        in_specs=[row(d), row(d), row(d), pl.BlockSpec((None, tm, D_PLE), lambda i: (layer, i, 0)), row(d),
                  full((d, d)), full((d, d)), full((D_PLE, d)), full((8, d))],
        out_specs=[row(d), row(d)] + [full(sh) for sh in acc_shapes],
        out_shape=[jax.ShapeDtypeStruct((t, d), F32)] * 2 + [jax.ShapeDtypeStruct(sh, F32) for sh in acc_shapes],
        compiler_params=_cparams(("arbitrary",)),
    )(dxn, r2, gate2, p_all, y2, w_out, wg, wple, vec)


def _place():
    x, y, c = lax.axis_index("x"), lax.axis_index("y"), lax.axis_index("c")
    chips = [(1 - x, y), (x, 1 - y), (1 - x, 1 - y)]
    return x, y, c, chips


def _shard_of(ref, ax, k, width, lo=None, ln=None):
    idx = [slice(None)] * 3
    if lo is not None:
        idx[0] = pl.ds(lo, ln)
    idx[ax] = pl.ds(k * width, width)
    return ref.at[tuple(idx)]


def _remote(src, dst, ssem, rsem, dev):
    return pltpu.make_async_remote_copy(src_ref=src, dst_ref=dst, send_sem=ssem, recv_sem=rsem, device_id=dev, device_id_type=MESH)


def _place_shard(w, ax, chip, dtype=MM_DTYPE, layer=None):
    l_dim, a_dim, b_dim = w.shape
    tr = min(256, a_dim)
    per = a_dim // tr
    first = 0
    if layer is not None:
        l_dim, first = 1, layer
    shape = [l_dim, a_dim, b_dim]
    shape[ax] *= 4
    if ax == 2:
        out_spec = pl.BlockSpec((1, tr, b_dim), lambda l, i, k: (l, i, k[0]))
    else:
        out_spec = pl.BlockSpec((1, tr, b_dim), lambda l, i, k: (l, k[0] * per + i, 0))

    def body(k_ref, w_ref, o_ref):
        o_ref[...] = w_ref[...].astype(o_ref.dtype)

    return pl.pallas_call(
        body, name="place_shard",
        grid_spec=pltpu.PrefetchScalarGridSpec(
            num_scalar_prefetch=1, grid=(l_dim, per),
            in_specs=[pl.BlockSpec((1, tr, b_dim), lambda l, i, k: (first + l, i, 0))], out_specs=out_spec),
        out_shape=jax.ShapeDtypeStruct(tuple(shape), dtype),
        compiler_params=_cparams(("parallel", "parallel")),
    )(chip, w)


def _shard_copies(src_refs, dst_refs, axes, sems):
    x, y, c, chips = _place()
    j = 2 * x + y
    cps = []
    for a, (s_ref, d_ref) in enumerate(zip(src_refs, dst_refs)):
        w = d_ref.shape[axes[a]] // 4
        for q, (qx, qy) in enumerate(chips):
            ssem, rsem = sems[3 * a + q]
            cps.append(_remote(_shard_of(s_ref, axes[a], j, w), _shard_of(d_ref, axes[a], j, w), ssem, rsem, (qx, qy, c)))
    return cps


def _chip_handshake():
    x, y, c, chips = _place()
    barrier = pltpu.get_barrier_semaphore()
    for qx, qy in chips:
        pl.semaphore_signal(barrier, inc=1, device_id=(qx, qy, c), device_id_type=MESH)
    pl.semaphore_wait(barrier, 3)


def _gather_async(name, collective_id, fulls, axes):
    n = len(fulls)
    refs = [jax.new_ref(f, memory_space=pltpu.MemorySpace.HBM) for f in fulls]

    @pl.kernel(mesh=plsc.ScalarSubcoreMesh(axis_name="seq", num_cores=1), name=name,
               scratch_types=(pltpu.SemaphoreType.DMA,) * (6 * n),
               compiler_params=pltpu.CompilerParams(collective_id=collective_id))
    def launch(*sems):
        _chip_handshake()
        cps = _shard_copies(refs, refs, axes, [(sems[2 * k], sems[2 * k + 1]) for k in range(3 * n)])
        for cp in cps:
            cp.start()
        for cp in cps:
            cp.wait()

    launch()
    return refs


def _scatter_async(name, collective_id, grads, axes):
    n = len(grads)
    outs = []
    for g, ax in zip(grads, axes):
        sh = list(g.shape)
        sh[ax] //= 4
        outs.append(jax.ShapeDtypeStruct((3,) + tuple(sh), g.dtype))

    def body(*refs):
        srcs, lands, sems = refs[:n], refs[n:2 * n], refs[2 * n:]
        _chip_handshake()
        x, y, c, chips = _place()
        cps = []
        for a in range(n):
            w = srcs[a].shape[axes[a]] // 4
            for q, (qx, qy) in enumerate(chips):
                k = 3 * a + q
                cps.append(_remote(_shard_of(srcs[a], axes[a], 2 * qx + qy, w), lands[a].at[q], sems[2 * k], sems[2 * k + 1],
                                   (qx, qy, c)))
        for cp in cps:
            cp.start()
        for cp in cps:
            cp.wait()

    return pl.kernel(body, out_type=outs, mesh=plsc.ScalarSubcoreMesh(axis_name="seq", num_cores=1), name=name,
                     scratch_types=(pltpu.SemaphoreType.DMA,) * (6 * n),
                     compiler_params=pltpu.CompilerParams(collective_id=collective_id))(*grads)


def _pair_swap_async(name, collective_id, sums):
    n = len(sums)

    def body(*refs):
        g_refs, o_refs, sems = refs[:n], refs[n:2 * n], refs[2 * n:]
        x, y, c, _ = _place()
        barrier = pltpu.get_barrier_semaphore()
        pl.semaphore_signal(barrier, inc=1, device_id=(x, y, 1 - c), device_id_type=MESH)
        pl.semaphore_wait(barrier, 1)
        cps = [_remote(g_refs[a], o_refs[a], sems[2 * a], sems[2 * a + 1], (x, y, 1 - c)) for a in range(n)]
        for cp in cps:
            cp.start()
        for cp in cps:
            cp.wait()

    return pl.kernel(body, out_type=[jax.ShapeDtypeStruct(g.shape, g.dtype) for g in sums],
                     mesh=plsc.ScalarSubcoreMesh(axis_name="seq", num_cores=1), name=name,
                     scratch_types=(pltpu.SemaphoreType.DMA,) * (2 * n),
                     compiler_params=pltpu.CompilerParams(collective_id=collective_id))(*sums)


def _pair_exchange(grads, small):
    n = len(grads)
    outs = [jax.ShapeDtypeStruct((g.shape[0] // 2,) + g.shape[1:], g.dtype) for g in grads]
    outs.append(jax.ShapeDtypeStruct((small.shape[0] // 2, small.shape[1]), small.dtype))

    def body(*refs):
        g_refs, o_refs = refs[:n + 1], refs[n + 1:2 * n + 2]
        ssem, rsem = refs[2 * n + 2:]
        x, y, c, _ = _place()
        cps = []
        for a in range(n + 1):
            lh = g_refs[a].shape[0] // 2
            cp = _remote(g_refs[a].at[pl.ds((1 - c) * lh, lh)], o_refs[a], ssem.at[a], rsem.at[a], (x, y, 1 - c))
            cp.start()
            cps.append(cp)
        for cp in cps:
            cp.wait()

    return pl.pallas_call(
        body, name="pair_exchange", in_specs=[ANY] * (n + 1), out_specs=[ANY] * (n + 1), out_shape=outs,
        scratch_shapes=[pltpu.SemaphoreType.DMA((n + 1,)), pltpu.SemaphoreType.DMA((n + 1,))],
        compiler_params=pltpu.CompilerParams(has_side_effects=True),
    )(*grads, small)


def _chip_scatter(sums, axes, small):
    n = len(sums)
    outs = []
    for g, ax in zip(sums, axes):
        sh = list(g.shape)
        sh[ax] //= 4
        outs.append(jax.ShapeDtypeStruct((3,) + tuple(sh), g.dtype))
    rq = small.shape[0] // 4
    outs.append(jax.ShapeDtypeStruct((3, rq, small.shape[1]), small.dtype))

    def body(*refs):
        g_refs, o_refs = refs[:n + 1], refs[n + 1:2 * n + 2]
        ssem, rsem = refs[2 * n + 2:]
        x, y, c, chips = _place()
        cps = []
        for a in range(n + 1):
            for q, (qx, qy) in enumerate(chips):
                k = 2 * qx + qy
                if a < n:
                    src = _shard_of(g_refs[a], axes[a], k, g_refs[a].shape[axes[a]] // 4)
                else:
                    src = g_refs[a].at[pl.ds(k * rq, rq)]
                cp = _remote(src, o_refs[a].at[q], ssem.at[a * 3 + q], rsem.at[a * 3 + q], (qx, qy, c))
                cp.start()
                cps.append(cp)
        for cp in cps:
            cp.wait()

    return pl.pallas_call(
        body, name="chip_scatter", in_specs=[ANY] * (n + 1), out_specs=[ANY] * (n + 1), out_shape=outs,
        scratch_shapes=[pltpu.SemaphoreType.DMA((3 * n + 3,)), pltpu.SemaphoreType.DMA((3 * n + 3,))],
        compiler_params=pltpu.CompilerParams(has_side_effects=True),
    )(*sums, small)


def _final_exchange(reds, small):
    n = len(reds)
    flips = [(fx, fy, fc) for fx in (0, 1) for fy in (0, 1) for fc in (0, 1)][1:]

    def body(*refs):
        g_refs, o_refs = refs[:n + 1], refs[n + 1:2 * n + 2]
        ssem, rsem = refs[2 * n + 2:]
        x, y, c, _ = _place()
        cps = []
        for a in range(n):
            lh = g_refs[a].shape[0] // 2
            cp = _remote(g_refs[a].at[pl.ds(c * lh, lh)], o_refs[a].at[pl.ds(c * lh, lh)], ssem.at[a], rsem.at[a], (x, y, 1 - c))
            cp.start()
            cps.append(cp)
        mine = 4 * c + 2 * x + y
        for f, (fx, fy, fc) in enumerate(flips):
            cp = _remote(g_refs[n].at[mine], o_refs[n].at[mine], ssem.at[n + f], rsem.at[n + f], (x ^ fx, y ^ fy, c ^ fc))
            cp.start()
            cps.append(cp)
        for cp in cps:
            cp.wait()

    return pl.pallas_call(
        body, name="final_exchange", in_specs=[ANY] * (n + 1), out_specs=[ANY] * (n + 1),
        out_shape=[jax.ShapeDtypeStruct(g.shape, g.dtype) for g in reds] + [jax.ShapeDtypeStruct(small.shape, small.dtype)],
        input_output_aliases={a: a for a in range(n + 1)},
        scratch_shapes=[pltpu.SemaphoreType.DMA((n + 7,)), pltpu.SemaphoreType.DMA((n + 7,))],
        compiler_params=pltpu.CompilerParams(has_side_effects=True),
    )(*reds, small)


def _small_allreduce(small):
    r = small.shape[0]
    rh, rq = r // 2, r // 8
    flips = [(fx, fy, fc) for fx in (0, 1) for fy in (0, 1) for fc in (0, 1)][1:]

    def body(g_ref, out_ref, pair_ref, chip_ref, s1_ref, ssem, rsem):
        x, y, c, chips = _place()
        cp = _remote(g_ref.at[pl.ds((1 - c) * rh, rh)], pair_ref, ssem.at[0], rsem.at[0], (x, y, 1 - c))
        cp.start()
        cp.wait()
        s1_ref[...] = g_ref[pl.ds(pl.multiple_of(c * rh, 8), rh), :] + pair_ref[...]
        cps = [_remote(s1_ref.at[pl.ds((2 * qx + qy) * rq, rq)], chip_ref.at[q], ssem.at[1 + q], rsem.at[1 + q], (qx, qy, c))
               for q, (qx, qy) in enumerate(chips)]
        for cp in cps:
            cp.start()
        for cp in cps:
            cp.wait()
        mine = out_ref.at[pl.ds(pl.multiple_of((4 * c + 2 * x + y) * rq, 8), rq)]
        mine[...] = ((s1_ref[pl.ds(pl.multiple_of((2 * x + y) * rq, 8), rq), :] + chip_ref[0]) + chip_ref[1]) + chip_ref[2]
        cps = [_remote(mine, mine, ssem.at[4 + f], rsem.at[4 + f], (x ^ fx, y ^ fy, c ^ fc)) for f, (fx, fy, fc) in enumerate(flips)]
        for cp in cps:
            cp.start()
        for cp in cps:
            cp.wait()

    vm = pl.BlockSpec(memory_space=pltpu.VMEM)
    return pl.pallas_call(
        body, name="small_allreduce", in_specs=[vm], out_specs=vm, out_shape=jax.ShapeDtypeStruct(small.shape, F32),
        scratch_shapes=[pltpu.VMEM((rh, 128), F32), pltpu.VMEM((3, rq, 128), F32), pltpu.VMEM((rh, 128), F32),
                        pltpu.SemaphoreType.DMA((11,)), pltpu.SemaphoreType.DMA((11,))],
        compiler_params=pltpu.CompilerParams(has_side_effects=True, vmem_limit_bytes=VMEM_LIMIT),
    )(small)


def _to_wire(g):
    _, a_dim, b_dim = g.shape
    tr = min(256, a_dim)

    def body(g_ref, o_ref):
        o_ref[...] = g_ref[...].astype(o_ref.dtype)

    blk = pl.BlockSpec((1, tr, b_dim), lambda i: (0, i, 0))
    return pl.pallas_call(
        body, name="to_wire", grid=(a_dim // tr,), in_specs=[blk], out_specs=blk,
        out_shape=jax.ShapeDtypeStruct(g.shape, WIRE_DTYPE), compiler_params=_cparams(("parallel",)),
    )(g)


def _chip_sum(owns, gots, ax, chip):
    n_layers = len(owns)
    _, _, a_dim, b_dim = gots[0].shape
    tr = min(256, a_dim)
    per = a_dim // tr

    def own_spec(layer):
        if ax == 2:
            return pl.BlockSpec((1, tr, b_dim), lambda l, i, k: (0, jnp.where(l == layer, i, 0), k[0]))
        return pl.BlockSpec((1, tr, b_dim), lambda l, i, k: (0, k[0] * per + jnp.where(l == layer, i, 0), 0))

    def got_spec(layer):
        return pl.BlockSpec((3, 1, tr, b_dim), lambda l, i, k: (0, 0, jnp.where(l == layer, i, 0), 0))

    def body(k_ref, *refs):
        s_ref = refs[-1]
        for layer in range(n_layers):
            @pl.when(pl.program_id(0) == layer)
            def _(own_ref=refs[layer], got_ref=refs[n_layers + layer]):
                s_ref[...] = ((own_ref[...].astype(F32) + got_ref[0].astype(F32)) + got_ref[1].astype(F32)) + got_ref[2].astype(F32)

    return pl.pallas_call(
        body, name="chip_sum",
        grid_spec=pltpu.PrefetchScalarGridSpec(
            num_scalar_prefetch=1, grid=(n_layers, per),
            in_specs=[own_spec(l) for l in range(n_layers)] + [got_spec(l) for l in range(n_layers)],
            out_specs=pl.BlockSpec((1, tr, b_dim), lambda l, i, k: (l, i, 0))),
        out_shape=jax.ShapeDtypeStruct((n_layers, a_dim, b_dim), F32),
        compiler_params=_cparams(("arbitrary", "arbitrary")),
    )(chip, *owns, *gots)


def _small_pair_sum(small, got, half):
    rh = small.shape[0] // 2

    def body(h_ref, g_ref, o_ref, s_ref):
        s_ref[...] = g_ref[...] + o_ref[...]

    return pl.pallas_call(
        body, name="small_pair_sum",
        grid_spec=pltpu.PrefetchScalarGridSpec(
            num_scalar_prefetch=1, grid=(1,),
            in_specs=[pl.BlockSpec((rh, 128), lambda i, h: (h[0], 0)), pl.BlockSpec((rh, 128), lambda i, h: (0, 0))],
            out_specs=pl.BlockSpec((rh, 128), lambda i, h: (0, 0))),
        out_shape=jax.ShapeDtypeStruct((rh, 128), F32),
    )(half, small, got)


def _small_chip_sum(s1, got, pos):
    rq = got.shape[1]

    def body(k_ref, own_ref, got_ref, s_ref):
        s_ref[0] = ((own_ref[...] + got_ref[0]) + got_ref[1]) + got_ref[2]

    return pl.pallas_call(
        body, name="small_chip_sum",
        grid_spec=pltpu.PrefetchScalarGridSpec(
            num_scalar_prefetch=1, grid=(1,),
            in_specs=[pl.BlockSpec((rq, 128), lambda i, k: (k[0], 0)), pl.BlockSpec((3, rq, 128), lambda i, k: (0, 0, 0))],
            out_specs=pl.BlockSpec((1, rq, 128), lambda i, k: (4 * k[1] + k[0], 0, 0))),
        out_shape=jax.ShapeDtypeStruct((8, rq, 128), F32),
    )(pos, s1, got)


def _adam_math(w, g, m, v):
    m = ADAM_B1 * m + (1.0 - ADAM_B1) * g
    v = ADAM_B2 * v + (1.0 - ADAM_B2) * (g * g)
    m_hat = m / (1.0 - ADAM_B1 ** ADAM_STEP)
    v_hat = v / (1.0 - ADAM_B2 ** ADAM_STEP)
    return -ADAM_LR * (m_hat / (jnp.sqrt(v_hat) + ADAM_EPS) + ADAM_WD * w), m, v


def _adamw_big(w, g_mine, g_other, m, v):
    l_dim, a_dim, b_dim = w.shape
    tr = min(256, a_dim)

    def body(w_ref, g1_ref, g2_ref, m_ref, v_ref, g_ref, d_ref, nm_ref, nv_ref):
        g = g1_ref[...] + g2_ref[...]
        g_ref[...] = g
        d_ref[...], nm_ref[...], nv_ref[...] = _adam_math(w_ref[...], g, m_ref[...], v_ref[...])

    blk = pl.BlockSpec((1, tr, b_dim), lambda l, i: (l, i, 0))
    return pl.pallas_call(
        body, name="adamw_big", grid=(l_dim, a_dim // tr), in_specs=[blk] * 5, out_specs=[blk] * 4,
        out_shape=[jax.ShapeDtypeStruct(w.shape, F32)] * 4,
        compiler_params=_cparams(("parallel", "parallel")),
    )(w, g_mine, g_other, m, v)


def _adamw_small(ws, gs, ms, vs):
    n = len(ws)

    def body(*refs):
        for i in range(n):
            w_ref, g_ref, m_ref, v_ref = refs[i], refs[n + i], refs[2 * n + i], refs[3 * n + i]
            d_ref, nm_ref, nv_ref = refs[4 * n + i], refs[5 * n + i], refs[6 * n + i]
            d_ref[...], nm_ref[...], nv_ref[...] = _adam_math(w_ref[...], g_ref[...], m_ref[...], v_ref[...])

    shapes = [jax.ShapeDtypeStruct(w.shape, F32) for w in ws]
    outs = pl.pallas_call(body, name="adamw_small", out_shape=shapes * 3,
                          compiler_params=_cparams())(*ws, *gs, *ms, *vs)
    return outs[:n], outs[n:2 * n], outs[2 * n:]


def _pack(arrs, row_mult):
    parts = []
    for a in arrs:
        flat = a.reshape(-1)
        pad = (-flat.shape[0]) % 1024
        parts.append(jnp.pad(flat, (0, pad)).reshape(-1, 128))
    buf = jnp.concatenate(parts, axis=0)
    pad = (-buf.shape[0]) % row_mult
    return jnp.pad(buf, ((0, pad), (0, 0)))


def _unpack(buf, shapes):
    out, row = [], 0
    for sh in shapes:
        n = math.prod(sh)
        rows = -(-n // 1024) * 8
        out.append(buf[row:row + rows].reshape(-1)[:n].reshape(sh))
        row += rows
    return out


_NAMES = ['w_in_e', 'b_in_e', 'conv_a_w', 'conv_a_b', 'ln_a_g', 'ln_a_b', 'ln_v_g', 'ln_v_b', 'w_s', 'b_s', 'w_out_e', 'b_out_e',
          'w_in_o', 'b_in_o', 'w_pool', 'pool_scale', 'conv_d_w', 'w_out_o', 'b_out_o', 'ln_g', 'ln_b', 'w_ple', 'w_ple_gate',
          'b_ple_gate']
_BIG = ['w_in_e', 'w_out_e', 'w_in_o', 'w_out_o', 'w_ple', 'w_ple_gate']
_BIG_AXES = [2, 1, 2, 1, 2, 1]
_SMALL_SHARDED = ['conv_a_w', 'b_in_o', 'pool_scale', 'conv_d_w', 'b_out_o']


def kernel(x, p, w_in_e, b_in_e, conv_a_w, conv_a_b, ln_a_g, ln_a_b, ln_v_g, ln_v_b, w_s, b_s, w_out_e, b_out_e, w_in_o, b_in_o, w_pool, pool_scale, conv_d_w, w_out_o, b_out_o, ln_g, ln_b, w_ple, w_ple_gate, b_ple_gate, loss_target, m_w_in_e, m_b_in_e, m_conv_a_w, m_conv_a_b, m_ln_a_g, m_ln_a_b, m_ln_v_g, m_ln_v_b, m_w_s, m_b_s, m_w_out_e, m_b_out_e, m_w_in_o, m_b_in_o, m_w_pool, m_pool_scale, m_conv_d_w, m_w_out_o, m_b_out_o, m_ln_g, m_ln_b, m_w_ple, m_w_ple_gate, m_b_ple_gate, v_w_in_e, v_b_in_e, v_conv_a_w, v_conv_a_b, v_ln_a_g, v_ln_a_b, v_ln_v_g, v_ln_v_b, v_w_s, v_b_s, v_w_out_e, v_b_out_e, v_w_in_o, v_b_in_o, v_w_pool, v_pool_scale, v_conv_d_w, v_w_out_o, v_b_out_o, v_ln_g, v_ln_b, v_w_ple, v_w_ple_gate, v_b_ple_gate):
    args = locals()
    wts = {n: args[n] for n in _NAMES}
    mom = {n: args["m_" + n] for n in _NAMES}
    var = {n: args["v_" + n] for n in _NAMES}
    bsz, s_len, d = x.shape
    t = bsz * s_len
    cx, cy, cc = lax.axis_index("x"), lax.axis_index("y"), lax.axis_index("c")
    chip = (2 * cx + cy).astype(jnp.int32).reshape(1)
    half = cc.astype(jnp.int32).reshape(1)
    pos = jnp.concatenate([chip, half])

    def placed(name, ax, layer):
        return _place_shard(wts[name], ax, chip, layer=layer)

    sv = _pack([wts[n] for n in _SMALL_SHARDED], 8)
    first_refs = _gather_async("gather_first", DEPTH, [placed('w_in_e', 2, 0), _place_shard(sv[None], 1, chip, F32)], [2, 1])
    layer_refs = []
    for i in range(DEPTH):
        sfx = '_e' if i % 2 == 0 else '_o'
        items = [('w_out' + sfx, 1, i // 2), ('w_ple_gate', 1, i), ('w_ple', 2, i)] + ([('w_in' + sfx, 2, i // 2)] if i else [])
        layer_refs.append(_gather_async("gather_layer%d" % i, i, [placed(*it) for it in items], [it[1] for it in items]))
    fw = {}
    w_in_first = first_refs[0][...]
    sv_all = first_refs[1][...].reshape((4,) + sv.shape)
    small_parts = [_unpack(sv_all[k], [wts[n].shape for n in _SMALL_SHARDED]) for k in range(4)]
    for i, n in enumerate(_SMALL_SHARDED):
        fw[n] = jnp.concatenate([small_parts[k][i] for k in range(4)], axis=-1)
    for n in _NAMES:
        fw.setdefault(n, wts[n])

    def row8(rows, width):
        rows = [r.reshape(1, width) for r in rows]
        return jnp.concatenate(rows + [jnp.zeros((8 - len(rows), width), F32)], axis=0)

    x2 = x.reshape(t, d)
    saved = []
    for i in range(DEPTH):
        j = i // 2
        even = i % 2 == 0
        b_in, b_out = (fw['b_in_e'], fw['b_out_e']) if even else (fw['b_in_o'], fw['b_out_o'])
        w_in = layer_refs[i][3][...][0] if i else w_in_first[0]
        z = _in_proj(x2, w_in, b_in[j].reshape(1, N_COLS))
        z3 = z.reshape(bsz, s_len, N_COLS)
        if even:
            cw = jnp.concatenate([fw['conv_a_w'][j], jnp.zeros((1, W_BR), F32)], axis=0)
            mvec = row8([fw['conv_a_b'][j], fw['ln_a_g'][j], fw['ln_a_b'][j], fw['ln_v_g'][j], fw['ln_v_b'][j]], W_BR)
            bsf = jnp.repeat(fw['b_s'][j].T, W_BR // 8, axis=1)
            mix = (cw, mvec, fw['w_s'][j], jnp.swapaxes(fw['w_s'][j], 1, 2), bsf)
            y3 = _even_fwd(z3, cw, mvec, fw['w_s'][j], bsf)
        else:
            mvec = row8([fw['pool_scale'][j]] + [fw['conv_d_w'][j][k] for k in range(3)], W_BR)
            mix = (fw['w_pool'][j].astype(MM_DTYPE), mvec)
            y3 = _odd_fwd(z3, mix[0], mvec)
        pvec = row8([b_out[j], fw['ln_g'][i], fw['ln_b'][i], fw['b_ple_gate'][i]], d)
        post_w = (layer_refs[i][0][...][0], layer_refs[i][1][...][0], layer_refs[i][2][...][0], pvec)
        p2 = (p.reshape(DEPTH, t, D_PLE), i)
        y2 = y3.reshape(t, 2 * W_BR)
        if i < DEPTH - 1:
            xn, r2, gate2 = _post_fwd(y2, x2, p2, *post_w)
        else:
            dx, r2, gate2, sq = _post_fwd(y2, x2, p2, *post_w, tgt=loss_target.reshape(t, d))
        saved.append((x2, z3, y2, r2, gate2, p2, w_in, mix, post_w))
        x2 = xn

    gr = {n: [None] * wts[n].shape[0] for n in _NAMES}
    for i in reversed(range(DEPTH)):
        j = i // 2
        even = i % 2 == 0
        x_in, z3, y2, r2, gate2, p2, w_in, mix, post_w = saved[i]
        dxr, dy, dwo, dwg, dwp, dpv = _post_bwd(dx, r2, gate2, p2, y2, *post_w)
        dy3 = dy.reshape(bsz, s_len, 2 * W_BR)
        sfx = '_e' if even else '_o'
        gr['w_out' + sfx][j], gr['b_out' + sfx][j] = dwo, dpv[0]
        gr['w_ple_gate'][i], gr['w_ple'][i] = dwg, dwp
        gr['ln_g'][i], gr['ln_b'][i], gr['b_ple_gate'][i] = dpv[1], dpv[2], dpv[3]
        if even:
            dz3, dbin, dcw, dmv, dws, dbsf = _even_bwd(z3, dy3, *mix)
            gr['conv_a_w'][j], gr['conv_a_b'][j] = dcw[:KA], dmv[0]
            gr['ln_a_g'][j], gr['ln_a_b'][j], gr['ln_v_g'][j], gr['ln_v_b'][j] = dmv[1], dmv[2], dmv[3], dmv[4]
            gr['w_s'][j] = dws
            gr['b_s'][j] = jnp.sum(dbsf.reshape(GBLK, 8, W_BR // 8), axis=2).T
        else:
            dz3, dbin, dwpool, dmv = _odd_bwd(z3, dy3, *mix)
            gr['w_pool'][j], gr['pool_scale'][j], gr['conv_d_w'][j] = dwpool, dmv[0], dmv[1:4]
        gr['b_in' + sfx][j] = dbin[0]
        dz2 = dz3.reshape(t, N_COLS)
        dwi, dw_done = _in_proj_bwd_dw(x_in, dz2)
        post_items = [('w_out' + sfx, j, dwo[None], 1), ('w_ple_gate', i, dwg[None], 1), ('w_ple', i, dwp[None], 2)]
        in_item = ('w_in' + sfx, j, dwi[None], 2)
        batches = [post_items + [in_item]] if i else [post_items, [in_item]]
        for bi, items in enumerate(batches):
            sent = [_to_wire(it[2]) if (i == 0 and bi == 1) else it[2] for it in items]
            lands = _scatter_async("scatter_layer%d_%d" % (i, bi), DEPTH + 1 + 2 * i + bi, sent, [it[3] for it in items])
            for it, land in zip(items, lands):
                gr[it[0]][it[1]] = (it[2], land, it[3])
        dx = _in_proj_bwd_dx(dxr, dz2, w_in, dw_done)
    grad_x = dx.reshape(bsz, s_len, d)

    small_names = [n for n in _NAMES if n not in _BIG]
    g_small_full = [jnp.stack(gr[n]) for n in small_names] + [sq]
    small_all = _small_allreduce(_pack(g_small_full, 64))
    *g_small, sq_all = _unpack(small_all, [g.shape for g in g_small_full])
    loss = 0.5 * jnp.sum(sq_all) / d
    big_order = _BIG[1:] + _BIG[:1]
    sums = [_chip_sum([g[0] for g in gr[n]], [g[1] for g in gr[n]], gr[n][0][2], chip) for n in big_order]
    others = (list(_pair_swap_async("pair_swap_a", 3 * DEPTH + 1, sums[:-1]))
              + list(_pair_swap_async("pair_swap_b", 3 * DEPTH + 2, sums[-1:])))
    grads = {}
    for n, g in zip(small_names, g_small):
        if n in _SMALL_SHARDED:
            w = wts[n].shape[-1]
            g = lax.dynamic_slice_in_dim(g, (2 * cx + cy) * w, w, axis=g.ndim - 1)
        grads[n] = g

    delta, new_m, new_v = {}, {}, {}
    ds, ms, vs = _adamw_small([wts[n] for n in small_names], [grads[n] for n in small_names],
                              [mom[n] for n in small_names], [var[n] for n in small_names])
    for n, a, b, c_ in zip(small_names, ds, ms, vs):
        delta[n], new_m[n], new_v[n] = a, b, c_
    for n, mine, other in zip(big_order, sums, others):
        grads[n], delta[n], new_m[n], new_v[n] = _adamw_big(wts[n], mine, other, mom[n], var[n])

    return (loss, grad_x, *[grads[n] for n in _NAMES], *[delta[n] for n in _NAMES],
            *[new_m[n] for n in _NAMES], *[new_v[n] for n in _NAMES])
```

```python
import functools
import math

import jax
import jax.numpy as jnp
from jax import lax
from jax.experimental import pallas as pl
from jax.experimental.pallas import tpu as pltpu
from jax.experimental.pallas import tpu_sc as plsc

F32 = jnp.float32
MM_DTYPE = jnp.bfloat16
WIRE_DTYPE = jnp.bfloat16
SEQ_TILE = 512
ROW_TILE = 512
BWD_ROW_TILE = 512
HALO = 32
CONV_CHUNK = 32
ROW_CHUNK = 64
GBLK = 128
VMEM_LIMIT = 56 * 1024 * 1024

D_MODEL = 1024
W_BR = 512
N_COLS = 6 * W_BR
D_PLE = 256
KA = 31
DEPTH = 4
POOL_WINDOWS = (2, 4, 8, 16)
ALPHA = (2.0 * DEPTH) ** 0.25
LN_EPS = 1e-5
GELU_C = math.sqrt(2.0 / math.pi)

ADAM_LR, ADAM_B1, ADAM_B2, ADAM_EPS, ADAM_WD, ADAM_STEP = 0.001, 0.9, 0.999, 1e-08, 0.01, 10

MESH = pl.DeviceIdType.MESH
ANY = pl.BlockSpec(memory_space=pl.ANY)


def _cparams(sem=None):
    return pltpu.CompilerParams(dimension_semantics=sem, vmem_limit_bytes=VMEM_LIMIT)


def _sigmoid(x):
    return 1.0 / (1.0 + jnp.exp(-x))


def _silu(x):
    return x * _sigmoid(x)


def _silu_grad(x):
    s = _sigmoid(x)
    return x * s, s * (1.0 + x * (1.0 - s))


def _gelu(x):
    return 0.5 * x * (1.0 + jnp.tanh(GELU_C * (x + 0.044715 * (x * x * x))))


def _gelu_grad(x):
    x2 = x * x
    th = jnp.tanh(GELU_C * (x + 0.044715 * (x * x2)))
    return 0.5 * x * (1.0 + th), 0.5 * (1.0 + th) + 0.5 * x * (1.0 - th * th) * (GELU_C * (1.0 + 3.0 * 0.044715 * x2))


def _ln_stats(x):
    mu = jnp.mean(x, axis=-1, keepdims=True)
    d = x - mu
    var = jnp.mean(d * d, axis=-1, keepdims=True)
    rs = lax.rsqrt(var + LN_EPS)
    return d * rs, rs


def _ln_bwd(dxh, xh, rs):
    return rs * (dxh - jnp.mean(dxh, axis=-1, keepdims=True) - xh * jnp.mean(dxh * xh, axis=-1, keepdims=True))


def _mm(a):
    return a.astype(MM_DTYPE)


def _dot(a, b):
    return jnp.dot(_mm(a), _mm(b), preferred_element_type=F32)


def _dot_nt(a, b):
    return lax.dot_general(_mm(a), _mm(b), (((1,), (1,)), ((), ())), preferred_element_type=F32)


def _dot_tn(a, b):
    return lax.dot_general(_mm(a), _mm(b), (((0,), (0,)), ((), ())), preferred_element_type=F32)


def _rowsum(x):
    return jnp.sum(x, axis=0, keepdims=True)


def _in_proj(x2, w, b):
    t, d = x2.shape
    n = w.shape[1]
    tm = min(ROW_TILE, t)
    nc = 768

    def body(x_ref, w_ref, b_ref, z_ref):
        xb = _mm(x_ref[...])
        for j in range(n // nc):
            cs = slice(j * nc, (j + 1) * nc)
            z_ref[:, cs] = jnp.dot(xb, w_ref[:, cs], preferred_element_type=F32) + b_ref[:, cs]

    return pl.pallas_call(
        body, name="in_proj", grid=(t // tm,),
        in_specs=[pl.BlockSpec((tm, d), lambda i: (i, 0)), pl.BlockSpec((d, n), lambda i: (0, 0)),
                  pl.BlockSpec((1, n), lambda i: (0, 0))],
        out_specs=pl.BlockSpec((tm, n), lambda i: (i, 0)),
        out_shape=jax.ShapeDtypeStruct((t, n), F32),
        compiler_params=_cparams(("parallel",)),
    )(x2, w, b)


def _in_proj_bwd_dx(dxr, dz, w, after):
    t, d = dxr.shape
    n = w.shape[1]
    tm = min(ROW_TILE, t)

    def body(dxr_ref, dz_ref, w_ref, after_ref, dx_ref):
        dx_ref[...] = dxr_ref[...] + _dot_nt(dz_ref[...], w_ref[...])

    return pl.pallas_call(
        body, name="in_proj_bwd_dx", grid=(t // tm,),
        in_specs=[pl.BlockSpec((tm, d), lambda i: (i, 0)), pl.BlockSpec((tm, n), lambda i: (i, 0)),
                  pl.BlockSpec((d, n), lambda i: (0, 0)), pl.BlockSpec((8, 128), lambda i: (0, 0))],
        out_specs=pl.BlockSpec((tm, d), lambda i: (i, 0)),
        out_shape=jax.ShapeDtypeStruct((t, d), F32),
        compiler_params=_cparams(("parallel",)),
    )(dxr, dz, w, after)


def _in_proj_bwd_dw(x2, dz):
    t, d = x2.shape
    n = dz.shape[1]
    tm = min(ROW_TILE, t)
    nc = 768

    def body(x_ref, dz_ref, dw_ref, done_ref):
        @pl.when(pl.program_id(0) == 0)
        def _():
            dw_ref[...] = jnp.zeros_like(dw_ref)
            done_ref[...] = jnp.zeros_like(done_ref)
        xb = _mm(x_ref[...])
        for j in range(n // nc):
            cs = slice(j * nc, (j + 1) * nc)
            dw_ref[:, cs] += _dot_tn(xb, dz_ref[:, cs])

    return pl.pallas_call(
        body, name="in_proj_bwd_dw", grid=(t // tm,),
        in_specs=[pl.BlockSpec((tm, d), lambda i: (i, 0)), pl.BlockSpec((tm, n), lambda i: (i, 0))],
        out_specs=[pl.BlockSpec((d, n), lambda i: (0, 0)), pl.BlockSpec((8, 128), lambda i: (0, 0))],
        out_shape=[jax.ShapeDtypeStruct((d, n), F32), jax.ShapeDtypeStruct((8, 128), F32)],
        compiler_params=_cparams(("arbitrary",)),
    )(x2, dz)


def _halo_specs(ts, s_len, cols, left=True):
    per = ts // HALO
    last = s_len // HALO - 1
    if left:
        return pl.BlockSpec((1, HALO, cols), lambda b, s: (b, jnp.maximum(s * per - 1, 0), 0))
    return pl.BlockSpec((1, HALO, cols), lambda b, s: (b, jnp.minimum((s + 1) * per, last), 0))


def _build_shifts(src_ref, sh_ref, rows):
    for r in range(1, 8):
        sh_ref[r - 1, 0:rows, :] = src_ref[r:r + rows, :]


def _shifted(src_ref, sh_ref, r, start, n):
    if r == 0:
        return src_ref[pl.ds(start, n), :]
    return sh_ref[r - 1, pl.ds(start, n), :]


def _tril_masks():
    ri = lax.broadcasted_iota(jnp.int32, (GBLK, GBLK), 0)
    ci = lax.broadcasted_iota(jnp.int32, (GBLK, GBLK), 1)
    return ri >= ci, ci >= ri


def _spatial(w_ref, keep, vb):
    lane = lax.broadcasted_iota(jnp.int32, (GBLK, GBLK), 1)
    outs = []
    for p in range(4):
        xs = vb[:, p * GBLK:(p + 1) * GBLK]
        r0 = jnp.dot(_mm(jnp.where(keep, w_ref[2 * p], 0.0)), xs, preferred_element_type=F32)
        r1 = jnp.dot(_mm(jnp.where(keep, w_ref[2 * p + 1], 0.0)), xs, preferred_element_type=F32)
        outs.append(jnp.where(lane < 64, r0, r1))
    return jnp.concatenate(outs, axis=1)


def _even_fwd(z3, cw, vec, ws, bsf):
    bsz, s_len, _ = z3.shape
    ts = min(SEQ_TILE, s_len)
    ext_rows = ts + HALO

    def body(z_ref, zl_ref, cw_ref, vec_ref, ws_ref, bsf_ref, y_ref, a1_ref, ext_ref, sh_ref):
        s = pl.program_id(1)
        hl = zl_ref[0]
        a0h = hl[:, 0:W_BR] * _sigmoid(hl[:, W_BR:2 * W_BR])
        ext_ref[0:HALO, :] = jnp.where(s > 0, a0h, 0.0)
        for a, b in _row_chunks(0, ts):
            ext_ref[HALO + a:HALO + b, :] = z_ref[0, a:b, 0:W_BR] * _sigmoid(z_ref[0, a:b, W_BR:2 * W_BR])
        ext_ref[ext_rows:ext_rows + 8, :] = jnp.zeros((8, W_BR), F32)
        _build_shifts(ext_ref, sh_ref, ext_rows)

        def conv_chunk(ci, carry):
            base = pl.multiple_of(ci * CONV_CHUNK, CONV_CHUNK)
            acc = jnp.zeros((CONV_CHUNK, W_BR), F32) + vec_ref[0:1, :]
            for k in range(KA):
                q, r = divmod(2 + k, 8)
                acc = acc + _shifted(ext_ref, sh_ref, r, base + 8 * q, CONV_CHUNK) * cw_ref[k:k + 1, :]
            a1_ref[0, pl.ds(base, CONV_CHUNK), :] = acc
            return carry

        lax.fori_loop(0, ts // CONV_CHUNK, conv_chunk, 0)
        keep, _ = _tril_masks()

        def block(bi, carry):
            rows = pl.ds(pl.multiple_of(bi * GBLK, GBLK), GBLK)
            xh, _ = _ln_stats(a1_ref[0, rows, :])
            a = _silu(xh * vec_ref[1:2, :] + vec_ref[2:3, :]) * _silu(z_ref[0, rows, 2 * W_BR:3 * W_BR])
            y_ref[0, rows, 0:W_BR] = a.astype(y_ref.dtype)
            ua = _gelu(z_ref[0, rows, 3 * W_BR:4 * W_BR])
            vh, _ = _ln_stats(_gelu(z_ref[0, rows, 4 * W_BR:5 * W_BR]))
            vl = vh * vec_ref[3:4, :] + vec_ref[4:5, :]
            sg = _spatial(ws_ref, keep, _mm(vl)) + bsf_ref[...]
            g = ua * sg * _silu(z_ref[0, rows, 5 * W_BR:6 * W_BR])
            y_ref[0, rows, W_BR:2 * W_BR] = g.astype(y_ref.dtype)
            return carry

        lax.fori_loop(0, ts // GBLK, block, 0)

    full = lambda shape: pl.BlockSpec(shape, lambda b, s: (0,) * len(shape))
    return pl.pallas_call(
        body, name="even_fwd", grid=(bsz, s_len // ts),
        in_specs=[pl.BlockSpec((1, ts, N_COLS), lambda b, s: (b, s, 0)), _halo_specs(ts, s_len, 2 * W_BR),
                  full((32, W_BR)), full((8, W_BR)), full((8, GBLK, GBLK)), full((GBLK, W_BR))],
        out_specs=[pl.BlockSpec((1, ts, 2 * W_BR), lambda b, s: (b, s, 0)), pl.BlockSpec((1, ts, W_BR), lambda b, s: (b, s, 0))],
        out_shape=[jax.ShapeDtypeStruct((bsz, s_len, 2 * W_BR), MM_DTYPE), jax.ShapeDtypeStruct((bsz, s_len, W_BR), F32)],
        scratch_shapes=[pltpu.VMEM((ext_rows + 8, W_BR), F32), pltpu.VMEM((7, ext_rows, W_BR), F32)],
        compiler_params=_cparams(("parallel", "parallel")),
    )(z3, z3, cw, vec, ws, bsf)


def _even_bwd(z3, dy3, a13, cw, vec, ws, wst, bsf):
    bsz, s_len, _ = z3.shape
    ts = min(SEQ_TILE, s_len)
    n_s = s_len // ts
    ext_rows = ts + 2 * HALO
    a_rows = ts + HALO

    def body(z_ref, zl_ref, zr_ref, dy_ref, dyr_ref, a1_ref, a1r_ref, cw_ref, vec_ref, ws_ref, wst_ref, bsf_ref,
             dz_ref, dbin_ref, dcw_ref, dvec_ref, dws_ref, dbsf_ref,
             ext_ref, sh_ref, ag_ref, dya_ref, d_ref, accw_ref):
        b = pl.program_id(0)
        s = pl.program_id(1)

        @pl.when((b == 0) & (s == 0))
        def _():
            dbin_ref[...] = jnp.zeros_like(dbin_ref)
            dcw_ref[...] = jnp.zeros_like(dcw_ref)
            dvec_ref[...] = jnp.zeros_like(dvec_ref)
            dws_ref[...] = jnp.zeros_like(dws_ref)
            dbsf_ref[...] = jnp.zeros_like(dbsf_ref)

        has_right = s < n_s - 1
        hl = zl_ref[0]
        hr = zr_ref[0]
        ext_ref[0:HALO, :] = jnp.where(s > 0, hl[:, 0:W_BR] * _sigmoid(hl[:, W_BR:2 * W_BR]), 0.0)
        for a, b in _row_chunks(0, ts):
            ext_ref[HALO + a:HALO + b, :] = z_ref[0, a:b, 0:W_BR] * _sigmoid(z_ref[0, a:b, W_BR:2 * W_BR])
            ag_ref[a:b, :] = z_ref[0, a:b, 2 * W_BR:3 * W_BR]
            dya_ref[a:b, :] = dy_ref[0, a:b, 0:W_BR]
        ext_ref[HALO + ts:ext_rows, :] = hr[:, 0:W_BR] * _sigmoid(hr[:, W_BR:2 * W_BR])
        ext_ref[ext_rows:ext_rows + 8, :] = jnp.zeros((8, W_BR), F32)
        ag_ref[ts:a_rows, :] = hr[:, 2 * W_BR:3 * W_BR]
        dya_ref[ts:a_rows, :] = dyr_ref[0][:, 0:W_BR]
        _build_shifts(ext_ref, sh_ref, ext_rows)

        def a_chunk(base, n, main):
            rows = pl.ds(base, n)
            xh, rs = _ln_stats(a1_ref[0, rows, :] if main else a1r_ref[0])
            ln = xh * vec_ref[1:2, :] + vec_ref[2:3, :]
            sl, dsl = _silu_grad(ln)
            sgt, dsgt = _silu_grad(ag_ref[rows, :])
            dya = dya_ref[rows, :]
            dln = dya * sgt * dsl
            da1 = _ln_bwd(dln * vec_ref[1:2, :], xh, rs)
            if main:
                d_ref[rows, :] = da1
                dag = dya * sl * dsgt
                dz_ref[0, rows, 2 * W_BR:3 * W_BR] = dag.astype(dz_ref.dtype)
                dbin_ref[0:1, 2 * W_BR:3 * W_BR] += _rowsum(dag)
                dvec_ref[0:1, :] += _rowsum(da1)
                dvec_ref[1:2, :] += _rowsum(dln * xh)
                dvec_ref[2:3, :] += _rowsum(dln)
            else:
                d_ref[rows, :] = jnp.where(has_right, da1, 0.0)

        def a_main(ci, carry):
            a_chunk(pl.multiple_of(ci * GBLK, GBLK), GBLK, True)
            return carry

        lax.fori_loop(0, ts // GBLK, a_main, 0)
        a_chunk(ts, HALO, False)
        d_ref[a_rows:a_rows + 8, :] = jnp.zeros((8, W_BR), F32)

        accw_ref[...] = jnp.zeros_like(accw_ref)

        def dw_chunk(ci, carry):
            base = pl.multiple_of(ci * CONV_CHUNK, CONV_CHUNK)
            d = d_ref[pl.ds(base, CONV_CHUNK), :]
            for k in range(KA):
                q, r = divmod(2 + k, 8)
                prod = d * _shifted(ext_ref, sh_ref, r, base + 8 * q, CONV_CHUNK)
                accw_ref[k] += jnp.sum(prod.reshape(CONV_CHUNK // 8, 8, W_BR), axis=0)
            return carry

        lax.fori_loop(0, ts // CONV_CHUNK, dw_chunk, 0)
        dcw_ref[...] += jnp.sum(accw_ref[...], axis=1)

        _build_shifts(d_ref, sh_ref, a_rows)

        def dx_chunk(ci, carry):
            base = pl.multiple_of(ci * CONV_CHUNK, CONV_CHUNK)
            rows = pl.ds(base, CONV_CHUNK)
            acc = jnp.zeros((CONV_CHUNK, W_BR), F32)
            for m in range(KA):
                q, r = divmod(m, 8)
                acc = acc + _shifted(d_ref, sh_ref, r, base + 8 * q, CONV_CHUNK) * cw_ref[KA - 1 - m:KA - m, :]
            aval = z_ref[0, rows, 0:W_BR]
            sg = _sigmoid(z_ref[0, rows, W_BR:2 * W_BR])
            dval = acc * sg
            dglu = acc * aval * sg * (1.0 - sg)
            dz_ref[0, rows, 0:W_BR] = dval.astype(dz_ref.dtype)
            dz_ref[0, rows, W_BR:2 * W_BR] = dglu.astype(dz_ref.dtype)
            dbin_ref[0:1, 0:W_BR] += _rowsum(dval)
            dbin_ref[0:1, W_BR:2 * W_BR] += _rowsum(dglu)
            return carry

        lax.fori_loop(0, ts // CONV_CHUNK, dx_chunk, 0)

        keep, keep_t = _tril_masks()
        lane = lax.broadcasted_iota(jnp.int32, (GBLK, GBLK), 1)

        def block(bi, carry):
            rows = pl.ds(pl.multiple_of(bi * GBLK, GBLK), GBLK)
            ua, dua = _gelu_grad(z_ref[0, rows, 3 * W_BR:4 * W_BR])
            va, dva = _gelu_grad(z_ref[0, rows, 4 * W_BR:5 * W_BR])
            sgt, dsgt = _silu_grad(z_ref[0, rows, 5 * W_BR:6 * W_BR])
            vh, rs = _ln_stats(va)
            vlb = _mm(vh * vec_ref[3:4, :] + vec_ref[4:5, :])
            sg = _spatial(ws_ref, keep, vlb) + bsf_ref[...]
            dyg = dy_ref[0, rows, W_BR:2 * W_BR]
            du = dyg * sg * sgt * dua
            dsg = dyg * ua * sgt
            dgg = dyg * ua * sg * dsgt
            dvl = _spatial(wst_ref, keep_t, _mm(dsg))
            for p in range(4):
                dsp = dsg[:, p * GBLK:(p + 1) * GBLK]
                vlp = vlb[:, p * GBLK:(p + 1) * GBLK]
                dws_ref[2 * p] += jnp.where(keep, _dot_nt(jnp.where(lane < 64, dsp, 0.0), vlp), 0.0)
                dws_ref[2 * p + 1] += jnp.where(keep, _dot_nt(jnp.where(lane >= 64, dsp, 0.0), vlp), 0.0)
            dbsf_ref[...] += dsg
            dvec_ref[3:4, :] += _rowsum(dvl * vh)
            dvec_ref[4:5, :] += _rowsum(dvl)
            dv = _ln_bwd(dvl * vec_ref[3:4, :], vh, rs) * dva
            dz_ref[0, rows, 3 * W_BR:4 * W_BR] = du.astype(dz_ref.dtype)
            dz_ref[0, rows, 4 * W_BR:5 * W_BR] = dv.astype(dz_ref.dtype)
            dz_ref[0, rows, 5 * W_BR:6 * W_BR] = dgg.astype(dz_ref.dtype)
            dbin_ref[0:1, 3 * W_BR:4 * W_BR] += _rowsum(du)
            dbin_ref[0:1, 4 * W_BR:5 * W_BR] += _rowsum(dv)
            dbin_ref[0:1, 5 * W_BR:6 * W_BR] += _rowsum(dgg)
            return carry

        lax.fori_loop(0, ts // GBLK, block, 0)

    full = lambda shape: pl.BlockSpec(shape, lambda b, s: (0,) * len(shape))
    acc_shapes = [(1, N_COLS), (32, W_BR), (8, W_BR), (8, GBLK, GBLK), (GBLK, W_BR)]
    return pl.pallas_call(
        body, name="even_bwd", grid=(bsz, n_s),
        in_specs=[pl.BlockSpec((1, ts, N_COLS), lambda b, s: (b, s, 0)),
                  _halo_specs(ts, s_len, N_COLS, True), _halo_specs(ts, s_len, N_COLS, False),
                  pl.BlockSpec((1, ts, 2 * W_BR), lambda b, s: (b, s, 0)), _halo_specs(ts, s_len, 2 * W_BR, False),
                  pl.BlockSpec((1, ts, W_BR), lambda b, s: (b, s, 0)), _halo_specs(ts, s_len, W_BR, False),
                  full((32, W_BR)), full((8, W_BR)), full((8, GBLK, GBLK)), full((8, GBLK, GBLK)), full((GBLK, W_BR))],
        out_specs=[pl.BlockSpec((1, ts, N_COLS), lambda b, s: (b, s, 0))] + [full(sh) for sh in acc_shapes],
        out_shape=[jax.ShapeDtypeStruct((bsz, s_len, N_COLS), MM_DTYPE)] + [jax.ShapeDtypeStruct(sh, F32) for sh in acc_shapes],
        scratch_shapes=[pltpu.VMEM((ext_rows + 8, W_BR), F32), pltpu.VMEM((7, ext_rows, W_BR), F32),
                        pltpu.VMEM((a_rows, W_BR), F32), pltpu.VMEM((a_rows, W_BR), F32),
                        pltpu.VMEM((a_rows + 8, W_BR), F32), pltpu.VMEM((32, 8, W_BR), F32)],
        compiler_params=_cparams(("arbitrary", "arbitrary")),
    )(z3, z3, z3, dy3, dy3, a13, a13, cw, vec, ws, wst, bsf)


def _row_chunks(lo, hi):
    return [(a, min(a + ROW_CHUNK, hi)) for a in range(lo, hi, ROW_CHUNK)]


def _pool_stages(e_refs, rows):
    e0, e1, e2, e3, e4 = e_refs
    for a, b in _row_chunks(8, rows):
        e1[a:b, :] = e0[a:b, :] + e0[a - 1:b - 1, :]
    for a, b in _row_chunks(16, rows):
        e2[a:b, GBLK:] = e1[a:b, GBLK:] + e1[a - 2:b - 2, GBLK:]
    for a, b in _row_chunks(24, rows):
        e3[a:b, 2 * GBLK:] = e2[a:b, 2 * GBLK:] + e2[a - 4:b - 4, 2 * GBLK:]
    for a, b in _row_chunks(32, rows):
        e4[a:b, 3 * GBLK:] = e3[a:b, 3 * GBLK:] + e3[a - 8:b - 8, 3 * GBLK:]


def _pool_counts(start, n):
    pos = (start + 1 + lax.broadcasted_iota(jnp.int32, (n, 1), 0)).astype(F32)
    return [jnp.minimum(pos, float(w)) for w in POOL_WINDOWS]


def _pooled_into(e_refs, pooled_ref, s, ts):
    for a, b in _row_chunks(0, ts):
        cnt = _pool_counts(s * ts + a, b - a)
        for g in range(4):
            cs = slice(g * GBLK, (g + 1) * GBLK)
            pooled_ref[a:b, cs] = e_refs[g + 1][HALO + a:HALO + b, cs] / cnt[g] - e_refs[0][HALO + a:HALO + b, cs]


def _odd_prologue(z_ref, hl, s, ts, e_refs, pooled_ref, dext_ref, ec_ref, vec_ref):
    e0 = e_refs[0]
    e0[0:HALO, :] = jnp.where(s > 0, hl[:, 0:W_BR], 0.0)
    dext_ref[0:HALO, :] = jnp.where(s > 0, hl[:, 2 * W_BR:3 * W_BR] * hl[:, 4 * W_BR:5 * W_BR], 0.0)
    for a, b in _row_chunks(0, ts):
        e0[HALO + a:HALO + b, :] = z_ref[0, a:b, 0:W_BR]
        dext_ref[HALO + a:HALO + b, :] = z_ref[0, a:b, 2 * W_BR:3 * W_BR] * z_ref[0, a:b, 4 * W_BR:5 * W_BR]
    _pool_stages(e_refs, ts + HALO)
    _pooled_into(e_refs, pooled_ref, s, ts)
    for a, b in _row_chunks(0, ts):
        ec_ref[a:b, :] = (vec_ref[1:2, :] * dext_ref[HALO - 2 + a:HALO - 2 + b, :] + vec_ref[2:3, :] * dext_ref[HALO - 1 + a:HALO - 1 + b, :]
                          + vec_ref[3:4, :] * dext_ref[HALO + a:HALO + b, :])


def _odd_fwd(z3, wp, vec):
    bsz, s_len, _ = z3.shape
    ts = min(SEQ_TILE, s_len)
    ext_rows = ts + HALO

    def body(z_ref, zl_ref, wp_ref, vec_ref, y_ref, e0, e1, e2, e3, e4, pooled_ref, dext_ref, ec_ref):
        s = pl.program_id(1)
        _odd_prologue(z_ref, zl_ref[0], s, ts, (e0, e1, e2, e3, e4), pooled_ref, dext_ref, ec_ref, vec_ref)

        def block(bi, carry):
            rows = pl.ds(pl.multiple_of(bi * GBLK, GBLK), GBLK)
            pb = _mm(pooled_ref[rows, :])
            cpre = jnp.concatenate([jnp.dot(pb[:, g * GBLK:(g + 1) * GBLK], wp_ref[g], preferred_element_type=F32)
                                    for g in range(4)], axis=1)
            c = cpre * vec_ref[0:1, :] * _silu(z_ref[0, rows, W_BR:2 * W_BR])
            d = z_ref[0, rows, 3 * W_BR:4 * W_BR] * ec_ref[rows, :] * _silu(z_ref[0, rows, 5 * W_BR:6 * W_BR])
            y_ref[0, rows, 0:W_BR] = c.astype(y_ref.dtype)
            y_ref[0, rows, W_BR:2 * W_BR] = d.astype(y_ref.dtype)
            return carry

        lax.fori_loop(0, ts // GBLK, block, 0)

    full = lambda shape: pl.BlockSpec(shape, lambda b, s: (0,) * len(shape))
    ebuf = pltpu.VMEM((ext_rows, W_BR), F32)
    return pl.pallas_call(
        body, name="odd_fwd", grid=(bsz, s_len // ts),
        in_specs=[pl.BlockSpec((1, ts, N_COLS), lambda b, s: (b, s, 0)), _halo_specs(ts, s_len, N_COLS),
                  full((4, GBLK, GBLK)), full((8, W_BR))],
        out_specs=pl.BlockSpec((1, ts, 2 * W_BR), lambda b, s: (b, s, 0)),
        out_shape=jax.ShapeDtypeStruct((bsz, s_len, 2 * W_BR), MM_DTYPE),
        scratch_shapes=[ebuf, ebuf, ebuf, ebuf, ebuf, pltpu.VMEM((ts, W_BR), F32), ebuf, pltpu.VMEM((ts, W_BR), F32)],
        compiler_params=_cparams(("parallel", "parallel")),
    )(z3, z3, wp, vec)


def _odd_bwd(z3, dy3, wp, vec):
    bsz, s_len, _ = z3.shape
    ts = min(SEQ_TILE, s_len)
    n_s = s_len // ts
    ext_rows = ts + HALO

    def body(z_ref, zl_ref, zr_ref, dy_ref, dyr_ref, wp_ref, vec_ref,
             dz_ref, dbin_ref, dwp_ref, dvec_ref,
             e0, e1, e2, e3, e4, pooled_ref, dext_ref, ec_ref, q_ref, dp_ref, de_ref, f1, f2, f3, f4):
        b = pl.program_id(0)
        s = pl.program_id(1)

        @pl.when((b == 0) & (s == 0))
        def _():
            dbin_ref[...] = jnp.zeros_like(dbin_ref)
            dwp_ref[...] = jnp.zeros_like(dwp_ref)
            dvec_ref[...] = jnp.zeros_like(dvec_ref)

        has_right = s < n_s - 1
        _odd_prologue(z_ref, zl_ref[0], s, ts, (e0, e1, e2, e3, e4), pooled_ref, dext_ref, ec_ref, vec_ref)

        def grads(zc_gate, zd_b, zd_gate, dyc, dyd, rows_out, n, start, valid):
            sgt = _silu(zc_gate)
            dcpre = dyc * vec_ref[0:1, :] * sgt
            db = _mm(dcpre)
            dpool = jnp.concatenate([_dot_nt(db[:, g * GBLK:(g + 1) * GBLK], wp_ref[g]) for g in range(4)], axis=1)
            cnt = _pool_counts(start, n)
            q = jnp.concatenate([dpool[:, g * GBLK:(g + 1) * GBLK] / cnt[g] for g in range(4)], axis=1)
            de = dyd * zd_b * _silu(zd_gate)
            if valid is not None:
                q = jnp.where(valid, q, 0.0)
                de = jnp.where(valid, de, 0.0)
            q_ref[rows_out, :] = q
            dp_ref[rows_out, :] = dpool
            de_ref[rows_out, :] = de
            return dcpre

        def block(bi, carry):
            base = pl.multiple_of(bi * GBLK, GBLK)
            rows = pl.ds(base, GBLK)
            cg = z_ref[0, rows, W_BR:2 * W_BR]
            dyc = dy_ref[0, rows, 0:W_BR]
            dyd = dy_ref[0, rows, W_BR:2 * W_BR]
            d_b = z_ref[0, rows, 3 * W_BR:4 * W_BR]
            d_gate = z_ref[0, rows, 5 * W_BR:6 * W_BR]
            dcpre = grads(cg, d_b, d_gate, dyc, dyd, rows, GBLK, s * ts + base, None)
            pb = _mm(pooled_ref[rows, :])
            dcb = _mm(dcpre)
            cpre = jnp.concatenate([jnp.dot(pb[:, g * GBLK:(g + 1) * GBLK], wp_ref[g], preferred_element_type=F32)
                                    for g in range(4)], axis=1)
            for g in range(4):
                cs = slice(g * GBLK, (g + 1) * GBLK)
                dwp_ref[g] += _dot_tn(pb[:, cs], dcb[:, cs])
            sgt, dsgt = _silu_grad(cg)
            dvec_ref[0:1, :] += _rowsum(dyc * cpre * sgt)
            dcg = dyc * cpre * vec_ref[0:1, :] * dsgt
            sdt, dsdt = _silu_grad(d_gate)
            ec = ec_ref[rows, :]
            ddb = dyd * ec * sdt
            ddg = dyd * d_b * ec * dsdt
            dz_ref[0, rows, W_BR:2 * W_BR] = dcg.astype(dz_ref.dtype)
            dz_ref[0, rows, 3 * W_BR:4 * W_BR] = ddb.astype(dz_ref.dtype)
            dz_ref[0, rows, 5 * W_BR:6 * W_BR] = ddg.astype(dz_ref.dtype)
            dbin_ref[0:1, W_BR:2 * W_BR] += _rowsum(dcg)
            dbin_ref[0:1, 3 * W_BR:4 * W_BR] += _rowsum(ddb)
            dbin_ref[0:1, 5 * W_BR:6 * W_BR] += _rowsum(ddg)
            return carry

        lax.fori_loop(0, ts // GBLK, block, 0)
        hr = zr_ref[0]
        dyr = dyr_ref[0]
        grads(hr[:, W_BR:2 * W_BR], hr[:, 3 * W_BR:4 * W_BR], hr[:, 5 * W_BR:6 * W_BR], dyr[:, 0:W_BR], dyr[:, W_BR:2 * W_BR],
              slice(ts, ext_rows), HALO, (s + 1) * ts, has_right)

        for a, b in _row_chunks(0, ts + 24):
            f1[a:b, :] = q_ref[a:b, :] + q_ref[a + 1:b + 1, :]
        for a, b in _row_chunks(0, ts + 16):
            f2[a:b, GBLK:] = f1[a:b, GBLK:] + f1[a + 2:b + 2, GBLK:]
        for a, b in _row_chunks(0, ts + 8):
            f3[a:b, 2 * GBLK:] = f2[a:b, 2 * GBLK:] + f2[a + 4:b + 4, 2 * GBLK:]
        for a, b in _row_chunks(0, ts):
            f4[a:b, 3 * GBLK:] = f3[a:b, 3 * GBLK:] + f3[a + 8:b + 8, 3 * GBLK:]

        for a, b in _row_chunks(0, ts):
            for g, f in enumerate((f1, f2, f3, f4)):
                cs = slice(g * GBLK, (g + 1) * GBLK)
                dvg = f[a:b, cs] - dp_ref[a:b, cs]
                dz_ref[0, a:b, cs] = dvg.astype(dz_ref.dtype)
                dbin_ref[0:1, cs] += _rowsum(dvg)
            de = de_ref[a:b, :]
            ddc = vec_ref[1:2, :] * de_ref[a + 2:b + 2, :] + vec_ref[2:3, :] * de_ref[a + 1:b + 1, :] + vec_ref[3:4, :] * de
            ddh = ddc * z_ref[0, a:b, 4 * W_BR:5 * W_BR]
            ddcc = ddc * z_ref[0, a:b, 2 * W_BR:3 * W_BR]
            dz_ref[0, a:b, 2 * W_BR:3 * W_BR] = ddh.astype(dz_ref.dtype)
            dz_ref[0, a:b, 4 * W_BR:5 * W_BR] = ddcc.astype(dz_ref.dtype)
            dbin_ref[0:1, 2 * W_BR:3 * W_BR] += _rowsum(ddh)
            dbin_ref[0:1, 4 * W_BR:5 * W_BR] += _rowsum(ddcc)
            for k in range(3):
                dvec_ref[1 + k:2 + k, :] += _rowsum(de * dext_ref[HALO - 2 + k + a:HALO - 2 + k + b, :])

    full = lambda shape: pl.BlockSpec(shape, lambda b, s: (0,) * len(shape))
    acc_shapes = [(1, N_COLS), (4, GBLK, GBLK), (8, W_BR)]
    ebuf = pltpu.VMEM((ext_rows, W_BR), F32)
    tbuf = pltpu.VMEM((ts, W_BR), F32)
    return pl.pallas_call(
        body, name="odd_bwd", grid=(bsz, n_s),
        in_specs=[pl.BlockSpec((1, ts, N_COLS), lambda b, s: (b, s, 0)),
                  _halo_specs(ts, s_len, N_COLS, True), _halo_specs(ts, s_len, N_COLS, False),
                  pl.BlockSpec((1, ts, 2 * W_BR), lambda b, s: (b, s, 0)), _halo_specs(ts, s_len, 2 * W_BR, False),
                  full((4, GBLK, GBLK)), full((8, W_BR))],
        out_specs=[pl.BlockSpec((1, ts, N_COLS), lambda b, s: (b, s, 0))] + [full(sh) for sh in acc_shapes],
        out_shape=[jax.ShapeDtypeStruct((bsz, s_len, N_COLS), MM_DTYPE)] + [jax.ShapeDtypeStruct(sh, F32) for sh in acc_shapes],
        scratch_shapes=[ebuf, ebuf, ebuf, ebuf, ebuf, tbuf, ebuf, tbuf, ebuf, ebuf, ebuf, ebuf, ebuf, ebuf, tbuf],
        compiler_params=_cparams(("arbitrary", "arbitrary")),
    )(z3, z3, z3, dy3, dy3, wp, vec)


def _post_fwd(y2, x2, p_layer, w_out, wg, wple, vec, tgt=None):
    p_all, layer = p_layer
    t, d = x2.shape
    tm = min(ROW_TILE, t)
    last = tgt is not None

    def body(*refs):
        y_ref, x_ref, p_ref, wo_ref, wg_ref, wp_ref, vec_ref = refs[:7]
        xn_ref, r_ref, gate_ref = refs[7 + last:10 + last]
        r = ALPHA * x_ref[...] + jnp.dot(y_ref[...], wo_ref[...], preferred_element_type=F32) + vec_ref[0:1, :]
        r_ref[...] = r
        xh, _ = _ln_stats(r)
        h = xh * vec_ref[1:2, :] + vec_ref[2:3, :]
        gate = _sigmoid(_dot(h, wg_ref[...]) + vec_ref[3:4, :])
        gate_ref[...] = gate
        xn = h + gate * _dot(p_ref[...], wp_ref[...])
        if last:
            sq_ref = refs[11]

            @pl.when(pl.program_id(0) == 0)
            def _():
                sq_ref[...] = jnp.zeros_like(sq_ref)
            e = xn - refs[7][...]
            xn_ref[...] = e / float(d)
            sq_ref[...] += _rowsum(e * e)
        else:
            xn_ref[...] = xn

    row = lambda c: pl.BlockSpec((tm, c), lambda i: (i, 0))
    full = lambda shape: pl.BlockSpec(shape, lambda i: (0,) * len(shape))
    return pl.pallas_call(
        body, name="post_fwd_loss" if last else "post_fwd", grid=(t // tm,),
        in_specs=[row(d), row(d), pl.BlockSpec((None, tm, D_PLE), lambda i: (layer, i, 0)),
                  full((d, d)), full((d, d)), full((D_PLE, d)), full((8, d))] + [row(d)] * last,
        out_specs=[row(d), row(d), row(d)] + [full((1, d))] * last,
        out_shape=[jax.ShapeDtypeStruct((t, d), F32)] * 3 + [jax.ShapeDtypeStruct((1, d), F32)] * last,
        compiler_params=_cparams(("arbitrary",) if last else ("parallel",)),
    )(y2, x2, p_all, w_out, wg, wple, vec, *([tgt] if last else []))


def _post_bwd(dxn, r2, gate2, p_layer, y2, w_out, wg, wple, vec):
    p_all, layer = p_layer
    t, d = r2.shape
    tm = min(BWD_ROW_TILE, t)

    def body(dxn_ref, r_ref, gate_ref, p_ref, y_ref, wo_ref, wg_ref, wp_ref, vec_ref,
             dxr_ref, dy_ref, dwo_ref, dwg_ref, dwp_ref, dvec_ref):
        @pl.when(pl.program_id(0) == 0)
        def _():
            dwo_ref[...] = jnp.zeros_like(dwo_ref)
            dwg_ref[...] = jnp.zeros_like(dwg_ref)
            dwp_ref[...] = jnp.zeros_like(dwp_ref)
            dvec_ref[...] = jnp.zeros_like(dvec_ref)

        dxn = dxn_ref[...]
        gate = gate_ref[...]
        xh, rs = _ln_stats(r_ref[...])
        hb = _mm(xh * vec_ref[1:2, :] + vec_ref[2:3, :])
        pb = _mm(p_ref[...])
        pe = jnp.dot(pb, wp_ref[...], preferred_element_type=F32)
        dpre = dxn * pe * gate * (1.0 - gate)
        dpb = _mm(dpre)
        dh = dxn + _dot_nt(dpb, wg_ref[...])
        dwg_ref[...] += _dot_tn(hb, dpb)
        dwp_ref[...] += _dot_tn(pb, dxn * gate)
        dr = _ln_bwd(dh * vec_ref[1:2, :], xh, rs)
        drb = _mm(dr)
        dxr_ref[...] = ALPHA * dr
        dy_ref[...] = _dot_nt(drb, wo_ref[...])
        dwo_ref[...] += _dot_tn(y_ref[...], drb)
        dvec_ref[0:1, :] += _rowsum(dr)
        dvec_ref[1:2, :] += _rowsum(dh * xh)
        dvec_ref[2:3, :] += _rowsum(dh)
        dvec_ref[3:4, :] += _rowsum(dpre)

    row = lambda c: pl.BlockSpec((tm, c), lambda i: (i, 0))
    full = lambda shape: pl.BlockSpec(shape, lambda i: (0,) * len(shape), pipeline_mode=pl.Buffered(1))
    acc_shapes = [(d, d), (d, d), (D_PLE, d), (8, d)]
    return pl.pallas_call(
        body, name="post_bwd", grid=(t // tm,),
        in_specs=[row(d), row(d), row(d), pl.BlockSpec((None, tm, D_PLE), lambda i: (layer, i, 0)), row(d),
                  full((d, d)), full((d, d)), full((D_PLE, d)), full((8, d))],
        out_specs=[row(d), row(d)] + [full(sh) for sh in acc_shapes],
        out_shape=[jax.ShapeDtypeStruct((t, d), F32)] * 2 + [jax.ShapeDtypeStruct(sh, F32) for sh in acc_shapes],
        compiler_params=_cparams(("arbitrary",)),
    )(dxn, r2, gate2, p_all, y2, w_out, wg, wple, vec)


def _place():
    x, y, c = lax.axis_index("x"), lax.axis_index("y"), lax.axis_index("c")
    chips = [(1 - x, y), (x, 1 - y), (1 - x, 1 - y)]
    return x, y, c, chips


def _shard_of(ref, ax, k, width, lo=None, ln=None):
    idx = [slice(None)] * 3
    if lo is not None:
        idx[0] = pl.ds(lo, ln)
    idx[ax] = pl.ds(k * width, width)
    return ref.at[tuple(idx)]


def _remote(src, dst, ssem, rsem, dev):
    return pltpu.make_async_remote_copy(src_ref=src, dst_ref=dst, send_sem=ssem, recv_sem=rsem, device_id=dev, device_id_type=MESH)


def _place_shard(w, ax, chip, dtype=MM_DTYPE, layer=None):
    l_dim, a_dim, b_dim = w.shape
    tr = min(256, a_dim)
    per = a_dim // tr
    first = 0
    if layer is not None:
        l_dim, first = 1, layer
    shape = [l_dim, a_dim, b_dim]
    shape[ax] *= 4
    if ax == 2:
        out_spec = pl.BlockSpec((1, tr, b_dim), lambda l, i, k: (l, i, k[0]))
    else:
        out_spec = pl.BlockSpec((1, tr, b_dim), lambda l, i, k: (l, k[0] * per + i, 0))

    def body(k_ref, w_ref, o_ref):
        o_ref[...] = w_ref[...].astype(o_ref.dtype)

    return pl.pallas_call(
        body, name="place_shard",
        grid_spec=pltpu.PrefetchScalarGridSpec(
            num_scalar_prefetch=1, grid=(l_dim, per),
            in_specs=[pl.BlockSpec((1, tr, b_dim), lambda l, i, k: (first + l, i, 0))], out_specs=out_spec),
        out_shape=jax.ShapeDtypeStruct(tuple(shape), dtype),
        compiler_params=_cparams(("parallel", "parallel")),
    )(chip, w)


def _shard_copies(src_refs, dst_refs, axes, sems):
    x, y, c, chips = _place()
    j = 2 * x + y
    cps = []
    for a, (s_ref, d_ref) in enumerate(zip(src_refs, dst_refs)):
        w = d_ref.shape[axes[a]] // 4
        for q, (qx, qy) in enumerate(chips):
            ssem, rsem = sems[3 * a + q]
            cps.append(_remote(_shard_of(s_ref, axes[a], j, w), _shard_of(d_ref, axes[a], j, w), ssem, rsem, (qx, qy, c)))
    return cps


def _chip_handshake():
    x, y, c, chips = _place()
    barrier = pltpu.get_barrier_semaphore()
    for qx, qy in chips:
        pl.semaphore_signal(barrier, inc=1, device_id=(qx, qy, c), device_id_type=MESH)
    pl.semaphore_wait(barrier, 3)


def _gather_async(name, collective_id, fulls, axes):
    n = len(fulls)
    refs = [jax.new_ref(f, memory_space=pltpu.MemorySpace.HBM) for f in fulls]

    @pl.kernel(mesh=plsc.ScalarSubcoreMesh(axis_name="seq", num_cores=1), name=name,
               scratch_types=(pltpu.SemaphoreType.DMA,) * (6 * n),
               compiler_params=pltpu.CompilerParams(collective_id=collective_id))
    def launch(*sems):
        _chip_handshake()
        cps = _shard_copies(refs, refs, axes, [(sems[2 * k], sems[2 * k + 1]) for k in range(3 * n)])
        for cp in cps:
            cp.start()
        for cp in cps:
            cp.wait()

    launch()
    return refs


def _scatter_async(name, collective_id, grads, axes):
    n = len(grads)
    outs = []
    for g, ax in zip(grads, axes):
        sh = list(g.shape)
        sh[ax] //= 4
        outs.append(jax.ShapeDtypeStruct((3,) + tuple(sh), g.dtype))

    def body(*refs):
        srcs, lands, sems = refs[:n], refs[n:2 * n], refs[2 * n:]
        _chip_handshake()
        x, y, c, chips = _place()
        cps = []
        for a in range(n):
            w = srcs[a].shape[axes[a]] // 4
            for q, (qx, qy) in enumerate(chips):
                k = 3 * a + q
                cps.append(_remote(_shard_of(srcs[a], axes[a], 2 * qx + qy, w), lands[a].at[q], sems[2 * k], sems[2 * k + 1],
                                   (qx, qy, c)))
        for cp in cps:
            cp.start()
        for cp in cps:
            cp.wait()

    return pl.kernel(body, out_type=outs, mesh=plsc.ScalarSubcoreMesh(axis_name="seq", num_cores=1), name=name,
                     scratch_types=(pltpu.SemaphoreType.DMA,) * (6 * n),
                     compiler_params=pltpu.CompilerParams(collective_id=collective_id))(*grads)


def _pair_swap_async(name, collective_id, sums):
    n = len(sums)

    def body(*refs):
        g_refs, o_refs, sems = refs[:n], refs[n:2 * n], refs[2 * n:]
        x, y, c, _ = _place()
        barrier = pltpu.get_barrier_semaphore()
        pl.semaphore_signal(barrier, inc=1, device_id=(x, y, 1 - c), device_id_type=MESH)
        pl.semaphore_wait(barrier, 1)
        cps = [_remote(g_refs[a], o_refs[a], sems[2 * a], sems[2 * a + 1], (x, y, 1 - c)) for a in range(n)]
        for cp in cps:
            cp.start()
        for cp in cps:
            cp.wait()

    return pl.kernel(body, out_type=[jax.ShapeDtypeStruct(g.shape, g.dtype) for g in sums],
                     mesh=plsc.ScalarSubcoreMesh(axis_name="seq", num_cores=1), name=name,
                     scratch_types=(pltpu.SemaphoreType.DMA,) * (2 * n),
                     compiler_params=pltpu.CompilerParams(collective_id=collective_id))(*sums)


def _pair_exchange(grads, small):
    n = len(grads)
    outs = [jax.ShapeDtypeStruct((g.shape[0] // 2,) + g.shape[1:], g.dtype) for g in grads]
    outs.append(jax.ShapeDtypeStruct((small.shape[0] // 2, small.shape[1]), small.dtype))

    def body(*refs):
        g_refs, o_refs = refs[:n + 1], refs[n + 1:2 * n + 2]
        ssem, rsem = refs[2 * n + 2:]
        x, y, c, _ = _place()
        cps = []
        for a in range(n + 1):
            lh = g_refs[a].shape[0] // 2
            cp = _remote(g_refs[a].at[pl.ds((1 - c) * lh, lh)], o_refs[a], ssem.at[a], rsem.at[a], (x, y, 1 - c))
            cp.start()
            cps.append(cp)
        for cp in cps:
            cp.wait()

    return pl.pallas_call(
        body, name="pair_exchange", in_specs=[ANY] * (n + 1), out_specs=[ANY] * (n + 1), out_shape=outs,
        scratch_shapes=[pltpu.SemaphoreType.DMA((n + 1,)), pltpu.SemaphoreType.DMA((n + 1,))],
        compiler_params=pltpu.CompilerParams(has_side_effects=True),
    )(*grads, small)


def _chip_scatter(sums, axes, small):
    n = len(sums)
    outs = []
    for g, ax in zip(sums, axes):
        sh = list(g.shape)
        sh[ax] //= 4
        outs.append(jax.ShapeDtypeStruct((3,) + tuple(sh), g.dtype))
    rq = small.shape[0] // 4
    outs.append(jax.ShapeDtypeStruct((3, rq, small.shape[1]), small.dtype))

    def body(*refs):
        g_refs, o_refs = refs[:n + 1], refs[n + 1:2 * n + 2]
        ssem, rsem = refs[2 * n + 2:]
        x, y, c, chips = _place()
        cps = []
        for a in range(n + 1):
            for q, (qx, qy) in enumerate(chips):
                k = 2 * qx + qy
                if a < n:
                    src = _shard_of(g_refs[a], axes[a], k, g_refs[a].shape[axes[a]] // 4)
                else:
                    src = g_refs[a].at[pl.ds(k * rq, rq)]
                cp = _remote(src, o_refs[a].at[q], ssem.at[a * 3 + q], rsem.at[a * 3 + q], (qx, qy, c))
                cp.start()
                cps.append(cp)
        for cp in cps:
            cp.wait()

    return pl.pallas_call(
        body, name="chip_scatter", in_specs=[ANY] * (n + 1), out_specs=[ANY] * (n + 1), out_shape=outs,
        scratch_shapes=[pltpu.SemaphoreType.DMA((3 * n + 3,)), pltpu.SemaphoreType.DMA((3 * n + 3,))],
        compiler_params=pltpu.CompilerParams(has_side_effects=True),
    )(*sums, small)


def _final_exchange(reds, small):
    n = len(reds)
    flips = [(fx, fy, fc) for fx in (0, 1) for fy in (0, 1) for fc in (0, 1)][1:]

    def body(*refs):
        g_refs, o_refs = refs[:n + 1], refs[n + 1:2 * n + 2]
        ssem, rsem = refs[2 * n + 2:]
        x, y, c, _ = _place()
        cps = []
        for a in range(n):
            lh = g_refs[a].shape[0] // 2
            cp = _remote(g_refs[a].at[pl.ds(c * lh, lh)], o_refs[a].at[pl.ds(c * lh, lh)], ssem.at[a], rsem.at[a], (x, y, 1 - c))
            cp.start()
            cps.append(cp)
        mine = 4 * c + 2 * x + y
        for f, (fx, fy, fc) in enumerate(flips):
            cp = _remote(g_refs[n].at[mine], o_refs[n].at[mine], ssem.at[n + f], rsem.at[n + f], (x ^ fx, y ^ fy, c ^ fc))
            cp.start()
            cps.append(cp)
        for cp in cps:
            cp.wait()

    return pl.pallas_call(
        body, name="final_exchange", in_specs=[ANY] * (n + 1), out_specs=[ANY] * (n + 1),
        out_shape=[jax.ShapeDtypeStruct(g.shape, g.dtype) for g in reds] + [jax.ShapeDtypeStruct(small.shape, small.dtype)],
        input_output_aliases={a: a for a in range(n + 1)},
        scratch_shapes=[pltpu.SemaphoreType.DMA((n + 7,)), pltpu.SemaphoreType.DMA((n + 7,))],
        compiler_params=pltpu.CompilerParams(has_side_effects=True),
    )(*reds, small)


def _small_allreduce(small):
    r = small.shape[0]
    rh, rq = r // 2, r // 8
    flips = [(fx, fy, fc) for fx in (0, 1) for fy in (0, 1) for fc in (0, 1)][1:]

    def body(g_ref, out_ref, pair_ref, chip_ref, s1_ref, ssem, rsem):
        x, y, c, chips = _place()
        cp = _remote(g_ref.at[pl.ds((1 - c) * rh, rh)], pair_ref, ssem.at[0], rsem.at[0], (x, y, 1 - c))
        cp.start()
        cp.wait()
        s1_ref[...] = g_ref[pl.ds(pl.multiple_of(c * rh, 8), rh), :] + pair_ref[...]
        cps = [_remote(s1_ref.at[pl.ds((2 * qx + qy) * rq, rq)], chip_ref.at[q], ssem.at[1 + q], rsem.at[1 + q], (qx, qy, c))
               for q, (qx, qy) in enumerate(chips)]
        for cp in cps:
            cp.start()
        for cp in cps:
            cp.wait()
        mine = out_ref.at[pl.ds(pl.multiple_of((4 * c + 2 * x + y) * rq, 8), rq)]
        mine[...] = ((s1_ref[pl.ds(pl.multiple_of((2 * x + y) * rq, 8), rq), :] + chip_ref[0]) + chip_ref[1]) + chip_ref[2]
        cps = [_remote(mine, mine, ssem.at[4 + f], rsem.at[4 + f], (x ^ fx, y ^ fy, c ^ fc)) for f, (fx, fy, fc) in enumerate(flips)]
        for cp in cps:
            cp.start()
        for cp in cps:
            cp.wait()

    vm = pl.BlockSpec(memory_space=pltpu.VMEM)
    return pl.pallas_call(
        body, name="small_allreduce", in_specs=[vm], out_specs=vm, out_shape=jax.ShapeDtypeStruct(small.shape, F32),
        scratch_shapes=[pltpu.VMEM((rh, 128), F32), pltpu.VMEM((3, rq, 128), F32), pltpu.VMEM((rh, 128), F32),
                        pltpu.SemaphoreType.DMA((11,)), pltpu.SemaphoreType.DMA((11,))],
        compiler_params=pltpu.CompilerParams(has_side_effects=True, vmem_limit_bytes=VMEM_LIMIT),
    )(small)


def _to_wire(g):
    _, a_dim, b_dim = g.shape
    tr = min(256, a_dim)

    def body(g_ref, o_ref, done_ref):
        o_ref[...] = g_ref[...].astype(o_ref.dtype)
        done_ref[...] = jnp.zeros_like(done_ref)

    blk = pl.BlockSpec((1, tr, b_dim), lambda i: (0, i, 0))
    return pl.pallas_call(
        body, name="to_wire", grid=(a_dim // tr,), in_specs=[blk], out_specs=[blk, pl.BlockSpec((8, 128), lambda i: (0, 0))],
        out_shape=[jax.ShapeDtypeStruct(g.shape, WIRE_DTYPE), jax.ShapeDtypeStruct((8, 128), F32)],
        compiler_params=_cparams(("arbitrary",)),
    )(g)


def _chip_sum(owns, gots, ax, chip):
    n_layers = len(owns)
    _, _, a_dim, b_dim = gots[0].shape
    tr = min(256, a_dim)
    per = a_dim // tr

    def own_spec(layer):
        if ax == 2:
            return pl.BlockSpec((1, tr, b_dim), lambda l, i, k: (0, jnp.where(l == layer, i, 0), k[0]))
        return pl.BlockSpec((1, tr, b_dim), lambda l, i, k: (0, k[0] * per + jnp.where(l == layer, i, 0), 0))

    def got_spec(layer):
        return pl.BlockSpec((3, 1, tr, b_dim), lambda l, i, k: (0, 0, jnp.where(l == layer, i, 0), 0))

    def body(k_ref, *refs):
        s_ref = refs[-1]
        for layer in range(n_layers):
            @pl.when(pl.program_id(0) == layer)
            def _(own_ref=refs[layer], got_ref=refs[n_layers + layer]):
                s_ref[...] = ((own_ref[...].astype(F32) + got_ref[0].astype(F32)) + got_ref[1].astype(F32)) + got_ref[2].astype(F32)

    return pl.pallas_call(
        body, name="chip_sum",
        grid_spec=pltpu.PrefetchScalarGridSpec(
            num_scalar_prefetch=1, grid=(n_layers, per),
            in_specs=[own_spec(l) for l in range(n_layers)] + [got_spec(l) for l in range(n_layers)],
            out_specs=pl.BlockSpec((1, tr, b_dim), lambda l, i, k: (l, i, 0))),
        out_shape=jax.ShapeDtypeStruct((n_layers, a_dim, b_dim), F32),
        compiler_params=_cparams(("arbitrary", "arbitrary")),
    )(chip, *owns, *gots)


def _small_pair_sum(small, got, half):
    rh = small.shape[0] // 2

    def body(h_ref, g_ref, o_ref, s_ref):
        s_ref[...] = g_ref[...] + o_ref[...]

    return pl.pallas_call(
        body, name="small_pair_sum",
        grid_spec=pltpu.PrefetchScalarGridSpec(
            num_scalar_prefetch=1, grid=(1,),
            in_specs=[pl.BlockSpec((rh, 128), lambda i, h: (h[0], 0)), pl.BlockSpec((rh, 128), lambda i, h: (0, 0))],
            out_specs=pl.BlockSpec((rh, 128), lambda i, h: (0, 0))),
        out_shape=jax.ShapeDtypeStruct((rh, 128), F32),
    )(half, small, got)


def _small_chip_sum(s1, got, pos):
    rq = got.shape[1]

    def body(k_ref, own_ref, got_ref, s_ref):
        s_ref[0] = ((own_ref[...] + got_ref[0]) + got_ref[1]) + got_ref[2]

    return pl.pallas_call(
        body, name="small_chip_sum",
        grid_spec=pltpu.PrefetchScalarGridSpec(
            num_scalar_prefetch=1, grid=(1,),
            in_specs=[pl.BlockSpec((rq, 128), lambda i, k: (k[0], 0)), pl.BlockSpec((3, rq, 128), lambda i, k: (0, 0, 0))],
            out_specs=pl.BlockSpec((1, rq, 128), lambda i, k: (4 * k[1] + k[0], 0, 0))),
        out_shape=jax.ShapeDtypeStruct((8, rq, 128), F32),
    )(pos, s1, got)


def _adam_math(w, g, m, v):
    m = ADAM_B1 * m + (1.0 - ADAM_B1) * g
    v = ADAM_B2 * v + (1.0 - ADAM_B2) * (g * g)
    m_hat = m / (1.0 - ADAM_B1 ** ADAM_STEP)
    v_hat = v / (1.0 - ADAM_B2 ** ADAM_STEP)
    return -ADAM_LR * (m_hat / (jnp.sqrt(v_hat) + ADAM_EPS) + ADAM_WD * w), m, v


def _adamw_big(w, g_mine, g_other, m, v):
    l_dim, a_dim, b_dim = w.shape
    tr = min(256, a_dim)

    def body(w_ref, g1_ref, g2_ref, m_ref, v_ref, g_ref, d_ref, nm_ref, nv_ref):
        g = g1_ref[...] + g2_ref[...]
        g_ref[...] = g
        d_ref[...], nm_ref[...], nv_ref[...] = _adam_math(w_ref[...], g, m_ref[...], v_ref[...])

    blk = pl.BlockSpec((1, tr, b_dim), lambda l, i: (l, i, 0))
    return pl.pallas_call(
        body, name="adamw_big", grid=(l_dim, a_dim // tr), in_specs=[blk] * 5, out_specs=[blk] * 4,
        out_shape=[jax.ShapeDtypeStruct(w.shape, F32)] * 4,
        compiler_params=_cparams(("parallel", "parallel")),
    )(w, g_mine, g_other, m, v)


def _adamw_small(ws, gs, ms, vs):
    n = len(ws)

    def body(*refs):
        for i in range(n):
            w_ref, g_ref, m_ref, v_ref = refs[i], refs[n + i], refs[2 * n + i], refs[3 * n + i]
            d_ref, nm_ref, nv_ref = refs[4 * n + i], refs[5 * n + i], refs[6 * n + i]
            d_ref[...], nm_ref[...], nv_ref[...] = _adam_math(w_ref[...], g_ref[...], m_ref[...], v_ref[...])

    shapes = [jax.ShapeDtypeStruct(w.shape, F32) for w in ws]
    outs = pl.pallas_call(body, name="adamw_small", out_shape=shapes * 3,
                          compiler_params=_cparams())(*ws, *gs, *ms, *vs)
    return outs[:n], outs[n:2 * n], outs[2 * n:]


def _pack(arrs, row_mult):
    parts = []
    for a in arrs:
        flat = a.reshape(-1)
        pad = (-flat.shape[0]) % 1024
        parts.append(jnp.pad(flat, (0, pad)).reshape(-1, 128))
    buf = jnp.concatenate(parts, axis=0)
    pad = (-buf.shape[0]) % row_mult
    return jnp.pad(buf, ((0, pad), (0, 0)))


def _unpack(buf, shapes):
    out, row = [], 0
    for sh in shapes:
        n = math.prod(sh)
        rows = -(-n // 1024) * 8
        out.append(buf[row:row + rows].reshape(-1)[:n].reshape(sh))
        row += rows
    return out


_NAMES = ['w_in_e', 'b_in_e', 'conv_a_w', 'conv_a_b', 'ln_a_g', 'ln_a_b', 'ln_v_g', 'ln_v_b', 'w_s', 'b_s', 'w_out_e', 'b_out_e',
          'w_in_o', 'b_in_o', 'w_pool', 'pool_scale', 'conv_d_w', 'w_out_o', 'b_out_o', 'ln_g', 'ln_b', 'w_ple', 'w_ple_gate',
          'b_ple_gate']
_BIG = ['w_in_e', 'w_out_e', 'w_in_o', 'w_out_o', 'w_ple', 'w_ple_gate']
_BIG_AXES = [2, 1, 2, 1, 2, 1]
_SMALL_SHARDED = ['conv_a_w', 'b_in_o', 'pool_scale', 'conv_d_w', 'b_out_o']


def kernel(x, p, w_in_e, b_in_e, conv_a_w, conv_a_b, ln_a_g, ln_a_b, ln_v_g, ln_v_b, w_s, b_s, w_out_e, b_out_e, w_in_o, b_in_o, w_pool, pool_scale, conv_d_w, w_out_o, b_out_o, ln_g, ln_b, w_ple, w_ple_gate, b_ple_gate, loss_target, m_w_in_e, m_b_in_e, m_conv_a_w, m_conv_a_b, m_ln_a_g, m_ln_a_b, m_ln_v_g, m_ln_v_b, m_w_s, m_b_s, m_w_out_e, m_b_out_e, m_w_in_o, m_b_in_o, m_w_pool, m_pool_scale, m_conv_d_w, m_w_out_o, m_b_out_o, m_ln_g, m_ln_b, m_w_ple, m_w_ple_gate, m_b_ple_gate, v_w_in_e, v_b_in_e, v_conv_a_w, v_conv_a_b, v_ln_a_g, v_ln_a_b, v_ln_v_g, v_ln_v_b, v_w_s, v_b_s, v_w_out_e, v_b_out_e, v_w_in_o, v_b_in_o, v_w_pool, v_pool_scale, v_conv_d_w, v_w_out_o, v_b_out_o, v_ln_g, v_ln_b, v_w_ple, v_w_ple_gate, v_b_ple_gate):
    args = locals()
    wts = {n: args[n] for n in _NAMES}
    mom = {n: args["m_" + n] for n in _NAMES}
    var = {n: args["v_" + n] for n in _NAMES}
    bsz, s_len, d = x.shape
    t = bsz * s_len
    cx, cy, cc = lax.axis_index("x"), lax.axis_index("y"), lax.axis_index("c")
    chip = (2 * cx + cy).astype(jnp.int32).reshape(1)
    half = cc.astype(jnp.int32).reshape(1)
    pos = jnp.concatenate([chip, half])

    def placed(name, ax, layer):
        return _place_shard(wts[name], ax, chip, layer=layer)

    sv = _pack([wts[n] for n in _SMALL_SHARDED], 8)
    first_refs = _gather_async("gather_first", DEPTH, [placed('w_in_e', 2, 0), _place_shard(sv[None], 1, chip, F32)], [2, 1])
    layer_refs = []
    for i in range(DEPTH):
        sfx = '_e' if i % 2 == 0 else '_o'
        items = [('w_out' + sfx, 1, i // 2), ('w_ple_gate', 1, i), ('w_ple', 2, i)] + ([('w_in' + sfx, 2, i // 2)] if i else [])
        layer_refs.append(_gather_async("gather_layer%d" % i, i, [placed(*it) for it in items], [it[1] for it in items]))
    fw = {}
    w_in_first = first_refs[0][...]
    sv_all = first_refs[1][...].reshape((4,) + sv.shape)
    small_parts = [_unpack(sv_all[k], [wts[n].shape for n in _SMALL_SHARDED]) for k in range(4)]
    for i, n in enumerate(_SMALL_SHARDED):
        fw[n] = jnp.concatenate([small_parts[k][i] for k in range(4)], axis=-1)
    for n in _NAMES:
        fw.setdefault(n, wts[n])

    def row8(rows, width):
        rows = [r.reshape(1, width) for r in rows]
        return jnp.concatenate(rows + [jnp.zeros((8 - len(rows), width), F32)], axis=0)

    x2 = x.reshape(t, d)
    saved = []
    for i in range(DEPTH):
        j = i // 2
        even = i % 2 == 0
        b_in, b_out = (fw['b_in_e'], fw['b_out_e']) if even else (fw['b_in_o'], fw['b_out_o'])
        w_in = layer_refs[i][3][...][0] if i else w_in_first[0]
        z = _in_proj(x2, w_in, b_in[j].reshape(1, N_COLS))
        z3 = z.reshape(bsz, s_len, N_COLS)
        if even:
            cw = jnp.concatenate([fw['conv_a_w'][j], jnp.zeros((1, W_BR), F32)], axis=0)
            mvec = row8([fw['conv_a_b'][j], fw['ln_a_g'][j], fw['ln_a_b'][j], fw['ln_v_g'][j], fw['ln_v_b'][j]], W_BR)
            bsf = jnp.repeat(fw['b_s'][j].T, W_BR // 8, axis=1)
            y3, a13 = _even_fwd(z3, cw, mvec, fw['w_s'][j], bsf)
            mix = (a13, cw, mvec, fw['w_s'][j], jnp.swapaxes(fw['w_s'][j], 1, 2), bsf)
        else:
            mvec = row8([fw['pool_scale'][j]] + [fw['conv_d_w'][j][k] for k in range(3)], W_BR)
            mix = (fw['w_pool'][j].astype(MM_DTYPE), mvec)
            y3 = _odd_fwd(z3, mix[0], mvec)
        pvec = row8([b_out[j], fw['ln_g'][i], fw['ln_b'][i], fw['b_ple_gate'][i]], d)
        post_w = (layer_refs[i][0][...][0], layer_refs[i][1][...][0], layer_refs[i][2][...][0], pvec)
        p2 = (p.reshape(DEPTH, t, D_PLE), i)
        y2 = y3.reshape(t, 2 * W_BR)
        if i < DEPTH - 1:
            xn, r2, gate2 = _post_fwd(y2, x2, p2, *post_w)
        else:
            dx, r2, gate2, sq = _post_fwd(y2, x2, p2, *post_w, tgt=loss_target.reshape(t, d))
        saved.append((x2, z3, y2, r2, gate2, p2, w_in, mix, post_w))
        x2 = xn

    gr = {n: [None] * wts[n].shape[0] for n in _NAMES}
    for i in reversed(range(DEPTH)):
        j = i // 2
        even = i % 2 == 0
        x_in, z3, y2, r2, gate2, p2, w_in, mix, post_w = saved[i]
        dxr, dy, dwo, dwg, dwp, dpv = _post_bwd(dx, r2, gate2, p2, y2, *post_w)
        dy3 = dy.reshape(bsz, s_len, 2 * W_BR)
        sfx = '_e' if even else '_o'
        gr['w_out' + sfx][j], gr['b_out' + sfx][j] = dwo, dpv[0]
        gr['w_ple_gate'][i], gr['w_ple'][i] = dwg, dwp
        gr['ln_g'][i], gr['ln_b'][i], gr['b_ple_gate'][i] = dpv[1], dpv[2], dpv[3]
        if even:
            dz3, dbin, dcw, dmv, dws, dbsf = _even_bwd(z3, dy3, *mix)
            gr['conv_a_w'][j], gr['conv_a_b'][j] = dcw[:KA], dmv[0]
            gr['ln_a_g'][j], gr['ln_a_b'][j], gr['ln_v_g'][j], gr['ln_v_b'][j] = dmv[1], dmv[2], dmv[3], dmv[4]
            gr['w_s'][j] = dws
            gr['b_s'][j] = jnp.sum(dbsf.reshape(GBLK, 8, W_BR // 8), axis=2).T
        else:
            dz3, dbin, dwpool, dmv = _odd_bwd(z3, dy3, *mix)
            gr['w_pool'][j], gr['pool_scale'][j], gr['conv_d_w'][j] = dwpool, dmv[0], dmv[1:4]
        gr['b_in' + sfx][j] = dbin[0]
        dz2 = dz3.reshape(t, N_COLS)
        dwi, dw_done = _in_proj_bwd_dw(x_in, dz2)
        post_items = [('w_out' + sfx, j, dwo[None], 1), ('w_ple_gate', i, dwg[None], 1), ('w_ple', i, dwp[None], 2)]
        in_item = ('w_in' + sfx, j, dwi[None], 2)
        batches = [post_items + [in_item]] if i else [post_items, [in_item]]
        for bi, items in enumerate(batches):
            sent = [it[2] for it in items]
            if i == 0 and bi == 1:
                wire, dw_done = _to_wire(sent[0])
                sent = [wire]
            lands = _scatter_async("scatter_layer%d_%d" % (i, bi), DEPTH + 1 + 2 * i + bi, sent, [it[3] for it in items])
            for it, land in zip(items, lands):
                gr[it[0]][it[1]] = (it[2], land, it[3])
        dx = _in_proj_bwd_dx(dxr, dz2, w_in, dw_done)
    grad_x = dx.reshape(bsz, s_len, d)

    small_names = [n for n in _NAMES if n not in _BIG]
    g_small_full = [jnp.stack(gr[n]) for n in small_names] + [sq]
    small_all = _small_allreduce(_pack(g_small_full, 64))
    *g_small, sq_all = _unpack(small_all, [g.shape for g in g_small_full])
    loss = 0.5 * jnp.sum(sq_all) / d
    big_order = _BIG[1:] + _BIG[:1]
    sums = [_chip_sum([g[0] for g in gr[n]], [g[1] for g in gr[n]], gr[n][0][2], chip) for n in big_order]
    others = (list(_pair_swap_async("pair_swap_a", 3 * DEPTH + 1, sums[:-1]))
              + list(_pair_swap_async("pair_swap_b", 3 * DEPTH + 2, sums[-1:])))
    grads = {}
    for n, g in zip(small_names, g_small):
        if n in _SMALL_SHARDED:
            w = wts[n].shape[-1]
            g = lax.dynamic_slice_in_dim(g, (2 * cx + cy) * w, w, axis=g.ndim - 1)
        grads[n] = g

    delta, new_m, new_v = {}, {}, {}
    ds, ms, vs = _adamw_small([wts[n] for n in small_names], [grads[n] for n in small_names],
                              [mom[n] for n in small_names], [var[n] for n in small_names])
    for n, a, b, c_ in zip(small_names, ds, ms, vs):
        delta[n], new_m[n], new_v[n] = a, b, c_
    for n, mine, other in zip(big_order, sums, others):
        grads[n], delta[n], new_m[n], new_v[n] = _adamw_big(wts[n], mine, other, mom[n], var[n])

    return (loss, grad_x, *[grads[n] for n in _NAMES], *[delta[n] for n in _NAMES],
            *[new_m[n] for n in _NAMES], *[new_v[n] for n in _NAMES])
```

```python
import functools
import math

import jax
import jax.numpy as jnp
from jax import lax
from jax.experimental import pallas as pl
from jax.experimental.pallas import tpu as pltpu
from jax.experimental.pallas import tpu_sc as plsc

F32 = jnp.float32
MM_DTYPE = jnp.bfloat16
WIRE_DTYPE = jnp.bfloat16
SEQ_TILE = 512
ROW_TILE = 512
BWD_ROW_TILE = 512
HALO = 32
CONV_CHUNK = 32
ROW_CHUNK = 64
GBLK = 128
VMEM_LIMIT = 56 * 1024 * 1024

D_MODEL = 1024
W_BR = 512
N_COLS = 6 * W_BR
D_PLE = 256
KA = 31
DEPTH = 4
POOL_WINDOWS = (2, 4, 8, 16)
ALPHA = (2.0 * DEPTH) ** 0.25
LN_EPS = 1e-5
GELU_C = math.sqrt(2.0 / math.pi)

ADAM_LR, ADAM_B1, ADAM_B2, ADAM_EPS, ADAM_WD, ADAM_STEP = 0.001, 0.9, 0.999, 1e-08, 0.01, 10

MESH = pl.DeviceIdType.MESH
ANY = pl.BlockSpec(memory_space=pl.ANY)


def _cparams(sem=None):
    return pltpu.CompilerParams(dimension_semantics=sem, vmem_limit_bytes=VMEM_LIMIT)


def _sigmoid(x):
    return 1.0 / (1.0 + jnp.exp(-x))


def _silu(x):
    return x * _sigmoid(x)


def _silu_grad(x):
    s = _sigmoid(x)
    return x * s, s * (1.0 + x * (1.0 - s))


def _gelu(x):
    return 0.5 * x * (1.0 + jnp.tanh(GELU_C * (x + 0.044715 * (x * x * x))))


def _gelu_grad(x):
    x2 = x * x
    th = jnp.tanh(GELU_C * (x + 0.044715 * (x * x2)))
    return 0.5 * x * (1.0 + th), 0.5 * (1.0 + th) + 0.5 * x * (1.0 - th * th) * (GELU_C * (1.0 + 3.0 * 0.044715 * x2))


def _ln_stats(x):
    mu = jnp.mean(x, axis=-1, keepdims=True)
    d = x - mu
    var = jnp.mean(d * d, axis=-1, keepdims=True)
    rs = lax.rsqrt(var + LN_EPS)
    return d * rs, rs


def _ln_bwd(dxh, xh, rs):
    return rs * (dxh - jnp.mean(dxh, axis=-1, keepdims=True) - xh * jnp.mean(dxh * xh, axis=-1, keepdims=True))


def _mm(a):
    return a.astype(MM_DTYPE)


def _dot(a, b):
    return jnp.dot(_mm(a), _mm(b), preferred_element_type=F32)


def _dot_nt(a, b):
    return lax.dot_general(_mm(a), _mm(b), (((1,), (1,)), ((), ())), preferred_element_type=F32)


def _dot_tn(a, b):
    return lax.dot_general(_mm(a), _mm(b), (((0,), (0,)), ((), ())), preferred_element_type=F32)


def _rowsum(x):
    return jnp.sum(x, axis=0, keepdims=True)


def _in_proj(x2, w, b):
    t, d = x2.shape
    n = w.shape[1]
    tm = min(ROW_TILE, t)
    nc = 768

    def body(x_ref, w_ref, b_ref, z_ref):
        xb = _mm(x_ref[...])
        for j in range(n // nc):
            cs = slice(j * nc, (j + 1) * nc)
            z_ref[:, cs] = jnp.dot(xb, w_ref[:, cs], preferred_element_type=F32) + b_ref[:, cs]

    return pl.pallas_call(
        body, name="in_proj", grid=(t // tm,),
        in_specs=[pl.BlockSpec((tm, d), lambda i: (i, 0)), pl.BlockSpec((d, n), lambda i: (0, 0)),
                  pl.BlockSpec((1, n), lambda i: (0, 0))],
        out_specs=pl.BlockSpec((tm, n), lambda i: (i, 0)),
        out_shape=jax.ShapeDtypeStruct((t, n), F32),
        compiler_params=_cparams(("parallel",)),
    )(x2, w, b)


def _in_proj_bwd_dx(dxr, dz, w, after):
    t, d = dxr.shape
    n = w.shape[1]
    tm = min(ROW_TILE, t)

    def body(dxr_ref, dz_ref, w_ref, after_ref, dx_ref):
        dx_ref[...] = dxr_ref[...] + _dot_nt(dz_ref[...], w_ref[...])

    return pl.pallas_call(
        body, name="in_proj_bwd_dx", grid=(t // tm,),
        in_specs=[pl.BlockSpec((tm, d), lambda i: (i, 0)), pl.BlockSpec((tm, n), lambda i: (i, 0)),
                  pl.BlockSpec((d, n), lambda i: (0, 0)), pl.BlockSpec((8, 128), lambda i: (0, 0))],
        out_specs=pl.BlockSpec((tm, d), lambda i: (i, 0)),
        out_shape=jax.ShapeDtypeStruct((t, d), F32),
        compiler_params=_cparams(("parallel",)),
    )(dxr, dz, w, after)


def _in_proj_bwd_dw(x2, dz):
    t, d = x2.shape
    n = dz.shape[1]
    tm = min(ROW_TILE, t)
    nc = 768

    def body(x_ref, dz_ref, dw_ref, done_ref):
        @pl.when(pl.program_id(0) == 0)
        def _():
            dw_ref[...] = jnp.zeros_like(dw_ref)
            done_ref[...] = jnp.zeros_like(done_ref)
        xb = _mm(x_ref[...])
        for j in range(n // nc):
            cs = slice(j * nc, (j + 1) * nc)
            dw_ref[:, cs] += _dot_tn(xb, dz_ref[:, cs])

    return pl.pallas_call(
        body, name="in_proj_bwd_dw", grid=(t // tm,),
        in_specs=[pl.BlockSpec((tm, d), lambda i: (i, 0)), pl.BlockSpec((tm, n), lambda i: (i, 0))],
        out_specs=[pl.BlockSpec((d, n), lambda i: (0, 0)), pl.BlockSpec((8, 128), lambda i: (0, 0))],
        out_shape=[jax.ShapeDtypeStruct((d, n), F32), jax.ShapeDtypeStruct((8, 128), F32)],
        compiler_params=_cparams(("arbitrary",)),
    )(x2, dz)


def _halo_specs(ts, s_len, cols, left=True):
    per = ts // HALO
    last = s_len // HALO - 1
    if left:
        return pl.BlockSpec((1, HALO, cols), lambda b, s: (b, jnp.maximum(s * per - 1, 0), 0))
    return pl.BlockSpec((1, HALO, cols), lambda b, s: (b, jnp.minimum((s + 1) * per, last), 0))


def _build_shifts(src_ref, sh_ref, rows):
    for r in range(1, 8):
        sh_ref[r - 1, 0:rows, :] = src_ref[r:r + rows, :]


def _shifted(src_ref, sh_ref, r, start, n):
    if r == 0:
        return src_ref[pl.ds(start, n), :]
    return sh_ref[r - 1, pl.ds(start, n), :]


def _tril_masks():
    ri = lax.broadcasted_iota(jnp.int32, (GBLK, GBLK), 0)
    ci = lax.broadcasted_iota(jnp.int32, (GBLK, GBLK), 1)
    return ri >= ci, ci >= ri


def _spatial(w_ref, keep, vb):
    lane = lax.broadcasted_iota(jnp.int32, (GBLK, GBLK), 1)
    outs = []
    for p in range(4):
        xs = vb[:, p * GBLK:(p + 1) * GBLK]
        r0 = jnp.dot(_mm(jnp.where(keep, w_ref[2 * p], 0.0)), xs, preferred_element_type=F32)
        r1 = jnp.dot(_mm(jnp.where(keep, w_ref[2 * p + 1], 0.0)), xs, preferred_element_type=F32)
        outs.append(jnp.where(lane < 64, r0, r1))
    return jnp.concatenate(outs, axis=1)


def _even_fwd(z3, cw, vec, ws, bsf):
    bsz, s_len, _ = z3.shape
    ts = min(SEQ_TILE, s_len)
    ext_rows = ts + HALO

    def body(z_ref, zl_ref, cw_ref, vec_ref, ws_ref, bsf_ref, y_ref, a1_ref, ext_ref, sh_ref):
        s = pl.program_id(1)
        hl = zl_ref[0]
        a0h = hl[:, 0:W_BR] * _sigmoid(hl[:, W_BR:2 * W_BR])
        ext_ref[0:HALO, :] = jnp.where(s > 0, a0h, 0.0)
        for a, b in _row_chunks(0, ts):
            ext_ref[HALO + a:HALO + b, :] = z_ref[0, a:b, 0:W_BR] * _sigmoid(z_ref[0, a:b, W_BR:2 * W_BR])
        ext_ref[ext_rows:ext_rows + 8, :] = jnp.zeros((8, W_BR), F32)
        _build_shifts(ext_ref, sh_ref, ext_rows)

        def conv_chunk(ci, carry):
            base = pl.multiple_of(ci * CONV_CHUNK, CONV_CHUNK)
            acc = jnp.zeros((CONV_CHUNK, W_BR), F32) + vec_ref[0:1, :]
            for k in range(KA):
                q, r = divmod(2 + k, 8)
                acc = acc + _shifted(ext_ref, sh_ref, r, base + 8 * q, CONV_CHUNK) * cw_ref[k:k + 1, :]
            a1_ref[0, pl.ds(base, CONV_CHUNK), :] = acc
            return carry

        lax.fori_loop(0, ts // CONV_CHUNK, conv_chunk, 0)
        keep, _ = _tril_masks()

        def block(bi, carry):
            rows = pl.ds(pl.multiple_of(bi * GBLK, GBLK), GBLK)
            xh, _ = _ln_stats(a1_ref[0, rows, :])
            a = _silu(xh * vec_ref[1:2, :] + vec_ref[2:3, :]) * _silu(z_ref[0, rows, 2 * W_BR:3 * W_BR])
            y_ref[0, rows, 0:W_BR] = a.astype(y_ref.dtype)
            ua = _gelu(z_ref[0, rows, 3 * W_BR:4 * W_BR])
            vh, _ = _ln_stats(_gelu(z_ref[0, rows, 4 * W_BR:5 * W_BR]))
            vl = vh * vec_ref[3:4, :] + vec_ref[4:5, :]
            sg = _spatial(ws_ref, keep, _mm(vl)) + bsf_ref[...]
            g = ua * sg * _silu(z_ref[0, rows, 5 * W_BR:6 * W_BR])
            y_ref[0, rows, W_BR:2 * W_BR] = g.astype(y_ref.dtype)
            return carry

        lax.fori_loop(0, ts // GBLK, block, 0)

    full = lambda shape: pl.BlockSpec(shape, lambda b, s: (0,) * len(shape))
    return pl.pallas_call(
        body, name="even_fwd", grid=(bsz, s_len // ts),
        in_specs=[pl.BlockSpec((1, ts, N_COLS), lambda b, s: (b, s, 0)), _halo_specs(ts, s_len, 2 * W_BR),
                  full((32, W_BR)), full((8, W_BR)), full((8, GBLK, GBLK)), full((GBLK, W_BR))],
        out_specs=[pl.BlockSpec((1, ts, 2 * W_BR), lambda b, s: (b, s, 0)), pl.BlockSpec((1, ts, W_BR), lambda b, s: (b, s, 0))],
        out_shape=[jax.ShapeDtypeStruct((bsz, s_len, 2 * W_BR), MM_DTYPE), jax.ShapeDtypeStruct((bsz, s_len, W_BR), F32)],
        scratch_shapes=[pltpu.VMEM((ext_rows + 8, W_BR), F32), pltpu.VMEM((7, ext_rows, W_BR), F32)],
        compiler_params=_cparams(("parallel", "parallel")),
    )(z3, z3, cw, vec, ws, bsf)


def _even_bwd(z3, dy3, a13, cw, vec, ws, wst, bsf):
    bsz, s_len, _ = z3.shape
    ts = min(SEQ_TILE, s_len)
    n_s = s_len // ts
    ext_rows = ts + 2 * HALO
    a_rows = ts + HALO

    def body(z_ref, zl_ref, zr_ref, dy_ref, dyr_ref, a1_ref, a1r_ref, cw_ref, vec_ref, ws_ref, wst_ref, bsf_ref,
             dz_ref, dbin_ref, dcw_ref, dvec_ref, dws_ref, dbsf_ref,
             ext_ref, sh_ref, ag_ref, dya_ref, d_ref, accw_ref):
        b = pl.program_id(0)
        s = pl.program_id(1)

        @pl.when((b == 0) & (s == 0))
        def _():
            dbin_ref[...] = jnp.zeros_like(dbin_ref)
            dcw_ref[...] = jnp.zeros_like(dcw_ref)
            dvec_ref[...] = jnp.zeros_like(dvec_ref)
            dws_ref[...] = jnp.zeros_like(dws_ref)
            dbsf_ref[...] = jnp.zeros_like(dbsf_ref)

        has_right = s < n_s - 1
        hl = zl_ref[0]
        hr = zr_ref[0]
        ext_ref[0:HALO, :] = jnp.where(s > 0, hl[:, 0:W_BR] * _sigmoid(hl[:, W_BR:2 * W_BR]), 0.0)
        for a, b in _row_chunks(0, ts):
            ext_ref[HALO + a:HALO + b, :] = z_ref[0, a:b, 0:W_BR] * _sigmoid(z_ref[0, a:b, W_BR:2 * W_BR])
            ag_ref[a:b, :] = z_ref[0, a:b, 2 * W_BR:3 * W_BR]
            dya_ref[a:b, :] = dy_ref[0, a:b, 0:W_BR]
        ext_ref[HALO + ts:ext_rows, :] = hr[:, 0:W_BR] * _sigmoid(hr[:, W_BR:2 * W_BR])
        ext_ref[ext_rows:ext_rows + 8, :] = jnp.zeros((8, W_BR), F32)
        ag_ref[ts:a_rows, :] = hr[:, 2 * W_BR:3 * W_BR]
        dya_ref[ts:a_rows, :] = dyr_ref[0][:, 0:W_BR]
        _build_shifts(ext_ref, sh_ref, ext_rows)

        def a_chunk(base, n, main):
            rows = pl.ds(base, n)
            xh, rs = _ln_stats(a1_ref[0, rows, :] if main else a1r_ref[0])
            ln = xh * vec_ref[1:2, :] + vec_ref[2:3, :]
            sl, dsl = _silu_grad(ln)
            sgt, dsgt = _silu_grad(ag_ref[rows, :])
            dya = dya_ref[rows, :]
            dln = dya * sgt * dsl
            da1 = _ln_bwd(dln * vec_ref[1:2, :], xh, rs)
            if main:
                d_ref[rows, :] = da1
                dag = dya * sl * dsgt
                dz_ref[0, rows, 2 * W_BR:3 * W_BR] = dag.astype(dz_ref.dtype)
                dbin_ref[0:1, 2 * W_BR:3 * W_BR] += _rowsum(dag)
                dvec_ref[0:1, :] += _rowsum(da1)
                dvec_ref[1:2, :] += _rowsum(dln * xh)
                dvec_ref[2:3, :] += _rowsum(dln)
            else:
                d_ref[rows, :] = jnp.where(has_right, da1, 0.0)

        def a_main(ci, carry):
            a_chunk(pl.multiple_of(ci * GBLK, GBLK), GBLK, True)
            return carry

        lax.fori_loop(0, ts // GBLK, a_main, 0)
        a_chunk(ts, HALO, False)
        d_ref[a_rows:a_rows + 8, :] = jnp.zeros((8, W_BR), F32)

        accw_ref[...] = jnp.zeros_like(accw_ref)

        def dw_chunk(ci, carry):
            base = pl.multiple_of(ci * CONV_CHUNK, CONV_CHUNK)
            d = d_ref[pl.ds(base, CONV_CHUNK), :]
            for k in range(KA):
                q, r = divmod(2 + k, 8)
                prod = d * _shifted(ext_ref, sh_ref, r, base + 8 * q, CONV_CHUNK)
                accw_ref[k] += jnp.sum(prod.reshape(CONV_CHUNK // 8, 8, W_BR), axis=0)
            return carry

        lax.fori_loop(0, ts // CONV_CHUNK, dw_chunk, 0)
        dcw_ref[...] += jnp.sum(accw_ref[...], axis=1)

        _build_shifts(d_ref, sh_ref, a_rows)

        def dx_chunk(ci, carry):
            base = pl.multiple_of(ci * CONV_CHUNK, CONV_CHUNK)
            rows = pl.ds(base, CONV_CHUNK)
            acc = jnp.zeros((CONV_CHUNK, W_BR), F32)
            for m in range(KA):
                q, r = divmod(m, 8)
                acc = acc + _shifted(d_ref, sh_ref, r, base + 8 * q, CONV_CHUNK) * cw_ref[KA - 1 - m:KA - m, :]
            aval = z_ref[0, rows, 0:W_BR]
            sg = _sigmoid(z_ref[0, rows, W_BR:2 * W_BR])
            dval = acc * sg
            dglu = acc * aval * sg * (1.0 - sg)
            dz_ref[0, rows, 0:W_BR] = dval.astype(dz_ref.dtype)
            dz_ref[0, rows, W_BR:2 * W_BR] = dglu.astype(dz_ref.dtype)
            dbin_ref[0:1, 0:W_BR] += _rowsum(dval)
            dbin_ref[0:1, W_BR:2 * W_BR] += _rowsum(dglu)
            return carry

        lax.fori_loop(0, ts // CONV_CHUNK, dx_chunk, 0)

        keep, keep_t = _tril_masks()
        lane = lax.broadcasted_iota(jnp.int32, (GBLK, GBLK), 1)

        def block(bi, carry):
            rows = pl.ds(pl.multiple_of(bi * GBLK, GBLK), GBLK)
            ua, dua = _gelu_grad(z_ref[0, rows, 3 * W_BR:4 * W_BR])
            va, dva = _gelu_grad(z_ref[0, rows, 4 * W_BR:5 * W_BR])
            sgt, dsgt = _silu_grad(z_ref[0, rows, 5 * W_BR:6 * W_BR])
            vh, rs = _ln_stats(va)
            vlb = _mm(vh * vec_ref[3:4, :] + vec_ref[4:5, :])
            sg = _spatial(ws_ref, keep, vlb) + bsf_ref[...]
            dyg = dy_ref[0, rows, W_BR:2 * W_BR]
            du = dyg * sg * sgt * dua
            dsg = dyg * ua * sgt
            dgg = dyg * ua * sg * dsgt
            dvl = _spatial(wst_ref, keep_t, _mm(dsg))
            for p in range(4):
                dsp = dsg[:, p * GBLK:(p + 1) * GBLK]
                vlp = vlb[:, p * GBLK:(p + 1) * GBLK]
                dws_ref[2 * p] += jnp.where(keep, _dot_nt(jnp.where(lane < 64, dsp, 0.0), vlp), 0.0)
                dws_ref[2 * p + 1] += jnp.where(keep, _dot_nt(jnp.where(lane >= 64, dsp, 0.0), vlp), 0.0)
            dbsf_ref[...] += dsg
            dvec_ref[3:4, :] += _rowsum(dvl * vh)
            dvec_ref[4:5, :] += _rowsum(dvl)
            dv = _ln_bwd(dvl * vec_ref[3:4, :], vh, rs) * dva
            dz_ref[0, rows, 3 * W_BR:4 * W_BR] = du.astype(dz_ref.dtype)
            dz_ref[0, rows, 4 * W_BR:5 * W_BR] = dv.astype(dz_ref.dtype)
            dz_ref[0, rows, 5 * W_BR:6 * W_BR] = dgg.astype(dz_ref.dtype)
            dbin_ref[0:1, 3 * W_BR:4 * W_BR] += _rowsum(du)
            dbin_ref[0:1, 4 * W_BR:5 * W_BR] += _rowsum(dv)
            dbin_ref[0:1, 5 * W_BR:6 * W_BR] += _rowsum(dgg)
            return carry

        lax.fori_loop(0, ts // GBLK, block, 0)

    full = lambda shape: pl.BlockSpec(shape, lambda b, s: (0,) * len(shape))
    acc_shapes = [(1, N_COLS), (32, W_BR), (8, W_BR), (8, GBLK, GBLK), (GBLK, W_BR)]
    return pl.pallas_call(
        body, name="even_bwd", grid=(bsz, n_s),
        in_specs=[pl.BlockSpec((1, ts, N_COLS), lambda b, s: (b, s, 0)),
                  _halo_specs(ts, s_len, N_COLS, True), _halo_specs(ts, s_len, N_COLS, False),
                  pl.BlockSpec((1, ts, 2 * W_BR), lambda b, s: (b, s, 0)), _halo_specs(ts, s_len, 2 * W_BR, False),
                  pl.BlockSpec((1, ts, W_BR), lambda b, s: (b, s, 0)), _halo_specs(ts, s_len, W_BR, False),
                  full((32, W_BR)), full((8, W_BR)), full((8, GBLK, GBLK)), full((8, GBLK, GBLK)), full((GBLK, W_BR))],
        out_specs=[pl.BlockSpec((1, ts, N_COLS), lambda b, s: (b, s, 0))] + [full(sh) for sh in acc_shapes],
        out_shape=[jax.ShapeDtypeStruct((bsz, s_len, N_COLS), MM_DTYPE)] + [jax.ShapeDtypeStruct(sh, F32) for sh in acc_shapes],
        scratch_shapes=[pltpu.VMEM((ext_rows + 8, W_BR), F32), pltpu.VMEM((7, ext_rows, W_BR), F32),
                        pltpu.VMEM((a_rows, W_BR), F32), pltpu.VMEM((a_rows, W_BR), F32),
                        pltpu.VMEM((a_rows + 8, W_BR), F32), pltpu.VMEM((32, 8, W_BR), F32)],
        compiler_params=_cparams(("arbitrary", "arbitrary")),
    )(z3, z3, z3, dy3, dy3, a13, a13, cw, vec, ws, wst, bsf)


def _row_chunks(lo, hi):
    return [(a, min(a + ROW_CHUNK, hi)) for a in range(lo, hi, ROW_CHUNK)]


def _pool_stages(e_refs, rows):
    e0, e1, e2, e3, e4 = e_refs
    for a, b in _row_chunks(8, rows):
        e1[a:b, :] = e0[a:b, :] + e0[a - 1:b - 1, :]
    for a, b in _row_chunks(16, rows):
        e2[a:b, GBLK:] = e1[a:b, GBLK:] + e1[a - 2:b - 2, GBLK:]
    for a, b in _row_chunks(24, rows):
        e3[a:b, 2 * GBLK:] = e2[a:b, 2 * GBLK:] + e2[a - 4:b - 4, 2 * GBLK:]
    for a, b in _row_chunks(32, rows):
        e4[a:b, 3 * GBLK:] = e3[a:b, 3 * GBLK:] + e3[a - 8:b - 8, 3 * GBLK:]


def _pool_counts(start, n):
    pos = (start + 1 + lax.broadcasted_iota(jnp.int32, (n, 1), 0)).astype(F32)
    return [jnp.minimum(pos, float(w)) for w in POOL_WINDOWS]


def _pooled_into(e_refs, pooled_ref, s, ts):
    for a, b in _row_chunks(0, ts):
        cnt = _pool_counts(s * ts + a, b - a)
        for g in range(4):
            cs = slice(g * GBLK, (g + 1) * GBLK)
            pooled_ref[a:b, cs] = e_refs[g + 1][HALO + a:HALO + b, cs] / cnt[g] - e_refs[0][HALO + a:HALO + b, cs]


def _odd_prologue(z_ref, hl, s, ts, e_refs, pooled_ref, dext_ref, ec_ref, vec_ref):
    e0 = e_refs[0]
    e0[0:HALO, :] = jnp.where(s > 0, hl[:, 0:W_BR], 0.0)
    dext_ref[0:HALO, :] = jnp.where(s > 0, hl[:, 2 * W_BR:3 * W_BR] * hl[:, 4 * W_BR:5 * W_BR], 0.0)
    for a, b in _row_chunks(0, ts):
        e0[HALO + a:HALO + b, :] = z_ref[0, a:b, 0:W_BR]
        dext_ref[HALO + a:HALO + b, :] = z_ref[0, a:b, 2 * W_BR:3 * W_BR] * z_ref[0, a:b, 4 * W_BR:5 * W_BR]
    _pool_stages(e_refs, ts + HALO)
    _pooled_into(e_refs, pooled_ref, s, ts)
    for a, b in _row_chunks(0, ts):
        ec_ref[a:b, :] = (vec_ref[1:2, :] * dext_ref[HALO - 2 + a:HALO - 2 + b, :] + vec_ref[2:3, :] * dext_ref[HALO - 1 + a:HALO - 1 + b, :]
                          + vec_ref[3:4, :] * dext_ref[HALO + a:HALO + b, :])


def _odd_fwd(z3, wp, vec):
    bsz, s_len, _ = z3.shape
    ts = min(SEQ_TILE, s_len)
    ext_rows = ts + HALO

    def body(z_ref, zl_ref, wp_ref, vec_ref, y_ref, e0, e1, e2, e3, e4, pooled_ref, dext_ref, ec_ref):
        s = pl.program_id(1)
        _odd_prologue(z_ref, zl_ref[0], s, ts, (e0, e1, e2, e3, e4), pooled_ref, dext_ref, ec_ref, vec_ref)

        def block(bi, carry):
            rows = pl.ds(pl.multiple_of(bi * GBLK, GBLK), GBLK)
            pb = _mm(pooled_ref[rows, :])
            cpre = jnp.concatenate([jnp.dot(pb[:, g * GBLK:(g + 1) * GBLK], wp_ref[g], preferred_element_type=F32)
                                    for g in range(4)], axis=1)
            c = cpre * vec_ref[0:1, :] * _silu(z_ref[0, rows, W_BR:2 * W_BR])
            d = z_ref[0, rows, 3 * W_BR:4 * W_BR] * ec_ref[rows, :] * _silu(z_ref[0, rows, 5 * W_BR:6 * W_BR])
            y_ref[0, rows, 0:W_BR] = c.astype(y_ref.dtype)
            y_ref[0, rows, W_BR:2 * W_BR] = d.astype(y_ref.dtype)
            return carry

        lax.fori_loop(0, ts // GBLK, block, 0)

    full = lambda shape: pl.BlockSpec(shape, lambda b, s: (0,) * len(shape))
    ebuf = pltpu.VMEM((ext_rows, W_BR), F32)
    return pl.pallas_call(
        body, name="odd_fwd", grid=(bsz, s_len // ts),
        in_specs=[pl.BlockSpec((1, ts, N_COLS), lambda b, s: (b, s, 0)), _halo_specs(ts, s_len, N_COLS),
                  full((4, GBLK, GBLK)), full((8, W_BR))],
        out_specs=pl.BlockSpec((1, ts, 2 * W_BR), lambda b, s: (b, s, 0)),
        out_shape=jax.ShapeDtypeStruct((bsz, s_len, 2 * W_BR), MM_DTYPE),
        scratch_shapes=[ebuf, ebuf, ebuf, ebuf, ebuf, pltpu.VMEM((ts, W_BR), F32), ebuf, pltpu.VMEM((ts, W_BR), F32)],
        compiler_params=_cparams(("parallel", "parallel")),
    )(z3, z3, wp, vec)


def _odd_bwd(z3, dy3, wp, vec):
    bsz, s_len, _ = z3.shape
    ts = min(SEQ_TILE, s_len)
    n_s = s_len // ts
    ext_rows = ts + HALO

    def body(z_ref, zl_ref, zr_ref, dy_ref, dyr_ref, wp_ref, vec_ref,
             dz_ref, dbin_ref, dwp_ref, dvec_ref,
             e0, e1, e2, e3, e4, pooled_ref, dext_ref, ec_ref, q_ref, dp_ref, de_ref, f1, f2, f3, f4):
        b = pl.program_id(0)
        s = pl.program_id(1)

        @pl.when((b == 0) & (s == 0))
        def _():
            dbin_ref[...] = jnp.zeros_like(dbin_ref)
            dwp_ref[...] = jnp.zeros_like(dwp_ref)
            dvec_ref[...] = jnp.zeros_like(dvec_ref)

        has_right = s < n_s - 1
        _odd_prologue(z_ref, zl_ref[0], s, ts, (e0, e1, e2, e3, e4), pooled_ref, dext_ref, ec_ref, vec_ref)

        def grads(zc_gate, zd_b, zd_gate, dyc, dyd, rows_out, n, start, valid):
            sgt = _silu(zc_gate)
            dcpre = dyc * vec_ref[0:1, :] * sgt
            db = _mm(dcpre)
            dpool = jnp.concatenate([_dot_nt(db[:, g * GBLK:(g + 1) * GBLK], wp_ref[g]) for g in range(4)], axis=1)
            cnt = _pool_counts(start, n)
            q = jnp.concatenate([dpool[:, g * GBLK:(g + 1) * GBLK] / cnt[g] for g in range(4)], axis=1)
            de = dyd * zd_b * _silu(zd_gate)
            if valid is not None:
                q = jnp.where(valid, q, 0.0)
                de = jnp.where(valid, de, 0.0)
            q_ref[rows_out, :] = q
            dp_ref[rows_out, :] = dpool
            de_ref[rows_out, :] = de
            return dcpre

        def block(bi, carry):
            base = pl.multiple_of(bi * GBLK, GBLK)
            rows = pl.ds(base, GBLK)
            cg = z_ref[0, rows, W_BR:2 * W_BR]
            dyc = dy_ref[0, rows, 0:W_BR]
            dyd = dy_ref[0, rows, W_BR:2 * W_BR]
            d_b = z_ref[0, rows, 3 * W_BR:4 * W_BR]
            d_gate = z_ref[0, rows, 5 * W_BR:6 * W_BR]
            dcpre = grads(cg, d_b, d_gate, dyc, dyd, rows, GBLK, s * ts + base, None)
            pb = _mm(pooled_ref[rows, :])
            dcb = _mm(dcpre)
            cpre = jnp.concatenate([jnp.dot(pb[:, g * GBLK:(g + 1) * GBLK], wp_ref[g], preferred_element_type=F32)
                                    for g in range(4)], axis=1)
            for g in range(4):
                cs = slice(g * GBLK, (g + 1) * GBLK)
                dwp_ref[g] += _dot_tn(pb[:, cs], dcb[:, cs])
            sgt, dsgt = _silu_grad(cg)
            dvec_ref[0:1, :] += _rowsum(dyc * cpre * sgt)
            dcg = dyc * cpre * vec_ref[0:1, :] * dsgt
            sdt, dsdt = _silu_grad(d_gate)
            ec = ec_ref[rows, :]
            ddb = dyd * ec * sdt
            ddg = dyd * d_b * ec * dsdt
            dz_ref[0, rows, W_BR:2 * W_BR] = dcg.astype(dz_ref.dtype)
            dz_ref[0, rows, 3 * W_BR:4 * W_BR] = ddb.astype(dz_ref.dtype)
            dz_ref[0, rows, 5 * W_BR:6 * W_BR] = ddg.astype(dz_ref.dtype)
            dbin_ref[0:1, W_BR:2 * W_BR] += _rowsum(dcg)
            dbin_ref[0:1, 3 * W_BR:4 * W_BR] += _rowsum(ddb)
            dbin_ref[0:1, 5 * W_BR:6 * W_BR] += _rowsum(ddg)
            return carry

        lax.fori_loop(0, ts // GBLK, block, 0)
        hr = zr_ref[0]
        dyr = dyr_ref[0]
        grads(hr[:, W_BR:2 * W_BR], hr[:, 3 * W_BR:4 * W_BR], hr[:, 5 * W_BR:6 * W_BR], dyr[:, 0:W_BR], dyr[:, W_BR:2 * W_BR],
              slice(ts, ext_rows), HALO, (s + 1) * ts, has_right)

        for a, b in _row_chunks(0, ts + 24):
            f1[a:b, :] = q_ref[a:b, :] + q_ref[a + 1:b + 1, :]
        for a, b in _row_chunks(0, ts + 16):
            f2[a:b, GBLK:] = f1[a:b, GBLK:] + f1[a + 2:b + 2, GBLK:]
        for a, b in _row_chunks(0, ts + 8):
            f3[a:b, 2 * GBLK:] = f2[a:b, 2 * GBLK:] + f2[a + 4:b + 4, 2 * GBLK:]
        for a, b in _row_chunks(0, ts):
            f4[a:b, 3 * GBLK:] = f3[a:b, 3 * GBLK:] + f3[a + 8:b + 8, 3 * GBLK:]

        for a, b in _row_chunks(0, ts):
            for g, f in enumerate((f1, f2, f3, f4)):
                cs = slice(g * GBLK, (g + 1) * GBLK)
                dvg = f[a:b, cs] - dp_ref[a:b, cs]
                dz_ref[0, a:b, cs] = dvg.astype(dz_ref.dtype)
                dbin_ref[0:1, cs] += _rowsum(dvg)
            de = de_ref[a:b, :]
            ddc = vec_ref[1:2, :] * de_ref[a + 2:b + 2, :] + vec_ref[2:3, :] * de_ref[a + 1:b + 1, :] + vec_ref[3:4, :] * de
            ddh = ddc * z_ref[0, a:b, 4 * W_BR:5 * W_BR]
            ddcc = ddc * z_ref[0, a:b, 2 * W_BR:3 * W_BR]
            dz_ref[0, a:b, 2 * W_BR:3 * W_BR] = ddh.astype(dz_ref.dtype)
            dz_ref[0, a:b, 4 * W_BR:5 * W_BR] = ddcc.astype(dz_ref.dtype)
            dbin_ref[0:1, 2 * W_BR:3 * W_BR] += _rowsum(ddh)
            dbin_ref[0:1, 4 * W_BR:5 * W_BR] += _rowsum(ddcc)
            for k in range(3):
                dvec_ref[1 + k:2 + k, :] += _rowsum(de * dext_ref[HALO - 2 + k + a:HALO - 2 + k + b, :])

    full = lambda shape: pl.BlockSpec(shape, lambda b, s: (0,) * len(shape))
    acc_shapes = [(1, N_COLS), (4, GBLK, GBLK), (8, W_BR)]
    ebuf = pltpu.VMEM((ext_rows, W_BR), F32)
    tbuf = pltpu.VMEM((ts, W_BR), F32)
    return pl.pallas_call(
        body, name="odd_bwd", grid=(bsz, n_s),
        in_specs=[pl.BlockSpec((1, ts, N_COLS), lambda b, s: (b, s, 0)),
                  _halo_specs(ts, s_len, N_COLS, True), _halo_specs(ts, s_len, N_COLS, False),
                  pl.BlockSpec((1, ts, 2 * W_BR), lambda b, s: (b, s, 0)), _halo_specs(ts, s_len, 2 * W_BR, False),
                  full((4, GBLK, GBLK)), full((8, W_BR))],
        out_specs=[pl.BlockSpec((1, ts, N_COLS), lambda b, s: (b, s, 0))] + [full(sh) for sh in acc_shapes],
        out_shape=[jax.ShapeDtypeStruct((bsz, s_len, N_COLS), MM_DTYPE)] + [jax.ShapeDtypeStruct(sh, F32) for sh in acc_shapes],
        scratch_shapes=[ebuf, ebuf, ebuf, ebuf, ebuf, tbuf, ebuf, tbuf, ebuf, ebuf, ebuf, ebuf, ebuf, ebuf, tbuf],
        compiler_params=_cparams(("arbitrary", "arbitrary")),
    )(z3, z3, z3, dy3, dy3, wp, vec)


def _post_fwd(y2, x2, p_layer, w_out, wg, wple, vec, tgt=None):
    p_all, layer = p_layer
    t, d = x2.shape
    tm = min(ROW_TILE, t)
    last = tgt is not None

    def body(*refs):
        y_ref, x_ref, p_ref, wo_ref, wg_ref, wp_ref, vec_ref = refs[:7]
        xn_ref, r_ref, gate_ref = refs[7 + last:10 + last]
        r = ALPHA * x_ref[...] + jnp.dot(y_ref[...], wo_ref[...], preferred_element_type=F32) + vec_ref[0:1, :]
        r_ref[...] = r
        xh, _ = _ln_stats(r)
        h = xh * vec_ref[1:2, :] + vec_ref[2:3, :]
        gate = _sigmoid(_dot(h, wg_ref[...]) + vec_ref[3:4, :])
        gate_ref[...] = gate
        xn = h + gate * _dot(p_ref[...], wp_ref[...])
        if last:
            sq_ref = refs[11]

            @pl.when(pl.program_id(0) == 0)
            def _():
                sq_ref[...] = jnp.zeros_like(sq_ref)
            e = xn - refs[7][...]
            xn_ref[...] = e / float(d)
            sq_ref[...] += _rowsum(e * e)
        else:
            xn_ref[...] = xn

    row = lambda c: pl.BlockSpec((tm, c), lambda i: (i, 0))
    full = lambda shape: pl.BlockSpec(shape, lambda i: (0,) * len(shape))
    return pl.pallas_call(
        body, name="post_fwd_loss" if last else "post_fwd", grid=(t // tm,),
        in_specs=[row(d), row(d), pl.BlockSpec((None, tm, D_PLE), lambda i: (layer, i, 0)),
                  full((d, d)), full((d, d)), full((D_PLE, d)), full((8, d))] + [row(d)] * last,
        out_specs=[row(d), row(d), row(d)] + [full((1, d))] * last,
        out_shape=[jax.ShapeDtypeStruct((t, d), F32)] * 3 + [jax.ShapeDtypeStruct((1, d), F32)] * last,
        compiler_params=_cparams(("arbitrary",) if last else ("parallel",)),
    )(y2, x2, p_all, w_out, wg, wple, vec, *([tgt] if last else []))


def _post_bwd(dxn, r2, gate2, p_layer, y2, w_out, wg, wple, vec):
    p_all, layer = p_layer
    t, d = r2.shape
    tm = min(BWD_ROW_TILE, t)

    def body(dxn_ref, r_ref, gate_ref, p_ref, y_ref, wo_ref, wg_ref, wp_ref, vec_ref,
             dxr_ref, dy_ref, dwo_ref, dwg_ref, dwp_ref, dvec_ref):
        @pl.when(pl.program_id(0) == 0)
        def _():
            dwo_ref[...] = jnp.zeros_like(dwo_ref)
            dwg_ref[...] = jnp.zeros_like(dwg_ref)
            dwp_ref[...] = jnp.zeros_like(dwp_ref)
            dvec_ref[...] = jnp.zeros_like(dvec_ref)

        dxn = dxn_ref[...]
        gate = gate_ref[...]
        xh, rs = _ln_stats(r_ref[...])
        hb = _mm(xh * vec_ref[1:2, :] + vec_ref[2:3, :])
        pb = _mm(p_ref[...])
        pe = jnp.dot(pb, wp_ref[...], preferred_element_type=F32)
        dpre = dxn * pe * gate * (1.0 - gate)
        dpb = _mm(dpre)
        dh = dxn + _dot_nt(dpb, wg_ref[...])
        dwg_ref[...] += _dot_tn(hb, dpb)
        dwp_ref[...] += _dot_tn(pb, dxn * gate)
        dr = _ln_bwd(dh * vec_ref[1:2, :], xh, rs)
        drb = _mm(dr)
        dxr_ref[...] = ALPHA * dr
        dy_ref[...] = _dot_nt(drb, wo_ref[...])
        dwo_ref[...] += _dot_tn(y_ref[...], drb)
        dvec_ref[0:1, :] += _rowsum(dr)
        dvec_ref[1:2, :] += _rowsum(dh * xh)
        dvec_ref[2:3, :] += _rowsum(dh)
        dvec_ref[3:4, :] += _rowsum(dpre)

    row = lambda c: pl.BlockSpec((tm, c), lambda i: (i, 0))
    full = lambda shape: pl.BlockSpec(shape, lambda i: (0,) * len(shape), pipeline_mode=pl.Buffered(1))
    acc_shapes = [(d, d), (d, d), (D_PLE, d), (8, d)]
    return pl.pallas_call(
        body, name="post_bwd", grid=(t // tm,),
        in_specs=[row(d), row(d), row(d), pl.BlockSpec((None, tm, D_PLE), lambda i: (layer, i, 0)), row(d),
                  full((d, d)), full((d, d)), full((D_PLE, d)), full((8, d))],
        out_specs=[row(d), row(d)] + [full(sh) for sh in acc_shapes],
        out_shape=[jax.ShapeDtypeStruct((t, d), F32)] * 2 + [jax.ShapeDtypeStruct(sh, F32) for sh in acc_shapes],
        compiler_params=_cparams(("arbitrary",)),
    )(dxn, r2, gate2, p_all, y2, w_out, wg, wple, vec)


def _place():
    x, y, c = lax.axis_index("x"), lax.axis_index("y"), lax.axis_index("c")
    chips = [(1 - x, y), (x, 1 - y), (1 - x, 1 - y)]
    return x, y, c, chips


def _shard_of(ref, ax, k, width, lo=None, ln=None):
    idx = [slice(None)] * 3
    if lo is not None:
        idx[0] = pl.ds(lo, ln)
    idx[ax] = pl.ds(k * width, width)
    return ref.at[tuple(idx)]


def _remote(src, dst, ssem, rsem, dev):
    return pltpu.make_async_remote_copy(src_ref=src, dst_ref=dst, send_sem=ssem, recv_sem=rsem, device_id=dev, device_id_type=MESH)


def _place_shard(w, ax, chip, dtype=MM_DTYPE, layer=None):
    l_dim, a_dim, b_dim = w.shape
    tr = min(256, a_dim)
    per = a_dim // tr
    first = 0
    if layer is not None:
        l_dim, first = 1, layer
    shape = [l_dim, a_dim, b_dim]
    shape[ax] *= 4
    if ax == 2:
        out_spec = pl.BlockSpec((1, tr, b_dim), lambda l, i, k: (l, i, k[0]))
    else:
        out_spec = pl.BlockSpec((1, tr, b_dim), lambda l, i, k: (l, k[0] * per + i, 0))

    def body(k_ref, w_ref, o_ref):
        o_ref[...] = w_ref[...].astype(o_ref.dtype)

    return pl.pallas_call(
        body, name="place_shard",
        grid_spec=pltpu.PrefetchScalarGridSpec(
            num_scalar_prefetch=1, grid=(l_dim, per),
            in_specs=[pl.BlockSpec((1, tr, b_dim), lambda l, i, k: (first + l, i, 0))], out_specs=out_spec),
        out_shape=jax.ShapeDtypeStruct(tuple(shape), dtype),
        compiler_params=_cparams(("parallel", "parallel")),
    )(chip, w)


def _shard_copies(src_refs, dst_refs, axes, sems):
    x, y, c, chips = _place()
    j = 2 * x + y
    cps = []
    for a, (s_ref, d_ref) in enumerate(zip(src_refs, dst_refs)):
        w = d_ref.shape[axes[a]] // 4
        for q, (qx, qy) in enumerate(chips):
            ssem, rsem = sems[3 * a + q]
            cps.append(_remote(_shard_of(s_ref, axes[a], j, w), _shard_of(d_ref, axes[a], j, w), ssem, rsem, (qx, qy, c)))
    return cps


def _chip_handshake():
    x, y, c, chips = _place()
    barrier = pltpu.get_barrier_semaphore()
    for qx, qy in chips:
        pl.semaphore_signal(barrier, inc=1, device_id=(qx, qy, c), device_id_type=MESH)
    pl.semaphore_wait(barrier, 3)


def _gather_async(name, collective_id, fulls, axes):
    n = len(fulls)
    refs = [jax.new_ref(f, memory_space=pltpu.MemorySpace.HBM) for f in fulls]
    moved = sum(2 * 3 * (f.size // 4) * jnp.dtype(f.dtype).itemsize for f in fulls)

    @pl.kernel(mesh=plsc.ScalarSubcoreMesh(axis_name="seq", num_cores=1), name=name,
               scratch_types=(pltpu.SemaphoreType.DMA,) * (6 * n),
               cost_estimate=pl.CostEstimate(flops=0, transcendentals=0, bytes_accessed=moved, remote_bytes_transferred=moved),
               compiler_params=pltpu.CompilerParams(collective_id=collective_id))
    def launch(*sems):
        _chip_handshake()
        cps = _shard_copies(refs, refs, axes, [(sems[2 * k], sems[2 * k + 1]) for k in range(3 * n)])
        for cp in cps:
            cp.start()
        for cp in cps:
            cp.wait()

    launch()
    return refs


def _scatter_async(name, collective_id, grads, axes, after):
    n = len(grads)
    extra = len(after)
    outs = []
    for g, ax in zip(grads, axes):
        sh = list(g.shape)
        sh[ax] //= 4
        outs.append(jax.ShapeDtypeStruct((3,) + tuple(sh), g.dtype))

    def body(*refs):
        srcs, lands, sems = refs[:n], refs[n + extra:2 * n + extra], refs[2 * n + extra:]
        _chip_handshake()
        x, y, c, chips = _place()
        cps = []
        for a in range(n):
            w = srcs[a].shape[axes[a]] // 4
            for q, (qx, qy) in enumerate(chips):
                k = 3 * a + q
                cps.append(_remote(_shard_of(srcs[a], axes[a], 2 * qx + qy, w), lands[a].at[q], sems[2 * k], sems[2 * k + 1],
                                   (qx, qy, c)))
        for cp in cps:
            cp.start()
        for cp in cps:
            cp.wait()

    moved = sum(2 * 3 * math.prod(o.shape[1:]) * jnp.dtype(o.dtype).itemsize for o in outs)
    return pl.kernel(body, out_type=outs, mesh=plsc.ScalarSubcoreMesh(axis_name="seq", num_cores=1), name=name,
                     scratch_types=(pltpu.SemaphoreType.DMA,) * (6 * n),
                     cost_estimate=pl.CostEstimate(flops=0, transcendentals=0, bytes_accessed=moved, remote_bytes_transferred=moved),
                     compiler_params=pltpu.CompilerParams(collective_id=collective_id))(*grads, *after)


def _pair_swap_async(name, collective_id, sums):
    n = len(sums)

    def body(*refs):
        g_refs, o_refs, sems = refs[:n], refs[n:2 * n], refs[2 * n:]
        x, y, c, _ = _place()
        barrier = pltpu.get_barrier_semaphore()
        pl.semaphore_signal(barrier, inc=1, device_id=(x, y, 1 - c), device_id_type=MESH)
        pl.semaphore_wait(barrier, 1)
        cps = [_remote(g_refs[a], o_refs[a], sems[2 * a], sems[2 * a + 1], (x, y, 1 - c)) for a in range(n)]
        for cp in cps:
            cp.start()
        for cp in cps:
            cp.wait()

    moved = sum(2 * g.size * jnp.dtype(g.dtype).itemsize for g in sums)
    return pl.kernel(body, out_type=[jax.ShapeDtypeStruct(g.shape, g.dtype) for g in sums],
                     mesh=plsc.ScalarSubcoreMesh(axis_name="seq", num_cores=1), name=name,
                     scratch_types=(pltpu.SemaphoreType.DMA,) * (2 * n),
                     cost_estimate=pl.CostEstimate(flops=0, transcendentals=0, bytes_accessed=moved, remote_bytes_transferred=moved),
                     compiler_params=pltpu.CompilerParams(collective_id=collective_id))(*sums)


def _pair_exchange(grads, small):
    n = len(grads)
    outs = [jax.ShapeDtypeStruct((g.shape[0] // 2,) + g.shape[1:], g.dtype) for g in grads]
    outs.append(jax.ShapeDtypeStruct((small.shape[0] // 2, small.shape[1]), small.dtype))

    def body(*refs):
        g_refs, o_refs = refs[:n + 1], refs[n + 1:2 * n + 2]
        ssem, rsem = refs[2 * n + 2:]
        x, y, c, _ = _place()
        cps = []
        for a in range(n + 1):
            lh = g_refs[a].shape[0] // 2
            cp = _remote(g_refs[a].at[pl.ds((1 - c) * lh, lh)], o_refs[a], ssem.at[a], rsem.at[a], (x, y, 1 - c))
            cp.start()
            cps.append(cp)
        for cp in cps:
            cp.wait()

    return pl.pallas_call(
        body, name="pair_exchange", in_specs=[ANY] * (n + 1), out_specs=[ANY] * (n + 1), out_shape=outs,
        scratch_shapes=[pltpu.SemaphoreType.DMA((n + 1,)), pltpu.SemaphoreType.DMA((n + 1,))],
        compiler_params=pltpu.CompilerParams(has_side_effects=True),
    )(*grads, small)


def _chip_scatter(sums, axes, small):
    n = len(sums)
    outs = []
    for g, ax in zip(sums, axes):
        sh = list(g.shape)
        sh[ax] //= 4
        outs.append(jax.ShapeDtypeStruct((3,) + tuple(sh), g.dtype))
    rq = small.shape[0] // 4
    outs.append(jax.ShapeDtypeStruct((3, rq, small.shape[1]), small.dtype))

    def body(*refs):
        g_refs, o_refs = refs[:n + 1], refs[n + 1:2 * n + 2]
        ssem, rsem = refs[2 * n + 2:]
        x, y, c, chips = _place()
        cps = []
        for a in range(n + 1):
            for q, (qx, qy) in enumerate(chips):
                k = 2 * qx + qy
                if a < n:
                    src = _shard_of(g_refs[a], axes[a], k, g_refs[a].shape[axes[a]] // 4)
                else:
                    src = g_refs[a].at[pl.ds(k * rq, rq)]
                cp = _remote(src, o_refs[a].at[q], ssem.at[a * 3 + q], rsem.at[a * 3 + q], (qx, qy, c))
                cp.start()
                cps.append(cp)
        for cp in cps:
            cp.wait()

    return pl.pallas_call(
        body, name="chip_scatter", in_specs=[ANY] * (n + 1), out_specs=[ANY] * (n + 1), out_shape=outs,
        scratch_shapes=[pltpu.SemaphoreType.DMA((3 * n + 3,)), pltpu.SemaphoreType.DMA((3 * n + 3,))],
        compiler_params=pltpu.CompilerParams(has_side_effects=True),
    )(*sums, small)


def _final_exchange(reds, small):
    n = len(reds)
    flips = [(fx, fy, fc) for fx in (0, 1) for fy in (0, 1) for fc in (0, 1)][1:]

    def body(*refs):
        g_refs, o_refs = refs[:n + 1], refs[n + 1:2 * n + 2]
        ssem, rsem = refs[2 * n + 2:]
        x, y, c, _ = _place()
        cps = []
        for a in range(n):
            lh = g_refs[a].shape[0] // 2
            cp = _remote(g_refs[a].at[pl.ds(c * lh, lh)], o_refs[a].at[pl.ds(c * lh, lh)], ssem.at[a], rsem.at[a], (x, y, 1 - c))
            cp.start()
            cps.append(cp)
        mine = 4 * c + 2 * x + y
        for f, (fx, fy, fc) in enumerate(flips):
            cp = _remote(g_refs[n].at[mine], o_refs[n].at[mine], ssem.at[n + f], rsem.at[n + f], (x ^ fx, y ^ fy, c ^ fc))
            cp.start()
            cps.append(cp)
        for cp in cps:
            cp.wait()

    return pl.pallas_call(
        body, name="final_exchange", in_specs=[ANY] * (n + 1), out_specs=[ANY] * (n + 1),
        out_shape=[jax.ShapeDtypeStruct(g.shape, g.dtype) for g in reds] + [jax.ShapeDtypeStruct(small.shape, small.dtype)],
        input_output_aliases={a: a for a in range(n + 1)},
        scratch_shapes=[pltpu.SemaphoreType.DMA((n + 7,)), pltpu.SemaphoreType.DMA((n + 7,))],
        compiler_params=pltpu.CompilerParams(has_side_effects=True),
    )(*reds, small)


def _small_allreduce(small):
    r = small.shape[0]
    rh, rq = r // 2, r // 8
    flips = [(fx, fy, fc) for fx in (0, 1) for fy in (0, 1) for fc in (0, 1)][1:]

    def body(g_ref, out_ref, pair_ref, chip_ref, s1_ref, ssem, rsem):
        x, y, c, chips = _place()
        cp = _remote(g_ref.at[pl.ds((1 - c) * rh, rh)], pair_ref, ssem.at[0], rsem.at[0], (x, y, 1 - c))
        cp.start()
        cp.wait()
        s1_ref[...] = g_ref[pl.ds(pl.multiple_of(c * rh, 8), rh), :] + pair_ref[...]
        cps = [_remote(s1_ref.at[pl.ds((2 * qx + qy) * rq, rq)], chip_ref.at[q], ssem.at[1 + q], rsem.at[1 + q], (qx, qy, c))
               for q, (qx, qy) in enumerate(chips)]
        for cp in cps:
            cp.start()
        for cp in cps:
            cp.wait()
        mine = out_ref.at[pl.ds(pl.multiple_of((4 * c + 2 * x + y) * rq, 8), rq)]
        mine[...] = ((s1_ref[pl.ds(pl.multiple_of((2 * x + y) * rq, 8), rq), :] + chip_ref[0]) + chip_ref[1]) + chip_ref[2]
        cps = [_remote(mine, mine, ssem.at[4 + f], rsem.at[4 + f], (x ^ fx, y ^ fy, c ^ fc)) for f, (fx, fy, fc) in enumerate(flips)]
        for cp in cps:
            cp.start()
        for cp in cps:
            cp.wait()

    vm = pl.BlockSpec(memory_space=pltpu.VMEM)
    return pl.pallas_call(
        body, name="small_allreduce", in_specs=[vm], out_specs=vm, out_shape=jax.ShapeDtypeStruct(small.shape, F32),
        scratch_shapes=[pltpu.VMEM((rh, 128), F32), pltpu.VMEM((3, rq, 128), F32), pltpu.VMEM((rh, 128), F32),
                        pltpu.SemaphoreType.DMA((11,)), pltpu.SemaphoreType.DMA((11,))],
        compiler_params=pltpu.CompilerParams(has_side_effects=True, vmem_limit_bytes=VMEM_LIMIT),
    )(small)


def _to_wire(g):
    _, a_dim, b_dim = g.shape
    tr = min(256, a_dim)

    def body(g_ref, o_ref, done_ref):
        o_ref[...] = g_ref[...].astype(o_ref.dtype)
        done_ref[...] = jnp.zeros_like(done_ref)

    blk = pl.BlockSpec((1, tr, b_dim), lambda i: (0, i, 0))
    return pl.pallas_call(
        body, name="to_wire", grid=(a_dim // tr,), in_specs=[blk], out_specs=[blk, pl.BlockSpec((8, 128), lambda i: (0, 0))],
        out_shape=[jax.ShapeDtypeStruct(g.shape, WIRE_DTYPE), jax.ShapeDtypeStruct((8, 128), F32)],
        compiler_params=_cparams(("arbitrary",)),
    )(g)


def _chip_sum(owns, gots, ax, chip, after=None):
    n_layers = len(owns)
    tail = [] if after is None else [after]
    _, _, a_dim, b_dim = gots[0].shape
    tr = min(256, a_dim)
    per = a_dim // tr

    def own_spec(layer):
        if ax == 2:
            return pl.BlockSpec((1, tr, b_dim), lambda l, i, k: (0, jnp.where(l == layer, i, 0), k[0]))
        return pl.BlockSpec((1, tr, b_dim), lambda l, i, k: (0, k[0] * per + jnp.where(l == layer, i, 0), 0))

    def got_spec(layer):
        return pl.BlockSpec((3, 1, tr, b_dim), lambda l, i, k: (0, 0, jnp.where(l == layer, i, 0), 0))

    def body(k_ref, *refs):
        s_ref = refs[-1]
        for layer in range(n_layers):
            @pl.when(pl.program_id(0) == layer)
            def _(own_ref=refs[layer], got_ref=refs[n_layers + layer]):
                s_ref[...] = ((own_ref[...].astype(F32) + got_ref[0].astype(F32)) + got_ref[1].astype(F32)) + got_ref[2].astype(F32)

    return pl.pallas_call(
        body, name="chip_sum",
        grid_spec=pltpu.PrefetchScalarGridSpec(
            num_scalar_prefetch=1, grid=(n_layers, per),
            in_specs=[own_spec(l) for l in range(n_layers)] + [got_spec(l) for l in range(n_layers)]
            + [pl.BlockSpec((8, 128), lambda l, i, k: (0, 0))] * len(tail),
            out_specs=pl.BlockSpec((1, tr, b_dim), lambda l, i, k: (l, i, 0))),
        out_shape=jax.ShapeDtypeStruct((n_layers, a_dim, b_dim), F32),
        compiler_params=_cparams(("arbitrary", "arbitrary")),
    )(chip, *owns, *gots, *tail)


def _small_pair_sum(small, got, half):
    rh = small.shape[0] // 2

    def body(h_ref, g_ref, o_ref, s_ref):
        s_ref[...] = g_ref[...] + o_ref[...]

    return pl.pallas_call(
        body, name="small_pair_sum",
        grid_spec=pltpu.PrefetchScalarGridSpec(
            num_scalar_prefetch=1, grid=(1,),
            in_specs=[pl.BlockSpec((rh, 128), lambda i, h: (h[0], 0)), pl.BlockSpec((rh, 128), lambda i, h: (0, 0))],
            out_specs=pl.BlockSpec((rh, 128), lambda i, h: (0, 0))),
        out_shape=jax.ShapeDtypeStruct((rh, 128), F32),
    )(half, small, got)


def _small_chip_sum(s1, got, pos):
    rq = got.shape[1]

    def body(k_ref, own_ref, got_ref, s_ref):
        s_ref[0] = ((own_ref[...] + got_ref[0]) + got_ref[1]) + got_ref[2]

    return pl.pallas_call(
        body, name="small_chip_sum",
        grid_spec=pltpu.PrefetchScalarGridSpec(
            num_scalar_prefetch=1, grid=(1,),
            in_specs=[pl.BlockSpec((rq, 128), lambda i, k: (k[0], 0)), pl.BlockSpec((3, rq, 128), lambda i, k: (0, 0, 0))],
            out_specs=pl.BlockSpec((1, rq, 128), lambda i, k: (4 * k[1] + k[0], 0, 0))),
        out_shape=jax.ShapeDtypeStruct((8, rq, 128), F32),
    )(pos, s1, got)


def _adam_math(w, g, m, v):
    m = ADAM_B1 * m + (1.0 - ADAM_B1) * g
    v = ADAM_B2 * v + (1.0 - ADAM_B2) * (g * g)
    m_hat = m / (1.0 - ADAM_B1 ** ADAM_STEP)
    v_hat = v / (1.0 - ADAM_B2 ** ADAM_STEP)
    return -ADAM_LR * (m_hat / (jnp.sqrt(v_hat) + ADAM_EPS) + ADAM_WD * w), m, v


def _adamw_big(w, g_mine, g_other, m, v):
    l_dim, a_dim, b_dim = w.shape
    tr = min(256, a_dim)

    def body(w_ref, g1_ref, g2_ref, m_ref, v_ref, g_ref, d_ref, nm_ref, nv_ref):
        g = g1_ref[...] + g2_ref[...]
        g_ref[...] = g
        d_ref[...], nm_ref[...], nv_ref[...] = _adam_math(w_ref[...], g, m_ref[...], v_ref[...])

    blk = pl.BlockSpec((1, tr, b_dim), lambda l, i: (l, i, 0))
    return pl.pallas_call(
        body, name="adamw_big", grid=(l_dim, a_dim // tr), in_specs=[blk] * 5, out_specs=[blk] * 4,
        out_shape=[jax.ShapeDtypeStruct(w.shape, F32)] * 4,
        compiler_params=_cparams(("parallel", "parallel")),
    )(w, g_mine, g_other, m, v)


def _adamw_small(ws, gs, ms, vs):
    n = len(ws)

    def body(*refs):
        for i in range(n):
            w_ref, g_ref, m_ref, v_ref = refs[i], refs[n + i], refs[2 * n + i], refs[3 * n + i]
            d_ref, nm_ref, nv_ref = refs[4 * n + i], refs[5 * n + i], refs[6 * n + i]
            d_ref[...], nm_ref[...], nv_ref[...] = _adam_math(w_ref[...], g_ref[...], m_ref[...], v_ref[...])

    shapes = [jax.ShapeDtypeStruct(w.shape, F32) for w in ws]
    outs = pl.pallas_call(body, name="adamw_small", out_shape=shapes * 3,
                          compiler_params=_cparams())(*ws, *gs, *ms, *vs)
    return outs[:n], outs[n:2 * n], outs[2 * n:]


def _pack(arrs, row_mult):
    parts = []
    for a in arrs:
        flat = a.reshape(-1)
        pad = (-flat.shape[0]) % 1024
        parts.append(jnp.pad(flat, (0, pad)).reshape(-1, 128))
    buf = jnp.concatenate(parts, axis=0)
    pad = (-buf.shape[0]) % row_mult
    return jnp.pad(buf, ((0, pad), (0, 0)))


def _unpack(buf, shapes):
    out, row = [], 0
    for sh in shapes:
        n = math.prod(sh)
        rows = -(-n // 1024) * 8
        out.append(buf[row:row + rows].reshape(-1)[:n].reshape(sh))
        row += rows
    return out


_NAMES = ['w_in_e', 'b_in_e', 'conv_a_w', 'conv_a_b', 'ln_a_g', 'ln_a_b', 'ln_v_g', 'ln_v_b', 'w_s', 'b_s', 'w_out_e', 'b_out_e',
          'w_in_o', 'b_in_o', 'w_pool', 'pool_scale', 'conv_d_w', 'w_out_o', 'b_out_o', 'ln_g', 'ln_b', 'w_ple', 'w_ple_gate',
          'b_ple_gate']
_BIG = ['w_in_e', 'w_out_e', 'w_in_o', 'w_out_o', 'w_ple', 'w_ple_gate']
_BIG_AXES = [2, 1, 2, 1, 2, 1]
_SMALL_SHARDED = ['conv_a_w', 'b_in_o', 'pool_scale', 'conv_d_w', 'b_out_o']


def kernel(x, p, w_in_e, b_in_e, conv_a_w, conv_a_b, ln_a_g, ln_a_b, ln_v_g, ln_v_b, w_s, b_s, w_out_e, b_out_e, w_in_o, b_in_o, w_pool, pool_scale, conv_d_w, w_out_o, b_out_o, ln_g, ln_b, w_ple, w_ple_gate, b_ple_gate, loss_target, m_w_in_e, m_b_in_e, m_conv_a_w, m_conv_a_b, m_ln_a_g, m_ln_a_b, m_ln_v_g, m_ln_v_b, m_w_s, m_b_s, m_w_out_e, m_b_out_e, m_w_in_o, m_b_in_o, m_w_pool, m_pool_scale, m_conv_d_w, m_w_out_o, m_b_out_o, m_ln_g, m_ln_b, m_w_ple, m_w_ple_gate, m_b_ple_gate, v_w_in_e, v_b_in_e, v_conv_a_w, v_conv_a_b, v_ln_a_g, v_ln_a_b, v_ln_v_g, v_ln_v_b, v_w_s, v_b_s, v_w_out_e, v_b_out_e, v_w_in_o, v_b_in_o, v_w_pool, v_pool_scale, v_conv_d_w, v_w_out_o, v_b_out_o, v_ln_g, v_ln_b, v_w_ple, v_w_ple_gate, v_b_ple_gate):
    args = locals()
    wts = {n: args[n] for n in _NAMES}
    mom = {n: args["m_" + n] for n in _NAMES}
    var = {n: args["v_" + n] for n in _NAMES}
    bsz, s_len, d = x.shape
    t = bsz * s_len
    cx, cy, cc = lax.axis_index("x"), lax.axis_index("y"), lax.axis_index("c")
    chip = (2 * cx + cy).astype(jnp.int32).reshape(1)
    half = cc.astype(jnp.int32).reshape(1)
    pos = jnp.concatenate([chip, half])

    def placed(name, ax, layer):
        return _place_shard(wts[name], ax, chip, layer=layer)

    sv = _pack([wts[n] for n in _SMALL_SHARDED], 8)
    first_refs = _gather_async("gather_first", DEPTH, [placed('w_in_e', 2, 0), _place_shard(sv[None], 1, chip, F32)], [2, 1])
    layer_refs = []
    for i in range(DEPTH):
        sfx = '_e' if i % 2 == 0 else '_o'
        items = [('w_out' + sfx, 1, i // 2), ('w_ple_gate', 1, i), ('w_ple', 2, i)] + ([('w_in' + sfx, 2, i // 2)] if i else [])
        layer_refs.append(_gather_async("gather_layer%d" % i, i, [placed(*it) for it in items], [it[1] for it in items]))
    fw = {}
    w_in_first = first_refs[0][...]
    sv_all = first_refs[1][...].reshape((4,) + sv.shape)
    small_parts = [_unpack(sv_all[k], [wts[n].shape for n in _SMALL_SHARDED]) for k in range(4)]
    for i, n in enumerate(_SMALL_SHARDED):
        fw[n] = jnp.concatenate([small_parts[k][i] for k in range(4)], axis=-1)
    for n in _NAMES:
        fw.setdefault(n, wts[n])

    def row8(rows, width):
        rows = [r.reshape(1, width) for r in rows]
        return jnp.concatenate(rows + [jnp.zeros((8 - len(rows), width), F32)], axis=0)

    x2 = x.reshape(t, d)
    saved = []
    for i in range(DEPTH):
        j = i // 2
        even = i % 2 == 0
        b_in, b_out = (fw['b_in_e'], fw['b_out_e']) if even else (fw['b_in_o'], fw['b_out_o'])
        w_in = layer_refs[i][3][...][0] if i else w_in_first[0]
        z = _in_proj(x2, w_in, b_in[j].reshape(1, N_COLS))
        z3 = z.reshape(bsz, s_len, N_COLS)
        if even:
            cw = jnp.concatenate([fw['conv_a_w'][j], jnp.zeros((1, W_BR), F32)], axis=0)
            mvec = row8([fw['conv_a_b'][j], fw['ln_a_g'][j], fw['ln_a_b'][j], fw['ln_v_g'][j], fw['ln_v_b'][j]], W_BR)
            bsf = jnp.repeat(fw['b_s'][j].T, W_BR // 8, axis=1)
            y3, a13 = _even_fwd(z3, cw, mvec, fw['w_s'][j], bsf)
            mix = (a13, cw, mvec, fw['w_s'][j], jnp.swapaxes(fw['w_s'][j], 1, 2), bsf)
        else:
            mvec = row8([fw['pool_scale'][j]] + [fw['conv_d_w'][j][k] for k in range(3)], W_BR)
            mix = (fw['w_pool'][j].astype(MM_DTYPE), mvec)
            y3 = _odd_fwd(z3, mix[0], mvec)
        pvec = row8([b_out[j], fw['ln_g'][i], fw['ln_b'][i], fw['b_ple_gate'][i]], d)
        post_w = (layer_refs[i][0][...][0], layer_refs[i][1][...][0], layer_refs[i][2][...][0], pvec)
        p2 = (p.reshape(DEPTH, t, D_PLE), i)
        y2 = y3.reshape(t, 2 * W_BR)
        if i < DEPTH - 1:
            xn, r2, gate2 = _post_fwd(y2, x2, p2, *post_w)
        else:
            dx, r2, gate2, sq = _post_fwd(y2, x2, p2, *post_w, tgt=loss_target.reshape(t, d))
        saved.append((x2, z3, y2, r2, gate2, p2, w_in, mix, post_w))
        x2 = xn

    gr = {n: [None] * wts[n].shape[0] for n in _NAMES}
    prev_lands = []
    for i in reversed(range(DEPTH)):
        j = i // 2
        even = i % 2 == 0
        x_in, z3, y2, r2, gate2, p2, w_in, mix, post_w = saved[i]
        dxr, dy, dwo, dwg, dwp, dpv = _post_bwd(dx, r2, gate2, p2, y2, *post_w)
        dy3 = dy.reshape(bsz, s_len, 2 * W_BR)
        sfx = '_e' if even else '_o'
        gr['w_out' + sfx][j], gr['b_out' + sfx][j] = dwo, dpv[0]
        gr['w_ple_gate'][i], gr['w_ple'][i] = dwg, dwp
        gr['ln_g'][i], gr['ln_b'][i], gr['b_ple_gate'][i] = dpv[1], dpv[2], dpv[3]
        if even:
            dz3, dbin, dcw, dmv, dws, dbsf = _even_bwd(z3, dy3, *mix)
            gr['conv_a_w'][j], gr['conv_a_b'][j] = dcw[:KA], dmv[0]
            gr['ln_a_g'][j], gr['ln_a_b'][j], gr['ln_v_g'][j], gr['ln_v_b'][j] = dmv[1], dmv[2], dmv[3], dmv[4]
            gr['w_s'][j] = dws
            gr['b_s'][j] = jnp.sum(dbsf.reshape(GBLK, 8, W_BR // 8), axis=2).T
        else:
            dz3, dbin, dwpool, dmv = _odd_bwd(z3, dy3, *mix)
            gr['w_pool'][j], gr['pool_scale'][j], gr['conv_d_w'][j] = dwpool, dmv[0], dmv[1:4]
        gr['b_in' + sfx][j] = dbin[0]
        dz2 = dz3.reshape(t, N_COLS)
        dwi, dw_done = _in_proj_bwd_dw(x_in, dz2)
        post_items = [('w_out' + sfx, j, dwo[None], 1), ('w_ple_gate', i, dwg[None], 1), ('w_ple', i, dwp[None], 2)]
        in_item = ('w_in' + sfx, j, dwi[None], 2)
        batches = [post_items + [in_item]] if i else [post_items, [in_item]]
        for bi, items in enumerate(batches):
            sent = [it[2] for it in items]
            if i == 0 and bi == 1:
                wire, dw_done = _to_wire(sent[0])
                sent = [wire]
            lands = _scatter_async("scatter_layer%d_%d" % (i, bi), DEPTH + 1 + 2 * i + bi, sent, [it[3] for it in items],
                                   prev_lands[:1])
            prev_lands = list(lands)
            for it, land in zip(items, lands):
                gr[it[0]][it[1]] = (it[2], land, it[3])
        dx = _in_proj_bwd_dx(dxr, dz2, w_in, dw_done)
    grad_x = dx.reshape(bsz, s_len, d)

    small_names = [n for n in _NAMES if n not in _BIG]
    g_small_full = [jnp.stack(gr[n]) for n in small_names] + [sq]
    small_all = _small_allreduce(_pack(g_small_full, 64))
    *g_small, sq_all = _unpack(small_all, [g.shape for g in g_small_full])
    loss = 0.5 * jnp.sum(sq_all) / d
    big_order = _BIG[1:] + _BIG[:1]
    sums = [_chip_sum([g[0] for g in gr[n]], [g[1] for g in gr[n]], gr[n][0][2], chip, dx if n == big_order[-1] else None)
            for n in big_order]
    others = (list(_pair_swap_async("pair_swap_a", 3 * DEPTH + 1, sums[:-1]))
              + list(_pair_swap_async("pair_swap_b", 3 * DEPTH + 2, sums[-1:])))
    grads = {}
    for n, g in zip(small_names, g_small):
        if n in _SMALL_SHARDED:
            w = wts[n].shape[-1]
            g = lax.dynamic_slice_in_dim(g, (2 * cx + cy) * w, w, axis=g.ndim - 1)
        grads[n] = g

    delta, new_m, new_v = {}, {}, {}
    ds, ms, vs = _adamw_small([wts[n] for n in small_names], [grads[n] for n in small_names],
                              [mom[n] for n in small_names], [var[n] for n in small_names])
    for n, a, b, c_ in zip(small_names, ds, ms, vs):
        delta[n], new_m[n], new_v[n] = a, b, c_
    for n, mine, other in zip(big_order, sums, others):
        grads[n], delta[n], new_m[n], new_v[n] = _adamw_big(wts[n], mine, other, mom[n], var[n])

    return (loss, grad_x, *[grads[n] for n in _NAMES], *[delta[n] for n in _NAMES],
            *[new_m[n] for n in _NAMES], *[new_v[n] for n in _NAMES])
```

```python
import functools
import math

import jax
import jax.numpy as jnp
from jax import lax
from jax.experimental import pallas as pl
from jax.experimental.pallas import tpu as pltpu
from jax.experimental.pallas import tpu_sc as plsc

F32 = jnp.float32
MM_DTYPE = jnp.bfloat16
WIRE_DTYPE = jnp.bfloat16
SEQ_TILE = 512
ROW_TILE = 512
BWD_ROW_TILE = 512
HALO = 32
CONV_CHUNK = 64
DW_CHUNK = 32
ROW_CHUNK = 64
GBLK = 128
VMEM_LIMIT = 56 * 1024 * 1024

D_MODEL = 1024
W_BR = 512
N_COLS = 6 * W_BR
D_PLE = 256
KA = 31
DEPTH = 4
POOL_WINDOWS = (2, 4, 8, 16)
ALPHA = (2.0 * DEPTH) ** 0.25
LN_EPS = 1e-5
GELU_C = math.sqrt(2.0 / math.pi)

ADAM_LR, ADAM_B1, ADAM_B2, ADAM_EPS, ADAM_WD, ADAM_STEP = 0.001, 0.9, 0.999, 1e-08, 0.01, 10

MESH = pl.DeviceIdType.MESH
ANY = pl.BlockSpec(memory_space=pl.ANY)


def _cparams(sem=None):
    return pltpu.CompilerParams(dimension_semantics=sem, vmem_limit_bytes=VMEM_LIMIT)


def _sigmoid(x):
    return 1.0 / (1.0 + jnp.exp(-x))


def _silu(x):
    return x * _sigmoid(x)


def _silu_grad(x):
    s = _sigmoid(x)
    return x * s, s * (1.0 + x * (1.0 - s))


def _gelu(x):
    return 0.5 * x * (1.0 + jnp.tanh(GELU_C * (x + 0.044715 * (x * x * x))))


def _gelu_grad(x):
    x2 = x * x
    th = jnp.tanh(GELU_C * (x + 0.044715 * (x * x2)))
    return 0.5 * x * (1.0 + th), 0.5 * (1.0 + th) + 0.5 * x * (1.0 - th * th) * (GELU_C * (1.0 + 3.0 * 0.044715 * x2))


def _ln_stats(x):
    mu = jnp.mean(x, axis=-1, keepdims=True)
    d = x - mu
    var = jnp.mean(d * d, axis=-1, keepdims=True)
    rs = lax.rsqrt(var + LN_EPS)
    return d * rs, rs


def _ln_bwd(dxh, xh, rs):
    return rs * (dxh - jnp.mean(dxh, axis=-1, keepdims=True) - xh * jnp.mean(dxh * xh, axis=-1, keepdims=True))


def _mm(a):
    return a.astype(MM_DTYPE)


def _dot(a, b):
    return jnp.dot(_mm(a), _mm(b), preferred_element_type=F32)


def _dot_nt(a, b):
    return lax.dot_general(_mm(a), _mm(b), (((1,), (1,)), ((), ())), preferred_element_type=F32)


def _dot_tn(a, b):
    return lax.dot_general(_mm(a), _mm(b), (((0,), (0,)), ((), ())), preferred_element_type=F32)


def _rowsum(x):
    return jnp.sum(x, axis=0, keepdims=True)


def _in_proj(x2, w, b):
    t, d = x2.shape
    n = w.shape[1]
    tm = min(ROW_TILE, t)
    nc = 768

    def body(x_ref, w_ref, b_ref, z_ref):
        xb = _mm(x_ref[...])
        for j in range(n // nc):
            cs = slice(j * nc, (j + 1) * nc)
            z_ref[:, cs] = jnp.dot(xb, w_ref[:, cs], preferred_element_type=F32) + b_ref[:, cs]

    return pl.pallas_call(
        body, name="in_proj", grid=(t // tm,),
        in_specs=[pl.BlockSpec((tm, d), lambda i: (i, 0)), pl.BlockSpec((d, n), lambda i: (0, 0)),
                  pl.BlockSpec((1, n), lambda i: (0, 0))],
        out_specs=pl.BlockSpec((tm, n), lambda i: (i, 0)),
        out_shape=jax.ShapeDtypeStruct((t, n), F32),
        compiler_params=_cparams(("parallel",)),
    )(x2, w, b)


def _in_proj_bwd_dx(dxr, dz, w, after):
    t, d = dxr.shape
    n = w.shape[1]
    tm = min(ROW_TILE, t)

    def body(dxr_ref, dz_ref, w_ref, after_ref, dx_ref):
        dx_ref[...] = dxr_ref[...] + _dot_nt(dz_ref[...], w_ref[...])

    return pl.pallas_call(
        body, name="in_proj_bwd_dx", grid=(t // tm,),
        in_specs=[pl.BlockSpec((tm, d), lambda i: (i, 0)), pl.BlockSpec((tm, n), lambda i: (i, 0)),
                  pl.BlockSpec((d, n), lambda i: (0, 0)), pl.BlockSpec((8, 128), lambda i: (0, 0))],
        out_specs=pl.BlockSpec((tm, d), lambda i: (i, 0)),
        out_shape=jax.ShapeDtypeStruct((t, d), F32),
        compiler_params=_cparams(("parallel",)),
    )(dxr, dz, w, after)


def _in_proj_bwd_dw(x2, dz):
    t, d = x2.shape
    n = dz.shape[1]
    tm = min(ROW_TILE, t)
    nc = 768

    def body(x_ref, dz_ref, dw_ref, done_ref):
        @pl.when(pl.program_id(0) == 0)
        def _():
            dw_ref[...] = jnp.zeros_like(dw_ref)
            done_ref[...] = jnp.zeros_like(done_ref)
        xb = _mm(x_ref[...])
        for j in range(n // nc):
            cs = slice(j * nc, (j + 1) * nc)
            dw_ref[:, cs] += _dot_tn(xb, dz_ref[:, cs])

    return pl.pallas_call(
        body, name="in_proj_bwd_dw", grid=(t // tm,),
        in_specs=[pl.BlockSpec((tm, d), lambda i: (i, 0)), pl.BlockSpec((tm, n), lambda i: (i, 0))],
        out_specs=[pl.BlockSpec((d, n), lambda i: (0, 0)), pl.BlockSpec((8, 128), lambda i: (0, 0))],
        out_shape=[jax.ShapeDtypeStruct((d, n), F32), jax.ShapeDtypeStruct((8, 128), F32)],
        compiler_params=_cparams(("arbitrary",)),
    )(x2, dz)


def _halo_specs(ts, s_len, cols, left=True):
    per = ts // HALO
    last = s_len // HALO - 1
    if left:
        return pl.BlockSpec((1, HALO, cols), lambda b, s: (b, jnp.maximum(s * per - 1, 0), 0))
    return pl.BlockSpec((1, HALO, cols), lambda b, s: (b, jnp.minimum((s + 1) * per, last), 0))


def _build_shifts(src_ref, sh_ref, rows):
    for r in range(1, 8):
        sh_ref[r - 1, 0:rows, :] = src_ref[r:r + rows, :]


def _shifted(src_ref, sh_ref, r, start, n):
    if r == 0:
        return src_ref[pl.ds(start, n), :]
    return sh_ref[r - 1, pl.ds(start, n), :]


def _tril_masks():
    ri = lax.broadcasted_iota(jnp.int32, (GBLK, GBLK), 0)
    ci = lax.broadcasted_iota(jnp.int32, (GBLK, GBLK), 1)
    return ri >= ci, ci >= ri


def _spatial(w_ref, keep, vb):
    lane = lax.broadcasted_iota(jnp.int32, (GBLK, GBLK), 1)
    outs = []
    for p in range(4):
        xs = vb[:, p * GBLK:(p + 1) * GBLK]
        r0 = jnp.dot(_mm(jnp.where(keep, w_ref[2 * p], 0.0)), xs, preferred_element_type=F32)
        r1 = jnp.dot(_mm(jnp.where(keep, w_ref[2 * p + 1], 0.0)), xs, preferred_element_type=F32)
        outs.append(jnp.where(lane < 64, r0, r1))
    return jnp.concatenate(outs, axis=1)


def _even_fwd(z3, cw, vec, ws, bsf):
    bsz, s_len, _ = z3.shape
    ts = min(SEQ_TILE, s_len)
    ext_rows = ts + HALO

    def body(z_ref, zl_ref, cw_ref, vec_ref, ws_ref, bsf_ref, y_ref, a1_ref, ext_ref, sh_ref):
        s = pl.program_id(1)
        hl = zl_ref[0]
        a0h = hl[:, 0:W_BR] * _sigmoid(hl[:, W_BR:2 * W_BR])
        ext_ref[0:HALO, :] = jnp.where(s > 0, a0h, 0.0)
        for a, b in _row_chunks(0, ts):
            ext_ref[HALO + a:HALO + b, :] = z_ref[0, a:b, 0:W_BR] * _sigmoid(z_ref[0, a:b, W_BR:2 * W_BR])
        ext_ref[ext_rows:ext_rows + 8, :] = jnp.zeros((8, W_BR), F32)
        _build_shifts(ext_ref, sh_ref, ext_rows)

        def conv_chunk(ci, carry):
            base = pl.multiple_of(ci * CONV_CHUNK, CONV_CHUNK)
            acc = jnp.zeros((CONV_CHUNK, W_BR), F32) + vec_ref[0:1, :]
            for k in range(KA):
                q, r = divmod(2 + k, 8)
                acc = acc + _shifted(ext_ref, sh_ref, r, base + 8 * q, CONV_CHUNK) * cw_ref[k:k + 1, :]
            a1_ref[0, pl.ds(base, CONV_CHUNK), :] = acc
            return carry

        lax.fori_loop(0, ts // CONV_CHUNK, conv_chunk, 0)
        keep, _ = _tril_masks()

        def block(bi, carry):
            rows = pl.ds(pl.multiple_of(bi * GBLK, GBLK), GBLK)
            xh, _ = _ln_stats(a1_ref[0, rows, :])
            a = _silu(xh * vec_ref[1:2, :] + vec_ref[2:3, :]) * _silu(z_ref[0, rows, 2 * W_BR:3 * W_BR])
            y_ref[0, rows, 0:W_BR] = a.astype(y_ref.dtype)
            ua = _gelu(z_ref[0, rows, 3 * W_BR:4 * W_BR])
            vh, _ = _ln_stats(_gelu(z_ref[0, rows, 4 * W_BR:5 * W_BR]))
            vl = vh * vec_ref[3:4, :] + vec_ref[4:5, :]
            sg = _spatial(ws_ref, keep, _mm(vl)) + bsf_ref[...]
            g = ua * sg * _silu(z_ref[0, rows, 5 * W_BR:6 * W_BR])
            y_ref[0, rows, W_BR:2 * W_BR] = g.astype(y_ref.dtype)
            return carry

        lax.fori_loop(0, ts // GBLK, block, 0)

    full = lambda shape: pl.BlockSpec(shape, lambda b, s: (0,) * len(shape))
    return pl.pallas_call(
        body, name="even_fwd", grid=(bsz, s_len // ts),
        in_specs=[pl.BlockSpec((1, ts, N_COLS), lambda b, s: (b, s, 0)), _halo_specs(ts, s_len, 2 * W_BR),
                  full((32, W_BR)), full((8, W_BR)), full((8, GBLK, GBLK)), full((GBLK, W_BR))],
        out_specs=[pl.BlockSpec((1, ts, 2 * W_BR), lambda b, s: (b, s, 0)), pl.BlockSpec((1, ts, W_BR), lambda b, s: (b, s, 0))],
        out_shape=[jax.ShapeDtypeStruct((bsz, s_len, 2 * W_BR), MM_DTYPE), jax.ShapeDtypeStruct((bsz, s_len, W_BR), F32)],
        scratch_shapes=[pltpu.VMEM((ext_rows + 8, W_BR), F32), pltpu.VMEM((7, ext_rows, W_BR), F32)],
        compiler_params=_cparams(("parallel", "parallel")),
    )(z3, z3, cw, vec, ws, bsf)


def _even_bwd(z3, dy3, a13, cw, vec, ws, wst, bsf):
    bsz, s_len, _ = z3.shape
    ts = min(SEQ_TILE, s_len)
    n_s = s_len // ts
    ext_rows = ts + 2 * HALO
    a_rows = ts + HALO

    def body(z_ref, zl_ref, zr_ref, dy_ref, dyr_ref, a1_ref, a1r_ref, cw_ref, vec_ref, ws_ref, wst_ref, bsf_ref,
             dz_ref, dbin_ref, dcw_ref, dvec_ref, dws_ref, dbsf_ref,
             ext_ref, sh_ref, ag_ref, dya_ref, d_ref, accw_ref):
        b = pl.program_id(0)
        s = pl.program_id(1)

        @pl.when((b == 0) & (s == 0))
        def _():
            dbin_ref[...] = jnp.zeros_like(dbin_ref)
            dcw_ref[...] = jnp.zeros_like(dcw_ref)
            dvec_ref[...] = jnp.zeros_like(dvec_ref)
            dws_ref[...] = jnp.zeros_like(dws_ref)
            dbsf_ref[...] = jnp.zeros_like(dbsf_ref)

        has_right = s < n_s - 1
        hl = zl_ref[0]
        hr = zr_ref[0]
        ext_ref[0:HALO, :] = jnp.where(s > 0, hl[:, 0:W_BR] * _sigmoid(hl[:, W_BR:2 * W_BR]), 0.0)
        for a, b in _row_chunks(0, ts):
            ext_ref[HALO + a:HALO + b, :] = z_ref[0, a:b, 0:W_BR] * _sigmoid(z_ref[0, a:b, W_BR:2 * W_BR])
            ag_ref[a:b, :] = z_ref[0, a:b, 2 * W_BR:3 * W_BR]
            dya_ref[a:b, :] = dy_ref[0, a:b, 0:W_BR]
        ext_ref[HALO + ts:ext_rows, :] = hr[:, 0:W_BR] * _sigmoid(hr[:, W_BR:2 * W_BR])
        ext_ref[ext_rows:ext_rows + 8, :] = jnp.zeros((8, W_BR), F32)
        ag_ref[ts:a_rows, :] = hr[:, 2 * W_BR:3 * W_BR]
        dya_ref[ts:a_rows, :] = dyr_ref[0][:, 0:W_BR]
        _build_shifts(ext_ref, sh_ref, ext_rows)

        def a_chunk(base, n, main):
            rows = pl.ds(base, n)
            xh, rs = _ln_stats(a1_ref[0, rows, :] if main else a1r_ref[0])
            ln = xh * vec_ref[1:2, :] + vec_ref[2:3, :]
            sl, dsl = _silu_grad(ln)
            sgt, dsgt = _silu_grad(ag_ref[rows, :])
            dya = dya_ref[rows, :]
            dln = dya * sgt * dsl
            da1 = _ln_bwd(dln * vec_ref[1:2, :], xh, rs)
            if main:
                d_ref[rows, :] = da1
                dag = dya * sl * dsgt
                dz_ref[0, rows, 2 * W_BR:3 * W_BR] = dag.astype(dz_ref.dtype)
                dbin_ref[0:1, 2 * W_BR:3 * W_BR] += _rowsum(dag)
                dvec_ref[0:1, :] += _rowsum(da1)
                dvec_ref[1:2, :] += _rowsum(dln * xh)
                dvec_ref[2:3, :] += _rowsum(dln)
            else:
                d_ref[rows, :] = jnp.where(has_right, da1, 0.0)

        def a_main(ci, carry):
            a_chunk(pl.multiple_of(ci * GBLK, GBLK), GBLK, True)
            return carry

        lax.fori_loop(0, ts // GBLK, a_main, 0)
        a_chunk(ts, HALO, False)
        d_ref[a_rows:a_rows + 8, :] = jnp.zeros((8, W_BR), F32)

        accw_ref[...] = jnp.zeros_like(accw_ref)

        def dw_chunk(ci, carry):
            base = pl.multiple_of(ci * DW_CHUNK, DW_CHUNK)
            d = d_ref[pl.ds(base, DW_CHUNK), :]
            for k in range(KA):
                q, r = divmod(2 + k, 8)
                prod = d * _shifted(ext_ref, sh_ref, r, base + 8 * q, DW_CHUNK)
                accw_ref[k] += jnp.sum(prod.reshape(DW_CHUNK // 8, 8, W_BR), axis=0)
            return carry

        lax.fori_loop(0, ts // DW_CHUNK, dw_chunk, 0)
        dcw_ref[...] += jnp.sum(accw_ref[...], axis=1)

        _build_shifts(d_ref, sh_ref, a_rows)

        def dx_chunk(ci, carry):
            base = pl.multiple_of(ci * CONV_CHUNK, CONV_CHUNK)
            rows = pl.ds(base, CONV_CHUNK)
            acc = jnp.zeros((CONV_CHUNK, W_BR), F32)
            for m in range(KA):
                q, r = divmod(m, 8)
                acc = acc + _shifted(d_ref, sh_ref, r, base + 8 * q, CONV_CHUNK) * cw_ref[KA - 1 - m:KA - m, :]
            aval = z_ref[0, rows, 0:W_BR]
            sg = _sigmoid(z_ref[0, rows, W_BR:2 * W_BR])
            dval = acc * sg
            dglu = acc * aval * sg * (1.0 - sg)
            dz_ref[0, rows, 0:W_BR] = dval.astype(dz_ref.dtype)
            dz_ref[0, rows, W_BR:2 * W_BR] = dglu.astype(dz_ref.dtype)
            dbin_ref[0:1, 0:W_BR] += _rowsum(dval)
            dbin_ref[0:1, W_BR:2 * W_BR] += _rowsum(dglu)
            return carry

        lax.fori_loop(0, ts // CONV_CHUNK, dx_chunk, 0)

        keep, keep_t = _tril_masks()
        lane = lax.broadcasted_iota(jnp.int32, (GBLK, GBLK), 1)

        def block(bi, carry):
            rows = pl.ds(pl.multiple_of(bi * GBLK, GBLK), GBLK)
            ua, dua = _gelu_grad(z_ref[0, rows, 3 * W_BR:4 * W_BR])
            va, dva = _gelu_grad(z_ref[0, rows, 4 * W_BR:5 * W_BR])
            sgt, dsgt = _silu_grad(z_ref[0, rows, 5 * W_BR:6 * W_BR])
            vh, rs = _ln_stats(va)
            vlb = _mm(vh * vec_ref[3:4, :] + vec_ref[4:5, :])
            sg = _spatial(ws_ref, keep, vlb) + bsf_ref[...]
            dyg = dy_ref[0, rows, W_BR:2 * W_BR]
            du = dyg * sg * sgt * dua
            dsg = dyg * ua * sgt
            dgg = dyg * ua * sg * dsgt
            dvl = _spatial(wst_ref, keep_t, _mm(dsg))
            for p in range(4):
                dsp = dsg[:, p * GBLK:(p + 1) * GBLK]
                vlp = vlb[:, p * GBLK:(p + 1) * GBLK]
                dws_ref[2 * p] += jnp.where(keep, _dot_nt(jnp.where(lane < 64, dsp, 0.0), vlp), 0.0)
                dws_ref[2 * p + 1] += jnp.where(keep, _dot_nt(jnp.where(lane >= 64, dsp, 0.0), vlp), 0.0)
            dbsf_ref[...] += dsg
            dvec_ref[3:4, :] += _rowsum(dvl * vh)
            dvec_ref[4:5, :] += _rowsum(dvl)
            dv = _ln_bwd(dvl * vec_ref[3:4, :], vh, rs) * dva
            dz_ref[0, rows, 3 * W_BR:4 * W_BR] = du.astype(dz_ref.dtype)
            dz_ref[0, rows, 4 * W_BR:5 * W_BR] = dv.astype(dz_ref.dtype)
            dz_ref[0, rows, 5 * W_BR:6 * W_BR] = dgg.astype(dz_ref.dtype)
            dbin_ref[0:1, 3 * W_BR:4 * W_BR] += _rowsum(du)
            dbin_ref[0:1, 4 * W_BR:5 * W_BR] += _rowsum(dv)
            dbin_ref[0:1, 5 * W_BR:6 * W_BR] += _rowsum(dgg)
            return carry

        lax.fori_loop(0, ts // GBLK, block, 0)

    full = lambda shape: pl.BlockSpec(shape, lambda b, s: (0,) * len(shape))
    acc_shapes = [(1, N_COLS), (32, W_BR), (8, W_BR), (8, GBLK, GBLK), (GBLK, W_BR)]
    return pl.pallas_call(
        body, name="even_bwd", grid=(bsz, n_s),
        in_specs=[pl.BlockSpec((1, ts, N_COLS), lambda b, s: (b, s, 0)),
                  _halo_specs(ts, s_len, N_COLS, True), _halo_specs(ts, s_len, N_COLS, False),
                  pl.BlockSpec((1, ts, 2 * W_BR), lambda b, s: (b, s, 0)), _halo_specs(ts, s_len, 2 * W_BR, False),
                  pl.BlockSpec((1, ts, W_BR), lambda b, s: (b, s, 0)), _halo_specs(ts, s_len, W_BR, False),
                  full((32, W_BR)), full((8, W_BR)), full((8, GBLK, GBLK)), full((8, GBLK, GBLK)), full((GBLK, W_BR))],
        out_specs=[pl.BlockSpec((1, ts, N_COLS), lambda b, s: (b, s, 0))] + [full(sh) for sh in acc_shapes],
        out_shape=[jax.ShapeDtypeStruct((bsz, s_len, N_COLS), MM_DTYPE)] + [jax.ShapeDtypeStruct(sh, F32) for sh in acc_shapes],
        scratch_shapes=[pltpu.VMEM((ext_rows + 8, W_BR), F32), pltpu.VMEM((7, ext_rows, W_BR), F32),
                        pltpu.VMEM((a_rows, W_BR), F32), pltpu.VMEM((a_rows, W_BR), F32),
                        pltpu.VMEM((a_rows + 8, W_BR), F32), pltpu.VMEM((32, 8, W_BR), F32)],
        compiler_params=_cparams(("arbitrary", "arbitrary")),
    )(z3, z3, z3, dy3, dy3, a13, a13, cw, vec, ws, wst, bsf)


def _row_chunks(lo, hi):
    return [(a, min(a + ROW_CHUNK, hi)) for a in range(lo, hi, ROW_CHUNK)]


def _pool_stages(e_refs, rows):
    e0, e1, e2, e3, e4 = e_refs
    for a, b in _row_chunks(8, rows):
        e1[a:b, :] = e0[a:b, :] + e0[a - 1:b - 1, :]
    for a, b in _row_chunks(16, rows):
        e2[a:b, GBLK:] = e1[a:b, GBLK:] + e1[a - 2:b - 2, GBLK:]
    for a, b in _row_chunks(24, rows):
        e3[a:b, 2 * GBLK:] = e2[a:b, 2 * GBLK:] + e2[a - 4:b - 4, 2 * GBLK:]
    for a, b in _row_chunks(32, rows):
        e4[a:b, 3 * GBLK:] = e3[a:b, 3 * GBLK:] + e3[a - 8:b - 8, 3 * GBLK:]


def _pool_counts(start, n):
    pos = (start + 1 + lax.broadcasted_iota(jnp.int32, (n, 1), 0)).astype(F32)
    return [jnp.minimum(pos, float(w)) for w in POOL_WINDOWS]


def _pooled_into(e_refs, pooled_ref, s, ts):
    for a, b in _row_chunks(0, ts):
        cnt = _pool_counts(s * ts + a, b - a)
        for g in range(4):
            cs = slice(g * GBLK, (g + 1) * GBLK)
            pooled_ref[a:b, cs] = e_refs[g + 1][HALO + a:HALO + b, cs] / cnt[g] - e_refs[0][HALO + a:HALO + b, cs]


def _odd_prologue(z_ref, hl, s, ts, e_refs, pooled_ref, dext_ref, ec_ref, vec_ref):
    e0 = e_refs[0]
    e0[0:HALO, :] = jnp.where(s > 0, hl[:, 0:W_BR], 0.0)
    dext_ref[0:HALO, :] = jnp.where(s > 0, hl[:, 2 * W_BR:3 * W_BR] * hl[:, 4 * W_BR:5 * W_BR], 0.0)
    for a, b in _row_chunks(0, ts):
        e0[HALO + a:HALO + b, :] = z_ref[0, a:b, 0:W_BR]
        dext_ref[HALO + a:HALO + b, :] = z_ref[0, a:b, 2 * W_BR:3 * W_BR] * z_ref[0, a:b, 4 * W_BR:5 * W_BR]
    _pool_stages(e_refs, ts + HALO)
    _pooled_into(e_refs, pooled_ref, s, ts)
    for a, b in _row_chunks(0, ts):
        ec_ref[a:b, :] = (vec_ref[1:2, :] * dext_ref[HALO - 2 + a:HALO - 2 + b, :] + vec_ref[2:3, :] * dext_ref[HALO - 1 + a:HALO - 1 + b, :]
                          + vec_ref[3:4, :] * dext_ref[HALO + a:HALO + b, :])


def _odd_fwd(z3, wp, vec):
    bsz, s_len, _ = z3.shape
    ts = min(SEQ_TILE, s_len)
    ext_rows = ts + HALO

    def body(z_ref, zl_ref, wp_ref, vec_ref, y_ref, e0, e1, e2, e3, e4, pooled_ref, dext_ref, ec_ref):
        s = pl.program_id(1)
        _odd_prologue(z_ref, zl_ref[0], s, ts, (e0, e1, e2, e3, e4), pooled_ref, dext_ref, ec_ref, vec_ref)

        def block(bi, carry):
            rows = pl.ds(pl.multiple_of(bi * GBLK, GBLK), GBLK)
            pb = _mm(pooled_ref[rows, :])
            cpre = jnp.concatenate([jnp.dot(pb[:, g * GBLK:(g + 1) * GBLK], wp_ref[g], preferred_element_type=F32)
                                    for g in range(4)], axis=1)
            c = cpre * vec_ref[0:1, :] * _silu(z_ref[0, rows, W_BR:2 * W_BR])
            d = z_ref[0, rows, 3 * W_BR:4 * W_BR] * ec_ref[rows, :] * _silu(z_ref[0, rows, 5 * W_BR:6 * W_BR])
            y_ref[0, rows, 0:W_BR] = c.astype(y_ref.dtype)
            y_ref[0, rows, W_BR:2 * W_BR] = d.astype(y_ref.dtype)
            return carry

        lax.fori_loop(0, ts // GBLK, block, 0)

    full = lambda shape: pl.BlockSpec(shape, lambda b, s: (0,) * len(shape))
    ebuf = pltpu.VMEM((ext_rows, W_BR), F32)
    return pl.pallas_call(
        body, name="odd_fwd", grid=(bsz, s_len // ts),
        in_specs=[pl.BlockSpec((1, ts, N_COLS), lambda b, s: (b, s, 0)), _halo_specs(ts, s_len, N_COLS),
                  full((4, GBLK, GBLK)), full((8, W_BR))],
        out_specs=pl.BlockSpec((1, ts, 2 * W_BR), lambda b, s: (b, s, 0)),
        out_shape=jax.ShapeDtypeStruct((bsz, s_len, 2 * W_BR), MM_DTYPE),
        scratch_shapes=[ebuf, ebuf, ebuf, ebuf, ebuf, pltpu.VMEM((ts, W_BR), F32), ebuf, pltpu.VMEM((ts, W_BR), F32)],
        compiler_params=_cparams(("parallel", "parallel")),
    )(z3, z3, wp, vec)


def _odd_bwd(z3, dy3, wp, vec):
    bsz, s_len, _ = z3.shape
    ts = min(SEQ_TILE, s_len)
    n_s = s_len // ts
    ext_rows = ts + HALO

    def body(z_ref, zl_ref, zr_ref, dy_ref, dyr_ref, wp_ref, vec_ref,
             dz_ref, dbin_ref, dwp_ref, dvec_ref,
             e0, e1, e2, e3, e4, pooled_ref, dext_ref, ec_ref, q_ref, dp_ref, de_ref, f1, f2, f3, f4):
        b = pl.program_id(0)
        s = pl.program_id(1)

        @pl.when((b == 0) & (s == 0))
        def _():
            dbin_ref[...] = jnp.zeros_like(dbin_ref)
            dwp_ref[...] = jnp.zeros_like(dwp_ref)
            dvec_ref[...] = jnp.zeros_like(dvec_ref)

        has_right = s < n_s - 1
        _odd_prologue(z_ref, zl_ref[0], s, ts, (e0, e1, e2, e3, e4), pooled_ref, dext_ref, ec_ref, vec_ref)

        def grads(zc_gate, zd_b, zd_gate, dyc, dyd, rows_out, n, start, valid):
            sgt = _silu(zc_gate)
            dcpre = dyc * vec_ref[0:1, :] * sgt
            db = _mm(dcpre)
            dpool = jnp.concatenate([_dot_nt(db[:, g * GBLK:(g + 1) * GBLK], wp_ref[g]) for g in range(4)], axis=1)
            cnt = _pool_counts(start, n)
            q = jnp.concatenate([dpool[:, g * GBLK:(g + 1) * GBLK] / cnt[g] for g in range(4)], axis=1)
            de = dyd * zd_b * _silu(zd_gate)
            if valid is not None:
                q = jnp.where(valid, q, 0.0)
                de = jnp.where(valid, de, 0.0)
            q_ref[rows_out, :] = q
            dp_ref[rows_out, :] = dpool
            de_ref[rows_out, :] = de
            return dcpre

        def block(bi, carry):
            base = pl.multiple_of(bi * GBLK, GBLK)
            rows = pl.ds(base, GBLK)
            cg = z_ref[0, rows, W_BR:2 * W_BR]
            dyc = dy_ref[0, rows, 0:W_BR]
            dyd = dy_ref[0, rows, W_BR:2 * W_BR]
            d_b = z_ref[0, rows, 3 * W_BR:4 * W_BR]
            d_gate = z_ref[0, rows, 5 * W_BR:6 * W_BR]
            dcpre = grads(cg, d_b, d_gate, dyc, dyd, rows, GBLK, s * ts + base, None)
            pb = _mm(pooled_ref[rows, :])
            dcb = _mm(dcpre)
            cpre = jnp.concatenate([jnp.dot(pb[:, g * GBLK:(g + 1) * GBLK], wp_ref[g], preferred_element_type=F32)
                                    for g in range(4)], axis=1)
            for g in range(4):
                cs = slice(g * GBLK, (g + 1) * GBLK)
                dwp_ref[g] += _dot_tn(pb[:, cs], dcb[:, cs])
            sgt, dsgt = _silu_grad(cg)
            dvec_ref[0:1, :] += _rowsum(dyc * cpre * sgt)
            dcg = dyc * cpre * vec_ref[0:1, :] * dsgt
            sdt, dsdt = _silu_grad(d_gate)
            ec = ec_ref[rows, :]
            ddb = dyd * ec * sdt
            ddg = dyd * d_b * ec * dsdt
            dz_ref[0, rows, W_BR:2 * W_BR] = dcg.astype(dz_ref.dtype)
            dz_ref[0, rows, 3 * W_BR:4 * W_BR] = ddb.astype(dz_ref.dtype)
            dz_ref[0, rows, 5 * W_BR:6 * W_BR] = ddg.astype(dz_ref.dtype)
            dbin_ref[0:1, W_BR:2 * W_BR] += _rowsum(dcg)
            dbin_ref[0:1, 3 * W_BR:4 * W_BR] += _rowsum(ddb)
            dbin_ref[0:1, 5 * W_BR:6 * W_BR] += _rowsum(ddg)
            return carry

        lax.fori_loop(0, ts // GBLK, block, 0)
        hr = zr_ref[0]
        dyr = dyr_ref[0]
        grads(hr[:, W_BR:2 * W_BR], hr[:, 3 * W_BR:4 * W_BR], hr[:, 5 * W_BR:6 * W_BR], dyr[:, 0:W_BR], dyr[:, W_BR:2 * W_BR],
              slice(ts, ext_rows), HALO, (s + 1) * ts, has_right)

        for a, b in _row_chunks(0, ts + 24):
            f1[a:b, :] = q_ref[a:b, :] + q_ref[a + 1:b + 1, :]
        for a, b in _row_chunks(0, ts + 16):
            f2[a:b, GBLK:] = f1[a:b, GBLK:] + f1[a + 2:b + 2, GBLK:]
        for a, b in _row_chunks(0, ts + 8):
            f3[a:b, 2 * GBLK:] = f2[a:b, 2 * GBLK:] + f2[a + 4:b + 4, 2 * GBLK:]
        for a, b in _row_chunks(0, ts):
            f4[a:b, 3 * GBLK:] = f3[a:b, 3 * GBLK:] + f3[a + 8:b + 8, 3 * GBLK:]

        for a, b in _row_chunks(0, ts):
            for g, f in enumerate((f1, f2, f3, f4)):
                cs = slice(g * GBLK, (g + 1) * GBLK)
                dvg = f[a:b, cs] - dp_ref[a:b, cs]
                dz_ref[0, a:b, cs] = dvg.astype(dz_ref.dtype)
                dbin_ref[0:1, cs] += _rowsum(dvg)
            de = de_ref[a:b, :]
            ddc = vec_ref[1:2, :] * de_ref[a + 2:b + 2, :] + vec_ref[2:3, :] * de_ref[a + 1:b + 1, :] + vec_ref[3:4, :] * de
            ddh = ddc * z_ref[0, a:b, 4 * W_BR:5 * W_BR]
            ddcc = ddc * z_ref[0, a:b, 2 * W_BR:3 * W_BR]
            dz_ref[0, a:b, 2 * W_BR:3 * W_BR] = ddh.astype(dz_ref.dtype)
            dz_ref[0, a:b, 4 * W_BR:5 * W_BR] = ddcc.astype(dz_ref.dtype)
            dbin_ref[0:1, 2 * W_BR:3 * W_BR] += _rowsum(ddh)
            dbin_ref[0:1, 4 * W_BR:5 * W_BR] += _rowsum(ddcc)
            for k in range(3):
                dvec_ref[1 + k:2 + k, :] += _rowsum(de * dext_ref[HALO - 2 + k + a:HALO - 2 + k + b, :])

    full = lambda shape: pl.BlockSpec(shape, lambda b, s: (0,) * len(shape))
    acc_shapes = [(1, N_COLS), (4, GBLK, GBLK), (8, W_BR)]
    ebuf = pltpu.VMEM((ext_rows, W_BR), F32)
    tbuf = pltpu.VMEM((ts, W_BR), F32)
    return pl.pallas_call(
        body, name="odd_bwd", grid=(bsz, n_s),
        in_specs=[pl.BlockSpec((1, ts, N_COLS), lambda b, s: (b, s, 0)),
                  _halo_specs(ts, s_len, N_COLS, True), _halo_specs(ts, s_len, N_COLS, False),
                  pl.BlockSpec((1, ts, 2 * W_BR), lambda b, s: (b, s, 0)), _halo_specs(ts, s_len, 2 * W_BR, False),
                  full((4, GBLK, GBLK)), full((8, W_BR))],
        out_specs=[pl.BlockSpec((1, ts, N_COLS), lambda b, s: (b, s, 0))] + [full(sh) for sh in acc_shapes],
        out_shape=[jax.ShapeDtypeStruct((bsz, s_len, N_COLS), MM_DTYPE)] + [jax.ShapeDtypeStruct(sh, F32) for sh in acc_shapes],
        scratch_shapes=[ebuf, ebuf, ebuf, ebuf, ebuf, tbuf, ebuf, tbuf, ebuf, ebuf, ebuf, ebuf, ebuf, ebuf, tbuf],
        compiler_params=_cparams(("arbitrary", "arbitrary")),
    )(z3, z3, z3, dy3, dy3, wp, vec)


def _post_fwd(y2, x2, p_layer, w_out, wg, wple, vec, tgt=None):
    p_all, layer = p_layer
    t, d = x2.shape
    tm = min(ROW_TILE, t)
    last = tgt is not None

    def body(*refs):
        y_ref, x_ref, p_ref, wo_ref, wg_ref, wp_ref, vec_ref = refs[:7]
        xn_ref, r_ref, gate_ref = refs[7 + last:10 + last]
        r = ALPHA * x_ref[...] + jnp.dot(y_ref[...], wo_ref[...], preferred_element_type=F32) + vec_ref[0:1, :]
        r_ref[...] = r
        xh, _ = _ln_stats(r)
        h = xh * vec_ref[1:2, :] + vec_ref[2:3, :]
        gate = _sigmoid(_dot(h, wg_ref[...]) + vec_ref[3:4, :])
        gate_ref[...] = gate
        xn = h + gate * _dot(p_ref[...], wp_ref[...])
        if last:
            sq_ref = refs[11]

            @pl.when(pl.program_id(0) == 0)
            def _():
                sq_ref[...] = jnp.zeros_like(sq_ref)
            e = xn - refs[7][...]
            xn_ref[...] = e / float(d)
            sq_ref[...] += _rowsum(e * e)
        else:
            xn_ref[...] = xn

    row = lambda c: pl.BlockSpec((tm, c), lambda i: (i, 0))
    full = lambda shape: pl.BlockSpec(shape, lambda i: (0,) * len(shape))
    return pl.pallas_call(
        body, name="post_fwd_loss" if last else "post_fwd", grid=(t // tm,),
        in_specs=[row(d), row(d), pl.BlockSpec((None, tm, D_PLE), lambda i: (layer, i, 0)),
                  full((d, d)), full((d, d)), full((D_PLE, d)), full((8, d))] + [row(d)] * last,
        out_specs=[row(d), row(d), row(d)] + [full((1, d))] * last,
        out_shape=[jax.ShapeDtypeStruct((t, d), F32)] * 3 + [jax.ShapeDtypeStruct((1, d), F32)] * last,
        compiler_params=_cparams(("arbitrary",) if last else ("parallel",)),
    )(y2, x2, p_all, w_out, wg, wple, vec, *([tgt] if last else []))


def _post_bwd(dxn, r2, gate2, p_layer, y2, w_out, wg, wple, vec):
    p_all, layer = p_layer
    t, d = r2.shape
    tm = min(BWD_ROW_TILE, t)

    def body(dxn_ref, r_ref, gate_ref, p_ref, y_ref, wo_ref, wg_ref, wp_ref, vec_ref,
             dxr_ref, dy_ref, dwo_ref, dwg_ref, dwp_ref, dvec_ref):
        @pl.when(pl.program_id(0) == 0)
        def _():
            dwo_ref[...] = jnp.zeros_like(dwo_ref)
            dwg_ref[...] = jnp.zeros_like(dwg_ref)
            dwp_ref[...] = jnp.zeros_like(dwp_ref)
            dvec_ref[...] = jnp.zeros_like(dvec_ref)

        dxn = dxn_ref[...]
        gate = gate_ref[...]
        xh, rs = _ln_stats(r_ref[...])
        hb = _mm(xh * vec_ref[1:2, :] + vec_ref[2:3, :])
        pb = _mm(p_ref[...])
        pe = jnp.dot(pb, wp_ref[...], preferred_element_type=F32)
        dpre = dxn * pe * gate * (1.0 - gate)
        dpb = _mm(dpre)
        dh = dxn + _dot_nt(dpb, wg_ref[...])
        dwg_ref[...] += _dot_tn(hb, dpb)
        dwp_ref[...] += _dot_tn(pb, dxn * gate)
        dr = _ln_bwd(dh * vec_ref[1:2, :], xh, rs)
        drb = _mm(dr)
        dxr_ref[...] = ALPHA * dr
        dy_ref[...] = _dot_nt(drb, wo_ref[...])
        dwo_ref[...] += _dot_tn(y_ref[...], drb)
        dvec_ref[0:1, :] += _rowsum(dr)
        dvec_ref[1:2, :] += _rowsum(dh * xh)
        dvec_ref[2:3, :] += _rowsum(dh)
        dvec_ref[3:4, :] += _rowsum(dpre)

    row = lambda c: pl.BlockSpec((tm, c), lambda i: (i, 0))
    full = lambda shape: pl.BlockSpec(shape, lambda i: (0,) * len(shape), pipeline_mode=pl.Buffered(1))
    acc_shapes = [(d, d), (d, d), (D_PLE, d), (8, d)]
    return pl.pallas_call(
        body, name="post_bwd", grid=(t // tm,),
        in_specs=[row(d), row(d), row(d), pl.BlockSpec((None, tm, D_PLE), lambda i: (layer, i, 0)), row(d),
                  full((d, d)), full((d, d)), full((D_PLE, d)), full((8, d))],
        out_specs=[row(d), row(d)] + [full(sh) for sh in acc_shapes],
        out_shape=[jax.ShapeDtypeStruct((t, d), F32)] * 2 + [jax.ShapeDtypeStruct(sh, F32) for sh in acc_shapes],
        compiler_params=_cparams(("arbitrary",)),
    )(dxn, r2, gate2, p_all, y2, w_out, wg, wple, vec)


def _place():
    x, y, c = lax.axis_index("x"), lax.axis_index("y"), lax.axis_index("c")
    chips = [(1 - x, y), (x, 1 - y), (1 - x, 1 - y)]
    return x, y, c, chips


def _shard_of(ref, ax, k, width, lo=None, ln=None):
    idx = [slice(None)] * 3
    if lo is not None:
        idx[0] = pl.ds(lo, ln)
    idx[ax] = pl.ds(k * width, width)
    return ref.at[tuple(idx)]


def _remote(src, dst, ssem, rsem, dev):
    return pltpu.make_async_remote_copy(src_ref=src, dst_ref=dst, send_sem=ssem, recv_sem=rsem, device_id=dev, device_id_type=MESH)


def _place_shard(w, ax, chip, dtype=MM_DTYPE, layer=None):
    l_dim, a_dim, b_dim = w.shape
    tr = min(256, a_dim)
    per = a_dim // tr
    first = 0
    if layer is not None:
        l_dim, first = 1, layer
    shape = [l_dim, a_dim, b_dim]
    shape[ax] *= 4
    if ax == 2:
        out_spec = pl.BlockSpec((1, tr, b_dim), lambda l, i, k: (l, i, k[0]))
    else:
        out_spec = pl.BlockSpec((1, tr, b_dim), lambda l, i, k: (l, k[0] * per + i, 0))

    def body(k_ref, w_ref, o_ref):
        o_ref[...] = w_ref[...].astype(o_ref.dtype)

    return pl.pallas_call(
        body, name="place_shard",
        grid_spec=pltpu.PrefetchScalarGridSpec(
            num_scalar_prefetch=1, grid=(l_dim, per),
            in_specs=[pl.BlockSpec((1, tr, b_dim), lambda l, i, k: (first + l, i, 0))], out_specs=out_spec),
        out_shape=jax.ShapeDtypeStruct(tuple(shape), dtype),
        compiler_params=_cparams(("parallel", "parallel")),
    )(chip, w)


def _shard_copies(src_refs, dst_refs, axes, sems):
    x, y, c, chips = _place()
    j = 2 * x + y
    cps = []
    for a, (s_ref, d_ref) in enumerate(zip(src_refs, dst_refs)):
        w = d_ref.shape[axes[a]] // 4
        for q, (qx, qy) in enumerate(chips):
            ssem, rsem = sems[3 * a + q]
            cps.append(_remote(_shard_of(s_ref, axes[a], j, w), _shard_of(d_ref, axes[a], j, w), ssem, rsem, (qx, qy, c)))
    return cps


def _chip_handshake():
    x, y, c, chips = _place()
    barrier = pltpu.get_barrier_semaphore()
    for qx, qy in chips:
        pl.semaphore_signal(barrier, inc=1, device_id=(qx, qy, c), device_id_type=MESH)
    pl.semaphore_wait(barrier, 3)


def _gather_async(name, collective_id, fulls, axes):
    n = len(fulls)
    refs = [jax.new_ref(f, memory_space=pltpu.MemorySpace.HBM) for f in fulls]
    moved = sum(2 * 3 * (f.size // 4) * jnp.dtype(f.dtype).itemsize for f in fulls)

    @pl.kernel(mesh=plsc.ScalarSubcoreMesh(axis_name="seq", num_cores=1), name=name,
               scratch_types=(pltpu.SemaphoreType.DMA,) * (6 * n),
               cost_estimate=pl.CostEstimate(flops=0, transcendentals=0, bytes_accessed=moved, remote_bytes_transferred=moved),
               compiler_params=pltpu.CompilerParams(collective_id=collective_id))
    def launch(*sems):
        _chip_handshake()
        cps = _shard_copies(refs, refs, axes, [(sems[2 * k], sems[2 * k + 1]) for k in range(3 * n)])
        for cp in cps:
            cp.start()
        for cp in cps:
            cp.wait()

    launch()
    return refs


def _scatter_async(name, collective_id, grads, axes, after):
    n = len(grads)
    extra = len(after)
    outs = []
    for g, ax in zip(grads, axes):
        sh = list(g.shape)
        sh[ax] //= 4
        outs.append(jax.ShapeDtypeStruct((3,) + tuple(sh), g.dtype))

    def body(*refs):
        srcs, lands, sems = refs[:n], refs[n + extra:2 * n + extra], refs[2 * n + extra:]
        _chip_handshake()
        x, y, c, chips = _place()
        cps = []
        for a in range(n):
            w = srcs[a].shape[axes[a]] // 4
            for q, (qx, qy) in enumerate(chips):
                k = 3 * a + q
                cps.append(_remote(_shard_of(srcs[a], axes[a], 2 * qx + qy, w), lands[a].at[q], sems[2 * k], sems[2 * k + 1],
                                   (qx, qy, c)))
        for cp in cps:
            cp.start()
        for cp in cps:
            cp.wait()

    moved = sum(2 * 3 * math.prod(o.shape[1:]) * jnp.dtype(o.dtype).itemsize for o in outs)
    return pl.kernel(body, out_type=outs, mesh=plsc.ScalarSubcoreMesh(axis_name="seq", num_cores=1), name=name,
                     scratch_types=(pltpu.SemaphoreType.DMA,) * (6 * n),
                     cost_estimate=pl.CostEstimate(flops=0, transcendentals=0, bytes_accessed=moved, remote_bytes_transferred=moved),
                     compiler_params=pltpu.CompilerParams(collective_id=collective_id))(*grads, *after)


def _pair_swap_async(name, collective_id, sums):
    n = len(sums)

    def body(*refs):
        g_refs, o_refs, sems = refs[:n], refs[n:2 * n], refs[2 * n:]
        x, y, c, _ = _place()
        barrier = pltpu.get_barrier_semaphore()
        pl.semaphore_signal(barrier, inc=1, device_id=(x, y, 1 - c), device_id_type=MESH)
        pl.semaphore_wait(barrier, 1)
        cps = [_remote(g_refs[a], o_refs[a], sems[2 * a], sems[2 * a + 1], (x, y, 1 - c)) for a in range(n)]
        for cp in cps:
            cp.start()
        for cp in cps:
            cp.wait()

    moved = sum(2 * g.size * jnp.dtype(g.dtype).itemsize for g in sums)
    return pl.kernel(body, out_type=[jax.ShapeDtypeStruct(g.shape, g.dtype) for g in sums],
                     mesh=plsc.ScalarSubcoreMesh(axis_name="seq", num_cores=1), name=name,
                     scratch_types=(pltpu.SemaphoreType.DMA,) * (2 * n),
                     cost_estimate=pl.CostEstimate(flops=0, transcendentals=0, bytes_accessed=moved, remote_bytes_transferred=moved),
                     compiler_params=pltpu.CompilerParams(collective_id=collective_id))(*sums)


def _pair_exchange(grads, small):
    n = len(grads)
    outs = [jax.ShapeDtypeStruct((g.shape[0] // 2,) + g.shape[1:], g.dtype) for g in grads]
    outs.append(jax.ShapeDtypeStruct((small.shape[0] // 2, small.shape[1]), small.dtype))

    def body(*refs):
        g_refs, o_refs = refs[:n + 1], refs[n + 1:2 * n + 2]
        ssem, rsem = refs[2 * n + 2:]
        x, y, c, _ = _place()
        cps = []
        for a in range(n + 1):
            lh = g_refs[a].shape[0] // 2
            cp = _remote(g_refs[a].at[pl.ds((1 - c) * lh, lh)], o_refs[a], ssem.at[a], rsem.at[a], (x, y, 1 - c))
            cp.start()
            cps.append(cp)
        for cp in cps:
            cp.wait()

    return pl.pallas_call(
        body, name="pair_exchange", in_specs=[ANY] * (n + 1), out_specs=[ANY] * (n + 1), out_shape=outs,
        scratch_shapes=[pltpu.SemaphoreType.DMA((n + 1,)), pltpu.SemaphoreType.DMA((n + 1,))],
        compiler_params=pltpu.CompilerParams(has_side_effects=True),
    )(*grads, small)


def _chip_scatter(sums, axes, small):
    n = len(sums)
    outs = []
    for g, ax in zip(sums, axes):
        sh = list(g.shape)
        sh[ax] //= 4
        outs.append(jax.ShapeDtypeStruct((3,) + tuple(sh), g.dtype))
    rq = small.shape[0] // 4
    outs.append(jax.ShapeDtypeStruct((3, rq, small.shape[1]), small.dtype))

    def body(*refs):
        g_refs, o_refs = refs[:n + 1], refs[n + 1:2 * n + 2]
        ssem, rsem = refs[2 * n + 2:]
        x, y, c, chips = _place()
        cps = []
        for a in range(n + 1):
            for q, (qx, qy) in enumerate(chips):
                k = 2 * qx + qy
                if a < n:
                    src = _shard_of(g_refs[a], axes[a], k, g_refs[a].shape[axes[a]] // 4)
                else:
                    src = g_refs[a].at[pl.ds(k * rq, rq)]
                cp = _remote(src, o_refs[a].at[q], ssem.at[a * 3 + q], rsem.at[a * 3 + q], (qx, qy, c))
                cp.start()
                cps.append(cp)
        for cp in cps:
            cp.wait()

    return pl.pallas_call(
        body, name="chip_scatter", in_specs=[ANY] * (n + 1), out_specs=[ANY] * (n + 1), out_shape=outs,
        scratch_shapes=[pltpu.SemaphoreType.DMA((3 * n + 3,)), pltpu.SemaphoreType.DMA((3 * n + 3,))],
        compiler_params=pltpu.CompilerParams(has_side_effects=True),
    )(*sums, small)


def _final_exchange(reds, small):
    n = len(reds)
    flips = [(fx, fy, fc) for fx in (0, 1) for fy in (0, 1) for fc in (0, 1)][1:]

    def body(*refs):
        g_refs, o_refs = refs[:n + 1], refs[n + 1:2 * n + 2]
        ssem, rsem = refs[2 * n + 2:]
        x, y, c, _ = _place()
        cps = []
        for a in range(n):
            lh = g_refs[a].shape[0] // 2
            cp = _remote(g_refs[a].at[pl.ds(c * lh, lh)], o_refs[a].at[pl.ds(c * lh, lh)], ssem.at[a], rsem.at[a], (x, y, 1 - c))
            cp.start()
            cps.append(cp)
        mine = 4 * c + 2 * x + y
        for f, (fx, fy, fc) in enumerate(flips):
            cp = _remote(g_refs[n].at[mine], o_refs[n].at[mine], ssem.at[n + f], rsem.at[n + f], (x ^ fx, y ^ fy, c ^ fc))
            cp.start()
            cps.append(cp)
        for cp in cps:
            cp.wait()

    return pl.pallas_call(
        body, name="final_exchange", in_specs=[ANY] * (n + 1), out_specs=[ANY] * (n + 1),
        out_shape=[jax.ShapeDtypeStruct(g.shape, g.dtype) for g in reds] + [jax.ShapeDtypeStruct(small.shape, small.dtype)],
        input_output_aliases={a: a for a in range(n + 1)},
        scratch_shapes=[pltpu.SemaphoreType.DMA((n + 7,)), pltpu.SemaphoreType.DMA((n + 7,))],
        compiler_params=pltpu.CompilerParams(has_side_effects=True),
    )(*reds, small)


def _small_allreduce(small):
    r = small.shape[0]
    rh, rq = r // 2, r // 8
    flips = [(fx, fy, fc) for fx in (0, 1) for fy in (0, 1) for fc in (0, 1)][1:]

    def body(g_ref, out_ref, pair_ref, chip_ref, s1_ref, ssem, rsem):
        x, y, c, chips = _place()
        cp = _remote(g_ref.at[pl.ds((1 - c) * rh, rh)], pair_ref, ssem.at[0], rsem.at[0], (x, y, 1 - c))
        cp.start()
        cp.wait()
        s1_ref[...] = g_ref[pl.ds(pl.multiple_of(c * rh, 8), rh), :] + pair_ref[...]
        cps = [_remote(s1_ref.at[pl.ds((2 * qx + qy) * rq, rq)], chip_ref.at[q], ssem.at[1 + q], rsem.at[1 + q], (qx, qy, c))
               for q, (qx, qy) in enumerate(chips)]
        for cp in cps:
            cp.start()
        for cp in cps:
            cp.wait()
        mine = out_ref.at[pl.ds(pl.multiple_of((4 * c + 2 * x + y) * rq, 8), rq)]
        mine[...] = ((s1_ref[pl.ds(pl.multiple_of((2 * x + y) * rq, 8), rq), :] + chip_ref[0]) + chip_ref[1]) + chip_ref[2]
        cps = [_remote(mine, mine, ssem.at[4 + f], rsem.at[4 + f], (x ^ fx, y ^ fy, c ^ fc)) for f, (fx, fy, fc) in enumerate(flips)]
        for cp in cps:
            cp.start()
        for cp in cps:
            cp.wait()

    vm = pl.BlockSpec(memory_space=pltpu.VMEM)
    return pl.pallas_call(
        body, name="small_allreduce", in_specs=[vm], out_specs=vm, out_shape=jax.ShapeDtypeStruct(small.shape, F32),
        scratch_shapes=[pltpu.VMEM((rh, 128), F32), pltpu.VMEM((3, rq, 128), F32), pltpu.VMEM((rh, 128), F32),
                        pltpu.SemaphoreType.DMA((11,)), pltpu.SemaphoreType.DMA((11,))],
        compiler_params=pltpu.CompilerParams(has_side_effects=True, vmem_limit_bytes=VMEM_LIMIT),
    )(small)


def _to_wire(g):
    _, a_dim, b_dim = g.shape
    tr = min(256, a_dim)

    def body(g_ref, o_ref, done_ref):
        o_ref[...] = g_ref[...].astype(o_ref.dtype)
        done_ref[...] = jnp.zeros_like(done_ref)

    blk = pl.BlockSpec((1, tr, b_dim), lambda i: (0, i, 0))
    return pl.pallas_call(
        body, name="to_wire", grid=(a_dim // tr,), in_specs=[blk], out_specs=[blk, pl.BlockSpec((8, 128), lambda i: (0, 0))],
        out_shape=[jax.ShapeDtypeStruct(g.shape, WIRE_DTYPE), jax.ShapeDtypeStruct((8, 128), F32)],
        compiler_params=_cparams(("arbitrary",)),
    )(g)


def _chip_sum(owns, gots, ax, chip, after=None):
    n_layers = len(owns)
    tail = [] if after is None else [after]
    _, _, a_dim, b_dim = gots[0].shape
    tr = min(256, a_dim)
    per = a_dim // tr

    def own_spec(layer):
        if ax == 2:
            return pl.BlockSpec((1, tr, b_dim), lambda l, i, k: (0, jnp.where(l == layer, i, 0), k[0]))
        return pl.BlockSpec((1, tr, b_dim), lambda l, i, k: (0, k[0] * per + jnp.where(l == layer, i, 0), 0))

    def got_spec(layer):
        return pl.BlockSpec((3, 1, tr, b_dim), lambda l, i, k: (0, 0, jnp.where(l == layer, i, 0), 0))

    def body(k_ref, *refs):
        s_ref = refs[-1]
        for layer in range(n_layers):
            @pl.when(pl.program_id(0) == layer)
            def _(own_ref=refs[layer], got_ref=refs[n_layers + layer]):
                s_ref[...] = ((own_ref[...].astype(F32) + got_ref[0].astype(F32)) + got_ref[1].astype(F32)) + got_ref[2].astype(F32)

    return pl.pallas_call(
        body, name="chip_sum",
        grid_spec=pltpu.PrefetchScalarGridSpec(
            num_scalar_prefetch=1, grid=(n_layers, per),
            in_specs=[own_spec(l) for l in range(n_layers)] + [got_spec(l) for l in range(n_layers)]
            + [pl.BlockSpec((8, 128), lambda l, i, k: (0, 0))] * len(tail),
            out_specs=pl.BlockSpec((1, tr, b_dim), lambda l, i, k: (l, i, 0))),
        out_shape=jax.ShapeDtypeStruct((n_layers, a_dim, b_dim), F32),
        compiler_params=_cparams(("arbitrary", "arbitrary")),
    )(chip, *owns, *gots, *tail)


def _small_pair_sum(small, got, half):
    rh = small.shape[0] // 2

    def body(h_ref, g_ref, o_ref, s_ref):
        s_ref[...] = g_ref[...] + o_ref[...]

    return pl.pallas_call(
        body, name="small_pair_sum",
        grid_spec=pltpu.PrefetchScalarGridSpec(
            num_scalar_prefetch=1, grid=(1,),
            in_specs=[pl.BlockSpec((rh, 128), lambda i, h: (h[0], 0)), pl.BlockSpec((rh, 128), lambda i, h: (0, 0))],
            out_specs=pl.BlockSpec((rh, 128), lambda i, h: (0, 0))),
        out_shape=jax.ShapeDtypeStruct((rh, 128), F32),
    )(half, small, got)


def _small_chip_sum(s1, got, pos):
    rq = got.shape[1]

    def body(k_ref, own_ref, got_ref, s_ref):
        s_ref[0] = ((own_ref[...] + got_ref[0]) + got_ref[1]) + got_ref[2]

    return pl.pallas_call(
        body, name="small_chip_sum",
        grid_spec=pltpu.PrefetchScalarGridSpec(
            num_scalar_prefetch=1, grid=(1,),
            in_specs=[pl.BlockSpec((rq, 128), lambda i, k: (k[0], 0)), pl.BlockSpec((3, rq, 128), lambda i, k: (0, 0, 0))],
            out_specs=pl.BlockSpec((1, rq, 128), lambda i, k: (4 * k[1] + k[0], 0, 0))),
        out_shape=jax.ShapeDtypeStruct((8, rq, 128), F32),
    )(pos, s1, got)


def _adam_math(w, g, m, v):
    m = ADAM_B1 * m + (1.0 - ADAM_B1) * g
    v = ADAM_B2 * v + (1.0 - ADAM_B2) * (g * g)
    m_hat = m / (1.0 - ADAM_B1 ** ADAM_STEP)
    v_hat = v / (1.0 - ADAM_B2 ** ADAM_STEP)
    return -ADAM_LR * (m_hat / (jnp.sqrt(v_hat) + ADAM_EPS) + ADAM_WD * w), m, v


def _adamw_big(w, g_mine, g_other, m, v):
    l_dim, a_dim, b_dim = w.shape
    tr = min(256, a_dim)

    def body(w_ref, g1_ref, g2_ref, m_ref, v_ref, g_ref, d_ref, nm_ref, nv_ref):
        g = g1_ref[...] + g2_ref[...]
        g_ref[...] = g
        d_ref[...], nm_ref[...], nv_ref[...] = _adam_math(w_ref[...], g, m_ref[...], v_ref[...])

    blk = pl.BlockSpec((1, tr, b_dim), lambda l, i: (l, i, 0))
    return pl.pallas_call(
        body, name="adamw_big", grid=(l_dim, a_dim // tr), in_specs=[blk] * 5, out_specs=[blk] * 4,
        out_shape=[jax.ShapeDtypeStruct(w.shape, F32)] * 4,
        compiler_params=_cparams(("parallel", "parallel")),
    )(w, g_mine, g_other, m, v)


def _adamw_small(ws, gs, ms, vs):
    n = len(ws)

    def body(*refs):
        for i in range(n):
            w_ref, g_ref, m_ref, v_ref = refs[i], refs[n + i], refs[2 * n + i], refs[3 * n + i]
            d_ref, nm_ref, nv_ref = refs[4 * n + i], refs[5 * n + i], refs[6 * n + i]
            d_ref[...], nm_ref[...], nv_ref[...] = _adam_math(w_ref[...], g_ref[...], m_ref[...], v_ref[...])

    shapes = [jax.ShapeDtypeStruct(w.shape, F32) for w in ws]
    outs = pl.pallas_call(body, name="adamw_small", out_shape=shapes * 3,
                          compiler_params=_cparams())(*ws, *gs, *ms, *vs)
    return outs[:n], outs[n:2 * n], outs[2 * n:]


def _pack(arrs, row_mult):
    parts = []
    for a in arrs:
        flat = a.reshape(-1)
        pad = (-flat.shape[0]) % 1024
        parts.append(jnp.pad(flat, (0, pad)).reshape(-1, 128))
    buf = jnp.concatenate(parts, axis=0)
    pad = (-buf.shape[0]) % row_mult
    return jnp.pad(buf, ((0, pad), (0, 0)))


def _unpack(buf, shapes):
    out, row = [], 0
    for sh in shapes:
        n = math.prod(sh)
        rows = -(-n // 1024) * 8
        out.append(buf[row:row + rows].reshape(-1)[:n].reshape(sh))
        row += rows
    return out


_NAMES = ['w_in_e', 'b_in_e', 'conv_a_w', 'conv_a_b', 'ln_a_g', 'ln_a_b', 'ln_v_g', 'ln_v_b', 'w_s', 'b_s', 'w_out_e', 'b_out_e',
          'w_in_o', 'b_in_o', 'w_pool', 'pool_scale', 'conv_d_w', 'w_out_o', 'b_out_o', 'ln_g', 'ln_b', 'w_ple', 'w_ple_gate',
          'b_ple_gate']
_BIG = ['w_in_e', 'w_out_e', 'w_in_o', 'w_out_o', 'w_ple', 'w_ple_gate']
_BIG_AXES = [2, 1, 2, 1, 2, 1]
_SMALL_SHARDED = ['conv_a_w', 'b_in_o', 'pool_scale', 'conv_d_w', 'b_out_o']


def kernel(x, p, w_in_e, b_in_e, conv_a_w, conv_a_b, ln_a_g, ln_a_b, ln_v_g, ln_v_b, w_s, b_s, w_out_e, b_out_e, w_in_o, b_in_o, w_pool, pool_scale, conv_d_w, w_out_o, b_out_o, ln_g, ln_b, w_ple, w_ple_gate, b_ple_gate, loss_target, m_w_in_e, m_b_in_e, m_conv_a_w, m_conv_a_b, m_ln_a_g, m_ln_a_b, m_ln_v_g, m_ln_v_b, m_w_s, m_b_s, m_w_out_e, m_b_out_e, m_w_in_o, m_b_in_o, m_w_pool, m_pool_scale, m_conv_d_w, m_w_out_o, m_b_out_o, m_ln_g, m_ln_b, m_w_ple, m_w_ple_gate, m_b_ple_gate, v_w_in_e, v_b_in_e, v_conv_a_w, v_conv_a_b, v_ln_a_g, v_ln_a_b, v_ln_v_g, v_ln_v_b, v_w_s, v_b_s, v_w_out_e, v_b_out_e, v_w_in_o, v_b_in_o, v_w_pool, v_pool_scale, v_conv_d_w, v_w_out_o, v_b_out_o, v_ln_g, v_ln_b, v_w_ple, v_w_ple_gate, v_b_ple_gate):
    args = locals()
    wts = {n: args[n] for n in _NAMES}
    mom = {n: args["m_" + n] for n in _NAMES}
    var = {n: args["v_" + n] for n in _NAMES}
    bsz, s_len, d = x.shape
    t = bsz * s_len
    cx, cy, cc = lax.axis_index("x"), lax.axis_index("y"), lax.axis_index("c")
    chip = (2 * cx + cy).astype(jnp.int32).reshape(1)
    half = cc.astype(jnp.int32).reshape(1)
    pos = jnp.concatenate([chip, half])

    def placed(name, ax, layer):
        return _place_shard(wts[name], ax, chip, layer=layer)

    sv = _pack([wts[n] for n in _SMALL_SHARDED], 8)
    first_refs = _gather_async("gather_first", DEPTH, [placed('w_in_e', 2, 0), _place_shard(sv[None], 1, chip, F32)], [2, 1])
    layer_refs = []
    for i in range(DEPTH):
        sfx = '_e' if i % 2 == 0 else '_o'
        items = [('w_out' + sfx, 1, i // 2), ('w_ple_gate', 1, i), ('w_ple', 2, i)] + ([('w_in' + sfx, 2, i // 2)] if i else [])
        layer_refs.append(_gather_async("gather_layer%d" % i, i, [placed(*it) for it in items], [it[1] for it in items]))
    fw = {}
    w_in_first = first_refs[0][...]
    sv_all = first_refs[1][...].reshape((4,) + sv.shape)
    small_parts = [_unpack(sv_all[k], [wts[n].shape for n in _SMALL_SHARDED]) for k in range(4)]
    for i, n in enumerate(_SMALL_SHARDED):
        fw[n] = jnp.concatenate([small_parts[k][i] for k in range(4)], axis=-1)
    for n in _NAMES:
        fw.setdefault(n, wts[n])

    def row8(rows, width):
        rows = [r.reshape(1, width) for r in rows]
        return jnp.concatenate(rows + [jnp.zeros((8 - len(rows), width), F32)], axis=0)

    x2 = x.reshape(t, d)
    saved = []
    for i in range(DEPTH):
        j = i // 2
        even = i % 2 == 0
        b_in, b_out = (fw['b_in_e'], fw['b_out_e']) if even else (fw['b_in_o'], fw['b_out_o'])
        w_in = layer_refs[i][3][...][0] if i else w_in_first[0]
        z = _in_proj(x2, w_in, b_in[j].reshape(1, N_COLS))
        z3 = z.reshape(bsz, s_len, N_COLS)
        if even:
            cw = jnp.concatenate([fw['conv_a_w'][j], jnp.zeros((1, W_BR), F32)], axis=0)
            mvec = row8([fw['conv_a_b'][j], fw['ln_a_g'][j], fw['ln_a_b'][j], fw['ln_v_g'][j], fw['ln_v_b'][j]], W_BR)
            bsf = jnp.repeat(fw['b_s'][j].T, W_BR // 8, axis=1)
            y3, a13 = _even_fwd(z3, cw, mvec, fw['w_s'][j], bsf)
            mix = (a13, cw, mvec, fw['w_s'][j], jnp.swapaxes(fw['w_s'][j], 1, 2), bsf)
        else:
            mvec = row8([fw['pool_scale'][j]] + [fw['conv_d_w'][j][k] for k in range(3)], W_BR)
            mix = (fw['w_pool'][j].astype(MM_DTYPE), mvec)
            y3 = _odd_fwd(z3, mix[0], mvec)
        pvec = row8([b_out[j], fw['ln_g'][i], fw['ln_b'][i], fw['b_ple_gate'][i]], d)
        post_w = (layer_refs[i][0][...][0], layer_refs[i][1][...][0], layer_refs[i][2][...][0], pvec)
        p2 = (p.reshape(DEPTH, t, D_PLE), i)
        y2 = y3.reshape(t, 2 * W_BR)
        if i < DEPTH - 1:
            xn, r2, gate2 = _post_fwd(y2, x2, p2, *post_w)
        else:
            dx, r2, gate2, sq = _post_fwd(y2, x2, p2, *post_w, tgt=loss_target.reshape(t, d))
        saved.append((x2, z3, y2, r2, gate2, p2, w_in, mix, post_w))
        x2 = xn

    gr = {n: [None] * wts[n].shape[0] for n in _NAMES}
    prev_lands = []
    for i in reversed(range(DEPTH)):
        j = i // 2
        even = i % 2 == 0
        x_in, z3, y2, r2, gate2, p2, w_in, mix, post_w = saved[i]
        dxr, dy, dwo, dwg, dwp, dpv = _post_bwd(dx, r2, gate2, p2, y2, *post_w)
        dy3 = dy.reshape(bsz, s_len, 2 * W_BR)
        sfx = '_e' if even else '_o'
        gr['w_out' + sfx][j], gr['b_out' + sfx][j] = dwo, dpv[0]
        gr['w_ple_gate'][i], gr['w_ple'][i] = dwg, dwp
        gr['ln_g'][i], gr['ln_b'][i], gr['b_ple_gate'][i] = dpv[1], dpv[2], dpv[3]
        if even:
            dz3, dbin, dcw, dmv, dws, dbsf = _even_bwd(z3, dy3, *mix)
            gr['conv_a_w'][j], gr['conv_a_b'][j] = dcw[:KA], dmv[0]
            gr['ln_a_g'][j], gr['ln_a_b'][j], gr['ln_v_g'][j], gr['ln_v_b'][j] = dmv[1], dmv[2], dmv[3], dmv[4]
            gr['w_s'][j] = dws
            gr['b_s'][j] = jnp.sum(dbsf.reshape(GBLK, 8, W_BR // 8), axis=2).T
        else:
            dz3, dbin, dwpool, dmv = _odd_bwd(z3, dy3, *mix)
            gr['w_pool'][j], gr['pool_scale'][j], gr['conv_d_w'][j] = dwpool, dmv[0], dmv[1:4]
        gr['b_in' + sfx][j] = dbin[0]
        dz2 = dz3.reshape(t, N_COLS)
        dwi, dw_done = _in_proj_bwd_dw(x_in, dz2)
        post_items = [('w_out' + sfx, j, dwo[None], 1), ('w_ple_gate', i, dwg[None], 1), ('w_ple', i, dwp[None], 2)]
        in_item = ('w_in' + sfx, j, dwi[None], 2)
        batches = [post_items + [in_item]] if i else [post_items, [in_item]]
        for bi, items in enumerate(batches):
            sent = [it[2] for it in items]
            if i == 0 and bi == 1:
                wire, dw_done = _to_wire(sent[0])
                sent = [wire]
            lands = _scatter_async("scatter_layer%d_%d" % (i, bi), DEPTH + 1 + 2 * i + bi, sent, [it[3] for it in items],
                                   prev_lands[:1] + ([dbin] if (i == 0 and bi == 0) else []))
            prev_lands = list(lands)
            for it, land in zip(items, lands):
                gr[it[0]][it[1]] = (it[2], land, it[3])
        dx = _in_proj_bwd_dx(dxr, dz2, w_in, dw_done)
    grad_x = dx.reshape(bsz, s_len, d)

    small_names = [n for n in _NAMES if n not in _BIG]
    g_small_full = [jnp.stack(gr[n]) for n in small_names] + [sq]
    small_all = _small_allreduce(_pack(g_small_full, 64))
    *g_small, sq_all = _unpack(small_all, [g.shape for g in g_small_full])
    loss = 0.5 * jnp.sum(sq_all) / d
    big_order = _BIG[1:] + _BIG[:1]
    sums = [_chip_sum([g[0] for g in gr[n]], [g[1] for g in gr[n]], gr[n][0][2], chip, dx if n == big_order[-1] else None)
            for n in big_order]
    others = (list(_pair_swap_async("pair_swap_a", 3 * DEPTH + 1, sums[:-1]))
              + list(_pair_swap_async("pair_swap_b", 3 * DEPTH + 2, sums[-1:])))
    grads = {}
    for n, g in zip(small_names, g_small):
        if n in _SMALL_SHARDED:
            w = wts[n].shape[-1]
            g = lax.dynamic_slice_in_dim(g, (2 * cx + cy) * w, w, axis=g.ndim - 1)
        grads[n] = g

    delta, new_m, new_v = {}, {}, {}
    ds, ms, vs = _adamw_small([wts[n] for n in small_names], [grads[n] for n in small_names],
                              [mom[n] for n in small_names], [var[n] for n in small_names])
    for n, a, b, c_ in zip(small_names, ds, ms, vs):
        delta[n], new_m[n], new_v[n] = a, b, c_
    for n, mine, other in zip(big_order, sums, others):
        grads[n], delta[n], new_m[n], new_v[n] = _adamw_big(wts[n], mine, other, mom[n], var[n])

    return (loss, grad_x, *[grads[n] for n in _NAMES], *[delta[n] for n in _NAMES],
            *[new_m[n] for n in _NAMES], *[new_v[n] for n in _NAMES])
```

```python
import functools
import math

import jax
import jax.numpy as jnp
from jax import lax
from jax.experimental import pallas as pl
from jax.experimental.pallas import tpu as pltpu
from jax.experimental.pallas import tpu_sc as plsc

F32 = jnp.float32
MM_DTYPE = jnp.bfloat16
WIRE_DTYPE = jnp.bfloat16
SEQ_TILE = 512
ROW_TILE = 512
BWD_ROW_TILE = 512
HALO = 32
CONV_CHUNK = 64
DW_CHUNK = 32
ROW_CHUNK = 64
GBLK = 128
VMEM_LIMIT = 56 * 1024 * 1024

D_MODEL = 1024
W_BR = 512
N_COLS = 6 * W_BR
D_PLE = 256
KA = 31
DEPTH = 4
POOL_WINDOWS = (2, 4, 8, 16)
ALPHA = (2.0 * DEPTH) ** 0.25
LN_EPS = 1e-5
GELU_C = math.sqrt(2.0 / math.pi)

ADAM_LR, ADAM_B1, ADAM_B2, ADAM_EPS, ADAM_WD, ADAM_STEP = 0.001, 0.9, 0.999, 1e-08, 0.01, 10

MESH = pl.DeviceIdType.MESH
ANY = pl.BlockSpec(memory_space=pl.ANY)


def _cparams(sem=None):
    return pltpu.CompilerParams(dimension_semantics=sem, vmem_limit_bytes=VMEM_LIMIT)


def _sigmoid(x):
    return 1.0 / (1.0 + jnp.exp(-x))


def _silu(x):
    return x * _sigmoid(x)


def _silu_grad(x):
    s = _sigmoid(x)
    return x * s, s * (1.0 + x * (1.0 - s))


def _gelu(x):
    return 0.5 * x * (1.0 + jnp.tanh(GELU_C * (x + 0.044715 * (x * x * x))))


def _gelu_grad(x):
    x2 = x * x
    th = jnp.tanh(GELU_C * (x + 0.044715 * (x * x2)))
    return 0.5 * x * (1.0 + th), 0.5 * (1.0 + th) + 0.5 * x * (1.0 - th * th) * (GELU_C * (1.0 + 3.0 * 0.044715 * x2))


def _ln_stats(x):
    mu = jnp.mean(x, axis=-1, keepdims=True)
    d = x - mu
    var = jnp.mean(d * d, axis=-1, keepdims=True)
    rs = lax.rsqrt(var + LN_EPS)
    return d * rs, rs


def _ln_bwd(dxh, xh, rs):
    return rs * (dxh - jnp.mean(dxh, axis=-1, keepdims=True) - xh * jnp.mean(dxh * xh, axis=-1, keepdims=True))


def _mm(a):
    return a.astype(MM_DTYPE)


def _dot(a, b):
    return jnp.dot(_mm(a), _mm(b), preferred_element_type=F32)


def _dot_nt(a, b):
    return lax.dot_general(_mm(a), _mm(b), (((1,), (1,)), ((), ())), preferred_element_type=F32)


def _dot_tn(a, b):
    return lax.dot_general(_mm(a), _mm(b), (((0,), (0,)), ((), ())), preferred_element_type=F32)


def _rowsum(x):
    return jnp.sum(x, axis=0, keepdims=True)


def _in_proj(x2, w, b):
    t, d = x2.shape
    n = w.shape[1]
    tm = min(ROW_TILE, t)
    nc = 768

    def body(x_ref, w_ref, b_ref, z_ref):
        xb = _mm(x_ref[...])
        for j in range(n // nc):
            cs = slice(j * nc, (j + 1) * nc)
            z_ref[:, cs] = jnp.dot(xb, w_ref[:, cs], preferred_element_type=F32) + b_ref[:, cs]

    return pl.pallas_call(
        body, name="in_proj", grid=(t // tm,),
        in_specs=[pl.BlockSpec((tm, d), lambda i: (i, 0)), pl.BlockSpec((d, n), lambda i: (0, 0)),
                  pl.BlockSpec((1, n), lambda i: (0, 0))],
        out_specs=pl.BlockSpec((tm, n), lambda i: (i, 0)),
        out_shape=jax.ShapeDtypeStruct((t, n), F32),
        compiler_params=_cparams(("parallel",)),
    )(x2, w, b)


def _in_proj_bwd_dx(dxr, dz, w, after):
    t, d = dxr.shape
    n = w.shape[1]
    tm = min(ROW_TILE, t)

    def body(dxr_ref, dz_ref, w_ref, after_ref, dx_ref):
        dx_ref[...] = dxr_ref[...] + _dot_nt(dz_ref[...], w_ref[...])

    return pl.pallas_call(
        body, name="in_proj_bwd_dx", grid=(t // tm,),
        in_specs=[pl.BlockSpec((tm, d), lambda i: (i, 0)), pl.BlockSpec((tm, n), lambda i: (i, 0)),
                  pl.BlockSpec((d, n), lambda i: (0, 0)), pl.BlockSpec((8, 128), lambda i: (0, 0))],
        out_specs=pl.BlockSpec((tm, d), lambda i: (i, 0)),
        out_shape=jax.ShapeDtypeStruct((t, d), F32),
        compiler_params=_cparams(("parallel",)),
    )(dxr, dz, w, after)


def _in_proj_bwd_dw(x2, dz):
    t, d = x2.shape
    n = dz.shape[1]
    tm = min(ROW_TILE, t)
    nc = 768

    def body(x_ref, dz_ref, dw_ref, done_ref):
        @pl.when(pl.program_id(0) == 0)
        def _():
            dw_ref[...] = jnp.zeros_like(dw_ref)
            done_ref[...] = jnp.zeros_like(done_ref)
        xb = _mm(x_ref[...])
        for j in range(n // nc):
            cs = slice(j * nc, (j + 1) * nc)
            dw_ref[:, cs] += _dot_tn(xb, dz_ref[:, cs])

    return pl.pallas_call(
        body, name="in_proj_bwd_dw", grid=(t // tm,),
        in_specs=[pl.BlockSpec((tm, d), lambda i: (i, 0)), pl.BlockSpec((tm, n), lambda i: (i, 0))],
        out_specs=[pl.BlockSpec((d, n), lambda i: (0, 0)), pl.BlockSpec((8, 128), lambda i: (0, 0))],
        out_shape=[jax.ShapeDtypeStruct((d, n), F32), jax.ShapeDtypeStruct((8, 128), F32)],
        compiler_params=_cparams(("arbitrary",)),
    )(x2, dz)


def _halo_specs(ts, s_len, cols, left=True):
    per = ts // HALO
    last = s_len // HALO - 1
    if left:
        return pl.BlockSpec((1, HALO, cols), lambda b, s: (b, jnp.maximum(s * per - 1, 0), 0))
    return pl.BlockSpec((1, HALO, cols), lambda b, s: (b, jnp.minimum((s + 1) * per, last), 0))


def _build_shifts(src_ref, sh_ref, rows):
    for r in range(1, 8):
        sh_ref[r - 1, 0:rows, :] = src_ref[r:r + rows, :]


def _shifted(src_ref, sh_ref, r, start, n):
    if r == 0:
        return src_ref[pl.ds(start, n), :]
    return sh_ref[r - 1, pl.ds(start, n), :]


def _tril_masks():
    ri = lax.broadcasted_iota(jnp.int32, (GBLK, GBLK), 0)
    ci = lax.broadcasted_iota(jnp.int32, (GBLK, GBLK), 1)
    return ri >= ci, ci >= ri


def _spatial(w_ref, keep, vb):
    lane = lax.broadcasted_iota(jnp.int32, (GBLK, GBLK), 1)
    outs = []
    for p in range(4):
        xs = vb[:, p * GBLK:(p + 1) * GBLK]
        r0 = jnp.dot(_mm(jnp.where(keep, w_ref[2 * p], 0.0)), xs, preferred_element_type=F32)
        r1 = jnp.dot(_mm(jnp.where(keep, w_ref[2 * p + 1], 0.0)), xs, preferred_element_type=F32)
        outs.append(jnp.where(lane < 64, r0, r1))
    return jnp.concatenate(outs, axis=1)


def _even_fwd(z3, cw, vec, ws, bsf):
    bsz, s_len, _ = z3.shape
    ts = min(SEQ_TILE, s_len)
    ext_rows = ts + HALO

    def body(z_ref, zl_ref, cw_ref, vec_ref, ws_ref, bsf_ref, y_ref, a1_ref, ext_ref, sh_ref):
        s = pl.program_id(1)
        hl = zl_ref[0]
        a0h = hl[:, 0:W_BR] * _sigmoid(hl[:, W_BR:2 * W_BR])
        ext_ref[0:HALO, :] = jnp.where(s > 0, a0h, 0.0)
        for a, b in _row_chunks(0, ts):
            ext_ref[HALO + a:HALO + b, :] = z_ref[0, a:b, 0:W_BR] * _sigmoid(z_ref[0, a:b, W_BR:2 * W_BR])
        ext_ref[ext_rows:ext_rows + 8, :] = jnp.zeros((8, W_BR), F32)
        _build_shifts(ext_ref, sh_ref, ext_rows)

        def conv_chunk(ci, carry):
            base = pl.multiple_of(ci * CONV_CHUNK, CONV_CHUNK)
            acc = jnp.zeros((CONV_CHUNK, W_BR), F32) + vec_ref[0:1, :]
            for k in range(KA):
                q, r = divmod(2 + k, 8)
                acc = acc + _shifted(ext_ref, sh_ref, r, base + 8 * q, CONV_CHUNK) * cw_ref[k:k + 1, :]
            a1_ref[0, pl.ds(base, CONV_CHUNK), :] = acc
            return carry

        lax.fori_loop(0, ts // CONV_CHUNK, conv_chunk, 0)
        keep, _ = _tril_masks()

        def block(bi, carry):
            rows = pl.ds(pl.multiple_of(bi * GBLK, GBLK), GBLK)
            xh, _ = _ln_stats(a1_ref[0, rows, :])
            a = _silu(xh * vec_ref[1:2, :] + vec_ref[2:3, :]) * _silu(z_ref[0, rows, 2 * W_BR:3 * W_BR])
            y_ref[0, rows, 0:W_BR] = a.astype(y_ref.dtype)
            ua = _gelu(z_ref[0, rows, 3 * W_BR:4 * W_BR])
            vh, _ = _ln_stats(_gelu(z_ref[0, rows, 4 * W_BR:5 * W_BR]))
            vl = vh * vec_ref[3:4, :] + vec_ref[4:5, :]
            sg = _spatial(ws_ref, keep, _mm(vl)) + bsf_ref[...]
            g = ua * sg * _silu(z_ref[0, rows, 5 * W_BR:6 * W_BR])
            y_ref[0, rows, W_BR:2 * W_BR] = g.astype(y_ref.dtype)
            return carry

        lax.fori_loop(0, ts // GBLK, block, 0)

    full = lambda shape: pl.BlockSpec(shape, lambda b, s: (0,) * len(shape))
    return pl.pallas_call(
        body, name="even_fwd", grid=(bsz, s_len // ts),
        in_specs=[pl.BlockSpec((1, ts, N_COLS), lambda b, s: (b, s, 0)), _halo_specs(ts, s_len, 2 * W_BR),
                  full((32, W_BR)), full((8, W_BR)), full((8, GBLK, GBLK)), full((GBLK, W_BR))],
        out_specs=[pl.BlockSpec((1, ts, 2 * W_BR), lambda b, s: (b, s, 0)), pl.BlockSpec((1, ts, W_BR), lambda b, s: (b, s, 0))],
        out_shape=[jax.ShapeDtypeStruct((bsz, s_len, 2 * W_BR), MM_DTYPE), jax.ShapeDtypeStruct((bsz, s_len, W_BR), F32)],
        scratch_shapes=[pltpu.VMEM((ext_rows + 8, W_BR), F32), pltpu.VMEM((7, ext_rows, W_BR), F32)],
        compiler_params=_cparams(("parallel", "parallel")),
    )(z3, z3, cw, vec, ws, bsf)


def _even_bwd(z3, dy3, a13, cw, vec, ws, wst, bsf):
    bsz, s_len, _ = z3.shape
    ts = min(SEQ_TILE, s_len)
    n_s = s_len // ts
    ext_rows = ts + 2 * HALO
    a_rows = ts + HALO

    def body(z_ref, zl_ref, zr_ref, dy_ref, dyr_ref, a1_ref, a1r_ref, cw_ref, vec_ref, ws_ref, wst_ref, bsf_ref,
             dz_ref, dbin_ref, dcw_ref, dvec_ref, dws_ref, dbsf_ref,
             ext_ref, sh_ref, ag_ref, dya_ref, d_ref, accw_ref):
        b = pl.program_id(0)
        s = pl.program_id(1)

        @pl.when((b == 0) & (s == 0))
        def _():
            dbin_ref[...] = jnp.zeros_like(dbin_ref)
            dcw_ref[...] = jnp.zeros_like(dcw_ref)
            dvec_ref[...] = jnp.zeros_like(dvec_ref)
            dws_ref[...] = jnp.zeros_like(dws_ref)
            dbsf_ref[...] = jnp.zeros_like(dbsf_ref)

        has_right = s < n_s - 1
        hl = zl_ref[0]
        hr = zr_ref[0]
        ext_ref[0:HALO, :] = jnp.where(s > 0, hl[:, 0:W_BR] * _sigmoid(hl[:, W_BR:2 * W_BR]), 0.0)
        for a, b in _row_chunks(0, ts):
            ext_ref[HALO + a:HALO + b, :] = z_ref[0, a:b, 0:W_BR] * _sigmoid(z_ref[0, a:b, W_BR:2 * W_BR])
            ag_ref[a:b, :] = z_ref[0, a:b, 2 * W_BR:3 * W_BR]
            dya_ref[a:b, :] = dy_ref[0, a:b, 0:W_BR]
        ext_ref[HALO + ts:ext_rows, :] = hr[:, 0:W_BR] * _sigmoid(hr[:, W_BR:2 * W_BR])
        ext_ref[ext_rows:ext_rows + 8, :] = jnp.zeros((8, W_BR), F32)
        ag_ref[ts:a_rows, :] = hr[:, 2 * W_BR:3 * W_BR]
        dya_ref[ts:a_rows, :] = dyr_ref[0][:, 0:W_BR]
        _build_shifts(ext_ref, sh_ref, ext_rows)

        def a_chunk(base, n, main):
            rows = pl.ds(base, n)
            xh, rs = _ln_stats(a1_ref[0, rows, :] if main else a1r_ref[0])
            ln = xh * vec_ref[1:2, :] + vec_ref[2:3, :]
            sl, dsl = _silu_grad(ln)
            sgt, dsgt = _silu_grad(ag_ref[rows, :])
            dya = dya_ref[rows, :]
            dln = dya * sgt * dsl
            da1 = _ln_bwd(dln * vec_ref[1:2, :], xh, rs)
            if main:
                d_ref[rows, :] = da1
                dag = dya * sl * dsgt
                dz_ref[0, rows, 2 * W_BR:3 * W_BR] = dag.astype(dz_ref.dtype)
                dbin_ref[0:1, 2 * W_BR:3 * W_BR] += _rowsum(dag)
                dvec_ref[0:1, :] += _rowsum(da1)
                dvec_ref[1:2, :] += _rowsum(dln * xh)
                dvec_ref[2:3, :] += _rowsum(dln)
            else:
                d_ref[rows, :] = jnp.where(has_right, da1, 0.0)

        def a_main(ci, carry):
            a_chunk(pl.multiple_of(ci * GBLK, GBLK), GBLK, True)
            return carry

        lax.fori_loop(0, ts // GBLK, a_main, 0)
        a_chunk(ts, HALO, False)
        d_ref[a_rows:a_rows + 8, :] = jnp.zeros((8, W_BR), F32)

        accw_ref[...] = jnp.zeros_like(accw_ref)

        def dw_chunk(ci, carry):
            base = pl.multiple_of(ci * DW_CHUNK, DW_CHUNK)
            d = d_ref[pl.ds(base, DW_CHUNK), :]
            for k in range(KA):
                q, r = divmod(2 + k, 8)
                prod = d * _shifted(ext_ref, sh_ref, r, base + 8 * q, DW_CHUNK)
                accw_ref[k] += jnp.sum(prod.reshape(DW_CHUNK // 8, 8, W_BR), axis=0)
            return carry

        lax.fori_loop(0, ts // DW_CHUNK, dw_chunk, 0)
        dcw_ref[...] += jnp.sum(accw_ref[...], axis=1)

        _build_shifts(d_ref, sh_ref, a_rows)

        def dx_chunk(ci, carry):
            base = pl.multiple_of(ci * CONV_CHUNK, CONV_CHUNK)
            rows = pl.ds(base, CONV_CHUNK)
            acc = jnp.zeros((CONV_CHUNK, W_BR), F32)
            for m in range(KA):
                q, r = divmod(m, 8)
                acc = acc + _shifted(d_ref, sh_ref, r, base + 8 * q, CONV_CHUNK) * cw_ref[KA - 1 - m:KA - m, :]
            aval = z_ref[0, rows, 0:W_BR]
            sg = _sigmoid(z_ref[0, rows, W_BR:2 * W_BR])
            dval = acc * sg
            dglu = acc * aval * sg * (1.0 - sg)
            dz_ref[0, rows, 0:W_BR] = dval.astype(dz_ref.dtype)
            dz_ref[0, rows, W_BR:2 * W_BR] = dglu.astype(dz_ref.dtype)
            dbin_ref[0:1, 0:W_BR] += _rowsum(dval)
            dbin_ref[0:1, W_BR:2 * W_BR] += _rowsum(dglu)
            return carry

        lax.fori_loop(0, ts // CONV_CHUNK, dx_chunk, 0)

        keep, keep_t = _tril_masks()
        lane = lax.broadcasted_iota(jnp.int32, (GBLK, GBLK), 1)

        def block(bi, carry):
            rows = pl.ds(pl.multiple_of(bi * GBLK, GBLK), GBLK)
            ua, dua = _gelu_grad(z_ref[0, rows, 3 * W_BR:4 * W_BR])
            va, dva = _gelu_grad(z_ref[0, rows, 4 * W_BR:5 * W_BR])
            sgt, dsgt = _silu_grad(z_ref[0, rows, 5 * W_BR:6 * W_BR])
            vh, rs = _ln_stats(va)
            vlb = _mm(vh * vec_ref[3:4, :] + vec_ref[4:5, :])
            sg = _spatial(ws_ref, keep, vlb) + bsf_ref[...]
            dyg = dy_ref[0, rows, W_BR:2 * W_BR]
            du = dyg * sg * sgt * dua
            dsg = dyg * ua * sgt
            dgg = dyg * ua * sg * dsgt
            dvl = _spatial(wst_ref, keep_t, _mm(dsg))
            for p in range(4):
                dsp = dsg[:, p * GBLK:(p + 1) * GBLK]
                vlp = vlb[:, p * GBLK:(p + 1) * GBLK]
                dws_ref[2 * p] += jnp.where(keep, _dot_nt(jnp.where(lane < 64, dsp, 0.0), vlp), 0.0)
                dws_ref[2 * p + 1] += jnp.where(keep, _dot_nt(jnp.where(lane >= 64, dsp, 0.0), vlp), 0.0)
            dbsf_ref[...] += dsg
            dvec_ref[3:4, :] += _rowsum(dvl * vh)
            dvec_ref[4:5, :] += _rowsum(dvl)
            dv = _ln_bwd(dvl * vec_ref[3:4, :], vh, rs) * dva
            dz_ref[0, rows, 3 * W_BR:4 * W_BR] = du.astype(dz_ref.dtype)
            dz_ref[0, rows, 4 * W_BR:5 * W_BR] = dv.astype(dz_ref.dtype)
            dz_ref[0, rows, 5 * W_BR:6 * W_BR] = dgg.astype(dz_ref.dtype)
            dbin_ref[0:1, 3 * W_BR:4 * W_BR] += _rowsum(du)
            dbin_ref[0:1, 4 * W_BR:5 * W_BR] += _rowsum(dv)
            dbin_ref[0:1, 5 * W_BR:6 * W_BR] += _rowsum(dgg)
            return carry

        lax.fori_loop(0, ts // GBLK, block, 0)

    full = lambda shape: pl.BlockSpec(shape, lambda b, s: (0,) * len(shape))
    acc_shapes = [(1, N_COLS), (32, W_BR), (8, W_BR), (8, GBLK, GBLK), (GBLK, W_BR)]
    return pl.pallas_call(
        body, name="even_bwd", grid=(bsz, n_s),
        in_specs=[pl.BlockSpec((1, ts, N_COLS), lambda b, s: (b, s, 0)),
                  _halo_specs(ts, s_len, N_COLS, True), _halo_specs(ts, s_len, N_COLS, False),
                  pl.BlockSpec((1, ts, 2 * W_BR), lambda b, s: (b, s, 0)), _halo_specs(ts, s_len, 2 * W_BR, False),
                  pl.BlockSpec((1, ts, W_BR), lambda b, s: (b, s, 0)), _halo_specs(ts, s_len, W_BR, False),
                  full((32, W_BR)), full((8, W_BR)), full((8, GBLK, GBLK)), full((8, GBLK, GBLK)), full((GBLK, W_BR))],
        out_specs=[pl.BlockSpec((1, ts, N_COLS), lambda b, s: (b, s, 0))] + [full(sh) for sh in acc_shapes],
        out_shape=[jax.ShapeDtypeStruct((bsz, s_len, N_COLS), MM_DTYPE)] + [jax.ShapeDtypeStruct(sh, F32) for sh in acc_shapes],
        scratch_shapes=[pltpu.VMEM((ext_rows + 8, W_BR), F32), pltpu.VMEM((7, ext_rows, W_BR), F32),
                        pltpu.VMEM((a_rows, W_BR), F32), pltpu.VMEM((a_rows, W_BR), F32),
                        pltpu.VMEM((a_rows + 8, W_BR), F32), pltpu.VMEM((32, 8, W_BR), F32)],
        compiler_params=_cparams(("arbitrary", "arbitrary")),
    )(z3, z3, z3, dy3, dy3, a13, a13, cw, vec, ws, wst, bsf)


def _row_chunks(lo, hi):
    return [(a, min(a + ROW_CHUNK, hi)) for a in range(lo, hi, ROW_CHUNK)]


def _pool_stages(e_refs, rows):
    e0, e1, e2, e3, e4 = e_refs
    for a, b in _row_chunks(8, rows):
        e1[a:b, :] = e0[a:b, :] + e0[a - 1:b - 1, :]
    for a, b in _row_chunks(16, rows):
        e2[a:b, GBLK:] = e1[a:b, GBLK:] + e1[a - 2:b - 2, GBLK:]
    for a, b in _row_chunks(24, rows):
        e3[a:b, 2 * GBLK:] = e2[a:b, 2 * GBLK:] + e2[a - 4:b - 4, 2 * GBLK:]
    for a, b in _row_chunks(32, rows):
        e4[a:b, 3 * GBLK:] = e3[a:b, 3 * GBLK:] + e3[a - 8:b - 8, 3 * GBLK:]


def _pool_counts(start, n):
    pos = (start + 1 + lax.broadcasted_iota(jnp.int32, (n, 1), 0)).astype(F32)
    return [jnp.minimum(pos, float(w)) for w in POOL_WINDOWS]


def _pooled_into(e_refs, pooled_ref, s, ts):
    for a, b in _row_chunks(0, ts):
        cnt = _pool_counts(s * ts + a, b - a)
        for g in range(4):
            cs = slice(g * GBLK, (g + 1) * GBLK)
            pooled_ref[a:b, cs] = e_refs[g + 1][HALO + a:HALO + b, cs] / cnt[g] - e_refs[0][HALO + a:HALO + b, cs]


def _odd_prologue(z_ref, hl, s, ts, e_refs, pooled_ref, dext_ref, ec_ref, vec_ref):
    e0 = e_refs[0]
    e0[0:HALO, :] = jnp.where(s > 0, hl[:, 0:W_BR], 0.0)
    dext_ref[0:HALO, :] = jnp.where(s > 0, hl[:, 2 * W_BR:3 * W_BR] * hl[:, 4 * W_BR:5 * W_BR], 0.0)
    for a, b in _row_chunks(0, ts):
        e0[HALO + a:HALO + b, :] = z_ref[0, a:b, 0:W_BR]
        dext_ref[HALO + a:HALO + b, :] = z_ref[0, a:b, 2 * W_BR:3 * W_BR] * z_ref[0, a:b, 4 * W_BR:5 * W_BR]
    _pool_stages(e_refs, ts + HALO)
    _pooled_into(e_refs, pooled_ref, s, ts)
    for a, b in _row_chunks(0, ts):
        ec_ref[a:b, :] = (vec_ref[1:2, :] * dext_ref[HALO - 2 + a:HALO - 2 + b, :] + vec_ref[2:3, :] * dext_ref[HALO - 1 + a:HALO - 1 + b, :]
                          + vec_ref[3:4, :] * dext_ref[HALO + a:HALO + b, :])


def _odd_fwd(z3, wp, vec):
    bsz, s_len, _ = z3.shape
    ts = min(SEQ_TILE, s_len)
    ext_rows = ts + HALO

    def body(z_ref, zl_ref, wp_ref, vec_ref, y_ref, e0, e1, e2, e3, e4, pooled_ref, dext_ref, ec_ref):
        s = pl.program_id(1)
        _odd_prologue(z_ref, zl_ref[0], s, ts, (e0, e1, e2, e3, e4), pooled_ref, dext_ref, ec_ref, vec_ref)

        def block(bi, carry):
            rows = pl.ds(pl.multiple_of(bi * GBLK, GBLK), GBLK)
            pb = _mm(pooled_ref[rows, :])
            cpre = jnp.concatenate([jnp.dot(pb[:, g * GBLK:(g + 1) * GBLK], wp_ref[g], preferred_element_type=F32)
                                    for g in range(4)], axis=1)
            c = cpre * vec_ref[0:1, :] * _silu(z_ref[0, rows, W_BR:2 * W_BR])
            d = z_ref[0, rows, 3 * W_BR:4 * W_BR] * ec_ref[rows, :] * _silu(z_ref[0, rows, 5 * W_BR:6 * W_BR])
            y_ref[0, rows, 0:W_BR] = c.astype(y_ref.dtype)
            y_ref[0, rows, W_BR:2 * W_BR] = d.astype(y_ref.dtype)
            return carry

        lax.fori_loop(0, ts // GBLK, block, 0)

    full = lambda shape: pl.BlockSpec(shape, lambda b, s: (0,) * len(shape))
    ebuf = pltpu.VMEM((ext_rows, W_BR), F32)
    return pl.pallas_call(
        body, name="odd_fwd", grid=(bsz, s_len // ts),
        in_specs=[pl.BlockSpec((1, ts, N_COLS), lambda b, s: (b, s, 0)), _halo_specs(ts, s_len, N_COLS),
                  full((4, GBLK, GBLK)), full((8, W_BR))],
        out_specs=pl.BlockSpec((1, ts, 2 * W_BR), lambda b, s: (b, s, 0)),
        out_shape=jax.ShapeDtypeStruct((bsz, s_len, 2 * W_BR), MM_DTYPE),
        scratch_shapes=[ebuf, ebuf, ebuf, ebuf, ebuf, pltpu.VMEM((ts, W_BR), F32), ebuf, pltpu.VMEM((ts, W_BR), F32)],
        compiler_params=_cparams(("parallel", "parallel")),
    )(z3, z3, wp, vec)


def _odd_bwd(z3, dy3, wp, vec):
    bsz, s_len, _ = z3.shape
    ts = min(SEQ_TILE, s_len)
    n_s = s_len // ts
    ext_rows = ts + HALO

    def body(z_ref, zl_ref, zr_ref, dy_ref, dyr_ref, wp_ref, vec_ref,
             dz_ref, dbin_ref, dwp_ref, dvec_ref,
             e0, e1, e2, e3, e4, pooled_ref, dext_ref, ec_ref, q_ref, dp_ref, de_ref, f1, f2, f3, f4):
        b = pl.program_id(0)
        s = pl.program_id(1)

        @pl.when((b == 0) & (s == 0))
        def _():
            dbin_ref[...] = jnp.zeros_like(dbin_ref)
            dwp_ref[...] = jnp.zeros_like(dwp_ref)
            dvec_ref[...] = jnp.zeros_like(dvec_ref)

        has_right = s < n_s - 1
        _odd_prologue(z_ref, zl_ref[0], s, ts, (e0, e1, e2, e3, e4), pooled_ref, dext_ref, ec_ref, vec_ref)

        def grads(zc_gate, zd_b, zd_gate, dyc, dyd, rows_out, n, start, valid):
            sgt = _silu(zc_gate)
            dcpre = dyc * vec_ref[0:1, :] * sgt
            db = _mm(dcpre)
            dpool = jnp.concatenate([_dot_nt(db[:, g * GBLK:(g + 1) * GBLK], wp_ref[g]) for g in range(4)], axis=1)
            cnt = _pool_counts(start, n)
            q = jnp.concatenate([dpool[:, g * GBLK:(g + 1) * GBLK] / cnt[g] for g in range(4)], axis=1)
            de = dyd * zd_b * _silu(zd_gate)
            if valid is not None:
                q = jnp.where(valid, q, 0.0)
                de = jnp.where(valid, de, 0.0)
            q_ref[rows_out, :] = q
            dp_ref[rows_out, :] = dpool
            de_ref[rows_out, :] = de
            return dcpre

        def block(bi, carry):
            base = pl.multiple_of(bi * GBLK, GBLK)
            rows = pl.ds(base, GBLK)
            cg = z_ref[0, rows, W_BR:2 * W_BR]
            dyc = dy_ref[0, rows, 0:W_BR]
            dyd = dy_ref[0, rows, W_BR:2 * W_BR]
            d_b = z_ref[0, rows, 3 * W_BR:4 * W_BR]
            d_gate = z_ref[0, rows, 5 * W_BR:6 * W_BR]
            dcpre = grads(cg, d_b, d_gate, dyc, dyd, rows, GBLK, s * ts + base, None)
            pb = _mm(pooled_ref[rows, :])
            dcb = _mm(dcpre)
            cpre = jnp.concatenate([jnp.dot(pb[:, g * GBLK:(g + 1) * GBLK], wp_ref[g], preferred_element_type=F32)
                                    for g in range(4)], axis=1)
            for g in range(4):
                cs = slice(g * GBLK, (g + 1) * GBLK)
                dwp_ref[g] += _dot_tn(pb[:, cs], dcb[:, cs])
            sgt, dsgt = _silu_grad(cg)
            dvec_ref[0:1, :] += _rowsum(dyc * cpre * sgt)
            dcg = dyc * cpre * vec_ref[0:1, :] * dsgt
            sdt, dsdt = _silu_grad(d_gate)
            ec = ec_ref[rows, :]
            ddb = dyd * ec * sdt
            ddg = dyd * d_b * ec * dsdt
            dz_ref[0, rows, W_BR:2 * W_BR] = dcg.astype(dz_ref.dtype)
            dz_ref[0, rows, 3 * W_BR:4 * W_BR] = ddb.astype(dz_ref.dtype)
            dz_ref[0, rows, 5 * W_BR:6 * W_BR] = ddg.astype(dz_ref.dtype)
            dbin_ref[0:1, W_BR:2 * W_BR] += _rowsum(dcg)
            dbin_ref[0:1, 3 * W_BR:4 * W_BR] += _rowsum(ddb)
            dbin_ref[0:1, 5 * W_BR:6 * W_BR] += _rowsum(ddg)
            return carry

        lax.fori_loop(0, ts // GBLK, block, 0)
        hr = zr_ref[0]
        dyr = dyr_ref[0]
        grads(hr[:, W_BR:2 * W_BR], hr[:, 3 * W_BR:4 * W_BR], hr[:, 5 * W_BR:6 * W_BR], dyr[:, 0:W_BR], dyr[:, W_BR:2 * W_BR],
              slice(ts, ext_rows), HALO, (s + 1) * ts, has_right)

        for a, b in _row_chunks(0, ts + 24):
            f1[a:b, :] = q_ref[a:b, :] + q_ref[a + 1:b + 1, :]
        for a, b in _row_chunks(0, ts + 16):
            f2[a:b, GBLK:] = f1[a:b, GBLK:] + f1[a + 2:b + 2, GBLK:]
        for a, b in _row_chunks(0, ts + 8):
            f3[a:b, 2 * GBLK:] = f2[a:b, 2 * GBLK:] + f2[a + 4:b + 4, 2 * GBLK:]
        for a, b in _row_chunks(0, ts):
            f4[a:b, 3 * GBLK:] = f3[a:b, 3 * GBLK:] + f3[a + 8:b + 8, 3 * GBLK:]

        for a, b in _row_chunks(0, ts):
            for g, f in enumerate((f1, f2, f3, f4)):
                cs = slice(g * GBLK, (g + 1) * GBLK)
                dvg = f[a:b, cs] - dp_ref[a:b, cs]
                dz_ref[0, a:b, cs] = dvg.astype(dz_ref.dtype)
                dbin_ref[0:1, cs] += _rowsum(dvg)
            de = de_ref[a:b, :]
            ddc = vec_ref[1:2, :] * de_ref[a + 2:b + 2, :] + vec_ref[2:3, :] * de_ref[a + 1:b + 1, :] + vec_ref[3:4, :] * de
            ddh = ddc * z_ref[0, a:b, 4 * W_BR:5 * W_BR]
            ddcc = ddc * z_ref[0, a:b, 2 * W_BR:3 * W_BR]
            dz_ref[0, a:b, 2 * W_BR:3 * W_BR] = ddh.astype(dz_ref.dtype)
            dz_ref[0, a:b, 4 * W_BR:5 * W_BR] = ddcc.astype(dz_ref.dtype)
            dbin_ref[0:1, 2 * W_BR:3 * W_BR] += _rowsum(ddh)
            dbin_ref[0:1, 4 * W_BR:5 * W_BR] += _rowsum(ddcc)
            for k in range(3):
                dvec_ref[1 + k:2 + k, :] += _rowsum(de * dext_ref[HALO - 2 + k + a:HALO - 2 + k + b, :])

    full = lambda shape: pl.BlockSpec(shape, lambda b, s: (0,) * len(shape))
    acc_shapes = [(1, N_COLS), (4, GBLK, GBLK), (8, W_BR)]
    ebuf = pltpu.VMEM((ext_rows, W_BR), F32)
    tbuf = pltpu.VMEM((ts, W_BR), F32)
    return pl.pallas_call(
        body, name="odd_bwd", grid=(bsz, n_s),
        in_specs=[pl.BlockSpec((1, ts, N_COLS), lambda b, s: (b, s, 0)),
                  _halo_specs(ts, s_len, N_COLS, True), _halo_specs(ts, s_len, N_COLS, False),
                  pl.BlockSpec((1, ts, 2 * W_BR), lambda b, s: (b, s, 0)), _halo_specs(ts, s_len, 2 * W_BR, False),
                  full((4, GBLK, GBLK)), full((8, W_BR))],
        out_specs=[pl.BlockSpec((1, ts, N_COLS), lambda b, s: (b, s, 0))] + [full(sh) for sh in acc_shapes],
        out_shape=[jax.ShapeDtypeStruct((bsz, s_len, N_COLS), MM_DTYPE)] + [jax.ShapeDtypeStruct(sh, F32) for sh in acc_shapes],
        scratch_shapes=[ebuf, ebuf, ebuf, ebuf, ebuf, tbuf, ebuf, tbuf, ebuf, ebuf, ebuf, ebuf, ebuf, ebuf, tbuf],
        compiler_params=_cparams(("arbitrary", "arbitrary")),
    )(z3, z3, z3, dy3, dy3, wp, vec)


def _post_fwd(y2, x2, p_layer, w_out, wg, wple, vec, tgt=None):
    p_all, layer = p_layer
    t, d = x2.shape
    tm = min(ROW_TILE, t)
    last = tgt is not None

    def body(*refs):
        y_ref, x_ref, p_ref, wo_ref, wg_ref, wp_ref, vec_ref = refs[:7]
        xn_ref, r_ref, gate_ref = refs[7 + last:10 + last]
        r = ALPHA * x_ref[...] + jnp.dot(y_ref[...], wo_ref[...], preferred_element_type=F32) + vec_ref[0:1, :]
        r_ref[...] = r
        xh, _ = _ln_stats(r)
        h = xh * vec_ref[1:2, :] + vec_ref[2:3, :]
        gate = _sigmoid(_dot(h, wg_ref[...]) + vec_ref[3:4, :])
        gate_ref[...] = gate
        xn = h + gate * _dot(p_ref[...], wp_ref[...])
        if last:
            sq_ref = refs[11]

            @pl.when(pl.program_id(0) == 0)
            def _():
                sq_ref[...] = jnp.zeros_like(sq_ref)
            e = xn - refs[7][...]
            xn_ref[...] = e / float(d)
            sq_ref[...] += _rowsum(e * e)
        else:
            xn_ref[...] = xn

    row = lambda c: pl.BlockSpec((tm, c), lambda i: (i, 0))
    full = lambda shape: pl.BlockSpec(shape, lambda i: (0,) * len(shape))
    return pl.pallas_call(
        body, name="post_fwd_loss" if last else "post_fwd", grid=(t // tm,),
        in_specs=[row(d), row(d), pl.BlockSpec((None, tm, D_PLE), lambda i: (layer, i, 0)),
                  full((d, d)), full((d, d)), full((D_PLE, d)), full((8, d))] + [row(d)] * last,
        out_specs=[row(d), row(d), row(d)] + [full((1, d))] * last,
        out_shape=[jax.ShapeDtypeStruct((t, d), F32)] * 3 + [jax.ShapeDtypeStruct((1, d), F32)] * last,
        compiler_params=_cparams(("arbitrary",) if last else ("parallel",)),
    )(y2, x2, p_all, w_out, wg, wple, vec, *([tgt] if last else []))


def _post_bwd(dxn, r2, gate2, p_layer, y2, w_out, wg, wple, vec):
    p_all, layer = p_layer
    t, d = r2.shape
    tm = min(BWD_ROW_TILE, t)

    def body(dxn_ref, r_ref, gate_ref, p_ref, y_ref, wo_ref, wg_ref, wp_ref, vec_ref,
             dxr_ref, dy_ref, dwo_ref, dwg_ref, dwp_ref, dvec_ref):
        @pl.when(pl.program_id(0) == 0)
        def _():
            dwo_ref[...] = jnp.zeros_like(dwo_ref)
            dwg_ref[...] = jnp.zeros_like(dwg_ref)
            dwp_ref[...] = jnp.zeros_like(dwp_ref)
            dvec_ref[...] = jnp.zeros_like(dvec_ref)

        dxn = dxn_ref[...]
        gate = gate_ref[...]
        xh, rs = _ln_stats(r_ref[...])
        hb = _mm(xh * vec_ref[1:2, :] + vec_ref[2:3, :])
        pb = _mm(p_ref[...])
        pe = jnp.dot(pb, wp_ref[...], preferred_element_type=F32)
        dpre = dxn * pe * gate * (1.0 - gate)
        dpb = _mm(dpre)
        dh = dxn + _dot_nt(dpb, wg_ref[...])
        dwg_ref[...] += _dot_tn(hb, dpb)
        dwp_ref[...] += _dot_tn(pb, dxn * gate)
        dr = _ln_bwd(dh * vec_ref[1:2, :], xh, rs)
        drb = _mm(dr)
        dxr_ref[...] = ALPHA * dr
        dy_ref[...] = _dot_nt(drb, wo_ref[...])
        dwo_ref[...] += _dot_tn(y_ref[...], drb)
        dvec_ref[0:1, :] += _rowsum(dr)
        dvec_ref[1:2, :] += _rowsum(dh * xh)
        dvec_ref[2:3, :] += _rowsum(dh)
        dvec_ref[3:4, :] += _rowsum(dpre)

    row = lambda c: pl.BlockSpec((tm, c), lambda i: (i, 0))
    full = lambda shape: pl.BlockSpec(shape, lambda i: (0,) * len(shape), pipeline_mode=pl.Buffered(1))
    acc_shapes = [(d, d), (d, d), (D_PLE, d), (8, d)]
    return pl.pallas_call(
        body, name="post_bwd", grid=(t // tm,),
        in_specs=[row(d), row(d), row(d), pl.BlockSpec((None, tm, D_PLE), lambda i: (layer, i, 0)), row(d),
                  full((d, d)), full((d, d)), full((D_PLE, d)), full((8, d))],
        out_specs=[row(d), row(d)] + [full(sh) for sh in acc_shapes],
        out_shape=[jax.ShapeDtypeStruct((t, d), F32)] * 2 + [jax.ShapeDtypeStruct(sh, F32) for sh in acc_shapes],
        compiler_params=_cparams(("arbitrary",)),
    )(dxn, r2, gate2, p_all, y2, w_out, wg, wple, vec)


def _place():
    x, y, c = lax.axis_index("x"), lax.axis_index("y"), lax.axis_index("c")
    chips = [(1 - x, y), (x, 1 - y), (1 - x, 1 - y)]
    return x, y, c, chips


def _shard_of(ref, ax, k, width, lo=None, ln=None):
    idx = [slice(None)] * 3
    if lo is not None:
        idx[0] = pl.ds(lo, ln)
    idx[ax] = pl.ds(k * width, width)
    return ref.at[tuple(idx)]


def _remote(src, dst, ssem, rsem, dev):
    return pltpu.make_async_remote_copy(src_ref=src, dst_ref=dst, send_sem=ssem, recv_sem=rsem, device_id=dev, device_id_type=MESH)


def _place_shard(w, ax, chip, dtype=MM_DTYPE, layer=None):
    l_dim, a_dim, b_dim = w.shape
    tr = min(256, a_dim)
    per = a_dim // tr
    first = 0
    if layer is not None:
        l_dim, first = 1, layer
    shape = [l_dim, a_dim, b_dim]
    shape[ax] *= 4
    if ax == 2:
        out_spec = pl.BlockSpec((1, tr, b_dim), lambda l, i, k: (l, i, k[0]))
    else:
        out_spec = pl.BlockSpec((1, tr, b_dim), lambda l, i, k: (l, k[0] * per + i, 0))

    def body(k_ref, w_ref, o_ref):
        o_ref[...] = w_ref[...].astype(o_ref.dtype)

    return pl.pallas_call(
        body, name="place_shard",
        grid_spec=pltpu.PrefetchScalarGridSpec(
            num_scalar_prefetch=1, grid=(l_dim, per),
            in_specs=[pl.BlockSpec((1, tr, b_dim), lambda l, i, k: (first + l, i, 0))], out_specs=out_spec),
        out_shape=jax.ShapeDtypeStruct(tuple(shape), dtype),
        compiler_params=_cparams(("parallel", "parallel")),
    )(chip, w)


def _shard_copies(src_refs, dst_refs, axes, sems):
    x, y, c, chips = _place()
    j = 2 * x + y
    cps = []
    for a, (s_ref, d_ref) in enumerate(zip(src_refs, dst_refs)):
        w = d_ref.shape[axes[a]] // 4
        for q, (qx, qy) in enumerate(chips):
            ssem, rsem = sems[3 * a + q]
            cps.append(_remote(_shard_of(s_ref, axes[a], j, w), _shard_of(d_ref, axes[a], j, w), ssem, rsem, (qx, qy, c)))
    return cps


def _chip_handshake():
    x, y, c, chips = _place()
    barrier = pltpu.get_barrier_semaphore()
    for qx, qy in chips:
        pl.semaphore_signal(barrier, inc=1, device_id=(qx, qy, c), device_id_type=MESH)
    pl.semaphore_wait(barrier, 3)


def _gather_async(name, collective_id, fulls, axes):
    n = len(fulls)
    refs = [jax.new_ref(f, memory_space=pltpu.MemorySpace.HBM) for f in fulls]
    moved = sum(2 * 3 * (f.size // 4) * jnp.dtype(f.dtype).itemsize for f in fulls)

    @pl.kernel(mesh=plsc.ScalarSubcoreMesh(axis_name="seq", num_cores=1), name=name,
               scratch_types=(pltpu.SemaphoreType.DMA,) * (6 * n),
               cost_estimate=pl.CostEstimate(flops=0, transcendentals=0, bytes_accessed=moved, remote_bytes_transferred=moved),
               compiler_params=pltpu.CompilerParams(collective_id=collective_id))
    def launch(*sems):
        _chip_handshake()
        cps = _shard_copies(refs, refs, axes, [(sems[2 * k], sems[2 * k + 1]) for k in range(3 * n)])
        for cp in cps:
            cp.start()
        for cp in cps:
            cp.wait()

    launch()
    return refs


def _gather_first_async(collective_id, w_full, sv_full):
    w_ref = jax.new_ref(w_full, memory_space=pltpu.MemorySpace.HBM)
    sv_ref = jax.new_ref(sv_full, memory_space=pltpu.MemorySpace.HBM)
    a_half = w_full.shape[1] // 2
    width = w_full.shape[2] // 4
    moved = 2 * 3 * (w_full.size // 4 + sv_full.size // 4 * 4)

    @pl.kernel(mesh=plsc.ScalarSubcoreMesh(axis_name="seq", num_cores=1), name="gather_first",
               scratch_types=(pltpu.SemaphoreType.DMA,) * 18,
               cost_estimate=pl.CostEstimate(flops=0, transcendentals=0, bytes_accessed=moved, remote_bytes_transferred=moved),
               compiler_params=pltpu.CompilerParams(collective_id=collective_id))
    def launch(*sems):
        x, y, c, chips = _place()
        barrier = pltpu.get_barrier_semaphore()
        for dev in [(qx, qy, c) for qx, qy in chips] + [(x, y, 1 - c)]:
            pl.semaphore_signal(barrier, inc=1, device_id=dev, device_id_type=MESH)
        pl.semaphore_wait(barrier, 4)
        j = 2 * x + y

        def piece(k, half):
            return w_ref.at[:, pl.ds(half * a_half, a_half), pl.ds(k * width, width)]

        small = _shard_copies([sv_ref], [sv_ref], [1], [(sems[2 * q], sems[2 * q + 1]) for q in range(3)])
        sends = [_remote(piece(j, c), piece(j, c), sems[6 + 2 * q], sems[7 + 2 * q], (qx, qy, c)) for q, (qx, qy) in enumerate(chips)]
        for cp in small + sends:
            cp.start()
        passed = []
        for q, (qx, qy) in enumerate(chips):
            got = piece(2 * qx + qy, c)
            _remote(got, got, sems[6 + 2 * q], sems[7 + 2 * q], (x, y, c)).wait_recv()
            cp = _remote(got, got, sems[12 + 2 * q], sems[13 + 2 * q], (x, y, 1 - c))
            cp.start()
            passed.append(cp)
        for q, (qx, qy) in enumerate(chips):
            other = piece(2 * qx + qy, 1 - c)
            _remote(other, other, sems[12 + 2 * q], sems[13 + 2 * q], (x, y, c)).wait_recv()
        for cp in small:
            cp.wait()
        for cp in sends + passed:
            cp.wait_send()

    launch()
    return w_ref, sv_ref


def _scatter_async(name, collective_id, grads, axes, after):
    n = len(grads)
    extra = len(after)
    outs = []
    for g, ax in zip(grads, axes):
        sh = list(g.shape)
        sh[ax] //= 4
        outs.append(jax.ShapeDtypeStruct((3,) + tuple(sh), g.dtype))

    def body(*refs):
        srcs, lands, sems = refs[:n], refs[n + extra:2 * n + extra], refs[2 * n + extra:]
        _chip_handshake()
        x, y, c, chips = _place()
        cps = []
        for a in range(n):
            w = srcs[a].shape[axes[a]] // 4
            for q, (qx, qy) in enumerate(chips):
                k = 3 * a + q
                cps.append(_remote(_shard_of(srcs[a], axes[a], 2 * qx + qy, w), lands[a].at[q], sems[2 * k], sems[2 * k + 1],
                                   (qx, qy, c)))
        for cp in cps:
            cp.start()
        for cp in cps:
            cp.wait()

    moved = sum(2 * 3 * math.prod(o.shape[1:]) * jnp.dtype(o.dtype).itemsize for o in outs)
    return pl.kernel(body, out_type=outs, mesh=plsc.ScalarSubcoreMesh(axis_name="seq", num_cores=1), name=name,
                     scratch_types=(pltpu.SemaphoreType.DMA,) * (6 * n),
                     cost_estimate=pl.CostEstimate(flops=0, transcendentals=0, bytes_accessed=moved, remote_bytes_transferred=moved),
                     compiler_params=pltpu.CompilerParams(collective_id=collective_id))(*grads, *after)


def _pair_swap_async(name, collective_id, sums):
    n = len(sums)

    def body(*refs):
        g_refs, o_refs, sems = refs[:n], refs[n:2 * n], refs[2 * n:]
        x, y, c, _ = _place()
        barrier = pltpu.get_barrier_semaphore()
        pl.semaphore_signal(barrier, inc=1, device_id=(x, y, 1 - c), device_id_type=MESH)
        pl.semaphore_wait(barrier, 1)
        cps = [_remote(g_refs[a], o_refs[a], sems[2 * a], sems[2 * a + 1], (x, y, 1 - c)) for a in range(n)]
        for cp in cps:
            cp.start()
        for cp in cps:
            cp.wait()

    moved = sum(2 * g.size * jnp.dtype(g.dtype).itemsize for g in sums)
    return pl.kernel(body, out_type=[jax.ShapeDtypeStruct(g.shape, g.dtype) for g in sums],
                     mesh=plsc.ScalarSubcoreMesh(axis_name="seq", num_cores=1), name=name,
                     scratch_types=(pltpu.SemaphoreType.DMA,) * (2 * n),
                     cost_estimate=pl.CostEstimate(flops=0, transcendentals=0, bytes_accessed=moved, remote_bytes_transferred=moved),
                     compiler_params=pltpu.CompilerParams(collective_id=collective_id))(*sums)


def _pair_exchange(grads, small):
    n = len(grads)
    outs = [jax.ShapeDtypeStruct((g.shape[0] // 2,) + g.shape[1:], g.dtype) for g in grads]
    outs.append(jax.ShapeDtypeStruct((small.shape[0] // 2, small.shape[1]), small.dtype))

    def body(*refs):
        g_refs, o_refs = refs[:n + 1], refs[n + 1:2 * n + 2]
        ssem, rsem = refs[2 * n + 2:]
        x, y, c, _ = _place()
        cps = []
        for a in range(n + 1):
            lh = g_refs[a].shape[0] // 2
            cp = _remote(g_refs[a].at[pl.ds((1 - c) * lh, lh)], o_refs[a], ssem.at[a], rsem.at[a], (x, y, 1 - c))
            cp.start()
            cps.append(cp)
        for cp in cps:
            cp.wait()

    return pl.pallas_call(
        body, name="pair_exchange", in_specs=[ANY] * (n + 1), out_specs=[ANY] * (n + 1), out_shape=outs,
        scratch_shapes=[pltpu.SemaphoreType.DMA((n + 1,)), pltpu.SemaphoreType.DMA((n + 1,))],
        compiler_params=pltpu.CompilerParams(has_side_effects=True),
    )(*grads, small)


def _chip_scatter(sums, axes, small):
    n = len(sums)
    outs = []
    for g, ax in zip(sums, axes):
        sh = list(g.shape)
        sh[ax] //= 4
        outs.append(jax.ShapeDtypeStruct((3,) + tuple(sh), g.dtype))
    rq = small.shape[0] // 4
    outs.append(jax.ShapeDtypeStruct((3, rq, small.shape[1]), small.dtype))

    def body(*refs):
        g_refs, o_refs = refs[:n + 1], refs[n + 1:2 * n + 2]
        ssem, rsem = refs[2 * n + 2:]
        x, y, c, chips = _place()
        cps = []
        for a in range(n + 1):
            for q, (qx, qy) in enumerate(chips):
                k = 2 * qx + qy
                if a < n:
                    src = _shard_of(g_refs[a], axes[a], k, g_refs[a].shape[axes[a]] // 4)
                else:
                    src = g_refs[a].at[pl.ds(k * rq, rq)]
                cp = _remote(src, o_refs[a].at[q], ssem.at[a * 3 + q], rsem.at[a * 3 + q], (qx, qy, c))
                cp.start()
                cps.append(cp)
        for cp in cps:
            cp.wait()

    return pl.pallas_call(
        body, name="chip_scatter", in_specs=[ANY] * (n + 1), out_specs=[ANY] * (n + 1), out_shape=outs,
        scratch_shapes=[pltpu.SemaphoreType.DMA((3 * n + 3,)), pltpu.SemaphoreType.DMA((3 * n + 3,))],
        compiler_params=pltpu.CompilerParams(has_side_effects=True),
    )(*sums, small)


def _final_exchange(reds, small):
    n = len(reds)
    flips = [(fx, fy, fc) for fx in (0, 1) for fy in (0, 1) for fc in (0, 1)][1:]

    def body(*refs):
        g_refs, o_refs = refs[:n + 1], refs[n + 1:2 * n + 2]
        ssem, rsem = refs[2 * n + 2:]
        x, y, c, _ = _place()
        cps = []
        for a in range(n):
            lh = g_refs[a].shape[0] // 2
            cp = _remote(g_refs[a].at[pl.ds(c * lh, lh)], o_refs[a].at[pl.ds(c * lh, lh)], ssem.at[a], rsem.at[a], (x, y, 1 - c))
            cp.start()
            cps.append(cp)
        mine = 4 * c + 2 * x + y
        for f, (fx, fy, fc) in enumerate(flips):
            cp = _remote(g_refs[n].at[mine], o_refs[n].at[mine], ssem.at[n + f], rsem.at[n + f], (x ^ fx, y ^ fy, c ^ fc))
            cp.start()
            cps.append(cp)
        for cp in cps:
            cp.wait()

    return pl.pallas_call(
        body, name="final_exchange", in_specs=[ANY] * (n + 1), out_specs=[ANY] * (n + 1),
        out_shape=[jax.ShapeDtypeStruct(g.shape, g.dtype) for g in reds] + [jax.ShapeDtypeStruct(small.shape, small.dtype)],
        input_output_aliases={a: a for a in range(n + 1)},
        scratch_shapes=[pltpu.SemaphoreType.DMA((n + 7,)), pltpu.SemaphoreType.DMA((n + 7,))],
        compiler_params=pltpu.CompilerParams(has_side_effects=True),
    )(*reds, small)


def _small_allreduce(small):
    r = small.shape[0]
    rh, rq = r // 2, r // 8
    flips = [(fx, fy, fc) for fx in (0, 1) for fy in (0, 1) for fc in (0, 1)][1:]

    def body(g_ref, out_ref, pair_ref, chip_ref, s1_ref, ssem, rsem):
        x, y, c, chips = _place()
        cp = _remote(g_ref.at[pl.ds((1 - c) * rh, rh)], pair_ref, ssem.at[0], rsem.at[0], (x, y, 1 - c))
        cp.start()
        cp.wait()
        s1_ref[...] = g_ref[pl.ds(pl.multiple_of(c * rh, 8), rh), :] + pair_ref[...]
        cps = [_remote(s1_ref.at[pl.ds((2 * qx + qy) * rq, rq)], chip_ref.at[q], ssem.at[1 + q], rsem.at[1 + q], (qx, qy, c))
               for q, (qx, qy) in enumerate(chips)]
        for cp in cps:
            cp.start()
        for cp in cps:
            cp.wait()
        mine = out_ref.at[pl.ds(pl.multiple_of((4 * c + 2 * x + y) * rq, 8), rq)]
        mine[...] = ((s1_ref[pl.ds(pl.multiple_of((2 * x + y) * rq, 8), rq), :] + chip_ref[0]) + chip_ref[1]) + chip_ref[2]
        cps = [_remote(mine, mine, ssem.at[4 + f], rsem.at[4 + f], (x ^ fx, y ^ fy, c ^ fc)) for f, (fx, fy, fc) in enumerate(flips)]
        for cp in cps:
            cp.start()
        for cp in cps:
            cp.wait()

    vm = pl.BlockSpec(memory_space=pltpu.VMEM)
    return pl.pallas_call(
        body, name="small_allreduce", in_specs=[vm], out_specs=vm, out_shape=jax.ShapeDtypeStruct(small.shape, F32),
        scratch_shapes=[pltpu.VMEM((rh, 128), F32), pltpu.VMEM((3, rq, 128), F32), pltpu.VMEM((rh, 128), F32),
                        pltpu.SemaphoreType.DMA((11,)), pltpu.SemaphoreType.DMA((11,))],
        compiler_params=pltpu.CompilerParams(has_side_effects=True, vmem_limit_bytes=VMEM_LIMIT),
    )(small)


def _to_wire(g):
    _, a_dim, b_dim = g.shape
    tr = min(256, a_dim)

    def body(g_ref, o_ref, done_ref):
        o_ref[...] = g_ref[...].astype(o_ref.dtype)
        done_ref[...] = jnp.zeros_like(done_ref)

    blk = pl.BlockSpec((1, tr, b_dim), lambda i: (0, i, 0))
    return pl.pallas_call(
        body, name="to_wire", grid=(a_dim // tr,), in_specs=[blk], out_specs=[blk, pl.BlockSpec((8, 128), lambda i: (0, 0))],
        out_shape=[jax.ShapeDtypeStruct(g.shape, WIRE_DTYPE), jax.ShapeDtypeStruct((8, 128), F32)],
        compiler_params=_cparams(("arbitrary",)),
    )(g)


def _chip_sum(owns, gots, ax, chip, after=None):
    n_layers = len(owns)
    tail = [] if after is None else [after]
    _, _, a_dim, b_dim = gots[0].shape
    tr = min(256, a_dim)
    per = a_dim // tr

    def own_spec(layer):
        if ax == 2:
            return pl.BlockSpec((1, tr, b_dim), lambda l, i, k: (0, jnp.where(l == layer, i, 0), k[0]))
        return pl.BlockSpec((1, tr, b_dim), lambda l, i, k: (0, k[0] * per + jnp.where(l == layer, i, 0), 0))

    def got_spec(layer):
        return pl.BlockSpec((3, 1, tr, b_dim), lambda l, i, k: (0, 0, jnp.where(l == layer, i, 0), 0))

    def body(k_ref, *refs):
        s_ref = refs[-1]
        for layer in range(n_layers):
            @pl.when(pl.program_id(0) == layer)
            def _(own_ref=refs[layer], got_ref=refs[n_layers + layer]):
                s_ref[...] = ((own_ref[...].astype(F32) + got_ref[0].astype(F32)) + got_ref[1].astype(F32)) + got_ref[2].astype(F32)

    return pl.pallas_call(
        body, name="chip_sum",
        grid_spec=pltpu.PrefetchScalarGridSpec(
            num_scalar_prefetch=1, grid=(n_layers, per),
            in_specs=[own_spec(l) for l in range(n_layers)] + [got_spec(l) for l in range(n_layers)]
            + [pl.BlockSpec((8, 128), lambda l, i, k: (0, 0))] * len(tail),
            out_specs=pl.BlockSpec((1, tr, b_dim), lambda l, i, k: (l, i, 0))),
        out_shape=jax.ShapeDtypeStruct((n_layers, a_dim, b_dim), F32),
        compiler_params=_cparams(("arbitrary", "arbitrary")),
    )(chip, *owns, *gots, *tail)


def _small_pair_sum(small, got, half):
    rh = small.shape[0] // 2

    def body(h_ref, g_ref, o_ref, s_ref):
        s_ref[...] = g_ref[...] + o_ref[...]

    return pl.pallas_call(
        body, name="small_pair_sum",
        grid_spec=pltpu.PrefetchScalarGridSpec(
            num_scalar_prefetch=1, grid=(1,),
            in_specs=[pl.BlockSpec((rh, 128), lambda i, h: (h[0], 0)), pl.BlockSpec((rh, 128), lambda i, h: (0, 0))],
            out_specs=pl.BlockSpec((rh, 128), lambda i, h: (0, 0))),
        out_shape=jax.ShapeDtypeStruct((rh, 128), F32),
    )(half, small, got)


def _small_chip_sum(s1, got, pos):
    rq = got.shape[1]

    def body(k_ref, own_ref, got_ref, s_ref):
        s_ref[0] = ((own_ref[...] + got_ref[0]) + got_ref[1]) + got_ref[2]

    return pl.pallas_call(
        body, name="small_chip_sum",
        grid_spec=pltpu.PrefetchScalarGridSpec(
            num_scalar_prefetch=1, grid=(1,),
            in_specs=[pl.BlockSpec((rq, 128), lambda i, k: (k[0], 0)), pl.BlockSpec((3, rq, 128), lambda i, k: (0, 0, 0))],
            out_specs=pl.BlockSpec((1, rq, 128), lambda i, k: (4 * k[1] + k[0], 0, 0))),
        out_shape=jax.ShapeDtypeStruct((8, rq, 128), F32),
    )(pos, s1, got)


def _adam_math(w, g, m, v):
    m = ADAM_B1 * m + (1.0 - ADAM_B1) * g
    v = ADAM_B2 * v + (1.0 - ADAM_B2) * (g * g)
    m_hat = m / (1.0 - ADAM_B1 ** ADAM_STEP)
    v_hat = v / (1.0 - ADAM_B2 ** ADAM_STEP)
    return -ADAM_LR * (m_hat / (jnp.sqrt(v_hat) + ADAM_EPS) + ADAM_WD * w), m, v


def _adamw_big(w, g_mine, g_other, m, v):
    l_dim, a_dim, b_dim = w.shape
    tr = min(256, a_dim)

    def body(w_ref, g1_ref, g2_ref, m_ref, v_ref, g_ref, d_ref, nm_ref, nv_ref):
        g = g1_ref[...] + g2_ref[...]
        g_ref[...] = g
        d_ref[...], nm_ref[...], nv_ref[...] = _adam_math(w_ref[...], g, m_ref[...], v_ref[...])

    blk = pl.BlockSpec((1, tr, b_dim), lambda l, i: (l, i, 0))
    return pl.pallas_call(
        body, name="adamw_big", grid=(l_dim, a_dim // tr), in_specs=[blk] * 5, out_specs=[blk] * 4,
        out_shape=[jax.ShapeDtypeStruct(w.shape, F32)] * 4,
        compiler_params=_cparams(("parallel", "parallel")),
    )(w, g_mine, g_other, m, v)


def _adamw_small(ws, gs, ms, vs):
    n = len(ws)

    def body(*refs):
        for i in range(n):
            w_ref, g_ref, m_ref, v_ref = refs[i], refs[n + i], refs[2 * n + i], refs[3 * n + i]
            d_ref, nm_ref, nv_ref = refs[4 * n + i], refs[5 * n + i], refs[6 * n + i]
            d_ref[...], nm_ref[...], nv_ref[...] = _adam_math(w_ref[...], g_ref[...], m_ref[...], v_ref[...])

    shapes = [jax.ShapeDtypeStruct(w.shape, F32) for w in ws]
    outs = pl.pallas_call(body, name="adamw_small", out_shape=shapes * 3,
                          compiler_params=_cparams())(*ws, *gs, *ms, *vs)
    return outs[:n], outs[n:2 * n], outs[2 * n:]


def _pack(arrs, row_mult):
    parts = []
    for a in arrs:
        flat = a.reshape(-1)
        pad = (-flat.shape[0]) % 1024
        parts.append(jnp.pad(flat, (0, pad)).reshape(-1, 128))
    buf = jnp.concatenate(parts, axis=0)
    pad = (-buf.shape[0]) % row_mult
    return jnp.pad(buf, ((0, pad), (0, 0)))


def _unpack(buf, shapes):
    out, row = [], 0
    for sh in shapes:
        n = math.prod(sh)
        rows = -(-n // 1024) * 8
        out.append(buf[row:row + rows].reshape(-1)[:n].reshape(sh))
        row += rows
    return out


_NAMES = ['w_in_e', 'b_in_e', 'conv_a_w', 'conv_a_b', 'ln_a_g', 'ln_a_b', 'ln_v_g', 'ln_v_b', 'w_s', 'b_s', 'w_out_e', 'b_out_e',
          'w_in_o', 'b_in_o', 'w_pool', 'pool_scale', 'conv_d_w', 'w_out_o', 'b_out_o', 'ln_g', 'ln_b', 'w_ple', 'w_ple_gate',
          'b_ple_gate']
_BIG = ['w_in_e', 'w_out_e', 'w_in_o', 'w_out_o', 'w_ple', 'w_ple_gate']
_BIG_AXES = [2, 1, 2, 1, 2, 1]
_SMALL_SHARDED = ['conv_a_w', 'b_in_o', 'pool_scale', 'conv_d_w', 'b_out_o']


def kernel(x, p, w_in_e, b_in_e, conv_a_w, conv_a_b, ln_a_g, ln_a_b, ln_v_g, ln_v_b, w_s, b_s, w_out_e, b_out_e, w_in_o, b_in_o, w_pool, pool_scale, conv_d_w, w_out_o, b_out_o, ln_g, ln_b, w_ple, w_ple_gate, b_ple_gate, loss_target, m_w_in_e, m_b_in_e, m_conv_a_w, m_conv_a_b, m_ln_a_g, m_ln_a_b, m_ln_v_g, m_ln_v_b, m_w_s, m_b_s, m_w_out_e, m_b_out_e, m_w_in_o, m_b_in_o, m_w_pool, m_pool_scale, m_conv_d_w, m_w_out_o, m_b_out_o, m_ln_g, m_ln_b, m_w_ple, m_w_ple_gate, m_b_ple_gate, v_w_in_e, v_b_in_e, v_conv_a_w, v_conv_a_b, v_ln_a_g, v_ln_a_b, v_ln_v_g, v_ln_v_b, v_w_s, v_b_s, v_w_out_e, v_b_out_e, v_w_in_o, v_b_in_o, v_w_pool, v_pool_scale, v_conv_d_w, v_w_out_o, v_b_out_o, v_ln_g, v_ln_b, v_w_ple, v_w_ple_gate, v_b_ple_gate):
    args = locals()
    wts = {n: args[n] for n in _NAMES}
    mom = {n: args["m_" + n] for n in _NAMES}
    var = {n: args["v_" + n] for n in _NAMES}
    bsz, s_len, d = x.shape
    t = bsz * s_len
    cx, cy, cc = lax.axis_index("x"), lax.axis_index("y"), lax.axis_index("c")
    chip = (2 * cx + cy).astype(jnp.int32).reshape(1)
    half = cc.astype(jnp.int32).reshape(1)
    pos = jnp.concatenate([chip, half])

    def placed(name, ax, layer):
        return _place_shard(wts[name], ax, chip, layer=layer)

    sv = _pack([wts[n] for n in _SMALL_SHARDED], 8)
    first_refs = _gather_first_async(DEPTH, placed('w_in_e', 2, 0), _place_shard(sv[None], 1, chip, F32))
    layer_refs = []
    for i in range(DEPTH):
        sfx = '_e' if i % 2 == 0 else '_o'
        items = [('w_out' + sfx, 1, i // 2), ('w_ple_gate', 1, i), ('w_ple', 2, i)] + ([('w_in' + sfx, 2, i // 2)] if i else [])
        layer_refs.append(_gather_async("gather_layer%d" % i, i, [placed(*it) for it in items], [it[1] for it in items]))
    fw = {}
    w_in_first = first_refs[0][...]
    sv_all = first_refs[1][...].reshape((4,) + sv.shape)
    small_parts = [_unpack(sv_all[k], [wts[n].shape for n in _SMALL_SHARDED]) for k in range(4)]
    for i, n in enumerate(_SMALL_SHARDED):
        fw[n] = jnp.concatenate([small_parts[k][i] for k in range(4)], axis=-1)
    for n in _NAMES:
        fw.setdefault(n, wts[n])

    def row8(rows, width):
        rows = [r.reshape(1, width) for r in rows]
        return jnp.concatenate(rows + [jnp.zeros((8 - len(rows), width), F32)], axis=0)

    x2 = x.reshape(t, d)
    saved = []
    for i in range(DEPTH):
        j = i // 2
        even = i % 2 == 0
        b_in, b_out = (fw['b_in_e'], fw['b_out_e']) if even else (fw['b_in_o'], fw['b_out_o'])
        w_in = layer_refs[i][3][...][0] if i else w_in_first[0]
        z = _in_proj(x2, w_in, b_in[j].reshape(1, N_COLS))
        z3 = z.reshape(bsz, s_len, N_COLS)
        if even:
            cw = jnp.concatenate([fw['conv_a_w'][j], jnp.zeros((1, W_BR), F32)], axis=0)
            mvec = row8([fw['conv_a_b'][j], fw['ln_a_g'][j], fw['ln_a_b'][j], fw['ln_v_g'][j], fw['ln_v_b'][j]], W_BR)
            bsf = jnp.repeat(fw['b_s'][j].T, W_BR // 8, axis=1)
            y3, a13 = _even_fwd(z3, cw, mvec, fw['w_s'][j], bsf)
            mix = (a13, cw, mvec, fw['w_s'][j], jnp.swapaxes(fw['w_s'][j], 1, 2), bsf)
        else:
            mvec = row8([fw['pool_scale'][j]] + [fw['conv_d_w'][j][k] for k in range(3)], W_BR)
            mix = (fw['w_pool'][j].astype(MM_DTYPE), mvec)
            y3 = _odd_fwd(z3, mix[0], mvec)
        pvec = row8([b_out[j], fw['ln_g'][i], fw['ln_b'][i], fw['b_ple_gate'][i]], d)
        post_w = (layer_refs[i][0][...][0], layer_refs[i][1][...][0], layer_refs[i][2][...][0], pvec)
        p2 = (p.reshape(DEPTH, t, D_PLE), i)
        y2 = y3.reshape(t, 2 * W_BR)
        if i < DEPTH - 1:
            xn, r2, gate2 = _post_fwd(y2, x2, p2, *post_w)
        else:
            dx, r2, gate2, sq = _post_fwd(y2, x2, p2, *post_w, tgt=loss_target.reshape(t, d))
        saved.append((x2, z3, y2, r2, gate2, p2, w_in, mix, post_w))
        x2 = xn

    gr = {n: [None] * wts[n].shape[0] for n in _NAMES}
    prev_lands = []
    for i in reversed(range(DEPTH)):
        j = i // 2
        even = i % 2 == 0
        x_in, z3, y2, r2, gate2, p2, w_in, mix, post_w = saved[i]
        dxr, dy, dwo, dwg, dwp, dpv = _post_bwd(dx, r2, gate2, p2, y2, *post_w)
        dy3 = dy.reshape(bsz, s_len, 2 * W_BR)
        sfx = '_e' if even else '_o'
        gr['w_out' + sfx][j], gr['b_out' + sfx][j] = dwo, dpv[0]
        gr['w_ple_gate'][i], gr['w_ple'][i] = dwg, dwp
        gr['ln_g'][i], gr['ln_b'][i], gr['b_ple_gate'][i] = dpv[1], dpv[2], dpv[3]
        if even:
            dz3, dbin, dcw, dmv, dws, dbsf = _even_bwd(z3, dy3, *mix)
            gr['conv_a_w'][j], gr['conv_a_b'][j] = dcw[:KA], dmv[0]
            gr['ln_a_g'][j], gr['ln_a_b'][j], gr['ln_v_g'][j], gr['ln_v_b'][j] = dmv[1], dmv[2], dmv[3], dmv[4]
            gr['w_s'][j] = dws
            gr['b_s'][j] = jnp.sum(dbsf.reshape(GBLK, 8, W_BR // 8), axis=2).T
        else:
            dz3, dbin, dwpool, dmv = _odd_bwd(z3, dy3, *mix)
            gr['w_pool'][j], gr['pool_scale'][j], gr['conv_d_w'][j] = dwpool, dmv[0], dmv[1:4]
        gr['b_in' + sfx][j] = dbin[0]
        dz2 = dz3.reshape(t, N_COLS)
        dwi, dw_done = _in_proj_bwd_dw(x_in, dz2)
        post_items = [('w_out' + sfx, j, dwo[None], 1), ('w_ple_gate', i, dwg[None], 1), ('w_ple', i, dwp[None], 2)]
        in_item = ('w_in' + sfx, j, dwi[None], 2)
        batches = [post_items + [in_item]] if i else [post_items, [in_item]]
        for bi, items in enumerate(batches):
            sent = [it[2] for it in items]
            if i == 0:
                cast = [_to_wire(g) for g in sent]
                sent = [w for w, _ in cast]
                if bi == 1:
                    dw_done = cast[0][1]
            lands = _scatter_async("scatter_layer%d_%d" % (i, bi), DEPTH + 1 + 2 * i + bi, sent, [it[3] for it in items],
                                   prev_lands[:1] + ([dbin] if (i == 0 and bi == 0) else []))
            prev_lands = list(lands)
            for it, land in zip(items, lands):
                gr[it[0]][it[1]] = (it[2], land, it[3])
        dx = _in_proj_bwd_dx(dxr, dz2, w_in, dw_done)
    grad_x = dx.reshape(bsz, s_len, d)

    small_names = [n for n in _NAMES if n not in _BIG]
    g_small_full = [jnp.stack(gr[n]) for n in small_names] + [sq]
    small_all = _small_allreduce(_pack(g_small_full, 64))
    *g_small, sq_all = _unpack(small_all, [g.shape for g in g_small_full])
    loss = 0.5 * jnp.sum(sq_all) / d
    big_order = _BIG[1:] + _BIG[:1]
    sums = [_chip_sum([g[0] for g in gr[n]], [g[1] for g in gr[n]], gr[n][0][2], chip, dx if n == big_order[-1] else None)
            for n in big_order]
    others = (list(_pair_swap_async("pair_swap_a", 3 * DEPTH + 1, sums[:-1]))
              + list(_pair_swap_async("pair_swap_b", 3 * DEPTH + 2, sums[-1:])))
    grads = {}
    for n, g in zip(small_names, g_small):
        if n in _SMALL_SHARDED:
            w = wts[n].shape[-1]
            g = lax.dynamic_slice_in_dim(g, (2 * cx + cy) * w, w, axis=g.ndim - 1)
        grads[n] = g

    delta, new_m, new_v = {}, {}, {}
    ds, ms, vs = _adamw_small([wts[n] for n in small_names], [grads[n] for n in small_names],
                              [mom[n] for n in small_names], [var[n] for n in small_names])
    for n, a, b, c_ in zip(small_names, ds, ms, vs):
        delta[n], new_m[n], new_v[n] = a, b, c_
    for n, mine, other in zip(big_order, sums, others):
        grads[n], delta[n], new_m[n], new_v[n] = _adamw_big(wts[n], mine, other, mom[n], var[n])

    return (loss, grad_x, *[grads[n] for n in _NAMES], *[delta[n] for n in _NAMES],
            *[new_m[n] for n in _NAMES], *[new_v[n] for n in _NAMES])
```

```python
import functools
import math

import jax
import jax.numpy as jnp
from jax import lax
from jax.experimental import pallas as pl
from jax.experimental.pallas import tpu as pltpu
from jax.experimental.pallas import tpu_sc as plsc

F32 = jnp.float32
MM_DTYPE = jnp.bfloat16
WIRE_DTYPE = jnp.bfloat16
SEQ_TILE = 512
ROW_TILE = 512
BWD_ROW_TILE = 512
HALO = 32
CONV_CHUNK = 64
DW_CHUNK = 32
ROW_CHUNK = 64
GBLK = 128
VMEM_LIMIT = 56 * 1024 * 1024

D_MODEL = 1024
W_BR = 512
N_COLS = 6 * W_BR
D_PLE = 256
KA = 31
DEPTH = 4
POOL_WINDOWS = (2, 4, 8, 16)
ALPHA = (2.0 * DEPTH) ** 0.25
LN_EPS = 1e-5
GELU_C = math.sqrt(2.0 / math.pi)

ADAM_LR, ADAM_B1, ADAM_B2, ADAM_EPS, ADAM_WD, ADAM_STEP = 0.001, 0.9, 0.999, 1e-08, 0.01, 10

MESH = pl.DeviceIdType.MESH
ANY = pl.BlockSpec(memory_space=pl.ANY)


def _cparams(sem=None):
    return pltpu.CompilerParams(dimension_semantics=sem, vmem_limit_bytes=VMEM_LIMIT)


def _sigmoid(x):
    return 1.0 / (1.0 + jnp.exp(-x))


def _silu(x):
    return x * _sigmoid(x)


def _silu_grad(x):
    s = _sigmoid(x)
    return x * s, s * (1.0 + x * (1.0 - s))


def _gelu(x):
    return 0.5 * x * (1.0 + jnp.tanh(GELU_C * (x + 0.044715 * (x * x * x))))


def _gelu_grad(x):
    x2 = x * x
    th = jnp.tanh(GELU_C * (x + 0.044715 * (x * x2)))
    return 0.5 * x * (1.0 + th), 0.5 * (1.0 + th) + 0.5 * x * (1.0 - th * th) * (GELU_C * (1.0 + 3.0 * 0.044715 * x2))


def _ln_stats(x):
    mu = jnp.mean(x, axis=-1, keepdims=True)
    d = x - mu
    var = jnp.mean(d * d, axis=-1, keepdims=True)
    rs = lax.rsqrt(var + LN_EPS)
    return d * rs, rs


def _ln_bwd(dxh, xh, rs):
    return rs * (dxh - jnp.mean(dxh, axis=-1, keepdims=True) - xh * jnp.mean(dxh * xh, axis=-1, keepdims=True))


def _mm(a):
    return a.astype(MM_DTYPE)


def _dot(a, b):
    return jnp.dot(_mm(a), _mm(b), preferred_element_type=F32)


def _dot_nt(a, b):
    return lax.dot_general(_mm(a), _mm(b), (((1,), (1,)), ((), ())), preferred_element_type=F32)


def _dot_tn(a, b):
    return lax.dot_general(_mm(a), _mm(b), (((0,), (0,)), ((), ())), preferred_element_type=F32)


def _rowsum(x):
    return jnp.sum(x, axis=0, keepdims=True)


def _in_proj(x2, w, b):
    t, d = x2.shape
    n = w.shape[1]
    tm = min(ROW_TILE, t)
    nc = 768

    def body(x_ref, w_ref, b_ref, z_ref):
        xb = _mm(x_ref[...])
        for j in range(n // nc):
            cs = slice(j * nc, (j + 1) * nc)
            z_ref[:, cs] = jnp.dot(xb, w_ref[:, cs], preferred_element_type=F32) + b_ref[:, cs]

    return pl.pallas_call(
        body, name="in_proj", grid=(t // tm,),
        in_specs=[pl.BlockSpec((tm, d), lambda i: (i, 0)), pl.BlockSpec((d, n), lambda i: (0, 0)),
                  pl.BlockSpec((1, n), lambda i: (0, 0))],
        out_specs=pl.BlockSpec((tm, n), lambda i: (i, 0)),
        out_shape=jax.ShapeDtypeStruct((t, n), F32),
        compiler_params=_cparams(("parallel",)),
    )(x2, w, b)


def _in_proj_bwd_dx(dxr, dz, w, after):
    t, d = dxr.shape
    n = w.shape[1]
    tm = min(ROW_TILE, t)

    def body(dxr_ref, dz_ref, w_ref, after_ref, dx_ref):
        dx_ref[...] = dxr_ref[...] + _dot_nt(dz_ref[...], w_ref[...])

    return pl.pallas_call(
        body, name="in_proj_bwd_dx", grid=(t // tm,),
        in_specs=[pl.BlockSpec((tm, d), lambda i: (i, 0)), pl.BlockSpec((tm, n), lambda i: (i, 0)),
                  pl.BlockSpec((d, n), lambda i: (0, 0)), pl.BlockSpec((8, 128), lambda i: (0, 0))],
        out_specs=pl.BlockSpec((tm, d), lambda i: (i, 0)),
        out_shape=jax.ShapeDtypeStruct((t, d), F32),
        compiler_params=_cparams(("parallel",)),
    )(dxr, dz, w, after)


def _in_proj_bwd_dw(x2, dz):
    t, d = x2.shape
    n = dz.shape[1]
    tm = min(ROW_TILE, t)
    nc = 768

    def body(x_ref, dz_ref, dw_ref, done_ref):
        @pl.when(pl.program_id(0) == 0)
        def _():
            dw_ref[...] = jnp.zeros_like(dw_ref)
            done_ref[...] = jnp.zeros_like(done_ref)
        xb = _mm(x_ref[...])
        for j in range(n // nc):
            cs = slice(j * nc, (j + 1) * nc)
            dw_ref[:, cs] += _dot_tn(xb, dz_ref[:, cs])

    return pl.pallas_call(
        body, name="in_proj_bwd_dw", grid=(t // tm,),
        in_specs=[pl.BlockSpec((tm, d), lambda i: (i, 0)), pl.BlockSpec((tm, n), lambda i: (i, 0))],
        out_specs=[pl.BlockSpec((d, n), lambda i: (0, 0)), pl.BlockSpec((8, 128), lambda i: (0, 0))],
        out_shape=[jax.ShapeDtypeStruct((d, n), F32), jax.ShapeDtypeStruct((8, 128), F32)],
        compiler_params=_cparams(("arbitrary",)),
    )(x2, dz)


def _halo_specs(ts, s_len, cols, left=True):
    per = ts // HALO
    last = s_len // HALO - 1
    if left:
        return pl.BlockSpec((1, HALO, cols), lambda b, s: (b, jnp.maximum(s * per - 1, 0), 0))
    return pl.BlockSpec((1, HALO, cols), lambda b, s: (b, jnp.minimum((s + 1) * per, last), 0))


def _build_shifts(src_ref, sh_ref, rows):
    for r in range(1, 8):
        sh_ref[r - 1, 0:rows, :] = src_ref[r:r + rows, :]


def _shifted(src_ref, sh_ref, r, start, n):
    if r == 0:
        return src_ref[pl.ds(start, n), :]
    return sh_ref[r - 1, pl.ds(start, n), :]


def _tril_masks():
    ri = lax.broadcasted_iota(jnp.int32, (GBLK, GBLK), 0)
    ci = lax.broadcasted_iota(jnp.int32, (GBLK, GBLK), 1)
    return ri >= ci, ci >= ri


def _spatial(w_ref, keep, vb):
    lane = lax.broadcasted_iota(jnp.int32, (GBLK, GBLK), 1)
    outs = []
    for p in range(4):
        xs = vb[:, p * GBLK:(p + 1) * GBLK]
        r0 = jnp.dot(_mm(jnp.where(keep, w_ref[2 * p], 0.0)), xs, preferred_element_type=F32)
        r1 = jnp.dot(_mm(jnp.where(keep, w_ref[2 * p + 1], 0.0)), xs, preferred_element_type=F32)
        outs.append(jnp.where(lane < 64, r0, r1))
    return jnp.concatenate(outs, axis=1)


def _even_fwd(z3, cw, vec, ws, bsf):
    bsz, s_len, _ = z3.shape
    ts = min(SEQ_TILE, s_len)
    ext_rows = ts + HALO

    def body(z_ref, zl_ref, cw_ref, vec_ref, ws_ref, bsf_ref, y_ref, a1_ref, ext_ref, sh_ref):
        s = pl.program_id(1)
        hl = zl_ref[0]
        a0h = hl[:, 0:W_BR] * _sigmoid(hl[:, W_BR:2 * W_BR])
        ext_ref[0:HALO, :] = jnp.where(s > 0, a0h, 0.0)
        for a, b in _row_chunks(0, ts):
            ext_ref[HALO + a:HALO + b, :] = z_ref[0, a:b, 0:W_BR] * _sigmoid(z_ref[0, a:b, W_BR:2 * W_BR])
        ext_ref[ext_rows:ext_rows + 8, :] = jnp.zeros((8, W_BR), F32)
        _build_shifts(ext_ref, sh_ref, ext_rows)

        def conv_chunk(ci, carry):
            base = pl.multiple_of(ci * CONV_CHUNK, CONV_CHUNK)
            acc = jnp.zeros((CONV_CHUNK, W_BR), F32) + vec_ref[0:1, :]
            for k in range(KA):
                q, r = divmod(2 + k, 8)
                acc = acc + _shifted(ext_ref, sh_ref, r, base + 8 * q, CONV_CHUNK) * cw_ref[k:k + 1, :]
            a1_ref[0, pl.ds(base, CONV_CHUNK), :] = acc
            return carry

        lax.fori_loop(0, ts // CONV_CHUNK, conv_chunk, 0)
        keep, _ = _tril_masks()

        def block(bi, carry):
            rows = pl.ds(pl.multiple_of(bi * GBLK, GBLK), GBLK)
            xh, _ = _ln_stats(a1_ref[0, rows, :])
            a = _silu(xh * vec_ref[1:2, :] + vec_ref[2:3, :]) * _silu(z_ref[0, rows, 2 * W_BR:3 * W_BR])
            y_ref[0, rows, 0:W_BR] = a.astype(y_ref.dtype)
            ua = _gelu(z_ref[0, rows, 3 * W_BR:4 * W_BR])
            vh, _ = _ln_stats(_gelu(z_ref[0, rows, 4 * W_BR:5 * W_BR]))
            vl = vh * vec_ref[3:4, :] + vec_ref[4:5, :]
            sg = _spatial(ws_ref, keep, _mm(vl)) + bsf_ref[...]
            g = ua * sg * _silu(z_ref[0, rows, 5 * W_BR:6 * W_BR])
            y_ref[0, rows, W_BR:2 * W_BR] = g.astype(y_ref.dtype)
            return carry

        lax.fori_loop(0, ts // GBLK, block, 0)

    full = lambda shape: pl.BlockSpec(shape, lambda b, s: (0,) * len(shape))
    return pl.pallas_call(
        body, name="even_fwd", grid=(bsz, s_len // ts),
        in_specs=[pl.BlockSpec((1, ts, N_COLS), lambda b, s: (b, s, 0)), _halo_specs(ts, s_len, 2 * W_BR),
                  full((32, W_BR)), full((8, W_BR)), full((8, GBLK, GBLK)), full((GBLK, W_BR))],
        out_specs=[pl.BlockSpec((1, ts, 2 * W_BR), lambda b, s: (b, s, 0)), pl.BlockSpec((1, ts, W_BR), lambda b, s: (b, s, 0))],
        out_shape=[jax.ShapeDtypeStruct((bsz, s_len, 2 * W_BR), MM_DTYPE), jax.ShapeDtypeStruct((bsz, s_len, W_BR), F32)],
        scratch_shapes=[pltpu.VMEM((ext_rows + 8, W_BR), F32), pltpu.VMEM((7, ext_rows, W_BR), F32)],
        compiler_params=_cparams(("parallel", "parallel")),
    )(z3, z3, cw, vec, ws, bsf)


def _even_bwd(z3, dy3, a13, cw, vec, ws, wst, bsf):
    bsz, s_len, _ = z3.shape
    ts = min(SEQ_TILE, s_len)
    n_s = s_len // ts
    ext_rows = ts + 2 * HALO
    a_rows = ts + HALO

    def body(z_ref, zl_ref, zr_ref, dy_ref, dyr_ref, a1_ref, a1r_ref, cw_ref, vec_ref, ws_ref, wst_ref, bsf_ref,
             dz_ref, dbin_ref, dcw_ref, dvec_ref, dws_ref, dbsf_ref,
             ext_ref, sh_ref, ag_ref, dya_ref, d_ref, accw_ref):
        b = pl.program_id(0)
        s = pl.program_id(1)

        @pl.when((b == 0) & (s == 0))
        def _():
            dbin_ref[...] = jnp.zeros_like(dbin_ref)
            dcw_ref[...] = jnp.zeros_like(dcw_ref)
            dvec_ref[...] = jnp.zeros_like(dvec_ref)
            dws_ref[...] = jnp.zeros_like(dws_ref)
            dbsf_ref[...] = jnp.zeros_like(dbsf_ref)

        has_right = s < n_s - 1
        hl = zl_ref[0]
        hr = zr_ref[0]
        ext_ref[0:HALO, :] = jnp.where(s > 0, hl[:, 0:W_BR] * _sigmoid(hl[:, W_BR:2 * W_BR]), 0.0)
        for a, b in _row_chunks(0, ts):
            ext_ref[HALO + a:HALO + b, :] = z_ref[0, a:b, 0:W_BR] * _sigmoid(z_ref[0, a:b, W_BR:2 * W_BR])
            ag_ref[a:b, :] = z_ref[0, a:b, 2 * W_BR:3 * W_BR]
            dya_ref[a:b, :] = dy_ref[0, a:b, 0:W_BR]
        ext_ref[HALO + ts:ext_rows, :] = hr[:, 0:W_BR] * _sigmoid(hr[:, W_BR:2 * W_BR])
        ext_ref[ext_rows:ext_rows + 8, :] = jnp.zeros((8, W_BR), F32)
        ag_ref[ts:a_rows, :] = hr[:, 2 * W_BR:3 * W_BR]
        dya_ref[ts:a_rows, :] = dyr_ref[0][:, 0:W_BR]
        _build_shifts(ext_ref, sh_ref, ext_rows)

        def a_chunk(base, n, main):
            rows = pl.ds(base, n)
            xh, rs = _ln_stats(a1_ref[0, rows, :] if main else a1r_ref[0])
            ln = xh * vec_ref[1:2, :] + vec_ref[2:3, :]
            sl, dsl = _silu_grad(ln)
            sgt, dsgt = _silu_grad(ag_ref[rows, :])
            dya = dya_ref[rows, :]
            dln = dya * sgt * dsl
            da1 = _ln_bwd(dln * vec_ref[1:2, :], xh, rs)
            if main:
                d_ref[rows, :] = da1
                dag = dya * sl * dsgt
                dz_ref[0, rows, 2 * W_BR:3 * W_BR] = dag.astype(dz_ref.dtype)
                dbin_ref[0:1, 2 * W_BR:3 * W_BR] += _rowsum(dag)
                dvec_ref[0:1, :] += _rowsum(da1)
                dvec_ref[1:2, :] += _rowsum(dln * xh)
                dvec_ref[2:3, :] += _rowsum(dln)
            else:
                d_ref[rows, :] = jnp.where(has_right, da1, 0.0)

        def a_main(ci, carry):
            a_chunk(pl.multiple_of(ci * GBLK, GBLK), GBLK, True)
            return carry

        lax.fori_loop(0, ts // GBLK, a_main, 0)
        a_chunk(ts, HALO, False)
        d_ref[a_rows:a_rows + 8, :] = jnp.zeros((8, W_BR), F32)

        accw_ref[...] = jnp.zeros_like(accw_ref)

        def dw_chunk(ci, carry):
            base = pl.multiple_of(ci * DW_CHUNK, DW_CHUNK)
            d = d_ref[pl.ds(base, DW_CHUNK), :]
            for k in range(KA):
                q, r = divmod(2 + k, 8)
                prod = d * _shifted(ext_ref, sh_ref, r, base + 8 * q, DW_CHUNK)
                accw_ref[k] += jnp.sum(prod.reshape(DW_CHUNK // 8, 8, W_BR), axis=0)
            return carry

        lax.fori_loop(0, ts // DW_CHUNK, dw_chunk, 0)
        dcw_ref[...] += jnp.sum(accw_ref[...], axis=1)

        _build_shifts(d_ref, sh_ref, a_rows)

        def dx_chunk(ci, carry):
            base = pl.multiple_of(ci * CONV_CHUNK, CONV_CHUNK)
            rows = pl.ds(base, CONV_CHUNK)
            acc = jnp.zeros((CONV_CHUNK, W_BR), F32)
            for m in range(KA):
                q, r = divmod(m, 8)
                acc = acc + _shifted(d_ref, sh_ref, r, base + 8 * q, CONV_CHUNK) * cw_ref[KA - 1 - m:KA - m, :]
            aval = z_ref[0, rows, 0:W_BR]
            sg = _sigmoid(z_ref[0, rows, W_BR:2 * W_BR])
            dval = acc * sg
            dglu = acc * aval * sg * (1.0 - sg)
            dz_ref[0, rows, 0:W_BR] = dval.astype(dz_ref.dtype)
            dz_ref[0, rows, W_BR:2 * W_BR] = dglu.astype(dz_ref.dtype)
            dbin_ref[0:1, 0:W_BR] += _rowsum(dval)
            dbin_ref[0:1, W_BR:2 * W_BR] += _rowsum(dglu)
            return carry

        lax.fori_loop(0, ts // CONV_CHUNK, dx_chunk, 0)

        keep, keep_t = _tril_masks()
        lane = lax.broadcasted_iota(jnp.int32, (GBLK, GBLK), 1)

        def block(bi, carry):
            rows = pl.ds(pl.multiple_of(bi * GBLK, GBLK), GBLK)
            ua, dua = _gelu_grad(z_ref[0, rows, 3 * W_BR:4 * W_BR])
            va, dva = _gelu_grad(z_ref[0, rows, 4 * W_BR:5 * W_BR])
            sgt, dsgt = _silu_grad(z_ref[0, rows, 5 * W_BR:6 * W_BR])
            vh, rs = _ln_stats(va)
            vlb = _mm(vh * vec_ref[3:4, :] + vec_ref[4:5, :])
            sg = _spatial(ws_ref, keep, vlb) + bsf_ref[...]
            dyg = dy_ref[0, rows, W_BR:2 * W_BR]
            du = dyg * sg * sgt * dua
            dsg = dyg * ua * sgt
            dgg = dyg * ua * sg * dsgt
            dvl = _spatial(wst_ref, keep_t, _mm(dsg))
            for p in range(4):
                dsp = dsg[:, p * GBLK:(p + 1) * GBLK]
                vlp = vlb[:, p * GBLK:(p + 1) * GBLK]
                dws_ref[2 * p] += jnp.where(keep, _dot_nt(jnp.where(lane < 64, dsp, 0.0), vlp), 0.0)
                dws_ref[2 * p + 1] += jnp.where(keep, _dot_nt(jnp.where(lane >= 64, dsp, 0.0), vlp), 0.0)
            dbsf_ref[...] += dsg
            dvec_ref[3:4, :] += _rowsum(dvl * vh)
            dvec_ref[4:5, :] += _rowsum(dvl)
            dv = _ln_bwd(dvl * vec_ref[3:4, :], vh, rs) * dva
            dz_ref[0, rows, 3 * W_BR:4 * W_BR] = du.astype(dz_ref.dtype)
            dz_ref[0, rows, 4 * W_BR:5 * W_BR] = dv.astype(dz_ref.dtype)
            dz_ref[0, rows, 5 * W_BR:6 * W_BR] = dgg.astype(dz_ref.dtype)
            dbin_ref[0:1, 3 * W_BR:4 * W_BR] += _rowsum(du)
            dbin_ref[0:1, 4 * W_BR:5 * W_BR] += _rowsum(dv)
            dbin_ref[0:1, 5 * W_BR:6 * W_BR] += _rowsum(dgg)
            return carry

        lax.fori_loop(0, ts // GBLK, block, 0)

    full = lambda shape: pl.BlockSpec(shape, lambda b, s: (0,) * len(shape))
    acc_shapes = [(1, N_COLS), (32, W_BR), (8, W_BR), (8, GBLK, GBLK), (GBLK, W_BR)]
    return pl.pallas_call(
        body, name="even_bwd", grid=(bsz, n_s),
        in_specs=[pl.BlockSpec((1, ts, N_COLS), lambda b, s: (b, s, 0)),
                  _halo_specs(ts, s_len, N_COLS, True), _halo_specs(ts, s_len, N_COLS, False),
                  pl.BlockSpec((1, ts, 2 * W_BR), lambda b, s: (b, s, 0)), _halo_specs(ts, s_len, 2 * W_BR, False),
                  pl.BlockSpec((1, ts, W_BR), lambda b, s: (b, s, 0)), _halo_specs(ts, s_len, W_BR, False),
                  full((32, W_BR)), full((8, W_BR)), full((8, GBLK, GBLK)), full((8, GBLK, GBLK)), full((GBLK, W_BR))],
        out_specs=[pl.BlockSpec((1, ts, N_COLS), lambda b, s: (b, s, 0))] + [full(sh) for sh in acc_shapes],
        out_shape=[jax.ShapeDtypeStruct((bsz, s_len, N_COLS), MM_DTYPE)] + [jax.ShapeDtypeStruct(sh, F32) for sh in acc_shapes],
        scratch_shapes=[pltpu.VMEM((ext_rows + 8, W_BR), F32), pltpu.VMEM((7, ext_rows, W_BR), F32),
                        pltpu.VMEM((a_rows, W_BR), F32), pltpu.VMEM((a_rows, W_BR), F32),
                        pltpu.VMEM((a_rows + 8, W_BR), F32), pltpu.VMEM((32, 8, W_BR), F32)],
        compiler_params=_cparams(("arbitrary", "arbitrary")),
    )(z3, z3, z3, dy3, dy3, a13, a13, cw, vec, ws, wst, bsf)


def _row_chunks(lo, hi):
    return [(a, min(a + ROW_CHUNK, hi)) for a in range(lo, hi, ROW_CHUNK)]


def _pool_stages(e_refs, rows):
    e0, e1, e2, e3, e4 = e_refs
    for a, b in _row_chunks(8, rows):
        e1[a:b, :] = e0[a:b, :] + e0[a - 1:b - 1, :]
    for a, b in _row_chunks(16, rows):
        e2[a:b, GBLK:] = e1[a:b, GBLK:] + e1[a - 2:b - 2, GBLK:]
    for a, b in _row_chunks(24, rows):
        e3[a:b, 2 * GBLK:] = e2[a:b, 2 * GBLK:] + e2[a - 4:b - 4, 2 * GBLK:]
    for a, b in _row_chunks(32, rows):
        e4[a:b, 3 * GBLK:] = e3[a:b, 3 * GBLK:] + e3[a - 8:b - 8, 3 * GBLK:]


def _pool_counts(start, n):
    pos = (start + 1 + lax.broadcasted_iota(jnp.int32, (n, 1), 0)).astype(F32)
    return [jnp.minimum(pos, float(w)) for w in POOL_WINDOWS]


def _pooled_into(e_refs, pooled_ref, s, ts):
    for a, b in _row_chunks(0, ts):
        cnt = _pool_counts(s * ts + a, b - a)
        for g in range(4):
            cs = slice(g * GBLK, (g + 1) * GBLK)
            pooled_ref[a:b, cs] = e_refs[g + 1][HALO + a:HALO + b, cs] / cnt[g] - e_refs[0][HALO + a:HALO + b, cs]


def _odd_prologue(z_ref, hl, s, ts, e_refs, pooled_ref, dext_ref, ec_ref, vec_ref):
    e0 = e_refs[0]
    e0[0:HALO, :] = jnp.where(s > 0, hl[:, 0:W_BR], 0.0)
    dext_ref[0:HALO, :] = jnp.where(s > 0, hl[:, 2 * W_BR:3 * W_BR] * hl[:, 4 * W_BR:5 * W_BR], 0.0)
    for a, b in _row_chunks(0, ts):
        e0[HALO + a:HALO + b, :] = z_ref[0, a:b, 0:W_BR]
        dext_ref[HALO + a:HALO + b, :] = z_ref[0, a:b, 2 * W_BR:3 * W_BR] * z_ref[0, a:b, 4 * W_BR:5 * W_BR]
    _pool_stages(e_refs, ts + HALO)
    _pooled_into(e_refs, pooled_ref, s, ts)
    for a, b in _row_chunks(0, ts):
        ec_ref[a:b, :] = (vec_ref[1:2, :] * dext_ref[HALO - 2 + a:HALO - 2 + b, :] + vec_ref[2:3, :] * dext_ref[HALO - 1 + a:HALO - 1 + b, :]
                          + vec_ref[3:4, :] * dext_ref[HALO + a:HALO + b, :])


def _odd_fwd(z3, wp, vec):
    bsz, s_len, _ = z3.shape
    ts = min(SEQ_TILE, s_len)
    ext_rows = ts + HALO

    def body(z_ref, zl_ref, wp_ref, vec_ref, y_ref, e0, e1, e2, e3, e4, pooled_ref, dext_ref, ec_ref):
        s = pl.program_id(1)
        _odd_prologue(z_ref, zl_ref[0], s, ts, (e0, e1, e2, e3, e4), pooled_ref, dext_ref, ec_ref, vec_ref)

        def block(bi, carry):
            rows = pl.ds(pl.multiple_of(bi * GBLK, GBLK), GBLK)
            pb = _mm(pooled_ref[rows, :])
            cpre = jnp.concatenate([jnp.dot(pb[:, g * GBLK:(g + 1) * GBLK], wp_ref[g], preferred_element_type=F32)
                                    for g in range(4)], axis=1)
            c = cpre * vec_ref[0:1, :] * _silu(z_ref[0, rows, W_BR:2 * W_BR])
            d = z_ref[0, rows, 3 * W_BR:4 * W_BR] * ec_ref[rows, :] * _silu(z_ref[0, rows, 5 * W_BR:6 * W_BR])
            y_ref[0, rows, 0:W_BR] = c.astype(y_ref.dtype)
            y_ref[0, rows, W_BR:2 * W_BR] = d.astype(y_ref.dtype)
            return carry

        lax.fori_loop(0, ts // GBLK, block, 0)

    full = lambda shape: pl.BlockSpec(shape, lambda b, s: (0,) * len(shape))
    ebuf = pltpu.VMEM((ext_rows, W_BR), F32)
    return pl.pallas_call(
        body, name="odd_fwd", grid=(bsz, s_len // ts),
        in_specs=[pl.BlockSpec((1, ts, N_COLS), lambda b, s: (b, s, 0)), _halo_specs(ts, s_len, N_COLS),
                  full((4, GBLK, GBLK)), full((8, W_BR))],
        out_specs=pl.BlockSpec((1, ts, 2 * W_BR), lambda b, s: (b, s, 0)),
        out_shape=jax.ShapeDtypeStruct((bsz, s_len, 2 * W_BR), MM_DTYPE),
        scratch_shapes=[ebuf, ebuf, ebuf, ebuf, ebuf, pltpu.VMEM((ts, W_BR), F32), ebuf, pltpu.VMEM((ts, W_BR), F32)],
        compiler_params=_cparams(("parallel", "parallel")),
    )(z3, z3, wp, vec)


def _odd_bwd(z3, dy3, wp, vec):
    bsz, s_len, _ = z3.shape
    ts = min(SEQ_TILE, s_len)
    n_s = s_len // ts
    ext_rows = ts + HALO

    def body(z_ref, zl_ref, zr_ref, dy_ref, dyr_ref, wp_ref, vec_ref,
             dz_ref, dbin_ref, dwp_ref, dvec_ref,
             e0, e1, e2, e3, e4, pooled_ref, dext_ref, ec_ref, q_ref, dp_ref, de_ref, f1, f2, f3, f4):
        b = pl.program_id(0)
        s = pl.program_id(1)

        @pl.when((b == 0) & (s == 0))
        def _():
            dbin_ref[...] = jnp.zeros_like(dbin_ref)
            dwp_ref[...] = jnp.zeros_like(dwp_ref)
            dvec_ref[...] = jnp.zeros_like(dvec_ref)

        has_right = s < n_s - 1
        _odd_prologue(z_ref, zl_ref[0], s, ts, (e0, e1, e2, e3, e4), pooled_ref, dext_ref, ec_ref, vec_ref)

        def grads(zc_gate, zd_b, zd_gate, dyc, dyd, rows_out, n, start, valid):
            sgt = _silu(zc_gate)
            dcpre = dyc * vec_ref[0:1, :] * sgt
            db = _mm(dcpre)
            dpool = jnp.concatenate([_dot_nt(db[:, g * GBLK:(g + 1) * GBLK], wp_ref[g]) for g in range(4)], axis=1)
            cnt = _pool_counts(start, n)
            q = jnp.concatenate([dpool[:, g * GBLK:(g + 1) * GBLK] / cnt[g] for g in range(4)], axis=1)
            de = dyd * zd_b * _silu(zd_gate)
            if valid is not None:
                q = jnp.where(valid, q, 0.0)
                de = jnp.where(valid, de, 0.0)
            q_ref[rows_out, :] = q
            dp_ref[rows_out, :] = dpool
            de_ref[rows_out, :] = de
            return dcpre

        def block(bi, carry):
            base = pl.multiple_of(bi * GBLK, GBLK)
            rows = pl.ds(base, GBLK)
            cg = z_ref[0, rows, W_BR:2 * W_BR]
            dyc = dy_ref[0, rows, 0:W_BR]
            dyd = dy_ref[0, rows, W_BR:2 * W_BR]
            d_b = z_ref[0, rows, 3 * W_BR:4 * W_BR]
            d_gate = z_ref[0, rows, 5 * W_BR:6 * W_BR]
            dcpre = grads(cg, d_b, d_gate, dyc, dyd, rows, GBLK, s * ts + base, None)
            pb = _mm(pooled_ref[rows, :])
            dcb = _mm(dcpre)
            cpre = jnp.concatenate([jnp.dot(pb[:, g * GBLK:(g + 1) * GBLK], wp_ref[g], preferred_element_type=F32)
                                    for g in range(4)], axis=1)
            for g in range(4):
                cs = slice(g * GBLK, (g + 1) * GBLK)
                dwp_ref[g] += _dot_tn(pb[:, cs], dcb[:, cs])
            sgt, dsgt = _silu_grad(cg)
            dvec_ref[0:1, :] += _rowsum(dyc * cpre * sgt)
            dcg = dyc * cpre * vec_ref[0:1, :] * dsgt
            sdt, dsdt = _silu_grad(d_gate)
            ec = ec_ref[rows, :]
            ddb = dyd * ec * sdt
            ddg = dyd * d_b * ec * dsdt
            dz_ref[0, rows, W_BR:2 * W_BR] = dcg.astype(dz_ref.dtype)
            dz_ref[0, rows, 3 * W_BR:4 * W_BR] = ddb.astype(dz_ref.dtype)
            dz_ref[0, rows, 5 * W_BR:6 * W_BR] = ddg.astype(dz_ref.dtype)
            dbin_ref[0:1, W_BR:2 * W_BR] += _rowsum(dcg)
            dbin_ref[0:1, 3 * W_BR:4 * W_BR] += _rowsum(ddb)
            dbin_ref[0:1, 5 * W_BR:6 * W_BR] += _rowsum(ddg)
            return carry

        lax.fori_loop(0, ts // GBLK, block, 0)
        hr = zr_ref[0]
        dyr = dyr_ref[0]
        grads(hr[:, W_BR:2 * W_BR], hr[:, 3 * W_BR:4 * W_BR], hr[:, 5 * W_BR:6 * W_BR], dyr[:, 0:W_BR], dyr[:, W_BR:2 * W_BR],
              slice(ts, ext_rows), HALO, (s + 1) * ts, has_right)

        for a, b in _row_chunks(0, ts + 24):
            f1[a:b, :] = q_ref[a:b, :] + q_ref[a + 1:b + 1, :]
        for a, b in _row_chunks(0, ts + 16):
            f2[a:b, GBLK:] = f1[a:b, GBLK:] + f1[a + 2:b + 2, GBLK:]
        for a, b in _row_chunks(0, ts + 8):
            f3[a:b, 2 * GBLK:] = f2[a:b, 2 * GBLK:] + f2[a + 4:b + 4, 2 * GBLK:]
        for a, b in _row_chunks(0, ts):
            f4[a:b, 3 * GBLK:] = f3[a:b, 3 * GBLK:] + f3[a + 8:b + 8, 3 * GBLK:]

        for a, b in _row_chunks(0, ts):
            for g, f in enumerate((f1, f2, f3, f4)):
                cs = slice(g * GBLK, (g + 1) * GBLK)
                dvg = f[a:b, cs] - dp_ref[a:b, cs]
                dz_ref[0, a:b, cs] = dvg.astype(dz_ref.dtype)
                dbin_ref[0:1, cs] += _rowsum(dvg)
            de = de_ref[a:b, :]
            ddc = vec_ref[1:2, :] * de_ref[a + 2:b + 2, :] + vec_ref[2:3, :] * de_ref[a + 1:b + 1, :] + vec_ref[3:4, :] * de
            ddh = ddc * z_ref[0, a:b, 4 * W_BR:5 * W_BR]
            ddcc = ddc * z_ref[0, a:b, 2 * W_BR:3 * W_BR]
            dz_ref[0, a:b, 2 * W_BR:3 * W_BR] = ddh.astype(dz_ref.dtype)
            dz_ref[0, a:b, 4 * W_BR:5 * W_BR] = ddcc.astype(dz_ref.dtype)
            dbin_ref[0:1, 2 * W_BR:3 * W_BR] += _rowsum(ddh)
            dbin_ref[0:1, 4 * W_BR:5 * W_BR] += _rowsum(ddcc)
            for k in range(3):
                dvec_ref[1 + k:2 + k, :] += _rowsum(de * dext_ref[HALO - 2 + k + a:HALO - 2 + k + b, :])

    full = lambda shape: pl.BlockSpec(shape, lambda b, s: (0,) * len(shape))
    acc_shapes = [(1, N_COLS), (4, GBLK, GBLK), (8, W_BR)]
    ebuf = pltpu.VMEM((ext_rows, W_BR), F32)
    tbuf = pltpu.VMEM((ts, W_BR), F32)
    return pl.pallas_call(
        body, name="odd_bwd", grid=(bsz, n_s),
        in_specs=[pl.BlockSpec((1, ts, N_COLS), lambda b, s: (b, s, 0)),
                  _halo_specs(ts, s_len, N_COLS, True), _halo_specs(ts, s_len, N_COLS, False),
                  pl.BlockSpec((1, ts, 2 * W_BR), lambda b, s: (b, s, 0)), _halo_specs(ts, s_len, 2 * W_BR, False),
                  full((4, GBLK, GBLK)), full((8, W_BR))],
        out_specs=[pl.BlockSpec((1, ts, N_COLS), lambda b, s: (b, s, 0))] + [full(sh) for sh in acc_shapes],
        out_shape=[jax.ShapeDtypeStruct((bsz, s_len, N_COLS), MM_DTYPE)] + [jax.ShapeDtypeStruct(sh, F32) for sh in acc_shapes],
        scratch_shapes=[ebuf, ebuf, ebuf, ebuf, ebuf, tbuf, ebuf, tbuf, ebuf, ebuf, ebuf, ebuf, ebuf, ebuf, tbuf],
        compiler_params=_cparams(("arbitrary", "arbitrary")),
    )(z3, z3, z3, dy3, dy3, wp, vec)


def _post_fwd(y2, x2, p_layer, w_out, wg, wple, vec, tgt=None, nxt=None):
    p_all, layer = p_layer
    t, d = x2.shape
    tm = min(ROW_TILE, t)
    last = tgt is not None
    assert not (last and nxt is not None)
    extra = [tgt] if last else (list(nxt) if nxt is not None else [])
    n_in = 7 + len(extra)
    n_cols = nxt[0].shape[1] if nxt is not None else 0
    nc = 768

    def body(*refs):
        y_ref, x_ref, p_ref, wo_ref, wg_ref, wp_ref, vec_ref = refs[:7]
        xn_ref, r_ref, gate_ref = refs[n_in:n_in + 3]
        r = ALPHA * x_ref[...] + jnp.dot(y_ref[...], wo_ref[...], preferred_element_type=F32) + vec_ref[0:1, :]
        r_ref[...] = r
        xh, _ = _ln_stats(r)
        h = xh * vec_ref[1:2, :] + vec_ref[2:3, :]
        gate = _sigmoid(_dot(h, wg_ref[...]) + vec_ref[3:4, :])
        gate_ref[...] = gate
        xn = h + gate * _dot(p_ref[...], wp_ref[...])
        if last:
            sq_ref = refs[n_in + 3]

            @pl.when(pl.program_id(0) == 0)
            def _():
                sq_ref[...] = jnp.zeros_like(sq_ref)
            e = xn - refs[7][...]
            xn_ref[...] = e / float(d)
            sq_ref[...] += _rowsum(e * e)
        else:
            xn_ref[...] = xn
        if nxt is not None:
            win_ref, bin_ref, z_ref = refs[7], refs[8], refs[n_in + 3]
            xb = _mm(xn)
            for j in range(n_cols // nc):
                cs = slice(j * nc, (j + 1) * nc)
                z_ref[:, cs] = jnp.dot(xb, win_ref[:, cs], preferred_element_type=F32) + bin_ref[:, cs]

    row = lambda c: pl.BlockSpec((tm, c), lambda i: (i, 0))
    full = lambda shape: pl.BlockSpec(shape, lambda i: (0,) * len(shape), pipeline_mode=pl.Buffered(1))
    extra_in = [row(d)] if last else ([full((d, n_cols)), full((1, n_cols))] if nxt is not None else [])
    extra_out = [full((1, d))] if last else ([row(n_cols)] if nxt is not None else [])
    extra_shape = [jax.ShapeDtypeStruct((1, d), F32)] if last else ([jax.ShapeDtypeStruct((t, n_cols), F32)] if nxt is not None else [])
    return pl.pallas_call(
        body, name="post_fwd_loss" if last else "post_fwd", grid=(t // tm,),
        in_specs=[row(d), row(d), pl.BlockSpec((None, tm, D_PLE), lambda i: (layer, i, 0)),
                  full((d, d)), full((d, d)), full((D_PLE, d)), full((8, d))] + extra_in,
        out_specs=[row(d), row(d), row(d)] + extra_out,
        out_shape=[jax.ShapeDtypeStruct((t, d), F32)] * 3 + extra_shape,
        compiler_params=_cparams(("arbitrary",) if last else ("parallel",)),
    )(y2, x2, p_all, w_out, wg, wple, vec, *extra)


def _post_bwd(dxn, r2, gate2, p_layer, y2, w_out, wg, wple, vec):
    p_all, layer = p_layer
    t, d = r2.shape
    tm = min(BWD_ROW_TILE, t)

    def body(dxn_ref, r_ref, gate_ref, p_ref, y_ref, wo_ref, wg_ref, wp_ref, vec_ref,
             dxr_ref, dy_ref, dwo_ref, dwg_ref, dwp_ref, dvec_ref):
        @pl.when(pl.program_id(0) == 0)
        def _():
            dwo_ref[...] = jnp.zeros_like(dwo_ref)
            dwg_ref[...] = jnp.zeros_like(dwg_ref)
            dwp_ref[...] = jnp.zeros_like(dwp_ref)
            dvec_ref[...] = jnp.zeros_like(dvec_ref)

        dxn = dxn_ref[...]
        gate = gate_ref[...]
        xh, rs = _ln_stats(r_ref[...])
        hb = _mm(xh * vec_ref[1:2, :] + vec_ref[2:3, :])
        pb = _mm(p_ref[...])
        pe = jnp.dot(pb, wp_ref[...], preferred_element_type=F32)
        dpre = dxn * pe * gate * (1.0 - gate)
        dpb = _mm(dpre)
        dh = dxn + _dot_nt(dpb, wg_ref[...])
        dwg_ref[...] += _dot_tn(hb, dpb)
        dwp_ref[...] += _dot_tn(pb, dxn * gate)
        dr = _ln_bwd(dh * vec_ref[1:2, :], xh, rs)
        drb = _mm(dr)
        dxr_ref[...] = ALPHA * dr
        dy_ref[...] = _dot_nt(drb, wo_ref[...])
        dwo_ref[...] += _dot_tn(y_ref[...], drb)
        dvec_ref[0:1, :] += _rowsum(dr)
        dvec_ref[1:2, :] += _rowsum(dh * xh)
        dvec_ref[2:3, :] += _rowsum(dh)
        dvec_ref[3:4, :] += _rowsum(dpre)

    row = lambda c: pl.BlockSpec((tm, c), lambda i: (i, 0))
    full = lambda shape: pl.BlockSpec(shape, lambda i: (0,) * len(shape), pipeline_mode=pl.Buffered(1))
    acc_shapes = [(d, d), (d, d), (D_PLE, d), (8, d)]
    return pl.pallas_call(
        body, name="post_bwd", grid=(t // tm,),
        in_specs=[row(d), row(d), row(d), pl.BlockSpec((None, tm, D_PLE), lambda i: (layer, i, 0)), row(d),
                  full((d, d)), full((d, d)), full((D_PLE, d)), full((8, d))],
        out_specs=[row(d), row(d)] + [full(sh) for sh in acc_shapes],
        out_shape=[jax.ShapeDtypeStruct((t, d), F32)] * 2 + [jax.ShapeDtypeStruct(sh, F32) for sh in acc_shapes],
        compiler_params=_cparams(("arbitrary",)),
    )(dxn, r2, gate2, p_all, y2, w_out, wg, wple, vec)


def _place():
    x, y, c = lax.axis_index("x"), lax.axis_index("y"), lax.axis_index("c")
    chips = [(1 - x, y), (x, 1 - y), (1 - x, 1 - y)]
    return x, y, c, chips


def _shard_of(ref, ax, k, width, lo=None, ln=None):
    idx = [slice(None)] * 3
    if lo is not None:
        idx[0] = pl.ds(lo, ln)
    idx[ax] = pl.ds(k * width, width)
    return ref.at[tuple(idx)]


def _remote(src, dst, ssem, rsem, dev):
    return pltpu.make_async_remote_copy(src_ref=src, dst_ref=dst, send_sem=ssem, recv_sem=rsem, device_id=dev, device_id_type=MESH)


def _place_shard(w, ax, chip, dtype=MM_DTYPE, layer=None):
    l_dim, a_dim, b_dim = w.shape
    tr = min(256, a_dim)
    per = a_dim // tr
    first = 0
    if layer is not None:
        l_dim, first = 1, layer
    shape = [l_dim, a_dim, b_dim]
    shape[ax] *= 4
    if ax == 2:
        out_spec = pl.BlockSpec((1, tr, b_dim), lambda l, i, k: (l, i, k[0]))
    else:
        out_spec = pl.BlockSpec((1, tr, b_dim), lambda l, i, k: (l, k[0] * per + i, 0))

    def body(k_ref, w_ref, o_ref):
        o_ref[...] = w_ref[...].astype(o_ref.dtype)

    return pl.pallas_call(
        body, name="place_shard",
        grid_spec=pltpu.PrefetchScalarGridSpec(
            num_scalar_prefetch=1, grid=(l_dim, per),
            in_specs=[pl.BlockSpec((1, tr, b_dim), lambda l, i, k: (first + l, i, 0))], out_specs=out_spec),
        out_shape=jax.ShapeDtypeStruct(tuple(shape), dtype),
        compiler_params=_cparams(("parallel", "parallel")),
    )(chip, w)


def _shard_copies(src_refs, dst_refs, axes, sems):
    x, y, c, chips = _place()
    j = 2 * x + y
    cps = []
    for a, (s_ref, d_ref) in enumerate(zip(src_refs, dst_refs)):
        w = d_ref.shape[axes[a]] // 4
        for q, (qx, qy) in enumerate(chips):
            ssem, rsem = sems[3 * a + q]
            cps.append(_remote(_shard_of(s_ref, axes[a], j, w), _shard_of(d_ref, axes[a], j, w), ssem, rsem, (qx, qy, c)))
    return cps


def _chip_handshake():
    x, y, c, chips = _place()
    barrier = pltpu.get_barrier_semaphore()
    for qx, qy in chips:
        pl.semaphore_signal(barrier, inc=1, device_id=(qx, qy, c), device_id_type=MESH)
    pl.semaphore_wait(barrier, 3)


def _gather_async(name, collective_id, fulls, axes):
    n = len(fulls)
    refs = [jax.new_ref(f, memory_space=pltpu.MemorySpace.HBM) for f in fulls]
    moved = sum(2 * 3 * (f.size // 4) * jnp.dtype(f.dtype).itemsize for f in fulls)

    @pl.kernel(mesh=plsc.ScalarSubcoreMesh(axis_name="seq", num_cores=1), name=name,
               scratch_types=(pltpu.SemaphoreType.DMA,) * (6 * n),
               cost_estimate=pl.CostEstimate(flops=0, transcendentals=0, bytes_accessed=moved, remote_bytes_transferred=moved),
               compiler_params=pltpu.CompilerParams(collective_id=collective_id))
    def launch(*sems):
        _chip_handshake()
        cps = _shard_copies(refs, refs, axes, [(sems[2 * k], sems[2 * k + 1]) for k in range(3 * n)])
        for cp in cps:
            cp.start()
        for cp in cps:
            cp.wait()

    launch()
    return refs


def _gather_first_async(collective_id, w_full, sv_full):
    w_ref = jax.new_ref(w_full, memory_space=pltpu.MemorySpace.HBM)
    sv_ref = jax.new_ref(sv_full, memory_space=pltpu.MemorySpace.HBM)
    a_half = w_full.shape[1] // 2
    width = w_full.shape[2] // 4
    moved = 2 * 3 * (w_full.size // 4 + sv_full.size // 4 * 4)

    @pl.kernel(mesh=plsc.ScalarSubcoreMesh(axis_name="seq", num_cores=1), name="gather_first",
               scratch_types=(pltpu.SemaphoreType.DMA,) * 18,
               cost_estimate=pl.CostEstimate(flops=0, transcendentals=0, bytes_accessed=moved, remote_bytes_transferred=moved),
               compiler_params=pltpu.CompilerParams(collective_id=collective_id))
    def launch(*sems):
        x, y, c, chips = _place()
        barrier = pltpu.get_barrier_semaphore()
        for dev in [(qx, qy, c) for qx, qy in chips] + [(x, y, 1 - c)]:
            pl.semaphore_signal(barrier, inc=1, device_id=dev, device_id_type=MESH)
        pl.semaphore_wait(barrier, 4)
        j = 2 * x + y

        def piece(k, half):
            return w_ref.at[:, pl.ds(half * a_half, a_half), pl.ds(k * width, width)]

        small = _shard_copies([sv_ref], [sv_ref], [1], [(sems[2 * q], sems[2 * q + 1]) for q in range(3)])
        sends = [_remote(piece(j, c), piece(j, c), sems[6 + 2 * q], sems[7 + 2 * q], (qx, qy, c)) for q, (qx, qy) in enumerate(chips)]
        for cp in small + sends:
            cp.start()
        passed = []
        for q, (qx, qy) in enumerate(chips):
            got = piece(2 * qx + qy, c)
            _remote(got, got, sems[6 + 2 * q], sems[7 + 2 * q], (x, y, c)).wait_recv()
            cp = _remote(got, got, sems[12 + 2 * q], sems[13 + 2 * q], (x, y, 1 - c))
            cp.start()
            passed.append(cp)
        for q, (qx, qy) in enumerate(chips):
            other = piece(2 * qx + qy, 1 - c)
            _remote(other, other, sems[12 + 2 * q], sems[13 + 2 * q], (x, y, c)).wait_recv()
        for cp in small:
            cp.wait()
        for cp in sends + passed:
            cp.wait_send()

    launch()
    return w_ref, sv_ref


def _scatter_async(name, collective_id, grads, axes, after):
    n = len(grads)
    extra = len(after)
    outs = []
    for g, ax in zip(grads, axes):
        sh = list(g.shape)
        sh[ax] //= 4
        outs.append(jax.ShapeDtypeStruct((3,) + tuple(sh), g.dtype))

    def body(*refs):
        srcs, lands, sems = refs[:n], refs[n + extra:2 * n + extra], refs[2 * n + extra:]
        _chip_handshake()
        x, y, c, chips = _place()
        cps = []
        for a in range(n):
            w = srcs[a].shape[axes[a]] // 4
            for q, (qx, qy) in enumerate(chips):
                k = 3 * a + q
                cps.append(_remote(_shard_of(srcs[a], axes[a], 2 * qx + qy, w), lands[a].at[q], sems[2 * k], sems[2 * k + 1],
                                   (qx, qy, c)))
        for cp in cps:
            cp.start()
        for cp in cps:
            cp.wait()

    moved = sum(2 * 3 * math.prod(o.shape[1:]) * jnp.dtype(o.dtype).itemsize for o in outs)
    return pl.kernel(body, out_type=outs, mesh=plsc.ScalarSubcoreMesh(axis_name="seq", num_cores=1), name=name,
                     scratch_types=(pltpu.SemaphoreType.DMA,) * (6 * n),
                     cost_estimate=pl.CostEstimate(flops=0, transcendentals=0, bytes_accessed=moved, remote_bytes_transferred=moved),
                     compiler_params=pltpu.CompilerParams(collective_id=collective_id))(*grads, *after)


def _pair_swap_async(name, collective_id, sums):
    n = len(sums)

    def body(*refs):
        g_refs, o_refs, sems = refs[:n], refs[n:2 * n], refs[2 * n:]
        x, y, c, _ = _place()
        barrier = pltpu.get_barrier_semaphore()
        pl.semaphore_signal(barrier, inc=1, device_id=(x, y, 1 - c), device_id_type=MESH)
        pl.semaphore_wait(barrier, 1)
        cps = [_remote(g_refs[a], o_refs[a], sems[2 * a], sems[2 * a + 1], (x, y, 1 - c)) for a in range(n)]
        for cp in cps:
            cp.start()
        for cp in cps:
            cp.wait()

    moved = sum(2 * g.size * jnp.dtype(g.dtype).itemsize for g in sums)
    return pl.kernel(body, out_type=[jax.ShapeDtypeStruct(g.shape, g.dtype) for g in sums],
                     mesh=plsc.ScalarSubcoreMesh(axis_name="seq", num_cores=1), name=name,
                     scratch_types=(pltpu.SemaphoreType.DMA,) * (2 * n),
                     cost_estimate=pl.CostEstimate(flops=0, transcendentals=0, bytes_accessed=moved, remote_bytes_transferred=moved),
                     compiler_params=pltpu.CompilerParams(collective_id=collective_id))(*sums)


def _small_allreduce(small):
    r = small.shape[0]
    rh, rq = r // 2, r // 8
    flips = [(fx, fy, fc) for fx in (0, 1) for fy in (0, 1) for fc in (0, 1)][1:]

    def body(g_ref, out_ref, pair_ref, chip_ref, s1_ref, ssem, rsem):
        x, y, c, chips = _place()
        cp = _remote(g_ref.at[pl.ds((1 - c) * rh, rh)], pair_ref, ssem.at[0], rsem.at[0], (x, y, 1 - c))
        cp.start()
        cp.wait()
        s1_ref[...] = g_ref[pl.ds(pl.multiple_of(c * rh, 8), rh), :] + pair_ref[...]
        cps = [_remote(s1_ref.at[pl.ds((2 * qx + qy) * rq, rq)], chip_ref.at[q], ssem.at[1 + q], rsem.at[1 + q], (qx, qy, c))
               for q, (qx, qy) in enumerate(chips)]
        for cp in cps:
            cp.start()
        for cp in cps:
            cp.wait()
        mine = out_ref.at[pl.ds(pl.multiple_of((4 * c + 2 * x + y) * rq, 8), rq)]
        mine[...] = ((s1_ref[pl.ds(pl.multiple_of((2 * x + y) * rq, 8), rq), :] + chip_ref[0]) + chip_ref[1]) + chip_ref[2]
        cps = [_remote(mine, mine, ssem.at[4 + f], rsem.at[4 + f], (x ^ fx, y ^ fy, c ^ fc)) for f, (fx, fy, fc) in enumerate(flips)]
        for cp in cps:
            cp.start()
        for cp in cps:
            cp.wait()

    vm = pl.BlockSpec(memory_space=pltpu.VMEM)
    return pl.pallas_call(
        body, name="small_allreduce", in_specs=[vm], out_specs=vm, out_shape=jax.ShapeDtypeStruct(small.shape, F32),
        scratch_shapes=[pltpu.VMEM((rh, 128), F32), pltpu.VMEM((3, rq, 128), F32), pltpu.VMEM((rh, 128), F32),
                        pltpu.SemaphoreType.DMA((11,)), pltpu.SemaphoreType.DMA((11,))],
        compiler_params=pltpu.CompilerParams(has_side_effects=True, vmem_limit_bytes=VMEM_LIMIT),
    )(small)


def _to_wire(g):
    _, a_dim, b_dim = g.shape
    tr = min(256, a_dim)

    def body(g_ref, o_ref, done_ref):
        o_ref[...] = g_ref[...].astype(o_ref.dtype)
        done_ref[...] = jnp.zeros_like(done_ref)

    blk = pl.BlockSpec((1, tr, b_dim), lambda i: (0, i, 0))
    return pl.pallas_call(
        body, name="to_wire", grid=(a_dim // tr,), in_specs=[blk], out_specs=[blk, pl.BlockSpec((8, 128), lambda i: (0, 0))],
        out_shape=[jax.ShapeDtypeStruct(g.shape, WIRE_DTYPE), jax.ShapeDtypeStruct((8, 128), F32)],
        compiler_params=_cparams(("arbitrary",)),
    )(g)


def _chip_sum(owns, gots, ax, chip, after=None):
    n_layers = len(owns)
    tail = [] if after is None else [after]
    _, _, a_dim, b_dim = gots[0].shape
    tr = min(256, a_dim)
    per = a_dim // tr

    def own_spec(layer):
        if ax == 2:
            return pl.BlockSpec((1, tr, b_dim), lambda l, i, k: (0, jnp.where(l == layer, i, 0), k[0]))
        return pl.BlockSpec((1, tr, b_dim), lambda l, i, k: (0, k[0] * per + jnp.where(l == layer, i, 0), 0))

    def got_spec(layer):
        return pl.BlockSpec((3, 1, tr, b_dim), lambda l, i, k: (0, 0, jnp.where(l == layer, i, 0), 0))

    def body(k_ref, *refs):
        s_ref = refs[-1]
        for layer in range(n_layers):
            @pl.when(pl.program_id(0) == layer)
            def _(own_ref=refs[layer], got_ref=refs[n_layers + layer]):
                s_ref[...] = ((own_ref[...].astype(F32) + got_ref[0].astype(F32)) + got_ref[1].astype(F32)) + got_ref[2].astype(F32)

    return pl.pallas_call(
        body, name="chip_sum",
        grid_spec=pltpu.PrefetchScalarGridSpec(
            num_scalar_prefetch=1, grid=(n_layers, per),
            in_specs=[own_spec(l) for l in range(n_layers)] + [got_spec(l) for l in range(n_layers)]
            + [pl.BlockSpec((8, 128), lambda l, i, k: (0, 0))] * len(tail),
            out_specs=pl.BlockSpec((1, tr, b_dim), lambda l, i, k: (l, i, 0))),
        out_shape=jax.ShapeDtypeStruct((n_layers, a_dim, b_dim), F32),
        compiler_params=_cparams(("arbitrary", "arbitrary")),
    )(chip, *owns, *gots, *tail)


def _adam_math(w, g, m, v):
    m = ADAM_B1 * m + (1.0 - ADAM_B1) * g
    v = ADAM_B2 * v + (1.0 - ADAM_B2) * (g * g)
    m_hat = m / (1.0 - ADAM_B1 ** ADAM_STEP)
    v_hat = v / (1.0 - ADAM_B2 ** ADAM_STEP)
    return -ADAM_LR * (m_hat / (jnp.sqrt(v_hat) + ADAM_EPS) + ADAM_WD * w), m, v


def _adamw_big(w, g_mine, g_other, m, v):
    l_dim, a_dim, b_dim = w.shape
    tr = min(256, a_dim)

    def body(w_ref, g1_ref, g2_ref, m_ref, v_ref, g_ref, d_ref, nm_ref, nv_ref):
        g = g1_ref[...] + g2_ref[...]
        g_ref[...] = g
        d_ref[...], nm_ref[...], nv_ref[...] = _adam_math(w_ref[...], g, m_ref[...], v_ref[...])

    blk = pl.BlockSpec((1, tr, b_dim), lambda l, i: (l, i, 0))
    return pl.pallas_call(
        body, name="adamw_big", grid=(l_dim, a_dim // tr), in_specs=[blk] * 5, out_specs=[blk] * 4,
        out_shape=[jax.ShapeDtypeStruct(w.shape, F32)] * 4,
        compiler_params=_cparams(("parallel", "parallel")),
    )(w, g_mine, g_other, m, v)


def _adamw_small(ws, gs, ms, vs):
    n = len(ws)

    def body(*refs):
        for i in range(n):
            w_ref, g_ref, m_ref, v_ref = refs[i], refs[n + i], refs[2 * n + i], refs[3 * n + i]
            d_ref, nm_ref, nv_ref = refs[4 * n + i], refs[5 * n + i], refs[6 * n + i]
            d_ref[...], nm_ref[...], nv_ref[...] = _adam_math(w_ref[...], g_ref[...], m_ref[...], v_ref[...])

    shapes = [jax.ShapeDtypeStruct(w.shape, F32) for w in ws]
    outs = pl.pallas_call(body, name="adamw_small", out_shape=shapes * 3,
                          compiler_params=_cparams())(*ws, *gs, *ms, *vs)
    return outs[:n], outs[n:2 * n], outs[2 * n:]


def _pack(arrs, row_mult):
    parts = []
    for a in arrs:
        flat = a.reshape(-1)
        pad = (-flat.shape[0]) % 1024
        parts.append(jnp.pad(flat, (0, pad)).reshape(-1, 128))
    buf = jnp.concatenate(parts, axis=0)
    pad = (-buf.shape[0]) % row_mult
    return jnp.pad(buf, ((0, pad), (0, 0)))


def _unpack(buf, shapes):
    out, row = [], 0
    for sh in shapes:
        n = math.prod(sh)
        rows = -(-n // 1024) * 8
        out.append(buf[row:row + rows].reshape(-1)[:n].reshape(sh))
        row += rows
    return out


_NAMES = ['w_in_e', 'b_in_e', 'conv_a_w', 'conv_a_b', 'ln_a_g', 'ln_a_b', 'ln_v_g', 'ln_v_b', 'w_s', 'b_s', 'w_out_e', 'b_out_e',
          'w_in_o', 'b_in_o', 'w_pool', 'pool_scale', 'conv_d_w', 'w_out_o', 'b_out_o', 'ln_g', 'ln_b', 'w_ple', 'w_ple_gate',
          'b_ple_gate']
_BIG = ['w_in_e', 'w_out_e', 'w_in_o', 'w_out_o', 'w_ple', 'w_ple_gate']
_SMALL_SHARDED = ['conv_a_w', 'b_in_o', 'pool_scale', 'conv_d_w', 'b_out_o']


def kernel(x, p, w_in_e, b_in_e, conv_a_w, conv_a_b, ln_a_g, ln_a_b, ln_v_g, ln_v_b, w_s, b_s, w_out_e, b_out_e, w_in_o, b_in_o, w_pool, pool_scale, conv_d_w, w_out_o, b_out_o, ln_g, ln_b, w_ple, w_ple_gate, b_ple_gate, loss_target, m_w_in_e, m_b_in_e, m_conv_a_w, m_conv_a_b, m_ln_a_g, m_ln_a_b, m_ln_v_g, m_ln_v_b, m_w_s, m_b_s, m_w_out_e, m_b_out_e, m_w_in_o, m_b_in_o, m_w_pool, m_pool_scale, m_conv_d_w, m_w_out_o, m_b_out_o, m_ln_g, m_ln_b, m_w_ple, m_w_ple_gate, m_b_ple_gate, v_w_in_e, v_b_in_e, v_conv_a_w, v_conv_a_b, v_ln_a_g, v_ln_a_b, v_ln_v_g, v_ln_v_b, v_w_s, v_b_s, v_w_out_e, v_b_out_e, v_w_in_o, v_b_in_o, v_w_pool, v_pool_scale, v_conv_d_w, v_w_out_o, v_b_out_o, v_ln_g, v_ln_b, v_w_ple, v_w_ple_gate, v_b_ple_gate):
    args = locals()
    wts = {n: args[n] for n in _NAMES}
    mom = {n: args["m_" + n] for n in _NAMES}
    var = {n: args["v_" + n] for n in _NAMES}
    bsz, s_len, d = x.shape
    t = bsz * s_len
    cx, cy, cc = lax.axis_index("x"), lax.axis_index("y"), lax.axis_index("c")
    chip = (2 * cx + cy).astype(jnp.int32).reshape(1)

    def placed(name, ax, layer):
        return _place_shard(wts[name], ax, chip, layer=layer)

    sv = _pack([wts[n] for n in _SMALL_SHARDED], 8)
    first_refs = _gather_first_async(DEPTH, placed('w_in_e', 2, 0), _place_shard(sv[None], 1, chip, F32))
    layer_refs = []
    for i in range(DEPTH):
        sfx = '_e' if i % 2 == 0 else '_o'
        nxt_sfx = '_o' if i % 2 == 0 else '_e'
        items = ([('w_out' + sfx, 1, i // 2), ('w_ple_gate', 1, i), ('w_ple', 2, i)]
                 + ([('w_in' + nxt_sfx, 2, (i + 1) // 2)] if i + 1 < DEPTH else []))
        layer_refs.append(_gather_async("gather_layer%d" % i, i, [placed(*it) for it in items], [it[1] for it in items]))
    fw = {}
    w_in_first = first_refs[0][...]
    sv_all = first_refs[1][...].reshape((4,) + sv.shape)
    small_parts = [_unpack(sv_all[k], [wts[n].shape for n in _SMALL_SHARDED]) for k in range(4)]
    for i, n in enumerate(_SMALL_SHARDED):
        fw[n] = jnp.concatenate([small_parts[k][i] for k in range(4)], axis=-1)
    for n in _NAMES:
        fw.setdefault(n, wts[n])

    def row8(rows, width):
        rows = [r.reshape(1, width) for r in rows]
        return jnp.concatenate(rows + [jnp.zeros((8 - len(rows), width), F32)], axis=0)

    x2 = x.reshape(t, d)
    saved = []
    for i in range(DEPTH):
        j = i // 2
        even = i % 2 == 0
        b_in, b_out = (fw['b_in_e'], fw['b_out_e']) if even else (fw['b_in_o'], fw['b_out_o'])
        if i == 0:
            w_in = w_in_first[0]
            z = _in_proj(x2, w_in, b_in[j].reshape(1, N_COLS))
        z3 = z.reshape(bsz, s_len, N_COLS)
        if even:
            cw = jnp.concatenate([fw['conv_a_w'][j], jnp.zeros((1, W_BR), F32)], axis=0)
            mvec = row8([fw['conv_a_b'][j], fw['ln_a_g'][j], fw['ln_a_b'][j], fw['ln_v_g'][j], fw['ln_v_b'][j]], W_BR)
            bsf = jnp.repeat(fw['b_s'][j].T, W_BR // 8, axis=1)
            y3, a13 = _even_fwd(z3, cw, mvec, fw['w_s'][j], bsf)
            mix = (a13, cw, mvec, fw['w_s'][j], jnp.swapaxes(fw['w_s'][j], 1, 2), bsf)
        else:
            mvec = row8([fw['pool_scale'][j]] + [fw['conv_d_w'][j][k] for k in range(3)], W_BR)
            mix = (fw['w_pool'][j].astype(MM_DTYPE), mvec)
            y3 = _odd_fwd(z3, mix[0], mvec)
        pvec = row8([b_out[j], fw['ln_g'][i], fw['ln_b'][i], fw['b_ple_gate'][i]], d)
        post_w = (layer_refs[i][0][...][0], layer_refs[i][1][...][0], layer_refs[i][2][...][0], pvec)
        p2 = (p.reshape(DEPTH, t, D_PLE), i)
        y2 = y3.reshape(t, 2 * W_BR)
        saved.append((x2, z3, y2, None, None, p2, w_in, mix, post_w))
        if i < DEPTH - 1:
            w_in = layer_refs[i][3][...][0]
            b_next = (fw['b_in_o'] if even else fw['b_in_e'])[(i + 1) // 2].reshape(1, N_COLS)
            x2, r2, gate2, z = _post_fwd(y2, x2, p2, *post_w, nxt=(w_in, b_next))
        else:
            dx, r2, gate2, sq = _post_fwd(y2, x2, p2, *post_w, tgt=loss_target.reshape(t, d))
        saved[-1] = saved[-1][:3] + (r2, gate2) + saved[-1][5:]

    gr = {n: [None] * wts[n].shape[0] for n in _NAMES}
    prev_lands = []
    for i in reversed(range(DEPTH)):
        j = i // 2
        even = i % 2 == 0
        x_in, z3, y2, r2, gate2, p2, w_in, mix, post_w = saved[i]
        dxr, dy, dwo, dwg, dwp, dpv = _post_bwd(dx, r2, gate2, p2, y2, *post_w)
        dy3 = dy.reshape(bsz, s_len, 2 * W_BR)
        sfx = '_e' if even else '_o'
        gr['w_out' + sfx][j], gr['b_out' + sfx][j] = dwo, dpv[0]
        gr['w_ple_gate'][i], gr['w_ple'][i] = dwg, dwp
        gr['ln_g'][i], gr['ln_b'][i], gr['b_ple_gate'][i] = dpv[1], dpv[2], dpv[3]
        if even:
            dz3, dbin, dcw, dmv, dws, dbsf = _even_bwd(z3, dy3, *mix)
            gr['conv_a_w'][j], gr['conv_a_b'][j] = dcw[:KA], dmv[0]
            gr['ln_a_g'][j], gr['ln_a_b'][j], gr['ln_v_g'][j], gr['ln_v_b'][j] = dmv[1], dmv[2], dmv[3], dmv[4]
            gr['w_s'][j] = dws
            gr['b_s'][j] = jnp.sum(dbsf.reshape(GBLK, 8, W_BR // 8), axis=2).T
        else:
            dz3, dbin, dwpool, dmv = _odd_bwd(z3, dy3, *mix)
            gr['w_pool'][j], gr['pool_scale'][j], gr['conv_d_w'][j] = dwpool, dmv[0], dmv[1:4]
        gr['b_in' + sfx][j] = dbin[0]
        dz2 = dz3.reshape(t, N_COLS)
        dwi, dw_done = _in_proj_bwd_dw(x_in, dz2)
        post_items = [('w_out' + sfx, j, dwo[None], 1), ('w_ple_gate', i, dwg[None], 1), ('w_ple', i, dwp[None], 2)]
        in_item = ('w_in' + sfx, j, dwi[None], 2)
        batches = [post_items + [in_item]] if i else [post_items, [in_item]]
        for bi, items in enumerate(batches):
            sent = [it[2] for it in items]
            if i == 0:
                cast = [_to_wire(g) for g in sent]
                sent = [w for w, _ in cast]
                if bi == 1:
                    dw_done = cast[0][1]
            lands = _scatter_async("scatter_layer%d_%d" % (i, bi), DEPTH + 1 + 2 * i + bi, sent, [it[3] for it in items],
                                   prev_lands[:1] + ([dbin] if (i == 0 and bi == 0) else []))
            prev_lands = list(lands)
            for it, land in zip(items, lands):
                gr[it[0]][it[1]] = (it[2], land, it[3])
        dx = _in_proj_bwd_dx(dxr, dz2, w_in, dw_done)
    grad_x = dx.reshape(bsz, s_len, d)

    small_names = [n for n in _NAMES if n not in _BIG]
    g_small_full = [jnp.stack(gr[n]) for n in small_names] + [sq]
    small_all = _small_allreduce(_pack(g_small_full, 64))
    *g_small, sq_all = _unpack(small_all, [g.shape for g in g_small_full])
    loss = 0.5 * jnp.sum(sq_all) / d
    big_order = _BIG[1:] + _BIG[:1]
    sums = [_chip_sum([g[0] for g in gr[n]], [g[1] for g in gr[n]], gr[n][0][2], chip, dx if n == big_order[-1] else None)
            for n in big_order]
    others = (list(_pair_swap_async("pair_swap_a", 3 * DEPTH + 1, sums[:-1]))
              + list(_pair_swap_async("pair_swap_b", 3 * DEPTH + 2, sums[-1:])))
    grads = {}
    for n, g in zip(small_names, g_small):
        if n in _SMALL_SHARDED:
            w = wts[n].shape[-1]
            g = lax.dynamic_slice_in_dim(g, (2 * cx + cy) * w, w, axis=g.ndim - 1)
        grads[n] = g

    delta, new_m, new_v = {}, {}, {}
    ds, ms, vs = _adamw_small([wts[n] for n in small_names], [grads[n] for n in small_names],
                              [mom[n] for n in small_names], [var[n] for n in small_names])
    for n, a, b, c_ in zip(small_names, ds, ms, vs):
        delta[n], new_m[n], new_v[n] = a, b, c_
    for n, mine, other in zip(big_order, sums, others):
        grads[n], delta[n], new_m[n], new_v[n] = _adamw_big(wts[n], mine, other, mom[n], var[n])

    return (loss, grad_x, *[grads[n] for n in _NAMES], *[delta[n] for n in _NAMES],
            *[new_m[n] for n in _NAMES], *[new_v[n] for n in _NAMES])
```

```python
import functools
import math

import jax
import jax.numpy as jnp
from jax import lax
from jax.experimental import pallas as pl
from jax.experimental.pallas import tpu as pltpu
from jax.experimental.pallas import tpu_sc as plsc

F32 = jnp.float32
MM_DTYPE = jnp.bfloat16
WIRE_DTYPE = jnp.bfloat16
SEQ_TILE = 512
ROW_TILE = 512
BWD_ROW_TILE = 512
HALO = 32
CONV_CHUNK = 64
DW_CHUNK = 32
ROW_CHUNK = 64
GBLK = 128
VMEM_LIMIT = 56 * 1024 * 1024

D_MODEL = 1024
W_BR = 512
N_COLS = 6 * W_BR
D_PLE = 256
KA = 31
DEPTH = 4
POOL_WINDOWS = (2, 4, 8, 16)
ALPHA = (2.0 * DEPTH) ** 0.25
LN_EPS = 1e-5
GELU_C = math.sqrt(2.0 / math.pi)

ADAM_LR, ADAM_B1, ADAM_B2, ADAM_EPS, ADAM_WD, ADAM_STEP = 0.001, 0.9, 0.999, 1e-08, 0.01, 10

MESH = pl.DeviceIdType.MESH
ANY = pl.BlockSpec(memory_space=pl.ANY)


def _cparams(sem=None):
    return pltpu.CompilerParams(dimension_semantics=sem, vmem_limit_bytes=VMEM_LIMIT)


def _sigmoid(x):
    return 1.0 / (1.0 + jnp.exp(-x))


def _silu(x):
    return x * _sigmoid(x)


def _silu_grad(x):
    s = _sigmoid(x)
    return x * s, s * (1.0 + x * (1.0 - s))


def _gelu(x):
    return 0.5 * x * (1.0 + jnp.tanh(GELU_C * (x + 0.044715 * (x * x * x))))


def _gelu_grad(x):
    x2 = x * x
    th = jnp.tanh(GELU_C * (x + 0.044715 * (x * x2)))
    return 0.5 * x * (1.0 + th), 0.5 * (1.0 + th) + 0.5 * x * (1.0 - th * th) * (GELU_C * (1.0 + 3.0 * 0.044715 * x2))


def _ln_stats(x):
    mu = jnp.mean(x, axis=-1, keepdims=True)
    d = x - mu
    var = jnp.mean(d * d, axis=-1, keepdims=True)
    rs = lax.rsqrt(var + LN_EPS)
    return d * rs, rs


def _ln_bwd(dxh, xh, rs):
    return rs * (dxh - jnp.mean(dxh, axis=-1, keepdims=True) - xh * jnp.mean(dxh * xh, axis=-1, keepdims=True))


def _mm(a):
    return a.astype(MM_DTYPE)


def _dot(a, b):
    return jnp.dot(_mm(a), _mm(b), preferred_element_type=F32)


def _dot_nt(a, b):
    return lax.dot_general(_mm(a), _mm(b), (((1,), (1,)), ((), ())), preferred_element_type=F32)


def _dot_tn(a, b):
    return lax.dot_general(_mm(a), _mm(b), (((0,), (0,)), ((), ())), preferred_element_type=F32)


def _rowsum(x):
    return jnp.sum(x, axis=0, keepdims=True)


def _in_proj(x2, w, b):
    t, d = x2.shape
    n = w.shape[1]
    tm = min(ROW_TILE, t)
    nc = 768

    def body(x_ref, w_ref, b_ref, z_ref):
        xb = _mm(x_ref[...])
        for j in range(n // nc):
            cs = slice(j * nc, (j + 1) * nc)
            z_ref[:, cs] = jnp.dot(xb, w_ref[:, cs], preferred_element_type=F32) + b_ref[:, cs]

    return pl.pallas_call(
        body, name="in_proj", grid=(t // tm,),
        in_specs=[pl.BlockSpec((tm, d), lambda i: (i, 0)), pl.BlockSpec((d, n), lambda i: (0, 0)),
                  pl.BlockSpec((1, n), lambda i: (0, 0))],
        out_specs=pl.BlockSpec((tm, n), lambda i: (i, 0)),
        out_shape=jax.ShapeDtypeStruct((t, n), F32),
        compiler_params=_cparams(("parallel",)),
    )(x2, w, b)


def _in_proj_bwd_dx(dxr, dz, w, after):
    t, d = dxr.shape
    n = w.shape[1]
    tm = min(ROW_TILE, t)

    def body(dxr_ref, dz_ref, w_ref, after_ref, dx_ref):
        dx_ref[...] = dxr_ref[...] + _dot_nt(dz_ref[...], w_ref[...])

    return pl.pallas_call(
        body, name="in_proj_bwd_dx", grid=(t // tm,),
        in_specs=[pl.BlockSpec((tm, d), lambda i: (i, 0)), pl.BlockSpec((tm, n), lambda i: (i, 0)),
                  pl.BlockSpec((d, n), lambda i: (0, 0)), pl.BlockSpec((8, 128), lambda i: (0, 0))],
        out_specs=pl.BlockSpec((tm, d), lambda i: (i, 0)),
        out_shape=jax.ShapeDtypeStruct((t, d), F32),
        compiler_params=_cparams(("parallel",)),
    )(dxr, dz, w, after)


def _in_proj_bwd_dw(x2, dz):
    t, d = x2.shape
    n = dz.shape[1]
    tm = min(ROW_TILE, t)
    nc = 768

    def body(x_ref, dz_ref, dw_ref, done_ref):
        @pl.when(pl.program_id(0) == 0)
        def _():
            dw_ref[...] = jnp.zeros_like(dw_ref)
            done_ref[...] = jnp.zeros_like(done_ref)
        xb = _mm(x_ref[...])
        for j in range(n // nc):
            cs = slice(j * nc, (j + 1) * nc)
            dw_ref[:, cs] += _dot_tn(xb, dz_ref[:, cs])

    return pl.pallas_call(
        body, name="in_proj_bwd_dw", grid=(t // tm,),
        in_specs=[pl.BlockSpec((tm, d), lambda i: (i, 0)), pl.BlockSpec((tm, n), lambda i: (i, 0))],
        out_specs=[pl.BlockSpec((d, n), lambda i: (0, 0)), pl.BlockSpec((8, 128), lambda i: (0, 0))],
        out_shape=[jax.ShapeDtypeStruct((d, n), F32), jax.ShapeDtypeStruct((8, 128), F32)],
        compiler_params=_cparams(("arbitrary",)),
    )(x2, dz)


def _halo_specs(ts, s_len, cols, left=True):
    per = ts // HALO
    last = s_len // HALO - 1
    if left:
        return pl.BlockSpec((1, HALO, cols), lambda b, s: (b, jnp.maximum(s * per - 1, 0), 0))
    return pl.BlockSpec((1, HALO, cols), lambda b, s: (b, jnp.minimum((s + 1) * per, last), 0))


def _build_shifts(src_ref, sh_ref, rows):
    for r in range(1, 8):
        sh_ref[r - 1, 0:rows, :] = src_ref[r:r + rows, :]


def _shifted(src_ref, sh_ref, r, start, n):
    if r == 0:
        return src_ref[pl.ds(start, n), :]
    return sh_ref[r - 1, pl.ds(start, n), :]


def _tril_masks():
    ri = lax.broadcasted_iota(jnp.int32, (GBLK, GBLK), 0)
    ci = lax.broadcasted_iota(jnp.int32, (GBLK, GBLK), 1)
    return ri >= ci, ci >= ri


def _spatial(w_ref, keep, vb):
    lane = lax.broadcasted_iota(jnp.int32, (GBLK, GBLK), 1)
    outs = []
    for p in range(4):
        xs = vb[:, p * GBLK:(p + 1) * GBLK]
        r0 = jnp.dot(_mm(jnp.where(keep, w_ref[2 * p], 0.0)), xs, preferred_element_type=F32)
        r1 = jnp.dot(_mm(jnp.where(keep, w_ref[2 * p + 1], 0.0)), xs, preferred_element_type=F32)
        outs.append(jnp.where(lane < 64, r0, r1))
    return jnp.concatenate(outs, axis=1)


def _even_fwd(z3, cw, vec, ws, bsf):
    bsz, s_len, _ = z3.shape
    ts = min(SEQ_TILE, s_len)
    ext_rows = ts + HALO

    def body(z_ref, zl_ref, cw_ref, vec_ref, ws_ref, bsf_ref, y_ref, a1_ref, ext_ref, sh_ref):
        s = pl.program_id(1)
        hl = zl_ref[0]
        a0h = hl[:, 0:W_BR] * _sigmoid(hl[:, W_BR:2 * W_BR])
        ext_ref[0:HALO, :] = jnp.where(s > 0, a0h, 0.0)
        for a, b in _row_chunks(0, ts):
            ext_ref[HALO + a:HALO + b, :] = z_ref[0, a:b, 0:W_BR] * _sigmoid(z_ref[0, a:b, W_BR:2 * W_BR])
        ext_ref[ext_rows:ext_rows + 8, :] = jnp.zeros((8, W_BR), F32)
        _build_shifts(ext_ref, sh_ref, ext_rows)

        def conv_chunk(ci, carry):
            base = pl.multiple_of(ci * CONV_CHUNK, CONV_CHUNK)
            acc = jnp.zeros((CONV_CHUNK, W_BR), F32) + vec_ref[0:1, :]
            for k in range(KA):
                q, r = divmod(2 + k, 8)
                acc = acc + _shifted(ext_ref, sh_ref, r, base + 8 * q, CONV_CHUNK) * cw_ref[k:k + 1, :]
            a1_ref[0, pl.ds(base, CONV_CHUNK), :] = acc
            return carry

        lax.fori_loop(0, ts // CONV_CHUNK, conv_chunk, 0)
        keep, _ = _tril_masks()

        def block(bi, carry):
            rows = pl.ds(pl.multiple_of(bi * GBLK, GBLK), GBLK)
            xh, _ = _ln_stats(a1_ref[0, rows, :])
            a = _silu(xh * vec_ref[1:2, :] + vec_ref[2:3, :]) * _silu(z_ref[0, rows, 2 * W_BR:3 * W_BR])
            y_ref[0, rows, 0:W_BR] = a.astype(y_ref.dtype)
            ua = _gelu(z_ref[0, rows, 3 * W_BR:4 * W_BR])
            vh, _ = _ln_stats(_gelu(z_ref[0, rows, 4 * W_BR:5 * W_BR]))
            vl = vh * vec_ref[3:4, :] + vec_ref[4:5, :]
            sg = _spatial(ws_ref, keep, _mm(vl)) + bsf_ref[...]
            g = ua * sg * _silu(z_ref[0, rows, 5 * W_BR:6 * W_BR])
            y_ref[0, rows, W_BR:2 * W_BR] = g.astype(y_ref.dtype)
            return carry

        lax.fori_loop(0, ts // GBLK, block, 0)

    full = lambda shape: pl.BlockSpec(shape, lambda b, s: (0,) * len(shape))
    return pl.pallas_call(
        body, name="even_fwd", grid=(bsz, s_len // ts),
        in_specs=[pl.BlockSpec((1, ts, N_COLS), lambda b, s: (b, s, 0)), _halo_specs(ts, s_len, 2 * W_BR),
                  full((32, W_BR)), full((8, W_BR)), full((8, GBLK, GBLK)), full((GBLK, W_BR))],
        out_specs=[pl.BlockSpec((1, ts, 2 * W_BR), lambda b, s: (b, s, 0)), pl.BlockSpec((1, ts, W_BR), lambda b, s: (b, s, 0))],
        out_shape=[jax.ShapeDtypeStruct((bsz, s_len, 2 * W_BR), MM_DTYPE), jax.ShapeDtypeStruct((bsz, s_len, W_BR), F32)],
        scratch_shapes=[pltpu.VMEM((ext_rows + 8, W_BR), F32), pltpu.VMEM((7, ext_rows, W_BR), F32)],
        compiler_params=_cparams(("parallel", "parallel")),
    )(z3, z3, cw, vec, ws, bsf)


def _even_bwd(z3, dy3, a13, cw, vec, ws, wst, bsf):
    bsz, s_len, _ = z3.shape
    ts = min(SEQ_TILE, s_len)
    n_s = s_len // ts
    ext_rows = ts + 2 * HALO
    a_rows = ts + HALO

    def body(z_ref, zl_ref, zr_ref, dy_ref, dyr_ref, a1_ref, a1r_ref, cw_ref, vec_ref, ws_ref, wst_ref, bsf_ref,
             dz_ref, dbin_ref, dcw_ref, dvec_ref, dws_ref, dbsf_ref,
             ext_ref, sh_ref, ag_ref, dya_ref, d_ref, accw_ref):
        b = pl.program_id(0)
        s = pl.program_id(1)

        @pl.when((b == 0) & (s == 0))
        def _():
            dbin_ref[...] = jnp.zeros_like(dbin_ref)
            dcw_ref[...] = jnp.zeros_like(dcw_ref)
            dvec_ref[...] = jnp.zeros_like(dvec_ref)
            dws_ref[...] = jnp.zeros_like(dws_ref)
            dbsf_ref[...] = jnp.zeros_like(dbsf_ref)

        has_right = s < n_s - 1
        hl = zl_ref[0]
        hr = zr_ref[0]
        ext_ref[0:HALO, :] = jnp.where(s > 0, hl[:, 0:W_BR] * _sigmoid(hl[:, W_BR:2 * W_BR]), 0.0)
        for a, b in _row_chunks(0, ts):
            ext_ref[HALO + a:HALO + b, :] = z_ref[0, a:b, 0:W_BR] * _sigmoid(z_ref[0, a:b, W_BR:2 * W_BR])
            ag_ref[a:b, :] = z_ref[0, a:b, 2 * W_BR:3 * W_BR]
            dya_ref[a:b, :] = dy_ref[0, a:b, 0:W_BR]
        ext_ref[HALO + ts:ext_rows, :] = hr[:, 0:W_BR] * _sigmoid(hr[:, W_BR:2 * W_BR])
        ext_ref[ext_rows:ext_rows + 8, :] = jnp.zeros((8, W_BR), F32)
        ag_ref[ts:a_rows, :] = hr[:, 2 * W_BR:3 * W_BR]
        dya_ref[ts:a_rows, :] = dyr_ref[0][:, 0:W_BR]
        _build_shifts(ext_ref, sh_ref, ext_rows)

        def a_chunk(base, n, main):
            rows = pl.ds(base, n)
            xh, rs = _ln_stats(a1_ref[0, rows, :] if main else a1r_ref[0])
            ln = xh * vec_ref[1:2, :] + vec_ref[2:3, :]
            sl, dsl = _silu_grad(ln)
            sgt, dsgt = _silu_grad(ag_ref[rows, :])
            dya = dya_ref[rows, :]
            dln = dya * sgt * dsl
            da1 = _ln_bwd(dln * vec_ref[1:2, :], xh, rs)
            if main:
                d_ref[rows, :] = da1
                dag = dya * sl * dsgt
                dz_ref[0, rows, 2 * W_BR:3 * W_BR] = dag.astype(dz_ref.dtype)
                dbin_ref[0:1, 2 * W_BR:3 * W_BR] += _rowsum(dag)
                dvec_ref[0:1, :] += _rowsum(da1)
                dvec_ref[1:2, :] += _rowsum(dln * xh)
                dvec_ref[2:3, :] += _rowsum(dln)
            else:
                d_ref[rows, :] = jnp.where(has_right, da1, 0.0)

        def a_main(ci, carry):
            a_chunk(pl.multiple_of(ci * GBLK, GBLK), GBLK, True)
            return carry

        lax.fori_loop(0, ts // GBLK, a_main, 0)
        a_chunk(ts, HALO, False)
        d_ref[a_rows:a_rows + 8, :] = jnp.zeros((8, W_BR), F32)

        accw_ref[...] = jnp.zeros_like(accw_ref)

        def dw_chunk(ci, carry):
            base = pl.multiple_of(ci * DW_CHUNK, DW_CHUNK)
            d = d_ref[pl.ds(base, DW_CHUNK), :]
            for k in range(KA):
                q, r = divmod(2 + k, 8)
                prod = d * _shifted(ext_ref, sh_ref, r, base + 8 * q, DW_CHUNK)
                accw_ref[k] += jnp.sum(prod.reshape(DW_CHUNK // 8, 8, W_BR), axis=0)
            return carry

        lax.fori_loop(0, ts // DW_CHUNK, dw_chunk, 0)
        dcw_ref[...] += jnp.sum(accw_ref[...], axis=1)

        _build_shifts(d_ref, sh_ref, a_rows)

        def dx_chunk(ci, carry):
            base = pl.multiple_of(ci * CONV_CHUNK, CONV_CHUNK)
            rows = pl.ds(base, CONV_CHUNK)
            acc = jnp.zeros((CONV_CHUNK, W_BR), F32)
            for m in range(KA):
                q, r = divmod(m, 8)
                acc = acc + _shifted(d_ref, sh_ref, r, base + 8 * q, CONV_CHUNK) * cw_ref[KA - 1 - m:KA - m, :]
            aval = z_ref[0, rows, 0:W_BR]
            sg = _sigmoid(z_ref[0, rows, W_BR:2 * W_BR])
            dval = acc * sg
            dglu = acc * aval * sg * (1.0 - sg)
            dz_ref[0, rows, 0:W_BR] = dval.astype(dz_ref.dtype)
            dz_ref[0, rows, W_BR:2 * W_BR] = dglu.astype(dz_ref.dtype)
            dbin_ref[0:1, 0:W_BR] += _rowsum(dval)
            dbin_ref[0:1, W_BR:2 * W_BR] += _rowsum(dglu)
            return carry

        lax.fori_loop(0, ts // CONV_CHUNK, dx_chunk, 0)

        keep, keep_t = _tril_masks()
        lane = lax.broadcasted_iota(jnp.int32, (GBLK, GBLK), 1)

        def block(bi, carry):
            rows = pl.ds(pl.multiple_of(bi * GBLK, GBLK), GBLK)
            ua, dua = _gelu_grad(z_ref[0, rows, 3 * W_BR:4 * W_BR])
            va, dva = _gelu_grad(z_ref[0, rows, 4 * W_BR:5 * W_BR])
            sgt, dsgt = _silu_grad(z_ref[0, rows, 5 * W_BR:6 * W_BR])
            vh, rs = _ln_stats(va)
            vlb = _mm(vh * vec_ref[3:4, :] + vec_ref[4:5, :])
            sg = _spatial(ws_ref, keep, vlb) + bsf_ref[...]
            dyg = dy_ref[0, rows, W_BR:2 * W_BR]
            du = dyg * sg * sgt * dua
            dsg = dyg * ua * sgt
            dgg = dyg * ua * sg * dsgt
            dvl = _spatial(wst_ref, keep_t, _mm(dsg))
            for p in range(4):
                dsp = dsg[:, p * GBLK:(p + 1) * GBLK]
                vlp = vlb[:, p * GBLK:(p + 1) * GBLK]
                dws_ref[2 * p] += jnp.where(keep, _dot_nt(jnp.where(lane < 64, dsp, 0.0), vlp), 0.0)
                dws_ref[2 * p + 1] += jnp.where(keep, _dot_nt(jnp.where(lane >= 64, dsp, 0.0), vlp), 0.0)
            dbsf_ref[...] += dsg
            dvec_ref[3:4, :] += _rowsum(dvl * vh)
            dvec_ref[4:5, :] += _rowsum(dvl)
            dv = _ln_bwd(dvl * vec_ref[3:4, :], vh, rs) * dva
            dz_ref[0, rows, 3 * W_BR:4 * W_BR] = du.astype(dz_ref.dtype)
            dz_ref[0, rows, 4 * W_BR:5 * W_BR] = dv.astype(dz_ref.dtype)
            dz_ref[0, rows, 5 * W_BR:6 * W_BR] = dgg.astype(dz_ref.dtype)
            dbin_ref[0:1, 3 * W_BR:4 * W_BR] += _rowsum(du)
            dbin_ref[0:1, 4 * W_BR:5 * W_BR] += _rowsum(dv)
            dbin_ref[0:1, 5 * W_BR:6 * W_BR] += _rowsum(dgg)
            return carry

        lax.fori_loop(0, ts // GBLK, block, 0)

    full = lambda shape: pl.BlockSpec(shape, lambda b, s: (0,) * len(shape))
    acc_shapes = [(1, N_COLS), (32, W_BR), (8, W_BR), (8, GBLK, GBLK), (GBLK, W_BR)]
    return pl.pallas_call(
        body, name="even_bwd", grid=(bsz, n_s),
        in_specs=[pl.BlockSpec((1, ts, N_COLS), lambda b, s: (b, s, 0)),
                  _halo_specs(ts, s_len, N_COLS, True), _halo_specs(ts, s_len, N_COLS, False),
                  pl.BlockSpec((1, ts, 2 * W_BR), lambda b, s: (b, s, 0)), _halo_specs(ts, s_len, 2 * W_BR, False),
                  pl.BlockSpec((1, ts, W_BR), lambda b, s: (b, s, 0)), _halo_specs(ts, s_len, W_BR, False),
                  full((32, W_BR)), full((8, W_BR)), full((8, GBLK, GBLK)), full((8, GBLK, GBLK)), full((GBLK, W_BR))],
        out_specs=[pl.BlockSpec((1, ts, N_COLS), lambda b, s: (b, s, 0))] + [full(sh) for sh in acc_shapes],
        out_shape=[jax.ShapeDtypeStruct((bsz, s_len, N_COLS), MM_DTYPE)] + [jax.ShapeDtypeStruct(sh, F32) for sh in acc_shapes],
        scratch_shapes=[pltpu.VMEM((ext_rows + 8, W_BR), F32), pltpu.VMEM((7, ext_rows, W_BR), F32),
                        pltpu.VMEM((a_rows, W_BR), F32), pltpu.VMEM((a_rows, W_BR), F32),
                        pltpu.VMEM((a_rows + 8, W_BR), F32), pltpu.VMEM((32, 8, W_BR), F32)],
        compiler_params=_cparams(("arbitrary", "arbitrary")),
    )(z3, z3, z3, dy3, dy3, a13, a13, cw, vec, ws, wst, bsf)


def _row_chunks(lo, hi):
    return [(a, min(a + ROW_CHUNK, hi)) for a in range(lo, hi, ROW_CHUNK)]


def _pool_stages(e_refs, rows):
    e0, e1, e2, e3, e4 = e_refs
    for a, b in _row_chunks(8, rows):
        e1[a:b, :] = e0[a:b, :] + e0[a - 1:b - 1, :]
    for a, b in _row_chunks(16, rows):
        e2[a:b, GBLK:] = e1[a:b, GBLK:] + e1[a - 2:b - 2, GBLK:]
    for a, b in _row_chunks(24, rows):
        e3[a:b, 2 * GBLK:] = e2[a:b, 2 * GBLK:] + e2[a - 4:b - 4, 2 * GBLK:]
    for a, b in _row_chunks(32, rows):
        e4[a:b, 3 * GBLK:] = e3[a:b, 3 * GBLK:] + e3[a - 8:b - 8, 3 * GBLK:]


def _pool_counts(start, n):
    pos = (start + 1 + lax.broadcasted_iota(jnp.int32, (n, 1), 0)).astype(F32)
    return [jnp.minimum(pos, float(w)) for w in POOL_WINDOWS]


def _pooled_into(e_refs, pooled_ref, s, ts):
    for a, b in _row_chunks(0, ts):
        cnt = _pool_counts(s * ts + a, b - a)
        for g in range(4):
            cs = slice(g * GBLK, (g + 1) * GBLK)
            pooled_ref[a:b, cs] = e_refs[g + 1][HALO + a:HALO + b, cs] / cnt[g] - e_refs[0][HALO + a:HALO + b, cs]


def _odd_prologue(z_ref, hl, s, ts, e_refs, pooled_ref, dext_ref, ec_ref, vec_ref):
    e0 = e_refs[0]
    e0[0:HALO, :] = jnp.where(s > 0, hl[:, 0:W_BR], 0.0)
    dext_ref[0:HALO, :] = jnp.where(s > 0, hl[:, 2 * W_BR:3 * W_BR] * hl[:, 4 * W_BR:5 * W_BR], 0.0)
    for a, b in _row_chunks(0, ts):
        e0[HALO + a:HALO + b, :] = z_ref[0, a:b, 0:W_BR]
        dext_ref[HALO + a:HALO + b, :] = z_ref[0, a:b, 2 * W_BR:3 * W_BR] * z_ref[0, a:b, 4 * W_BR:5 * W_BR]
    _pool_stages(e_refs, ts + HALO)
    _pooled_into(e_refs, pooled_ref, s, ts)
    for a, b in _row_chunks(0, ts):
        ec_ref[a:b, :] = (vec_ref[1:2, :] * dext_ref[HALO - 2 + a:HALO - 2 + b, :] + vec_ref[2:3, :] * dext_ref[HALO - 1 + a:HALO - 1 + b, :]
                          + vec_ref[3:4, :] * dext_ref[HALO + a:HALO + b, :])


def _odd_fwd(z3, wp, vec):
    bsz, s_len, _ = z3.shape
    ts = min(SEQ_TILE, s_len)
    ext_rows = ts + HALO

    def body(z_ref, zl_ref, wp_ref, vec_ref, y_ref, e0, e1, e2, e3, e4, pooled_ref, dext_ref, ec_ref):
        s = pl.program_id(1)
        _odd_prologue(z_ref, zl_ref[0], s, ts, (e0, e1, e2, e3, e4), pooled_ref, dext_ref, ec_ref, vec_ref)

        def block(bi, carry):
            rows = pl.ds(pl.multiple_of(bi * GBLK, GBLK), GBLK)
            pb = _mm(pooled_ref[rows, :])
            cpre = jnp.concatenate([jnp.dot(pb[:, g * GBLK:(g + 1) * GBLK], wp_ref[g], preferred_element_type=F32)
                                    for g in range(4)], axis=1)
            c = cpre * vec_ref[0:1, :] * _silu(z_ref[0, rows, W_BR:2 * W_BR])
            d = z_ref[0, rows, 3 * W_BR:4 * W_BR] * ec_ref[rows, :] * _silu(z_ref[0, rows, 5 * W_BR:6 * W_BR])
            y_ref[0, rows, 0:W_BR] = c.astype(y_ref.dtype)
            y_ref[0, rows, W_BR:2 * W_BR] = d.astype(y_ref.dtype)
            return carry

        lax.fori_loop(0, ts // GBLK, block, 0)

    full = lambda shape: pl.BlockSpec(shape, lambda b, s: (0,) * len(shape))
    ebuf = pltpu.VMEM((ext_rows, W_BR), F32)
    return pl.pallas_call(
        body, name="odd_fwd", grid=(bsz, s_len // ts),
        in_specs=[pl.BlockSpec((1, ts, N_COLS), lambda b, s: (b, s, 0)), _halo_specs(ts, s_len, N_COLS),
                  full((4, GBLK, GBLK)), full((8, W_BR))],
        out_specs=pl.BlockSpec((1, ts, 2 * W_BR), lambda b, s: (b, s, 0)),
        out_shape=jax.ShapeDtypeStruct((bsz, s_len, 2 * W_BR), MM_DTYPE),
        scratch_shapes=[ebuf, ebuf, ebuf, ebuf, ebuf, pltpu.VMEM((ts, W_BR), F32), ebuf, pltpu.VMEM((ts, W_BR), F32)],
        compiler_params=_cparams(("parallel", "parallel")),
    )(z3, z3, wp, vec)


def _odd_bwd(z3, dy3, wp, vec):
    bsz, s_len, _ = z3.shape
    ts = min(SEQ_TILE, s_len)
    n_s = s_len // ts
    ext_rows = ts + HALO

    def body(z_ref, zl_ref, zr_ref, dy_ref, dyr_ref, wp_ref, vec_ref,
             dz_ref, dbin_ref, dwp_ref, dvec_ref,
             e0, e1, e2, e3, e4, pooled_ref, dext_ref, ec_ref, q_ref, dp_ref, de_ref, f1, f2, f3, f4):
        b = pl.program_id(0)
        s = pl.program_id(1)

        @pl.when((b == 0) & (s == 0))
        def _():
            dbin_ref[...] = jnp.zeros_like(dbin_ref)
            dwp_ref[...] = jnp.zeros_like(dwp_ref)
            dvec_ref[...] = jnp.zeros_like(dvec_ref)

        has_right = s < n_s - 1
        _odd_prologue(z_ref, zl_ref[0], s, ts, (e0, e1, e2, e3, e4), pooled_ref, dext_ref, ec_ref, vec_ref)

        def grads(zc_gate, zd_b, zd_gate, dyc, dyd, rows_out, n, start, valid):
            sgt = _silu(zc_gate)
            dcpre = dyc * vec_ref[0:1, :] * sgt
            db = _mm(dcpre)
            dpool = jnp.concatenate([_dot_nt(db[:, g * GBLK:(g + 1) * GBLK], wp_ref[g]) for g in range(4)], axis=1)
            cnt = _pool_counts(start, n)
            q = jnp.concatenate([dpool[:, g * GBLK:(g + 1) * GBLK] / cnt[g] for g in range(4)], axis=1)
            de = dyd * zd_b * _silu(zd_gate)
            if valid is not None:
                q = jnp.where(valid, q, 0.0)
                de = jnp.where(valid, de, 0.0)
            q_ref[rows_out, :] = q
            dp_ref[rows_out, :] = dpool
            de_ref[rows_out, :] = de
            return dcpre

        def block(bi, carry):
            base = pl.multiple_of(bi * GBLK, GBLK)
            rows = pl.ds(base, GBLK)
            cg = z_ref[0, rows, W_BR:2 * W_BR]
            dyc = dy_ref[0, rows, 0:W_BR]
            dyd = dy_ref[0, rows, W_BR:2 * W_BR]
            d_b = z_ref[0, rows, 3 * W_BR:4 * W_BR]
            d_gate = z_ref[0, rows, 5 * W_BR:6 * W_BR]
            dcpre = grads(cg, d_b, d_gate, dyc, dyd, rows, GBLK, s * ts + base, None)
            pb = _mm(pooled_ref[rows, :])
            dcb = _mm(dcpre)
            cpre = jnp.concatenate([jnp.dot(pb[:, g * GBLK:(g + 1) * GBLK], wp_ref[g], preferred_element_type=F32)
                                    for g in range(4)], axis=1)
            for g in range(4):
                cs = slice(g * GBLK, (g + 1) * GBLK)
                dwp_ref[g] += _dot_tn(pb[:, cs], dcb[:, cs])
            sgt, dsgt = _silu_grad(cg)
            dvec_ref[0:1, :] += _rowsum(dyc * cpre * sgt)
            dcg = dyc * cpre * vec_ref[0:1, :] * dsgt
            sdt, dsdt = _silu_grad(d_gate)
            ec = ec_ref[rows, :]
            ddb = dyd * ec * sdt
            ddg = dyd * d_b * ec * dsdt
            dz_ref[0, rows, W_BR:2 * W_BR] = dcg.astype(dz_ref.dtype)
            dz_ref[0, rows, 3 * W_BR:4 * W_BR] = ddb.astype(dz_ref.dtype)
            dz_ref[0, rows, 5 * W_BR:6 * W_BR] = ddg.astype(dz_ref.dtype)
            dbin_ref[0:1, W_BR:2 * W_BR] += _rowsum(dcg)
            dbin_ref[0:1, 3 * W_BR:4 * W_BR] += _rowsum(ddb)
            dbin_ref[0:1, 5 * W_BR:6 * W_BR] += _rowsum(ddg)
            return carry

        lax.fori_loop(0, ts // GBLK, block, 0)
        hr = zr_ref[0]
        dyr = dyr_ref[0]
        grads(hr[:, W_BR:2 * W_BR], hr[:, 3 * W_BR:4 * W_BR], hr[:, 5 * W_BR:6 * W_BR], dyr[:, 0:W_BR], dyr[:, W_BR:2 * W_BR],
              slice(ts, ext_rows), HALO, (s + 1) * ts, has_right)

        for a, b in _row_chunks(0, ts + 24):
            f1[a:b, :] = q_ref[a:b, :] + q_ref[a + 1:b + 1, :]
        for a, b in _row_chunks(0, ts + 16):
            f2[a:b, GBLK:] = f1[a:b, GBLK:] + f1[a + 2:b + 2, GBLK:]
        for a, b in _row_chunks(0, ts + 8):
            f3[a:b, 2 * GBLK:] = f2[a:b, 2 * GBLK:] + f2[a + 4:b + 4, 2 * GBLK:]
        for a, b in _row_chunks(0, ts):
            f4[a:b, 3 * GBLK:] = f3[a:b, 3 * GBLK:] + f3[a + 8:b + 8, 3 * GBLK:]

        for a, b in _row_chunks(0, ts):
            for g, f in enumerate((f1, f2, f3, f4)):
                cs = slice(g * GBLK, (g + 1) * GBLK)
                dvg = f[a:b, cs] - dp_ref[a:b, cs]
                dz_ref[0, a:b, cs] = dvg.astype(dz_ref.dtype)
                dbin_ref[0:1, cs] += _rowsum(dvg)
            de = de_ref[a:b, :]
            ddc = vec_ref[1:2, :] * de_ref[a + 2:b + 2, :] + vec_ref[2:3, :] * de_ref[a + 1:b + 1, :] + vec_ref[3:4, :] * de
            ddh = ddc * z_ref[0, a:b, 4 * W_BR:5 * W_BR]
            ddcc = ddc * z_ref[0, a:b, 2 * W_BR:3 * W_BR]
            dz_ref[0, a:b, 2 * W_BR:3 * W_BR] = ddh.astype(dz_ref.dtype)
            dz_ref[0, a:b, 4 * W_BR:5 * W_BR] = ddcc.astype(dz_ref.dtype)
            dbin_ref[0:1, 2 * W_BR:3 * W_BR] += _rowsum(ddh)
            dbin_ref[0:1, 4 * W_BR:5 * W_BR] += _rowsum(ddcc)
            for k in range(3):
                dvec_ref[1 + k:2 + k, :] += _rowsum(de * dext_ref[HALO - 2 + k + a:HALO - 2 + k + b, :])

    full = lambda shape: pl.BlockSpec(shape, lambda b, s: (0,) * len(shape))
    acc_shapes = [(1, N_COLS), (4, GBLK, GBLK), (8, W_BR)]
    ebuf = pltpu.VMEM((ext_rows, W_BR), F32)
    tbuf = pltpu.VMEM((ts, W_BR), F32)
    return pl.pallas_call(
        body, name="odd_bwd", grid=(bsz, n_s),
        in_specs=[pl.BlockSpec((1, ts, N_COLS), lambda b, s: (b, s, 0)),
                  _halo_specs(ts, s_len, N_COLS, True), _halo_specs(ts, s_len, N_COLS, False),
                  pl.BlockSpec((1, ts, 2 * W_BR), lambda b, s: (b, s, 0)), _halo_specs(ts, s_len, 2 * W_BR, False),
                  full((4, GBLK, GBLK)), full((8, W_BR))],
        out_specs=[pl.BlockSpec((1, ts, N_COLS), lambda b, s: (b, s, 0))] + [full(sh) for sh in acc_shapes],
        out_shape=[jax.ShapeDtypeStruct((bsz, s_len, N_COLS), MM_DTYPE)] + [jax.ShapeDtypeStruct(sh, F32) for sh in acc_shapes],
        scratch_shapes=[ebuf, ebuf, ebuf, ebuf, ebuf, tbuf, ebuf, tbuf, ebuf, ebuf, ebuf, ebuf, ebuf, ebuf, tbuf],
        compiler_params=_cparams(("arbitrary", "arbitrary")),
    )(z3, z3, z3, dy3, dy3, wp, vec)


def _post_fwd(y2, x2, p_layer, w_out, wg, wple, vec, tgt=None, nxt=None):
    p_all, layer = p_layer
    t, d = x2.shape
    tm = min(ROW_TILE, t)
    last = tgt is not None
    assert not (last and nxt is not None)
    extra = [tgt] if last else (list(nxt) if nxt is not None else [])
    n_in = 7 + len(extra)
    n_cols = nxt[0].shape[1] if nxt is not None else 0
    nc = 768

    def body(*refs):
        y_ref, x_ref, p_ref, wo_ref, wg_ref, wp_ref, vec_ref = refs[:7]
        xn_ref, r_ref, gate_ref = refs[n_in:n_in + 3]
        r = ALPHA * x_ref[...] + jnp.dot(y_ref[...], wo_ref[...], preferred_element_type=F32) + vec_ref[0:1, :]
        r_ref[...] = r
        xh, _ = _ln_stats(r)
        h = xh * vec_ref[1:2, :] + vec_ref[2:3, :]
        gate = _sigmoid(_dot(h, wg_ref[...]) + vec_ref[3:4, :])
        gate_ref[...] = gate
        xn = h + gate * _dot(p_ref[...], wp_ref[...])
        if last:
            sq_ref = refs[n_in + 3]

            @pl.when(pl.program_id(0) == 0)
            def _():
                sq_ref[...] = jnp.zeros_like(sq_ref)
            e = xn - refs[7][...]
            xn_ref[...] = e / float(d)
            sq_ref[...] += _rowsum(e * e)
        else:
            xn_ref[...] = xn
        if nxt is not None:
            win_ref, bin_ref, z_ref = refs[7], refs[8], refs[n_in + 3]
            xb = _mm(xn)
            for j in range(n_cols // nc):
                cs = slice(j * nc, (j + 1) * nc)
                z_ref[:, cs] = jnp.dot(xb, win_ref[:, cs], preferred_element_type=F32) + bin_ref[:, cs]

    row = lambda c: pl.BlockSpec((tm, c), lambda i: (i, 0))
    full = lambda shape: pl.BlockSpec(shape, lambda i: (0,) * len(shape), pipeline_mode=pl.Buffered(1))
    extra_in = [row(d)] if last else ([full((d, n_cols)), full((1, n_cols))] if nxt is not None else [])
    extra_out = [full((1, d))] if last else ([row(n_cols)] if nxt is not None else [])
    extra_shape = [jax.ShapeDtypeStruct((1, d), F32)] if last else ([jax.ShapeDtypeStruct((t, n_cols), F32)] if nxt is not None else [])
    return pl.pallas_call(
        body, name="post_fwd_loss" if last else "post_fwd", grid=(t // tm,),
        in_specs=[row(d), row(d), pl.BlockSpec((None, tm, D_PLE), lambda i: (layer, i, 0)),
                  full((d, d)), full((d, d)), full((D_PLE, d)), full((8, d))] + extra_in,
        out_specs=[row(d), row(d), row(d)] + extra_out,
        out_shape=[jax.ShapeDtypeStruct((t, d), F32)] * 3 + extra_shape,
        compiler_params=_cparams(("arbitrary",) if last else ("parallel",)),
    )(y2, x2, p_all, w_out, wg, wple, vec, *extra)


def _post_bwd(dxn, r2, gate2, p_layer, y2, w_out, wg, wple, vec):
    p_all, layer = p_layer
    t, d = r2.shape
    tm = min(BWD_ROW_TILE, t)

    def body(dxn_ref, r_ref, gate_ref, p_ref, y_ref, wo_ref, wg_ref, wp_ref, vec_ref,
             dxr_ref, dy_ref, dwo_ref, dwg_ref, dwp_ref, dvec_ref):
        @pl.when(pl.program_id(0) == 0)
        def _():
            dwo_ref[...] = jnp.zeros_like(dwo_ref)
            dwg_ref[...] = jnp.zeros_like(dwg_ref)
            dwp_ref[...] = jnp.zeros_like(dwp_ref)
            dvec_ref[...] = jnp.zeros_like(dvec_ref)

        dxn = dxn_ref[...]
        gate = gate_ref[...]
        xh, rs = _ln_stats(r_ref[...])
        hb = _mm(xh * vec_ref[1:2, :] + vec_ref[2:3, :])
        pb = _mm(p_ref[...])
        pe = jnp.dot(pb, wp_ref[...], preferred_element_type=F32)
        dpre = dxn * pe * gate * (1.0 - gate)
        dpb = _mm(dpre)
        dh = dxn + _dot_nt(dpb, wg_ref[...])
        dwg_ref[...] += _dot_tn(hb, dpb)
        dwp_ref[...] += _dot_tn(pb, dxn * gate)
        dr = _ln_bwd(dh * vec_ref[1:2, :], xh, rs)
        drb = _mm(dr)
        dxr_ref[...] = ALPHA * dr
        dy_ref[...] = _dot_nt(drb, wo_ref[...])
        dwo_ref[...] += _dot_tn(y_ref[...], drb)
        dvec_ref[0:1, :] += _rowsum(dr)
        dvec_ref[1:2, :] += _rowsum(dh * xh)
        dvec_ref[2:3, :] += _rowsum(dh)
        dvec_ref[3:4, :] += _rowsum(dpre)

    row = lambda c: pl.BlockSpec((tm, c), lambda i: (i, 0))
    full = lambda shape: pl.BlockSpec(shape, lambda i: (0,) * len(shape), pipeline_mode=pl.Buffered(1))
    acc_shapes = [(d, d), (d, d), (D_PLE, d), (8, d)]
    return pl.pallas_call(
        body, name="post_bwd", grid=(t // tm,),
        in_specs=[row(d), row(d), row(d), pl.BlockSpec((None, tm, D_PLE), lambda i: (layer, i, 0)), row(d),
                  full((d, d)), full((d, d)), full((D_PLE, d)), full((8, d))],
        out_specs=[row(d), row(d)] + [full(sh) for sh in acc_shapes],
        out_shape=[jax.ShapeDtypeStruct((t, d), F32)] * 2 + [jax.ShapeDtypeStruct(sh, F32) for sh in acc_shapes],
        compiler_params=_cparams(("arbitrary",)),
    )(dxn, r2, gate2, p_all, y2, w_out, wg, wple, vec)


def _place():
    x, y, c = lax.axis_index("x"), lax.axis_index("y"), lax.axis_index("c")
    chips = [(1 - x, y), (x, 1 - y), (1 - x, 1 - y)]
    return x, y, c, chips


def _shard_of(ref, ax, k, width, lo=None, ln=None):
    idx = [slice(None)] * 3
    if lo is not None:
        idx[0] = pl.ds(lo, ln)
    idx[ax] = pl.ds(k * width, width)
    return ref.at[tuple(idx)]


def _remote(src, dst, ssem, rsem, dev):
    return pltpu.make_async_remote_copy(src_ref=src, dst_ref=dst, send_sem=ssem, recv_sem=rsem, device_id=dev, device_id_type=MESH)


def _place_shard(w, ax, chip, dtype=MM_DTYPE, layer=None):
    l_dim, a_dim, b_dim = w.shape
    tr = min(256, a_dim)
    per = a_dim // tr
    first = 0
    if layer is not None:
        l_dim, first = 1, layer
    shape = [l_dim, a_dim, b_dim]
    shape[ax] *= 4
    if ax == 2:
        out_spec = pl.BlockSpec((1, tr, b_dim), lambda l, i, k: (l, i, k[0]))
    else:
        out_spec = pl.BlockSpec((1, tr, b_dim), lambda l, i, k: (l, k[0] * per + i, 0))

    def body(k_ref, w_ref, o_ref):
        o_ref[...] = w_ref[...].astype(o_ref.dtype)

    return pl.pallas_call(
        body, name="place_shard",
        grid_spec=pltpu.PrefetchScalarGridSpec(
            num_scalar_prefetch=1, grid=(l_dim, per),
            in_specs=[pl.BlockSpec((1, tr, b_dim), lambda l, i, k: (first + l, i, 0))], out_specs=out_spec),
        out_shape=jax.ShapeDtypeStruct(tuple(shape), dtype),
        compiler_params=_cparams(("parallel", "parallel")),
    )(chip, w)


def _shard_copies(src_refs, dst_refs, axes, sems):
    x, y, c, chips = _place()
    j = 2 * x + y
    cps = []
    for a, (s_ref, d_ref) in enumerate(zip(src_refs, dst_refs)):
        w = d_ref.shape[axes[a]] // 4
        for q, (qx, qy) in enumerate(chips):
            ssem, rsem = sems[3 * a + q]
            cps.append(_remote(_shard_of(s_ref, axes[a], j, w), _shard_of(d_ref, axes[a], j, w), ssem, rsem, (qx, qy, c)))
    return cps


def _chip_handshake():
    x, y, c, chips = _place()
    barrier = pltpu.get_barrier_semaphore()
    for qx, qy in chips:
        pl.semaphore_signal(barrier, inc=1, device_id=(qx, qy, c), device_id_type=MESH)
    pl.semaphore_wait(barrier, 3)


def _gather_async(name, collective_id, fulls, axes):
    n = len(fulls)
    refs = [jax.new_ref(f, memory_space=pltpu.MemorySpace.HBM) for f in fulls]
    moved = sum(2 * 3 * (f.size // 4) * jnp.dtype(f.dtype).itemsize for f in fulls)

    @pl.kernel(mesh=plsc.ScalarSubcoreMesh(axis_name="seq", num_cores=1), name=name,
               scratch_types=(pltpu.SemaphoreType.DMA,) * (6 * n),
               cost_estimate=pl.CostEstimate(flops=0, transcendentals=0, bytes_accessed=moved, remote_bytes_transferred=moved),
               compiler_params=pltpu.CompilerParams(collective_id=collective_id))
    def launch(*sems):
        _chip_handshake()
        cps = _shard_copies(refs, refs, axes, [(sems[2 * k], sems[2 * k + 1]) for k in range(3 * n)])
        for cp in cps:
            cp.start()
        for cp in cps:
            cp.wait()

    launch()
    return refs


def _gather_first_async(collective_id, w_full, sv_full):
    w_ref = jax.new_ref(w_full, memory_space=pltpu.MemorySpace.HBM)
    sv_ref = jax.new_ref(sv_full, memory_space=pltpu.MemorySpace.HBM)
    a_half = w_full.shape[1] // 2
    width = w_full.shape[2] // 4
    moved = 2 * 3 * (w_full.size // 4 + sv_full.size // 4 * 4)

    @pl.kernel(mesh=plsc.ScalarSubcoreMesh(axis_name="seq", num_cores=1), name="gather_first",
               scratch_types=(pltpu.SemaphoreType.DMA,) * 18,
               cost_estimate=pl.CostEstimate(flops=0, transcendentals=0, bytes_accessed=moved, remote_bytes_transferred=moved),
               compiler_params=pltpu.CompilerParams(collective_id=collective_id))
    def launch(*sems):
        x, y, c, chips = _place()
        barrier = pltpu.get_barrier_semaphore()
        for dev in [(qx, qy, c) for qx, qy in chips] + [(x, y, 1 - c)]:
            pl.semaphore_signal(barrier, inc=1, device_id=dev, device_id_type=MESH)
        pl.semaphore_wait(barrier, 4)
        j = 2 * x + y

        def piece(k, half):
            return w_ref.at[:, pl.ds(half * a_half, a_half), pl.ds(k * width, width)]

        small = _shard_copies([sv_ref], [sv_ref], [1], [(sems[2 * q], sems[2 * q + 1]) for q in range(3)])
        sends = [_remote(piece(j, c), piece(j, c), sems[6 + 2 * q], sems[7 + 2 * q], (qx, qy, c)) for q, (qx, qy) in enumerate(chips)]
        for cp in small + sends:
            cp.start()
        passed = []
        for q, (qx, qy) in enumerate(chips):
            got = piece(2 * qx + qy, c)
            _remote(got, got, sems[6 + 2 * q], sems[7 + 2 * q], (x, y, c)).wait_recv()
            cp = _remote(got, got, sems[12 + 2 * q], sems[13 + 2 * q], (x, y, 1 - c))
            cp.start()
            passed.append(cp)
        for q, (qx, qy) in enumerate(chips):
            other = piece(2 * qx + qy, 1 - c)
            _remote(other, other, sems[12 + 2 * q], sems[13 + 2 * q], (x, y, c)).wait_recv()
        for cp in small:
            cp.wait()
        for cp in sends + passed:
            cp.wait_send()

    launch()
    return w_ref, sv_ref


def _scatter_async(name, collective_id, grads, axes, after):
    n = len(grads)
    extra = len(after)
    outs = []
    for g, ax in zip(grads, axes):
        sh = list(g.shape)
        sh[ax] //= 4
        outs.append(jax.ShapeDtypeStruct((3,) + tuple(sh), g.dtype))

    def body(*refs):
        srcs, lands, sems = refs[:n], refs[n + extra:2 * n + extra], refs[2 * n + extra:]
        _chip_handshake()
        x, y, c, chips = _place()
        cps = []
        for a in range(n):
            w = srcs[a].shape[axes[a]] // 4
            for q, (qx, qy) in enumerate(chips):
                k = 3 * a + q
                cps.append(_remote(_shard_of(srcs[a], axes[a], 2 * qx + qy, w), lands[a].at[q], sems[2 * k], sems[2 * k + 1],
                                   (qx, qy, c)))
        for cp in cps:
            cp.start()
        for cp in cps:
            cp.wait()

    moved = sum(2 * 3 * math.prod(o.shape[1:]) * jnp.dtype(o.dtype).itemsize for o in outs)
    return pl.kernel(body, out_type=outs, mesh=plsc.ScalarSubcoreMesh(axis_name="seq", num_cores=1), name=name,
                     scratch_types=(pltpu.SemaphoreType.DMA,) * (6 * n),
                     cost_estimate=pl.CostEstimate(flops=0, transcendentals=0, bytes_accessed=moved, remote_bytes_transferred=moved),
                     compiler_params=pltpu.CompilerParams(collective_id=collective_id))(*grads, *after)


def _pair_swap_async(name, collective_id, sums):
    n = len(sums)

    def body(*refs):
        g_refs, o_refs, sems = refs[:n], refs[n:2 * n], refs[2 * n:]
        x, y, c, _ = _place()
        barrier = pltpu.get_barrier_semaphore()
        pl.semaphore_signal(barrier, inc=1, device_id=(x, y, 1 - c), device_id_type=MESH)
        pl.semaphore_wait(barrier, 1)
        cps = [_remote(g_refs[a], o_refs[a], sems[2 * a], sems[2 * a + 1], (x, y, 1 - c)) for a in range(n)]
        for cp in cps:
            cp.start()
        for cp in cps:
            cp.wait()

    moved = sum(2 * g.size * jnp.dtype(g.dtype).itemsize for g in sums)
    return pl.kernel(body, out_type=[jax.ShapeDtypeStruct(g.shape, g.dtype) for g in sums],
                     mesh=plsc.ScalarSubcoreMesh(axis_name="seq", num_cores=1), name=name,
                     scratch_types=(pltpu.SemaphoreType.DMA,) * (2 * n),
                     cost_estimate=pl.CostEstimate(flops=0, transcendentals=0, bytes_accessed=moved, remote_bytes_transferred=moved),
                     compiler_params=pltpu.CompilerParams(collective_id=collective_id))(*sums)


_FLIPS = [(fx, fy, fc) for fx in (0, 1) for fy in (0, 1) for fc in (0, 1)][1:]


def _small_reduce_scatter(small):
    r = small.shape[0]
    rh, rq = r // 2, r // 8

    def body(g_ref, out_ref, pair_ref, chip_ref, s1_ref, ssem, rsem):
        x, y, c, chips = _place()
        cp = _remote(g_ref.at[pl.ds((1 - c) * rh, rh)], pair_ref, ssem.at[0], rsem.at[0], (x, y, 1 - c))
        cp.start()
        cp.wait()
        s1_ref[...] = g_ref[pl.ds(pl.multiple_of(c * rh, 8), rh), :] + pair_ref[...]
        cps = [_remote(s1_ref.at[pl.ds((2 * qx + qy) * rq, rq)], chip_ref.at[q], ssem.at[1 + q], rsem.at[1 + q], (qx, qy, c))
               for q, (qx, qy) in enumerate(chips)]
        for cp in cps:
            cp.start()
        for cp in cps:
            cp.wait()
        out_ref[...] = ((s1_ref[pl.ds(pl.multiple_of((2 * x + y) * rq, 8), rq), :] + chip_ref[0]) + chip_ref[1]) + chip_ref[2]

    vm = pl.BlockSpec(memory_space=pltpu.VMEM)
    return pl.pallas_call(
        body, name="small_reduce_scatter", in_specs=[vm], out_specs=vm, out_shape=jax.ShapeDtypeStruct((rq, 128), F32),
        scratch_shapes=[pltpu.VMEM((rh, 128), F32), pltpu.VMEM((3, rq, 128), F32), pltpu.VMEM((rh, 128), F32),
                        pltpu.SemaphoreType.DMA((4,)), pltpu.SemaphoreType.DMA((4,))],
        compiler_params=pltpu.CompilerParams(has_side_effects=True, vmem_limit_bytes=VMEM_LIMIT),
    )(small)


def _eighths_async(collective_id, piece):
    rq = piece.shape[0]

    def body(p_ref, o_ref, *sems):
        x, y, c, _ = _place()
        barrier = pltpu.get_barrier_semaphore()
        for fx, fy, fc in _FLIPS:
            pl.semaphore_signal(barrier, inc=1, device_id=(x ^ fx, y ^ fy, c ^ fc), device_id_type=MESH)
        pl.semaphore_wait(barrier, 7)
        mine = o_ref.at[pl.ds((4 * c + 2 * x + y) * rq, rq)]
        cps = [_remote(p_ref, mine, sems[2 * f], sems[2 * f + 1], (x ^ fx, y ^ fy, c ^ fc)) for f, (fx, fy, fc) in enumerate(_FLIPS)]
        for cp in cps:
            cp.start()
        for cp in cps:
            cp.wait()

    moved = 2 * 7 * piece.size * 4
    (out,) = pl.kernel(body, out_type=[jax.ShapeDtypeStruct((8 * rq, 128), F32)],
                       mesh=plsc.ScalarSubcoreMesh(axis_name="seq", num_cores=1), name="small_eighths",
                       scratch_types=(pltpu.SemaphoreType.DMA,) * 14,
                       cost_estimate=pl.CostEstimate(flops=0, transcendentals=0, bytes_accessed=moved, remote_bytes_transferred=moved),
                       compiler_params=pltpu.CompilerParams(collective_id=collective_id))(piece)
    return out


def _small_allreduce(small):
    r = small.shape[0]
    rh, rq = r // 2, r // 8
    flips = _FLIPS

    def body(g_ref, out_ref, pair_ref, chip_ref, s1_ref, ssem, rsem):
        x, y, c, chips = _place()
        cp = _remote(g_ref.at[pl.ds((1 - c) * rh, rh)], pair_ref, ssem.at[0], rsem.at[0], (x, y, 1 - c))
        cp.start()
        cp.wait()
        s1_ref[...] = g_ref[pl.ds(pl.multiple_of(c * rh, 8), rh), :] + pair_ref[...]
        cps = [_remote(s1_ref.at[pl.ds((2 * qx + qy) * rq, rq)], chip_ref.at[q], ssem.at[1 + q], rsem.at[1 + q], (qx, qy, c))
               for q, (qx, qy) in enumerate(chips)]
        for cp in cps:
            cp.start()
        for cp in cps:
            cp.wait()
        mine = out_ref.at[pl.ds(pl.multiple_of((4 * c + 2 * x + y) * rq, 8), rq)]
        mine[...] = ((s1_ref[pl.ds(pl.multiple_of((2 * x + y) * rq, 8), rq), :] + chip_ref[0]) + chip_ref[1]) + chip_ref[2]
        cps = [_remote(mine, mine, ssem.at[4 + f], rsem.at[4 + f], (x ^ fx, y ^ fy, c ^ fc)) for f, (fx, fy, fc) in enumerate(flips)]
        for cp in cps:
            cp.start()
        for cp in cps:
            cp.wait()

    vm = pl.BlockSpec(memory_space=pltpu.VMEM)
    return pl.pallas_call(
        body, name="small_allreduce", in_specs=[vm], out_specs=vm, out_shape=jax.ShapeDtypeStruct(small.shape, F32),
        scratch_shapes=[pltpu.VMEM((rh, 128), F32), pltpu.VMEM((3, rq, 128), F32), pltpu.VMEM((rh, 128), F32),
                        pltpu.SemaphoreType.DMA((11,)), pltpu.SemaphoreType.DMA((11,))],
        compiler_params=pltpu.CompilerParams(has_side_effects=True, vmem_limit_bytes=VMEM_LIMIT),
    )(small)


def _to_wire(g):
    _, a_dim, b_dim = g.shape
    tr = min(256, a_dim)

    def body(g_ref, o_ref, done_ref):
        o_ref[...] = g_ref[...].astype(o_ref.dtype)
        done_ref[...] = jnp.zeros_like(done_ref)

    blk = pl.BlockSpec((1, tr, b_dim), lambda i: (0, i, 0))
    return pl.pallas_call(
        body, name="to_wire", grid=(a_dim // tr,), in_specs=[blk], out_specs=[blk, pl.BlockSpec((8, 128), lambda i: (0, 0))],
        out_shape=[jax.ShapeDtypeStruct(g.shape, WIRE_DTYPE), jax.ShapeDtypeStruct((8, 128), F32)],
        compiler_params=_cparams(("arbitrary",)),
    )(g)


def _chip_sum(owns, gots, ax, chip, after=None):
    n_layers = len(owns)
    tail = [] if after is None else [after]
    _, _, a_dim, b_dim = gots[0].shape
    tr = min(256, a_dim)
    per = a_dim // tr

    def own_spec(layer):
        if ax == 2:
            return pl.BlockSpec((1, tr, b_dim), lambda l, i, k: (0, jnp.where(l == layer, i, 0), k[0]))
        return pl.BlockSpec((1, tr, b_dim), lambda l, i, k: (0, k[0] * per + jnp.where(l == layer, i, 0), 0))

    def got_spec(layer):
        return pl.BlockSpec((3, 1, tr, b_dim), lambda l, i, k: (0, 0, jnp.where(l == layer, i, 0), 0))

    def body(k_ref, *refs):
        s_ref = refs[-1]
        for layer in range(n_layers):
            @pl.when(pl.program_id(0) == layer)
            def _(own_ref=refs[layer], got_ref=refs[n_layers + layer]):
                s_ref[...] = ((own_ref[...].astype(F32) + got_ref[0].astype(F32)) + got_ref[1].astype(F32)) + got_ref[2].astype(F32)

    return pl.pallas_call(
        body, name="chip_sum",
        grid_spec=pltpu.PrefetchScalarGridSpec(
            num_scalar_prefetch=1, grid=(n_layers, per),
            in_specs=[own_spec(l) for l in range(n_layers)] + [got_spec(l) for l in range(n_layers)]
            + [pl.BlockSpec((8, 128), lambda l, i, k: (0, 0))] * len(tail),
            out_specs=pl.BlockSpec((1, tr, b_dim), lambda l, i, k: (l, i, 0))),
        out_shape=jax.ShapeDtypeStruct((n_layers, a_dim, b_dim), F32),
        compiler_params=_cparams(("arbitrary", "arbitrary")),
    )(chip, *owns, *gots, *tail)


def _adam_math(w, g, m, v):
    m = ADAM_B1 * m + (1.0 - ADAM_B1) * g
    v = ADAM_B2 * v + (1.0 - ADAM_B2) * (g * g)
    m_hat = m / (1.0 - ADAM_B1 ** ADAM_STEP)
    v_hat = v / (1.0 - ADAM_B2 ** ADAM_STEP)
    return -ADAM_LR * (m_hat / (jnp.sqrt(v_hat) + ADAM_EPS) + ADAM_WD * w), m, v


def _adamw_big(w, g_mine, g_other, m, v):
    l_dim, a_dim, b_dim = w.shape
    tr = min(256, a_dim)

    def body(w_ref, g1_ref, g2_ref, m_ref, v_ref, g_ref, d_ref, nm_ref, nv_ref):
        g = g1_ref[...] + g2_ref[...]
        g_ref[...] = g
        d_ref[...], nm_ref[...], nv_ref[...] = _adam_math(w_ref[...], g, m_ref[...], v_ref[...])

    blk = pl.BlockSpec((1, tr, b_dim), lambda l, i: (l, i, 0))
    return pl.pallas_call(
        body, name="adamw_big", grid=(l_dim, a_dim // tr), in_specs=[blk] * 5, out_specs=[blk] * 4,
        out_shape=[jax.ShapeDtypeStruct(w.shape, F32)] * 4,
        compiler_params=_cparams(("parallel", "parallel")),
    )(w, g_mine, g_other, m, v)


def _adamw_small(ws, gs, ms, vs):
    n = len(ws)

    def body(*refs):
        for i in range(n):
            w_ref, g_ref, m_ref, v_ref = refs[i], refs[n + i], refs[2 * n + i], refs[3 * n + i]
            d_ref, nm_ref, nv_ref = refs[4 * n + i], refs[5 * n + i], refs[6 * n + i]
            d_ref[...], nm_ref[...], nv_ref[...] = _adam_math(w_ref[...], g_ref[...], m_ref[...], v_ref[...])

    shapes = [jax.ShapeDtypeStruct(w.shape, F32) for w in ws]
    outs = pl.pallas_call(body, name="adamw_small", out_shape=shapes * 3,
                          compiler_params=_cparams())(*ws, *gs, *ms, *vs)
    return outs[:n], outs[n:2 * n], outs[2 * n:]


def _pack(arrs, row_mult):
    parts = []
    for a in arrs:
        flat = a.reshape(-1)
        pad = (-flat.shape[0]) % 1024
        parts.append(jnp.pad(flat, (0, pad)).reshape(-1, 128))
    buf = jnp.concatenate(parts, axis=0)
    pad = (-buf.shape[0]) % row_mult
    return jnp.pad(buf, ((0, pad), (0, 0)))


def _unpack(buf, shapes):
    out, row = [], 0
    for sh in shapes:
        n = math.prod(sh)
        rows = -(-n // 1024) * 8
        out.append(buf[row:row + rows].reshape(-1)[:n].reshape(sh))
        row += rows
    return out


_NAMES = ['w_in_e', 'b_in_e', 'conv_a_w', 'conv_a_b', 'ln_a_g', 'ln_a_b', 'ln_v_g', 'ln_v_b', 'w_s', 'b_s', 'w_out_e', 'b_out_e',
          'w_in_o', 'b_in_o', 'w_pool', 'pool_scale', 'conv_d_w', 'w_out_o', 'b_out_o', 'ln_g', 'ln_b', 'w_ple', 'w_ple_gate',
          'b_ple_gate']
_BIG = ['w_in_e', 'w_out_e', 'w_in_o', 'w_out_o', 'w_ple', 'w_ple_gate']
_SMALL_SHARDED = ['conv_a_w', 'b_in_o', 'pool_scale', 'conv_d_w', 'b_out_o']


def kernel(x, p, w_in_e, b_in_e, conv_a_w, conv_a_b, ln_a_g, ln_a_b, ln_v_g, ln_v_b, w_s, b_s, w_out_e, b_out_e, w_in_o, b_in_o, w_pool, pool_scale, conv_d_w, w_out_o, b_out_o, ln_g, ln_b, w_ple, w_ple_gate, b_ple_gate, loss_target, m_w_in_e, m_b_in_e, m_conv_a_w, m_conv_a_b, m_ln_a_g, m_ln_a_b, m_ln_v_g, m_ln_v_b, m_w_s, m_b_s, m_w_out_e, m_b_out_e, m_w_in_o, m_b_in_o, m_w_pool, m_pool_scale, m_conv_d_w, m_w_out_o, m_b_out_o, m_ln_g, m_ln_b, m_w_ple, m_w_ple_gate, m_b_ple_gate, v_w_in_e, v_b_in_e, v_conv_a_w, v_conv_a_b, v_ln_a_g, v_ln_a_b, v_ln_v_g, v_ln_v_b, v_w_s, v_b_s, v_w_out_e, v_b_out_e, v_w_in_o, v_b_in_o, v_w_pool, v_pool_scale, v_conv_d_w, v_w_out_o, v_b_out_o, v_ln_g, v_ln_b, v_w_ple, v_w_ple_gate, v_b_ple_gate):
    args = locals()
    wts = {n: args[n] for n in _NAMES}
    mom = {n: args["m_" + n] for n in _NAMES}
    var = {n: args["v_" + n] for n in _NAMES}
    bsz, s_len, d = x.shape
    t = bsz * s_len
    cx, cy, cc = lax.axis_index("x"), lax.axis_index("y"), lax.axis_index("c")
    chip = (2 * cx + cy).astype(jnp.int32).reshape(1)

    def placed(name, ax, layer):
        return _place_shard(wts[name], ax, chip, layer=layer)

    sv = _pack([wts[n] for n in _SMALL_SHARDED], 8)
    first_refs = _gather_first_async(DEPTH, placed('w_in_e', 2, 0), _place_shard(sv[None], 1, chip, F32))
    layer_refs = []
    for i in range(DEPTH):
        sfx = '_e' if i % 2 == 0 else '_o'
        nxt_sfx = '_o' if i % 2 == 0 else '_e'
        items = ([('w_out' + sfx, 1, i // 2), ('w_ple_gate', 1, i), ('w_ple', 2, i)]
                 + ([('w_in' + nxt_sfx, 2, (i + 1) // 2)] if i + 1 < DEPTH else []))
        layer_refs.append(_gather_async("gather_layer%d" % i, i, [placed(*it) for it in items], [it[1] for it in items]))
    fw = {}
    w_in_first = first_refs[0][...]
    sv_all = first_refs[1][...].reshape((4,) + sv.shape)
    small_parts = [_unpack(sv_all[k], [wts[n].shape for n in _SMALL_SHARDED]) for k in range(4)]
    for i, n in enumerate(_SMALL_SHARDED):
        fw[n] = jnp.concatenate([small_parts[k][i] for k in range(4)], axis=-1)
    for n in _NAMES:
        fw.setdefault(n, wts[n])

    def row8(rows, width):
        rows = [r.reshape(1, width) for r in rows]
        return jnp.concatenate(rows + [jnp.zeros((8 - len(rows), width), F32)], axis=0)

    x2 = x.reshape(t, d)
    saved = []
    for i in range(DEPTH):
        j = i // 2
        even = i % 2 == 0
        b_in, b_out = (fw['b_in_e'], fw['b_out_e']) if even else (fw['b_in_o'], fw['b_out_o'])
        if i == 0:
            w_in = w_in_first[0]
            z = _in_proj(x2, w_in, b_in[j].reshape(1, N_COLS))
        z3 = z.reshape(bsz, s_len, N_COLS)
        if even:
            cw = jnp.concatenate([fw['conv_a_w'][j], jnp.zeros((1, W_BR), F32)], axis=0)
            mvec = row8([fw['conv_a_b'][j], fw['ln_a_g'][j], fw['ln_a_b'][j], fw['ln_v_g'][j], fw['ln_v_b'][j]], W_BR)
            bsf = jnp.repeat(fw['b_s'][j].T, W_BR // 8, axis=1)
            y3, a13 = _even_fwd(z3, cw, mvec, fw['w_s'][j], bsf)
            mix = (a13, cw, mvec, fw['w_s'][j], jnp.swapaxes(fw['w_s'][j], 1, 2), bsf)
        else:
            mvec = row8([fw['pool_scale'][j]] + [fw['conv_d_w'][j][k] for k in range(3)], W_BR)
            mix = (fw['w_pool'][j].astype(MM_DTYPE), mvec)
            y3 = _odd_fwd(z3, mix[0], mvec)
        pvec = row8([b_out[j], fw['ln_g'][i], fw['ln_b'][i], fw['b_ple_gate'][i]], d)
        post_w = (layer_refs[i][0][...][0], layer_refs[i][1][...][0], layer_refs[i][2][...][0], pvec)
        p2 = (p.reshape(DEPTH, t, D_PLE), i)
        y2 = y3.reshape(t, 2 * W_BR)
        saved.append((x2, z3, y2, None, None, p2, w_in, mix, post_w))
        if i < DEPTH - 1:
            w_in = layer_refs[i][3][...][0]
            b_next = (fw['b_in_o'] if even else fw['b_in_e'])[(i + 1) // 2].reshape(1, N_COLS)
            x2, r2, gate2, z = _post_fwd(y2, x2, p2, *post_w, nxt=(w_in, b_next))
        else:
            dx, r2, gate2, sq = _post_fwd(y2, x2, p2, *post_w, tgt=loss_target.reshape(t, d))
        saved[-1] = saved[-1][:3] + (r2, gate2) + saved[-1][5:]

    gr = {n: [None] * wts[n].shape[0] for n in _NAMES}
    prev_lands = []
    for i in reversed(range(DEPTH)):
        j = i // 2
        even = i % 2 == 0
        x_in, z3, y2, r2, gate2, p2, w_in, mix, post_w = saved[i]
        dxr, dy, dwo, dwg, dwp, dpv = _post_bwd(dx, r2, gate2, p2, y2, *post_w)
        dy3 = dy.reshape(bsz, s_len, 2 * W_BR)
        sfx = '_e' if even else '_o'
        gr['w_out' + sfx][j], gr['b_out' + sfx][j] = dwo, dpv[0]
        gr['w_ple_gate'][i], gr['w_ple'][i] = dwg, dwp
        gr['ln_g'][i], gr['ln_b'][i], gr['b_ple_gate'][i] = dpv[1], dpv[2], dpv[3]
        if even:
            dz3, dbin, dcw, dmv, dws, dbsf = _even_bwd(z3, dy3, *mix)
            gr['conv_a_w'][j], gr['conv_a_b'][j] = dcw[:KA], dmv[0]
            gr['ln_a_g'][j], gr['ln_a_b'][j], gr['ln_v_g'][j], gr['ln_v_b'][j] = dmv[1], dmv[2], dmv[3], dmv[4]
            gr['w_s'][j] = dws
            gr['b_s'][j] = jnp.sum(dbsf.reshape(GBLK, 8, W_BR // 8), axis=2).T
        else:
            dz3, dbin, dwpool, dmv = _odd_bwd(z3, dy3, *mix)
            gr['w_pool'][j], gr['pool_scale'][j], gr['conv_d_w'][j] = dwpool, dmv[0], dmv[1:4]
        gr['b_in' + sfx][j] = dbin[0]
        dz2 = dz3.reshape(t, N_COLS)
        dwi, dw_done = _in_proj_bwd_dw(x_in, dz2)
        post_items = [('w_out' + sfx, j, dwo[None], 1), ('w_ple_gate', i, dwg[None], 1), ('w_ple', i, dwp[None], 2)]
        in_item = ('w_in' + sfx, j, dwi[None], 2)
        batches = [post_items + [in_item]] if i else [post_items, [in_item]]
        for bi, items in enumerate(batches):
            sent = [it[2] for it in items]
            if i == 0:
                cast = [_to_wire(g) for g in sent]
                sent = [w for w, _ in cast]
                if bi == 1:
                    dw_done = cast[0][1]
            lands = _scatter_async("scatter_layer%d_%d" % (i, bi), DEPTH + 1 + 2 * i + bi, sent, [it[3] for it in items],
                                   prev_lands[:1] + ([dbin] if (i == 0 and bi == 0) else []))
            prev_lands = list(lands)
            for it, land in zip(items, lands):
                gr[it[0]][it[1]] = (it[2], land, it[3])
        dx = _in_proj_bwd_dx(dxr, dz2, w_in, dw_done)
    grad_x = dx.reshape(bsz, s_len, d)

    small_names = [n for n in _NAMES if n not in _BIG]
    g_small_full = [jnp.stack(gr[n]) for n in small_names] + [sq]
    piece = _small_reduce_scatter(_pack(g_small_full, 64))
    eighths = _eighths_async(3 * DEPTH + 3, piece)
    big_order = _BIG[1:] + _BIG[:1]
    sums = [_chip_sum([g[0] for g in gr[n]], [g[1] for g in gr[n]], gr[n][0][2], chip, dx if n == big_order[-1] else None)
            for n in big_order]
    others = (list(_pair_swap_async("pair_swap_a", 3 * DEPTH + 1, sums[:-1]))
              + list(_pair_swap_async("pair_swap_b", 3 * DEPTH + 2, sums[-1:])))
    small_all = lax.dynamic_update_slice(eighths, piece, ((4 * cc + 2 * cx + cy) * piece.shape[0], 0))
    *g_small, sq_all = _unpack(small_all, [g.shape for g in g_small_full])
    loss = 0.5 * jnp.sum(sq_all) / d
    grads = {}
    for n, g in zip(small_names, g_small):
        if n in _SMALL_SHARDED:
            w = wts[n].shape[-1]
            g = lax.dynamic_slice_in_dim(g, (2 * cx + cy) * w, w, axis=g.ndim - 1)
        grads[n] = g

    delta, new_m, new_v = {}, {}, {}
    for n, mine, other in zip(big_order, sums, others):
        grads[n], delta[n], new_m[n], new_v[n] = _adamw_big(wts[n], mine, other, mom[n], var[n])
    ds, ms, vs = _adamw_small([wts[n] for n in small_names], [grads[n] for n in small_names],
                              [mom[n] for n in small_names], [var[n] for n in small_names])
    for n, a, b, c_ in zip(small_names, ds, ms, vs):
        delta[n], new_m[n], new_v[n] = a, b, c_

    return (loss, grad_x, *[grads[n] for n in _NAMES], *[delta[n] for n in _NAMES],
            *[new_m[n] for n in _NAMES], *[new_v[n] for n in _NAMES])
```

```python
import functools
import math

import jax
import jax.numpy as jnp
from jax import lax
from jax.experimental import pallas as pl
from jax.experimental.pallas import tpu as pltpu
from jax.experimental.pallas import tpu_sc as plsc

F32 = jnp.float32
MM_DTYPE = jnp.bfloat16
WIRE_DTYPE = jnp.bfloat16
SEQ_TILE = 512
ROW_TILE = 512
BWD_ROW_TILE = 512
HALO = 32
CONV_CHUNK = 64
DW_CHUNK = 32
ROW_CHUNK = 64
GBLK = 128
VMEM_LIMIT = 56 * 1024 * 1024

D_MODEL = 1024
W_BR = 512
N_COLS = 6 * W_BR
D_PLE = 256
KA = 31
DEPTH = 4
POOL_WINDOWS = (2, 4, 8, 16)
ALPHA = (2.0 * DEPTH) ** 0.25
LN_EPS = 1e-5
GELU_C = math.sqrt(2.0 / math.pi)

ADAM_LR, ADAM_B1, ADAM_B2, ADAM_EPS, ADAM_WD, ADAM_STEP = 0.001, 0.9, 0.999, 1e-08, 0.01, 10

MESH = pl.DeviceIdType.MESH
ANY = pl.BlockSpec(memory_space=pl.ANY)


def _cparams(sem=None):
    return pltpu.CompilerParams(dimension_semantics=sem, vmem_limit_bytes=VMEM_LIMIT)


def _sigmoid(x):
    return 1.0 / (1.0 + jnp.exp(-x))


def _silu(x):
    return x * _sigmoid(x)


def _silu_grad(x):
    s = _sigmoid(x)
    return x * s, s * (1.0 + x * (1.0 - s))


def _gelu(x):
    return 0.5 * x * (1.0 + jnp.tanh(GELU_C * (x + 0.044715 * (x * x * x))))


def _gelu_grad(x):
    x2 = x * x
    th = jnp.tanh(GELU_C * (x + 0.044715 * (x * x2)))
    return 0.5 * x * (1.0 + th), 0.5 * (1.0 + th) + 0.5 * x * (1.0 - th * th) * (GELU_C * (1.0 + 3.0 * 0.044715 * x2))


def _ln_stats(x):
    mu = jnp.mean(x, axis=-1, keepdims=True)
    d = x - mu
    var = jnp.mean(d * d, axis=-1, keepdims=True)
    rs = lax.rsqrt(var + LN_EPS)
    return d * rs, rs


def _ln_bwd(dxh, xh, rs):
    return rs * (dxh - jnp.mean(dxh, axis=-1, keepdims=True) - xh * jnp.mean(dxh * xh, axis=-1, keepdims=True))


def _mm(a):
    return a.astype(MM_DTYPE)


def _dot(a, b):
    return jnp.dot(_mm(a), _mm(b), preferred_element_type=F32)


def _dot_nt(a, b):
    return lax.dot_general(_mm(a), _mm(b), (((1,), (1,)), ((), ())), preferred_element_type=F32)


def _dot_tn(a, b):
    return lax.dot_general(_mm(a), _mm(b), (((0,), (0,)), ((), ())), preferred_element_type=F32)


def _rowsum(x):
    return jnp.sum(x, axis=0, keepdims=True)


def _in_proj(x2, w, b):
    t, d = x2.shape
    n = w.shape[1]
    tm = min(ROW_TILE, t)
    nc = 768

    def body(x_ref, w_ref, b_ref, z_ref):
        xb = _mm(x_ref[...])
        for j in range(n // nc):
            cs = slice(j * nc, (j + 1) * nc)
            z_ref[:, cs] = jnp.dot(xb, w_ref[:, cs], preferred_element_type=F32) + b_ref[:, cs]

    return pl.pallas_call(
        body, name="in_proj", grid=(t // tm,),
        in_specs=[pl.BlockSpec((tm, d), lambda i: (i, 0)), pl.BlockSpec((d, n), lambda i: (0, 0)),
                  pl.BlockSpec((1, n), lambda i: (0, 0))],
        out_specs=pl.BlockSpec((tm, n), lambda i: (i, 0)),
        out_shape=jax.ShapeDtypeStruct((t, n), F32),
        compiler_params=_cparams(("parallel",)),
    )(x2, w, b)


def _in_proj_bwd_dx(dxr, dz, w, after):
    t, d = dxr.shape
    n = w.shape[1]
    tm = min(ROW_TILE, t)

    def body(dxr_ref, dz_ref, w_ref, after_ref, dx_ref):
        dx_ref[...] = dxr_ref[...] + _dot_nt(dz_ref[...], w_ref[...])

    return pl.pallas_call(
        body, name="in_proj_bwd_dx", grid=(t // tm,),
        in_specs=[pl.BlockSpec((tm, d), lambda i: (i, 0)), pl.BlockSpec((tm, n), lambda i: (i, 0)),
                  pl.BlockSpec((d, n), lambda i: (0, 0)), pl.BlockSpec((8, 128), lambda i: (0, 0))],
        out_specs=pl.BlockSpec((tm, d), lambda i: (i, 0)),
        out_shape=jax.ShapeDtypeStruct((t, d), F32),
        compiler_params=_cparams(("parallel",)),
    )(dxr, dz, w, after)


def _in_proj_bwd_dw(x2, dz, wire=False):
    t, d = x2.shape
    n = dz.shape[1]
    tm = min(ROW_TILE, t)
    nc = 768
    steps = t // tm

    def body(x_ref, dz_ref, dw_ref, done_ref, *wire_ref):
        @pl.when(pl.program_id(0) == 0)
        def _():
            dw_ref[...] = jnp.zeros_like(dw_ref)
            done_ref[...] = jnp.zeros_like(done_ref)
        xb = _mm(x_ref[...])
        for j in range(n // nc):
            cs = slice(j * nc, (j + 1) * nc)
            dw_ref[:, cs] += _dot_tn(xb, dz_ref[:, cs])
        if wire:
            @pl.when(pl.program_id(0) == steps - 1)
            def _():
                wire_ref[0][...] = dw_ref[...].astype(WIRE_DTYPE)

    full = pl.BlockSpec((d, n), lambda i: (0, 0), pipeline_mode=pl.Buffered(1))
    return pl.pallas_call(
        body, name="in_proj_bwd_dw", grid=(steps,),
        in_specs=[pl.BlockSpec((tm, d), lambda i: (i, 0)), pl.BlockSpec((tm, n), lambda i: (i, 0))],
        out_specs=[full, pl.BlockSpec((8, 128), lambda i: (0, 0))] + [full] * wire,
        out_shape=[jax.ShapeDtypeStruct((d, n), F32), jax.ShapeDtypeStruct((8, 128), F32)]
        + [jax.ShapeDtypeStruct((d, n), WIRE_DTYPE)] * wire,
        compiler_params=_cparams(("arbitrary",)),
    )(x2, dz)


def _halo_specs(ts, s_len, cols, left=True):
    per = ts // HALO
    last = s_len // HALO - 1
    if left:
        return pl.BlockSpec((1, HALO, cols), lambda b, s: (b, jnp.maximum(s * per - 1, 0), 0))
    return pl.BlockSpec((1, HALO, cols), lambda b, s: (b, jnp.minimum((s + 1) * per, last), 0))


def _build_shifts(src_ref, sh_ref, rows):
    for r in range(1, 8):
        sh_ref[r - 1, 0:rows, :] = src_ref[r:r + rows, :]


def _shifted(src_ref, sh_ref, r, start, n):
    if r == 0:
        return src_ref[pl.ds(start, n), :]
    return sh_ref[r - 1, pl.ds(start, n), :]


def _tril_masks():
    ri = lax.broadcasted_iota(jnp.int32, (GBLK, GBLK), 0)
    ci = lax.broadcasted_iota(jnp.int32, (GBLK, GBLK), 1)
    return ri >= ci, ci >= ri


def _spatial(w_ref, keep, vb):
    lane = lax.broadcasted_iota(jnp.int32, (GBLK, GBLK), 1)
    outs = []
    for p in range(4):
        xs = vb[:, p * GBLK:(p + 1) * GBLK]
        r0 = jnp.dot(_mm(jnp.where(keep, w_ref[2 * p], 0.0)), xs, preferred_element_type=F32)
        r1 = jnp.dot(_mm(jnp.where(keep, w_ref[2 * p + 1], 0.0)), xs, preferred_element_type=F32)
        outs.append(jnp.where(lane < 64, r0, r1))
    return jnp.concatenate(outs, axis=1)


def _even_fwd(z3, cw, vec, ws, bsf):
    bsz, s_len, _ = z3.shape
    ts = min(SEQ_TILE, s_len)
    ext_rows = ts + HALO

    def body(z_ref, zl_ref, cw_ref, vec_ref, ws_ref, bsf_ref, y_ref, a1_ref, ext_ref, sh_ref):
        s = pl.program_id(1)
        hl = zl_ref[0]
        a0h = hl[:, 0:W_BR] * _sigmoid(hl[:, W_BR:2 * W_BR])
        ext_ref[0:HALO, :] = jnp.where(s > 0, a0h, 0.0)
        for a, b in _row_chunks(0, ts):
            ext_ref[HALO + a:HALO + b, :] = z_ref[0, a:b, 0:W_BR] * _sigmoid(z_ref[0, a:b, W_BR:2 * W_BR])
        ext_ref[ext_rows:ext_rows + 8, :] = jnp.zeros((8, W_BR), F32)
        _build_shifts(ext_ref, sh_ref, ext_rows)

        def conv_chunk(ci, carry):
            base = pl.multiple_of(ci * CONV_CHUNK, CONV_CHUNK)
            acc = jnp.zeros((CONV_CHUNK, W_BR), F32) + vec_ref[0:1, :]
            for k in range(KA):
                q, r = divmod(2 + k, 8)
                acc = acc + _shifted(ext_ref, sh_ref, r, base + 8 * q, CONV_CHUNK) * cw_ref[k:k + 1, :]
            a1_ref[0, pl.ds(base, CONV_CHUNK), :] = acc
            return carry

        lax.fori_loop(0, ts // CONV_CHUNK, conv_chunk, 0)
        keep, _ = _tril_masks()

        def block(bi, carry):
            rows = pl.ds(pl.multiple_of(bi * GBLK, GBLK), GBLK)
            xh, _ = _ln_stats(a1_ref[0, rows, :])
            a = _silu(xh * vec_ref[1:2, :] + vec_ref[2:3, :]) * _silu(z_ref[0, rows, 2 * W_BR:3 * W_BR])
            y_ref[0, rows, 0:W_BR] = a.astype(y_ref.dtype)
            ua = _gelu(z_ref[0, rows, 3 * W_BR:4 * W_BR])
            vh, _ = _ln_stats(_gelu(z_ref[0, rows, 4 * W_BR:5 * W_BR]))
            vl = vh * vec_ref[3:4, :] + vec_ref[4:5, :]
            sg = _spatial(ws_ref, keep, _mm(vl)) + bsf_ref[...]
            g = ua * sg * _silu(z_ref[0, rows, 5 * W_BR:6 * W_BR])
            y_ref[0, rows, W_BR:2 * W_BR] = g.astype(y_ref.dtype)
            return carry

        lax.fori_loop(0, ts // GBLK, block, 0)

    full = lambda shape: pl.BlockSpec(shape, lambda b, s: (0,) * len(shape))
    return pl.pallas_call(
        body, name="even_fwd", grid=(bsz, s_len // ts),
        in_specs=[pl.BlockSpec((1, ts, N_COLS), lambda b, s: (b, s, 0)), _halo_specs(ts, s_len, 2 * W_BR),
                  full((32, W_BR)), full((8, W_BR)), full((8, GBLK, GBLK)), full((GBLK, W_BR))],
        out_specs=[pl.BlockSpec((1, ts, 2 * W_BR), lambda b, s: (b, s, 0)), pl.BlockSpec((1, ts, W_BR), lambda b, s: (b, s, 0))],
        out_shape=[jax.ShapeDtypeStruct((bsz, s_len, 2 * W_BR), MM_DTYPE), jax.ShapeDtypeStruct((bsz, s_len, W_BR), F32)],
        scratch_shapes=[pltpu.VMEM((ext_rows + 8, W_BR), F32), pltpu.VMEM((7, ext_rows, W_BR), F32)],
        compiler_params=_cparams(("parallel", "parallel")),
    )(z3, z3, cw, vec, ws, bsf)


def _even_bwd(z3, dy3, a13, cw, vec, ws, wst, bsf):
    bsz, s_len, _ = z3.shape
    ts = min(SEQ_TILE, s_len)
    n_s = s_len // ts
    ext_rows = ts + 2 * HALO
    a_rows = ts + HALO

    def body(z_ref, zl_ref, zr_ref, dy_ref, dyr_ref, a1_ref, a1r_ref, cw_ref, vec_ref, ws_ref, wst_ref, bsf_ref,
             dz_ref, dbin_ref, dcw_ref, dvec_ref, dws_ref, dbsf_ref,
             ext_ref, sh_ref, ag_ref, dya_ref, d_ref, accw_ref):
        b = pl.program_id(0)
        s = pl.program_id(1)

        @pl.when((b == 0) & (s == 0))
        def _():
            dbin_ref[...] = jnp.zeros_like(dbin_ref)
            dcw_ref[...] = jnp.zeros_like(dcw_ref)
            dvec_ref[...] = jnp.zeros_like(dvec_ref)
            dws_ref[...] = jnp.zeros_like(dws_ref)
            dbsf_ref[...] = jnp.zeros_like(dbsf_ref)

        has_right = s < n_s - 1
        hl = zl_ref[0]
        hr = zr_ref[0]
        ext_ref[0:HALO, :] = jnp.where(s > 0, hl[:, 0:W_BR] * _sigmoid(hl[:, W_BR:2 * W_BR]), 0.0)
        for a, b in _row_chunks(0, ts):
            ext_ref[HALO + a:HALO + b, :] = z_ref[0, a:b, 0:W_BR] * _sigmoid(z_ref[0, a:b, W_BR:2 * W_BR])
            ag_ref[a:b, :] = z_ref[0, a:b, 2 * W_BR:3 * W_BR]
            dya_ref[a:b, :] = dy_ref[0, a:b, 0:W_BR]
        ext_ref[HALO + ts:ext_rows, :] = hr[:, 0:W_BR] * _sigmoid(hr[:, W_BR:2 * W_BR])
        ext_ref[ext_rows:ext_rows + 8, :] = jnp.zeros((8, W_BR), F32)
        ag_ref[ts:a_rows, :] = hr[:, 2 * W_BR:3 * W_BR]
        dya_ref[ts:a_rows, :] = dyr_ref[0][:, 0:W_BR]
        _build_shifts(ext_ref, sh_ref, ext_rows)

        def a_chunk(base, n, main):
            rows = pl.ds(base, n)
            xh, rs = _ln_stats(a1_ref[0, rows, :] if main else a1r_ref[0])
            ln = xh * vec_ref[1:2, :] + vec_ref[2:3, :]
            sl, dsl = _silu_grad(ln)
            sgt, dsgt = _silu_grad(ag_ref[rows, :])
            dya = dya_ref[rows, :]
            dln = dya * sgt * dsl
            da1 = _ln_bwd(dln * vec_ref[1:2, :], xh, rs)
            if main:
                d_ref[rows, :] = da1
                dag = dya * sl * dsgt
                dz_ref[0, rows, 2 * W_BR:3 * W_BR] = dag.astype(dz_ref.dtype)
                dbin_ref[0:1, 2 * W_BR:3 * W_BR] += _rowsum(dag)
                dvec_ref[0:1, :] += _rowsum(da1)
                dvec_ref[1:2, :] += _rowsum(dln * xh)
                dvec_ref[2:3, :] += _rowsum(dln)
            else:
                d_ref[rows, :] = jnp.where(has_right, da1, 0.0)

        def a_main(ci, carry):
            a_chunk(pl.multiple_of(ci * GBLK, GBLK), GBLK, True)
            return carry

        lax.fori_loop(0, ts // GBLK, a_main, 0)
        a_chunk(ts, HALO, False)
        d_ref[a_rows:a_rows + 8, :] = jnp.zeros((8, W_BR), F32)

        accw_ref[...] = jnp.zeros_like(accw_ref)

        def dw_chunk(ci, carry):
            base = pl.multiple_of(ci * DW_CHUNK, DW_CHUNK)
            d = d_ref[pl.ds(base, DW_CHUNK), :]
            for k in range(KA):
                q, r = divmod(2 + k, 8)
                prod = d * _shifted(ext_ref, sh_ref, r, base + 8 * q, DW_CHUNK)
                accw_ref[k] += jnp.sum(prod.reshape(DW_CHUNK // 8, 8, W_BR), axis=0)
            return carry

        lax.fori_loop(0, ts // DW_CHUNK, dw_chunk, 0)
        dcw_ref[...] += jnp.sum(accw_ref[...], axis=1)

        _build_shifts(d_ref, sh_ref, a_rows)

        def dx_chunk(ci, carry):
            base = pl.multiple_of(ci * CONV_CHUNK, CONV_CHUNK)
            rows = pl.ds(base, CONV_CHUNK)
            acc = jnp.zeros((CONV_CHUNK, W_BR), F32)
            for m in range(KA):
                q, r = divmod(m, 8)
                acc = acc + _shifted(d_ref, sh_ref, r, base + 8 * q, CONV_CHUNK) * cw_ref[KA - 1 - m:KA - m, :]
            aval = z_ref[0, rows, 0:W_BR]
            sg = _sigmoid(z_ref[0, rows, W_BR:2 * W_BR])
            dval = acc * sg
            dglu = acc * aval * sg * (1.0 - sg)
            dz_ref[0, rows, 0:W_BR] = dval.astype(dz_ref.dtype)
            dz_ref[0, rows, W_BR:2 * W_BR] = dglu.astype(dz_ref.dtype)
            dbin_ref[0:1, 0:W_BR] += _rowsum(dval)
            dbin_ref[0:1, W_BR:2 * W_BR] += _rowsum(dglu)
            return carry

        lax.fori_loop(0, ts // CONV_CHUNK, dx_chunk, 0)

        keep, keep_t = _tril_masks()
        lane = lax.broadcasted_iota(jnp.int32, (GBLK, GBLK), 1)

        def block(bi, carry):
            rows = pl.ds(pl.multiple_of(bi * GBLK, GBLK), GBLK)
            ua, dua = _gelu_grad(z_ref[0, rows, 3 * W_BR:4 * W_BR])
            va, dva = _gelu_grad(z_ref[0, rows, 4 * W_BR:5 * W_BR])
            sgt, dsgt = _silu_grad(z_ref[0, rows, 5 * W_BR:6 * W_BR])
            vh, rs = _ln_stats(va)
            vlb = _mm(vh * vec_ref[3:4, :] + vec_ref[4:5, :])
            sg = _spatial(ws_ref, keep, vlb) + bsf_ref[...]
            dyg = dy_ref[0, rows, W_BR:2 * W_BR]
            du = dyg * sg * sgt * dua
            dsg = dyg * ua * sgt
            dgg = dyg * ua * sg * dsgt
            dvl = _spatial(wst_ref, keep_t, _mm(dsg))
            for p in range(4):
                dsp = dsg[:, p * GBLK:(p + 1) * GBLK]
                vlp = vlb[:, p * GBLK:(p + 1) * GBLK]
                dws_ref[2 * p] += jnp.where(keep, _dot_nt(jnp.where(lane < 64, dsp, 0.0), vlp), 0.0)
                dws_ref[2 * p + 1] += jnp.where(keep, _dot_nt(jnp.where(lane >= 64, dsp, 0.0), vlp), 0.0)
            dbsf_ref[...] += dsg
            dvec_ref[3:4, :] += _rowsum(dvl * vh)
            dvec_ref[4:5, :] += _rowsum(dvl)
            dv = _ln_bwd(dvl * vec_ref[3:4, :], vh, rs) * dva
            dz_ref[0, rows, 3 * W_BR:4 * W_BR] = du.astype(dz_ref.dtype)
            dz_ref[0, rows, 4 * W_BR:5 * W_BR] = dv.astype(dz_ref.dtype)
            dz_ref[0, rows, 5 * W_BR:6 * W_BR] = dgg.astype(dz_ref.dtype)
            dbin_ref[0:1, 3 * W_BR:4 * W_BR] += _rowsum(du)
            dbin_ref[0:1, 4 * W_BR:5 * W_BR] += _rowsum(dv)
            dbin_ref[0:1, 5 * W_BR:6 * W_BR] += _rowsum(dgg)
            return carry

        lax.fori_loop(0, ts // GBLK, block, 0)

    full = lambda shape: pl.BlockSpec(shape, lambda b, s: (0,) * len(shape))
    acc_shapes = [(1, N_COLS), (32, W_BR), (8, W_BR), (8, GBLK, GBLK), (GBLK, W_BR)]
    return pl.pallas_call(
        body, name="even_bwd", grid=(bsz, n_s),
        in_specs=[pl.BlockSpec((1, ts, N_COLS), lambda b, s: (b, s, 0)),
                  _halo_specs(ts, s_len, N_COLS, True), _halo_specs(ts, s_len, N_COLS, False),
                  pl.BlockSpec((1, ts, 2 * W_BR), lambda b, s: (b, s, 0)), _halo_specs(ts, s_len, 2 * W_BR, False),
                  pl.BlockSpec((1, ts, W_BR), lambda b, s: (b, s, 0)), _halo_specs(ts, s_len, W_BR, False),
                  full((32, W_BR)), full((8, W_BR)), full((8, GBLK, GBLK)), full((8, GBLK, GBLK)), full((GBLK, W_BR))],
        out_specs=[pl.BlockSpec((1, ts, N_COLS), lambda b, s: (b, s, 0))] + [full(sh) for sh in acc_shapes],
        out_shape=[jax.ShapeDtypeStruct((bsz, s_len, N_COLS), MM_DTYPE)] + [jax.ShapeDtypeStruct(sh, F32) for sh in acc_shapes],
        scratch_shapes=[pltpu.VMEM((ext_rows + 8, W_BR), F32), pltpu.VMEM((7, ext_rows, W_BR), F32),
                        pltpu.VMEM((a_rows, W_BR), F32), pltpu.VMEM((a_rows, W_BR), F32),
                        pltpu.VMEM((a_rows + 8, W_BR), F32), pltpu.VMEM((32, 8, W_BR), F32)],
        compiler_params=_cparams(("arbitrary", "arbitrary")),
    )(z3, z3, z3, dy3, dy3, a13, a13, cw, vec, ws, wst, bsf)


def _row_chunks(lo, hi):
    return [(a, min(a + ROW_CHUNK, hi)) for a in range(lo, hi, ROW_CHUNK)]


def _pool_stages(e_refs, rows):
    e0, e1, e2, e3, e4 = e_refs
    for a, b in _row_chunks(8, rows):
        e1[a:b, :] = e0[a:b, :] + e0[a - 1:b - 1, :]
    for a, b in _row_chunks(16, rows):
        e2[a:b, GBLK:] = e1[a:b, GBLK:] + e1[a - 2:b - 2, GBLK:]
    for a, b in _row_chunks(24, rows):
        e3[a:b, 2 * GBLK:] = e2[a:b, 2 * GBLK:] + e2[a - 4:b - 4, 2 * GBLK:]
    for a, b in _row_chunks(32, rows):
        e4[a:b, 3 * GBLK:] = e3[a:b, 3 * GBLK:] + e3[a - 8:b - 8, 3 * GBLK:]


def _pool_counts(start, n):
    pos = (start + 1 + lax.broadcasted_iota(jnp.int32, (n, 1), 0)).astype(F32)
    return [jnp.minimum(pos, float(w)) for w in POOL_WINDOWS]


def _pooled_into(e_refs, pooled_ref, s, ts):
    for a, b in _row_chunks(0, ts):
        cnt = _pool_counts(s * ts + a, b - a)
        for g in range(4):
            cs = slice(g * GBLK, (g + 1) * GBLK)
            pooled_ref[a:b, cs] = e_refs[g + 1][HALO + a:HALO + b, cs] / cnt[g] - e_refs[0][HALO + a:HALO + b, cs]


def _odd_prologue(z_ref, hl, s, ts, e_refs, pooled_ref, dext_ref, ec_ref, vec_ref):
    e0 = e_refs[0]
    e0[0:HALO, :] = jnp.where(s > 0, hl[:, 0:W_BR], 0.0)
    dext_ref[0:HALO, :] = jnp.where(s > 0, hl[:, 2 * W_BR:3 * W_BR] * hl[:, 4 * W_BR:5 * W_BR], 0.0)
    for a, b in _row_chunks(0, ts):
        e0[HALO + a:HALO + b, :] = z_ref[0, a:b, 0:W_BR]
        dext_ref[HALO + a:HALO + b, :] = z_ref[0, a:b, 2 * W_BR:3 * W_BR] * z_ref[0, a:b, 4 * W_BR:5 * W_BR]
    _pool_stages(e_refs, ts + HALO)
    _pooled_into(e_refs, pooled_ref, s, ts)
    for a, b in _row_chunks(0, ts):
        ec_ref[a:b, :] = (vec_ref[1:2, :] * dext_ref[HALO - 2 + a:HALO - 2 + b, :] + vec_ref[2:3, :] * dext_ref[HALO - 1 + a:HALO - 1 + b, :]
                          + vec_ref[3:4, :] * dext_ref[HALO + a:HALO + b, :])


def _odd_fwd(z3, wp, vec):
    bsz, s_len, _ = z3.shape
    ts = min(SEQ_TILE, s_len)
    ext_rows = ts + HALO

    def body(z_ref, zl_ref, wp_ref, vec_ref, y_ref, e0, e1, e2, e3, e4, pooled_ref, dext_ref, ec_ref):
        s = pl.program_id(1)
        _odd_prologue(z_ref, zl_ref[0], s, ts, (e0, e1, e2, e3, e4), pooled_ref, dext_ref, ec_ref, vec_ref)

        def block(bi, carry):
            rows = pl.ds(pl.multiple_of(bi * GBLK, GBLK), GBLK)
            pb = _mm(pooled_ref[rows, :])
            cpre = jnp.concatenate([jnp.dot(pb[:, g * GBLK:(g + 1) * GBLK], wp_ref[g], preferred_element_type=F32)
                                    for g in range(4)], axis=1)
            c = cpre * vec_ref[0:1, :] * _silu(z_ref[0, rows, W_BR:2 * W_BR])
            d = z_ref[0, rows, 3 * W_BR:4 * W_BR] * ec_ref[rows, :] * _silu(z_ref[0, rows, 5 * W_BR:6 * W_BR])
            y_ref[0, rows, 0:W_BR] = c.astype(y_ref.dtype)
            y_ref[0, rows, W_BR:2 * W_BR] = d.astype(y_ref.dtype)
            return carry

        lax.fori_loop(0, ts // GBLK, block, 0)

    full = lambda shape: pl.BlockSpec(shape, lambda b, s: (0,) * len(shape))
    ebuf = pltpu.VMEM((ext_rows, W_BR), F32)
    return pl.pallas_call(
        body, name="odd_fwd", grid=(bsz, s_len // ts),
        in_specs=[pl.BlockSpec((1, ts, N_COLS), lambda b, s: (b, s, 0)), _halo_specs(ts, s_len, N_COLS),
                  full((4, GBLK, GBLK)), full((8, W_BR))],
        out_specs=pl.BlockSpec((1, ts, 2 * W_BR), lambda b, s: (b, s, 0)),
        out_shape=jax.ShapeDtypeStruct((bsz, s_len, 2 * W_BR), MM_DTYPE),
        scratch_shapes=[ebuf, ebuf, ebuf, ebuf, ebuf, pltpu.VMEM((ts, W_BR), F32), ebuf, pltpu.VMEM((ts, W_BR), F32)],
        compiler_params=_cparams(("parallel", "parallel")),
    )(z3, z3, wp, vec)


def _odd_bwd(z3, dy3, wp, vec):
    bsz, s_len, _ = z3.shape
    ts = min(SEQ_TILE, s_len)
    n_s = s_len // ts
    ext_rows = ts + HALO

    def body(z_ref, zl_ref, zr_ref, dy_ref, dyr_ref, wp_ref, vec_ref,
             dz_ref, dbin_ref, dwp_ref, dvec_ref,
             e0, e1, e2, e3, e4, pooled_ref, dext_ref, ec_ref, q_ref, dp_ref, de_ref, f1, f2, f3, f4):
        b = pl.program_id(0)
        s = pl.program_id(1)

        @pl.when((b == 0) & (s == 0))
        def _():
            dbin_ref[...] = jnp.zeros_like(dbin_ref)
            dwp_ref[...] = jnp.zeros_like(dwp_ref)
            dvec_ref[...] = jnp.zeros_like(dvec_ref)

        has_right = s < n_s - 1
        _odd_prologue(z_ref, zl_ref[0], s, ts, (e0, e1, e2, e3, e4), pooled_ref, dext_ref, ec_ref, vec_ref)

        def grads(zc_gate, zd_b, zd_gate, dyc, dyd, rows_out, n, start, valid):
            sgt = _silu(zc_gate)
            dcpre = dyc * vec_ref[0:1, :] * sgt
            db = _mm(dcpre)
            dpool = jnp.concatenate([_dot_nt(db[:, g * GBLK:(g + 1) * GBLK], wp_ref[g]) for g in range(4)], axis=1)
            cnt = _pool_counts(start, n)
            q = jnp.concatenate([dpool[:, g * GBLK:(g + 1) * GBLK] / cnt[g] for g in range(4)], axis=1)
            de = dyd * zd_b * _silu(zd_gate)
            if valid is not None:
                q = jnp.where(valid, q, 0.0)
                de = jnp.where(valid, de, 0.0)
            q_ref[rows_out, :] = q
            dp_ref[rows_out, :] = dpool
            de_ref[rows_out, :] = de
            return dcpre

        def block(bi, carry):
            base = pl.multiple_of(bi * GBLK, GBLK)
            rows = pl.ds(base, GBLK)
            cg = z_ref[0, rows, W_BR:2 * W_BR]
            dyc = dy_ref[0, rows, 0:W_BR]
            dyd = dy_ref[0, rows, W_BR:2 * W_BR]
            d_b = z_ref[0, rows, 3 * W_BR:4 * W_BR]
            d_gate = z_ref[0, rows, 5 * W_BR:6 * W_BR]
            dcpre = grads(cg, d_b, d_gate, dyc, dyd, rows, GBLK, s * ts + base, None)
            pb = _mm(pooled_ref[rows, :])
            dcb = _mm(dcpre)
            cpre = jnp.concatenate([jnp.dot(pb[:, g * GBLK:(g + 1) * GBLK], wp_ref[g], preferred_element_type=F32)
                                    for g in range(4)], axis=1)
            for g in range(4):
                cs = slice(g * GBLK, (g + 1) * GBLK)
                dwp_ref[g] += _dot_tn(pb[:, cs], dcb[:, cs])
            sgt, dsgt = _silu_grad(cg)
            dvec_ref[0:1, :] += _rowsum(dyc * cpre * sgt)
            dcg = dyc * cpre * vec_ref[0:1, :] * dsgt
            sdt, dsdt = _silu_grad(d_gate)
            ec = ec_ref[rows, :]
            ddb = dyd * ec * sdt
            ddg = dyd * d_b * ec * dsdt
            dz_ref[0, rows, W_BR:2 * W_BR] = dcg.astype(dz_ref.dtype)
            dz_ref[0, rows, 3 * W_BR:4 * W_BR] = ddb.astype(dz_ref.dtype)
            dz_ref[0, rows, 5 * W_BR:6 * W_BR] = ddg.astype(dz_ref.dtype)
            dbin_ref[0:1, W_BR:2 * W_BR] += _rowsum(dcg)
            dbin_ref[0:1, 3 * W_BR:4 * W_BR] += _rowsum(ddb)
            dbin_ref[0:1, 5 * W_BR:6 * W_BR] += _rowsum(ddg)
            return carry

        lax.fori_loop(0, ts // GBLK, block, 0)
        hr = zr_ref[0]
        dyr = dyr_ref[0]
        grads(hr[:, W_BR:2 * W_BR], hr[:, 3 * W_BR:4 * W_BR], hr[:, 5 * W_BR:6 * W_BR], dyr[:, 0:W_BR], dyr[:, W_BR:2 * W_BR],
              slice(ts, ext_rows), HALO, (s + 1) * ts, has_right)

        for a, b in _row_chunks(0, ts + 24):
            f1[a:b, :] = q_ref[a:b, :] + q_ref[a + 1:b + 1, :]
        for a, b in _row_chunks(0, ts + 16):
            f2[a:b, GBLK:] = f1[a:b, GBLK:] + f1[a + 2:b + 2, GBLK:]
        for a, b in _row_chunks(0, ts + 8):
            f3[a:b, 2 * GBLK:] = f2[a:b, 2 * GBLK:] + f2[a + 4:b + 4, 2 * GBLK:]
        for a, b in _row_chunks(0, ts):
            f4[a:b, 3 * GBLK:] = f3[a:b, 3 * GBLK:] + f3[a + 8:b + 8, 3 * GBLK:]

        for a, b in _row_chunks(0, ts):
            for g, f in enumerate((f1, f2, f3, f4)):
                cs = slice(g * GBLK, (g + 1) * GBLK)
                dvg = f[a:b, cs] - dp_ref[a:b, cs]
                dz_ref[0, a:b, cs] = dvg.astype(dz_ref.dtype)
                dbin_ref[0:1, cs] += _rowsum(dvg)
            de = de_ref[a:b, :]
            ddc = vec_ref[1:2, :] * de_ref[a + 2:b + 2, :] + vec_ref[2:3, :] * de_ref[a + 1:b + 1, :] + vec_ref[3:4, :] * de
            ddh = ddc * z_ref[0, a:b, 4 * W_BR:5 * W_BR]
            ddcc = ddc * z_ref[0, a:b, 2 * W_BR:3 * W_BR]
            dz_ref[0, a:b, 2 * W_BR:3 * W_BR] = ddh.astype(dz_ref.dtype)
            dz_ref[0, a:b, 4 * W_BR:5 * W_BR] = ddcc.astype(dz_ref.dtype)
            dbin_ref[0:1, 2 * W_BR:3 * W_BR] += _rowsum(ddh)
            dbin_ref[0:1, 4 * W_BR:5 * W_BR] += _rowsum(ddcc)
            for k in range(3):
                dvec_ref[1 + k:2 + k, :] += _rowsum(de * dext_ref[HALO - 2 + k + a:HALO - 2 + k + b, :])

    full = lambda shape: pl.BlockSpec(shape, lambda b, s: (0,) * len(shape))
    acc_shapes = [(1, N_COLS), (4, GBLK, GBLK), (8, W_BR)]
    ebuf = pltpu.VMEM((ext_rows, W_BR), F32)
    tbuf = pltpu.VMEM((ts, W_BR), F32)
    return pl.pallas_call(
        body, name="odd_bwd", grid=(bsz, n_s),
        in_specs=[pl.BlockSpec((1, ts, N_COLS), lambda b, s: (b, s, 0)),
                  _halo_specs(ts, s_len, N_COLS, True), _halo_specs(ts, s_len, N_COLS, False),
                  pl.BlockSpec((1, ts, 2 * W_BR), lambda b, s: (b, s, 0)), _halo_specs(ts, s_len, 2 * W_BR, False),
                  full((4, GBLK, GBLK)), full((8, W_BR))],
        out_specs=[pl.BlockSpec((1, ts, N_COLS), lambda b, s: (b, s, 0))] + [full(sh) for sh in acc_shapes],
        out_shape=[jax.ShapeDtypeStruct((bsz, s_len, N_COLS), MM_DTYPE)] + [jax.ShapeDtypeStruct(sh, F32) for sh in acc_shapes],
        scratch_shapes=[ebuf, ebuf, ebuf, ebuf, ebuf, tbuf, ebuf, tbuf, ebuf, ebuf, ebuf, ebuf, ebuf, ebuf, tbuf],
        compiler_params=_cparams(("arbitrary", "arbitrary")),
    )(z3, z3, z3, dy3, dy3, wp, vec)


def _post_fwd(y2, x2, p_layer, w_out, wg, wple, vec, tgt=None, nxt=None):
    p_all, layer = p_layer
    t, d = x2.shape
    tm = min(ROW_TILE, t)
    last = tgt is not None
    assert not (last and nxt is not None)
    extra = [tgt] if last else (list(nxt) if nxt is not None else [])
    n_in = 7 + len(extra)
    n_cols = nxt[0].shape[1] if nxt is not None else 0
    nc = 768

    def body(*refs):
        y_ref, x_ref, p_ref, wo_ref, wg_ref, wp_ref, vec_ref = refs[:7]
        xn_ref, r_ref, gate_ref = refs[n_in:n_in + 3]
        r = ALPHA * x_ref[...] + jnp.dot(y_ref[...], wo_ref[...], preferred_element_type=F32) + vec_ref[0:1, :]
        r_ref[...] = r
        xh, _ = _ln_stats(r)
        h = xh * vec_ref[1:2, :] + vec_ref[2:3, :]
        gate = _sigmoid(_dot(h, wg_ref[...]) + vec_ref[3:4, :])
        gate_ref[...] = gate
        xn = h + gate * _dot(p_ref[...], wp_ref[...])
        if last:
            sq_ref = refs[n_in + 3]

            @pl.when(pl.program_id(0) == 0)
            def _():
                sq_ref[...] = jnp.zeros_like(sq_ref)
            e = xn - refs[7][...]
            xn_ref[...] = e / float(d)
            sq_ref[...] += _rowsum(e * e)
        else:
            xn_ref[...] = xn
        if nxt is not None:
            win_ref, bin_ref, z_ref = refs[7], refs[8], refs[n_in + 3]
            xb = _mm(xn)
            for j in range(n_cols // nc):
                cs = slice(j * nc, (j + 1) * nc)
                z_ref[:, cs] = jnp.dot(xb, win_ref[:, cs], preferred_element_type=F32) + bin_ref[:, cs]

    row = lambda c: pl.BlockSpec((tm, c), lambda i: (i, 0))
    full = lambda shape: pl.BlockSpec(shape, lambda i: (0,) * len(shape), pipeline_mode=pl.Buffered(1))
    extra_in = [row(d)] if last else ([full((d, n_cols)), full((1, n_cols))] if nxt is not None else [])
    extra_out = [full((1, d))] if last else ([row(n_cols)] if nxt is not None else [])
    extra_shape = [jax.ShapeDtypeStruct((1, d), F32)] if last else ([jax.ShapeDtypeStruct((t, n_cols), F32)] if nxt is not None else [])
    return pl.pallas_call(
        body, name="post_fwd_loss" if last else "post_fwd", grid=(t // tm,),
        in_specs=[row(d), row(d), pl.BlockSpec((None, tm, D_PLE), lambda i: (layer, i, 0)),
                  full((d, d)), full((d, d)), full((D_PLE, d)), full((8, d))] + extra_in,
        out_specs=[row(d), row(d), row(d)] + extra_out,
        out_shape=[jax.ShapeDtypeStruct((t, d), F32)] * 3 + extra_shape,
        compiler_params=_cparams(("arbitrary",) if last else ("parallel",)),
    )(y2, x2, p_all, w_out, wg, wple, vec, *extra)


def _post_bwd(dxn, r2, gate2, p_layer, y2, w_out, wg, wple, vec):
    p_all, layer = p_layer
    t, d = r2.shape
    tm = min(BWD_ROW_TILE, t)

    def body(dxn_ref, r_ref, gate_ref, p_ref, y_ref, wo_ref, wg_ref, wp_ref, vec_ref,
             dxr_ref, dy_ref, dwo_ref, dwg_ref, dwp_ref, dvec_ref):
        @pl.when(pl.program_id(0) == 0)
        def _():
            dwo_ref[...] = jnp.zeros_like(dwo_ref)
            dwg_ref[...] = jnp.zeros_like(dwg_ref)
            dwp_ref[...] = jnp.zeros_like(dwp_ref)
            dvec_ref[...] = jnp.zeros_like(dvec_ref)

        dxn = dxn_ref[...]
        gate = gate_ref[...]
        xh, rs = _ln_stats(r_ref[...])
        hb = _mm(xh * vec_ref[1:2, :] + vec_ref[2:3, :])
        pb = _mm(p_ref[...])
        pe = jnp.dot(pb, wp_ref[...], preferred_element_type=F32)
        dpre = dxn * pe * gate * (1.0 - gate)
        dpb = _mm(dpre)
        dh = dxn + _dot_nt(dpb, wg_ref[...])
        dwg_ref[...] += _dot_tn(hb, dpb)
        dwp_ref[...] += _dot_tn(pb, dxn * gate)
        dr = _ln_bwd(dh * vec_ref[1:2, :], xh, rs)
        drb = _mm(dr)
        dxr_ref[...] = ALPHA * dr
        dy_ref[...] = _dot_nt(drb, wo_ref[...])
        dwo_ref[...] += _dot_tn(y_ref[...], drb)
        dvec_ref[0:1, :] += _rowsum(dr)
        dvec_ref[1:2, :] += _rowsum(dh * xh)
        dvec_ref[2:3, :] += _rowsum(dh)
        dvec_ref[3:4, :] += _rowsum(dpre)

    row = lambda c: pl.BlockSpec((tm, c), lambda i: (i, 0))
    full = lambda shape: pl.BlockSpec(shape, lambda i: (0,) * len(shape), pipeline_mode=pl.Buffered(1))
    acc_shapes = [(d, d), (d, d), (D_PLE, d), (8, d)]
    return pl.pallas_call(
        body, name="post_bwd", grid=(t // tm,),
        in_specs=[row(d), row(d), row(d), pl.BlockSpec((None, tm, D_PLE), lambda i: (layer, i, 0)), row(d),
                  full((d, d)), full((d, d)), full((D_PLE, d)), full((8, d))],
        out_specs=[row(d), row(d)] + [full(sh) for sh in acc_shapes],
        out_shape=[jax.ShapeDtypeStruct((t, d), F32)] * 2 + [jax.ShapeDtypeStruct(sh, F32) for sh in acc_shapes],
        compiler_params=_cparams(("arbitrary",)),
    )(dxn, r2, gate2, p_all, y2, w_out, wg, wple, vec)


def _place():
    x, y, c = lax.axis_index("x"), lax.axis_index("y"), lax.axis_index("c")
    chips = [(1 - x, y), (x, 1 - y), (1 - x, 1 - y)]
    return x, y, c, chips


def _shard_of(ref, ax, k, width, lo=None, ln=None):
    idx = [slice(None)] * 3
    if lo is not None:
        idx[0] = pl.ds(lo, ln)
    idx[ax] = pl.ds(k * width, width)
    return ref.at[tuple(idx)]


def _remote(src, dst, ssem, rsem, dev):
    return pltpu.make_async_remote_copy(src_ref=src, dst_ref=dst, send_sem=ssem, recv_sem=rsem, device_id=dev, device_id_type=MESH)


def _place_shard(w, ax, chip, dtype=MM_DTYPE, layer=None):
    l_dim, a_dim, b_dim = w.shape
    tr = min(256, a_dim)
    per = a_dim // tr
    first = 0
    if layer is not None:
        l_dim, first = 1, layer
    shape = [l_dim, a_dim, b_dim]
    shape[ax] *= 4
    if ax == 2:
        out_spec = pl.BlockSpec((1, tr, b_dim), lambda l, i, k: (l, i, k[0]))
    else:
        out_spec = pl.BlockSpec((1, tr, b_dim), lambda l, i, k: (l, k[0] * per + i, 0))

    def body(k_ref, w_ref, o_ref):
        o_ref[...] = w_ref[...].astype(o_ref.dtype)

    return pl.pallas_call(
        body, name="place_shard",
        grid_spec=pltpu.PrefetchScalarGridSpec(
            num_scalar_prefetch=1, grid=(l_dim, per),
            in_specs=[pl.BlockSpec((1, tr, b_dim), lambda l, i, k: (first + l, i, 0))], out_specs=out_spec),
        out_shape=jax.ShapeDtypeStruct(tuple(shape), dtype),
        compiler_params=_cparams(("parallel", "parallel")),
    )(chip, w)


def _shard_copies(src_refs, dst_refs, axes, sems):
    x, y, c, chips = _place()
    j = 2 * x + y
    cps = []
    for a, (s_ref, d_ref) in enumerate(zip(src_refs, dst_refs)):
        w = d_ref.shape[axes[a]] // 4
        for q, (qx, qy) in enumerate(chips):
            ssem, rsem = sems[3 * a + q]
            cps.append(_remote(_shard_of(s_ref, axes[a], j, w), _shard_of(d_ref, axes[a], j, w), ssem, rsem, (qx, qy, c)))
    return cps


def _chip_handshake():
    x, y, c, chips = _place()
    barrier = pltpu.get_barrier_semaphore()
    for qx, qy in chips:
        pl.semaphore_signal(barrier, inc=1, device_id=(qx, qy, c), device_id_type=MESH)
    pl.semaphore_wait(barrier, 3)


def _gather_async(name, collective_id, fulls, axes):
    n = len(fulls)
    refs = [jax.new_ref(f, memory_space=pltpu.MemorySpace.HBM) for f in fulls]
    moved = sum(2 * 3 * (f.size // 4) * jnp.dtype(f.dtype).itemsize for f in fulls)

    @pl.kernel(mesh=plsc.ScalarSubcoreMesh(axis_name="seq", num_cores=1), name=name,
               scratch_types=(pltpu.SemaphoreType.DMA,) * (6 * n),
               cost_estimate=pl.CostEstimate(flops=0, transcendentals=0, bytes_accessed=moved, remote_bytes_transferred=moved),
               compiler_params=pltpu.CompilerParams(collective_id=collective_id))
    def launch(*sems):
        _chip_handshake()
        cps = _shard_copies(refs, refs, axes, [(sems[2 * k], sems[2 * k + 1]) for k in range(3 * n)])
        for cp in cps:
            cp.start()
        for cp in cps:
            cp.wait()

    launch()
    return refs


def _gather_first_async(collective_id, w_full, sv_full):
    w_ref = jax.new_ref(w_full, memory_space=pltpu.MemorySpace.HBM)
    sv_ref = jax.new_ref(sv_full, memory_space=pltpu.MemorySpace.HBM)
    a_half = w_full.shape[1] // 2
    width = w_full.shape[2] // 4
    moved = 2 * 3 * (w_full.size // 4 + sv_full.size // 4 * 4)

    @pl.kernel(mesh=plsc.ScalarSubcoreMesh(axis_name="seq", num_cores=1), name="gather_first",
               scratch_types=(pltpu.SemaphoreType.DMA,) * 18,
               cost_estimate=pl.CostEstimate(flops=0, transcendentals=0, bytes_accessed=moved, remote_bytes_transferred=moved),
               compiler_params=pltpu.CompilerParams(collective_id=collective_id))
    def launch(*sems):
        x, y, c, chips = _place()
        barrier = pltpu.get_barrier_semaphore()
        for dev in [(qx, qy, c) for qx, qy in chips] + [(x, y, 1 - c)]:
            pl.semaphore_signal(barrier, inc=1, device_id=dev, device_id_type=MESH)
        pl.semaphore_wait(barrier, 4)
        j = 2 * x + y

        def piece(k, half):
            return w_ref.at[:, pl.ds(half * a_half, a_half), pl.ds(k * width, width)]

        small = _shard_copies([sv_ref], [sv_ref], [1], [(sems[2 * q], sems[2 * q + 1]) for q in range(3)])
        sends = [_remote(piece(j, c), piece(j, c), sems[6 + 2 * q], sems[7 + 2 * q], (qx, qy, c)) for q, (qx, qy) in enumerate(chips)]
        for cp in small + sends:
            cp.start()
        passed = []
        for q, (qx, qy) in enumerate(chips):
            got = piece(2 * qx + qy, c)
            _remote(got, got, sems[6 + 2 * q], sems[7 + 2 * q], (x, y, c)).wait_recv()
            cp = _remote(got, got, sems[12 + 2 * q], sems[13 + 2 * q], (x, y, 1 - c))
            cp.start()
            passed.append(cp)
        for q, (qx, qy) in enumerate(chips):
            other = piece(2 * qx + qy, 1 - c)
            _remote(other, other, sems[12 + 2 * q], sems[13 + 2 * q], (x, y, c)).wait_recv()
        for cp in small:
            cp.wait()
        for cp in sends + passed:
            cp.wait_send()

    launch()
    return w_ref, sv_ref


def _scatter_async(name, collective_id, grads, axes, after):
    n = len(grads)
    extra = len(after)
    outs = []
    for g, ax in zip(grads, axes):
        sh = list(g.shape)
        sh[ax] //= 4
        outs.append(jax.ShapeDtypeStruct((3,) + tuple(sh), g.dtype))

    def body(*refs):
        srcs, lands, sems = refs[:n], refs[n + extra:2 * n + extra], refs[2 * n + extra:]
        _chip_handshake()
        x, y, c, chips = _place()
        cps = []
        for a in range(n):
            w = srcs[a].shape[axes[a]] // 4
            for q, (qx, qy) in enumerate(chips):
                k = 3 * a + q
                cps.append(_remote(_shard_of(srcs[a], axes[a], 2 * qx + qy, w), lands[a].at[q], sems[2 * k], sems[2 * k + 1],
                                   (qx, qy, c)))
        for cp in cps:
            cp.start()
        for cp in cps:
            cp.wait()

    moved = sum(2 * 3 * math.prod(o.shape[1:]) * jnp.dtype(o.dtype).itemsize for o in outs)
    return pl.kernel(body, out_type=outs, mesh=plsc.ScalarSubcoreMesh(axis_name="seq", num_cores=1), name=name,
                     scratch_types=(pltpu.SemaphoreType.DMA,) * (6 * n),
                     cost_estimate=pl.CostEstimate(flops=0, transcendentals=0, bytes_accessed=moved, remote_bytes_transferred=moved),
                     compiler_params=pltpu.CompilerParams(collective_id=collective_id))(*grads, *after)


def _pair_swap_async(name, collective_id, sums):
    n = len(sums)

    def body(*refs):
        g_refs, o_refs, sems = refs[:n], refs[n:2 * n], refs[2 * n:]
        x, y, c, _ = _place()
        barrier = pltpu.get_barrier_semaphore()
        pl.semaphore_signal(barrier, inc=1, device_id=(x, y, 1 - c), device_id_type=MESH)
        pl.semaphore_wait(barrier, 1)
        cps = [_remote(g_refs[a], o_refs[a], sems[2 * a], sems[2 * a + 1], (x, y, 1 - c)) for a in range(n)]
        for cp in cps:
            cp.start()
        for cp in cps:
            cp.wait()

    moved = sum(2 * g.size * jnp.dtype(g.dtype).itemsize for g in sums)
    return pl.kernel(body, out_type=[jax.ShapeDtypeStruct(g.shape, g.dtype) for g in sums],
                     mesh=plsc.ScalarSubcoreMesh(axis_name="seq", num_cores=1), name=name,
                     scratch_types=(pltpu.SemaphoreType.DMA,) * (2 * n),
                     cost_estimate=pl.CostEstimate(flops=0, transcendentals=0, bytes_accessed=moved, remote_bytes_transferred=moved),
                     compiler_params=pltpu.CompilerParams(collective_id=collective_id))(*sums)


_FLIPS = [(fx, fy, fc) for fx in (0, 1) for fy in (0, 1) for fc in (0, 1)][1:]


def _small_reduce_scatter(small):
    r = small.shape[0]
    rh, rq = r // 2, r // 8

    def body(g_ref, out_ref, pair_ref, chip_ref, s1_ref, ssem, rsem):
        x, y, c, chips = _place()
        cp = _remote(g_ref.at[pl.ds((1 - c) * rh, rh)], pair_ref, ssem.at[0], rsem.at[0], (x, y, 1 - c))
        cp.start()
        cp.wait()
        s1_ref[...] = g_ref[pl.ds(pl.multiple_of(c * rh, 8), rh), :] + pair_ref[...]
        cps = [_remote(s1_ref.at[pl.ds((2 * qx + qy) * rq, rq)], chip_ref.at[q], ssem.at[1 + q], rsem.at[1 + q], (qx, qy, c))
               for q, (qx, qy) in enumerate(chips)]
        for cp in cps:
            cp.start()
        for cp in cps:
            cp.wait()
        out_ref[...] = ((s1_ref[pl.ds(pl.multiple_of((2 * x + y) * rq, 8), rq), :] + chip_ref[0]) + chip_ref[1]) + chip_ref[2]

    vm = pl.BlockSpec(memory_space=pltpu.VMEM)
    return pl.pallas_call(
        body, name="small_reduce_scatter", in_specs=[vm], out_specs=vm, out_shape=jax.ShapeDtypeStruct((rq, 128), F32),
        scratch_shapes=[pltpu.VMEM((rh, 128), F32), pltpu.VMEM((3, rq, 128), F32), pltpu.VMEM((rh, 128), F32),
                        pltpu.SemaphoreType.DMA((4,)), pltpu.SemaphoreType.DMA((4,))],
        compiler_params=pltpu.CompilerParams(has_side_effects=True, vmem_limit_bytes=VMEM_LIMIT),
    )(small)


def _eighths_async(collective_id, piece):
    rq = piece.shape[0]

    def body(p_ref, o_ref, *sems):
        x, y, c, _ = _place()
        barrier = pltpu.get_barrier_semaphore()
        for fx, fy, fc in _FLIPS:
            pl.semaphore_signal(barrier, inc=1, device_id=(x ^ fx, y ^ fy, c ^ fc), device_id_type=MESH)
        pl.semaphore_wait(barrier, 7)
        mine = o_ref.at[pl.ds((4 * c + 2 * x + y) * rq, rq)]
        cps = [_remote(p_ref, mine, sems[2 * f], sems[2 * f + 1], (x ^ fx, y ^ fy, c ^ fc)) for f, (fx, fy, fc) in enumerate(_FLIPS)]
        for cp in cps:
            cp.start()
        for cp in cps:
            cp.wait()

    moved = 2 * 7 * piece.size * 4
    (out,) = pl.kernel(body, out_type=[jax.ShapeDtypeStruct((8 * rq, 128), F32)],
                       mesh=plsc.ScalarSubcoreMesh(axis_name="seq", num_cores=1), name="small_eighths",
                       scratch_types=(pltpu.SemaphoreType.DMA,) * 14,
                       cost_estimate=pl.CostEstimate(flops=0, transcendentals=0, bytes_accessed=moved, remote_bytes_transferred=moved),
                       compiler_params=pltpu.CompilerParams(collective_id=collective_id))(piece)
    return out


def _to_wire(g):
    _, a_dim, b_dim = g.shape
    tr = min(256, a_dim)

    def body(g_ref, o_ref, done_ref):
        o_ref[...] = g_ref[...].astype(o_ref.dtype)
        done_ref[...] = jnp.zeros_like(done_ref)

    blk = pl.BlockSpec((1, tr, b_dim), lambda i: (0, i, 0))
    return pl.pallas_call(
        body, name="to_wire", grid=(a_dim // tr,), in_specs=[blk], out_specs=[blk, pl.BlockSpec((8, 128), lambda i: (0, 0))],
        out_shape=[jax.ShapeDtypeStruct(g.shape, WIRE_DTYPE), jax.ShapeDtypeStruct((8, 128), F32)],
        compiler_params=_cparams(("arbitrary",)),
    )(g)


def _chip_sum(owns, gots, ax, chip, after=None):
    n_layers = len(owns)
    tail = [] if after is None else [after]
    _, _, a_dim, b_dim = gots[0].shape
    tr = min(256, a_dim)
    per = a_dim // tr

    def own_spec(layer):
        if ax == 2:
            return pl.BlockSpec((1, tr, b_dim), lambda l, i, k: (0, jnp.where(l == layer, i, 0), k[0]))
        return pl.BlockSpec((1, tr, b_dim), lambda l, i, k: (0, k[0] * per + jnp.where(l == layer, i, 0), 0))

    def got_spec(layer):
        return pl.BlockSpec((3, 1, tr, b_dim), lambda l, i, k: (0, 0, jnp.where(l == layer, i, 0), 0))

    def body(k_ref, *refs):
        s_ref = refs[-1]
        for layer in range(n_layers):
            @pl.when(pl.program_id(0) == layer)
            def _(own_ref=refs[layer], got_ref=refs[n_layers + layer]):
                s_ref[...] = ((own_ref[...].astype(F32) + got_ref[0].astype(F32)) + got_ref[1].astype(F32)) + got_ref[2].astype(F32)

    return pl.pallas_call(
        body, name="chip_sum",
        grid_spec=pltpu.PrefetchScalarGridSpec(
            num_scalar_prefetch=1, grid=(n_layers, per),
            in_specs=[own_spec(l) for l in range(n_layers)] + [got_spec(l) for l in range(n_layers)]
            + [pl.BlockSpec((8, 128), lambda l, i, k: (0, 0))] * len(tail),
            out_specs=pl.BlockSpec((1, tr, b_dim), lambda l, i, k: (l, i, 0))),
        out_shape=jax.ShapeDtypeStruct((n_layers, a_dim, b_dim), F32),
        compiler_params=_cparams(("arbitrary", "arbitrary")),
    )(chip, *owns, *gots, *tail)


def _adam_math(w, g, m, v):
    m = ADAM_B1 * m + (1.0 - ADAM_B1) * g
    v = ADAM_B2 * v + (1.0 - ADAM_B2) * (g * g)
    m_hat = m / (1.0 - ADAM_B1 ** ADAM_STEP)
    v_hat = v / (1.0 - ADAM_B2 ** ADAM_STEP)
    return -ADAM_LR * (m_hat / (jnp.sqrt(v_hat) + ADAM_EPS) + ADAM_WD * w), m, v


def _adamw_big(w, g_mine, g_other, m, v):
    l_dim, a_dim, b_dim = w.shape
    tr = min(256, a_dim)

    def body(w_ref, g1_ref, g2_ref, m_ref, v_ref, g_ref, d_ref, nm_ref, nv_ref):
        g = g1_ref[...] + g2_ref[...]
        g_ref[...] = g
        d_ref[...], nm_ref[...], nv_ref[...] = _adam_math(w_ref[...], g, m_ref[...], v_ref[...])

    blk = pl.BlockSpec((1, tr, b_dim), lambda l, i: (l, i, 0))
    return pl.pallas_call(
        body, name="adamw_big", grid=(l_dim, a_dim // tr), in_specs=[blk] * 5, out_specs=[blk] * 4,
        out_shape=[jax.ShapeDtypeStruct(w.shape, F32)] * 4,
        compiler_params=_cparams(("parallel", "parallel")),
    )(w, g_mine, g_other, m, v)


def _adamw_small(ws, gs, ms, vs):
    n = len(ws)

    def body(*refs):
        for i in range(n):
            w_ref, g_ref, m_ref, v_ref = refs[i], refs[n + i], refs[2 * n + i], refs[3 * n + i]
            d_ref, nm_ref, nv_ref = refs[4 * n + i], refs[5 * n + i], refs[6 * n + i]
            d_ref[...], nm_ref[...], nv_ref[...] = _adam_math(w_ref[...], g_ref[...], m_ref[...], v_ref[...])

    shapes = [jax.ShapeDtypeStruct(w.shape, F32) for w in ws]
    outs = pl.pallas_call(body, name="adamw_small", out_shape=shapes * 3,
                          compiler_params=_cparams())(*ws, *gs, *ms, *vs)
    return outs[:n], outs[n:2 * n], outs[2 * n:]


def _pack(arrs, row_mult):
    parts = []
    for a in arrs:
        flat = a.reshape(-1)
        pad = (-flat.shape[0]) % 1024
        parts.append(jnp.pad(flat, (0, pad)).reshape(-1, 128))
    buf = jnp.concatenate(parts, axis=0)
    pad = (-buf.shape[0]) % row_mult
    return jnp.pad(buf, ((0, pad), (0, 0)))


def _unpack(buf, shapes):
    out, row = [], 0
    for sh in shapes:
        n = math.prod(sh)
        rows = -(-n // 1024) * 8
        out.append(buf[row:row + rows].reshape(-1)[:n].reshape(sh))
        row += rows
    return out


_NAMES = ['w_in_e', 'b_in_e', 'conv_a_w', 'conv_a_b', 'ln_a_g', 'ln_a_b', 'ln_v_g', 'ln_v_b', 'w_s', 'b_s', 'w_out_e', 'b_out_e',
          'w_in_o', 'b_in_o', 'w_pool', 'pool_scale', 'conv_d_w', 'w_out_o', 'b_out_o', 'ln_g', 'ln_b', 'w_ple', 'w_ple_gate',
          'b_ple_gate']
_BIG = ['w_in_e', 'w_out_e', 'w_in_o', 'w_out_o', 'w_ple', 'w_ple_gate']
_SMALL_SHARDED = ['conv_a_w', 'b_in_o', 'pool_scale', 'conv_d_w', 'b_out_o']


def kernel(x, p, w_in_e, b_in_e, conv_a_w, conv_a_b, ln_a_g, ln_a_b, ln_v_g, ln_v_b, w_s, b_s, w_out_e, b_out_e, w_in_o, b_in_o, w_pool, pool_scale, conv_d_w, w_out_o, b_out_o, ln_g, ln_b, w_ple, w_ple_gate, b_ple_gate, loss_target, m_w_in_e, m_b_in_e, m_conv_a_w, m_conv_a_b, m_ln_a_g, m_ln_a_b, m_ln_v_g, m_ln_v_b, m_w_s, m_b_s, m_w_out_e, m_b_out_e, m_w_in_o, m_b_in_o, m_w_pool, m_pool_scale, m_conv_d_w, m_w_out_o, m_b_out_o, m_ln_g, m_ln_b, m_w_ple, m_w_ple_gate, m_b_ple_gate, v_w_in_e, v_b_in_e, v_conv_a_w, v_conv_a_b, v_ln_a_g, v_ln_a_b, v_ln_v_g, v_ln_v_b, v_w_s, v_b_s, v_w_out_e, v_b_out_e, v_w_in_o, v_b_in_o, v_w_pool, v_pool_scale, v_conv_d_w, v_w_out_o, v_b_out_o, v_ln_g, v_ln_b, v_w_ple, v_w_ple_gate, v_b_ple_gate):
    args = locals()
    wts = {n: args[n] for n in _NAMES}
    mom = {n: args["m_" + n] for n in _NAMES}
    var = {n: args["v_" + n] for n in _NAMES}
    bsz, s_len, d = x.shape
    t = bsz * s_len
    cx, cy, cc = lax.axis_index("x"), lax.axis_index("y"), lax.axis_index("c")
    chip = (2 * cx + cy).astype(jnp.int32).reshape(1)

    def placed(name, ax, layer):
        return _place_shard(wts[name], ax, chip, layer=layer)

    sv = _pack([wts[n] for n in _SMALL_SHARDED], 8)
    first_refs = _gather_first_async(DEPTH, placed('w_in_e', 2, 0), _place_shard(sv[None], 1, chip, F32))
    layer_refs = []
    for i in range(DEPTH):
        sfx = '_e' if i % 2 == 0 else '_o'
        nxt_sfx = '_o' if i % 2 == 0 else '_e'
        items = ([('w_out' + sfx, 1, i // 2), ('w_ple_gate', 1, i), ('w_ple', 2, i)]
                 + ([('w_in' + nxt_sfx, 2, (i + 1) // 2)] if i + 1 < DEPTH else []))
        layer_refs.append(_gather_async("gather_layer%d" % i, i, [placed(*it) for it in items], [it[1] for it in items]))
    fw = {}
    w_in_first = first_refs[0][...]
    sv_all = first_refs[1][...].reshape((4,) + sv.shape)
    small_parts = [_unpack(sv_all[k], [wts[n].shape for n in _SMALL_SHARDED]) for k in range(4)]
    for i, n in enumerate(_SMALL_SHARDED):
        fw[n] = jnp.concatenate([small_parts[k][i] for k in range(4)], axis=-1)
    for n in _NAMES:
        fw.setdefault(n, wts[n])

    def row8(rows, width):
        rows = [r.reshape(1, width) for r in rows]
        return jnp.concatenate(rows + [jnp.zeros((8 - len(rows), width), F32)], axis=0)

    x2 = x.reshape(t, d)
    saved = []
    for i in range(DEPTH):
        j = i // 2
        even = i % 2 == 0
        b_in, b_out = (fw['b_in_e'], fw['b_out_e']) if even else (fw['b_in_o'], fw['b_out_o'])
        if i == 0:
            w_in = w_in_first[0]
            z = _in_proj(x2, w_in, b_in[j].reshape(1, N_COLS))
        z3 = z.reshape(bsz, s_len, N_COLS)
        if even:
            cw = jnp.concatenate([fw['conv_a_w'][j], jnp.zeros((1, W_BR), F32)], axis=0)
            mvec = row8([fw['conv_a_b'][j], fw['ln_a_g'][j], fw['ln_a_b'][j], fw['ln_v_g'][j], fw['ln_v_b'][j]], W_BR)
            bsf = jnp.repeat(fw['b_s'][j].T, W_BR // 8, axis=1)
            y3, a13 = _even_fwd(z3, cw, mvec, fw['w_s'][j], bsf)
            mix = (a13, cw, mvec, fw['w_s'][j], jnp.swapaxes(fw['w_s'][j], 1, 2), bsf)
        else:
            mvec = row8([fw['pool_scale'][j]] + [fw['conv_d_w'][j][k] for k in range(3)], W_BR)
            mix = (fw['w_pool'][j].astype(MM_DTYPE), mvec)
            y3 = _odd_fwd(z3, mix[0], mvec)
        pvec = row8([b_out[j], fw['ln_g'][i], fw['ln_b'][i], fw['b_ple_gate'][i]], d)
        post_w = (layer_refs[i][0][...][0], layer_refs[i][1][...][0], layer_refs[i][2][...][0], pvec)
        p2 = (p.reshape(DEPTH, t, D_PLE), i)
        y2 = y3.reshape(t, 2 * W_BR)
        saved.append((x2, z3, y2, None, None, p2, w_in, mix, post_w))
        if i < DEPTH - 1:
            w_in = layer_refs[i][3][...][0]
            b_next = (fw['b_in_o'] if even else fw['b_in_e'])[(i + 1) // 2].reshape(1, N_COLS)
            x2, r2, gate2, z = _post_fwd(y2, x2, p2, *post_w, nxt=(w_in, b_next))
        else:
            dx, r2, gate2, sq = _post_fwd(y2, x2, p2, *post_w, tgt=loss_target.reshape(t, d))
        saved[-1] = saved[-1][:3] + (r2, gate2) + saved[-1][5:]

    gr = {n: [None] * wts[n].shape[0] for n in _NAMES}
    prev_lands = []
    for i in reversed(range(DEPTH)):
        j = i // 2
        even = i % 2 == 0
        x_in, z3, y2, r2, gate2, p2, w_in, mix, post_w = saved[i]
        dxr, dy, dwo, dwg, dwp, dpv = _post_bwd(dx, r2, gate2, p2, y2, *post_w)
        dy3 = dy.reshape(bsz, s_len, 2 * W_BR)
        sfx = '_e' if even else '_o'
        gr['w_out' + sfx][j], gr['b_out' + sfx][j] = dwo, dpv[0]
        gr['w_ple_gate'][i], gr['w_ple'][i] = dwg, dwp
        gr['ln_g'][i], gr['ln_b'][i], gr['b_ple_gate'][i] = dpv[1], dpv[2], dpv[3]
        if even:
            dz3, dbin, dcw, dmv, dws, dbsf = _even_bwd(z3, dy3, *mix)
            gr['conv_a_w'][j], gr['conv_a_b'][j] = dcw[:KA], dmv[0]
            gr['ln_a_g'][j], gr['ln_a_b'][j], gr['ln_v_g'][j], gr['ln_v_b'][j] = dmv[1], dmv[2], dmv[3], dmv[4]
            gr['w_s'][j] = dws
            gr['b_s'][j] = jnp.sum(dbsf.reshape(GBLK, 8, W_BR // 8), axis=2).T
        else:
            dz3, dbin, dwpool, dmv = _odd_bwd(z3, dy3, *mix)
            gr['w_pool'][j], gr['pool_scale'][j], gr['conv_d_w'][j] = dwpool, dmv[0], dmv[1:4]
        gr['b_in' + sfx][j] = dbin[0]
        dz2 = dz3.reshape(t, N_COLS)
        dwi, dw_done, *dwi_wire = _in_proj_bwd_dw(x_in, dz2, wire=(i == 0))
        post_items = [('w_out' + sfx, j, dwo[None], 1), ('w_ple_gate', i, dwg[None], 1), ('w_ple', i, dwp[None], 2)]
        in_item = ('w_in' + sfx, j, dwi[None], 2)
        batches = [post_items + [in_item]] if i else [post_items, [in_item]]
        for bi, items in enumerate(batches):
            sent = [it[2] for it in items]
            if i == 0:
                sent = [_to_wire(g)[0] for g in sent] if bi == 0 else [dwi_wire[0][None]]
            lands = _scatter_async("scatter_layer%d_%d" % (i, bi), DEPTH + 1 + 2 * i + bi, sent, [it[3] for it in items],
                                   prev_lands[:1] + ([dbin] if (i == 0 and bi == 0) else []))
            prev_lands = list(lands)
            for it, land in zip(items, lands):
                gr[it[0]][it[1]] = (it[2], land, it[3])
        dx = _in_proj_bwd_dx(dxr, dz2, w_in, dw_done)
    grad_x = dx.reshape(bsz, s_len, d)

    small_names = [n for n in _NAMES if n not in _BIG]
    g_small_full = [jnp.stack(gr[n]) for n in small_names] + [sq]
    piece = _small_reduce_scatter(_pack(g_small_full, 64))
    eighths = _eighths_async(3 * DEPTH + 3, piece)
    big_order = _BIG[1:] + _BIG[:1]
    sums = [_chip_sum([g[0] for g in gr[n]], [g[1] for g in gr[n]], gr[n][0][2], chip, dx if n == big_order[-1] else None)
            for n in big_order]
    others = (list(_pair_swap_async("pair_swap_a", 3 * DEPTH + 1, sums[:-1]))
              + list(_pair_swap_async("pair_swap_b", 3 * DEPTH + 2, sums[-1:])))
    small_all = lax.dynamic_update_slice(eighths, piece, ((4 * cc + 2 * cx + cy) * piece.shape[0], 0))
    *g_small, sq_all = _unpack(small_all, [g.shape for g in g_small_full])
    loss = 0.5 * jnp.sum(sq_all) / d
    grads = {}
    for n, g in zip(small_names, g_small):
        if n in _SMALL_SHARDED:
            w = wts[n].shape[-1]
            g = lax.dynamic_slice_in_dim(g, (2 * cx + cy) * w, w, axis=g.ndim - 1)
        grads[n] = g

    delta, new_m, new_v = {}, {}, {}
    for n, mine, other in zip(big_order, sums, others):
        grads[n], delta[n], new_m[n], new_v[n] = _adamw_big(wts[n], mine, other, mom[n], var[n])
    ds, ms, vs = _adamw_small([wts[n] for n in small_names], [grads[n] for n in small_names],
                              [mom[n] for n in small_names], [var[n] for n in small_names])
    for n, a, b, c_ in zip(small_names, ds, ms, vs):
        delta[n], new_m[n], new_v[n] = a, b, c_

    return (loss, grad_x, *[grads[n] for n in _NAMES], *[delta[n] for n in _NAMES],
            *[new_m[n] for n in _NAMES], *[new_v[n] for n in _NAMES])
```
